```python
import jax, jax.numpy as jnp
from jax import lax
import numpy as np

D_MODEL = 1024
BATCH = 8
SEQ = 2048
DEPTH = 2

N_META = 16
FOX_HEADS = 8
FOX_HEAD_DIM = 64
FOX_WIDTH = FOX_HEADS * FOX_HEAD_DIM
Q_BLOCK = 128
CONV_CH = D_MODEL - FOX_WIDTH
CONV_WIDTH = 31
IN_COLS = 3 * FOX_WIDTH + FOX_HEADS + 2 * CONV_CH
POOL_WINDOWS = (2, 4, 8, 16)
N_POOL_GROUPS = len(POOL_WINDOWS)
POOL_GROUP = D_MODEL // N_POOL_GROUPS
D_FF = 2816
FFN_CONV_WIDTH = 3
RMS_EPS = 1e-6
LN_EPS = 1e-5
N_EVEN = (DEPTH + 1) // 2
N_ODD = DEPTH // 2

kernel_name = "fox_conformer_pool_hybrid_block"


def rms_norm(x, g):
    xf = x.astype(jnp.float32)
    y = xf * lax.rsqrt(jnp.mean(xf * xf, axis=-1, keepdims=True) + RMS_EPS)
    return (y * g.astype(jnp.float32)).astype(x.dtype)


def layer_norm(x, g, b):
    xf = x.astype(jnp.float32)
    mu = jnp.mean(xf, axis=-1, keepdims=True)
    var = jnp.mean(jnp.square(xf - mu), axis=-1, keepdims=True)
    y = (xf - mu) * lax.rsqrt(var + LN_EPS)
    return (y * g.astype(jnp.float32) + b.astype(jnp.float32)).astype(x.dtype)


def causal_depthwise_conv(x, w, b):
    K, C = w.shape
    y = lax.conv_general_dilated(
        x, w[:, None, :].astype(x.dtype), window_strides=(1,), padding=[(K - 1, 0)],
        dimension_numbers=('NWC', 'WIO', 'NWC'), feature_group_count=C)
    return y + b.astype(x.dtype)


def forgetting_attention(q, k, v, log_f):
    B, L, H, Dh = q.shape
    n_blk = (L - N_META) // Q_BLOCK
    qf, kf, vf = (a.astype(jnp.float32) for a in (q, k, v))
    c = jnp.transpose(jnp.cumsum(log_f, axis=1), (0, 2, 1))
    scale = Dh ** -0.5
    pos = jnp.arange(L)

    def attend(q_blk, c_q, q_pos):
        s = jnp.einsum('bqhd,bkhd->bhqk', q_blk, kf) * scale
        s = s + c_q[..., :, None] - c[..., None, :]
        s = jnp.where(pos[None, :] <= q_pos[:, None], s, -jnp.inf)
        p = jax.nn.softmax(s, axis=-1)
        return jnp.einsum('bhqk,bkhd->bqhd', p, vf)

    out_meta = attend(qf[:, :N_META], c[:, :, :N_META], pos[:N_META])
    q_r = jnp.transpose(qf[:, N_META:].reshape(B, n_blk, Q_BLOCK, H, Dh), (1, 0, 2, 3, 4))
    c_r = jnp.transpose(c[:, :, N_META:].reshape(B, H, n_blk, Q_BLOCK), (2, 0, 1, 3))
    p_r = pos[N_META:].reshape(n_blk, Q_BLOCK)
    out_r = lax.map(lambda a: attend(*a), (q_r, c_r, p_r))
    out_r = jnp.transpose(out_r, (1, 0, 2, 3, 4)).reshape(B, L - N_META, H, Dh)
    out = jnp.concatenate([out_meta, out_r], axis=1)
    return out.reshape(B, L, H * Dh)


def fox_conformer_mixer(h, w_in, b_f, conv_w, conv_b, ln_g, ln_b, w_out):
    B, L, _ = h.shape
    proj = h @ w_in.astype(h.dtype)
    q, k, v, f_logit, glu = jnp.split(
        proj, [FOX_WIDTH, 2 * FOX_WIDTH, 3 * FOX_WIDTH, 3 * FOX_WIDTH + FOX_HEADS], axis=-1)
    log_f = jax.nn.log_sigmoid(f_logit.astype(jnp.float32) + b_f.astype(jnp.float32))
    shp = (B, L, FOX_HEADS, FOX_HEAD_DIM)
    attn = forgetting_attention(q.reshape(shp), k.reshape(shp), v.reshape(shp), log_f).astype(h.dtype)
    a, g = jnp.split(glu, 2, axis=-1)
    u = a * jax.nn.sigmoid(g)
    u = causal_depthwise_conv(u, conv_w, conv_b)
    u = jax.nn.silu(layer_norm(u, ln_g, ln_b))
    return jnp.concatenate([attn, u], axis=-1) @ w_out.astype(h.dtype)


def multiscale_pool_mixer(h, pool_w, pool_b, pool_scale):
    B, L, D = h.shape
    hf = h.astype(jnp.float32).reshape(B, L, N_POOL_GROUPS, POOL_GROUP)
    cs = jnp.cumsum(hf, axis=1)
    n_seen = jnp.arange(1, L + 1)
    outs = []
    for gi, w in enumerate(POOL_WINDOWS):
        csg = cs[:, :, gi]
        lag = jnp.pad(csg, ((0, 0), (w, 0), (0, 0)))[:, :L]
        cnt = jnp.minimum(n_seen, w).astype(jnp.float32)[None, :, None]
        outs.append((csg - lag) / cnt - hf[:, :, gi])
    d = jnp.stack(outs, axis=2)
    y = jnp.einsum('blgc,gcd->blgd', d, pool_w.astype(jnp.float32)) + pool_b.astype(jnp.float32)
    return (y.reshape(B, L, D) * pool_scale.astype(jnp.float32)).astype(h.dtype)


def conv_glu_ffn(h, w_up, conv_w, conv_b, w_down):
    u = h @ w_up.astype(h.dtype)
    u = causal_depthwise_conv(u, conv_w, conv_b)
    gate, val = jnp.split(u, 2, axis=-1)
    return (jax.nn.silu(gate) * val) @ w_down.astype(h.dtype)


def _fwd_setup_inputs(seed: int = 0) -> dict:
    key = jax.random.key(seed)
    ks = jax.random.split(key, 24)
    nrm = lambda k, shp, s: jax.random.normal(k, shp, jnp.float32) * s
    D = D_MODEL
    return {
        "x": nrm(ks[0], (BATCH, SEQ, D), 1.0),
        "meta_tokens": nrm(ks[1], (N_META, D), 1.0),
        "mix_norm_even": 1.0 + nrm(ks[2], (N_EVEN, D), 0.02),
        "w_in": nrm(ks[3], (N_EVEN, D, IN_COLS), D ** -0.5),
        "b_f": jax.random.uniform(ks[4], (N_EVEN, FOX_HEADS), jnp.float32, 2.0, 5.0),
        "conv_w": nrm(ks[5], (N_EVEN, CONV_WIDTH, CONV_CH), CONV_WIDTH ** -0.5),
        "conv_b": nrm(ks[6], (N_EVEN, CONV_CH), 0.02),
        "ln_g": 1.0 + nrm(ks[7], (N_EVEN, CONV_CH), 0.02),
        "ln_b": nrm(ks[8], (N_EVEN, CONV_CH), 0.02),
        "w_out": nrm(ks[9], (N_EVEN, FOX_WIDTH + CONV_CH, D), (FOX_WIDTH + CONV_CH) ** -0.5),
        "mix_norm_odd": 1.0 + nrm(ks[10], (N_ODD, D), 0.02),
        "pool_w": nrm(ks[11], (N_ODD, N_POOL_GROUPS, POOL_GROUP, POOL_GROUP), POOL_GROUP ** -0.5),
        "pool_b": nrm(ks[12], (N_ODD, N_POOL_GROUPS, POOL_GROUP), 0.02),
        "pool_scale": 0.5 + nrm(ks[13], (N_ODD, D), 0.1),
        "ffn_norm": 1.0 + nrm(ks[14], (DEPTH, D), 0.02),
        "w_up": nrm(ks[15], (DEPTH, D, 2 * D_FF), D ** -0.5),
        "ffn_conv_w": nrm(ks[16], (DEPTH, FFN_CONV_WIDTH, 2 * D_FF), FFN_CONV_WIDTH ** -0.5),
        "ffn_conv_b": nrm(ks[17], (DEPTH, 2 * D_FF), 0.02),
        "w_down": nrm(ks[18], (DEPTH, D_FF, D), D_FF ** -0.5),
        "final_norm": 1.0 + nrm(ks[19], (D,), 0.02),
    }


def _fwd_reference(x, meta_tokens, mix_norm_even, w_in, b_f, conv_w, conv_b, ln_g, ln_b, w_out,
              mix_norm_odd, pool_w, pool_b, pool_scale,
              ffn_norm, w_up, ffn_conv_w, ffn_conv_b, w_down, final_norm):
    B = x.shape[0]
    meta = jnp.broadcast_to(meta_tokens.astype(x.dtype)[None], (B, N_META, x.shape[-1]))
    h = jnp.concatenate([meta, x], axis=1)
    for i in range(DEPTH):
        j = i // 2
        if i % 2 == 0:
            h = h + fox_conformer_mixer(rms_norm(h, mix_norm_even[j]), w_in[j], b_f[j], conv_w[j],
                                        conv_b[j], ln_g[j], ln_b[j], w_out[j])
        else:
            h = h + multiscale_pool_mixer(rms_norm(h, mix_norm_odd[j]), pool_w[j], pool_b[j],
                                          pool_scale[j])
        h = h + conv_glu_ffn(rms_norm(h, ffn_norm[i]), w_up[i], ffn_conv_w[i], ffn_conv_b[i],
                             w_down[i])
    h = rms_norm(h, final_norm)
    return h[:, N_META:]


import jax as _jax
import jax.numpy as _jnp

TWIN_FORMAT = 'train_step'
FWD_PARAMS = ['x', 'meta_tokens', 'mix_norm_even', 'w_in', 'b_f', 'conv_w', 'conv_b', 'ln_g', 'ln_b', 'w_out', 'mix_norm_odd', 'pool_w', 'pool_b', 'pool_scale', 'ffn_norm', 'w_up', 'ffn_conv_w', 'ffn_conv_b', 'w_down', 'final_norm']
TWIN_WEIGHTS = ['meta_tokens', 'mix_norm_even', 'w_in', 'b_f', 'conv_w', 'conv_b', 'ln_g', 'ln_b', 'w_out', 'mix_norm_odd', 'pool_w', 'pool_b', 'pool_scale', 'ffn_norm', 'w_up', 'ffn_conv_w', 'ffn_conv_b', 'w_down', 'final_norm']
TWIN_DIFF_INPUT = 'x'
TWIN_INPUTS = ['x', 'meta_tokens', 'mix_norm_even', 'w_in', 'b_f', 'conv_w', 'conv_b', 'ln_g', 'ln_b', 'w_out', 'mix_norm_odd', 'pool_w', 'pool_b', 'pool_scale', 'ffn_norm', 'w_up', 'ffn_conv_w', 'ffn_conv_b', 'w_down', 'final_norm', 'loss_target', 'm_meta_tokens', 'm_mix_norm_even', 'm_w_in', 'm_b_f', 'm_conv_w', 'm_conv_b', 'm_ln_g', 'm_ln_b', 'm_w_out', 'm_mix_norm_odd', 'm_pool_w', 'm_pool_b', 'm_pool_scale', 'm_ffn_norm', 'm_w_up', 'm_ffn_conv_w', 'm_ffn_conv_b', 'm_w_down', 'm_final_norm', 'v_meta_tokens', 'v_mix_norm_even', 'v_w_in', 'v_b_f', 'v_conv_w', 'v_conv_b', 'v_ln_g', 'v_ln_b', 'v_w_out', 'v_mix_norm_odd', 'v_pool_w', 'v_pool_b', 'v_pool_scale', 'v_ffn_norm', 'v_w_up', 'v_ffn_conv_w', 'v_ffn_conv_b', 'v_w_down', 'v_final_norm']
TWIN_OUTPUTS = ['loss', 'grad_x', 'grad_meta_tokens', 'grad_mix_norm_even', 'grad_w_in', 'grad_b_f', 'grad_conv_w', 'grad_conv_b', 'grad_ln_g', 'grad_ln_b', 'grad_w_out', 'grad_mix_norm_odd', 'grad_pool_w', 'grad_pool_b', 'grad_pool_scale', 'grad_ffn_norm', 'grad_w_up', 'grad_ffn_conv_w', 'grad_ffn_conv_b', 'grad_w_down', 'grad_final_norm', 'delta_meta_tokens', 'delta_mix_norm_even', 'delta_w_in', 'delta_b_f', 'delta_conv_w', 'delta_conv_b', 'delta_ln_g', 'delta_ln_b', 'delta_w_out', 'delta_mix_norm_odd', 'delta_pool_w', 'delta_pool_b', 'delta_pool_scale', 'delta_ffn_norm', 'delta_w_up', 'delta_ffn_conv_w', 'delta_ffn_conv_b', 'delta_w_down', 'delta_final_norm', 'new_m_meta_tokens', 'new_m_mix_norm_even', 'new_m_w_in', 'new_m_b_f', 'new_m_conv_w', 'new_m_conv_b', 'new_m_ln_g', 'new_m_ln_b', 'new_m_w_out', 'new_m_mix_norm_odd', 'new_m_pool_w', 'new_m_pool_b', 'new_m_pool_scale', 'new_m_ffn_norm', 'new_m_w_up', 'new_m_ffn_conv_w', 'new_m_ffn_conv_b', 'new_m_w_down', 'new_m_final_norm', 'new_v_meta_tokens', 'new_v_mix_norm_even', 'new_v_w_in', 'new_v_b_f', 'new_v_conv_w', 'new_v_conv_b', 'new_v_ln_g', 'new_v_ln_b', 'new_v_w_out', 'new_v_mix_norm_odd', 'new_v_pool_w', 'new_v_pool_b', 'new_v_pool_scale', 'new_v_ffn_norm', 'new_v_w_up', 'new_v_ffn_conv_w', 'new_v_ffn_conv_b', 'new_v_w_down', 'new_v_final_norm']
TWIN_LEAF_KINDS = {'loss': 'loss', 'grad_x': 'grad_x', 'grad_meta_tokens': 'grad_w', 'grad_mix_norm_even': 'grad_w', 'grad_w_in': 'grad_w', 'grad_b_f': 'grad_w', 'grad_conv_w': 'grad_w', 'grad_conv_b': 'grad_w', 'grad_ln_g': 'grad_w', 'grad_ln_b': 'grad_w', 'grad_w_out': 'grad_w', 'grad_mix_norm_odd': 'grad_w', 'grad_pool_w': 'grad_w', 'grad_pool_b': 'grad_w', 'grad_pool_scale': 'grad_w', 'grad_ffn_norm': 'grad_w', 'grad_w_up': 'grad_w', 'grad_ffn_conv_w': 'grad_w', 'grad_ffn_conv_b': 'grad_w', 'grad_w_down': 'grad_w', 'grad_final_norm': 'grad_w', 'delta_meta_tokens': 'delta_w', 'delta_mix_norm_even': 'delta_w', 'delta_w_in': 'delta_w', 'delta_b_f': 'delta_w', 'delta_conv_w': 'delta_w', 'delta_conv_b': 'delta_w', 'delta_ln_g': 'delta_w', 'delta_ln_b': 'delta_w', 'delta_w_out': 'delta_w', 'delta_mix_norm_odd': 'delta_w', 'delta_pool_w': 'delta_w', 'delta_pool_b': 'delta_w', 'delta_pool_scale': 'delta_w', 'delta_ffn_norm': 'delta_w', 'delta_w_up': 'delta_w', 'delta_ffn_conv_w': 'delta_w', 'delta_ffn_conv_b': 'delta_w', 'delta_w_down': 'delta_w', 'delta_final_norm': 'delta_w', 'new_m_meta_tokens': 'new_m', 'new_m_mix_norm_even': 'new_m', 'new_m_w_in': 'new_m', 'new_m_b_f': 'new_m', 'new_m_conv_w': 'new_m', 'new_m_conv_b': 'new_m', 'new_m_ln_g': 'new_m', 'new_m_ln_b': 'new_m', 'new_m_w_out': 'new_m', 'new_m_mix_norm_odd': 'new_m', 'new_m_pool_w': 'new_m', 'new_m_pool_b': 'new_m', 'new_m_pool_scale': 'new_m', 'new_m_ffn_norm': 'new_m', 'new_m_w_up': 'new_m', 'new_m_ffn_conv_w': 'new_m', 'new_m_ffn_conv_b': 'new_m', 'new_m_w_down': 'new_m', 'new_m_final_norm': 'new_m', 'new_v_meta_tokens': 'new_v', 'new_v_mix_norm_even': 'new_v', 'new_v_w_in': 'new_v', 'new_v_b_f': 'new_v', 'new_v_conv_w': 'new_v', 'new_v_conv_b': 'new_v', 'new_v_ln_g': 'new_v', 'new_v_ln_b': 'new_v', 'new_v_w_out': 'new_v', 'new_v_mix_norm_odd': 'new_v', 'new_v_pool_w': 'new_v', 'new_v_pool_b': 'new_v', 'new_v_pool_scale': 'new_v', 'new_v_ffn_norm': 'new_v', 'new_v_w_up': 'new_v', 'new_v_ffn_conv_w': 'new_v', 'new_v_ffn_conv_b': 'new_v', 'new_v_w_down': 'new_v', 'new_v_final_norm': 'new_v'}


def _forward(args):
    return _fwd_reference(*[args[k] for k in FWD_PARAMS])


def _output_shape():
    out = _jax.eval_shape(lambda: _forward(_fwd_setup_inputs(0)))
    return out.shape, out.dtype

N_MICROBATCH = 1
ADAM_LR = 0.001
ADAM_B1 = 0.9
ADAM_B2 = 0.999
ADAM_EPS = 1e-08
ADAM_WD = 0.01
ADAM_STEP = 10
PER_EXAMPLE_BATCH_AXIS = {'x': 0, 'loss_target': 0}
SHARED_INPUTS = []
_WEIGHT_DTYPES = {'meta_tokens': _jnp.float32, 'mix_norm_even': _jnp.float32, 'w_in': _jnp.float32, 'b_f': _jnp.float32, 'conv_w': _jnp.float32, 'conv_b': _jnp.float32, 'ln_g': _jnp.float32, 'ln_b': _jnp.float32, 'w_out': _jnp.float32, 'mix_norm_odd': _jnp.float32, 'pool_w': _jnp.float32, 'pool_b': _jnp.float32, 'pool_scale': _jnp.float32, 'ffn_norm': _jnp.float32, 'w_up': _jnp.float32, 'ffn_conv_w': _jnp.float32, 'ffn_conv_b': _jnp.float32, 'w_down': _jnp.float32, 'final_norm': _jnp.float32}
MOMENT_SCALE = {'meta_tokens': 4.819918e-03, 'mix_norm_even': 7.789585e-02, 'w_in': 4.968963e-02, 'b_f': 2.506935e-01, 'conv_w': 8.530188e-02, 'conv_b': 1.438201e-01, 'ln_g': 9.580888e-02, 'ln_b': 7.794114e-02, 'w_out': 6.458120e-02, 'mix_norm_odd': 4.566895e-02, 'pool_w': 4.552459e-02, 'pool_b': 5.297330e-02, 'pool_scale': 1.111346e-01, 'ffn_norm': 8.849685e-02, 'w_up': 3.640254e-02, 'ffn_conv_w': 3.657595e-02, 'ffn_conv_b': 3.528567e-02, 'w_down': 5.955030e-02, 'final_norm': 1.601485e+01}


def _to_microbatches(a, axis):
    t = _jnp.moveaxis(a, axis, 0)
    t = t.reshape((N_MICROBATCH, t.shape[0] // N_MICROBATCH) + t.shape[1:])
    return _jnp.moveaxis(t, 1, axis + 1)


def setup_inputs(seed: int = 0) -> dict:
    inp = _fwd_setup_inputs(seed)
    key = _jax.random.fold_in(_jax.random.key(seed), 7919)
    shape, _ = _output_shape()
    out = dict(inp)
    out["loss_target"] = _jax.random.normal(_jax.random.fold_in(key, 0), shape, _jnp.float32)
    for i, name in enumerate(TWIN_WEIGHTS):
        w = inp[name].astype(_jnp.float32)
        if MOMENT_SCALE is None:
            s = _jnp.sqrt(_jnp.mean(_jnp.square(w)) + 1e-30)
        else:
            s = MOMENT_SCALE[name]
        km, kv = _jax.random.split(_jax.random.fold_in(key, i + 1))
        out[name] = w
        out["m_" + name] = s * _jax.random.normal(km, w.shape, _jnp.float32)
        out["v_" + name] = (s * s) * _jax.random.uniform(kv, w.shape, _jnp.float32, 0.5, 1.5)
    if N_MICROBATCH > 1:
        for name, axis in PER_EXAMPLE_BATCH_AXIS.items():
            out[name] = _to_microbatches(out[name], axis)
    return {'x': out['x'], 'meta_tokens': out['meta_tokens'], 'mix_norm_even': out['mix_norm_even'], 'w_in': out['w_in'], 'b_f': out['b_f'], 'conv_w': out['conv_w'], 'conv_b': out['conv_b'], 'ln_g': out['ln_g'], 'ln_b': out['ln_b'], 'w_out': out['w_out'], 'mix_norm_odd': out['mix_norm_odd'], 'pool_w': out['pool_w'], 'pool_b': out['pool_b'], 'pool_scale': out['pool_scale'], 'ffn_norm': out['ffn_norm'], 'w_up': out['w_up'], 'ffn_conv_w': out['ffn_conv_w'], 'ffn_conv_b': out['ffn_conv_b'], 'w_down': out['w_down'], 'final_norm': out['final_norm'], 'loss_target': out['loss_target'], 'm_meta_tokens': out['m_meta_tokens'], 'm_mix_norm_even': out['m_mix_norm_even'], 'm_w_in': out['m_w_in'], 'm_b_f': out['m_b_f'], 'm_conv_w': out['m_conv_w'], 'm_conv_b': out['m_conv_b'], 'm_ln_g': out['m_ln_g'], 'm_ln_b': out['m_ln_b'], 'm_w_out': out['m_w_out'], 'm_mix_norm_odd': out['m_mix_norm_odd'], 'm_pool_w': out['m_pool_w'], 'm_pool_b': out['m_pool_b'], 'm_pool_scale': out['m_pool_scale'], 'm_ffn_norm': out['m_ffn_norm'], 'm_w_up': out['m_w_up'], 'm_ffn_conv_w': out['m_ffn_conv_w'], 'm_ffn_conv_b': out['m_ffn_conv_b'], 'm_w_down': out['m_w_down'], 'm_final_norm': out['m_final_norm'], 'v_meta_tokens': out['v_meta_tokens'], 'v_mix_norm_even': out['v_mix_norm_even'], 'v_w_in': out['v_w_in'], 'v_b_f': out['v_b_f'], 'v_conv_w': out['v_conv_w'], 'v_conv_b': out['v_conv_b'], 'v_ln_g': out['v_ln_g'], 'v_ln_b': out['v_ln_b'], 'v_w_out': out['v_w_out'], 'v_mix_norm_odd': out['v_mix_norm_odd'], 'v_pool_w': out['v_pool_w'], 'v_pool_b': out['v_pool_b'], 'v_pool_scale': out['v_pool_scale'], 'v_ffn_norm': out['v_ffn_norm'], 'v_w_up': out['v_w_up'], 'v_ffn_conv_w': out['v_ffn_conv_w'], 'v_ffn_conv_b': out['v_ffn_conv_b'], 'v_w_down': out['v_w_down'], 'v_final_norm': out['v_final_norm']}


def _loss(weights, diff, rest, loss_target):
    with _jax.named_scope("forward"):
        args = {**rest, TWIN_DIFF_INPUT: diff, **{k: w.astype(_WEIGHT_DTYPES[k]) for k, w in weights.items()}}
        y = _forward(args)
    with _jax.named_scope("loss_head"):
        err = _jnp.square(y.astype(_jnp.float32) - loss_target)
        return 0.5 * _jnp.sum(_jnp.mean(err, axis=-1)) if err.ndim else 0.5 * err


def _adamw(w, g, m, v):
    m = ADAM_B1 * m + (1.0 - ADAM_B1) * g
    v = ADAM_B2 * v + (1.0 - ADAM_B2) * _jnp.square(g)
    m_hat = m / (1.0 - ADAM_B1 ** ADAM_STEP)
    v_hat = v / (1.0 - ADAM_B2 ** ADAM_STEP)
    delta = -ADAM_LR * (m_hat / (_jnp.sqrt(v_hat) + ADAM_EPS) + ADAM_WD * w)
    return delta, m, v


def reference(x, meta_tokens, mix_norm_even, w_in, b_f, conv_w, conv_b, ln_g, ln_b, w_out, mix_norm_odd, pool_w, pool_b, pool_scale, ffn_norm, w_up, ffn_conv_w, ffn_conv_b, w_down, final_norm, loss_target, m_meta_tokens, m_mix_norm_even, m_w_in, m_b_f, m_conv_w, m_conv_b, m_ln_g, m_ln_b, m_w_out, m_mix_norm_odd, m_pool_w, m_pool_b, m_pool_scale, m_ffn_norm, m_w_up, m_ffn_conv_w, m_ffn_conv_b, m_w_down, m_final_norm, v_meta_tokens, v_mix_norm_even, v_w_in, v_b_f, v_conv_w, v_conv_b, v_ln_g, v_ln_b, v_w_out, v_mix_norm_odd, v_pool_w, v_pool_b, v_pool_scale, v_ffn_norm, v_w_up, v_ffn_conv_w, v_ffn_conv_b, v_w_down, v_final_norm):
    given = dict(x=x, meta_tokens=meta_tokens, mix_norm_even=mix_norm_even, w_in=w_in, b_f=b_f, conv_w=conv_w, conv_b=conv_b, ln_g=ln_g, ln_b=ln_b, w_out=w_out, mix_norm_odd=mix_norm_odd, pool_w=pool_w, pool_b=pool_b, pool_scale=pool_scale, ffn_norm=ffn_norm, w_up=w_up, ffn_conv_w=ffn_conv_w, ffn_conv_b=ffn_conv_b, w_down=w_down, final_norm=final_norm, loss_target=loss_target, m_meta_tokens=m_meta_tokens, m_mix_norm_even=m_mix_norm_even, m_w_in=m_w_in, m_b_f=m_b_f, m_conv_w=m_conv_w, m_conv_b=m_conv_b, m_ln_g=m_ln_g, m_ln_b=m_ln_b, m_w_out=m_w_out, m_mix_norm_odd=m_mix_norm_odd, m_pool_w=m_pool_w, m_pool_b=m_pool_b, m_pool_scale=m_pool_scale, m_ffn_norm=m_ffn_norm, m_w_up=m_w_up, m_ffn_conv_w=m_ffn_conv_w, m_ffn_conv_b=m_ffn_conv_b, m_w_down=m_w_down, m_final_norm=m_final_norm, v_meta_tokens=v_meta_tokens, v_mix_norm_even=v_mix_norm_even, v_w_in=v_w_in, v_b_f=v_b_f, v_conv_w=v_conv_w, v_conv_b=v_conv_b, v_ln_g=v_ln_g, v_ln_b=v_ln_b, v_w_out=v_w_out, v_mix_norm_odd=v_mix_norm_odd, v_pool_w=v_pool_w, v_pool_b=v_pool_b, v_pool_scale=v_pool_scale, v_ffn_norm=v_ffn_norm, v_w_up=v_w_up, v_ffn_conv_w=v_ffn_conv_w, v_ffn_conv_b=v_ffn_conv_b, v_w_down=v_w_down, v_final_norm=v_final_norm)
    weights = {n: given[n] for n in TWIN_WEIGHTS}
    shared = {n: given[n] for n in SHARED_INPUTS}
    per_example = {n: given[n] for n in ['x']}
    grad_fn = _jax.value_and_grad(_loss, argnums=(0, 1))

    def one_microbatch(ex, loss_target):
        ex = dict(ex)
        diff = ex.pop(TWIN_DIFF_INPUT)
        return grad_fn(weights, diff, {**shared, **ex}, loss_target)

    if N_MICROBATCH == 1:
        loss, (grad_w, grad_x) = one_microbatch(per_example, given["loss_target"])
    else:
        def body(carry, xs):
            loss_sum, grad_sum = carry
            l_k, (gw_k, gx_k) = one_microbatch(xs[0], xs[1])
            with _jax.named_scope("update"):
                return (loss_sum + l_k, _jax.tree.map(_jnp.add, grad_sum, gw_k)), gx_k

        init = (_jnp.zeros((), _jnp.float32), _jax.tree.map(_jnp.zeros_like, weights))
        (loss, grad_w), grad_x = _jax.lax.scan(body, init, (per_example, given["loss_target"]))
    with _jax.named_scope("update"):
        delta_w, new_m, new_v = {}, {}, {}
        for n in TWIN_WEIGHTS:
            delta_w[n], new_m[n], new_v[n] = _adamw(weights[n], grad_w[n], given["m_" + n], given["v_" + n])
    return (loss, grad_x, *[grad_w[n] for n in TWIN_WEIGHTS], *[delta_w[n] for n in TWIN_WEIGHTS],
            *[new_m[n] for n in TWIN_WEIGHTS], *[new_v[n] for n in TWIN_WEIGHTS])
```

```python
import functools

import jax
import jax.numpy as jnp
from jax import lax
from jax.experimental import pallas as pl
from jax.experimental.pallas import tpu as pltpu

F32 = jnp.float32
BF16 = jnp.bfloat16

N_DEV = 8
DEPTH = 2
D_MODEL = 1024
N_META = 16
FOX_HEADS = 8
FOX_HEAD_DIM = 64
FOX_WIDTH = 512
CONV_CH = 512
CONV_WIDTH = 31
POOL_WINDOWS = (2, 4, 8, 16)
POOL_GROUP = 256
D_FF = 2816
FFN_CONV_WIDTH = 3
RMS_EPS = 1e-6
LN_EPS = 1e-5
ADAM_LR = 0.001
ADAM_B1 = 0.9
ADAM_B2 = 0.999
ADAM_EPS = 1e-08
ADAM_WD = 0.01
ADAM_STEP = 10

CHUNK = 128
HALO = 32
NEG_BIG = -1e30
FLAT_W = 1024


def _round_up(n, m):
    return (n + m - 1) // m * m


def _sigmoid(x):
    return 1.0 / (1.0 + jnp.exp(-x))


def _fold8(p):
    acc = p[0:8, :]
    for r in range(1, p.shape[0] // 8):
        acc = acc + p[8 * r:8 * r + 8, :]
    return acc


def _mm(a, b, *, name, tb=False, tm=None, tn=None, tk=None, out_dtype=F32, res=None,
        a_map=None, b_map=None, o_map=None, out_shape=None, dims=None, b_layer=None):
    if dims is None:
        m, k = a.shape
        n = b.shape[-2] if tb else b.shape[-1]
    else:
        m, n, k = dims
    tm, tn, tk = tm or m, tn or n, tk or k
    assert m % tm == 0 and n % tn == 0 and k % tk == 0, (name, m, n, k, tm, tn, tk)
    nk = k // tk
    a_map = a_map or (lambda i, j, kk: (i, kk))
    b_map = b_map or ((lambda i, j, kk: (j, kk)) if tb else (lambda i, j, kk: (kk, j)))
    o_map = o_map or (lambda i, j, kk: (i, j))
    out_shape = out_shape or (m, n)
    contract = (((1,), (1,)), ((), ())) if tb else (((1,), (0,)), ((), ()))
    has_res = res is not None

    def body(*refs):
        a_ref, b_ref = refs[0], refs[1]
        res_ref = refs[2] if has_res else None
        o_ref = refs[3] if has_res else refs[2]
        p = lax.dot_general(a_ref[...], b_ref[...], contract, preferred_element_type=F32)
        if nk == 1:
            if has_res:
                p = p + res_ref[...]
            o_ref[...] = p.astype(o_ref.dtype)
        else:
            acc_ref = refs[-1]
            kk = pl.program_id(2)

            @pl.when(kk == 0)
            def _():
                acc_ref[...] = p

            @pl.when(kk > 0)
            def _():
                acc_ref[...] += p

            @pl.when(kk == nk - 1)
            def _():
                r = acc_ref[...]
                if has_res:
                    r = r + res_ref[...]
                o_ref[...] = r.astype(o_ref.dtype)

    b_block = (tn, tk) if tb else (tk, tn)
    if b_layer is None:
        b_spec = pl.BlockSpec(b_block, b_map)
    else:
        b_spec = pl.BlockSpec((None,) + b_block, lambda i, j, kk: (b_layer,) + tuple(b_map(i, j, kk)))
    in_specs = [pl.BlockSpec((tm, tk), a_map), b_spec]
    operands = [a, b]
    if has_res:
        in_specs.append(pl.BlockSpec((tm, tn), o_map))
        operands.append(res)
    return pl.pallas_call(
        body, name=name, grid=(m // tm, n // tn, nk),
        in_specs=in_specs, out_specs=pl.BlockSpec((tm, tn), o_map),
        out_shape=jax.ShapeDtypeStruct(out_shape, out_dtype),
        scratch_shapes=[pltpu.VMEM((tm, tn), F32)] if nk > 1 else [],
    )(*operands)


def _transpose(x, *, name, out_dtype, cols=None, col0=0):
    r, c = x.shape
    cols = cols or c
    assert r % CHUNK == 0 and col0 % cols == 0
    cb = col0 // cols

    def body(x_ref, o_ref):
        o_ref[...] = x_ref[...].astype(F32).T.astype(o_ref.dtype)

    return pl.pallas_call(
        body, name=name, grid=(r // CHUNK,),
        in_specs=[pl.BlockSpec((CHUNK, cols), lambda i: (i, cb))],
        out_specs=pl.BlockSpec((cols, CHUNK), lambda i: (0, i)),
        out_shape=jax.ShapeDtypeStruct((cols, r), out_dtype),
    )(x)


def _rms_fwd(x, g, *, name, out_dtype):
    lp, dm = x.shape
    tr = lp // 4

    def body(x_ref, g_ref, o_ref):
        xv = x_ref[...]
        r = lax.rsqrt(jnp.mean(xv * xv, axis=-1, keepdims=True) + RMS_EPS)
        o_ref[...] = (xv * r * g_ref[...]).astype(o_ref.dtype)

    return pl.pallas_call(
        body, name=name, grid=(lp // tr,),
        in_specs=[pl.BlockSpec((tr, dm), lambda i: (i, 0)), pl.BlockSpec((1, dm), lambda i: (0, 0))],
        out_specs=pl.BlockSpec((tr, dm), lambda i: (i, 0)),
        out_shape=jax.ShapeDtypeStruct((lp, dm), out_dtype),
    )(x, g)


def _rms_bwd(x, g, dn, dres, *, name):
    lp, dm = x.shape
    tr = lp // 4

    def body(x_ref, g_ref, dn_ref, dres_ref, dh_ref, dhb_ref, dg_ref):
        xv = x_ref[...]
        r = lax.rsqrt(jnp.mean(xv * xv, axis=-1, keepdims=True) + RMS_EPS)
        xhat = xv * r
        dnv = dn_ref[...]

        @pl.when(pl.program_id(0) == 0)
        def _():
            dg_ref[...] = jnp.zeros_like(dg_ref)

        dg_ref[...] += jnp.sum(dnv * xhat, axis=0, keepdims=True)
        dxhat = dnv * g_ref[...]
        dx = r * (dxhat - xhat * jnp.mean(dxhat * xhat, axis=-1, keepdims=True))
        dh = dres_ref[...] + dx
        dh_ref[...] = dh
        dhb_ref[...] = dh.astype(BF16)

    row = pl.BlockSpec((tr, dm), lambda i: (i, 0))
    vec = pl.BlockSpec((1, dm), lambda i: (0, 0))
    return pl.pallas_call(
        body, name=name, grid=(lp // tr,),
        in_specs=[row, vec, row, row], out_specs=[row, row, vec],
        out_shape=[jax.ShapeDtypeStruct((lp, dm), F32), jax.ShapeDtypeStruct((lp, dm), BF16),
                   jax.ShapeDtypeStruct((1, dm), F32)],
    )(x, g, dn, dres)


def _loss_head(h, g, tgt, n_real, *, name):
    lp, dm = h.shape
    tr = lp // 4

    def body(x_ref, g_ref, t_ref, loss_ref, dh_ref, dhb_ref, dg_ref):
        i = pl.program_id(0)
        xv = x_ref[...]
        r = lax.rsqrt(jnp.mean(xv * xv, axis=-1, keepdims=True) + RMS_EPS)
        xhat = xv * r
        gv = g_ref[...]
        y = xhat * gv
        t = i * tr + lax.broadcasted_iota(jnp.int32, (tr, 1), 0)
        valid = (t >= N_META) & (t < n_real)
        diff = jnp.where(valid, y - t_ref[...], 0.0)

        @pl.when(i == 0)
        def _():
            loss_ref[...] = jnp.zeros_like(loss_ref)
            dg_ref[...] = jnp.zeros_like(dg_ref)

        row_sq = jnp.sum(diff * diff, axis=-1, keepdims=True) * (1.0 / dm)
        part = 0.5 * jnp.sum(row_sq, axis=0, keepdims=True)
        loss_ref[...] += jnp.broadcast_to(part, loss_ref.shape)
        dy = diff * (1.0 / dm)
        dg_ref[...] += jnp.sum(dy * xhat, axis=0, keepdims=True)
        dxhat = dy * gv
        dx = r * (dxhat - xhat * jnp.mean(dxhat * xhat, axis=-1, keepdims=True))
        dh_ref[...] = dx
        dhb_ref[...] = dx.astype(BF16)

    row = pl.BlockSpec((tr, dm), lambda i: (i, 0))
    vec = pl.BlockSpec((1, dm), lambda i: (0, 0))
    return pl.pallas_call(
        body, name=name, grid=(lp // tr,),
        in_specs=[row, vec, row],
        out_specs=[pl.BlockSpec((1, 128), lambda i: (0, 0)), row, row, vec],
        out_shape=[jax.ShapeDtypeStruct((1, 128), F32), jax.ShapeDtypeStruct((lp, dm), F32),
                   jax.ShapeDtypeStruct((lp, dm), BF16), jax.ShapeDtypeStruct((1, dm), F32)],
    )(h, g, tgt)


def _tri(upper):
    r = lax.broadcasted_iota(jnp.int32, (CHUNK, CHUNK), 0)
    c = lax.broadcasted_iota(jnp.int32, (CHUNK, CHUNK), 1)
    return jnp.where(r <= c if upper else r >= c, 1.0, 0.0).astype(F32)


def _fox_prep(f_t, b_f, *, name):
    nh, lp = f_t.shape
    nch = lp // CHUNK

    def body(f_ref, b_ref, c_ref):
        tri = _tri(True)
        carry = jnp.zeros((nh, 1), F32)
        for blk in range(nch):
            cols = slice(blk * CHUNK, (blk + 1) * CHUNK)
            z = f_ref[:, cols] + b_ref[...]
            logf = jnp.minimum(z, 0.0) - jnp.log(1.0 + jnp.exp(-jnp.abs(z)))
            cb = jnp.dot(logf, tri, preferred_element_type=F32, precision=lax.Precision.HIGHEST)
            c_ref[:, cols] = cb + carry
            carry = carry + jnp.sum(logf, axis=1, keepdims=True)

    return pl.pallas_call(
        body, name=name, out_shape=jax.ShapeDtypeStruct((nh, lp), F32),
    )(f_t, b_f)


def _fox_bwd(dc, f_t, b_f, *, name):
    nh, lp = f_t.shape
    nch = lp // CHUNK

    def body(dc_ref, f_ref, b_ref, df_ref, db_ref):
        tri = _tri(False)
        carry = jnp.zeros((nh, 1), F32)
        db = jnp.zeros((nh, 1), F32)
        df_ref[...] = jnp.zeros_like(df_ref)
        for blk in reversed(range(nch)):
            cols = slice(blk * CHUNK, (blk + 1) * CHUNK)
            dcb = dc_ref[:, cols]
            dlogf = jnp.dot(dcb, tri, preferred_element_type=F32, precision=lax.Precision.HIGHEST) + carry
            carry = carry + jnp.sum(dcb, axis=1, keepdims=True)
            z = f_ref[:, cols] + b_ref[...]
            dz = dlogf * _sigmoid(-z)
            df_ref[0:nh, cols] = dz
            db = db + jnp.sum(dz, axis=1, keepdims=True)
        db_ref[...] = db

    return pl.pallas_call(
        body, name=name,
        out_shape=[jax.ShapeDtypeStruct((128, lp), F32), jax.ShapeDtypeStruct((nh, 1), F32)],
    )(dc, f_t, b_f)


def _attn_blocks(lp):
    tq = lp // 4
    return tq, [(i * tq, min(lp, _round_up((i + 1) * tq, CHUNK))) for i in range(4)]


def _attn_probs(q2, k_h, c_row, row0, n):
    tq = q2.shape[0]
    s = lax.dot_general(q2, k_h, (((1,), (1,)), ((), ())), preferred_element_type=F32)
    s = s * (FOX_HEAD_DIM ** -0.5) - c_row
    t = row0 + lax.broadcasted_iota(jnp.int32, (tq, n), 0)
    sidx = lax.broadcasted_iota(jnp.int32, (tq, n), 1)
    s = jnp.where(sidx <= t, s, NEG_BIG)
    p = jnp.exp(s - jnp.max(s, axis=1, keepdims=True))
    return p / jnp.sum(p, axis=1, keepdims=True)


def _attn_fwd(qkv, c3, *, name):
    lp = qkv.shape[0]
    tq, blocks = _attn_blocks(lp)

    def body(q_ref, k_ref, v_ref, c_ref, o_ref):
        lane = lax.broadcasted_iota(jnp.int32, (1, 128), 1)
        zero = jnp.zeros((), BF16)
        for i, (row0, n) in enumerate(blocks):
            q2 = q_ref[row0:row0 + tq, :]
            acc = jnp.zeros((tq, 128), F32)
            for hd in range(2):
                sel = (lane < 64) if hd == 0 else (lane >= 64)
                k_h = jnp.where(sel, k_ref[0:n, :], zero)
                v_h = jnp.where(sel, v_ref[0:n, :], zero)
                p = _attn_probs(q2, k_h, c_ref[hd:hd + 1, 0:n], row0, n)
                acc = acc + jnp.dot(p.astype(BF16), v_h, preferred_element_type=F32)
            o_ref[row0:row0 + tq, :] = acc.astype(BF16)

    blk = lambda off: pl.BlockSpec((lp, 128), lambda p: (0, off + p))
    return pl.pallas_call(
        body, name=name, grid=(4,),
        in_specs=[blk(0), blk(4), blk(8), pl.BlockSpec((None, 2, lp), lambda p: (p, 0, 0))],
        out_specs=pl.BlockSpec((lp, 128), lambda p: (0, p)),
        out_shape=jax.ShapeDtypeStruct((lp, FOX_WIDTH), BF16),
    )(qkv, qkv, qkv, c3)


def _attn_bwd(qkv, q_t, dcat, do_t, c3, *, name):
    lp = qkv.shape[0]
    tq, blocks = _attn_blocks(lp)
    scale = FOX_HEAD_DIM ** -0.5

    def body(q_ref, k_ref, v_ref, qt_ref, do_ref, dot_ref, c_ref, dq_ref, dkt_ref, dvt_ref, dc_ref,
             dkt_acc, dvt_acc):
        lane = lax.broadcasted_iota(jnp.int32, (1, 128), 1)
        sub = lax.broadcasted_iota(jnp.int32, (128, 1), 0)
        zero = jnp.zeros((), BF16)
        dkt_acc[...] = jnp.zeros_like(dkt_acc)
        dvt_acc[...] = jnp.zeros_like(dvt_acc)
        dc_ref[...] = jnp.zeros_like(dc_ref)
        for i, (row0, n) in enumerate(blocks):
            rows = slice(row0, row0 + tq)
            q2 = q_ref[rows, :]
            do2 = do_ref[rows, :].astype(BF16)
            dq_acc = jnp.zeros((tq, 128), F32)
            for hd in range(2):
                sel = (lane < 64) if hd == 0 else (lane >= 64)
                sel_t = (sub < 64) if hd == 0 else (sub >= 64)
                k_h = jnp.where(sel, k_ref[0:n, :], zero)
                v_h = jnp.where(sel, v_ref[0:n, :], zero)
                p = _attn_probs(q2, k_h, c_ref[hd:hd + 1, 0:n], row0, n)
                dp = lax.dot_general(do2, v_h, (((1,), (1,)), ((), ())), preferred_element_type=F32)
                delta = jnp.sum(p * dp, axis=1, keepdims=True)
                ds = p * (dp - delta)
                dsb = ds.astype(BF16)
                dq_acc = dq_acc + jnp.dot(dsb, k_h, preferred_element_type=F32)
                qt_h = jnp.where(sel_t, qt_ref[:, rows], zero)
                dot_h = jnp.where(sel_t, dot_ref[:, rows], zero)
                dkt_acc[:, 0:n] += jnp.dot(qt_h, dsb, preferred_element_type=F32)
                dvt_acc[:, 0:n] += jnp.dot(dot_h, p.astype(BF16), preferred_element_type=F32)
                dc_ref[hd:hd + 1, 0:n] -= jnp.sum(ds, axis=0, keepdims=True)
            dq_ref[rows, :] = (dq_acc * scale).astype(BF16)
        dkt_ref[...] = (dkt_acc[...] * scale).astype(BF16)
        dvt_ref[...] = dvt_acc[...].astype(BF16)

    blk = lambda off: pl.BlockSpec((lp, 128), lambda p: (0, off + p))
    blk_t = pl.BlockSpec((128, lp), lambda p: (p, 0))
    c_spec = pl.BlockSpec((None, 2, lp), lambda p: (p, 0, 0))
    return pl.pallas_call(
        body, name=name, grid=(4,),
        in_specs=[blk(0), blk(4), blk(8), blk_t, blk(0), blk_t, c_spec],
        out_specs=[blk(0), blk_t, blk_t, c_spec],
        out_shape=[jax.ShapeDtypeStruct((lp, FOX_WIDTH), BF16), jax.ShapeDtypeStruct((FOX_WIDTH, lp), BF16),
                   jax.ShapeDtypeStruct((FOX_WIDTH, lp), BF16), jax.ShapeDtypeStruct((4, 2, lp), F32)],
        scratch_shapes=[pltpu.VMEM((128, lp), F32), pltpu.VMEM((128, lp), F32)],
    )(qkv, qkv, qkv, q_t, dcat, do_t, c3)


def _ln_stats(x):
    mu = jnp.mean(x, axis=-1, keepdims=True)
    xc = x - mu
    var = jnp.mean(xc * xc, axis=-1, keepdims=True)
    rstd = lax.rsqrt(var + LN_EPS)
    return xc * rstd, rstd


def _conv_fwd(agf, conv_w, conv_b, ln_g, ln_b, *, name):
    lp = agf.shape[0]
    nch = lp // CHUNK
    c = CONV_CH

    def body(a_ref, g_ref, w_ref, b_ref, lg_ref, lb_ref, u0_ref, u1_ref, u3_ref, u0s):
        u0s[0:HALO, :] = jnp.zeros((HALO, c), F32)

        def glu(ci, _):
            rows = pl.ds(pl.multiple_of(ci * CHUNK, CHUNK), CHUNK)
            u0 = a_ref[rows, :] * _sigmoid(g_ref[rows, :])
            u0_ref[rows, :] = u0
            u0s[pl.ds(pl.multiple_of(ci * CHUNK + HALO, 8), CHUNK), :] = u0
            return 0

        lax.fori_loop(0, nch, glu, 0)

        def conv(ci, _):
            r0 = pl.multiple_of(ci * CHUNK, CHUNK)
            rows = pl.ds(r0, CHUNK)
            for lg in range(c // 128):
                lanes = slice(lg * 128, (lg + 1) * 128)
                win = u0s[pl.ds(r0, CHUNK + HALO), lanes]
                acc = jnp.broadcast_to(b_ref[:, lanes], (CHUNK, 128))
                for k in range(CONV_WIDTH):
                    s = CONV_WIDTH - 1 - k
                    sh = win if s == 0 else pltpu.roll(win, s, 0)
                    acc = acc + w_ref[k:k + 1, lanes] * sh[HALO:HALO + CHUNK, :]
                u1_ref[rows, lanes] = acc
            xhat, _ = _ln_stats(u1_ref[rows, :])
            y = xhat * lg_ref[...] + lb_ref[...]
            u3_ref[rows, :] = (y * _sigmoid(y)).astype(BF16)
            return 0

        lax.fori_loop(0, nch, conv, 0)

    full = lambda shape: pl.BlockSpec(shape, lambda i: (0, 0))
    return pl.pallas_call(
        body, name=name, grid=(1,),
        in_specs=[pl.BlockSpec((lp, c), lambda i: (0, 0)), pl.BlockSpec((lp, c), lambda i: (0, 1)),
                  full((CONV_WIDTH, c)), full((1, c)), full((1, c)), full((1, c))],
        out_specs=[full((lp, c)), full((lp, c)), full((lp, c))],
        out_shape=[jax.ShapeDtypeStruct((lp, c), F32), jax.ShapeDtypeStruct((lp, c), F32),
                   jax.ShapeDtypeStruct((lp, c), BF16)],
        scratch_shapes=[pltpu.VMEM((lp + HALO, c), F32)],
    )(agf, agf, conv_w, conv_b, ln_g, ln_b)


def _conv_bwd(dcat, u0, u1, agf, conv_w, ln_g, ln_b, *, name):
    lp = agf.shape[0]
    nch = lp // CHUNK
    c = CONV_CH
    wlen = CHUNK + HALO

    def body(du3_ref, u0_ref, u1_ref, a_ref, g_ref, w_ref, lg_ref, lb_ref,
             dag_ref, dw_ref, db_ref, dlg_ref, dlb_ref, du1s, u0s, dwacc, vacc):
        du1s[lp:lp + HALO, :] = jnp.zeros((HALO, c), F32)
        u0s[0:HALO, :] = jnp.zeros((HALO, c), F32)
        dwacc[...] = jnp.zeros_like(dwacc)
        vacc[...] = jnp.zeros_like(vacc)

        def ln_bwd(ci, _):
            r0 = pl.multiple_of(ci * CHUNK, CHUNK)
            rows = pl.ds(r0, CHUNK)
            xhat, rstd = _ln_stats(u1_ref[rows, :])
            y = xhat * lg_ref[...] + lb_ref[...]
            sg = _sigmoid(y)
            du2 = du3_ref[rows, :] * (sg * (1.0 + y * (1.0 - sg)))
            vacc[0:8, :] += _fold8(du2 * xhat)
            vacc[8:16, :] += _fold8(du2)
            dxhat = du2 * lg_ref[...]
            du1 = rstd * (dxhat - jnp.mean(dxhat, axis=-1, keepdims=True)
                          - xhat * jnp.mean(dxhat * xhat, axis=-1, keepdims=True))
            vacc[16:24, :] += _fold8(du1)
            du1s[rows, :] = du1
            u0s[pl.ds(pl.multiple_of(ci * CHUNK + HALO, 8), CHUNK), :] = u0_ref[rows, :]
            return 0

        lax.fori_loop(0, nch, ln_bwd, 0)

        def conv_bwd(ci, _):
            r0 = pl.multiple_of(ci * CHUNK, CHUNK)
            rows = pl.ds(r0, CHUNK)
            for lg in range(c // 128):
                lanes = slice(lg * 128, (lg + 1) * 128)
                dwin = du1s[pl.ds(r0, wlen), lanes]
                uwin = u0s[pl.ds(r0, wlen), lanes]
                du1 = dwin[0:CHUNK, :]
                acc = jnp.zeros((CHUNK, 128), F32)
                for k in range(CONV_WIDTH):
                    s = CONV_WIDTH - 1 - k
                    dsh = dwin if s == 0 else pltpu.roll(dwin, wlen - s, 0)
                    acc = acc + w_ref[k:k + 1, lanes] * dsh[0:CHUNK, :]
                    ush = uwin if s == 0 else pltpu.roll(uwin, s, 0)
                    dwacc[8 * k:8 * k + 8, lanes] += _fold8(du1 * ush[HALO:HALO + CHUNK, :])
                sg = _sigmoid(g_ref[rows, lanes])
                a = a_ref[rows, lanes]
                dag_ref[rows, lanes] = (acc * sg).astype(BF16)
                dag_ref[rows, slice(c + lg * 128, c + (lg + 1) * 128)] = (acc * a * sg * (1.0 - sg)).astype(BF16)
            return 0

        lax.fori_loop(0, nch, conv_bwd, 0)
        for k in range(CONV_WIDTH):
            dw_ref[k:k + 1, :] = jnp.sum(dwacc[8 * k:8 * k + 8, :], axis=0, keepdims=True)
        dlg_ref[...] = jnp.sum(vacc[0:8, :], axis=0, keepdims=True)
        dlb_ref[...] = jnp.sum(vacc[8:16, :], axis=0, keepdims=True)
        db_ref[...] = jnp.sum(vacc[16:24, :], axis=0, keepdims=True)

    full = lambda shape: pl.BlockSpec(shape, lambda i: (0, 0))
    vec = jax.ShapeDtypeStruct((1, c), F32)
    return pl.pallas_call(
        body, name=name, grid=(1,),
        in_specs=[pl.BlockSpec((lp, c), lambda i: (0, 1)), full((lp, c)), full((lp, c)),
                  pl.BlockSpec((lp, c), lambda i: (0, 0)), pl.BlockSpec((lp, c), lambda i: (0, 1)),
                  full((CONV_WIDTH, c)), full((1, c)), full((1, c))],
        out_specs=[full((lp, 2 * c)), full((CONV_WIDTH, c)), full((1, c)), full((1, c)), full((1, c))],
        out_shape=[jax.ShapeDtypeStruct((lp, 2 * c), BF16), jax.ShapeDtypeStruct((CONV_WIDTH, c), F32), vec, vec, vec],
        scratch_shapes=[pltpu.VMEM((lp + HALO, c), F32), pltpu.VMEM((lp + HALO, c), F32),
                        pltpu.VMEM((8 * CONV_WIDTH, c), F32), pltpu.VMEM((24, c), F32)],
    )(dcat, u0, u1, agf, agf, conv_w, ln_g, ln_b)


FFN_TILE = 256
FFN_PAD = 8


def _ffn_conv(xs, w_ref, b_ref, half, r0):
    win = xs[half, pl.ds(r0, CHUNK + FFN_PAD), :]
    acc = jnp.broadcast_to(b_ref[half], (CHUNK, FFN_TILE))
    for k in range(FFN_CONV_WIDTH):
        s = FFN_CONV_WIDTH - 1 - k
        sh = win if s == 0 else pltpu.roll(win, s, 0)
        acc = acc + w_ref[half, k:k + 1, :] * sh[FFN_PAD:FFN_PAD + CHUNK, :]
    return acc


def _ffn_act_fwd(up3, w3, b3, *, name):
    _, lp, f = up3.shape
    nch = lp // CHUNK

    def body(up_ref, w_ref, b_ref, act_ref, xs):
        for half in range(2):
            xs[half, 0:FFN_PAD, :] = jnp.zeros((FFN_PAD, FFN_TILE), F32)
            xs[half, FFN_PAD:FFN_PAD + lp, :] = up_ref[half]

        def chunk(ci, _):
            r0 = pl.multiple_of(ci * CHUNK, CHUNK)
            gate = _ffn_conv(xs, w_ref, b_ref, 0, r0)
            val = _ffn_conv(xs, w_ref, b_ref, 1, r0)
            act_ref[pl.ds(r0, CHUNK), :] = (gate * _sigmoid(gate) * val).astype(BF16)
            return 0

        lax.fori_loop(0, nch, chunk, 0)

    return pl.pallas_call(
        body, name=name, grid=(f // FFN_TILE,),
        in_specs=[pl.BlockSpec((2, lp, FFN_TILE), lambda j: (0, 0, j)),
                  pl.BlockSpec((2, FFN_CONV_WIDTH, FFN_TILE), lambda j: (0, 0, j)),
                  pl.BlockSpec((2, 1, FFN_TILE), lambda j: (0, 0, j))],
        out_specs=pl.BlockSpec((lp, FFN_TILE), lambda j: (0, j)),
        out_shape=jax.ShapeDtypeStruct((lp, f), BF16),
        scratch_shapes=[pltpu.VMEM((2, lp + FFN_PAD, FFN_TILE), F32)],
    )(up3, w3, b3)


def _ffn_act_bwd(up3, w3, b3, dact, *, name):
    _, lp, f = up3.shape
    nch = lp // CHUNK
    wlen = CHUNK + FFN_PAD

    def body(up_ref, w_ref, b_ref, dact_ref, dup_ref, dw_ref, db_ref, xs, ds, wacc):
        for half in range(2):
            xs[half, 0:FFN_PAD, :] = jnp.zeros((FFN_PAD, FFN_TILE), F32)
            xs[half, FFN_PAD:FFN_PAD + lp, :] = up_ref[half]
            ds[half, lp:lp + FFN_PAD, :] = jnp.zeros((FFN_PAD, FFN_TILE), F32)
        wacc[...] = jnp.zeros_like(wacc)

        def act_bwd(ci, _):
            r0 = pl.multiple_of(ci * CHUNK, CHUNK)
            rows = pl.ds(r0, CHUNK)
            gate = _ffn_conv(xs, w_ref, b_ref, 0, r0)
            val = _ffn_conv(xs, w_ref, b_ref, 1, r0)
            sg = _sigmoid(gate)
            da = dact_ref[rows, :]
            ds[0, rows, :] = da * val * (sg * (1.0 + gate * (1.0 - sg)))
            ds[1, rows, :] = da * (gate * sg)
            return 0

        lax.fori_loop(0, nch, act_bwd, 0)

        def conv_bwd(ci, _):
            r0 = pl.multiple_of(ci * CHUNK, CHUNK)
            rows = pl.ds(r0, CHUNK)
            for half in range(2):
                dwin = ds[half, pl.ds(r0, wlen), :]
                xwin = xs[half, pl.ds(r0, wlen), :]
                d0 = dwin[0:CHUNK, :]
                acc = jnp.zeros((CHUNK, FFN_TILE), F32)
                for k in range(FFN_CONV_WIDTH):
                    s = FFN_CONV_WIDTH - 1 - k
                    dsh = dwin if s == 0 else pltpu.roll(dwin, wlen - s, 0)
                    acc = acc + w_ref[half, k:k + 1, :] * dsh[0:CHUNK, :]
                    xsh = xwin if s == 0 else pltpu.roll(xwin, s, 0)
                    wacc[half, 8 * k:8 * k + 8, :] += _fold8(d0 * xsh[FFN_PAD:FFN_PAD + CHUNK, :])
                wacc[half, 24:32, :] += _fold8(d0)
                dup_ref[half, rows, :] = acc.astype(BF16)
            return 0

        lax.fori_loop(0, nch, conv_bwd, 0)
        for half in range(2):
            for k in range(FFN_CONV_WIDTH):
                dw_ref[half, k:k + 1, :] = jnp.sum(wacc[half, 8 * k:8 * k + 8, :], axis=0, keepdims=True)
            db_ref[half] = jnp.sum(wacc[half, 24:32, :], axis=0, keepdims=True)

    return pl.pallas_call(
        body, name=name, grid=(f // FFN_TILE,),
        in_specs=[pl.BlockSpec((2, lp, FFN_TILE), lambda j: (0, 0, j)),
                  pl.BlockSpec((2, FFN_CONV_WIDTH, FFN_TILE), lambda j: (0, 0, j)),
                  pl.BlockSpec((2, 1, FFN_TILE), lambda j: (0, 0, j)),
                  pl.BlockSpec((lp, FFN_TILE), lambda j: (0, j))],
        out_specs=[pl.BlockSpec((2, lp, FFN_TILE), lambda j: (0, 0, j)),
                   pl.BlockSpec((2, FFN_CONV_WIDTH, FFN_TILE), lambda j: (0, 0, j)),
                   pl.BlockSpec((2, 1, FFN_TILE), lambda j: (0, 0, j))],
        out_shape=[jax.ShapeDtypeStruct((2, lp, f), BF16), jax.ShapeDtypeStruct((2, FFN_CONV_WIDTH, f), F32),
                   jax.ShapeDtypeStruct((2, 1, f), F32)],
        scratch_shapes=[pltpu.VMEM((2, lp + FFN_PAD, FFN_TILE), F32), pltpu.VMEM((2, lp + FFN_PAD, FFN_TILE), F32),
                        pltpu.VMEM((2, 32, FFN_TILE), F32)],
    )(up3, w3, b3, dact)


POOL_PAD = 16


def _inv_count(r0, w):
    t = r0 + lax.broadcasted_iota(jnp.int32, (CHUNK, 1), 0)
    return 1.0 / jnp.minimum(t + 1, w).astype(F32)


def _pool_fwd(n, pool_w, pool_b, pool_scale, h, *, name):
    lp, dm = n.shape
    nch = lp // CHUNK
    g = POOL_GROUP

    def body(n_ref, w_ref, b_ref, s_ref, h_ref, ho_ref, d_ref, z_ref, xs):
        gi = pl.program_id(0)
        xs[0:POOL_PAD, :] = jnp.zeros((POOL_PAD, g), F32)
        xs[POOL_PAD:POOL_PAD + lp, :] = n_ref[...]
        for idx, w in enumerate(POOL_WINDOWS):
            @pl.when(gi == idx)
            def _(w=w):
                def chunk(ci, _):
                    r0 = pl.multiple_of(ci * CHUNK, CHUNK)
                    win = xs[pl.ds(r0, CHUNK + POOL_PAD), :]
                    acc = win
                    for j in range(1, w):
                        acc = acc + pltpu.roll(win, j, 0)
                    x = win[POOL_PAD:POOL_PAD + CHUNK, :]
                    d_ref[pl.ds(r0, CHUNK), :] = (acc[POOL_PAD:POOL_PAD + CHUNK, :] * _inv_count(r0, w) - x).astype(BF16)
                    return 0

                lax.fori_loop(0, nch, chunk, 0)

        z = jnp.dot(d_ref[...], w_ref[...], preferred_element_type=F32) + b_ref[...]
        z_ref[...] = z
        ho_ref[...] = h_ref[...] + z * s_ref[...]

    col = pl.BlockSpec((lp, g), lambda i: (0, i))
    vec = pl.BlockSpec((1, g), lambda i: (0, i))
    return pl.pallas_call(
        body, name=name, grid=(len(POOL_WINDOWS),),
        in_specs=[col, pl.BlockSpec((None, g, g), lambda i: (i, 0, 0)), vec, vec, col],
        out_specs=[col, col, col],
        out_shape=[jax.ShapeDtypeStruct((lp, dm), F32), jax.ShapeDtypeStruct((lp, dm), BF16),
                   jax.ShapeDtypeStruct((lp, dm), F32)],
        scratch_shapes=[pltpu.VMEM((lp + POOL_PAD, g), F32)],
    )(n, pool_w, pool_b, pool_scale, h)


def _pool_bwd(dy, z, pool_w, pool_scale, *, name):
    lp, dm = dy.shape
    nch = lp // CHUNK
    g = POOL_GROUP
    wlen = CHUNK + POOL_PAD

    def body(dy_ref, z_ref, w_ref, s_ref, dn_ref, dz_ref, dsc_ref, db_ref, ys, dd):
        gi = pl.program_id(0)
        dyv = dy_ref[...]
        dsc_ref[...] = jnp.sum(dyv * z_ref[...], axis=0, keepdims=True)
        dz = dyv * s_ref[...]
        db_ref[...] = jnp.sum(dz, axis=0, keepdims=True)
        dzb = dz.astype(BF16)
        dz_ref[...] = dzb
        dd[...] = lax.dot_general(dzb, w_ref[...], (((1,), (1,)), ((), ())), preferred_element_type=F32)
        ys[lp:lp + POOL_PAD, :] = jnp.zeros((POOL_PAD, g), F32)
        for idx, w in enumerate(POOL_WINDOWS):
            @pl.when(gi == idx)
            def _(w=w):
                def scale(ci, _):
                    r0 = pl.multiple_of(ci * CHUNK, CHUNK)
                    ys[pl.ds(r0, CHUNK), :] = dd[pl.ds(r0, CHUNK), :] * _inv_count(r0, w)
                    return 0

                lax.fori_loop(0, nch, scale, 0)

                def chunk(ci, _):
                    r0 = pl.multiple_of(ci * CHUNK, CHUNK)
                    win = ys[pl.ds(r0, wlen), :]
                    acc = win
                    for j in range(1, w):
                        acc = acc + pltpu.roll(win, wlen - j, 0)
                    dn_ref[pl.ds(r0, CHUNK), :] = acc[0:CHUNK, :] - dd[pl.ds(r0, CHUNK), :]
                    return 0

                lax.fori_loop(0, nch, chunk, 0)

    col = pl.BlockSpec((lp, g), lambda i: (0, i))
    vec = pl.BlockSpec((1, g), lambda i: (0, i))
    return pl.pallas_call(
        body, name=name, grid=(len(POOL_WINDOWS),),
        in_specs=[col, col, pl.BlockSpec((None, g, g), lambda i: (i, 0, 0)), vec],
        out_specs=[col, col, vec, vec],
        out_shape=[jax.ShapeDtypeStruct((lp, dm), F32), jax.ShapeDtypeStruct((lp, dm), BF16),
                   jax.ShapeDtypeStruct((1, dm), F32), jax.ShapeDtypeStruct((1, dm), F32)],
        scratch_shapes=[pltpu.VMEM((lp + POOL_PAD, g), F32), pltpu.VMEM((lp, g), F32)],
    )(dy, z, pool_w, pool_scale)


def _ffn_fwd(h, g, w_up, w3, b3, w_down, tag):
    lp = h.shape[0]
    nj = D_FF // FFN_TILE
    n = _rms_fwd(h, g, name=f"rms_ffn{tag}", out_dtype=BF16)
    up2 = _mm(n, w_up, name=f"mm_up{tag}", tn=FFN_TILE, dims=(lp, 2 * D_FF, D_MODEL), b_layer=tag,
              o_map=lambda i, j, k: (j // nj, j % nj), out_shape=(2 * lp, D_FF))
    up3 = up2.reshape(2, lp, D_FF)
    act = _ffn_act_fwd(up3, w3, b3, name=f"ffn_act{tag}")
    h_out = _mm(act, w_down, name=f"mm_down{tag}", tn=256, res=h, b_layer=tag)
    return h_out, (n, up3, act)


def _ffn_bwd(dh, dhb, h, g, saved, w_up, w3, b3, w_down, tag):
    lp = h.shape[0]
    nj = D_FF // FFN_TILE
    n, up3, act = saved
    act_t = _transpose(act, name=f"t_act{tag}", out_dtype=BF16)
    dw_down = _mm(act_t, dhb, name=f"mm_dwdown{tag}", tm=704, out_dtype=BF16)
    dact = _mm(dhb, w_down, name=f"mm_dact{tag}", tb=True, tn=256, b_layer=tag)
    dup3, dcw, dcb = _ffn_act_bwd(up3, w3, b3, dact, name=f"ffn_act_bwd{tag}")
    dup2 = dup3.reshape(2 * lp, D_FF)
    n_t = _transpose(n, name=f"t_nffn{tag}", out_dtype=BF16)
    dw_up = _mm_dw_up(n_t, dup2, name=f"mm_dwup{tag}")
    dn = _mm(dup2, w_up, name=f"mm_dnffn{tag}", tb=True, tk=FFN_TILE, dims=(lp, D_MODEL, 2 * D_FF), b_layer=tag,
             a_map=lambda i, j, k: (k // nj, k % nj), b_map=lambda i, j, k: (0, k))
    dh_in, dh_in_b, dg = _rms_bwd(h, g, dn, dh, name=f"rms_bwd_ffn{tag}")
    return dh_in, dh_in_b, (dg, dw_up, dcw, dcb, dw_down)


def _local_step(x, tgt, wt):
    seq = x.shape[0]
    n_real = N_META + seq
    lp = _round_up(n_real, CHUNK)
    pad = jnp.zeros((lp - n_real, D_MODEL), F32)
    h0 = jnp.concatenate([wt["meta"], x, pad], axis=0)
    tgt_p = jnp.concatenate([jnp.zeros((N_META, D_MODEL), F32), tgt, pad], axis=0)
    w_in_p = wt["w_in_p"]

    n0 = _rms_fwd(h0, wt["g_even"], name="rms_even", out_dtype=BF16)
    qkv = _mm(n0, w_in_p, name="mm_qkv", tn=512, dims=(lp, 3 * FOX_WIDTH, D_MODEL), out_dtype=BF16)
    ag = _mm(n0, w_in_p, name="mm_ag", tn=512, dims=(lp, 2 * CONV_CH, D_MODEL),
             b_map=lambda i, j, k: (0, 3 + j))
    f_t = _mm(wt["wf_t"], n0, name="mm_ft", tb=True)
    c_row = _fox_prep(f_t, wt["b_f"], name="fox_prep")
    c3 = c_row.reshape(4, 2, lp)
    o = _attn_fwd(qkv, c3, name="attn_fwd")
    u0, u1, u3 = _conv_fwd(ag, wt["conv_w"], wt["conv_b"], wt["ln_g"], wt["ln_b"], name="conv_fwd")
    cat = jnp.concatenate([o, u3], axis=1)
    h1 = _mm(cat, wt["w_out"], name="mm_out", tn=256, res=h0)
    h2, saved0 = _ffn_fwd(h1, wt["ffn_norm"][0:1], wt["w_up"], wt["fcw3"][0], wt["fcb3"][0], wt["w_down"], 0)

    n2 = _rms_fwd(h2, wt["g_odd"], name="rms_odd", out_dtype=F32)
    h3, dpool, z = _pool_fwd(n2, wt["pool_w"], wt["pool_b"], wt["pool_scale"], h2, name="pool_fwd")
    h4, saved1 = _ffn_fwd(h3, wt["ffn_norm"][1:2], wt["w_up"], wt["fcw3"][1], wt["fcb3"][1], wt["w_down"], 1)

    loss, dh4, dh4b, d_gfinal = _loss_head(h4, wt["g_final"], tgt_p, n_real, name="loss_head")

    dh3, dh3b, gf1 = _ffn_bwd(dh4, dh4b, h3, wt["ffn_norm"][1:2], saved1, wt["w_up"], wt["fcw3"][1],
                              wt["fcb3"][1], wt["w_down"], 1)
    dn2, dzb, d_pscale, d_pb = _pool_bwd(dh3, z, wt["pool_w"], wt["pool_scale"], name="pool_bwd")
    dpool_t = _transpose(dpool, name="t_dpool", out_dtype=BF16)
    d_pw = _mm(dpool_t, dzb, name="mm_dpoolw", tm=POOL_GROUP, tn=POOL_GROUP, dims=(D_MODEL, POOL_GROUP, lp),
               b_map=lambda i, j, k: (0, i), o_map=lambda i, j, k: (i, 0), out_shape=(D_MODEL, POOL_GROUP),
               out_dtype=BF16)
    dh2, dh2b, d_godd = _rms_bwd(h2, wt["g_odd"], dn2, dh3, name="rms_bwd_odd")
    dh1, dh1b, gf0 = _ffn_bwd(dh2, dh2b, h1, wt["ffn_norm"][0:1], saved0, wt["w_up"], wt["fcw3"][0],
                              wt["fcb3"][0], wt["w_down"], 0)

    cat_t = _transpose(cat, name="t_cat", out_dtype=BF16)
    d_wout = _mm(cat_t, dh1b, name="mm_dwout", tm=512, out_dtype=BF16)
    dcat = _mm(dh1b, wt["w_out"], name="mm_dcat", tb=True, tn=256)
    q_t = _transpose(qkv, name="t_q", out_dtype=BF16, cols=FOX_WIDTH)
    do_t = _transpose(dcat, name="t_do", out_dtype=BF16, cols=FOX_WIDTH)
    dq, dk_t, dv_t, dc3 = _attn_bwd(qkv, q_t, dcat, do_t, c3, name="attn_bwd")
    dk = _transpose(dk_t, name="t_dk", out_dtype=BF16)
    dv = _transpose(dv_t, name="t_dv", out_dtype=BF16)
    df_t, d_bf = _fox_bwd(dc3.reshape(FOX_HEADS, lp), f_t, wt["b_f"], name="fox_bwd")
    df = _transpose(df_t, name="t_df", out_dtype=BF16)
    dag, d_convw, d_convb, d_lng, d_lnb = _conv_bwd(dcat, u0, u1, ag, wt["conv_w"], wt["ln_g"], wt["ln_b"],
                                                    name="conv_bwd")
    dproj = jnp.concatenate([dq, dk, dv, dag, df], axis=1)
    n0_t = _transpose(n0, name="t_n0", out_dtype=BF16)
    d_win = _mm_dw_in(n0_t, dproj, name="mm_dwin")
    dn0 = _mm(dproj, w_in_p, name="mm_dn0", tb=True, tk=384)
    dh0, _, d_geven = _rms_bwd(h0, wt["g_even"], dn0, dh1, name="rms_bwd_even")

    two = DEPTH
    layers = lambda i: jnp.stack([gf0[i], gf1[i]])
    grads = dict(
        meta_tokens=dh0[0:N_META], mix_norm_even=d_geven, w_in=d_win, b_f=d_bf.reshape(1, FOX_HEADS),
        conv_w=d_convw[None], conv_b=d_convb, ln_g=d_lng, ln_b=d_lnb, w_out=d_wout, mix_norm_odd=d_godd,
        pool_w=d_pw.reshape(len(POOL_WINDOWS), POOL_GROUP, POOL_GROUP),
        pool_b=d_pb.reshape(1, len(POOL_WINDOWS), POOL_GROUP), pool_scale=d_pscale,
        ffn_norm=jnp.concatenate([gf0[0], gf1[0]], axis=0), w_up=(gf0[1], gf1[1]),
        ffn_conv_w=layers(2).transpose(0, 2, 1, 3).reshape(two, FFN_CONV_WIDTH, 2 * D_FF),
        ffn_conv_b=layers(3).reshape(two, 2 * D_FF), w_down=(gf0[4], gf1[4]), final_norm=d_gfinal.reshape(D_MODEL))
    return loss, dh0[N_META:n_real], grads


_QKV = 3 * FOX_WIDTH
_GLU0 = _QKV + FOX_HEADS
_IN_COLS = _GLU0 + 2 * CONV_CH
_F_PAD = 128


_IN_SHARD = _IN_COLS // N_DEV
_UP_SHARD = 2 * D_FF // N_DEV
_ROW_TILE = 256


def _assemble_w_in(st, *, name):
    tr = _ROW_TILE

    def body(s_ref, o_ref):
        full = jnp.concatenate([s_ref[i].astype(F32) for i in range(N_DEV)], axis=1)
        parts = [full[:, :_QKV], full[:, _GLU0:], full[:, _QKV:_GLU0], jnp.zeros((tr, _F_PAD - FOX_HEADS), F32)]
        o_ref[...] = jnp.concatenate(parts, axis=1).astype(BF16)

    return pl.pallas_call(
        body, name=name, grid=(D_MODEL // tr,),
        in_specs=[pl.BlockSpec((N_DEV, tr, _IN_SHARD), lambda i: (0, i, 0))],
        out_specs=pl.BlockSpec((tr, _QKV + 2 * CONV_CH + _F_PAD), lambda i: (i, 0)),
        out_shape=jax.ShapeDtypeStruct((D_MODEL, _QKV + 2 * CONV_CH + _F_PAD), BF16),
    )(st)


def _assemble_w_up(st, *, name):
    tr = _ROW_TILE

    def body(s_ref, o_ref):
        o_ref[...] = jnp.concatenate([s_ref[0].astype(F32), s_ref[1].astype(F32)], axis=1).astype(BF16)

    return pl.pallas_call(
        body, name=name, grid=(DEPTH, D_MODEL // tr, N_DEV // 2),
        in_specs=[pl.BlockSpec((2, None, tr, _UP_SHARD), lambda l, r, p: (p, l, r, 0))],
        out_specs=pl.BlockSpec((None, tr, 2 * _UP_SHARD), lambda l, r, p: (l, r, p)),
        out_shape=jax.ShapeDtypeStruct((DEPTH, D_MODEL, 2 * D_FF), BF16),
    )(st)


def _mm_dw_in(n_t, dproj, *, name):
    dm, lp = n_t.shape
    tr = _ROW_TILE
    ag0 = _QKV + 2 * CONV_CH

    def body(a_ref, b_ref, o_ref):
        r = jnp.dot(a_ref[...], b_ref[...], preferred_element_type=F32)
        full = jnp.concatenate([r[:, :_QKV], r[:, ag0:ag0 + FOX_HEADS], r[:, _QKV:ag0]], axis=1)
        for i in range(N_DEV):
            o_ref[i] = full[:, i * _IN_SHARD:(i + 1) * _IN_SHARD].astype(BF16)

    return pl.pallas_call(
        body, name=name, grid=(dm // tr,),
        in_specs=[pl.BlockSpec((tr, lp), lambda i: (i, 0)), pl.BlockSpec(dproj.shape, lambda i: (0, 0))],
        out_specs=pl.BlockSpec((N_DEV, tr, _IN_SHARD), lambda i: (0, i, 0)),
        out_shape=jax.ShapeDtypeStruct((N_DEV, dm, _IN_SHARD), BF16),
    )(n_t, dproj)


def _mm_dw_up(n_t, dup2, *, name):
    dm, lp = n_t.shape
    pairs_per_half = D_FF // (2 * _UP_SHARD)

    def body(a_ref, b_ref, o_ref):
        r = jnp.dot(a_ref[...], b_ref[...], preferred_element_type=F32)
        o_ref[0] = r[:, :_UP_SHARD].astype(BF16)
        o_ref[1] = r[:, _UP_SHARD:].astype(BF16)

    return pl.pallas_call(
        body, name=name, grid=(N_DEV // 2,),
        in_specs=[pl.BlockSpec((dm, lp), lambda p: (0, 0)),
                  pl.BlockSpec((lp, 2 * _UP_SHARD), lambda p: (p // pairs_per_half, p % pairs_per_half))],
        out_specs=pl.BlockSpec((2, dm, _UP_SHARD), lambda p: (p, 0, 0)),
        out_shape=jax.ShapeDtypeStruct((N_DEV, dm, _UP_SHARD), BF16),
    )(n_t, dup2)


MESH = pl.DeviceIdType.MESH
ANY = pl.BlockSpec(memory_space=pl.ANY)


def _slot(px, py, pc):
    return 4 * px + 2 * py + pc


def _by_owner(ref, slot):
    return ref.at[slot]


def _row_block(rows, axis=0):
    def place(ref, slot):
        idx = (slice(None),) * axis + (pl.ds(slot * rows, rows),)
        return ref.at[idx]
    return place


def _all_gather(arrs, out_shapes, places, *, name):
    n = len(arrs)

    def body(*refs):
        ins, outs = refs[:n], refs[n:2 * n]
        send_sems, recv_sems, local_sems = refs[2 * n:]
        x, y, c = lax.axis_index("x"), lax.axis_index("y"), lax.axis_index("c")
        me, sibling = (x, y, c), (x, y, 1 - c)
        chips = [(1 - x, y), (x, 1 - y), (1 - x, 1 - y)]

        def copy(a, k, block, to, from_input=False):
            dst = places[a](outs[a], _slot(*block))
            return pltpu.make_async_remote_copy(
                src_ref=ins[a] if from_input else dst, dst_ref=dst,
                send_sem=send_sems.at[7 * a + k], recv_sem=recv_sems.at[7 * a + k],
                device_id=to, device_id_type=MESH)

        own, sent = [], []
        for a in range(n):
            mine = pltpu.make_async_copy(ins[a], places[a](outs[a], _slot(*me)), local_sems.at[a])
            mine.start()
            own.append(mine)
            first = [copy(a, 0, me, sibling, True)]
            first += [copy(a, 1 + j, me, (*chip, c), True) for j, chip in enumerate(chips)]
            for cp in first:
                cp.start()
            sent += first
        for a in range(n):
            for j, chip in enumerate(chips):
                copy(a, 1 + j, (*chip, c), me).wait_recv()
                passed = copy(a, 4 + j, (*chip, c), sibling)
                passed.start()
                sent.append(passed)
        for a in range(n):
            copy(a, 0, sibling, me).wait_recv()
            for j, chip in enumerate(chips):
                copy(a, 4 + j, (*chip, 1 - c), me).wait_recv()
        for cp in sent:
            cp.wait_send()
        for cp in own:
            cp.wait()

    return pl.pallas_call(
        body, name=name,
        in_specs=[ANY] * n, out_specs=[ANY] * n,
        out_shape=[jax.ShapeDtypeStruct(s, a.dtype) for s, a in zip(out_shapes, arrs)],
        scratch_shapes=[pltpu.SemaphoreType.DMA((7 * n,)), pltpu.SemaphoreType.DMA((7 * n,)),
                        pltpu.SemaphoreType.DMA((n,))],
    )(*arrs)


def _exchange(srcs, recv_structs, copies, *, name):
    ns, nc = len(srcs), len(copies)

    def body(*refs):
        src_refs, recv_refs = refs[:ns], refs[ns:ns + len(recv_structs)]
        send_sems, recv_sems, local_sems = refs[ns + len(recv_structs):]
        x, y, c = lax.axis_index("x"), lax.axis_index("y"), lax.axis_index("c")
        me = _slot(x, y, c)

        def peer(k):
            flip = lambda v, bit: 1 - v if bit else v
            return flip(x, k & 4), flip(y, k & 2), flip(c, k & 1)

        def copy(k, j, arriving):
            si, take, ri, put = copies[j]
            p = _slot(*peer(k))
            sem = (k - 1) * nc + j
            return pltpu.make_async_remote_copy(
                src_ref=take(src_refs[si], p), dst_ref=put(recv_refs[ri], p if arriving else me),
                send_sem=send_sems.at[sem], recv_sem=recv_sems.at[sem], device_id=peer(k), device_id_type=MESH)

        own = [pltpu.make_async_copy(take(src_refs[si], me), put(recv_refs[ri], me), local_sems.at[j])
               for j, (si, take, ri, put) in enumerate(copies)]
        for cp in own:
            cp.start()
        sent = [copy(k, j, False) for k in range(1, N_DEV) for j in range(nc)]
        for cp in sent:
            cp.start()
        for k in range(1, N_DEV):
            for j in range(nc):
                copy(k, j, True).wait_recv()
        for cp in sent:
            cp.wait_send()
        for cp in own:
            cp.wait()

    return pl.pallas_call(
        body, name=name,
        in_specs=[ANY] * ns, out_specs=[ANY] * len(recv_structs), out_shape=list(recv_structs),
        scratch_shapes=[pltpu.SemaphoreType.DMA((7 * nc,)), pltpu.SemaphoreType.DMA((7 * nc,)),
                        pltpu.SemaphoreType.DMA((nc,))],
    )(*srcs)


def _sum_slabs(stack, *, name):
    _, rows, w = stack.shape
    tr = CHUNK if rows % CHUNK == 0 else rows

    def body(s_ref, o_ref):
        acc = s_ref[0].astype(F32)
        for i in range(1, N_DEV):
            acc = acc + s_ref[i].astype(F32)
        o_ref[...] = acc

    return pl.pallas_call(
        body, name=name, grid=(rows // tr,),
        in_specs=[pl.BlockSpec((N_DEV, tr, w), lambda i: (0, i, 0))],
        out_specs=pl.BlockSpec((tr, w), lambda i: (i, 0)),
        out_shape=jax.ShapeDtypeStruct((rows, w), F32),
    )(stack)


def _adam_math(w, g, m, v):
    mn = ADAM_B1 * m + (1.0 - ADAM_B1) * g
    vn = ADAM_B2 * v + (1.0 - ADAM_B2) * (g * g)
    m_hat = mn / (1.0 - ADAM_B1 ** ADAM_STEP)
    v_hat = vn / (1.0 - ADAM_B2 ** ADAM_STEP)
    return -ADAM_LR * (m_hat / (jnp.sqrt(v_hat) + ADAM_EPS) + ADAM_WD * w), mn, vn


def _adamw(w, g, m, v, *, name):
    def body(w_ref, g_ref, m_ref, v_ref, d_ref, mo_ref, vo_ref):
        d_ref[...], mo_ref[...], vo_ref[...] = _adam_math(w_ref[...], g_ref[...], m_ref[...], v_ref[...])

    return pl.pallas_call(body, name=name, out_shape=[jax.ShapeDtypeStruct(w.shape, F32)] * 3)(w, g, m, v)


def _adamw_reduce(w, stack, m, v, tr, *, name):
    rows, cols = w.shape
    assert rows % tr == 0

    def body(w_ref, s_ref, m_ref, v_ref, g_ref, d_ref, mo_ref, vo_ref):
        g = s_ref[0].astype(F32)
        for i in range(1, N_DEV):
            g = g + s_ref[i].astype(F32)
        g_ref[...] = g
        d_ref[...], mo_ref[...], vo_ref[...] = _adam_math(w_ref[...], g, m_ref[...], v_ref[...])

    blk = pl.BlockSpec((tr, cols), lambda i: (i, 0))
    stk = pl.BlockSpec((N_DEV, tr, cols), lambda i: (0, i, 0))
    return pl.pallas_call(
        body, name=name, grid=(rows // tr,),
        in_specs=[blk, stk, blk, blk], out_specs=[blk] * 4,
        out_shape=[jax.ShapeDtypeStruct((rows, cols), F32)] * 4,
    )(w, stack, m, v)


_WEIGHTS = (
    ("meta_tokens", (16, 1024), 1), ("mix_norm_even", (1, 1024), None), ("w_in", (1, 1024, 2568), 2),
    ("b_f", (1, 8), None), ("conv_w", (1, 31, 512), 2), ("conv_b", (1, 512), None), ("ln_g", (1, 512), None),
    ("ln_b", (1, 512), None), ("w_out", (1, 1024, 1024), 1), ("mix_norm_odd", (1, 1024), 1),
    ("pool_w", (1, 4, 256, 256), 2), ("pool_b", (1, 4, 256), 2), ("pool_scale", (1, 1024), 1),
    ("ffn_norm", (2, 1024), None), ("w_up", (2, 1024, 5632), 2), ("ffn_conv_w", (2, 3, 5632), 2),
    ("ffn_conv_b", (2, 5632), None), ("w_down", (2, 2816, 1024), 1), ("final_norm", (1024,), None),
)
_MATMUL_WEIGHTS = ("w_in", "w_out", "pool_w", "w_up", "w_down")
_ADAM_ROWS = dict(w_in=256, w_out=128, pool_w=128, w_up=256, w_down=352)


def _shard_shape(shape, axis):
    return shape[:axis] + (shape[axis] // N_DEV,) + shape[axis + 1:]


def _size(shape):
    n = 1
    for s in shape:
        n *= s
    return n


def _pack(parts, dtype, lead=(), align=16):
    flat = jnp.concatenate([p.reshape(lead + (-1,)).astype(dtype) for p in parts], axis=-1)
    n = flat.shape[-1]
    rows = _round_up(-(-n // FLAT_W), align)
    flat = jnp.pad(flat, [(0, 0)] * len(lead) + [(0, rows * FLAT_W - n)])
    return flat.reshape(lead + (rows, FLAT_W))


def _unpack(buf, shapes, lead=()):
    flat = buf.reshape(lead + (-1,))
    out, off = [], 0
    for shp in shapes:
        n = _size(shp)
        out.append(flat[..., off:off + n].reshape(lead + shp))
        off += n
    return out


def _gathered_to_full(stack, shape, axis):
    return jnp.moveaxis(stack, 0, axis).reshape(shape)


def _full_to_slabs(full, shape, axis):
    split = shape[:axis] + (N_DEV, shape[axis] // N_DEV) + shape[axis + 1:]
    return jnp.moveaxis(full.reshape(split), axis, 0)


def kernel(x, meta_tokens, mix_norm_even, w_in, b_f, conv_w, conv_b, ln_g, ln_b, w_out, mix_norm_odd, pool_w, pool_b, pool_scale, ffn_norm, w_up, ffn_conv_w, ffn_conv_b, w_down, final_norm, loss_target, m_meta_tokens, m_mix_norm_even, m_w_in, m_b_f, m_conv_w, m_conv_b, m_ln_g, m_ln_b, m_w_out, m_mix_norm_odd, m_pool_w, m_pool_b, m_pool_scale, m_ffn_norm, m_w_up, m_ffn_conv_w, m_ffn_conv_b, m_w_down, m_final_norm, v_meta_tokens, v_mix_norm_even, v_w_in, v_b_f, v_conv_w, v_conv_b, v_ln_g, v_ln_b, v_w_out, v_mix_norm_odd, v_pool_w, v_pool_b, v_pool_scale, v_ffn_norm, v_w_up, v_ffn_conv_w, v_ffn_conv_b, v_w_down, v_final_norm):
    names = [n for n, _, _ in _WEIGHTS]
    w_loc = dict(zip(names, (meta_tokens, mix_norm_even, w_in, b_f, conv_w, conv_b, ln_g, ln_b, w_out, mix_norm_odd,
                             pool_w, pool_b, pool_scale, ffn_norm, w_up, ffn_conv_w, ffn_conv_b, w_down, final_norm)))
    m_loc = dict(zip(names, (m_meta_tokens, m_mix_norm_even, m_w_in, m_b_f, m_conv_w, m_conv_b, m_ln_g, m_ln_b,
                             m_w_out, m_mix_norm_odd, m_pool_w, m_pool_b, m_pool_scale, m_ffn_norm, m_w_up,
                             m_ffn_conv_w, m_ffn_conv_b, m_w_down, m_final_norm)))
    v_loc = dict(zip(names, (v_meta_tokens, v_mix_norm_even, v_w_in, v_b_f, v_conv_w, v_conv_b, v_ln_g, v_ln_b,
                             v_w_out, v_mix_norm_odd, v_pool_w, v_pool_b, v_pool_scale, v_ffn_norm, v_w_up,
                             v_ffn_conv_w, v_ffn_conv_b, v_w_down, v_final_norm)))
    replicated = [(n, s) for n, s, a in _WEIGHTS if a is None]
    little = [(n, s, a) for n, s, a in _WEIGHTS if a is not None and n not in _MATMUL_WEIGHTS]
    little_shards = [_shard_shape(s, a) for _, s, a in little]
    out_rows, down_rows, pool_rows = D_MODEL // N_DEV, D_FF // N_DEV, POOL_GROUP // N_DEV
    n_groups = len(POOL_WINDOWS)

    little_pack = _pack([w_loc[n] for n, _, _ in little], F32)
    g_win, g_wout, g_poolw, g_wup, g_wdown, g_little = _all_gather(
        [w_in[0].astype(BF16), w_out[0].astype(BF16), pool_w[0].astype(BF16), w_up.astype(BF16),
         w_down.astype(BF16), little_pack],
        [(N_DEV, D_MODEL, _IN_SHARD), (D_MODEL, D_MODEL), (n_groups, POOL_GROUP, POOL_GROUP),
         (N_DEV, DEPTH, D_MODEL, _UP_SHARD), (DEPTH, D_FF, D_MODEL), (N_DEV,) + little_pack.shape],
        [_by_owner, _row_block(out_rows), _row_block(pool_rows, axis=1), _by_owner, _row_block(down_rows, axis=1),
         _by_owner],
        name="gather_weights")
    w_in_p = _assemble_w_in(g_win, name="assemble_w_in")
    full = {n: _gathered_to_full(st, s, a)
            for (n, s, a), st in zip(little, _unpack(g_little, little_shards, lead=(N_DEV,)))}
    f0 = _QKV + 2 * CONV_CH
    wt = dict(
        meta=full["meta_tokens"], g_even=mix_norm_even, w_in_p=w_in_p, wf_t=w_in_p[:, f0:f0 + FOX_HEADS].T,
        b_f=b_f.reshape(FOX_HEADS, 1), conv_w=full["conv_w"][0], conv_b=conv_b, ln_g=ln_g, ln_b=ln_b, w_out=g_wout,
        g_odd=full["mix_norm_odd"], pool_w=g_poolw, pool_b=full["pool_b"].reshape(1, D_MODEL),
        pool_scale=full["pool_scale"], ffn_norm=ffn_norm, w_up=_assemble_w_up(g_wup, name="assemble_w_up"),
        fcw3=full["ffn_conv_w"].reshape(DEPTH, FFN_CONV_WIDTH, 2, D_FF).transpose(0, 2, 1, 3),
        fcb3=ffn_conv_b.reshape(DEPTH, 2, 1, D_FF), w_down=g_wdown, g_final=final_norm.reshape(1, D_MODEL))

    loss_part, grad_x, g = _local_step(x[0], loss_target[0], wt)

    little_slabs = _pack([_full_to_slabs(g[n], s, a) for n, s, a in little], F32, lead=(N_DEV,))
    small = _pack([loss_part[:, 0:1]] + [g[n] for n, _ in replicated], F32, align=8)
    layer_put = lambda l: (lambda ref, slot: ref.at[slot, l])
    stack_of = lambda shape, dtype: jax.ShapeDtypeStruct((N_DEV,) + shape, dtype)
    recv_in, recv_out, recv_pool, recv_up, recv_down, recv_little, everyone = _exchange(
        [g["w_in"], g["w_out"], g["pool_w"], g["w_up"][0], g["w_up"][1], g["w_down"][0], g["w_down"][1],
         little_slabs, small],
        [stack_of((D_MODEL, _IN_SHARD), BF16), stack_of((out_rows, D_MODEL), BF16),
         stack_of((n_groups, pool_rows, POOL_GROUP), BF16), stack_of((DEPTH, D_MODEL, _UP_SHARD), BF16),
         stack_of((DEPTH, down_rows, D_MODEL), BF16), stack_of(little_slabs.shape[1:], F32),
         stack_of(small.shape, F32)],
        [(0, _by_owner, 0, _by_owner), (1, _row_block(out_rows), 1, _by_owner),
         (2, _row_block(pool_rows, axis=1), 2, _by_owner), (3, _by_owner, 3, layer_put(0)),
         (4, _by_owner, 3, layer_put(1)), (5, _row_block(down_rows), 4, layer_put(0)),
         (6, _row_block(down_rows), 4, layer_put(1)), (7, _by_owner, 5, _by_owner),
         (8, lambda ref, slot: ref, 6, _by_owner)],
        name="exchange_grads")

    grads, delta, new_m, new_v = {}, {}, {}, {}
    two_d = lambda shp: (_size(shp[:-1]), shp[-1])
    for n, stack in (("w_in", recv_in), ("w_out", recv_out), ("pool_w", recv_pool), ("w_up", recv_up),
                     ("w_down", recv_down)):
        shp = w_loc[n].shape
        outs = _adamw_reduce(w_loc[n].reshape(two_d(shp)), stack.reshape((N_DEV,) + two_d(shp)),
                             m_loc[n].reshape(two_d(shp)), v_loc[n].reshape(two_d(shp)), _ADAM_ROWS[n],
                             name=f"adamw_{n}")
        grads[n], delta[n], new_m[n], new_v[n] = (o.reshape(shp) for o in outs)
    g_little = _unpack(_sum_slabs(recv_little, name="sum_little"), little_shards)
    g_rep = _unpack(_sum_slabs(everyone, name="sum_replicated"), [(1, 1)] + [s for _, s in replicated])
    loss = g_rep[0].reshape(())
    grads.update({n: gl for (n, _, _), gl in zip(little, g_little)})
    grads.update({n: gr for (n, _), gr in zip(replicated, g_rep[1:])})
    for n in names:
        if n in _MATMUL_WEIGHTS:
            continue
        shp = w_loc[n].shape
        d, mn, vn = _adamw(w_loc[n].reshape(two_d(shp)), grads[n].reshape(two_d(shp)), m_loc[n].reshape(two_d(shp)),
                           v_loc[n].reshape(two_d(shp)), name=f"adamw_{n}")
        delta[n], new_m[n], new_v[n] = d.reshape(shp), mn.reshape(shp), vn.reshape(shp)
    return (loss, grad_x[None], *[grads[n] for n in names], *[delta[n] for n in names],
            *[new_m[n] for n in names], *[new_v[n] for n in names])
```

```python
import functools

import jax
import jax.numpy as jnp
from jax import lax
from jax.experimental import pallas as pl
from jax.experimental.pallas import tpu as pltpu

F32 = jnp.float32
BF16 = jnp.bfloat16

N_DEV = 8
DEPTH = 2
D_MODEL = 1024
N_META = 16
FOX_HEADS = 8
FOX_HEAD_DIM = 64
FOX_WIDTH = 512
CONV_CH = 512
CONV_WIDTH = 31
POOL_WINDOWS = (2, 4, 8, 16)
POOL_GROUP = 256
D_FF = 2816
FFN_CONV_WIDTH = 3
RMS_EPS = 1e-6
LN_EPS = 1e-5
ADAM_LR = 0.001
ADAM_B1 = 0.9
ADAM_B2 = 0.999
ADAM_EPS = 1e-08
ADAM_WD = 0.01
ADAM_STEP = 10

CHUNK = 128
HALO = 32
NEG_BIG = -1e30
FLAT_W = 1024


def _round_up(n, m):
    return (n + m - 1) // m * m


def _sigmoid(x):
    return 1.0 / (1.0 + jnp.exp(-x))


def _fold8(p):
    acc = p[0:8, :]
    for r in range(1, p.shape[0] // 8):
        acc = acc + p[8 * r:8 * r + 8, :]
    return acc


def _mm(a, b, *, name, tb=False, tm=None, tn=None, tk=None, out_dtype=F32, res=None,
        a_map=None, b_map=None, o_map=None, out_shape=None, dims=None, b_layer=None):
    if dims is None:
        m, k = a.shape
        n = b.shape[-2] if tb else b.shape[-1]
    else:
        m, n, k = dims
    tm, tn, tk = tm or m, tn or n, tk or k
    assert m % tm == 0 and n % tn == 0 and k % tk == 0, (name, m, n, k, tm, tn, tk)
    nk = k // tk
    a_map = a_map or (lambda i, j, kk: (i, kk))
    b_map = b_map or ((lambda i, j, kk: (j, kk)) if tb else (lambda i, j, kk: (kk, j)))
    o_map = o_map or (lambda i, j, kk: (i, j))
    out_shape = out_shape or (m, n)
    contract = (((1,), (1,)), ((), ())) if tb else (((1,), (0,)), ((), ()))
    has_res = res is not None

    def body(*refs):
        a_ref, b_ref = refs[0], refs[1]
        res_ref = refs[2] if has_res else None
        o_ref = refs[3] if has_res else refs[2]
        p = lax.dot_general(a_ref[...], b_ref[...], contract, preferred_element_type=F32)
        if nk == 1:
            if has_res:
                p = p + res_ref[...]
            o_ref[...] = p.astype(o_ref.dtype)
        else:
            acc_ref = refs[-1]
            kk = pl.program_id(2)

            @pl.when(kk == 0)
            def _():
                acc_ref[...] = p

            @pl.when(kk > 0)
            def _():
                acc_ref[...] += p

            @pl.when(kk == nk - 1)
            def _():
                r = acc_ref[...]
                if has_res:
                    r = r + res_ref[...]
                o_ref[...] = r.astype(o_ref.dtype)

    b_block = (tn, tk) if tb else (tk, tn)
    if b_layer is None:
        b_spec = pl.BlockSpec(b_block, b_map)
    else:
        b_spec = pl.BlockSpec((None,) + b_block, lambda i, j, kk: (b_layer,) + tuple(b_map(i, j, kk)))
    in_specs = [pl.BlockSpec((tm, tk), a_map), b_spec]
    operands = [a, b]
    if has_res:
        in_specs.append(pl.BlockSpec((tm, tn), o_map))
        operands.append(res)
    return pl.pallas_call(
        body, name=name, grid=(m // tm, n // tn, nk),
        in_specs=in_specs, out_specs=pl.BlockSpec((tm, tn), o_map),
        out_shape=jax.ShapeDtypeStruct(out_shape, out_dtype),
        scratch_shapes=[pltpu.VMEM((tm, tn), F32)] if nk > 1 else [],
    )(*operands)


def _transpose(x, *, name, out_dtype, cols=None, after=None):
    r, c = x.shape
    cols = cols or c
    assert r % CHUNK == 0

    def body(x_ref, *rest):
        o_ref = rest[-1]
        o_ref[...] = x_ref[...].astype(F32).T.astype(o_ref.dtype)

    return pl.pallas_call(
        body, name=name, grid=(r // CHUNK,),
        in_specs=[pl.BlockSpec((CHUNK, cols), lambda i: (i, 0))] + ([] if after is None else [ANY]),
        out_specs=pl.BlockSpec((cols, CHUNK), lambda i: (0, i)),
        out_shape=jax.ShapeDtypeStruct((cols, r), out_dtype),
    )(x, *([] if after is None else [after]))


def _rms_fwd(x, g, *, name, out_dtype):
    lp, dm = x.shape
    tr = lp // 4

    def body(x_ref, g_ref, o_ref):
        xv = x_ref[...]
        r = lax.rsqrt(jnp.mean(xv * xv, axis=-1, keepdims=True) + RMS_EPS)
        o_ref[...] = (xv * r * g_ref[...]).astype(o_ref.dtype)

    return pl.pallas_call(
        body, name=name, grid=(lp // tr,),
        in_specs=[pl.BlockSpec((tr, dm), lambda i: (i, 0)), pl.BlockSpec((1, dm), lambda i: (0, 0))],
        out_specs=pl.BlockSpec((tr, dm), lambda i: (i, 0)),
        out_shape=jax.ShapeDtypeStruct((lp, dm), out_dtype),
    )(x, g)


def _rms_bwd(x, g, dn, dres, *, name):
    lp, dm = x.shape
    tr = lp // 4

    def body(x_ref, g_ref, dn_ref, dres_ref, dh_ref, dhb_ref, dg_ref):
        xv = x_ref[...]
        r = lax.rsqrt(jnp.mean(xv * xv, axis=-1, keepdims=True) + RMS_EPS)
        xhat = xv * r
        dnv = dn_ref[...]

        @pl.when(pl.program_id(0) == 0)
        def _():
            dg_ref[...] = jnp.zeros_like(dg_ref)

        dg_ref[...] += jnp.sum(dnv * xhat, axis=0, keepdims=True)
        dxhat = dnv * g_ref[...]
        dx = r * (dxhat - xhat * jnp.mean(dxhat * xhat, axis=-1, keepdims=True))
        dh = dres_ref[...] + dx
        dh_ref[...] = dh
        dhb_ref[...] = dh.astype(BF16)

    row = pl.BlockSpec((tr, dm), lambda i: (i, 0))
    vec = pl.BlockSpec((1, dm), lambda i: (0, 0))
    return pl.pallas_call(
        body, name=name, grid=(lp // tr,),
        in_specs=[row, vec, row, row], out_specs=[row, row, vec],
        out_shape=[jax.ShapeDtypeStruct((lp, dm), F32), jax.ShapeDtypeStruct((lp, dm), BF16),
                   jax.ShapeDtypeStruct((1, dm), F32)],
    )(x, g, dn, dres)


def _loss_head(h, g, tgt, n_real, *, name):
    lp, dm = h.shape
    tr = lp // 4

    def body(x_ref, g_ref, t_ref, loss_ref, dh_ref, dhb_ref, dg_ref):
        i = pl.program_id(0)
        xv = x_ref[...]
        r = lax.rsqrt(jnp.mean(xv * xv, axis=-1, keepdims=True) + RMS_EPS)
        xhat = xv * r
        gv = g_ref[...]
        y = xhat * gv
        t = i * tr + lax.broadcasted_iota(jnp.int32, (tr, 1), 0)
        valid = (t >= N_META) & (t < n_real)
        diff = jnp.where(valid, y - t_ref[...], 0.0)

        @pl.when(i == 0)
        def _():
            loss_ref[...] = jnp.zeros_like(loss_ref)
            dg_ref[...] = jnp.zeros_like(dg_ref)

        row_sq = jnp.sum(diff * diff, axis=-1, keepdims=True) * (1.0 / dm)
        part = 0.5 * jnp.sum(row_sq, axis=0, keepdims=True)
        loss_ref[...] += jnp.broadcast_to(part, loss_ref.shape)
        dy = diff * (1.0 / dm)
        dg_ref[...] += jnp.sum(dy * xhat, axis=0, keepdims=True)
        dxhat = dy * gv
        dx = r * (dxhat - xhat * jnp.mean(dxhat * xhat, axis=-1, keepdims=True))
        dh_ref[...] = dx
        dhb_ref[...] = dx.astype(BF16)

    row = pl.BlockSpec((tr, dm), lambda i: (i, 0))
    vec = pl.BlockSpec((1, dm), lambda i: (0, 0))
    return pl.pallas_call(
        body, name=name, grid=(lp // tr,),
        in_specs=[row, vec, row],
        out_specs=[pl.BlockSpec((1, 128), lambda i: (0, 0)), row, row, vec],
        out_shape=[jax.ShapeDtypeStruct((1, 128), F32), jax.ShapeDtypeStruct((lp, dm), F32),
                   jax.ShapeDtypeStruct((lp, dm), BF16), jax.ShapeDtypeStruct((1, dm), F32)],
    )(h, g, tgt)


def _tri(upper):
    r = lax.broadcasted_iota(jnp.int32, (CHUNK, CHUNK), 0)
    c = lax.broadcasted_iota(jnp.int32, (CHUNK, CHUNK), 1)
    return jnp.where(r <= c if upper else r >= c, 1.0, 0.0).astype(F32)


def _fox_prep(f_t, b_f, *, name):
    nh, lp = f_t.shape
    nch = lp // CHUNK

    def body(f_ref, b_ref, c_ref):
        tri = _tri(True)
        carry = jnp.zeros((nh, 1), F32)
        for blk in range(nch):
            cols = slice(blk * CHUNK, (blk + 1) * CHUNK)
            z = f_ref[:, cols] + b_ref[...]
            logf = jnp.minimum(z, 0.0) - jnp.log(1.0 + jnp.exp(-jnp.abs(z)))
            cb = jnp.dot(logf, tri, preferred_element_type=F32, precision=lax.Precision.HIGHEST)
            c_ref[:, cols] = cb + carry
            carry = carry + jnp.sum(logf, axis=1, keepdims=True)

    return pl.pallas_call(
        body, name=name, out_shape=jax.ShapeDtypeStruct((nh, lp), F32),
    )(f_t, b_f)


def _fox_bwd(dc, f_t, b_f, *, name):
    nh, lp = f_t.shape
    nch = lp // CHUNK

    def body(dc_ref, f_ref, b_ref, df_ref, db_ref):
        tri = _tri(False)
        carry = jnp.zeros((nh, 1), F32)
        db = jnp.zeros((nh, 1), F32)
        df_ref[...] = jnp.zeros_like(df_ref)
        for blk in reversed(range(nch)):
            cols = slice(blk * CHUNK, (blk + 1) * CHUNK)
            dcb = dc_ref[:, cols]
            dlogf = jnp.dot(dcb, tri, preferred_element_type=F32, precision=lax.Precision.HIGHEST) + carry
            carry = carry + jnp.sum(dcb, axis=1, keepdims=True)
            z = f_ref[:, cols] + b_ref[...]
            dz = dlogf * _sigmoid(-z)
            df_ref[0:nh, cols] = dz
            db = db + jnp.sum(dz, axis=1, keepdims=True)
        db_ref[...] = db

    return pl.pallas_call(
        body, name=name,
        out_shape=[jax.ShapeDtypeStruct((128, lp), F32), jax.ShapeDtypeStruct((nh, 1), F32)],
    )(dc, f_t, b_f)


def _attn_blocks(lp):
    tq = lp // 4
    return tq, [(i * tq, min(lp, _round_up((i + 1) * tq, CHUNK))) for i in range(4)]


def _attn_probs(q2, k_h, c_row, row0, n):
    tq = q2.shape[0]
    s = lax.dot_general(q2, k_h, (((1,), (1,)), ((), ())), preferred_element_type=F32)
    s = s * (FOX_HEAD_DIM ** -0.5) - c_row
    t = row0 + lax.broadcasted_iota(jnp.int32, (tq, n), 0)
    sidx = lax.broadcasted_iota(jnp.int32, (tq, n), 1)
    s = jnp.where(sidx <= t, s, NEG_BIG)
    p = jnp.exp(s - jnp.max(s, axis=1, keepdims=True))
    return p / jnp.sum(p, axis=1, keepdims=True)


def _attn_fwd(qkv, c3, *, name):
    lp = qkv.shape[0]
    tq, blocks = _attn_blocks(lp)

    def body(q_ref, k_ref, v_ref, c_ref, o_ref):
        lane = lax.broadcasted_iota(jnp.int32, (1, 128), 1)
        zero = jnp.zeros((), BF16)
        for i, (row0, n) in enumerate(blocks):
            q2 = q_ref[row0:row0 + tq, :]
            acc = jnp.zeros((tq, 128), F32)
            for hd in range(2):
                sel = (lane < 64) if hd == 0 else (lane >= 64)
                k_h = jnp.where(sel, k_ref[0:n, :], zero)
                v_h = jnp.where(sel, v_ref[0:n, :], zero)
                p = _attn_probs(q2, k_h, c_ref[hd:hd + 1, 0:n], row0, n)
                acc = acc + jnp.dot(p.astype(BF16), v_h, preferred_element_type=F32)
            o_ref[row0:row0 + tq, :] = acc.astype(BF16)

    blk = lambda off: pl.BlockSpec((lp, 128), lambda p: (0, off + p))
    return pl.pallas_call(
        body, name=name, grid=(4,),
        in_specs=[blk(0), blk(4), blk(8), pl.BlockSpec((None, 2, lp), lambda p: (p, 0, 0))],
        out_specs=pl.BlockSpec((lp, 128), lambda p: (0, p)),
        out_shape=jax.ShapeDtypeStruct((lp, FOX_WIDTH), BF16),
    )(qkv, qkv, qkv, c3)


def _attn_bwd(qkv, q_t, dcat, do_t, c3, *, name):
    lp = qkv.shape[0]
    tq, blocks = _attn_blocks(lp)
    scale = FOX_HEAD_DIM ** -0.5

    def body(q_ref, k_ref, v_ref, qt_ref, do_ref, dot_ref, c_ref, dq_ref, dkt_ref, dvt_ref, dc_ref,
             dkt_acc, dvt_acc):
        lane = lax.broadcasted_iota(jnp.int32, (1, 128), 1)
        sub = lax.broadcasted_iota(jnp.int32, (128, 1), 0)
        zero = jnp.zeros((), BF16)
        dkt_acc[...] = jnp.zeros_like(dkt_acc)
        dvt_acc[...] = jnp.zeros_like(dvt_acc)
        dc_ref[...] = jnp.zeros_like(dc_ref)
        for i, (row0, n) in enumerate(blocks):
            rows = slice(row0, row0 + tq)
            q2 = q_ref[rows, :]
            do2 = do_ref[rows, :].astype(BF16)
            dq_acc = jnp.zeros((tq, 128), F32)
            for hd in range(2):
                sel = (lane < 64) if hd == 0 else (lane >= 64)
                sel_t = (sub < 64) if hd == 0 else (sub >= 64)
                k_h = jnp.where(sel, k_ref[0:n, :], zero)
                v_h = jnp.where(sel, v_ref[0:n, :], zero)
                p = _attn_probs(q2, k_h, c_ref[hd:hd + 1, 0:n], row0, n)
                dp = lax.dot_general(do2, v_h, (((1,), (1,)), ((), ())), preferred_element_type=F32)
                delta = jnp.sum(p * dp, axis=1, keepdims=True)
                ds = p * (dp - delta)
                dsb = ds.astype(BF16)
                dq_acc = dq_acc + jnp.dot(dsb, k_h, preferred_element_type=F32)
                qt_h = jnp.where(sel_t, qt_ref[:, rows], zero)
                dot_h = jnp.where(sel_t, dot_ref[:, rows], zero)
                dkt_acc[:, 0:n] += jnp.dot(qt_h, dsb, preferred_element_type=F32)
                dvt_acc[:, 0:n] += jnp.dot(dot_h, p.astype(BF16), preferred_element_type=F32)
                dc_ref[hd:hd + 1, 0:n] -= jnp.sum(ds, axis=0, keepdims=True)
            dq_ref[rows, :] = (dq_acc * scale).astype(BF16)
        dkt_ref[...] = (dkt_acc[...] * scale).astype(BF16)
        dvt_ref[...] = dvt_acc[...].astype(BF16)

    blk = lambda off: pl.BlockSpec((lp, 128), lambda p: (0, off + p))
    blk_t = pl.BlockSpec((128, lp), lambda p: (p, 0))
    c_spec = pl.BlockSpec((None, 2, lp), lambda p: (p, 0, 0))
    return pl.pallas_call(
        body, name=name, grid=(4,),
        in_specs=[blk(0), blk(4), blk(8), blk_t, blk(0), blk_t, c_spec],
        out_specs=[blk(0), blk_t, blk_t, c_spec],
        out_shape=[jax.ShapeDtypeStruct((lp, FOX_WIDTH), BF16), jax.ShapeDtypeStruct((FOX_WIDTH, lp), BF16),
                   jax.ShapeDtypeStruct((FOX_WIDTH, lp), BF16), jax.ShapeDtypeStruct((4, 2, lp), F32)],
        scratch_shapes=[pltpu.VMEM((128, lp), F32), pltpu.VMEM((128, lp), F32)],
    )(qkv, qkv, qkv, q_t, dcat, do_t, c3)


def _ln_stats(x):
    mu = jnp.mean(x, axis=-1, keepdims=True)
    xc = x - mu
    var = jnp.mean(xc * xc, axis=-1, keepdims=True)
    rstd = lax.rsqrt(var + LN_EPS)
    return xc * rstd, rstd


def _conv_fwd(agf, conv_w, conv_b, ln_g, ln_b, *, name):
    lp = agf.shape[0]
    nch = lp // CHUNK
    c = CONV_CH

    def body(a_ref, g_ref, w_ref, b_ref, lg_ref, lb_ref, u0_ref, u1_ref, u3_ref, u0s):
        u0s[0:HALO, :] = jnp.zeros((HALO, c), F32)

        def glu(ci, _):
            rows = pl.ds(pl.multiple_of(ci * CHUNK, CHUNK), CHUNK)
            u0 = a_ref[rows, :] * _sigmoid(g_ref[rows, :])
            u0_ref[rows, :] = u0
            u0s[pl.ds(pl.multiple_of(ci * CHUNK + HALO, 8), CHUNK), :] = u0
            return 0

        lax.fori_loop(0, nch, glu, 0)

        def conv(ci, _):
            r0 = pl.multiple_of(ci * CHUNK, CHUNK)
            rows = pl.ds(r0, CHUNK)
            for lg in range(c // 128):
                lanes = slice(lg * 128, (lg + 1) * 128)
                win = u0s[pl.ds(r0, CHUNK + HALO), lanes]
                acc = jnp.broadcast_to(b_ref[:, lanes], (CHUNK, 128))
                for k in range(CONV_WIDTH):
                    s = CONV_WIDTH - 1 - k
                    sh = win if s == 0 else pltpu.roll(win, s, 0)
                    acc = acc + w_ref[k:k + 1, lanes] * sh[HALO:HALO + CHUNK, :]
                u1_ref[rows, lanes] = acc
            xhat, _ = _ln_stats(u1_ref[rows, :])
            y = xhat * lg_ref[...] + lb_ref[...]
            u3_ref[rows, :] = (y * _sigmoid(y)).astype(BF16)
            return 0

        lax.fori_loop(0, nch, conv, 0)

    full = lambda shape: pl.BlockSpec(shape, lambda i: (0, 0))
    return pl.pallas_call(
        body, name=name, grid=(1,),
        in_specs=[pl.BlockSpec((lp, c), lambda i: (0, 0)), pl.BlockSpec((lp, c), lambda i: (0, 1)),
                  full((CONV_WIDTH, c)), full((1, c)), full((1, c)), full((1, c))],
        out_specs=[full((lp, c)), full((lp, c)), full((lp, c))],
        out_shape=[jax.ShapeDtypeStruct((lp, c), F32), jax.ShapeDtypeStruct((lp, c), F32),
                   jax.ShapeDtypeStruct((lp, c), BF16)],
        scratch_shapes=[pltpu.VMEM((lp + HALO, c), F32)],
    )(agf, agf, conv_w, conv_b, ln_g, ln_b)


def _conv_bwd(dcat, u0, u1, agf, conv_w, ln_g, ln_b, *, name):
    lp = agf.shape[0]
    nch = lp // CHUNK
    c = CONV_CH
    wlen = CHUNK + HALO

    def body(du3_ref, u0_ref, u1_ref, a_ref, g_ref, w_ref, lg_ref, lb_ref,
             dag_ref, dw_ref, db_ref, dlg_ref, dlb_ref, du1s, u0s, dwacc, vacc):
        du1s[lp:lp + HALO, :] = jnp.zeros((HALO, c), F32)
        u0s[0:HALO, :] = jnp.zeros((HALO, c), F32)
        dwacc[...] = jnp.zeros_like(dwacc)
        vacc[...] = jnp.zeros_like(vacc)

        def ln_bwd(ci, _):
            r0 = pl.multiple_of(ci * CHUNK, CHUNK)
            rows = pl.ds(r0, CHUNK)
            xhat, rstd = _ln_stats(u1_ref[rows, :])
            y = xhat * lg_ref[...] + lb_ref[...]
            sg = _sigmoid(y)
            du2 = du3_ref[rows, :] * (sg * (1.0 + y * (1.0 - sg)))
            vacc[0:8, :] += _fold8(du2 * xhat)
            vacc[8:16, :] += _fold8(du2)
            dxhat = du2 * lg_ref[...]
            du1 = rstd * (dxhat - jnp.mean(dxhat, axis=-1, keepdims=True)
                          - xhat * jnp.mean(dxhat * xhat, axis=-1, keepdims=True))
            vacc[16:24, :] += _fold8(du1)
            du1s[rows, :] = du1
            u0s[pl.ds(pl.multiple_of(ci * CHUNK + HALO, 8), CHUNK), :] = u0_ref[rows, :]
            return 0

        lax.fori_loop(0, nch, ln_bwd, 0)

        def conv_bwd(ci, _):
            r0 = pl.multiple_of(ci * CHUNK, CHUNK)
            rows = pl.ds(r0, CHUNK)
            for lg in range(c // 128):
                lanes = slice(lg * 128, (lg + 1) * 128)
                dwin = du1s[pl.ds(r0, wlen), lanes]
                uwin = u0s[pl.ds(r0, wlen), lanes]
                du1 = dwin[0:CHUNK, :]
                acc = jnp.zeros((CHUNK, 128), F32)
                for k in range(CONV_WIDTH):
                    s = CONV_WIDTH - 1 - k
                    dsh = dwin if s == 0 else pltpu.roll(dwin, wlen - s, 0)
                    acc = acc + w_ref[k:k + 1, lanes] * dsh[0:CHUNK, :]
                    ush = uwin if s == 0 else pltpu.roll(uwin, s, 0)
                    dwacc[8 * k:8 * k + 8, lanes] += _fold8(du1 * ush[HALO:HALO + CHUNK, :])
                sg = _sigmoid(g_ref[rows, lanes])
                a = a_ref[rows, lanes]
                dag_ref[rows, lanes] = (acc * sg).astype(BF16)
                dag_ref[rows, slice(c + lg * 128, c + (lg + 1) * 128)] = (acc * a * sg * (1.0 - sg)).astype(BF16)
            return 0

        lax.fori_loop(0, nch, conv_bwd, 0)
        for k in range(CONV_WIDTH):
            dw_ref[k:k + 1, :] = jnp.sum(dwacc[8 * k:8 * k + 8, :], axis=0, keepdims=True)
        dlg_ref[...] = jnp.sum(vacc[0:8, :], axis=0, keepdims=True)
        dlb_ref[...] = jnp.sum(vacc[8:16, :], axis=0, keepdims=True)
        db_ref[...] = jnp.sum(vacc[16:24, :], axis=0, keepdims=True)

    full = lambda shape: pl.BlockSpec(shape, lambda i: (0, 0))
    vec = jax.ShapeDtypeStruct((1, c), F32)
    return pl.pallas_call(
        body, name=name, grid=(1,),
        in_specs=[pl.BlockSpec((lp, c), lambda i: (0, 1)), full((lp, c)), full((lp, c)),
                  pl.BlockSpec((lp, c), lambda i: (0, 0)), pl.BlockSpec((lp, c), lambda i: (0, 1)),
                  full((CONV_WIDTH, c)), full((1, c)), full((1, c))],
        out_specs=[full((lp, 2 * c)), full((CONV_WIDTH, c)), full((1, c)), full((1, c)), full((1, c))],
        out_shape=[jax.ShapeDtypeStruct((lp, 2 * c), BF16), jax.ShapeDtypeStruct((CONV_WIDTH, c), F32), vec, vec, vec],
        scratch_shapes=[pltpu.VMEM((lp + HALO, c), F32), pltpu.VMEM((lp + HALO, c), F32),
                        pltpu.VMEM((8 * CONV_WIDTH, c), F32), pltpu.VMEM((24, c), F32)],
    )(dcat, u0, u1, agf, agf, conv_w, ln_g, ln_b)


FFN_TILE = 256
FFN_PAD = 8


def _ffn_conv(xs, w_ref, b_ref, half, r0):
    win = xs[half, pl.ds(r0, CHUNK + FFN_PAD), :]
    acc = jnp.broadcast_to(b_ref[half], (CHUNK, FFN_TILE))
    for k in range(FFN_CONV_WIDTH):
        s = FFN_CONV_WIDTH - 1 - k
        sh = win if s == 0 else pltpu.roll(win, s, 0)
        acc = acc + w_ref[half, k:k + 1, :] * sh[FFN_PAD:FFN_PAD + CHUNK, :]
    return acc


def _ffn_act_fwd(up3, w3, b3, *, name):
    _, lp, f = up3.shape
    nch = lp // CHUNK

    def body(up_ref, w_ref, b_ref, act_ref, xs):
        for half in range(2):
            xs[half, 0:FFN_PAD, :] = jnp.zeros((FFN_PAD, FFN_TILE), F32)
            xs[half, FFN_PAD:FFN_PAD + lp, :] = up_ref[half]

        def chunk(ci, _):
            r0 = pl.multiple_of(ci * CHUNK, CHUNK)
            gate = _ffn_conv(xs, w_ref, b_ref, 0, r0)
            val = _ffn_conv(xs, w_ref, b_ref, 1, r0)
            act_ref[pl.ds(r0, CHUNK), :] = (gate * _sigmoid(gate) * val).astype(BF16)
            return 0

        lax.fori_loop(0, nch, chunk, 0)

    return pl.pallas_call(
        body, name=name, grid=(f // FFN_TILE,),
        in_specs=[pl.BlockSpec((2, lp, FFN_TILE), lambda j: (0, 0, j)),
                  pl.BlockSpec((2, FFN_CONV_WIDTH, FFN_TILE), lambda j: (0, 0, j)),
                  pl.BlockSpec((2, 1, FFN_TILE), lambda j: (0, 0, j))],
        out_specs=pl.BlockSpec((lp, FFN_TILE), lambda j: (0, j)),
        out_shape=jax.ShapeDtypeStruct((lp, f), BF16),
        scratch_shapes=[pltpu.VMEM((2, lp + FFN_PAD, FFN_TILE), F32)],
    )(up3, w3, b3)


def _ffn_act_bwd(up3, w3, b3, dact, *, name):
    _, lp, f = up3.shape
    nch = lp // CHUNK
    wlen = CHUNK + FFN_PAD

    def body(up_ref, w_ref, b_ref, dact_ref, dup_ref, dw_ref, db_ref, xs, ds, wacc):
        for half in range(2):
            xs[half, 0:FFN_PAD, :] = jnp.zeros((FFN_PAD, FFN_TILE), F32)
            xs[half, FFN_PAD:FFN_PAD + lp, :] = up_ref[half]
            ds[half, lp:lp + FFN_PAD, :] = jnp.zeros((FFN_PAD, FFN_TILE), F32)
        wacc[...] = jnp.zeros_like(wacc)

        def act_bwd(ci, _):
            r0 = pl.multiple_of(ci * CHUNK, CHUNK)
            rows = pl.ds(r0, CHUNK)
            gate = _ffn_conv(xs, w_ref, b_ref, 0, r0)
            val = _ffn_conv(xs, w_ref, b_ref, 1, r0)
            sg = _sigmoid(gate)
            da = dact_ref[rows, :]
            ds[0, rows, :] = da * val * (sg * (1.0 + gate * (1.0 - sg)))
            ds[1, rows, :] = da * (gate * sg)
            return 0

        lax.fori_loop(0, nch, act_bwd, 0)

        def conv_bwd(ci, _):
            r0 = pl.multiple_of(ci * CHUNK, CHUNK)
            rows = pl.ds(r0, CHUNK)
            for half in range(2):
                dwin = ds[half, pl.ds(r0, wlen), :]
                xwin = xs[half, pl.ds(r0, wlen), :]
                d0 = dwin[0:CHUNK, :]
                acc = jnp.zeros((CHUNK, FFN_TILE), F32)
                for k in range(FFN_CONV_WIDTH):
                    s = FFN_CONV_WIDTH - 1 - k
                    dsh = dwin if s == 0 else pltpu.roll(dwin, wlen - s, 0)
                    acc = acc + w_ref[half, k:k + 1, :] * dsh[0:CHUNK, :]
                    xsh = xwin if s == 0 else pltpu.roll(xwin, s, 0)
                    wacc[half, 8 * k:8 * k + 8, :] += _fold8(d0 * xsh[FFN_PAD:FFN_PAD + CHUNK, :])
                wacc[half, 24:32, :] += _fold8(d0)
                dup_ref[half, rows, :] = acc.astype(BF16)
            return 0

        lax.fori_loop(0, nch, conv_bwd, 0)
        for half in range(2):
            for k in range(FFN_CONV_WIDTH):
                dw_ref[half, k:k + 1, :] = jnp.sum(wacc[half, 8 * k:8 * k + 8, :], axis=0, keepdims=True)
            db_ref[half] = jnp.sum(wacc[half, 24:32, :], axis=0, keepdims=True)

    return pl.pallas_call(
        body, name=name, grid=(f // FFN_TILE,),
        in_specs=[pl.BlockSpec((2, lp, FFN_TILE), lambda j: (0, 0, j)),
                  pl.BlockSpec((2, FFN_CONV_WIDTH, FFN_TILE), lambda j: (0, 0, j)),
                  pl.BlockSpec((2, 1, FFN_TILE), lambda j: (0, 0, j)),
                  pl.BlockSpec((lp, FFN_TILE), lambda j: (0, j))],
        out_specs=[pl.BlockSpec((2, lp, FFN_TILE), lambda j: (0, 0, j)),
                   pl.BlockSpec((2, FFN_CONV_WIDTH, FFN_TILE), lambda j: (0, 0, j)),
                   pl.BlockSpec((2, 1, FFN_TILE), lambda j: (0, 0, j))],
        out_shape=[jax.ShapeDtypeStruct((2, lp, f), BF16), jax.ShapeDtypeStruct((2, FFN_CONV_WIDTH, f), F32),
                   jax.ShapeDtypeStruct((2, 1, f), F32)],
        scratch_shapes=[pltpu.VMEM((2, lp + FFN_PAD, FFN_TILE), F32), pltpu.VMEM((2, lp + FFN_PAD, FFN_TILE), F32),
                        pltpu.VMEM((2, 32, FFN_TILE), F32)],
    )(up3, w3, b3, dact)


POOL_PAD = 16


def _inv_count(r0, w):
    t = r0 + lax.broadcasted_iota(jnp.int32, (CHUNK, 1), 0)
    return 1.0 / jnp.minimum(t + 1, w).astype(F32)


def _pool_fwd(n, pool_w, pool_b, pool_scale, h, *, name):
    lp, dm = n.shape
    nch = lp // CHUNK
    g = POOL_GROUP

    def body(n_ref, w_ref, b_ref, s_ref, h_ref, ho_ref, d_ref, z_ref, xs):
        gi = pl.program_id(0)
        xs[0:POOL_PAD, :] = jnp.zeros((POOL_PAD, g), F32)
        xs[POOL_PAD:POOL_PAD + lp, :] = n_ref[...]
        for idx, w in enumerate(POOL_WINDOWS):
            @pl.when(gi == idx)
            def _(w=w):
                def chunk(ci, _):
                    r0 = pl.multiple_of(ci * CHUNK, CHUNK)
                    win = xs[pl.ds(r0, CHUNK + POOL_PAD), :]
                    acc = win
                    for j in range(1, w):
                        acc = acc + pltpu.roll(win, j, 0)
                    x = win[POOL_PAD:POOL_PAD + CHUNK, :]
                    d_ref[pl.ds(r0, CHUNK), :] = (acc[POOL_PAD:POOL_PAD + CHUNK, :] * _inv_count(r0, w) - x).astype(BF16)
                    return 0

                lax.fori_loop(0, nch, chunk, 0)

        z = jnp.dot(d_ref[...], w_ref[...], preferred_element_type=F32) + b_ref[...]
        z_ref[...] = z
        ho_ref[...] = h_ref[...] + z * s_ref[...]

    col = pl.BlockSpec((lp, g), lambda i: (0, i))
    vec = pl.BlockSpec((1, g), lambda i: (0, i))
    return pl.pallas_call(
        body, name=name, grid=(len(POOL_WINDOWS),),
        in_specs=[col, pl.BlockSpec((None, g, g), lambda i: (i, 0, 0)), vec, vec, col],
        out_specs=[col, col, col],
        out_shape=[jax.ShapeDtypeStruct((lp, dm), F32), jax.ShapeDtypeStruct((lp, dm), BF16),
                   jax.ShapeDtypeStruct((lp, dm), F32)],
        scratch_shapes=[pltpu.VMEM((lp + POOL_PAD, g), F32)],
    )(n, pool_w, pool_b, pool_scale, h)


def _pool_bwd(dy, z, pool_w, pool_scale, *, name):
    lp, dm = dy.shape
    nch = lp // CHUNK
    g = POOL_GROUP
    wlen = CHUNK + POOL_PAD

    def body(dy_ref, z_ref, w_ref, s_ref, dn_ref, dz_ref, dsc_ref, db_ref, ys, dd):
        gi = pl.program_id(0)
        dyv = dy_ref[...]
        dsc_ref[...] = jnp.sum(dyv * z_ref[...], axis=0, keepdims=True)
        dz = dyv * s_ref[...]
        db_ref[...] = jnp.sum(dz, axis=0, keepdims=True)
        dzb = dz.astype(BF16)
        dz_ref[...] = dzb
        dd[...] = lax.dot_general(dzb, w_ref[...], (((1,), (1,)), ((), ())), preferred_element_type=F32)
        ys[lp:lp + POOL_PAD, :] = jnp.zeros((POOL_PAD, g), F32)
        for idx, w in enumerate(POOL_WINDOWS):
            @pl.when(gi == idx)
            def _(w=w):
                def scale(ci, _):
                    r0 = pl.multiple_of(ci * CHUNK, CHUNK)
                    ys[pl.ds(r0, CHUNK), :] = dd[pl.ds(r0, CHUNK), :] * _inv_count(r0, w)
                    return 0

                lax.fori_loop(0, nch, scale, 0)

                def chunk(ci, _):
                    r0 = pl.multiple_of(ci * CHUNK, CHUNK)
                    win = ys[pl.ds(r0, wlen), :]
                    acc = win
                    for j in range(1, w):
                        acc = acc + pltpu.roll(win, wlen - j, 0)
                    dn_ref[pl.ds(r0, CHUNK), :] = acc[0:CHUNK, :] - dd[pl.ds(r0, CHUNK), :]
                    return 0

                lax.fori_loop(0, nch, chunk, 0)

    col = pl.BlockSpec((lp, g), lambda i: (0, i))
    vec = pl.BlockSpec((1, g), lambda i: (0, i))
    return pl.pallas_call(
        body, name=name, grid=(len(POOL_WINDOWS),),
        in_specs=[col, col, pl.BlockSpec((None, g, g), lambda i: (i, 0, 0)), vec],
        out_specs=[col, col, vec, vec],
        out_shape=[jax.ShapeDtypeStruct((lp, dm), F32), jax.ShapeDtypeStruct((lp, dm), BF16),
                   jax.ShapeDtypeStruct((1, dm), F32), jax.ShapeDtypeStruct((1, dm), F32)],
        scratch_shapes=[pltpu.VMEM((lp + POOL_PAD, g), F32), pltpu.VMEM((lp, g), F32)],
    )(dy, z, pool_w, pool_scale)


def _ffn_fwd(h, g, w_up, w3, b3, w_down, tag):
    lp = h.shape[0]
    nj = D_FF // FFN_TILE
    n = _rms_fwd(h, g, name=f"rms_ffn{tag}", out_dtype=BF16)
    up2 = _mm(n, w_up, name=f"mm_up{tag}", tn=FFN_TILE, dims=(lp, 2 * D_FF, D_MODEL), b_layer=tag,
              o_map=lambda i, j, k: (j // nj, j % nj), out_shape=(2 * lp, D_FF))
    up3 = up2.reshape(2, lp, D_FF)
    act = _ffn_act_fwd(up3, w3, b3, name=f"ffn_act{tag}")
    h_out = _mm(act, w_down, name=f"mm_down{tag}", tn=256, res=h, b_layer=tag)
    return h_out, (n, up3, act)


def _ffn_bwd(dh, dhb, h, g, saved, w_up, w3, b3, w_down, tag):
    lp = h.shape[0]
    nj = D_FF // FFN_TILE
    n, up3, act = saved
    act_t = _transpose(act, name=f"t_act{tag}", out_dtype=BF16)
    dw_down = _mm(act_t, dhb, name=f"mm_dwdown{tag}", tm=704, out_dtype=BF16)
    dact = _mm(dhb, w_down, name=f"mm_dact{tag}", tb=True, tn=256, b_layer=tag)
    dup3, dcw, dcb = _ffn_act_bwd(up3, w3, b3, dact, name=f"ffn_act_bwd{tag}")
    dup2 = dup3.reshape(2 * lp, D_FF)
    n_t = _transpose(n, name=f"t_nffn{tag}", out_dtype=BF16)
    dw_up = _mm_dw_up(n_t, dup2, name=f"mm_dwup{tag}")
    dn = _mm(dup2, w_up, name=f"mm_dnffn{tag}", tb=True, tk=FFN_TILE, dims=(lp, D_MODEL, 2 * D_FF), b_layer=tag,
             a_map=lambda i, j, k: (k // nj, k % nj), b_map=lambda i, j, k: (0, k))
    dh_in, dh_in_b, dg = _rms_bwd(h, g, dn, dh, name=f"rms_bwd_ffn{tag}")
    return dh_in, dh_in_b, (dg, dw_up, dcw, dcb, dw_down)


def _local_step(x, tgt, wt):
    seq = x.shape[0]
    n_real = N_META + seq
    lp = _round_up(n_real, CHUNK)
    pad = jnp.zeros((lp - n_real, D_MODEL), F32)
    h0 = jnp.concatenate([wt["meta"], x, pad], axis=0)
    tgt_p = jnp.concatenate([jnp.zeros((N_META, D_MODEL), F32), tgt, pad], axis=0)
    w_in_p = wt["w_in_p"]

    n0 = _rms_fwd(h0, wt["g_even"], name="rms_even", out_dtype=BF16)
    qkv = _mm(n0, w_in_p, name="mm_qkv", tn=512, dims=(lp, 3 * FOX_WIDTH, D_MODEL), out_dtype=BF16)
    ag = _mm(n0, w_in_p, name="mm_ag", tn=512, dims=(lp, 2 * CONV_CH, D_MODEL),
             b_map=lambda i, j, k: (0, 3 + j))
    f_t = _mm(wt["wf_t"], n0, name="mm_ft", tb=True)
    c_row = _fox_prep(f_t, wt["b_f"], name="fox_prep")
    c3 = c_row.reshape(4, 2, lp)
    o = _attn_fwd(qkv, c3, name="attn_fwd")
    u0, u1, u3 = _conv_fwd(ag, wt["conv_w"], wt["conv_b"], wt["ln_g"], wt["ln_b"], name="conv_fwd")
    cat = jnp.concatenate([o, u3], axis=1)
    h1 = _mm(cat, wt["w_out"], name="mm_out", tn=256, res=h0)
    h2, saved0 = _ffn_fwd(h1, wt["ffn_norm"][0:1], wt["w_up"], wt["fcw3"][0], wt["fcb3"][0], wt["w_down"], 0)

    n2 = _rms_fwd(h2, wt["g_odd"], name="rms_odd", out_dtype=F32)
    h3, dpool, z = _pool_fwd(n2, wt["pool_w"], wt["pool_b"], wt["pool_scale"], h2, name="pool_fwd")
    h4, saved1 = _ffn_fwd(h3, wt["ffn_norm"][1:2], wt["w_up"], wt["fcw3"][1], wt["fcb3"][1], wt["w_down"], 1)

    loss, dh4, dh4b, d_gfinal = _loss_head(h4, wt["g_final"], tgt_p, n_real, name="loss_head")

    dh3, dh3b, gf1 = _ffn_bwd(dh4, dh4b, h3, wt["ffn_norm"][1:2], saved1, wt["w_up"], wt["fcw3"][1],
                              wt["fcb3"][1], wt["w_down"], 1)
    send1, token1 = _send_ffn_grads(gf1[1], gf1[4], 1)
    dn2, dzb, d_pscale, d_pb = _pool_bwd(dh3, z, wt["pool_w"], wt["pool_scale"], name="pool_bwd")
    dpool_t = _transpose(dpool, name="t_dpool", out_dtype=BF16, after=token1)
    d_pw = _mm(dpool_t, dzb, name="mm_dpoolw", tm=POOL_GROUP, tn=POOL_GROUP, dims=(D_MODEL, POOL_GROUP, lp),
               b_map=lambda i, j, k: (0, i), o_map=lambda i, j, k: (i, 0), out_shape=(D_MODEL, POOL_GROUP),
               out_dtype=BF16)
    dh2, dh2b, d_godd = _rms_bwd(h2, wt["g_odd"], dn2, dh3, name="rms_bwd_odd")
    dh1, dh1b, gf0 = _ffn_bwd(dh2, dh2b, h1, wt["ffn_norm"][0:1], saved0, wt["w_up"], wt["fcw3"][0],
                              wt["fcb3"][0], wt["w_down"], 0)

    send0, token0 = _send_ffn_grads(gf0[1], gf0[4], 0)
    cat_t = _transpose(cat, name="t_cat", out_dtype=BF16, after=token0)
    d_wout = _mm(cat_t, dh1b, name="mm_dwout", tm=512, out_dtype=BF16)
    dcat = _mm(dh1b, wt["w_out"], name="mm_dcat", tb=True, tn=256)
    q_t = _transpose(qkv, name="t_q", out_dtype=BF16, cols=FOX_WIDTH)
    do_t = _transpose(dcat, name="t_do", out_dtype=BF16, cols=FOX_WIDTH)
    dq, dk_t, dv_t, dc3 = _attn_bwd(qkv, q_t, dcat, do_t, c3, name="attn_bwd")
    dk = _transpose(dk_t, name="t_dk", out_dtype=BF16)
    dv = _transpose(dv_t, name="t_dv", out_dtype=BF16)
    df_t, d_bf = _fox_bwd(dc3.reshape(FOX_HEADS, lp), f_t, wt["b_f"], name="fox_bwd")
    df = _transpose(df_t, name="t_df", out_dtype=BF16)
    dag, d_convw, d_convb, d_lng, d_lnb = _conv_bwd(dcat, u0, u1, ag, wt["conv_w"], wt["ln_g"], wt["ln_b"],
                                                    name="conv_bwd")
    dproj = jnp.concatenate([dq, dk, dv, dag, df], axis=1)
    n0_t = _transpose(n0, name="t_n0", out_dtype=BF16)
    d_win = _mm_dw_in(n0_t, dproj, name="mm_dwin")
    dn0 = _mm(dproj, w_in_p, name="mm_dn0", tb=True, tk=384)
    dh0, _, d_geven = _rms_bwd(h0, wt["g_even"], dn0, dh1, name="rms_bwd_even")

    two = DEPTH
    layers = lambda i: jnp.stack([gf0[i], gf1[i]])
    grads = dict(
        meta_tokens=dh0[0:N_META], mix_norm_even=d_geven, w_in=d_win, b_f=d_bf.reshape(1, FOX_HEADS),
        conv_w=d_convw[None], conv_b=d_convb, ln_g=d_lng, ln_b=d_lnb, w_out=d_wout, mix_norm_odd=d_godd,
        pool_w=d_pw.reshape(len(POOL_WINDOWS), POOL_GROUP, POOL_GROUP),
        pool_b=d_pb.reshape(1, len(POOL_WINDOWS), POOL_GROUP), pool_scale=d_pscale,
        ffn_norm=jnp.concatenate([gf0[0], gf1[0]], axis=0), w_up=(gf0[1], gf1[1]),
        ffn_conv_w=layers(2).transpose(0, 2, 1, 3).reshape(two, FFN_CONV_WIDTH, 2 * D_FF),
        ffn_conv_b=layers(3).reshape(two, 2 * D_FF), w_down=(gf0[4], gf1[4]), final_norm=d_gfinal.reshape(D_MODEL),
        ffn_sends=(send0, send1))
    return loss, dh0[N_META:n_real], grads


def _send_ffn_grads(dw_up, dw_down, tag):
    rows = D_FF // N_DEV
    lands = [jax.ShapeDtypeStruct((N_DEV - 1,) + dw_up.shape[1:], BF16),
             jax.ShapeDtypeStruct((N_DEV - 1, rows, D_MODEL), BF16)]
    return _send_start([dw_up, dw_down], lands, [(0, _by_owner, 0), (1, _row_block(rows), 1)], name=f"send_ffn{tag}")


_QKV = 3 * FOX_WIDTH
_GLU0 = _QKV + FOX_HEADS
_IN_COLS = _GLU0 + 2 * CONV_CH
_F_PAD = 128


_IN_SHARD = _IN_COLS // N_DEV
_UP_SHARD = 2 * D_FF // N_DEV
_ROW_TILE = 256


def _assemble_w_in(st, *, name):
    tr = _ROW_TILE

    def body(s_ref, o_ref):
        full = jnp.concatenate([s_ref[i].astype(F32) for i in range(N_DEV)], axis=1)
        parts = [full[:, :_QKV], full[:, _GLU0:], full[:, _QKV:_GLU0], jnp.zeros((tr, _F_PAD - FOX_HEADS), F32)]
        o_ref[...] = jnp.concatenate(parts, axis=1).astype(BF16)

    return pl.pallas_call(
        body, name=name, grid=(D_MODEL // tr,),
        in_specs=[pl.BlockSpec((N_DEV, tr, _IN_SHARD), lambda i: (0, i, 0))],
        out_specs=pl.BlockSpec((tr, _QKV + 2 * CONV_CH + _F_PAD), lambda i: (i, 0)),
        out_shape=jax.ShapeDtypeStruct((D_MODEL, _QKV + 2 * CONV_CH + _F_PAD), BF16),
    )(st)


def _assemble_w_up(st, *, name):
    tr = _ROW_TILE

    def body(s_ref, o_ref):
        o_ref[...] = jnp.concatenate([s_ref[0].astype(F32), s_ref[1].astype(F32)], axis=1).astype(BF16)

    return pl.pallas_call(
        body, name=name, grid=(DEPTH, D_MODEL // tr, N_DEV // 2),
        in_specs=[pl.BlockSpec((2, None, tr, _UP_SHARD), lambda l, r, p: (p, l, r, 0))],
        out_specs=pl.BlockSpec((None, tr, 2 * _UP_SHARD), lambda l, r, p: (l, r, p)),
        out_shape=jax.ShapeDtypeStruct((DEPTH, D_MODEL, 2 * D_FF), BF16),
    )(st)


def _mm_dw_in(n_t, dproj, *, name):
    dm, lp = n_t.shape
    tr = _ROW_TILE
    ag0 = _QKV + 2 * CONV_CH

    def body(a_ref, b_ref, o_ref):
        r = jnp.dot(a_ref[...], b_ref[...], preferred_element_type=F32)
        full = jnp.concatenate([r[:, :_QKV], r[:, ag0:ag0 + FOX_HEADS], r[:, _QKV:ag0]], axis=1)
        for i in range(N_DEV):
            o_ref[i] = full[:, i * _IN_SHARD:(i + 1) * _IN_SHARD].astype(BF16)

    return pl.pallas_call(
        body, name=name, grid=(dm // tr,),
        in_specs=[pl.BlockSpec((tr, lp), lambda i: (i, 0)), pl.BlockSpec(dproj.shape, lambda i: (0, 0))],
        out_specs=pl.BlockSpec((N_DEV, tr, _IN_SHARD), lambda i: (0, i, 0)),
        out_shape=jax.ShapeDtypeStruct((N_DEV, dm, _IN_SHARD), BF16),
    )(n_t, dproj)


def _mm_dw_up(n_t, dup2, *, name):
    dm, lp = n_t.shape
    pairs_per_half = D_FF // (2 * _UP_SHARD)

    def body(a_ref, b_ref, o_ref):
        r = jnp.dot(a_ref[...], b_ref[...], preferred_element_type=F32)
        o_ref[0] = r[:, :_UP_SHARD].astype(BF16)
        o_ref[1] = r[:, _UP_SHARD:].astype(BF16)

    return pl.pallas_call(
        body, name=name, grid=(N_DEV // 2,),
        in_specs=[pl.BlockSpec((dm, lp), lambda p: (0, 0)),
                  pl.BlockSpec((lp, 2 * _UP_SHARD), lambda p: (p // pairs_per_half, p % pairs_per_half))],
        out_specs=pl.BlockSpec((2, dm, _UP_SHARD), lambda p: (p, 0, 0)),
        out_shape=jax.ShapeDtypeStruct((N_DEV, dm, _UP_SHARD), BF16),
    )(n_t, dup2)


MESH = pl.DeviceIdType.MESH
ANY = pl.BlockSpec(memory_space=pl.ANY)


def _slot(px, py, pc):
    return 4 * px + 2 * py + pc


def _by_owner(ref, slot):
    return ref.at[slot]


def _row_block(rows, axis=0):
    def place(ref, slot):
        idx = (slice(None),) * axis + (pl.ds(slot * rows, rows),)
        return ref.at[idx]
    return place


def _all_gather(arrs, out_shapes, places, *, name):
    n = len(arrs)

    def body(*refs):
        ins, outs = refs[:n], refs[n:2 * n]
        send_sems, recv_sems, local_sems = refs[2 * n:]
        x, y, c = lax.axis_index("x"), lax.axis_index("y"), lax.axis_index("c")
        me, sibling = (x, y, c), (x, y, 1 - c)
        chips = [(1 - x, y), (x, 1 - y), (1 - x, 1 - y)]

        def copy(a, k, block, to, from_input=False):
            dst = places[a](outs[a], _slot(*block))
            return pltpu.make_async_remote_copy(
                src_ref=ins[a] if from_input else dst, dst_ref=dst,
                send_sem=send_sems.at[7 * a + k], recv_sem=recv_sems.at[7 * a + k],
                device_id=to, device_id_type=MESH)

        own, sent = [], []
        for a in range(n):
            mine = pltpu.make_async_copy(ins[a], places[a](outs[a], _slot(*me)), local_sems.at[a])
            mine.start()
            own.append(mine)
            first = [copy(a, 0, me, sibling, True)]
            first += [copy(a, 1 + j, me, (*chip, c), True) for j, chip in enumerate(chips)]
            for cp in first:
                cp.start()
            sent += first
        for a in range(n):
            for j, chip in enumerate(chips):
                copy(a, 1 + j, (*chip, c), me).wait_recv()
                passed = copy(a, 4 + j, (*chip, c), sibling)
                passed.start()
                sent.append(passed)
        for a in range(n):
            copy(a, 0, sibling, me).wait_recv()
            for j, chip in enumerate(chips):
                copy(a, 4 + j, (*chip, 1 - c), me).wait_recv()
        for cp in sent:
            cp.wait_send()
        for cp in own:
            cp.wait()

    return pl.pallas_call(
        body, name=name,
        in_specs=[ANY] * n, out_specs=[ANY] * n,
        out_shape=[jax.ShapeDtypeStruct(s, a.dtype) for s, a in zip(out_shapes, arrs)],
        scratch_shapes=[pltpu.SemaphoreType.DMA((7 * n,)), pltpu.SemaphoreType.DMA((7 * n,)),
                        pltpu.SemaphoreType.DMA((n,))],
    )(*arrs)


def _exchange(srcs, recv_structs, copies, *, name):
    ns, nc = len(srcs), len(copies)

    def body(*refs):
        src_refs, recv_refs = refs[:ns], refs[ns:ns + len(recv_structs)]
        send_sems, recv_sems, local_sems = refs[ns + len(recv_structs):]
        x, y, c = lax.axis_index("x"), lax.axis_index("y"), lax.axis_index("c")
        me = _slot(x, y, c)

        def peer(k):
            flip = lambda v, bit: 1 - v if bit else v
            return flip(x, k & 4), flip(y, k & 2), flip(c, k & 1)

        def copy(k, j, arriving):
            si, take, ri, put = copies[j]
            p = _slot(*peer(k))
            sem = (k - 1) * nc + j
            return pltpu.make_async_remote_copy(
                src_ref=take(src_refs[si], p), dst_ref=put(recv_refs[ri], p if arriving else me),
                send_sem=send_sems.at[sem], recv_sem=recv_sems.at[sem], device_id=peer(k), device_id_type=MESH)

        own = [pltpu.make_async_copy(take(src_refs[si], me), put(recv_refs[ri], me), local_sems.at[j])
               for j, (si, take, ri, put) in enumerate(copies)]
        for cp in own:
            cp.start()
        sent = [copy(k, j, False) for k in range(1, N_DEV) for j in range(nc)]
        for cp in sent:
            cp.start()
        for k in range(1, N_DEV):
            for j in range(nc):
                copy(k, j, True).wait_recv()
        for cp in sent:
            cp.wait_send()
        for cp in own:
            cp.wait()

    return pl.pallas_call(
        body, name=name,
        in_specs=[ANY] * ns, out_specs=[ANY] * len(recv_structs), out_shape=list(recv_structs),
        scratch_shapes=[pltpu.SemaphoreType.DMA((7 * nc,)), pltpu.SemaphoreType.DMA((7 * nc,)),
                        pltpu.SemaphoreType.DMA((nc,))],
    )(*srcs)


HBM = pl.BlockSpec(memory_space=pltpu.HBM)
SEM = pl.BlockSpec(memory_space=pltpu.SEMAPHORE)
EFFECT = pltpu.SideEffectType.DATAFLOW_SIDE_EFFECTING


def _relation_copies(src_refs, land_refs, copies, send_sems, recv_sems):
    x, y, c = lax.axis_index("x"), lax.axis_index("y"), lax.axis_index("c")
    flip = lambda v, bit: 1 - v if bit else v
    out = []
    for k in range(1, N_DEV):
        p = (flip(x, k & 4), flip(y, k & 2), flip(c, k & 1))
        for j, (si, take, li) in enumerate(copies):
            sem = (k - 1) * len(copies) + j
            out.append(pltpu.make_async_remote_copy(
                src_ref=take(src_refs[si], _slot(*p)), dst_ref=land_refs[li].at[k - 1],
                send_sem=send_sems.at[sem], recv_sem=recv_sems.at[sem], device_id=p, device_id_type=MESH))
    return out


def _send_start(srcs, land_structs, copies, *, name):
    ns, nl = len(srcs), len(land_structs)
    n_sem = (N_DEV - 1) * len(copies)

    def body(*refs):
        send_sems, recv_sems, token = refs[ns + nl], refs[ns + nl + 1], refs[-1]
        for cp in _relation_copies(refs[:ns], refs[ns:ns + nl], copies, send_sems, recv_sems):
            cp.start()
        token[...] = jnp.zeros_like(token)

    in_hbm = lambda a: pltpu.with_memory_space_constraint(a, pltpu.HBM)
    outs = pl.pallas_call(
        body, name=name,
        out_shape=(pltpu.SemaphoreType.DMA((n_sem,)), pltpu.SemaphoreType.DMA((n_sem,)),
                   *[pltpu.HBM(s.shape, s.dtype) for s in srcs],
                   *[pltpu.HBM(s.shape, s.dtype) for s in land_structs],
                   jax.ShapeDtypeStruct((8, 128), F32)),
        in_specs=(HBM,) * (ns + nl),
        out_specs=(SEM, SEM) + (HBM,) * (ns + nl) + (pl.BlockSpec(memory_space=pltpu.VMEM),),
        input_output_aliases={i: 2 + i for i in range(ns + nl)},
        compiler_params=pltpu.CompilerParams(has_side_effects=EFFECT),
    )(*[in_hbm(s) for s in srcs], *[in_hbm(lax.empty(s.shape, s.dtype)) for s in land_structs])
    return (outs[0], outs[1], outs[2:2 + ns], outs[2 + ns:2 + ns + nl], copies), outs[-1]


def _send_wait(handle, after, *, name):
    send_sems, recv_sems, srcs, lands, copies = handle
    ns, nl = len(srcs), len(lands)

    def body(*refs):
        for cp in _relation_copies(refs[:ns], refs[ns:ns + nl], copies, refs[ns + nl], refs[ns + nl + 1]):
            cp.wait_send()
            cp.wait_recv()

    outs = pl.pallas_call(
        body, name=name,
        out_shape=tuple(pltpu.HBM(a.shape, a.dtype) for a in (*srcs, *lands)),
        in_specs=(HBM,) * (ns + nl) + (SEM, SEM, ANY), out_specs=(HBM,) * (ns + nl),
        input_output_aliases={i: i for i in range(ns + nl)},
        compiler_params=pltpu.CompilerParams(has_side_effects=EFFECT),
    )(*srcs, *lands, send_sems, recv_sems, after)
    return outs[ns:]


def _sum_slabs(stack, *, name):
    _, rows, w = stack.shape
    tr = CHUNK if rows % CHUNK == 0 else rows

    def body(s_ref, o_ref):
        acc = s_ref[0].astype(F32)
        for i in range(1, N_DEV):
            acc = acc + s_ref[i].astype(F32)
        o_ref[...] = acc

    return pl.pallas_call(
        body, name=name, grid=(rows // tr,),
        in_specs=[pl.BlockSpec((N_DEV, tr, w), lambda i: (0, i, 0))],
        out_specs=pl.BlockSpec((tr, w), lambda i: (i, 0)),
        out_shape=jax.ShapeDtypeStruct((rows, w), F32),
    )(stack)


def _adam_math(w, g, m, v):
    mn = ADAM_B1 * m + (1.0 - ADAM_B1) * g
    vn = ADAM_B2 * v + (1.0 - ADAM_B2) * (g * g)
    m_hat = mn / (1.0 - ADAM_B1 ** ADAM_STEP)
    v_hat = vn / (1.0 - ADAM_B2 ** ADAM_STEP)
    return -ADAM_LR * (m_hat / (jnp.sqrt(v_hat) + ADAM_EPS) + ADAM_WD * w), mn, vn


def _adamw(w, g, m, v, *, name):
    def body(w_ref, g_ref, m_ref, v_ref, d_ref, mo_ref, vo_ref):
        d_ref[...], mo_ref[...], vo_ref[...] = _adam_math(w_ref[...], g_ref[...], m_ref[...], v_ref[...])

    return pl.pallas_call(body, name=name, out_shape=[jax.ShapeDtypeStruct(w.shape, F32)] * 3)(w, g, m, v)


def _adamw_reduce(w, stack, m, v, tr, *, name):
    rows, cols = w.shape
    assert rows % tr == 0

    def body(w_ref, s_ref, m_ref, v_ref, g_ref, d_ref, mo_ref, vo_ref):
        g = s_ref[0].astype(F32)
        for i in range(1, N_DEV):
            g = g + s_ref[i].astype(F32)
        g_ref[...] = g
        d_ref[...], mo_ref[...], vo_ref[...] = _adam_math(w_ref[...], g, m_ref[...], v_ref[...])

    blk = pl.BlockSpec((tr, cols), lambda i: (i, 0))
    stk = pl.BlockSpec((N_DEV, tr, cols), lambda i: (0, i, 0))
    return pl.pallas_call(
        body, name=name, grid=(rows // tr,),
        in_specs=[blk, stk, blk, blk], out_specs=[blk] * 4,
        out_shape=[jax.ShapeDtypeStruct((rows, cols), F32)] * 4,
    )(w, stack, m, v)


def _adamw_layers(w, owns, lands, m, v, tr, *, name):
    nl, rows, cols = w.shape
    steps = rows // tr
    assert rows % tr == 0

    def body(*refs):
        w_ref, m_ref, v_ref = refs[:3]
        own_refs, land_refs = refs[3:3 + nl], refs[3 + nl:3 + 2 * nl]
        g_ref, d_ref, mo_ref, vo_ref = refs[3 + 2 * nl:]
        for li in range(nl):
            @pl.when(pl.program_id(0) == li)
            def _(li=li):
                g = own_refs[li][...].astype(F32)
                for k in range(N_DEV - 1):
                    g = g + land_refs[li][k].astype(F32)
                g_ref[...] = g
                d_ref[...], mo_ref[...], vo_ref[...] = _adam_math(w_ref[...], g, m_ref[...], v_ref[...])

    def held(li):
        return lambda l, i: jnp.where(l == li, i, jnp.where(l < li, 0, steps - 1))

    blk = pl.BlockSpec((None, tr, cols), lambda l, i: (l, i, 0))
    own_specs = [pl.BlockSpec((tr, cols), lambda l, i, f=held(li): (f(l, i), 0)) for li in range(nl)]
    land_specs = [pl.BlockSpec((N_DEV - 1, tr, cols), lambda l, i, f=held(li): (0, f(l, i), 0)) for li in range(nl)]
    return pl.pallas_call(
        body, name=name, grid=(nl, steps),
        in_specs=[blk, blk, blk] + own_specs + land_specs, out_specs=[blk] * 4,
        out_shape=[jax.ShapeDtypeStruct(w.shape, F32)] * 4,
    )(w, m, v, *owns, *lands)


_WEIGHTS = (
    ("meta_tokens", (16, 1024), 1), ("mix_norm_even", (1, 1024), None), ("w_in", (1, 1024, 2568), 2),
    ("b_f", (1, 8), None), ("conv_w", (1, 31, 512), 2), ("conv_b", (1, 512), None), ("ln_g", (1, 512), None),
    ("ln_b", (1, 512), None), ("w_out", (1, 1024, 1024), 1), ("mix_norm_odd", (1, 1024), 1),
    ("pool_w", (1, 4, 256, 256), 2), ("pool_b", (1, 4, 256), 2), ("pool_scale", (1, 1024), 1),
    ("ffn_norm", (2, 1024), None), ("w_up", (2, 1024, 5632), 2), ("ffn_conv_w", (2, 3, 5632), 2),
    ("ffn_conv_b", (2, 5632), None), ("w_down", (2, 2816, 1024), 1), ("final_norm", (1024,), None),
)
_MATMUL_WEIGHTS = ("w_in", "w_out", "pool_w", "w_up", "w_down")
_ADAM_ROWS = dict(w_in=256, w_out=128, pool_w=128, w_up=256, w_down=352)


def _shard_shape(shape, axis):
    return shape[:axis] + (shape[axis] // N_DEV,) + shape[axis + 1:]


def _size(shape):
    n = 1
    for s in shape:
        n *= s
    return n


def _pack(parts, dtype, lead=(), align=16):
    flat = jnp.concatenate([p.reshape(lead + (-1,)).astype(dtype) for p in parts], axis=-1)
    n = flat.shape[-1]
    rows = _round_up(-(-n // FLAT_W), align)
    flat = jnp.pad(flat, [(0, 0)] * len(lead) + [(0, rows * FLAT_W - n)])
    return flat.reshape(lead + (rows, FLAT_W))


def _unpack(buf, shapes, lead=()):
    flat = buf.reshape(lead + (-1,))
    out, off = [], 0
    for shp in shapes:
        n = _size(shp)
        out.append(flat[..., off:off + n].reshape(lead + shp))
        off += n
    return out


def _gathered_to_full(stack, shape, axis):
    return jnp.moveaxis(stack, 0, axis).reshape(shape)


def _full_to_slabs(full, shape, axis):
    split = shape[:axis] + (N_DEV, shape[axis] // N_DEV) + shape[axis + 1:]
    return jnp.moveaxis(full.reshape(split), axis, 0)


def kernel(x, meta_tokens, mix_norm_even, w_in, b_f, conv_w, conv_b, ln_g, ln_b, w_out, mix_norm_odd, pool_w, pool_b, pool_scale, ffn_norm, w_up, ffn_conv_w, ffn_conv_b, w_down, final_norm, loss_target, m_meta_tokens, m_mix_norm_even, m_w_in, m_b_f, m_conv_w, m_conv_b, m_ln_g, m_ln_b, m_w_out, m_mix_norm_odd, m_pool_w, m_pool_b, m_pool_scale, m_ffn_norm, m_w_up, m_ffn_conv_w, m_ffn_conv_b, m_w_down, m_final_norm, v_meta_tokens, v_mix_norm_even, v_w_in, v_b_f, v_conv_w, v_conv_b, v_ln_g, v_ln_b, v_w_out, v_mix_norm_odd, v_pool_w, v_pool_b, v_pool_scale, v_ffn_norm, v_w_up, v_ffn_conv_w, v_ffn_conv_b, v_w_down, v_final_norm):
    names = [n for n, _, _ in _WEIGHTS]
    w_loc = dict(zip(names, (meta_tokens, mix_norm_even, w_in, b_f, conv_w, conv_b, ln_g, ln_b, w_out, mix_norm_odd,
                             pool_w, pool_b, pool_scale, ffn_norm, w_up, ffn_conv_w, ffn_conv_b, w_down, final_norm)))
    m_loc = dict(zip(names, (m_meta_tokens, m_mix_norm_even, m_w_in, m_b_f, m_conv_w, m_conv_b, m_ln_g, m_ln_b,
                             m_w_out, m_mix_norm_odd, m_pool_w, m_pool_b, m_pool_scale, m_ffn_norm, m_w_up,
                             m_ffn_conv_w, m_ffn_conv_b, m_w_down, m_final_norm)))
    v_loc = dict(zip(names, (v_meta_tokens, v_mix_norm_even, v_w_in, v_b_f, v_conv_w, v_conv_b, v_ln_g, v_ln_b,
                             v_w_out, v_mix_norm_odd, v_pool_w, v_pool_b, v_pool_scale, v_ffn_norm, v_w_up,
                             v_ffn_conv_w, v_ffn_conv_b, v_w_down, v_final_norm)))
    replicated = [(n, s) for n, s, a in _WEIGHTS if a is None]
    little = [(n, s, a) for n, s, a in _WEIGHTS if a is not None and n not in _MATMUL_WEIGHTS]
    little_shards = [_shard_shape(s, a) for _, s, a in little]
    out_rows, down_rows, pool_rows = D_MODEL // N_DEV, D_FF // N_DEV, POOL_GROUP // N_DEV
    n_groups = len(POOL_WINDOWS)

    little_pack = _pack([w_loc[n] for n, _, _ in little], F32)
    g_win, g_wout, g_poolw, g_wup, g_wdown, g_little = _all_gather(
        [w_in[0].astype(BF16), w_out[0].astype(BF16), pool_w[0].astype(BF16), w_up.astype(BF16),
         w_down.astype(BF16), little_pack],
        [(N_DEV, D_MODEL, _IN_SHARD), (D_MODEL, D_MODEL), (n_groups, POOL_GROUP, POOL_GROUP),
         (N_DEV, DEPTH, D_MODEL, _UP_SHARD), (DEPTH, D_FF, D_MODEL), (N_DEV,) + little_pack.shape],
        [_by_owner, _row_block(out_rows), _row_block(pool_rows, axis=1), _by_owner, _row_block(down_rows, axis=1),
         _by_owner],
        name="gather_weights")
    w_in_p = _assemble_w_in(g_win, name="assemble_w_in")
    full = {n: _gathered_to_full(st, s, a)
            for (n, s, a), st in zip(little, _unpack(g_little, little_shards, lead=(N_DEV,)))}
    f0 = _QKV + 2 * CONV_CH
    wt = dict(
        meta=full["meta_tokens"], g_even=mix_norm_even, w_in_p=w_in_p, wf_t=w_in_p[:, f0:f0 + FOX_HEADS].T,
        b_f=b_f.reshape(FOX_HEADS, 1), conv_w=full["conv_w"][0], conv_b=conv_b, ln_g=ln_g, ln_b=ln_b, w_out=g_wout,
        g_odd=full["mix_norm_odd"], pool_w=g_poolw, pool_b=full["pool_b"].reshape(1, D_MODEL),
        pool_scale=full["pool_scale"], ffn_norm=ffn_norm, w_up=_assemble_w_up(g_wup, name="assemble_w_up"),
        fcw3=full["ffn_conv_w"].reshape(DEPTH, FFN_CONV_WIDTH, 2, D_FF).transpose(0, 2, 1, 3),
        fcb3=ffn_conv_b.reshape(DEPTH, 2, 1, D_FF), w_down=g_wdown, g_final=final_norm.reshape(1, D_MODEL))

    loss_part, grad_x, g = _local_step(x[0], loss_target[0], wt)

    little_slabs = _pack([_full_to_slabs(g[n], s, a) for n, s, a in little], F32, lead=(N_DEV,))
    small = _pack([loss_part[:, 0:1]] + [g[n] for n, _ in replicated], F32, align=8)
    stack_of = lambda shape, dtype: jax.ShapeDtypeStruct((N_DEV,) + shape, dtype)
    recv_in, recv_out, recv_pool, recv_little, everyone = _exchange(
        [g["w_in"], g["w_out"], g["pool_w"], little_slabs, small],
        [stack_of((D_MODEL, _IN_SHARD), BF16), stack_of((out_rows, D_MODEL), BF16),
         stack_of((n_groups, pool_rows, POOL_GROUP), BF16), stack_of(little_slabs.shape[1:], F32),
         stack_of(small.shape, F32)],
        [(0, _by_owner, 0, _by_owner), (1, _row_block(out_rows), 1, _by_owner),
         (2, _row_block(pool_rows, axis=1), 2, _by_owner), (3, _by_owner, 3, _by_owner),
         (4, lambda ref, slot: ref, 4, _by_owner)],
        name="exchange_grads")
    ffn_lands = [_send_wait(send, recv_in, name=f"wait_ffn{l}") for l, send in enumerate(g["ffn_sends"])]

    grads, delta, new_m, new_v = {}, {}, {}, {}
    two_d = lambda shp: (_size(shp[:-1]), shp[-1])
    for n, stack in (("w_in", recv_in), ("w_out", recv_out), ("pool_w", recv_pool)):
        shp = w_loc[n].shape
        outs = _adamw_reduce(w_loc[n].reshape(two_d(shp)), stack.reshape((N_DEV,) + two_d(shp)),
                             m_loc[n].reshape(two_d(shp)), v_loc[n].reshape(two_d(shp)), _ADAM_ROWS[n],
                             name=f"adamw_{n}")
        grads[n], delta[n], new_m[n], new_v[n] = (o.reshape(shp) for o in outs)
    me = _slot(lax.axis_index("x"), lax.axis_index("y"), lax.axis_index("c"))
    own_up = [lax.dynamic_index_in_dim(d, me, 0, keepdims=False) for d in g["w_up"]]
    own_down = [lax.dynamic_slice_in_dim(d, me * down_rows, down_rows, 0) for d in g["w_down"]]
    for n, owns, idx in (("w_up", own_up, 0), ("w_down", own_down, 1)):
        grads[n], delta[n], new_m[n], new_v[n] = _adamw_layers(
            w_loc[n], owns, [ffn_lands[l][idx] for l in range(DEPTH)], m_loc[n], v_loc[n], _ADAM_ROWS[n],
            name=f"adamw_{n}")
    g_little = _unpack(_sum_slabs(recv_little, name="sum_little"), little_shards)
    g_rep = _unpack(_sum_slabs(everyone, name="sum_replicated"), [(1, 1)] + [s for _, s in replicated])
    loss = g_rep[0].reshape(())
    grads.update({n: gl for (n, _, _), gl in zip(little, g_little)})
    grads.update({n: gr for (n, _), gr in zip(replicated, g_rep[1:])})
    for n in names:
        if n in _MATMUL_WEIGHTS:
            continue
        shp = w_loc[n].shape
        d, mn, vn = _adamw(w_loc[n].reshape(two_d(shp)), grads[n].reshape(two_d(shp)), m_loc[n].reshape(two_d(shp)),
                           v_loc[n].reshape(two_d(shp)), name=f"adamw_{n}")
        delta[n], new_m[n], new_v[n] = d.reshape(shp), mn.reshape(shp), vn.reshape(shp)
    return (loss, grad_x[None], *[grads[n] for n in names], *[delta[n] for n in names],
            *[new_m[n] for n in names], *[new_v[n] for n in names])
```

```python
import functools

import jax
import jax.numpy as jnp
from jax import lax
from jax.experimental import pallas as pl
from jax.experimental.pallas import tpu as pltpu

F32 = jnp.float32
BF16 = jnp.bfloat16

N_DEV = 8
DEPTH = 2
D_MODEL = 1024
N_META = 16
FOX_HEADS = 8
FOX_HEAD_DIM = 64
FOX_WIDTH = 512
CONV_CH = 512
CONV_WIDTH = 31
POOL_WINDOWS = (2, 4, 8, 16)
POOL_GROUP = 256
D_FF = 2816
FFN_CONV_WIDTH = 3
RMS_EPS = 1e-6
LN_EPS = 1e-5
ADAM_LR = 0.001
ADAM_B1 = 0.9
ADAM_B2 = 0.999
ADAM_EPS = 1e-08
ADAM_WD = 0.01
ADAM_STEP = 10

CHUNK = 128
HALO = 32
NEG_BIG = -1e30
FLAT_W = 1024


def _round_up(n, m):
    return (n + m - 1) // m * m


def _sigmoid(x):
    return 1.0 / (1.0 + jnp.exp(-x))


def _fold8(p):
    acc = p[0:8, :]
    for r in range(1, p.shape[0] // 8):
        acc = acc + p[8 * r:8 * r + 8, :]
    return acc


def _mm(a, b, *, name, tb=False, tm=None, tn=None, tk=None, out_dtype=F32, res=None,
        a_map=None, b_map=None, o_map=None, out_shape=None, dims=None, after=None):
    if dims is None:
        m, k = a.shape
        n = b.shape[-2] if tb else b.shape[-1]
    else:
        m, n, k = dims
    tm, tn, tk = tm or m, tn or n, tk or k
    assert m % tm == 0 and n % tn == 0 and k % tk == 0, (name, m, n, k, tm, tn, tk)
    nk = k // tk
    a_map = a_map or (lambda i, j, kk: (i, kk))
    b_map = b_map or ((lambda i, j, kk: (j, kk)) if tb else (lambda i, j, kk: (kk, j)))
    o_map = o_map or (lambda i, j, kk: (i, j))
    out_shape = out_shape or (m, n)
    contract = (((1,), (1,)), ((), ())) if tb else (((1,), (0,)), ((), ()))
    has_res = res is not None

    def body(*refs):
        a_ref, b_ref = refs[0], refs[1]
        res_ref = refs[2] if has_res else None
        o_ref = refs[2 + has_res + (after is not None)]
        p = lax.dot_general(a_ref[...], b_ref[...], contract, preferred_element_type=F32)
        if nk == 1:
            if has_res:
                p = p + res_ref[...]
            o_ref[...] = p.astype(o_ref.dtype)
        else:
            acc_ref = refs[-1]
            kk = pl.program_id(2)

            @pl.when(kk == 0)
            def _():
                acc_ref[...] = p

            @pl.when(kk > 0)
            def _():
                acc_ref[...] += p

            @pl.when(kk == nk - 1)
            def _():
                r = acc_ref[...]
                if has_res:
                    r = r + res_ref[...]
                o_ref[...] = r.astype(o_ref.dtype)

    in_specs = [pl.BlockSpec((tm, tk), a_map), pl.BlockSpec((tn, tk) if tb else (tk, tn), b_map)]
    operands = [a, b]
    if has_res:
        in_specs.append(pl.BlockSpec((tm, tn), o_map))
        operands.append(res)
    if after is not None:
        in_specs.append(pl.BlockSpec(memory_space=pl.ANY))
        operands.append(after)
    return pl.pallas_call(
        body, name=name, grid=(m // tm, n // tn, nk),
        in_specs=in_specs, out_specs=pl.BlockSpec((tm, tn), o_map),
        out_shape=jax.ShapeDtypeStruct(out_shape, out_dtype),
        scratch_shapes=[pltpu.VMEM((tm, tn), F32)] if nk > 1 else [],
    )(*operands)


def _transpose(x, *, name, out_dtype, cols=None):
    r, c = x.shape
    cols = cols or c
    assert r % CHUNK == 0

    def body(x_ref, o_ref):
        o_ref[...] = x_ref[...].astype(F32).T.astype(o_ref.dtype)

    return pl.pallas_call(
        body, name=name, grid=(r // CHUNK,),
        in_specs=[pl.BlockSpec((CHUNK, cols), lambda i: (i, 0))],
        out_specs=pl.BlockSpec((cols, CHUNK), lambda i: (0, i)),
        out_shape=jax.ShapeDtypeStruct((cols, r), out_dtype),
    )(x)


def _rms_fwd(x, g, *, name, out_dtype):
    lp, dm = x.shape
    tr = lp // 4

    def body(x_ref, g_ref, o_ref):
        xv = x_ref[...]
        r = lax.rsqrt(jnp.mean(xv * xv, axis=-1, keepdims=True) + RMS_EPS)
        o_ref[...] = (xv * r * g_ref[...]).astype(o_ref.dtype)

    return pl.pallas_call(
        body, name=name, grid=(lp // tr,),
        in_specs=[pl.BlockSpec((tr, dm), lambda i: (i, 0)), pl.BlockSpec((1, dm), lambda i: (0, 0))],
        out_specs=pl.BlockSpec((tr, dm), lambda i: (i, 0)),
        out_shape=jax.ShapeDtypeStruct((lp, dm), out_dtype),
    )(x, g)


def _rms_bwd(x, g, dn, dres, *, name):
    lp, dm = x.shape
    tr = lp // 4

    def body(x_ref, g_ref, dn_ref, dres_ref, dh_ref, dhb_ref, dg_ref):
        xv = x_ref[...]
        r = lax.rsqrt(jnp.mean(xv * xv, axis=-1, keepdims=True) + RMS_EPS)
        xhat = xv * r
        dnv = dn_ref[...]

        @pl.when(pl.program_id(0) == 0)
        def _():
            dg_ref[...] = jnp.zeros_like(dg_ref)

        dg_ref[...] += jnp.sum(dnv * xhat, axis=0, keepdims=True)
        dxhat = dnv * g_ref[...]
        dx = r * (dxhat - xhat * jnp.mean(dxhat * xhat, axis=-1, keepdims=True))
        dh = dres_ref[...] + dx
        dh_ref[...] = dh
        dhb_ref[...] = dh.astype(BF16)

    row = pl.BlockSpec((tr, dm), lambda i: (i, 0))
    vec = pl.BlockSpec((1, dm), lambda i: (0, 0))
    return pl.pallas_call(
        body, name=name, grid=(lp // tr,),
        in_specs=[row, vec, row, row], out_specs=[row, row, vec],
        out_shape=[jax.ShapeDtypeStruct((lp, dm), F32), jax.ShapeDtypeStruct((lp, dm), BF16),
                   jax.ShapeDtypeStruct((1, dm), F32)],
    )(x, g, dn, dres)


def _loss_head(h, g, tgt, n_real, *, name):
    lp, dm = h.shape
    tr = lp // 4

    def body(x_ref, g_ref, t_ref, loss_ref, dh_ref, dhb_ref, dg_ref):
        i = pl.program_id(0)
        xv = x_ref[...]
        r = lax.rsqrt(jnp.mean(xv * xv, axis=-1, keepdims=True) + RMS_EPS)
        xhat = xv * r
        gv = g_ref[...]
        y = xhat * gv
        t = i * tr + lax.broadcasted_iota(jnp.int32, (tr, 1), 0)
        valid = (t >= N_META) & (t < n_real)
        diff = jnp.where(valid, y - t_ref[...], 0.0)

        @pl.when(i == 0)
        def _():
            loss_ref[...] = jnp.zeros_like(loss_ref)
            dg_ref[...] = jnp.zeros_like(dg_ref)

        row_sq = jnp.sum(diff * diff, axis=-1, keepdims=True) * (1.0 / dm)
        part = 0.5 * jnp.sum(row_sq, axis=0, keepdims=True)
        loss_ref[...] += jnp.broadcast_to(part, loss_ref.shape)
        dy = diff * (1.0 / dm)
        dg_ref[...] += jnp.sum(dy * xhat, axis=0, keepdims=True)
        dxhat = dy * gv
        dx = r * (dxhat - xhat * jnp.mean(dxhat * xhat, axis=-1, keepdims=True))
        dh_ref[...] = dx
        dhb_ref[...] = dx.astype(BF16)

    row = pl.BlockSpec((tr, dm), lambda i: (i, 0))
    vec = pl.BlockSpec((1, dm), lambda i: (0, 0))
    return pl.pallas_call(
        body, name=name, grid=(lp // tr,),
        in_specs=[row, vec, row],
        out_specs=[pl.BlockSpec((1, 128), lambda i: (0, 0)), row, row, vec],
        out_shape=[jax.ShapeDtypeStruct((1, 128), F32), jax.ShapeDtypeStruct((lp, dm), F32),
                   jax.ShapeDtypeStruct((lp, dm), BF16), jax.ShapeDtypeStruct((1, dm), F32)],
    )(h, g, tgt)


def _tri(upper):
    r = lax.broadcasted_iota(jnp.int32, (CHUNK, CHUNK), 0)
    c = lax.broadcasted_iota(jnp.int32, (CHUNK, CHUNK), 1)
    return jnp.where(r <= c if upper else r >= c, 1.0, 0.0).astype(F32)


def _fox_prep(f_t, b_f, *, name):
    nh, lp = f_t.shape
    nch = lp // CHUNK

    def body(f_ref, b_ref, c_ref):
        tri = _tri(True)
        carry = jnp.zeros((nh, 1), F32)
        for blk in range(nch):
            cols = slice(blk * CHUNK, (blk + 1) * CHUNK)
            z = f_ref[:, cols] + b_ref[...]
            logf = jnp.minimum(z, 0.0) - jnp.log(1.0 + jnp.exp(-jnp.abs(z)))
            cb = jnp.dot(logf, tri, preferred_element_type=F32, precision=lax.Precision.HIGHEST)
            c_ref[:, cols] = cb + carry
            carry = carry + jnp.sum(logf, axis=1, keepdims=True)

    return pl.pallas_call(
        body, name=name, out_shape=jax.ShapeDtypeStruct((nh, lp), F32),
    )(f_t, b_f)


def _fox_bwd(dc, f_t, b_f, *, name):
    nh, lp = f_t.shape
    nch = lp // CHUNK

    def body(dc_ref, f_ref, b_ref, df_ref, db_ref):
        tri = _tri(False)
        carry = jnp.zeros((nh, 1), F32)
        db = jnp.zeros((nh, 1), F32)
        df_ref[...] = jnp.zeros_like(df_ref)
        for blk in reversed(range(nch)):
            cols = slice(blk * CHUNK, (blk + 1) * CHUNK)
            dcb = dc_ref[:, cols]
            dlogf = jnp.dot(dcb, tri, preferred_element_type=F32, precision=lax.Precision.HIGHEST) + carry
            carry = carry + jnp.sum(dcb, axis=1, keepdims=True)
            z = f_ref[:, cols] + b_ref[...]
            dz = dlogf * _sigmoid(-z)
            df_ref[0:nh, cols] = dz
            db = db + jnp.sum(dz, axis=1, keepdims=True)
        db_ref[...] = db

    return pl.pallas_call(
        body, name=name,
        out_shape=[jax.ShapeDtypeStruct((128, lp), F32), jax.ShapeDtypeStruct((nh, 1), F32)],
    )(dc, f_t, b_f)


def _attn_blocks(lp):
    tq = lp // 4
    return tq, [(i * tq, min(lp, _round_up((i + 1) * tq, CHUNK))) for i in range(4)]


def _attn_probs(q2, k_h, c_row, row0, n):
    tq = q2.shape[0]
    s = lax.dot_general(q2, k_h, (((1,), (1,)), ((), ())), preferred_element_type=F32)
    s = s * (FOX_HEAD_DIM ** -0.5) - c_row
    t = row0 + lax.broadcasted_iota(jnp.int32, (tq, n), 0)
    sidx = lax.broadcasted_iota(jnp.int32, (tq, n), 1)
    s = jnp.where(sidx <= t, s, NEG_BIG)
    p = jnp.exp(s - jnp.max(s, axis=1, keepdims=True))
    return p / jnp.sum(p, axis=1, keepdims=True)


def _attn_fwd(qkv, c3, *, name):
    lp = qkv.shape[0]
    tq, blocks = _attn_blocks(lp)

    def body(q_ref, k_ref, v_ref, c_ref, o_ref):
        lane = lax.broadcasted_iota(jnp.int32, (1, 128), 1)
        zero = jnp.zeros((), BF16)
        for i, (row0, n) in enumerate(blocks):
            q2 = q_ref[row0:row0 + tq, :]
            acc = jnp.zeros((tq, 128), F32)
            for hd in range(2):
                sel = (lane < 64) if hd == 0 else (lane >= 64)
                k_h = jnp.where(sel, k_ref[0:n, :], zero)
                v_h = jnp.where(sel, v_ref[0:n, :], zero)
                p = _attn_probs(q2, k_h, c_ref[hd:hd + 1, 0:n], row0, n)
                acc = acc + jnp.dot(p.astype(BF16), v_h, preferred_element_type=F32)
            o_ref[row0:row0 + tq, :] = acc.astype(BF16)

    blk = lambda off: pl.BlockSpec((lp, 128), lambda p: (0, off + p))
    return pl.pallas_call(
        body, name=name, grid=(4,),
        in_specs=[blk(0), blk(4), blk(8), pl.BlockSpec((None, 2, lp), lambda p: (p, 0, 0))],
        out_specs=pl.BlockSpec((lp, 128), lambda p: (0, p)),
        out_shape=jax.ShapeDtypeStruct((lp, FOX_WIDTH), BF16),
    )(qkv, qkv, qkv, c3)


def _attn_bwd(qkv, q_t, dcat, do_t, c3, *, name):
    lp = qkv.shape[0]
    tq, blocks = _attn_blocks(lp)
    scale = FOX_HEAD_DIM ** -0.5

    def body(q_ref, k_ref, v_ref, qt_ref, do_ref, dot_ref, c_ref, dq_ref, dkt_ref, dvt_ref, dc_ref,
             dkt_acc, dvt_acc):
        lane = lax.broadcasted_iota(jnp.int32, (1, 128), 1)
        sub = lax.broadcasted_iota(jnp.int32, (128, 1), 0)
        zero = jnp.zeros((), BF16)
        dkt_acc[...] = jnp.zeros_like(dkt_acc)
        dvt_acc[...] = jnp.zeros_like(dvt_acc)
        dc_ref[...] = jnp.zeros_like(dc_ref)
        for i, (row0, n) in enumerate(blocks):
            rows = slice(row0, row0 + tq)
            q2 = q_ref[rows, :]
            do2 = do_ref[rows, :].astype(BF16)
            dq_acc = jnp.zeros((tq, 128), F32)
            for hd in range(2):
                sel = (lane < 64) if hd == 0 else (lane >= 64)
                sel_t = (sub < 64) if hd == 0 else (sub >= 64)
                k_h = jnp.where(sel, k_ref[0:n, :], zero)
                v_h = jnp.where(sel, v_ref[0:n, :], zero)
                p = _attn_probs(q2, k_h, c_ref[hd:hd + 1, 0:n], row0, n)
                dp = lax.dot_general(do2, v_h, (((1,), (1,)), ((), ())), preferred_element_type=F32)
                delta = jnp.sum(p * dp, axis=1, keepdims=True)
                ds = p * (dp - delta)
                dsb = ds.astype(BF16)
                dq_acc = dq_acc + jnp.dot(dsb, k_h, preferred_element_type=F32)
                qt_h = jnp.where(sel_t, qt_ref[:, rows], zero)
                dot_h = jnp.where(sel_t, dot_ref[:, rows], zero)
                dkt_acc[:, 0:n] += jnp.dot(qt_h, dsb, preferred_element_type=F32)
                dvt_acc[:, 0:n] += jnp.dot(dot_h, p.astype(BF16), preferred_element_type=F32)
                dc_ref[hd:hd + 1, 0:n] -= jnp.sum(ds, axis=0, keepdims=True)
            dq_ref[rows, :] = (dq_acc * scale).astype(BF16)
        dkt_ref[...] = (dkt_acc[...] * scale).astype(BF16)
        dvt_ref[...] = dvt_acc[...].astype(BF16)

    blk = lambda off: pl.BlockSpec((lp, 128), lambda p: (0, off + p))
    blk_t = pl.BlockSpec((128, lp), lambda p: (p, 0))
    c_spec = pl.BlockSpec((None, 2, lp), lambda p: (p, 0, 0))
    return pl.pallas_call(
        body, name=name, grid=(4,),
        in_specs=[blk(0), blk(4), blk(8), blk_t, blk(0), blk_t, c_spec],
        out_specs=[blk(0), blk_t, blk_t, c_spec],
        out_shape=[jax.ShapeDtypeStruct((lp, FOX_WIDTH), BF16), jax.ShapeDtypeStruct((FOX_WIDTH, lp), BF16),
                   jax.ShapeDtypeStruct((FOX_WIDTH, lp), BF16), jax.ShapeDtypeStruct((4, 2, lp), F32)],
        scratch_shapes=[pltpu.VMEM((128, lp), F32), pltpu.VMEM((128, lp), F32)],
    )(qkv, qkv, qkv, q_t, dcat, do_t, c3)


def _ln_stats(x):
    mu = jnp.mean(x, axis=-1, keepdims=True)
    xc = x - mu
    var = jnp.mean(xc * xc, axis=-1, keepdims=True)
    rstd = lax.rsqrt(var + LN_EPS)
    return xc * rstd, rstd


def _conv_fwd(agf, conv_w, conv_b, ln_g, ln_b, *, name):
    lp = agf.shape[0]
    nch = lp // CHUNK
    c = CONV_CH

    def body(a_ref, g_ref, w_ref, b_ref, lg_ref, lb_ref, u0_ref, u1_ref, u3_ref, u0s):
        u0s[0:HALO, :] = jnp.zeros((HALO, c), F32)

        def glu(ci, _):
            rows = pl.ds(pl.multiple_of(ci * CHUNK, CHUNK), CHUNK)
            u0 = a_ref[rows, :] * _sigmoid(g_ref[rows, :])
            u0_ref[rows, :] = u0
            u0s[pl.ds(pl.multiple_of(ci * CHUNK + HALO, 8), CHUNK), :] = u0
            return 0

        lax.fori_loop(0, nch, glu, 0)

        def conv(ci, _):
            r0 = pl.multiple_of(ci * CHUNK, CHUNK)
            rows = pl.ds(r0, CHUNK)
            for lg in range(c // 128):
                lanes = slice(lg * 128, (lg + 1) * 128)
                win = u0s[pl.ds(r0, CHUNK + HALO), lanes]
                acc = jnp.broadcast_to(b_ref[:, lanes], (CHUNK, 128))
                for k in range(CONV_WIDTH):
                    s = CONV_WIDTH - 1 - k
                    sh = win if s == 0 else pltpu.roll(win, s, 0)
                    acc = acc + w_ref[k:k + 1, lanes] * sh[HALO:HALO + CHUNK, :]
                u1_ref[rows, lanes] = acc
            xhat, _ = _ln_stats(u1_ref[rows, :])
            y = xhat * lg_ref[...] + lb_ref[...]
            u3_ref[rows, :] = (y * _sigmoid(y)).astype(BF16)
            return 0

        lax.fori_loop(0, nch, conv, 0)

    full = lambda shape: pl.BlockSpec(shape, lambda i: (0, 0))
    return pl.pallas_call(
        body, name=name, grid=(1,),
        in_specs=[pl.BlockSpec((lp, c), lambda i: (0, 0)), pl.BlockSpec((lp, c), lambda i: (0, 1)),
                  full((CONV_WIDTH, c)), full((1, c)), full((1, c)), full((1, c))],
        out_specs=[full((lp, c)), full((lp, c)), full((lp, c))],
        out_shape=[jax.ShapeDtypeStruct((lp, c), F32), jax.ShapeDtypeStruct((lp, c), F32),
                   jax.ShapeDtypeStruct((lp, c), BF16)],
        scratch_shapes=[pltpu.VMEM((lp + HALO, c), F32)],
    )(agf, agf, conv_w, conv_b, ln_g, ln_b)


def _conv_bwd(dcat, u0, u1, agf, conv_w, ln_g, ln_b, *, name):
    lp = agf.shape[0]
    nch = lp // CHUNK
    c = CONV_CH
    wlen = CHUNK + HALO

    def body(du3_ref, u0_ref, u1_ref, a_ref, g_ref, w_ref, lg_ref, lb_ref,
             dag_ref, dw_ref, db_ref, dlg_ref, dlb_ref, du1s, u0s, dwacc, vacc):
        du1s[lp:lp + HALO, :] = jnp.zeros((HALO, c), F32)
        u0s[0:HALO, :] = jnp.zeros((HALO, c), F32)
        dwacc[...] = jnp.zeros_like(dwacc)
        vacc[...] = jnp.zeros_like(vacc)

        def ln_bwd(ci, _):
            r0 = pl.multiple_of(ci * CHUNK, CHUNK)
            rows = pl.ds(r0, CHUNK)
            xhat, rstd = _ln_stats(u1_ref[rows, :])
            y = xhat * lg_ref[...] + lb_ref[...]
            sg = _sigmoid(y)
            du2 = du3_ref[rows, :] * (sg * (1.0 + y * (1.0 - sg)))
            vacc[0:8, :] += _fold8(du2 * xhat)
            vacc[8:16, :] += _fold8(du2)
            dxhat = du2 * lg_ref[...]
            du1 = rstd * (dxhat - jnp.mean(dxhat, axis=-1, keepdims=True)
                          - xhat * jnp.mean(dxhat * xhat, axis=-1, keepdims=True))
            vacc[16:24, :] += _fold8(du1)
            du1s[rows, :] = du1
            u0s[pl.ds(pl.multiple_of(ci * CHUNK + HALO, 8), CHUNK), :] = u0_ref[rows, :]
            return 0

        lax.fori_loop(0, nch, ln_bwd, 0)

        def conv_bwd(ci, _):
            r0 = pl.multiple_of(ci * CHUNK, CHUNK)
            rows = pl.ds(r0, CHUNK)
            for lg in range(c // 128):
                lanes = slice(lg * 128, (lg + 1) * 128)
                dwin = du1s[pl.ds(r0, wlen), lanes]
                uwin = u0s[pl.ds(r0, wlen), lanes]
                du1 = dwin[0:CHUNK, :]
                acc = jnp.zeros((CHUNK, 128), F32)
                for k in range(CONV_WIDTH):
                    s = CONV_WIDTH - 1 - k
                    dsh = dwin if s == 0 else pltpu.roll(dwin, wlen - s, 0)
                    acc = acc + w_ref[k:k + 1, lanes] * dsh[0:CHUNK, :]
                    ush = uwin if s == 0 else pltpu.roll(uwin, s, 0)
                    dwacc[8 * k:8 * k + 8, lanes] += _fold8(du1 * ush[HALO:HALO + CHUNK, :])
                sg = _sigmoid(g_ref[rows, lanes])
                a = a_ref[rows, lanes]
                dag_ref[rows, lanes] = (acc * sg).astype(BF16)
                dag_ref[rows, slice(c + lg * 128, c + (lg + 1) * 128)] = (acc * a * sg * (1.0 - sg)).astype(BF16)
            return 0

        lax.fori_loop(0, nch, conv_bwd, 0)
        for k in range(CONV_WIDTH):
            dw_ref[k:k + 1, :] = jnp.sum(dwacc[8 * k:8 * k + 8, :], axis=0, keepdims=True)
        dlg_ref[...] = jnp.sum(vacc[0:8, :], axis=0, keepdims=True)
        dlb_ref[...] = jnp.sum(vacc[8:16, :], axis=0, keepdims=True)
        db_ref[...] = jnp.sum(vacc[16:24, :], axis=0, keepdims=True)

    full = lambda shape: pl.BlockSpec(shape, lambda i: (0, 0))
    vec = jax.ShapeDtypeStruct((1, c), F32)
    return pl.pallas_call(
        body, name=name, grid=(1,),
        in_specs=[pl.BlockSpec((lp, c), lambda i: (0, 1)), full((lp, c)), full((lp, c)),
                  pl.BlockSpec((lp, c), lambda i: (0, 0)), pl.BlockSpec((lp, c), lambda i: (0, 1)),
                  full((CONV_WIDTH, c)), full((1, c)), full((1, c))],
        out_specs=[full((lp, 2 * c)), full((CONV_WIDTH, c)), full((1, c)), full((1, c)), full((1, c))],
        out_shape=[jax.ShapeDtypeStruct((lp, 2 * c), BF16), jax.ShapeDtypeStruct((CONV_WIDTH, c), F32), vec, vec, vec],
        scratch_shapes=[pltpu.VMEM((lp + HALO, c), F32), pltpu.VMEM((lp + HALO, c), F32),
                        pltpu.VMEM((8 * CONV_WIDTH, c), F32), pltpu.VMEM((24, c), F32)],
    )(dcat, u0, u1, agf, agf, conv_w, ln_g, ln_b)


FFN_TILE = 256
FFN_PAD = 8


def _ffn_conv(xs, w_ref, b_ref, half, r0):
    win = xs[half, pl.ds(r0, CHUNK + FFN_PAD), :]
    acc = jnp.broadcast_to(b_ref[half], (CHUNK, FFN_TILE))
    for k in range(FFN_CONV_WIDTH):
        s = FFN_CONV_WIDTH - 1 - k
        sh = win if s == 0 else pltpu.roll(win, s, 0)
        acc = acc + w_ref[half, k:k + 1, :] * sh[FFN_PAD:FFN_PAD + CHUNK, :]
    return acc


def _ffn_act_fwd(up3, w3, b3, *, name):
    _, lp, f = up3.shape
    nch = lp // CHUNK

    def body(up_ref, w_ref, b_ref, act_ref, xs):
        for half in range(2):
            xs[half, 0:FFN_PAD, :] = jnp.zeros((FFN_PAD, FFN_TILE), F32)
            xs[half, FFN_PAD:FFN_PAD + lp, :] = up_ref[half]

        def chunk(ci, _):
            r0 = pl.multiple_of(ci * CHUNK, CHUNK)
            gate = _ffn_conv(xs, w_ref, b_ref, 0, r0)
            val = _ffn_conv(xs, w_ref, b_ref, 1, r0)
            act_ref[pl.ds(r0, CHUNK), :] = (gate * _sigmoid(gate) * val).astype(BF16)
            return 0

        lax.fori_loop(0, nch, chunk, 0)

    return pl.pallas_call(
        body, name=name, grid=(f // FFN_TILE,),
        in_specs=[pl.BlockSpec((2, lp, FFN_TILE), lambda j: (0, 0, j)),
                  pl.BlockSpec((2, FFN_CONV_WIDTH, FFN_TILE), lambda j: (0, 0, j)),
                  pl.BlockSpec((2, 1, FFN_TILE), lambda j: (0, 0, j))],
        out_specs=pl.BlockSpec((lp, FFN_TILE), lambda j: (0, j)),
        out_shape=jax.ShapeDtypeStruct((lp, f), BF16),
        scratch_shapes=[pltpu.VMEM((2, lp + FFN_PAD, FFN_TILE), F32)],
    )(up3, w3, b3)


def _ffn_act_bwd(up3, w3, b3, dact, *, name):
    _, lp, f = up3.shape
    nch = lp // CHUNK
    wlen = CHUNK + FFN_PAD

    def body(up_ref, w_ref, b_ref, dact_ref, dup_ref, dw_ref, db_ref, xs, ds, wacc):
        for half in range(2):
            xs[half, 0:FFN_PAD, :] = jnp.zeros((FFN_PAD, FFN_TILE), F32)
            xs[half, FFN_PAD:FFN_PAD + lp, :] = up_ref[half]
            ds[half, lp:lp + FFN_PAD, :] = jnp.zeros((FFN_PAD, FFN_TILE), F32)
        wacc[...] = jnp.zeros_like(wacc)

        def act_bwd(ci, _):
            r0 = pl.multiple_of(ci * CHUNK, CHUNK)
            rows = pl.ds(r0, CHUNK)
            gate = _ffn_conv(xs, w_ref, b_ref, 0, r0)
            val = _ffn_conv(xs, w_ref, b_ref, 1, r0)
            sg = _sigmoid(gate)
            da = dact_ref[rows, :]
            ds[0, rows, :] = da * val * (sg * (1.0 + gate * (1.0 - sg)))
            ds[1, rows, :] = da * (gate * sg)
            return 0

        lax.fori_loop(0, nch, act_bwd, 0)

        def conv_bwd(ci, _):
            r0 = pl.multiple_of(ci * CHUNK, CHUNK)
            rows = pl.ds(r0, CHUNK)
            for half in range(2):
                dwin = ds[half, pl.ds(r0, wlen), :]
                xwin = xs[half, pl.ds(r0, wlen), :]
                d0 = dwin[0:CHUNK, :]
                acc = jnp.zeros((CHUNK, FFN_TILE), F32)
                for k in range(FFN_CONV_WIDTH):
                    s = FFN_CONV_WIDTH - 1 - k
                    dsh = dwin if s == 0 else pltpu.roll(dwin, wlen - s, 0)
                    acc = acc + w_ref[half, k:k + 1, :] * dsh[0:CHUNK, :]
                    xsh = xwin if s == 0 else pltpu.roll(xwin, s, 0)
                    wacc[half, 8 * k:8 * k + 8, :] += _fold8(d0 * xsh[FFN_PAD:FFN_PAD + CHUNK, :])
                wacc[half, 24:32, :] += _fold8(d0)
                dup_ref[half, rows, :] = acc.astype(BF16)
            return 0

        lax.fori_loop(0, nch, conv_bwd, 0)
        for half in range(2):
            for k in range(FFN_CONV_WIDTH):
                dw_ref[half, k:k + 1, :] = jnp.sum(wacc[half, 8 * k:8 * k + 8, :], axis=0, keepdims=True)
            db_ref[half] = jnp.sum(wacc[half, 24:32, :], axis=0, keepdims=True)

    return pl.pallas_call(
        body, name=name, grid=(f // FFN_TILE,),
        in_specs=[pl.BlockSpec((2, lp, FFN_TILE), lambda j: (0, 0, j)),
                  pl.BlockSpec((2, FFN_CONV_WIDTH, FFN_TILE), lambda j: (0, 0, j)),
                  pl.BlockSpec((2, 1, FFN_TILE), lambda j: (0, 0, j)),
                  pl.BlockSpec((lp, FFN_TILE), lambda j: (0, j))],
        out_specs=[pl.BlockSpec((2, lp, FFN_TILE), lambda j: (0, 0, j)),
                   pl.BlockSpec((2, FFN_CONV_WIDTH, FFN_TILE), lambda j: (0, 0, j)),
                   pl.BlockSpec((2, 1, FFN_TILE), lambda j: (0, 0, j))],
        out_shape=[jax.ShapeDtypeStruct((2, lp, f), BF16), jax.ShapeDtypeStruct((2, FFN_CONV_WIDTH, f), F32),
                   jax.ShapeDtypeStruct((2, 1, f), F32)],
        scratch_shapes=[pltpu.VMEM((2, lp + FFN_PAD, FFN_TILE), F32), pltpu.VMEM((2, lp + FFN_PAD, FFN_TILE), F32),
                        pltpu.VMEM((2, 32, FFN_TILE), F32)],
    )(up3, w3, b3, dact)


POOL_PAD = 16


def _inv_count(r0, w):
    t = r0 + lax.broadcasted_iota(jnp.int32, (CHUNK, 1), 0)
    return 1.0 / jnp.minimum(t + 1, w).astype(F32)


def _pool_fwd(n, pool_w, pool_b, pool_scale, h, *, name):
    lp, dm = n.shape
    nch = lp // CHUNK
    g = POOL_GROUP

    def body(n_ref, w_ref, b_ref, s_ref, h_ref, ho_ref, d_ref, z_ref, xs):
        gi = pl.program_id(0)
        xs[0:POOL_PAD, :] = jnp.zeros((POOL_PAD, g), F32)
        xs[POOL_PAD:POOL_PAD + lp, :] = n_ref[...]
        for idx, w in enumerate(POOL_WINDOWS):
            @pl.when(gi == idx)
            def _(w=w):
                def chunk(ci, _):
                    r0 = pl.multiple_of(ci * CHUNK, CHUNK)
                    win = xs[pl.ds(r0, CHUNK + POOL_PAD), :]
                    acc = win
                    for j in range(1, w):
                        acc = acc + pltpu.roll(win, j, 0)
                    x = win[POOL_PAD:POOL_PAD + CHUNK, :]
                    d_ref[pl.ds(r0, CHUNK), :] = (acc[POOL_PAD:POOL_PAD + CHUNK, :] * _inv_count(r0, w) - x).astype(BF16)
                    return 0

                lax.fori_loop(0, nch, chunk, 0)

        z = jnp.dot(d_ref[...], w_ref[...], preferred_element_type=F32) + b_ref[...]
        z_ref[...] = z
        ho_ref[...] = h_ref[...] + z * s_ref[...]

    col = pl.BlockSpec((lp, g), lambda i: (0, i))
    vec = pl.BlockSpec((1, g), lambda i: (0, i))
    return pl.pallas_call(
        body, name=name, grid=(len(POOL_WINDOWS),),
        in_specs=[col, pl.BlockSpec((None, g, g), lambda i: (i, 0, 0)), vec, vec, col],
        out_specs=[col, col, col],
        out_shape=[jax.ShapeDtypeStruct((lp, dm), F32), jax.ShapeDtypeStruct((lp, dm), BF16),
                   jax.ShapeDtypeStruct((lp, dm), F32)],
        scratch_shapes=[pltpu.VMEM((lp + POOL_PAD, g), F32)],
    )(n, pool_w, pool_b, pool_scale, h)


def _pool_bwd(dy, z, pool_w, pool_scale, *, name):
    lp, dm = dy.shape
    nch = lp // CHUNK
    g = POOL_GROUP
    wlen = CHUNK + POOL_PAD

    def body(dy_ref, z_ref, w_ref, s_ref, dn_ref, dz_ref, dsc_ref, db_ref, ys, dd):
        gi = pl.program_id(0)
        dyv = dy_ref[...]
        dsc_ref[...] = jnp.sum(dyv * z_ref[...], axis=0, keepdims=True)
        dz = dyv * s_ref[...]
        db_ref[...] = jnp.sum(dz, axis=0, keepdims=True)
        dzb = dz.astype(BF16)
        dz_ref[...] = dzb
        dd[...] = lax.dot_general(dzb, w_ref[...], (((1,), (1,)), ((), ())), preferred_element_type=F32)
        ys[lp:lp + POOL_PAD, :] = jnp.zeros((POOL_PAD, g), F32)
        for idx, w in enumerate(POOL_WINDOWS):
            @pl.when(gi == idx)
            def _(w=w):
                def scale(ci, _):
                    r0 = pl.multiple_of(ci * CHUNK, CHUNK)
                    ys[pl.ds(r0, CHUNK), :] = dd[pl.ds(r0, CHUNK), :] * _inv_count(r0, w)
                    return 0

                lax.fori_loop(0, nch, scale, 0)

                def chunk(ci, _):
                    r0 = pl.multiple_of(ci * CHUNK, CHUNK)
                    win = ys[pl.ds(r0, wlen), :]
                    acc = win
                    for j in range(1, w):
                        acc = acc + pltpu.roll(win, wlen - j, 0)
                    dn_ref[pl.ds(r0, CHUNK), :] = acc[0:CHUNK, :] - dd[pl.ds(r0, CHUNK), :]
                    return 0

                lax.fori_loop(0, nch, chunk, 0)

    col = pl.BlockSpec((lp, g), lambda i: (0, i))
    vec = pl.BlockSpec((1, g), lambda i: (0, i))
    return pl.pallas_call(
        body, name=name, grid=(len(POOL_WINDOWS),),
        in_specs=[col, col, pl.BlockSpec((None, g, g), lambda i: (i, 0, 0)), vec],
        out_specs=[col, col, vec, vec],
        out_shape=[jax.ShapeDtypeStruct((lp, dm), F32), jax.ShapeDtypeStruct((lp, dm), BF16),
                   jax.ShapeDtypeStruct((1, dm), F32), jax.ShapeDtypeStruct((1, dm), F32)],
        scratch_shapes=[pltpu.VMEM((lp + POOL_PAD, g), F32), pltpu.VMEM((lp, g), F32)],
    )(dy, z, pool_w, pool_scale)


def _ffn_fwd(h, g, w_up, w3, b3, w_down, tag):
    lp = h.shape[0]
    nj = D_FF // FFN_TILE
    n = _rms_fwd(h, g, name=f"rms_ffn{tag}", out_dtype=BF16)
    up2 = _mm(n, w_up, name=f"mm_up{tag}", tn=FFN_TILE, dims=(lp, 2 * D_FF, D_MODEL),
              o_map=lambda i, j, k: (j // nj, j % nj), out_shape=(2 * lp, D_FF))
    up3 = up2.reshape(2, lp, D_FF)
    act = _ffn_act_fwd(up3, w3, b3, name=f"ffn_act{tag}")
    h_out = _mm(act, w_down, name=f"mm_down{tag}", tn=256, res=h)
    return h_out, (n, up3, act)


def _ffn_bwd(dh, dhb, h, g, saved, w_up, w3, b3, w_down, tag, after=None):
    lp = h.shape[0]
    nj = D_FF // FFN_TILE
    n, up3, act = saved
    act_t = _transpose(act, name=f"t_act{tag}", out_dtype=BF16)
    dw_down = _mm(act_t, dhb, name=f"mm_dwdown{tag}", tm=704, out_dtype=BF16)
    dact = _mm(dhb, w_down, name=f"mm_dact{tag}", tb=True, tn=256, after=after)
    dup3, dcw, dcb = _ffn_act_bwd(up3, w3, b3, dact, name=f"ffn_act_bwd{tag}")
    dup2 = dup3.reshape(2 * lp, D_FF)
    n_t = _transpose(n, name=f"t_nffn{tag}", out_dtype=BF16)
    dw_up = _mm_dw_up(n_t, dup2, name=f"mm_dwup{tag}")
    dn = _mm(dup2, w_up, name=f"mm_dnffn{tag}", tb=True, tk=FFN_TILE, dims=(lp, D_MODEL, 2 * D_FF),
             a_map=lambda i, j, k: (k // nj, k % nj), b_map=lambda i, j, k: (0, k))
    dh_in, dh_in_b, dg = _rms_bwd(h, g, dn, dh, name=f"rms_bwd_ffn{tag}")
    return dh_in, dh_in_b, (dg, dw_up, dcw, dcb, dw_down)


def _local_step(x, tgt, wt):
    seq = x.shape[0]
    n_real = N_META + seq
    lp = _round_up(n_real, CHUNK)
    pad = jnp.zeros((lp - n_real, D_MODEL), F32)
    h0 = jnp.concatenate([wt["meta"], x, pad], axis=0)
    tgt_p = jnp.concatenate([jnp.zeros((N_META, D_MODEL), F32), tgt, pad], axis=0)
    w_in_p = wt["w_in_p"]

    n0 = _rms_fwd(h0, wt["g_even"], name="rms_even", out_dtype=BF16)
    qkv = _mm(n0, w_in_p, name="mm_qkv", tn=512, dims=(lp, 3 * FOX_WIDTH, D_MODEL), out_dtype=BF16,
              after=wt["ffn_started"])
    ag = _mm(n0, w_in_p, name="mm_ag", tn=512, dims=(lp, 2 * CONV_CH, D_MODEL),
             b_map=lambda i, j, k: (0, 3 + j))
    f_t = _mm(wt["wf_t"], n0, name="mm_ft", tb=True)
    c_row = _fox_prep(f_t, wt["b_f"], name="fox_prep")
    c3 = c_row.reshape(4, 2, lp)
    o = _attn_fwd(qkv, c3, name="attn_fwd")
    u0, u1, u3 = _conv_fwd(ag, wt["conv_w"], wt["conv_b"], wt["ln_g"], wt["ln_b"], name="conv_fwd")
    cat = jnp.concatenate([o, u3], axis=1)
    h1 = _mm(cat, wt["w_out"], name="mm_out", tn=256, res=h0)
    w_up0, w_down0 = wt["ffn_weights"](0, h1)
    h2, saved0 = _ffn_fwd(h1, wt["ffn_norm"][0:1], w_up0, wt["fcw3"][0], wt["fcb3"][0], w_down0, 0)

    n2 = _rms_fwd(h2, wt["g_odd"], name="rms_odd", out_dtype=F32)
    h3, dpool, z = _pool_fwd(n2, wt["pool_w"], wt["pool_b"], wt["pool_scale"], h2, name="pool_fwd")
    w_up1, w_down1 = wt["ffn_weights"](1, h3)
    h4, saved1 = _ffn_fwd(h3, wt["ffn_norm"][1:2], w_up1, wt["fcw3"][1], wt["fcb3"][1], w_down1, 1)

    loss, dh4, dh4b, d_gfinal = _loss_head(h4, wt["g_final"], tgt_p, n_real, name="loss_head")

    dh3, dh3b, gf1 = _ffn_bwd(dh4, dh4b, h3, wt["ffn_norm"][1:2], saved1, w_up1, wt["fcw3"][1],
                              wt["fcb3"][1], w_down1, 1)
    send1, token1 = _send_ffn_grads(gf1[1], gf1[4], 1)
    dn2, dzb, d_pscale, d_pb = _pool_bwd(dh3, z, wt["pool_w"], wt["pool_scale"], name="pool_bwd")
    dpool_t = _transpose(dpool, name="t_dpool", out_dtype=BF16)
    d_pw = _mm(dpool_t, dzb, name="mm_dpoolw", tm=POOL_GROUP, tn=POOL_GROUP, dims=(D_MODEL, POOL_GROUP, lp),
               b_map=lambda i, j, k: (0, i), o_map=lambda i, j, k: (i, 0), out_shape=(D_MODEL, POOL_GROUP),
               out_dtype=BF16)
    dh2, dh2b, d_godd = _rms_bwd(h2, wt["g_odd"], dn2, dh3, name="rms_bwd_odd")
    dh1, dh1b, gf0 = _ffn_bwd(dh2, dh2b, h1, wt["ffn_norm"][0:1], saved0, w_up0, wt["fcw3"][0],
                              wt["fcb3"][0], w_down0, 0, after=token1)

    send0, token0 = _send_ffn_grads(gf0[1], gf0[4], 0)
    cat_t = _transpose(cat, name="t_cat", out_dtype=BF16)
    d_wout = _mm(cat_t, dh1b, name="mm_dwout", tm=512, out_dtype=BF16)
    dcat = _mm(dh1b, wt["w_out"], name="mm_dcat", tb=True, tn=256, after=token0)
    q_t = _transpose(qkv, name="t_q", out_dtype=BF16, cols=FOX_WIDTH)
    do_t = _transpose(dcat, name="t_do", out_dtype=BF16, cols=FOX_WIDTH)
    dq, dk_t, dv_t, dc3 = _attn_bwd(qkv, q_t, dcat, do_t, c3, name="attn_bwd")
    dk = _transpose(dk_t, name="t_dk", out_dtype=BF16)
    dv = _transpose(dv_t, name="t_dv", out_dtype=BF16)
    df_t, d_bf = _fox_bwd(dc3.reshape(FOX_HEADS, lp), f_t, wt["b_f"], name="fox_bwd")
    df = _transpose(df_t, name="t_df", out_dtype=BF16)
    dag, d_convw, d_convb, d_lng, d_lnb = _conv_bwd(dcat, u0, u1, ag, wt["conv_w"], wt["ln_g"], wt["ln_b"],
                                                    name="conv_bwd")
    dproj = jnp.concatenate([dq, dk, dv, dag, df], axis=1)
    n0_t = _transpose(n0, name="t_n0", out_dtype=BF16)
    d_win = _mm_dw_in(n0_t, dproj, name="mm_dwin")

    layers = lambda i: jnp.stack([gf0[i], gf1[i]])
    grads = dict(
        w_in=d_win, conv_w=d_convw[None], w_out=d_wout, mix_norm_odd=d_godd,
        pool_w=d_pw.reshape(len(POOL_WINDOWS), POOL_GROUP, POOL_GROUP),
        pool_b=d_pb.reshape(1, len(POOL_WINDOWS), POOL_GROUP), pool_scale=d_pscale, w_up=(gf0[1], gf1[1]),
        ffn_conv_w=layers(2).transpose(0, 2, 1, 3).reshape(DEPTH, FFN_CONV_WIDTH, 2 * D_FF), w_down=(gf0[4], gf1[4]))
    send_rest, token_rest, grads["little_slabs"] = _send_rest_grads(grads)
    dn0 = _mm(dproj, w_in_p, name="mm_dn0", tb=True, tk=384, after=token_rest)
    dh0, _, d_geven = _rms_bwd(h0, wt["g_even"], dn0, dh1, name="rms_bwd_even")
    grads.update(
        meta_tokens=dh0[0:N_META], mix_norm_even=d_geven, b_f=d_bf.reshape(1, FOX_HEADS), conv_b=d_convb, ln_g=d_lng,
        ln_b=d_lnb, ffn_norm=jnp.concatenate([gf0[0], gf1[0]], axis=0),
        ffn_conv_b=layers(3).reshape(DEPTH, 2 * D_FF), final_norm=d_gfinal.reshape(D_MODEL),
        sends=(send0, send1, send_rest))
    return loss, dh0[N_META:n_real], grads


_LITTLE = (("conv_w", (1, 31, 512), 2), ("mix_norm_odd", (1, 1024), 1), ("pool_b", (1, 4, 256), 2),
           ("pool_scale", (1, 1024), 1), ("ffn_conv_w", (2, 3, 5632), 2))


def _send_rest_grads(g):
    out_rows, pool_rows, groups = D_MODEL // N_DEV, POOL_GROUP // N_DEV, len(POOL_WINDOWS)
    little_slabs = _pack([_full_to_slabs(g[n], s, a) for n, s, a in _LITTLE], F32, lead=(N_DEV,), align=8)
    land = lambda shape, dtype: jax.ShapeDtypeStruct((N_DEV - 1,) + shape, dtype)
    handle, token = _send_start(
        [g["w_in"], g["w_out"], g["pool_w"], little_slabs],
        [land((D_MODEL, _IN_SHARD), BF16), land((out_rows, D_MODEL), BF16), land((groups, pool_rows, POOL_GROUP), BF16),
         land(little_slabs.shape[1:], F32)],
        [(0, _by_owner, 0, None), (1, _row_block(out_rows), 1, None), (2, _row_block(pool_rows, axis=1), 2, None),
         (3, _by_owner, 3, None)],
        name="send_rest")
    return handle, token, little_slabs


def _send_ffn_grads(dw_up, dw_down, tag):
    rows = D_FF // N_DEV
    lands = [jax.ShapeDtypeStruct((N_DEV - 1,) + dw_up.shape[1:], BF16),
             jax.ShapeDtypeStruct((N_DEV - 1, rows, D_MODEL), BF16)]
    return _send_start([dw_up, dw_down], lands, [(0, _by_owner, 0, None), (1, _row_block(rows), 1, None)],
                       name=f"send_ffn{tag}")


_QKV = 3 * FOX_WIDTH
_GLU0 = _QKV + FOX_HEADS
_IN_COLS = _GLU0 + 2 * CONV_CH
_F_PAD = 128


_IN_SHARD = _IN_COLS // N_DEV
_UP_SHARD = 2 * D_FF // N_DEV
_ROW_TILE = 256


def _assemble_w_in(st, *, name):
    tr = _ROW_TILE

    def body(s_ref, o_ref):
        full = jnp.concatenate([s_ref[i].astype(F32) for i in range(N_DEV)], axis=1)
        parts = [full[:, :_QKV], full[:, _GLU0:], full[:, _QKV:_GLU0], jnp.zeros((tr, _F_PAD - FOX_HEADS), F32)]
        o_ref[...] = jnp.concatenate(parts, axis=1).astype(BF16)

    return pl.pallas_call(
        body, name=name, grid=(D_MODEL // tr,),
        in_specs=[pl.BlockSpec((N_DEV, tr, _IN_SHARD), lambda i: (0, i, 0))],
        out_specs=pl.BlockSpec((tr, _QKV + 2 * CONV_CH + _F_PAD), lambda i: (i, 0)),
        out_shape=jax.ShapeDtypeStruct((D_MODEL, _QKV + 2 * CONV_CH + _F_PAD), BF16),
    )(st)


def _assemble_w_up(st, *, name):
    tr = _ROW_TILE

    def body(s_ref, o_ref):
        o_ref[...] = jnp.concatenate([s_ref[0].astype(F32), s_ref[1].astype(F32)], axis=1).astype(BF16)

    return pl.pallas_call(
        body, name=name, grid=(D_MODEL // tr, N_DEV // 2),
        in_specs=[pl.BlockSpec((2, tr, _UP_SHARD), lambda r, p: (p, r, 0))],
        out_specs=pl.BlockSpec((tr, 2 * _UP_SHARD), lambda r, p: (r, p)),
        out_shape=jax.ShapeDtypeStruct((D_MODEL, 2 * D_FF), BF16),
    )(st)


def _mm_dw_in(n_t, dproj, *, name):
    dm, lp = n_t.shape
    tr = _ROW_TILE
    ag0 = _QKV + 2 * CONV_CH

    def body(a_ref, b_ref, o_ref):
        r = jnp.dot(a_ref[...], b_ref[...], preferred_element_type=F32)
        full = jnp.concatenate([r[:, :_QKV], r[:, ag0:ag0 + FOX_HEADS], r[:, _QKV:ag0]], axis=1)
        for i in range(N_DEV):
            o_ref[i] = full[:, i * _IN_SHARD:(i + 1) * _IN_SHARD].astype(BF16)

    return pl.pallas_call(
        body, name=name, grid=(dm // tr,),
        in_specs=[pl.BlockSpec((tr, lp), lambda i: (i, 0)), pl.BlockSpec(dproj.shape, lambda i: (0, 0))],
        out_specs=pl.BlockSpec((N_DEV, tr, _IN_SHARD), lambda i: (0, i, 0)),
        out_shape=jax.ShapeDtypeStruct((N_DEV, dm, _IN_SHARD), BF16),
    )(n_t, dproj)


def _mm_dw_up(n_t, dup2, *, name):
    dm, lp = n_t.shape
    pairs_per_half = D_FF // (2 * _UP_SHARD)

    def body(a_ref, b_ref, o_ref):
        r = jnp.dot(a_ref[...], b_ref[...], preferred_element_type=F32)
        o_ref[0] = r[:, :_UP_SHARD].astype(BF16)
        o_ref[1] = r[:, _UP_SHARD:].astype(BF16)

    return pl.pallas_call(
        body, name=name, grid=(N_DEV // 2,),
        in_specs=[pl.BlockSpec((dm, lp), lambda p: (0, 0)),
                  pl.BlockSpec((lp, 2 * _UP_SHARD), lambda p: (p // pairs_per_half, p % pairs_per_half))],
        out_specs=pl.BlockSpec((2, dm, _UP_SHARD), lambda p: (p, 0, 0)),
        out_shape=jax.ShapeDtypeStruct((N_DEV, dm, _UP_SHARD), BF16),
    )(n_t, dup2)


MESH = pl.DeviceIdType.MESH
ANY = pl.BlockSpec(memory_space=pl.ANY)


def _slot(px, py, pc):
    return 4 * px + 2 * py + pc


def _by_owner(ref, slot):
    return ref.at[slot]


def _row_block(rows, axis=0):
    def place(ref, slot):
        idx = (slice(None),) * axis + (pl.ds(slot * rows, rows),)
        return ref.at[idx]
    return place


def _all_gather(arrs, out_shapes, places, *, name):
    n = len(arrs)

    def body(*refs):
        ins, outs = refs[:n], refs[n:2 * n]
        send_sems, recv_sems, local_sems = refs[2 * n:]
        x, y, c = lax.axis_index("x"), lax.axis_index("y"), lax.axis_index("c")
        me, sibling = (x, y, c), (x, y, 1 - c)
        chips = [(1 - x, y), (x, 1 - y), (1 - x, 1 - y)]

        def copy(a, k, block, to, from_input=False):
            dst = places[a](outs[a], _slot(*block))
            return pltpu.make_async_remote_copy(
                src_ref=ins[a] if from_input else dst, dst_ref=dst,
                send_sem=send_sems.at[7 * a + k], recv_sem=recv_sems.at[7 * a + k],
                device_id=to, device_id_type=MESH)

        own, sent = [], []
        for a in range(n):
            mine = pltpu.make_async_copy(ins[a], places[a](outs[a], _slot(*me)), local_sems.at[a])
            mine.start()
            own.append(mine)
            first = [copy(a, 0, me, sibling, True)]
            first += [copy(a, 1 + j, me, (*chip, c), True) for j, chip in enumerate(chips)]
            for cp in first:
                cp.start()
            sent += first
        for a in range(n):
            for j, chip in enumerate(chips):
                copy(a, 1 + j, (*chip, c), me).wait_recv()
                passed = copy(a, 4 + j, (*chip, c), sibling)
                passed.start()
                sent.append(passed)
        for a in range(n):
            copy(a, 0, sibling, me).wait_recv()
            for j, chip in enumerate(chips):
                copy(a, 4 + j, (*chip, 1 - c), me).wait_recv()
        for cp in sent:
            cp.wait_send()
        for cp in own:
            cp.wait()

    return pl.pallas_call(
        body, name=name,
        in_specs=[ANY] * n, out_specs=[ANY] * n,
        out_shape=[jax.ShapeDtypeStruct(s, a.dtype) for s, a in zip(out_shapes, arrs)],
        scratch_shapes=[pltpu.SemaphoreType.DMA((7 * n,)), pltpu.SemaphoreType.DMA((7 * n,)),
                        pltpu.SemaphoreType.DMA((n,))],
    )(*arrs)


def _exchange(srcs, recv_structs, copies, *, name):
    ns, nc = len(srcs), len(copies)

    def body(*refs):
        src_refs, recv_refs = refs[:ns], refs[ns:ns + len(recv_structs)]
        send_sems, recv_sems, local_sems = refs[ns + len(recv_structs):]
        x, y, c = lax.axis_index("x"), lax.axis_index("y"), lax.axis_index("c")
        me = _slot(x, y, c)

        def peer(k):
            flip = lambda v, bit: 1 - v if bit else v
            return flip(x, k & 4), flip(y, k & 2), flip(c, k & 1)

        def copy(k, j, arriving):
            si, take, ri, put = copies[j]
            p = _slot(*peer(k))
            sem = (k - 1) * nc + j
            return pltpu.make_async_remote_copy(
                src_ref=take(src_refs[si], p), dst_ref=put(recv_refs[ri], p if arriving else me),
                send_sem=send_sems.at[sem], recv_sem=recv_sems.at[sem], device_id=peer(k), device_id_type=MESH)

        own = [pltpu.make_async_copy(take(src_refs[si], me), put(recv_refs[ri], me), local_sems.at[j])
               for j, (si, take, ri, put) in enumerate(copies)]
        for cp in own:
            cp.start()
        sent = [copy(k, j, False) for k in range(1, N_DEV) for j in range(nc)]
        for cp in sent:
            cp.start()
        for k in range(1, N_DEV):
            for j in range(nc):
                copy(k, j, True).wait_recv()
        for cp in sent:
            cp.wait_send()
        for cp in own:
            cp.wait()

    return pl.pallas_call(
        body, name=name,
        in_specs=[ANY] * ns, out_specs=[ANY] * len(recv_structs), out_shape=list(recv_structs),
        scratch_shapes=[pltpu.SemaphoreType.DMA((7 * nc,)), pltpu.SemaphoreType.DMA((7 * nc,)),
                        pltpu.SemaphoreType.DMA((nc,))],
    )(*srcs)


HBM = pl.BlockSpec(memory_space=pltpu.HBM)
SEM = pl.BlockSpec(memory_space=pltpu.SEMAPHORE)
EFFECT = pltpu.SideEffectType.DATAFLOW_SIDE_EFFECTING


def _relation_copies(src_refs, land_refs, copies, send_sems, recv_sems):
    x, y, c = lax.axis_index("x"), lax.axis_index("y"), lax.axis_index("c")
    flip = lambda v, bit: 1 - v if bit else v
    out = []
    for k in range(1, N_DEV):
        p = (flip(x, k & 4), flip(y, k & 2), flip(c, k & 1))
        for j, (si, take, li, put) in enumerate(copies):
            sem = (k - 1) * len(copies) + j
            dst = land_refs[li].at[k - 1] if put is None else put(land_refs[li], _slot(x, y, c))
            out.append(pltpu.make_async_remote_copy(
                src_ref=take(src_refs[si], _slot(*p)), dst_ref=dst,
                send_sem=send_sems.at[sem], recv_sem=recv_sems.at[sem], device_id=p, device_id_type=MESH))
    return out


def _send_start(srcs, land_structs, copies, *, name, after=None):
    ns, nl = len(srcs), len(land_structs)
    n_sem = (N_DEV - 1) * len(copies)
    behind = [] if after is None else [after]

    def body(*refs):
        first_out = ns + nl + len(behind)
        send_sems, recv_sems, token = refs[first_out], refs[first_out + 1], refs[-1]
        for cp in _relation_copies(refs[:ns], refs[ns:ns + nl], copies, send_sems, recv_sems):
            cp.start()
        token[...] = jnp.zeros_like(token)

    in_hbm = lambda a: pltpu.with_memory_space_constraint(a, pltpu.HBM)
    outs = pl.pallas_call(
        body, name=name,
        out_shape=(pltpu.SemaphoreType.DMA((n_sem,)), pltpu.SemaphoreType.DMA((n_sem,)),
                   *[pltpu.HBM(s.shape, s.dtype) for s in srcs],
                   *[pltpu.HBM(s.shape, s.dtype) for s in land_structs],
                   jax.ShapeDtypeStruct((8, 128), F32)),
        in_specs=(HBM,) * (ns + nl) + (ANY,) * len(behind),
        out_specs=(SEM, SEM) + (HBM,) * (ns + nl) + (pl.BlockSpec(memory_space=pltpu.VMEM),),
        input_output_aliases={i: 2 + i for i in range(ns + nl)},
        compiler_params=pltpu.CompilerParams(has_side_effects=EFFECT),
    )(*[in_hbm(s) for s in srcs], *[in_hbm(lax.empty(s.shape, s.dtype)) for s in land_structs], *behind)
    return (outs[0], outs[1], outs[2:2 + ns], outs[2 + ns:2 + ns + nl], copies), outs[-1]


def _send_wait(handle, after, *, name):
    send_sems, recv_sems, srcs, lands, copies = handle
    ns, nl = len(srcs), len(lands)

    def body(*refs):
        for cp in _relation_copies(refs[:ns], refs[ns:ns + nl], copies, refs[ns + nl], refs[ns + nl + 1]):
            cp.wait_send()
            cp.wait_recv()

    outs = pl.pallas_call(
        body, name=name,
        out_shape=tuple(pltpu.HBM(a.shape, a.dtype) for a in (*srcs, *lands)),
        in_specs=(HBM,) * (ns + nl) + (SEM, SEM, ANY), out_specs=(HBM,) * (ns + nl),
        input_output_aliases={i: i for i in range(ns + nl)},
        compiler_params=pltpu.CompilerParams(has_side_effects=EFFECT),
    )(*srcs, *lands, send_sems, recv_sems, after)
    return outs[ns:]


def _sum_slabs(stack, *, name, own=None):
    n, rows, w = stack.shape

    def body(*refs):
        s_ref, o_ref = refs[-2], refs[-1]
        acc = s_ref[0] if own is None else refs[0][...] + s_ref[0]
        for i in range(1, n):
            acc = acc + s_ref[i]
        o_ref[...] = acc

    return pl.pallas_call(body, name=name, out_shape=jax.ShapeDtypeStruct((rows, w), F32))(
        *([] if own is None else [own]), stack)


def _adam_math(w, g, m, v):
    mn = ADAM_B1 * m + (1.0 - ADAM_B1) * g
    vn = ADAM_B2 * v + (1.0 - ADAM_B2) * (g * g)
    m_hat = mn / (1.0 - ADAM_B1 ** ADAM_STEP)
    v_hat = vn / (1.0 - ADAM_B2 ** ADAM_STEP)
    return -ADAM_LR * (m_hat / (jnp.sqrt(v_hat) + ADAM_EPS) + ADAM_WD * w), mn, vn


def _adamw(w, g, m, v, *, name):
    def body(w_ref, g_ref, m_ref, v_ref, d_ref, mo_ref, vo_ref):
        d_ref[...], mo_ref[...], vo_ref[...] = _adam_math(w_ref[...], g_ref[...], m_ref[...], v_ref[...])

    return pl.pallas_call(body, name=name, out_shape=[jax.ShapeDtypeStruct(w.shape, F32)] * 3)(w, g, m, v)


def _adamw_layers(w, owns, lands, m, v, tr, *, name):
    nl, rows, cols = w.shape
    steps = rows // tr
    assert rows % tr == 0

    def body(*refs):
        w_ref, m_ref, v_ref = refs[:3]
        own_refs, land_refs = refs[3:3 + nl], refs[3 + nl:3 + 2 * nl]
        g_ref, d_ref, mo_ref, vo_ref = refs[3 + 2 * nl:]
        for li in range(nl):
            @pl.when(pl.program_id(0) == li)
            def _(li=li):
                g = own_refs[li][...].astype(F32)
                for k in range(N_DEV - 1):
                    g = g + land_refs[li][k].astype(F32)
                g_ref[...] = g
                d_ref[...], mo_ref[...], vo_ref[...] = _adam_math(w_ref[...], g, m_ref[...], v_ref[...])

    def held(li):
        return lambda l, i: jnp.where(l == li, i, jnp.where(l < li, 0, steps - 1))

    blk = pl.BlockSpec((None, tr, cols), lambda l, i: (l, i, 0))
    own_specs = [pl.BlockSpec((tr, cols), lambda l, i, f=held(li): (f(l, i), 0)) for li in range(nl)]
    land_specs = [pl.BlockSpec((N_DEV - 1, tr, cols), lambda l, i, f=held(li): (0, f(l, i), 0)) for li in range(nl)]
    return pl.pallas_call(
        body, name=name, grid=(nl, steps),
        in_specs=[blk, blk, blk] + own_specs + land_specs, out_specs=[blk] * 4,
        out_shape=[jax.ShapeDtypeStruct(w.shape, F32)] * 4,
    )(w, m, v, *owns, *lands)


_WEIGHTS = (
    ("meta_tokens", (16, 1024), 1), ("mix_norm_even", (1, 1024), None), ("w_in", (1, 1024, 2568), 2),
    ("b_f", (1, 8), None), ("conv_w", (1, 31, 512), 2), ("conv_b", (1, 512), None), ("ln_g", (1, 512), None),
    ("ln_b", (1, 512), None), ("w_out", (1, 1024, 1024), 1), ("mix_norm_odd", (1, 1024), 1),
    ("pool_w", (1, 4, 256, 256), 2), ("pool_b", (1, 4, 256), 2), ("pool_scale", (1, 1024), 1),
    ("ffn_norm", (2, 1024), None), ("w_up", (2, 1024, 5632), 2), ("ffn_conv_w", (2, 3, 5632), 2),
    ("ffn_conv_b", (2, 5632), None), ("w_down", (2, 2816, 1024), 1), ("final_norm", (1024,), None),
)
_MATMUL_WEIGHTS = ("w_in", "w_out", "pool_w", "w_up", "w_down")
_ADAM_ROWS = dict(w_in=256, w_out=128, pool_w=128, w_up=256, w_down=352)


def _shard_shape(shape, axis):
    return shape[:axis] + (shape[axis] // N_DEV,) + shape[axis + 1:]


def _size(shape):
    n = 1
    for s in shape:
        n *= s
    return n


def _pack(parts, dtype, lead=(), align=16):
    flat = jnp.concatenate([p.reshape(lead + (-1,)).astype(dtype) for p in parts], axis=-1)
    n = flat.shape[-1]
    rows = _round_up(-(-n // FLAT_W), align)
    flat = jnp.pad(flat, [(0, 0)] * len(lead) + [(0, rows * FLAT_W - n)])
    return flat.reshape(lead + (rows, FLAT_W))


def _unpack(buf, shapes, lead=()):
    flat = buf.reshape(lead + (-1,))
    out, off = [], 0
    for shp in shapes:
        n = _size(shp)
        out.append(flat[..., off:off + n].reshape(lead + shp))
        off += n
    return out


def _gathered_to_full(stack, shape, axis):
    return jnp.moveaxis(stack, 0, axis).reshape(shape)


def _full_to_slabs(full, shape, axis):
    split = shape[:axis] + (N_DEV, shape[axis] // N_DEV) + shape[axis + 1:]
    return jnp.moveaxis(full.reshape(split), axis, 0)


def kernel(x, meta_tokens, mix_norm_even, w_in, b_f, conv_w, conv_b, ln_g, ln_b, w_out, mix_norm_odd, pool_w, pool_b, pool_scale, ffn_norm, w_up, ffn_conv_w, ffn_conv_b, w_down, final_norm, loss_target, m_meta_tokens, m_mix_norm_even, m_w_in, m_b_f, m_conv_w, m_conv_b, m_ln_g, m_ln_b, m_w_out, m_mix_norm_odd, m_pool_w, m_pool_b, m_pool_scale, m_ffn_norm, m_w_up, m_ffn_conv_w, m_ffn_conv_b, m_w_down, m_final_norm, v_meta_tokens, v_mix_norm_even, v_w_in, v_b_f, v_conv_w, v_conv_b, v_ln_g, v_ln_b, v_w_out, v_mix_norm_odd, v_pool_w, v_pool_b, v_pool_scale, v_ffn_norm, v_w_up, v_ffn_conv_w, v_ffn_conv_b, v_w_down, v_final_norm):
    names = [n for n, _, _ in _WEIGHTS]
    w_loc = dict(zip(names, (meta_tokens, mix_norm_even, w_in, b_f, conv_w, conv_b, ln_g, ln_b, w_out, mix_norm_odd,
                             pool_w, pool_b, pool_scale, ffn_norm, w_up, ffn_conv_w, ffn_conv_b, w_down, final_norm)))
    m_loc = dict(zip(names, (m_meta_tokens, m_mix_norm_even, m_w_in, m_b_f, m_conv_w, m_conv_b, m_ln_g, m_ln_b,
                             m_w_out, m_mix_norm_odd, m_pool_w, m_pool_b, m_pool_scale, m_ffn_norm, m_w_up,
                             m_ffn_conv_w, m_ffn_conv_b, m_w_down, m_final_norm)))
    v_loc = dict(zip(names, (v_meta_tokens, v_mix_norm_even, v_w_in, v_b_f, v_conv_w, v_conv_b, v_ln_g, v_ln_b,
                             v_w_out, v_mix_norm_odd, v_pool_w, v_pool_b, v_pool_scale, v_ffn_norm, v_w_up,
                             v_ffn_conv_w, v_ffn_conv_b, v_w_down, v_final_norm)))
    replicated = [(n, s) for n, s, a in _WEIGHTS if a is None]
    little = [(n, s, a) for n, s, a in _WEIGHTS if a is not None and n not in _MATMUL_WEIGHTS]
    little_shards = [_shard_shape(s, a) for _, s, a in little]
    out_rows, down_rows, pool_rows = D_MODEL // N_DEV, D_FF // N_DEV, POOL_GROUP // N_DEV
    n_groups = len(POOL_WINDOWS)

    little_pack = _pack([w_loc[n] for n, _, _ in little], F32)
    g_win, g_wout, g_poolw, g_little = _all_gather(
        [w_in[0].astype(BF16), w_out[0].astype(BF16), pool_w[0].astype(BF16), little_pack],
        [(N_DEV, D_MODEL, _IN_SHARD), (D_MODEL, D_MODEL), (n_groups, POOL_GROUP, POOL_GROUP),
         (N_DEV,) + little_pack.shape],
        [_by_owner, _row_block(out_rows), _row_block(pool_rows, axis=1), _by_owner],
        name="gather_weights")
    me = _slot(lax.axis_index("x"), lax.axis_index("y"), lax.axis_index("c"))
    w_up_b, w_down_b = w_up.astype(BF16), w_down.astype(BF16)
    whole = lambda ref, slot: ref
    ffn_lands = [jax.ShapeDtypeStruct((N_DEV, D_MODEL, _UP_SHARD), BF16), jax.ShapeDtypeStruct((D_FF, D_MODEL), BF16)]
    ffn_gathers, behind = [], g_little
    for l in range(DEPTH):
        handle, behind = _send_start([w_up_b[l], w_down_b[l]], ffn_lands,
                                     [(0, whole, 0, _by_owner), (1, whole, 1, _row_block(down_rows))],
                                     name=f"gather_ffn{l}_start", after=behind)
        ffn_gathers.append(handle)

    def ffn_weights(l, after):
        up_by_owner, down = _send_wait(ffn_gathers[l], after, name=f"gather_ffn{l}_wait")
        up_by_owner = lax.dynamic_update_index_in_dim(up_by_owner, w_up_b[l], me, 0)
        down = lax.dynamic_update_slice_in_dim(down, w_down_b[l], me * down_rows, 0)
        return _assemble_w_up(up_by_owner, name=f"assemble_w_up{l}"), down

    w_in_p = _assemble_w_in(g_win, name="assemble_w_in")
    full = {n: _gathered_to_full(st, s, a)
            for (n, s, a), st in zip(little, _unpack(g_little, little_shards, lead=(N_DEV,)))}
    f0 = _QKV + 2 * CONV_CH
    wt = dict(
        meta=full["meta_tokens"], g_even=mix_norm_even, w_in_p=w_in_p, wf_t=w_in_p[:, f0:f0 + FOX_HEADS].T,
        b_f=b_f.reshape(FOX_HEADS, 1), conv_w=full["conv_w"][0], conv_b=conv_b, ln_g=ln_g, ln_b=ln_b, w_out=g_wout,
        g_odd=full["mix_norm_odd"], pool_w=g_poolw, pool_b=full["pool_b"].reshape(1, D_MODEL),
        pool_scale=full["pool_scale"], ffn_norm=ffn_norm, ffn_weights=ffn_weights, ffn_started=behind,
        fcw3=full["ffn_conv_w"].reshape(DEPTH, FFN_CONV_WIDTH, 2, D_FF).transpose(0, 2, 1, 3),
        fcb3=ffn_conv_b.reshape(DEPTH, 2, 1, D_FF), g_final=final_norm.reshape(1, D_MODEL))

    loss_part, grad_x, g = _local_step(x[0], loss_target[0], wt)

    small = _pack([loss_part[:, 0:1]] + [g[n] for n, _ in replicated] + [g["meta_tokens"]], F32, align=8)
    (everyone,) = _exchange([small], [jax.ShapeDtypeStruct((N_DEV,) + small.shape, F32)],
                            [(0, lambda ref, slot: ref, 0, _by_owner)], name="exchange_small")
    summed = _unpack(_sum_slabs(everyone, name="sum_small"),
                     [(1, 1)] + [s for _, s in replicated] + [(N_META, D_MODEL)])
    loss = summed[0].reshape(())
    grads = {n: gr for (n, _), gr in zip(replicated, summed[1:-1])}
    grads["meta_tokens"] = lax.dynamic_slice_in_dim(summed[-1], me * out_rows, out_rows, 1)

    delta, new_m, new_v = {}, {}, {}
    send0, send1, send_rest = g["sends"]
    ffn_lands = [_send_wait(send, everyone, name=f"wait_ffn{l}") for l, send in enumerate((send0, send1))]
    own_up = [lax.dynamic_index_in_dim(d, me, 0, keepdims=False) for d in g["w_up"]]
    own_down = [lax.dynamic_slice_in_dim(d, me * down_rows, down_rows, 0) for d in g["w_down"]]
    for n, owns, idx in (("w_up", own_up, 0), ("w_down", own_down, 1)):
        grads[n], delta[n], new_m[n], new_v[n] = _adamw_layers(
            w_loc[n], owns, [ffn_lands[l][idx] for l in range(DEPTH)], m_loc[n], v_loc[n], _ADAM_ROWS[n],
            name=f"adamw_{n}")
    land_in, land_out, land_pool, land_little = _send_wait(send_rest, delta["w_down"], name="wait_rest")
    pool_2d = (n_groups * pool_rows, POOL_GROUP)
    own_pool = lax.dynamic_slice_in_dim(g["pool_w"], me * pool_rows, pool_rows, 1)
    for n, own, land, shp in (
            ("w_in", lax.dynamic_index_in_dim(g["w_in"], me, 0, keepdims=False), land_in, w_in.shape),
            ("w_out", lax.dynamic_slice_in_dim(g["w_out"], me * out_rows, out_rows, 0), land_out, w_out.shape),
            ("pool_w", own_pool.reshape(pool_2d), land_pool.reshape((N_DEV - 1,) + pool_2d), (1,) + pool_2d)):
        outs = _adamw_layers(w_loc[n].reshape(shp), [own], [land], m_loc[n].reshape(shp), v_loc[n].reshape(shp),
                             _ADAM_ROWS[n], name=f"adamw_{n}")
        grads[n], delta[n], new_m[n], new_v[n] = (o.reshape(w_loc[n].shape) for o in outs)
    own_little = lax.dynamic_index_in_dim(g["little_slabs"], me, 0, keepdims=False)
    g_little = _unpack(_sum_slabs(land_little, own=own_little, name="sum_little"),
                       [_shard_shape(s, a) for _, s, a in _LITTLE])
    grads.update({n: gl for (n, _, _), gl in zip(_LITTLE, g_little)})
    two_d = lambda shp: (_size(shp[:-1]), shp[-1])
    for n in names:
        if n in _MATMUL_WEIGHTS:
            continue
        shp = w_loc[n].shape
        d, mn, vn = _adamw(w_loc[n].reshape(two_d(shp)), grads[n].reshape(two_d(shp)), m_loc[n].reshape(two_d(shp)),
                           v_loc[n].reshape(two_d(shp)), name=f"adamw_{n}")
        delta[n], new_m[n], new_v[n] = d.reshape(shp), mn.reshape(shp), vn.reshape(shp)
    return (loss, grad_x[None], *[grads[n] for n in names], *[delta[n] for n in names],
            *[new_m[n] for n in names], *[new_v[n] for n in names])
```

```python
import functools

import jax
import jax.numpy as jnp
from jax import lax
from jax.experimental import pallas as pl
from jax.experimental.pallas import tpu as pltpu

F32 = jnp.float32
BF16 = jnp.bfloat16

N_DEV = 8
DEPTH = 2
D_MODEL = 1024
N_META = 16
FOX_HEADS = 8
FOX_HEAD_DIM = 64
FOX_WIDTH = 512
CONV_CH = 512
CONV_WIDTH = 31
POOL_WINDOWS = (2, 4, 8, 16)
POOL_GROUP = 256
D_FF = 2816
FFN_CONV_WIDTH = 3
RMS_EPS = 1e-6
LN_EPS = 1e-5
ADAM_LR = 0.001
ADAM_B1 = 0.9
ADAM_B2 = 0.999
ADAM_EPS = 1e-08
ADAM_WD = 0.01
ADAM_STEP = 10

CHUNK = 128
HALO = 32
NEG_BIG = -1e30
FLAT_W = 1024


def _round_up(n, m):
    return (n + m - 1) // m * m


def _sigmoid(x):
    return 1.0 / (1.0 + jnp.exp(-x))


def _fold8(p):
    acc = p[0:8, :]
    for r in range(1, p.shape[0] // 8):
        acc = acc + p[8 * r:8 * r + 8, :]
    return acc


def _mm(a, b, *, name, tb=False, tm=None, tn=None, tk=None, out_dtype=F32, res=None,
        a_map=None, b_map=None, o_map=None, out_shape=None, dims=None, after=None):
    if dims is None:
        m, k = a.shape
        n = b.shape[-2] if tb else b.shape[-1]
    else:
        m, n, k = dims
    tm, tn, tk = tm or m, tn or n, tk or k
    assert m % tm == 0 and n % tn == 0 and k % tk == 0, (name, m, n, k, tm, tn, tk)
    nk = k // tk
    a_map = a_map or (lambda i, j, kk: (i, kk))
    b_map = b_map or ((lambda i, j, kk: (j, kk)) if tb else (lambda i, j, kk: (kk, j)))
    o_map = o_map or (lambda i, j, kk: (i, j))
    out_shape = out_shape or (m, n)
    contract = (((1,), (1,)), ((), ())) if tb else (((1,), (0,)), ((), ()))
    has_res = res is not None

    def body(*refs):
        a_ref, b_ref = refs[0], refs[1]
        res_ref = refs[2] if has_res else None
        o_ref = refs[2 + has_res + (after is not None)]
        p = lax.dot_general(a_ref[...], b_ref[...], contract, preferred_element_type=F32)
        if nk == 1:
            if has_res:
                p = p + res_ref[...]
            o_ref[...] = p.astype(o_ref.dtype)
        else:
            acc_ref = refs[-1]
            kk = pl.program_id(2)

            @pl.when(kk == 0)
            def _():
                acc_ref[...] = p

            @pl.when(kk > 0)
            def _():
                acc_ref[...] += p

            @pl.when(kk == nk - 1)
            def _():
                r = acc_ref[...]
                if has_res:
                    r = r + res_ref[...]
                o_ref[...] = r.astype(o_ref.dtype)

    in_specs = [pl.BlockSpec((tm, tk), a_map), pl.BlockSpec((tn, tk) if tb else (tk, tn), b_map)]
    operands = [a, b]
    if has_res:
        in_specs.append(pl.BlockSpec((tm, tn), o_map))
        operands.append(res)
    if after is not None:
        in_specs.append(pl.BlockSpec(memory_space=pl.ANY))
        operands.append(after)
    return pl.pallas_call(
        body, name=name, grid=(m // tm, n // tn, nk),
        in_specs=in_specs, out_specs=pl.BlockSpec((tm, tn), o_map),
        out_shape=jax.ShapeDtypeStruct(out_shape, out_dtype),
        scratch_shapes=[pltpu.VMEM((tm, tn), F32)] if nk > 1 else [],
    )(*operands)


def _transpose(x, *, name, out_dtype, cols=None, after=None):
    r, c = x.shape
    cols = cols or c
    assert r % CHUNK == 0
    behind = [] if after is None else [after]

    def body(x_ref, *rest):
        o_ref = rest[-1]
        o_ref[...] = x_ref[...].astype(F32).T.astype(o_ref.dtype)

    return pl.pallas_call(
        body, name=name, grid=(r // CHUNK,),
        in_specs=[pl.BlockSpec((CHUNK, cols), lambda i: (i, 0))] + [pl.BlockSpec(memory_space=pl.ANY)] * len(behind),
        out_specs=pl.BlockSpec((cols, CHUNK), lambda i: (0, i)),
        out_shape=jax.ShapeDtypeStruct((cols, r), out_dtype),
    )(x, *behind)


def _rms_fwd(x, g, *, name, out_dtype, after=None):
    lp, dm = x.shape
    tr = lp // 4
    behind = [] if after is None else [after]

    def body(x_ref, g_ref, *rest):
        o_ref = rest[-1]
        xv = x_ref[...]
        r = lax.rsqrt(jnp.mean(xv * xv, axis=-1, keepdims=True) + RMS_EPS)
        o_ref[...] = (xv * r * g_ref[...]).astype(o_ref.dtype)

    return pl.pallas_call(
        body, name=name, grid=(lp // tr,),
        in_specs=[pl.BlockSpec((tr, dm), lambda i: (i, 0)), pl.BlockSpec((1, dm), lambda i: (0, 0))]
        + [pl.BlockSpec(memory_space=pl.ANY)] * len(behind),
        out_specs=pl.BlockSpec((tr, dm), lambda i: (i, 0)),
        out_shape=jax.ShapeDtypeStruct((lp, dm), out_dtype),
    )(x, g, *behind)


def _rms_bwd(x, g, dn, dres, *, name):
    lp, dm = x.shape
    tr = lp // 4

    def body(x_ref, g_ref, dn_ref, dres_ref, dh_ref, dhb_ref, dg_ref):
        xv = x_ref[...]
        r = lax.rsqrt(jnp.mean(xv * xv, axis=-1, keepdims=True) + RMS_EPS)
        xhat = xv * r
        dnv = dn_ref[...]

        @pl.when(pl.program_id(0) == 0)
        def _():
            dg_ref[...] = jnp.zeros_like(dg_ref)

        dg_ref[...] += jnp.sum(dnv * xhat, axis=0, keepdims=True)
        dxhat = dnv * g_ref[...]
        dx = r * (dxhat - xhat * jnp.mean(dxhat * xhat, axis=-1, keepdims=True))
        dh = dres_ref[...] + dx
        dh_ref[...] = dh
        dhb_ref[...] = dh.astype(BF16)

    row = pl.BlockSpec((tr, dm), lambda i: (i, 0))
    vec = pl.BlockSpec((1, dm), lambda i: (0, 0))
    return pl.pallas_call(
        body, name=name, grid=(lp // tr,),
        in_specs=[row, vec, row, row], out_specs=[row, row, vec],
        out_shape=[jax.ShapeDtypeStruct((lp, dm), F32), jax.ShapeDtypeStruct((lp, dm), BF16),
                   jax.ShapeDtypeStruct((1, dm), F32)],
    )(x, g, dn, dres)


def _loss_head(h, g, tgt, n_real, *, name):
    lp, dm = h.shape
    tr = lp // 4

    def body(x_ref, g_ref, t_ref, loss_ref, dh_ref, dhb_ref, dg_ref):
        i = pl.program_id(0)
        xv = x_ref[...]
        r = lax.rsqrt(jnp.mean(xv * xv, axis=-1, keepdims=True) + RMS_EPS)
        xhat = xv * r
        gv = g_ref[...]
        y = xhat * gv
        t = i * tr + lax.broadcasted_iota(jnp.int32, (tr, 1), 0)
        valid = (t >= N_META) & (t < n_real)
        diff = jnp.where(valid, y - t_ref[...], 0.0)

        @pl.when(i == 0)
        def _():
            loss_ref[...] = jnp.zeros_like(loss_ref)
            dg_ref[...] = jnp.zeros_like(dg_ref)

        row_sq = jnp.sum(diff * diff, axis=-1, keepdims=True) * (1.0 / dm)
        part = 0.5 * jnp.sum(row_sq, axis=0, keepdims=True)
        loss_ref[...] += jnp.broadcast_to(part, loss_ref.shape)
        dy = diff * (1.0 / dm)
        dg_ref[...] += jnp.sum(dy * xhat, axis=0, keepdims=True)
        dxhat = dy * gv
        dx = r * (dxhat - xhat * jnp.mean(dxhat * xhat, axis=-1, keepdims=True))
        dh_ref[...] = dx
        dhb_ref[...] = dx.astype(BF16)

    row = pl.BlockSpec((tr, dm), lambda i: (i, 0))
    vec = pl.BlockSpec((1, dm), lambda i: (0, 0))
    return pl.pallas_call(
        body, name=name, grid=(lp // tr,),
        in_specs=[row, vec, row],
        out_specs=[pl.BlockSpec((1, 128), lambda i: (0, 0)), row, row, vec],
        out_shape=[jax.ShapeDtypeStruct((1, 128), F32), jax.ShapeDtypeStruct((lp, dm), F32),
                   jax.ShapeDtypeStruct((lp, dm), BF16), jax.ShapeDtypeStruct((1, dm), F32)],
    )(h, g, tgt)


def _tri(upper):
    r = lax.broadcasted_iota(jnp.int32, (CHUNK, CHUNK), 0)
    c = lax.broadcasted_iota(jnp.int32, (CHUNK, CHUNK), 1)
    return jnp.where(r <= c if upper else r >= c, 1.0, 0.0).astype(F32)


def _fox_prep(f_t, b_f, *, name):
    nh, lp = f_t.shape
    nch = lp // CHUNK

    def body(f_ref, b_ref, c_ref):
        tri = _tri(True)
        carry = jnp.zeros((nh, 1), F32)
        for blk in range(nch):
            cols = slice(blk * CHUNK, (blk + 1) * CHUNK)
            z = f_ref[:, cols] + b_ref[...]
            logf = jnp.minimum(z, 0.0) - jnp.log(1.0 + jnp.exp(-jnp.abs(z)))
            cb = jnp.dot(logf, tri, preferred_element_type=F32, precision=lax.Precision.HIGHEST)
            c_ref[:, cols] = cb + carry
            carry = carry + jnp.sum(logf, axis=1, keepdims=True)

    return pl.pallas_call(
        body, name=name, out_shape=jax.ShapeDtypeStruct((nh, lp), F32),
    )(f_t, b_f)


def _fox_bwd(dc, f_t, b_f, *, name):
    nh, lp = f_t.shape
    nch = lp // CHUNK

    def body(dc_ref, f_ref, b_ref, df_ref, db_ref):
        tri = _tri(False)
        carry = jnp.zeros((nh, 1), F32)
        db = jnp.zeros((nh, 1), F32)
        df_ref[...] = jnp.zeros_like(df_ref)
        for blk in reversed(range(nch)):
            cols = slice(blk * CHUNK, (blk + 1) * CHUNK)
            dcb = dc_ref[:, cols]
            dlogf = jnp.dot(dcb, tri, preferred_element_type=F32, precision=lax.Precision.HIGHEST) + carry
            carry = carry + jnp.sum(dcb, axis=1, keepdims=True)
            z = f_ref[:, cols] + b_ref[...]
            dz = dlogf * _sigmoid(-z)
            df_ref[0:nh, cols] = dz
            db = db + jnp.sum(dz, axis=1, keepdims=True)
        db_ref[...] = db

    return pl.pallas_call(
        body, name=name,
        out_shape=[jax.ShapeDtypeStruct((128, lp), F32), jax.ShapeDtypeStruct((nh, 1), F32)],
    )(dc, f_t, b_f)


def _attn_blocks(lp):
    tq = lp // 4
    return tq, [(i * tq, min(lp, _round_up((i + 1) * tq, CHUNK))) for i in range(4)]


ATTN_SCALE = FOX_HEAD_DIM ** -0.5


def _attn_probs(q2s, k_h, c_row, row0, n):
    tq = q2s.shape[0]
    lo = row0 // CHUNK * CHUNK
    logits = []
    for c0, c1 in ([(0, lo)] if lo else []) + [(lo, n)]:
        s = lax.dot_general(q2s, k_h[c0:c1], (((1,), (1,)), ((), ())), preferred_element_type=F32) - c_row[:, c0:c1]
        if c1 > row0:
            t = row0 + lax.broadcasted_iota(jnp.int32, (tq, c1 - c0), 0)
            sidx = c0 + lax.broadcasted_iota(jnp.int32, (tq, c1 - c0), 1)
            s = jnp.where(sidx <= t, s, NEG_BIG)
        logits.append((s, c0, c1))
    m = functools.reduce(jnp.maximum, [jnp.max(s, axis=1, keepdims=True) for s, _, _ in logits])
    ps = [(jnp.exp(s - m), c0, c1) for s, c0, c1 in logits]
    inv = 1.0 / sum(jnp.sum(p, axis=1, keepdims=True) for p, _, _ in ps)
    return [(p * inv, c0, c1) for p, c0, c1 in ps]


def _attn_fwd(qkv, c3, *, name):
    lp = qkv.shape[0]
    tq, blocks = _attn_blocks(lp)

    def body(q_ref, k_ref, v_ref, c_ref, o_ref):
        lane = lax.broadcasted_iota(jnp.int32, (1, 128), 1)
        zero = jnp.zeros((), BF16)
        for i, (row0, n) in enumerate(blocks):
            q2s = q_ref[row0:row0 + tq, :] * ATTN_SCALE
            acc = jnp.zeros((tq, 128), F32)
            for hd in range(2):
                sel = (lane < 64) if hd == 0 else (lane >= 64)
                k_h = jnp.where(sel, k_ref[0:n, :], zero)
                v_h = jnp.where(sel, v_ref[0:n, :], zero)
                for p, c0, c1 in _attn_probs(q2s, k_h, c_ref[hd:hd + 1, 0:n], row0, n):
                    acc = acc + jnp.dot(p.astype(BF16), v_h[c0:c1], preferred_element_type=F32)
            o_ref[row0:row0 + tq, :] = acc.astype(BF16)

    blk = lambda off: pl.BlockSpec((lp, 128), lambda p: (0, off + p))
    return pl.pallas_call(
        body, name=name, grid=(4,),
        in_specs=[blk(0), blk(4), blk(8), pl.BlockSpec((None, 2, lp), lambda p: (p, 0, 0))],
        out_specs=pl.BlockSpec((lp, 128), lambda p: (0, p)),
        out_shape=jax.ShapeDtypeStruct((lp, FOX_WIDTH), BF16),
    )(qkv, qkv, qkv, c3)


def _attn_bwd(qkv, q_t, dcat, do_t, c3, *, name):
    lp = qkv.shape[0]
    tq, blocks = _attn_blocks(lp)
    scale = FOX_HEAD_DIM ** -0.5

    def body(q_ref, k_ref, v_ref, qt_ref, do_ref, dot_ref, c_ref, dq_ref, dkt_ref, dvt_ref, dc_ref,
             dkt_acc, dvt_acc):
        lane = lax.broadcasted_iota(jnp.int32, (1, 128), 1)
        sub = lax.broadcasted_iota(jnp.int32, (128, 1), 0)
        zero = jnp.zeros((), BF16)
        dkt_acc[...] = jnp.zeros_like(dkt_acc)
        dvt_acc[...] = jnp.zeros_like(dvt_acc)
        dc_ref[...] = jnp.zeros_like(dc_ref)
        for i, (row0, n) in enumerate(blocks):
            rows = slice(row0, row0 + tq)
            q2s = q_ref[rows, :] * ATTN_SCALE
            do2 = do_ref[rows, :].astype(BF16)
            dq_acc = jnp.zeros((tq, 128), F32)
            for hd in range(2):
                sel = (lane < 64) if hd == 0 else (lane >= 64)
                sel_t = (sub < 64) if hd == 0 else (sub >= 64)
                k_h = jnp.where(sel, k_ref[0:n, :], zero)
                v_h = jnp.where(sel, v_ref[0:n, :], zero)
                qt_h = jnp.where(sel_t, qt_ref[:, rows], zero)
                dot_h = jnp.where(sel_t, dot_ref[:, rows], zero)
                segs = [(p, lax.dot_general(do2, v_h[c0:c1], (((1,), (1,)), ((), ())), preferred_element_type=F32),
                         c0, c1) for p, c0, c1 in _attn_probs(q2s, k_h, c_ref[hd:hd + 1, 0:n], row0, n)]
                delta = sum(jnp.sum(p * dp, axis=1, keepdims=True) for p, dp, _, _ in segs)
                for p, dp, c0, c1 in segs:
                    ds = p * (dp - delta)
                    dsb = ds.astype(BF16)
                    dq_acc = dq_acc + jnp.dot(dsb, k_h[c0:c1], preferred_element_type=F32)
                    dkt_acc[:, c0:c1] += jnp.dot(qt_h, dsb, preferred_element_type=F32)
                    dvt_acc[:, c0:c1] += jnp.dot(dot_h, p.astype(BF16), preferred_element_type=F32)
                    dc_ref[hd:hd + 1, c0:c1] -= jnp.sum(ds, axis=0, keepdims=True)
            dq_ref[rows, :] = (dq_acc * scale).astype(BF16)
        dkt_ref[...] = (dkt_acc[...] * scale).astype(BF16)
        dvt_ref[...] = dvt_acc[...].astype(BF16)

    blk = lambda off: pl.BlockSpec((lp, 128), lambda p: (0, off + p))
    blk_t = pl.BlockSpec((128, lp), lambda p: (p, 0))
    c_spec = pl.BlockSpec((None, 2, lp), lambda p: (p, 0, 0))
    return pl.pallas_call(
        body, name=name, grid=(4,),
        in_specs=[blk(0), blk(4), blk(8), blk_t, blk(0), blk_t, c_spec],
        out_specs=[blk(0), blk_t, blk_t, c_spec],
        out_shape=[jax.ShapeDtypeStruct((lp, FOX_WIDTH), BF16), jax.ShapeDtypeStruct((FOX_WIDTH, lp), BF16),
                   jax.ShapeDtypeStruct((FOX_WIDTH, lp), BF16), jax.ShapeDtypeStruct((4, 2, lp), F32)],
        scratch_shapes=[pltpu.VMEM((128, lp), F32), pltpu.VMEM((128, lp), F32)],
    )(qkv, qkv, qkv, q_t, dcat, do_t, c3)


def _ln_stats(x):
    mu = jnp.mean(x, axis=-1, keepdims=True)
    xc = x - mu
    var = jnp.mean(xc * xc, axis=-1, keepdims=True)
    rstd = lax.rsqrt(var + LN_EPS)
    return xc * rstd, rstd


def _conv_fwd(agf, conv_w, conv_b, ln_g, ln_b, *, name):
    lp = agf.shape[0]
    nch = lp // CHUNK
    c = CONV_CH

    def body(a_ref, g_ref, w_ref, b_ref, lg_ref, lb_ref, u0_ref, u1_ref, u3_ref, u0s):
        u0s[0:HALO, :] = jnp.zeros((HALO, c), F32)

        def glu(ci, _):
            rows = pl.ds(pl.multiple_of(ci * CHUNK, CHUNK), CHUNK)
            u0 = a_ref[rows, :] * _sigmoid(g_ref[rows, :])
            u0_ref[rows, :] = u0
            u0s[pl.ds(pl.multiple_of(ci * CHUNK + HALO, 8), CHUNK), :] = u0
            return 0

        lax.fori_loop(0, nch, glu, 0)

        def conv(ci, _):
            r0 = pl.multiple_of(ci * CHUNK, CHUNK)
            rows = pl.ds(r0, CHUNK)
            for lg in range(c // 128):
                lanes = slice(lg * 128, (lg + 1) * 128)
                win = u0s[pl.ds(r0, CHUNK + HALO), lanes]
                acc = jnp.broadcast_to(b_ref[:, lanes], (CHUNK, 128))
                for k in range(CONV_WIDTH):
                    s = CONV_WIDTH - 1 - k
                    sh = win if s == 0 else pltpu.roll(win, s, 0)
                    acc = acc + w_ref[k:k + 1, lanes] * sh[HALO:HALO + CHUNK, :]
                u1_ref[rows, lanes] = acc
            xhat, _ = _ln_stats(u1_ref[rows, :])
            y = xhat * lg_ref[...] + lb_ref[...]
            u3_ref[rows, :] = (y * _sigmoid(y)).astype(BF16)
            return 0

        lax.fori_loop(0, nch, conv, 0)

    full = lambda shape: pl.BlockSpec(shape, lambda i: (0, 0))
    return pl.pallas_call(
        body, name=name, grid=(1,),
        in_specs=[pl.BlockSpec((lp, c), lambda i: (0, 0)), pl.BlockSpec((lp, c), lambda i: (0, 1)),
                  full((CONV_WIDTH, c)), full((1, c)), full((1, c)), full((1, c))],
        out_specs=[full((lp, c)), full((lp, c)), full((lp, c))],
        out_shape=[jax.ShapeDtypeStruct((lp, c), F32), jax.ShapeDtypeStruct((lp, c), F32),
                   jax.ShapeDtypeStruct((lp, c), BF16)],
        scratch_shapes=[pltpu.VMEM((lp + HALO, c), F32)],
    )(agf, agf, conv_w, conv_b, ln_g, ln_b)


def _conv_bwd(dcat, u0, u1, agf, conv_w, ln_g, ln_b, *, name):
    lp = agf.shape[0]
    nch = lp // CHUNK
    c = CONV_CH
    wlen = CHUNK + HALO

    def body(du3_ref, u0_ref, u1_ref, a_ref, g_ref, w_ref, lg_ref, lb_ref,
             dag_ref, dw_ref, db_ref, dlg_ref, dlb_ref, du1s, dwacc, vacc):
        du1s[lp:lp + HALO, :] = jnp.zeros((HALO, c), F32)
        dwacc[...] = jnp.zeros_like(dwacc)
        vacc[...] = jnp.zeros_like(vacc)

        def ln_bwd(ci, _):
            r0 = pl.multiple_of(ci * CHUNK, CHUNK)
            rows = pl.ds(r0, CHUNK)
            xhat, rstd = _ln_stats(u1_ref[rows, :])
            y = xhat * lg_ref[...] + lb_ref[...]
            sg = _sigmoid(y)
            du2 = du3_ref[rows, :] * (sg * (1.0 + y * (1.0 - sg)))
            vacc[0:8, :] += _fold8(du2 * xhat)
            vacc[8:16, :] += _fold8(du2)
            dxhat = du2 * lg_ref[...]
            du1 = rstd * (dxhat - jnp.mean(dxhat, axis=-1, keepdims=True)
                          - xhat * jnp.mean(dxhat * xhat, axis=-1, keepdims=True))
            vacc[16:24, :] += _fold8(du1)
            du1s[rows, :] = du1
            return 0

        lax.fori_loop(0, nch, ln_bwd, 0)

        def conv_bwd(ci, _):
            r0 = pl.multiple_of(ci * CHUNK, CHUNK)
            rows = pl.ds(r0, CHUNK)
            for lg in range(c // 128):
                lanes = slice(lg * 128, (lg + 1) * 128)
                dwin = du1s[pl.ds(r0, wlen), lanes]
                u0 = u0_ref[rows, lanes]
                acc = jnp.zeros((CHUNK, 128), F32)
                for k in range(CONV_WIDTH):
                    s = CONV_WIDTH - 1 - k
                    d_s = (dwin if s == 0 else pltpu.roll(dwin, wlen - s, 0))[0:CHUNK, :]
                    acc = acc + w_ref[k:k + 1, lanes] * d_s
                    dwacc[8 * k:8 * k + 8, lanes] += _fold8(d_s * u0)
                sg = _sigmoid(g_ref[rows, lanes])
                a = a_ref[rows, lanes]
                dag_ref[rows, lanes] = (acc * sg).astype(BF16)
                dag_ref[rows, slice(c + lg * 128, c + (lg + 1) * 128)] = (acc * a * sg * (1.0 - sg)).astype(BF16)
            return 0

        lax.fori_loop(0, nch, conv_bwd, 0)
        for k in range(CONV_WIDTH):
            dw_ref[k:k + 1, :] = jnp.sum(dwacc[8 * k:8 * k + 8, :], axis=0, keepdims=True)
        dlg_ref[...] = jnp.sum(vacc[0:8, :], axis=0, keepdims=True)
        dlb_ref[...] = jnp.sum(vacc[8:16, :], axis=0, keepdims=True)
        db_ref[...] = jnp.sum(vacc[16:24, :], axis=0, keepdims=True)

    full = lambda shape: pl.BlockSpec(shape, lambda i: (0, 0))
    vec = jax.ShapeDtypeStruct((1, c), F32)
    return pl.pallas_call(
        body, name=name, grid=(1,),
        in_specs=[pl.BlockSpec((lp, c), lambda i: (0, 1)), full((lp, c)), full((lp, c)),
                  pl.BlockSpec((lp, c), lambda i: (0, 0)), pl.BlockSpec((lp, c), lambda i: (0, 1)),
                  full((CONV_WIDTH, c)), full((1, c)), full((1, c))],
        out_specs=[full((lp, 2 * c)), full((CONV_WIDTH, c)), full((1, c)), full((1, c)), full((1, c))],
        out_shape=[jax.ShapeDtypeStruct((lp, 2 * c), BF16), jax.ShapeDtypeStruct((CONV_WIDTH, c), F32), vec, vec, vec],
        scratch_shapes=[pltpu.VMEM((lp + HALO, c), F32), pltpu.VMEM((8 * CONV_WIDTH, c), F32),
                        pltpu.VMEM((24, c), F32)],
    )(dcat, u0, u1, agf, agf, conv_w, ln_g, ln_b)


FFN_TILE = 256
FFN_PAD = 8


def _ffn_conv(xs, w_ref, b_ref, half, r0):
    win = xs[half, pl.ds(r0, CHUNK + FFN_PAD), :]
    acc = jnp.broadcast_to(b_ref[half], (CHUNK, FFN_TILE))
    for k in range(FFN_CONV_WIDTH):
        s = FFN_CONV_WIDTH - 1 - k
        sh = win if s == 0 else pltpu.roll(win, s, 0)
        acc = acc + w_ref[half, k:k + 1, :] * sh[FFN_PAD:FFN_PAD + CHUNK, :]
    return acc


def _ffn_act_fwd(up3, w3, b3, *, name):
    _, lp, f = up3.shape
    nch = lp // CHUNK

    def body(up_ref, w_ref, b_ref, act_ref, gv_ref, xs):
        for half in range(2):
            xs[half, 0:FFN_PAD, :] = jnp.zeros((FFN_PAD, FFN_TILE), F32)
            xs[half, FFN_PAD:FFN_PAD + lp, :] = up_ref[half]

        def chunk(ci, _):
            r0 = pl.multiple_of(ci * CHUNK, CHUNK)
            rows = pl.ds(r0, CHUNK)
            gate = _ffn_conv(xs, w_ref, b_ref, 0, r0)
            val = _ffn_conv(xs, w_ref, b_ref, 1, r0)
            gv_ref[0, rows, :] = gate
            gv_ref[1, rows, :] = val
            act_ref[rows, :] = (gate * _sigmoid(gate) * val).astype(BF16)
            return 0

        lax.fori_loop(0, nch, chunk, 0)

    halves = pl.BlockSpec((2, lp, FFN_TILE), lambda j: (0, 0, j))
    return pl.pallas_call(
        body, name=name, grid=(f // FFN_TILE,),
        in_specs=[halves, pl.BlockSpec((2, FFN_CONV_WIDTH, FFN_TILE), lambda j: (0, 0, j)),
                  pl.BlockSpec((2, 1, FFN_TILE), lambda j: (0, 0, j))],
        out_specs=[pl.BlockSpec((lp, FFN_TILE), lambda j: (0, j)), halves],
        out_shape=[jax.ShapeDtypeStruct((lp, f), BF16), jax.ShapeDtypeStruct((2, lp, f), F32)],
        scratch_shapes=[pltpu.VMEM((2, lp + FFN_PAD, FFN_TILE), F32)],
    )(up3, w3, b3)


def _ffn_act_bwd(up3, gv3, w3, dact, *, name):
    _, lp, f = up3.shape
    nch = lp // CHUNK
    wlen = CHUNK + FFN_PAD

    def body(up_ref, gv_ref, w_ref, dact_ref, dup_ref, dw_ref, db_ref, ds, wacc):
        for half in range(2):
            ds[half, lp:lp + FFN_PAD, :] = jnp.zeros((FFN_PAD, FFN_TILE), F32)
        wacc[...] = jnp.zeros_like(wacc)

        def act_bwd(ci, _):
            rows = pl.ds(pl.multiple_of(ci * CHUNK, CHUNK), CHUNK)
            gate, val = gv_ref[0, rows, :], gv_ref[1, rows, :]
            sg = _sigmoid(gate)
            da = dact_ref[rows, :]
            ds[0, rows, :] = da * val * (sg * (1.0 + gate * (1.0 - sg)))
            ds[1, rows, :] = da * (gate * sg)
            return 0

        lax.fori_loop(0, nch, act_bwd, 0)

        def conv_bwd(ci, _):
            r0 = pl.multiple_of(ci * CHUNK, CHUNK)
            rows = pl.ds(r0, CHUNK)
            for half in range(2):
                dwin = ds[half, pl.ds(r0, wlen), :]
                x = up_ref[half, rows, :]
                acc = jnp.zeros((CHUNK, FFN_TILE), F32)
                for k in range(FFN_CONV_WIDTH):
                    s = FFN_CONV_WIDTH - 1 - k
                    d_s = (dwin if s == 0 else pltpu.roll(dwin, wlen - s, 0))[0:CHUNK, :]
                    acc = acc + w_ref[half, k:k + 1, :] * d_s
                    wacc[half, 8 * k:8 * k + 8, :] += _fold8(d_s * x)
                wacc[half, 24:32, :] += _fold8(dwin[0:CHUNK, :])
                dup_ref[half, rows, :] = acc.astype(BF16)
            return 0

        lax.fori_loop(0, nch, conv_bwd, 0)
        for half in range(2):
            for k in range(FFN_CONV_WIDTH):
                dw_ref[half, k:k + 1, :] = jnp.sum(wacc[half, 8 * k:8 * k + 8, :], axis=0, keepdims=True)
            db_ref[half] = jnp.sum(wacc[half, 24:32, :], axis=0, keepdims=True)

    halves = pl.BlockSpec((2, lp, FFN_TILE), lambda j: (0, 0, j))
    taps = pl.BlockSpec((2, FFN_CONV_WIDTH, FFN_TILE), lambda j: (0, 0, j))
    bias = pl.BlockSpec((2, 1, FFN_TILE), lambda j: (0, 0, j))
    return pl.pallas_call(
        body, name=name, grid=(f // FFN_TILE,),
        in_specs=[halves, halves, taps, pl.BlockSpec((lp, FFN_TILE), lambda j: (0, j))],
        out_specs=[halves, taps, bias],
        out_shape=[jax.ShapeDtypeStruct((2, lp, f), BF16), jax.ShapeDtypeStruct((2, FFN_CONV_WIDTH, f), F32),
                   jax.ShapeDtypeStruct((2, 1, f), F32)],
        scratch_shapes=[pltpu.VMEM((2, lp + FFN_PAD, FFN_TILE), F32), pltpu.VMEM((2, 32, FFN_TILE), F32)],
    )(up3, gv3, w3, dact)


POOL_PAD = 16


def _inv_count(r0, w):
    t = r0 + lax.broadcasted_iota(jnp.int32, (CHUNK, 1), 0)
    return 1.0 / jnp.minimum(t + 1, w).astype(F32)


def _pool_fwd(n, pool_w, pool_b, pool_scale, h, *, name):
    lp, dm = n.shape
    nch = lp // CHUNK
    g = POOL_GROUP

    def body(n_ref, w_ref, b_ref, s_ref, h_ref, ho_ref, d_ref, z_ref, xs):
        gi = pl.program_id(0)
        xs[0:POOL_PAD, :] = jnp.zeros((POOL_PAD, g), F32)
        xs[POOL_PAD:POOL_PAD + lp, :] = n_ref[...]
        for idx, w in enumerate(POOL_WINDOWS):
            @pl.when(gi == idx)
            def _(w=w):
                def chunk(ci, _):
                    r0 = pl.multiple_of(ci * CHUNK, CHUNK)
                    win = xs[pl.ds(r0, CHUNK + POOL_PAD), :]
                    acc = win
                    for j in range(1, w):
                        acc = acc + pltpu.roll(win, j, 0)
                    x = win[POOL_PAD:POOL_PAD + CHUNK, :]
                    d_ref[pl.ds(r0, CHUNK), :] = (acc[POOL_PAD:POOL_PAD + CHUNK, :] * _inv_count(r0, w) - x).astype(BF16)
                    return 0

                lax.fori_loop(0, nch, chunk, 0)

        z = jnp.dot(d_ref[...], w_ref[...], preferred_element_type=F32) + b_ref[...]
        z_ref[...] = z
        ho_ref[...] = h_ref[...] + z * s_ref[...]

    col = pl.BlockSpec((lp, g), lambda i: (0, i))
    vec = pl.BlockSpec((1, g), lambda i: (0, i))
    return pl.pallas_call(
        body, name=name, grid=(len(POOL_WINDOWS),),
        in_specs=[col, pl.BlockSpec((None, g, g), lambda i: (i, 0, 0)), vec, vec, col],
        out_specs=[col, col, col],
        out_shape=[jax.ShapeDtypeStruct((lp, dm), F32), jax.ShapeDtypeStruct((lp, dm), BF16),
                   jax.ShapeDtypeStruct((lp, dm), F32)],
        scratch_shapes=[pltpu.VMEM((lp + POOL_PAD, g), F32)],
    )(n, pool_w, pool_b, pool_scale, h)


def _pool_bwd(dy, z, pool_w, pool_scale, *, name):
    lp, dm = dy.shape
    nch = lp // CHUNK
    g = POOL_GROUP
    wlen = CHUNK + POOL_PAD

    def body(dy_ref, z_ref, w_ref, s_ref, dn_ref, dz_ref, dsc_ref, db_ref, ys, dd):
        gi = pl.program_id(0)
        dyv = dy_ref[...]
        dsc_ref[...] = jnp.sum(dyv * z_ref[...], axis=0, keepdims=True)
        dz = dyv * s_ref[...]
        db_ref[...] = jnp.sum(dz, axis=0, keepdims=True)
        dzb = dz.astype(BF16)
        dz_ref[...] = dzb
        dd[...] = lax.dot_general(dzb, w_ref[...], (((1,), (1,)), ((), ())), preferred_element_type=F32)
        ys[lp:lp + POOL_PAD, :] = jnp.zeros((POOL_PAD, g), F32)
        for idx, w in enumerate(POOL_WINDOWS):
            @pl.when(gi == idx)
            def _(w=w):
                def scale(ci, _):
                    r0 = pl.multiple_of(ci * CHUNK, CHUNK)
                    ys[pl.ds(r0, CHUNK), :] = dd[pl.ds(r0, CHUNK), :] * _inv_count(r0, w)
                    return 0

                lax.fori_loop(0, nch, scale, 0)

                def chunk(ci, _):
                    r0 = pl.multiple_of(ci * CHUNK, CHUNK)
                    win = ys[pl.ds(r0, wlen), :]
                    acc = win
                    for j in range(1, w):
                        acc = acc + pltpu.roll(win, wlen - j, 0)
                    dn_ref[pl.ds(r0, CHUNK), :] = acc[0:CHUNK, :] - dd[pl.ds(r0, CHUNK), :]
                    return 0

                lax.fori_loop(0, nch, chunk, 0)

    col = pl.BlockSpec((lp, g), lambda i: (0, i))
    vec = pl.BlockSpec((1, g), lambda i: (0, i))
    return pl.pallas_call(
        body, name=name, grid=(len(POOL_WINDOWS),),
        in_specs=[col, col, pl.BlockSpec((None, g, g), lambda i: (i, 0, 0)), vec],
        out_specs=[col, col, vec, vec],
        out_shape=[jax.ShapeDtypeStruct((lp, dm), F32), jax.ShapeDtypeStruct((lp, dm), BF16),
                   jax.ShapeDtypeStruct((1, dm), F32), jax.ShapeDtypeStruct((1, dm), F32)],
        scratch_shapes=[pltpu.VMEM((lp + POOL_PAD, g), F32), pltpu.VMEM((lp, g), F32)],
    )(dy, z, pool_w, pool_scale)


def _ffn_fwd(h, g, w_up_t, w3, b3, w_down, tag):
    lp = h.shape[0]
    nj = D_FF // FFN_TILE
    n = _rms_fwd(h, g, name=f"rms_ffn{tag}", out_dtype=BF16)
    up2 = _mm(n, w_up_t, name=f"mm_up{tag}", tb=True, tn=FFN_TILE, dims=(lp, 2 * D_FF, D_MODEL),
              o_map=lambda i, j, k: (j // nj, j % nj), out_shape=(2 * lp, D_FF))
    up3 = up2.reshape(2, lp, D_FF)
    act, gv3 = _ffn_act_fwd(up3, w3, b3, name=f"ffn_act{tag}")
    h_out = _mm(act, w_down, name=f"mm_down{tag}", tn=256, res=h)
    return h_out, (n, up3, gv3, act)


def _ffn_bwd(dh, dhb, h, g, saved, w_up_t, w3, w_down, tag, after=None):
    lp = h.shape[0]
    n, up3, gv3, act = saved
    act_t = _transpose(act, name=f"t_act{tag}", out_dtype=BF16)
    dw_down = _mm(act_t, dhb, name=f"mm_dwdown{tag}", tm=704, out_dtype=BF16)
    dact = _mm(dhb, w_down, name=f"mm_dact{tag}", tb=True, tn=256, after=after)
    dup3, dcw, dcb = _ffn_act_bwd(up3, gv3, w3, dact, name=f"ffn_act_bwd{tag}")
    dup2 = dup3.reshape(2 * lp, D_FF)
    n_t = _transpose(n, name=f"t_nffn{tag}", out_dtype=BF16)
    dw_up = _mm_dw_up(n_t, dup2, name=f"mm_dwup{tag}")
    dn = _mm(dup2, w_up_t, name=f"mm_dnffn{tag}", tm=lp // 2, tk=D_FF // 2, dims=(lp, D_MODEL, 2 * D_FF),
             a_map=lambda i, j, k: (2 * (k // 2) + i, k % 2))
    dh_in, dh_in_b, dg = _rms_bwd(h, g, dn, dh, name=f"rms_bwd_ffn{tag}")
    return dh_in, dh_in_b, (dg, dw_up, dcw, dcb, dw_down)


def _local_step(x, tgt, wt):
    seq = x.shape[0]
    n_real = N_META + seq
    lp = _round_up(n_real, CHUNK)
    pad = jnp.zeros((lp - n_real, D_MODEL), F32)
    h0 = jnp.concatenate([wt["meta"], x, pad], axis=0)
    tgt_p = jnp.concatenate([jnp.zeros((N_META, D_MODEL), F32), tgt, pad], axis=0)
    w_in_p = wt["w_in_p"]

    n0 = _rms_fwd(h0, wt["g_even"], name="rms_even", out_dtype=BF16, after=wt["ffn_started"])
    qkv = _mm(n0, w_in_p, name="mm_qkv", tn=512, dims=(lp, 3 * FOX_WIDTH, D_MODEL), out_dtype=BF16)
    ag = _mm(n0, w_in_p, name="mm_ag", tn=512, dims=(lp, 2 * CONV_CH, D_MODEL),
             b_map=lambda i, j, k: (0, 3 + j))
    f_t = _mm(wt["wf_t"], n0, name="mm_ft", tb=True)
    c_row = _fox_prep(f_t, wt["b_f"], name="fox_prep")
    c3 = c_row.reshape(4, 2, lp)
    o = _attn_fwd(qkv, c3, name="attn_fwd")
    u0, u1, u3 = _conv_fwd(ag, wt["conv_w"], wt["conv_b"], wt["ln_g"], wt["ln_b"], name="conv_fwd")
    cat = jnp.concatenate([o, u3], axis=1)
    h1 = _mm(cat, wt["w_out"], name="mm_out", tn=256, res=h0)
    w_up0, w_down0 = wt["ffn_weights"](0, h1)
    h2, saved0 = _ffn_fwd(h1, wt["ffn_norm"][0:1], w_up0, wt["fcw3"][0], wt["fcb3"][0], w_down0, 0)

    n2 = _rms_fwd(h2, wt["g_odd"], name="rms_odd", out_dtype=F32)
    h3, dpool, z = _pool_fwd(n2, wt["pool_w"], wt["pool_b"], wt["pool_scale"], h2, name="pool_fwd")
    w_up1, w_down1 = wt["ffn_weights"](1, h3)
    h4, saved1 = _ffn_fwd(h3, wt["ffn_norm"][1:2], w_up1, wt["fcw3"][1], wt["fcb3"][1], w_down1, 1)

    loss, dh4, dh4b, d_gfinal = _loss_head(h4, wt["g_final"], tgt_p, n_real, name="loss_head")

    dh3, dh3b, gf1 = _ffn_bwd(dh4, dh4b, h3, wt["ffn_norm"][1:2], saved1, w_up1, wt["fcw3"][1], w_down1, 1)
    send1, token1 = _send_ffn_grads(gf1[1], gf1[4], 1)
    dn2, dzb, d_pscale, d_pb = _pool_bwd(dh3, z, wt["pool_w"], wt["pool_scale"], name="pool_bwd")
    dpool_t = _transpose(dpool, name="t_dpool", out_dtype=BF16)
    d_pw = _mm(dpool_t, dzb, name="mm_dpoolw", tm=POOL_GROUP, tn=POOL_GROUP, dims=(D_MODEL, POOL_GROUP, lp),
               b_map=lambda i, j, k: (0, i), o_map=lambda i, j, k: (i, 0), out_shape=(D_MODEL, POOL_GROUP),
               out_dtype=BF16)
    dh2, dh2b, d_godd = _rms_bwd(h2, wt["g_odd"], dn2, dh3, name="rms_bwd_odd")
    dh1, dh1b, gf0 = _ffn_bwd(dh2, dh2b, h1, wt["ffn_norm"][0:1], saved0, w_up0, wt["fcw3"][0], w_down0, 0,
                              after=token1)

    send0, token0 = _send_ffn_grads(gf0[1], gf0[4], 0)
    cat_t = _transpose(cat, name="t_cat", out_dtype=BF16)
    d_wout = _mm(cat_t, dh1b, name="mm_dwout", tm=512, out_dtype=BF16)
    dcat = _mm(dh1b, wt["w_out"], name="mm_dcat", tb=True, tn=256, after=token0)
    dag, d_convw, d_convb, d_lng, d_lnb = _conv_bwd(dcat, u0, u1, ag, wt["conv_w"], wt["ln_g"], wt["ln_b"],
                                                    name="conv_bwd")
    layers = lambda i: jnp.stack([gf0[i], gf1[i]])
    grads = dict(
        conv_w=d_convw[None], w_out=d_wout, mix_norm_odd=d_godd,
        pool_w=d_pw.reshape(len(POOL_WINDOWS), POOL_GROUP, POOL_GROUP),
        pool_b=d_pb.reshape(1, len(POOL_WINDOWS), POOL_GROUP), pool_scale=d_pscale, w_up=(gf0[1], gf1[1]),
        ffn_conv_w=layers(2).transpose(0, 2, 1, 3).reshape(DEPTH, FFN_CONV_WIDTH, 2 * D_FF), w_down=(gf0[4], gf1[4]))
    send_rest, token_rest, grads["little_slabs"] = _send_rest_grads(grads)
    q_t = _transpose(qkv, name="t_q", out_dtype=BF16, cols=FOX_WIDTH, after=token_rest)
    do_t = _transpose(dcat, name="t_do", out_dtype=BF16, cols=FOX_WIDTH)
    dq, dk_t, dv_t, dc3 = _attn_bwd(qkv, q_t, dcat, do_t, c3, name="attn_bwd")
    dk = _transpose(dk_t, name="t_dk", out_dtype=BF16)
    dv = _transpose(dv_t, name="t_dv", out_dtype=BF16)
    df_t, d_bf = _fox_bwd(dc3.reshape(FOX_HEADS, lp), f_t, wt["b_f"], name="fox_bwd")
    df = _transpose(df_t, name="t_df", out_dtype=BF16)
    dproj = jnp.concatenate([dq, dk, dv, dag, df], axis=1)
    n0_t = _transpose(n0, name="t_n0", out_dtype=BF16)
    grads["w_in"] = _mm_dw_in(n0_t, dproj, name="mm_dwin")
    send_in, token_in = _send_start(
        [grads["w_in"]], [jax.ShapeDtypeStruct((N_DEV - 1, D_MODEL, _IN_SHARD), BF16)], [(0, _by_owner, 0, None)],
        name="send_w_in")
    dn0 = _mm(dproj, w_in_p, name="mm_dn0", tb=True, tm=lp // 2, tk=896, after=token_in)
    dh0, _, d_geven = _rms_bwd(h0, wt["g_even"], dn0, dh1, name="rms_bwd_even")
    grads.update(
        meta_tokens=dh0[0:N_META], mix_norm_even=d_geven, b_f=d_bf.reshape(1, FOX_HEADS), conv_b=d_convb, ln_g=d_lng,
        ln_b=d_lnb, ffn_norm=jnp.concatenate([gf0[0], gf1[0]], axis=0),
        ffn_conv_b=layers(3).reshape(DEPTH, 2 * D_FF), final_norm=d_gfinal.reshape(D_MODEL),
        sends=(send0, send1, send_rest, send_in))
    return loss, dh0[N_META:n_real], grads


_LITTLE = (("conv_w", (1, 31, 512), 2), ("mix_norm_odd", (1, 1024), 1), ("pool_b", (1, 4, 256), 2),
           ("pool_scale", (1, 1024), 1), ("ffn_conv_w", (2, 3, 5632), 2))


def _send_rest_grads(g):
    out_rows, pool_rows, groups = D_MODEL // N_DEV, POOL_GROUP // N_DEV, len(POOL_WINDOWS)
    little_slabs = _pack([_full_to_slabs(g[n], s, a) for n, s, a in _LITTLE], F32, lead=(N_DEV,), align=8)
    land = lambda shape, dtype: jax.ShapeDtypeStruct((N_DEV - 1,) + shape, dtype)
    handle, token = _send_start(
        [g["w_out"], g["pool_w"], little_slabs],
        [land((out_rows, D_MODEL), BF16), land((groups, pool_rows, POOL_GROUP), BF16), land(little_slabs.shape[1:], F32)],
        [(0, _row_block(out_rows), 0, None), (1, _row_block(pool_rows, axis=1), 1, None), (2, _by_owner, 2, None)],
        name="send_rest")
    return handle, token, little_slabs


def _send_ffn_grads(dw_up, dw_down, tag):
    rows = D_FF // N_DEV
    lands = [jax.ShapeDtypeStruct((N_DEV - 1,) + dw_up.shape[1:], BF16),
             jax.ShapeDtypeStruct((N_DEV - 1, rows, D_MODEL), BF16)]
    return _send_start([dw_up, dw_down], lands, [(0, _by_owner, 0, None), (1, _row_block(rows), 1, None)],
                       name=f"send_ffn{tag}")


_QKV = 3 * FOX_WIDTH
_GLU0 = _QKV + FOX_HEADS
_IN_COLS = _GLU0 + 2 * CONV_CH
_F_PAD = 128


_IN_SHARD = _IN_COLS // N_DEV
_UP_SHARD = 2 * D_FF // N_DEV
_ROW_TILE = 256


def _assemble_w_in(st, *, name):
    tr = _ROW_TILE

    def body(s_ref, o_ref):
        full = jnp.concatenate([s_ref[i].astype(F32) for i in range(N_DEV)], axis=1)
        parts = [full[:, :_QKV], full[:, _GLU0:], full[:, _QKV:_GLU0], jnp.zeros((tr, _F_PAD - FOX_HEADS), F32)]
        o_ref[...] = jnp.concatenate(parts, axis=1).astype(BF16)

    return pl.pallas_call(
        body, name=name, grid=(D_MODEL // tr,),
        in_specs=[pl.BlockSpec((N_DEV, tr, _IN_SHARD), lambda i: (0, i, 0))],
        out_specs=pl.BlockSpec((tr, _QKV + 2 * CONV_CH + _F_PAD), lambda i: (i, 0)),
        out_shape=jax.ShapeDtypeStruct((D_MODEL, _QKV + 2 * CONV_CH + _F_PAD), BF16),
    )(st)


def _mm_dw_in(n_t, dproj, *, name):
    dm, lp = n_t.shape
    tr = _ROW_TILE
    ag0 = _QKV + 2 * CONV_CH

    def body(a_ref, b_ref, o_ref):
        r = jnp.dot(a_ref[...], b_ref[...], preferred_element_type=F32)
        full = jnp.concatenate([r[:, :_QKV], r[:, ag0:ag0 + FOX_HEADS], r[:, _QKV:ag0]], axis=1)
        for i in range(N_DEV):
            o_ref[i] = full[:, i * _IN_SHARD:(i + 1) * _IN_SHARD].astype(BF16)

    return pl.pallas_call(
        body, name=name, grid=(dm // tr,),
        in_specs=[pl.BlockSpec((tr, lp), lambda i: (i, 0)), pl.BlockSpec(dproj.shape, lambda i: (0, 0))],
        out_specs=pl.BlockSpec((N_DEV, tr, _IN_SHARD), lambda i: (0, i, 0)),
        out_shape=jax.ShapeDtypeStruct((N_DEV, dm, _IN_SHARD), BF16),
    )(n_t, dproj)


def _mm_dw_up(n_t, dup2, *, name):
    dm, lp = n_t.shape
    pairs_per_half = D_FF // (2 * _UP_SHARD)

    def body(a_ref, b_ref, o_ref):
        r_t = jnp.dot(a_ref[...], b_ref[...], preferred_element_type=F32).T
        o_ref[0] = r_t[:_UP_SHARD, :].astype(BF16)
        o_ref[1] = r_t[_UP_SHARD:, :].astype(BF16)

    return pl.pallas_call(
        body, name=name, grid=(N_DEV // 2,),
        in_specs=[pl.BlockSpec((dm, lp), lambda p: (0, 0)),
                  pl.BlockSpec((lp, 2 * _UP_SHARD), lambda p: (p // pairs_per_half, p % pairs_per_half))],
        out_specs=pl.BlockSpec((2, _UP_SHARD, dm), lambda p: (p, 0, 0)),
        out_shape=jax.ShapeDtypeStruct((N_DEV, _UP_SHARD, dm), BF16),
    )(n_t, dup2)


MESH = pl.DeviceIdType.MESH
ANY = pl.BlockSpec(memory_space=pl.ANY)


def _slot(px, py, pc):
    return 4 * px + 2 * py + pc


def _by_owner(ref, slot):
    return ref.at[slot]


def _row_block(rows, axis=0):
    def place(ref, slot):
        idx = (slice(None),) * axis + (pl.ds(slot * rows, rows),)
        return ref.at[idx]
    return place


def _all_gather(arrs, out_shapes, places, *, name):
    n = len(arrs)

    def body(*refs):
        ins, outs = refs[:n], refs[n:2 * n]
        send_sems, recv_sems, local_sems = refs[2 * n:]
        x, y, c = lax.axis_index("x"), lax.axis_index("y"), lax.axis_index("c")
        me, sibling = (x, y, c), (x, y, 1 - c)
        chips = [(1 - x, y), (x, 1 - y), (1 - x, 1 - y)]

        def copy(a, k, block, to, from_input=False):
            dst = places[a](outs[a], _slot(*block))
            return pltpu.make_async_remote_copy(
                src_ref=ins[a] if from_input else dst, dst_ref=dst,
                send_sem=send_sems.at[7 * a + k], recv_sem=recv_sems.at[7 * a + k],
                device_id=to, device_id_type=MESH)

        own, sent = [], []
        for a in range(n):
            mine = pltpu.make_async_copy(ins[a], places[a](outs[a], _slot(*me)), local_sems.at[a])
            mine.start()
            own.append(mine)
            first = [copy(a, 0, me, sibling, True)]
            first += [copy(a, 1 + j, me, (*chip, c), True) for j, chip in enumerate(chips)]
            for cp in first:
                cp.start()
            sent += first
        for a in range(n):
            for j, chip in enumerate(chips):
                copy(a, 1 + j, (*chip, c), me).wait_recv()
                passed = copy(a, 4 + j, (*chip, c), sibling)
                passed.start()
                sent.append(passed)
        for a in range(n):
            copy(a, 0, sibling, me).wait_recv()
            for j, chip in enumerate(chips):
                copy(a, 4 + j, (*chip, 1 - c), me).wait_recv()
        for cp in sent:
            cp.wait_send()
        for cp in own:
            cp.wait()

    return pl.pallas_call(
        body, name=name,
        in_specs=[ANY] * n, out_specs=[ANY] * n,
        out_shape=[jax.ShapeDtypeStruct(s, a.dtype) for s, a in zip(out_shapes, arrs)],
        scratch_shapes=[pltpu.SemaphoreType.DMA((7 * n,)), pltpu.SemaphoreType.DMA((7 * n,)),
                        pltpu.SemaphoreType.DMA((n,))],
    )(*arrs)


def _exchange(srcs, recv_structs, copies, *, name):
    ns, nc = len(srcs), len(copies)

    def body(*refs):
        src_refs, recv_refs = refs[:ns], refs[ns:ns + len(recv_structs)]
        send_sems, recv_sems, local_sems = refs[ns + len(recv_structs):]
        x, y, c = lax.axis_index("x"), lax.axis_index("y"), lax.axis_index("c")
        me = _slot(x, y, c)

        def peer(k):
            flip = lambda v, bit: 1 - v if bit else v
            return flip(x, k & 4), flip(y, k & 2), flip(c, k & 1)

        def copy(k, j, arriving):
            si, take, ri, put = copies[j]
            p = _slot(*peer(k))
            sem = (k - 1) * nc + j
            return pltpu.make_async_remote_copy(
                src_ref=take(src_refs[si], p), dst_ref=put(recv_refs[ri], p if arriving else me),
                send_sem=send_sems.at[sem], recv_sem=recv_sems.at[sem], device_id=peer(k), device_id_type=MESH)

        own = [pltpu.make_async_copy(take(src_refs[si], me), put(recv_refs[ri], me), local_sems.at[j])
               for j, (si, take, ri, put) in enumerate(copies)]
        for cp in own:
            cp.start()
        sent = [copy(k, j, False) for k in range(1, N_DEV) for j in range(nc)]
        for cp in sent:
            cp.start()
        for k in range(1, N_DEV):
            for j in range(nc):
                copy(k, j, True).wait_recv()
        for cp in sent:
            cp.wait_send()
        for cp in own:
            cp.wait()

    return pl.pallas_call(
        body, name=name,
        in_specs=[ANY] * ns, out_specs=[ANY] * len(recv_structs), out_shape=list(recv_structs),
        scratch_shapes=[pltpu.SemaphoreType.DMA((7 * nc,)), pltpu.SemaphoreType.DMA((7 * nc,)),
                        pltpu.SemaphoreType.DMA((nc,))],
    )(*srcs)


HBM = pl.BlockSpec(memory_space=pltpu.HBM)
SEM = pl.BlockSpec(memory_space=pltpu.SEMAPHORE)
EFFECT = pltpu.SideEffectType.DATAFLOW_SIDE_EFFECTING


def _relation_copies(src_refs, land_refs, copies, send_sems, recv_sems):
    x, y, c = lax.axis_index("x"), lax.axis_index("y"), lax.axis_index("c")
    flip = lambda v, bit: 1 - v if bit else v
    out = []
    for k in range(1, N_DEV):
        p = (flip(x, k & 4), flip(y, k & 2), flip(c, k & 1))
        for j, (si, take, li, put) in enumerate(copies):
            sem = (k - 1) * len(copies) + j
            dst = land_refs[li].at[k - 1] if put is None else put(land_refs[li], _slot(x, y, c))
            out.append(pltpu.make_async_remote_copy(
                src_ref=take(src_refs[si], _slot(*p)), dst_ref=dst,
                send_sem=send_sems.at[sem], recv_sem=recv_sems.at[sem], device_id=p, device_id_type=MESH))
    return out


def _own_copies(src_refs, land_refs, copies, sems):
    me = _slot(lax.axis_index("x"), lax.axis_index("y"), lax.axis_index("c"))
    placed = [(si, take, li, put) for si, take, li, put in copies if put is not None]
    return [pltpu.make_async_copy(take(src_refs[si], me), put(land_refs[li], me),
                                  sems.at[(N_DEV - 1) * len(copies) + j])
            for j, (si, take, li, put) in enumerate(placed)]


def _send_start(srcs, land_structs, copies, *, name, after=None):
    ns, nl = len(srcs), len(land_structs)
    n_sem = (N_DEV - 1) * len(copies) + sum(put is not None for _, _, _, put in copies)
    behind = [] if after is None else [after]

    def body(*refs):
        first_out = ns + nl + len(behind)
        send_sems, recv_sems, token = refs[first_out], refs[first_out + 1], refs[-1]
        for cp in _relation_copies(refs[:ns], refs[ns:ns + nl], copies, send_sems, recv_sems):
            cp.start()
        for cp in _own_copies(refs[:ns], refs[ns:ns + nl], copies, send_sems):
            cp.start()
        token[...] = jnp.zeros_like(token)

    in_hbm = lambda a: pltpu.with_memory_space_constraint(a, pltpu.HBM)
    outs = pl.pallas_call(
        body, name=name,
        out_shape=(pltpu.SemaphoreType.DMA((n_sem,)), pltpu.SemaphoreType.DMA((n_sem,)),
                   *[pltpu.HBM(s.shape, s.dtype) for s in srcs],
                   *[pltpu.HBM(s.shape, s.dtype) for s in land_structs],
                   jax.ShapeDtypeStruct((8, 128), F32)),
        in_specs=(HBM,) * (ns + nl) + (ANY,) * len(behind),
        out_specs=(SEM, SEM) + (HBM,) * (ns + nl) + (pl.BlockSpec(memory_space=pltpu.VMEM),),
        input_output_aliases={i: 2 + i for i in range(ns + nl)},
        compiler_params=pltpu.CompilerParams(has_side_effects=EFFECT),
    )(*[in_hbm(s) for s in srcs], *[in_hbm(lax.empty(s.shape, s.dtype)) for s in land_structs], *behind)
    return (outs[0], outs[1], outs[2:2 + ns], outs[2 + ns:2 + ns + nl], copies), outs[-1]


def _send_wait(handle, after, *, name):
    send_sems, recv_sems, srcs, lands, copies = handle
    ns, nl = len(srcs), len(lands)

    def body(*refs):
        for cp in _relation_copies(refs[:ns], refs[ns:ns + nl], copies, refs[ns + nl], refs[ns + nl + 1]):
            cp.wait_send()
            cp.wait_recv()
        for cp in _own_copies(refs[:ns], refs[ns:ns + nl], copies, refs[ns + nl]):
            cp.wait()

    outs = pl.pallas_call(
        body, name=name,
        out_shape=tuple(pltpu.HBM(a.shape, a.dtype) for a in (*srcs, *lands)),
        in_specs=(HBM,) * (ns + nl) + (SEM, SEM, ANY), out_specs=(HBM,) * (ns + nl),
        input_output_aliases={i: i for i in range(ns + nl)},
        compiler_params=pltpu.CompilerParams(has_side_effects=EFFECT),
    )(*srcs, *lands, send_sems, recv_sems, after)
    return outs[ns:]


def _sum_slabs(stack, *, name, own=None):
    n, rows, w = stack.shape

    def body(*refs):
        s_ref, o_ref = refs[-2], refs[-1]
        acc = s_ref[0] if own is None else refs[0][...] + s_ref[0]
        for i in range(1, n):
            acc = acc + s_ref[i]
        o_ref[...] = acc

    return pl.pallas_call(body, name=name, out_shape=jax.ShapeDtypeStruct((rows, w), F32))(
        *([] if own is None else [own]), stack)


def _adam_math(w, g, m, v):
    mn = ADAM_B1 * m + (1.0 - ADAM_B1) * g
    vn = ADAM_B2 * v + (1.0 - ADAM_B2) * (g * g)
    m_hat = mn / (1.0 - ADAM_B1 ** ADAM_STEP)
    v_hat = vn / (1.0 - ADAM_B2 ** ADAM_STEP)
    return -ADAM_LR * (m_hat / (jnp.sqrt(v_hat) + ADAM_EPS) + ADAM_WD * w), mn, vn


def _adamw(w, g, m, v, *, name):
    def body(w_ref, g_ref, m_ref, v_ref, d_ref, mo_ref, vo_ref):
        d_ref[...], mo_ref[...], vo_ref[...] = _adam_math(w_ref[...], g_ref[...], m_ref[...], v_ref[...])

    return pl.pallas_call(body, name=name, out_shape=[jax.ShapeDtypeStruct(w.shape, F32)] * 3)(w, g, m, v)


def _adamw_layers(w, owns, lands, m, v, tr, *, name):
    nl, rows, cols = w.shape
    steps = rows // tr
    assert rows % tr == 0

    def body(*refs):
        w_ref, m_ref, v_ref = refs[:3]
        own_refs, land_refs = refs[3:3 + nl], refs[3 + nl:3 + 2 * nl]
        g_ref, d_ref, mo_ref, vo_ref = refs[3 + 2 * nl:]
        for li in range(nl):
            @pl.when(pl.program_id(0) == li)
            def _(li=li):
                g = own_refs[li][...].astype(F32)
                for k in range(N_DEV - 1):
                    g = g + land_refs[li][k].astype(F32)
                g_ref[...] = g
                d_ref[...], mo_ref[...], vo_ref[...] = _adam_math(w_ref[...], g, m_ref[...], v_ref[...])

    def held(li):
        return lambda l, i: jnp.where(l == li, i, jnp.where(l < li, 0, steps - 1))

    blk = pl.BlockSpec((None, tr, cols), lambda l, i: (l, i, 0))
    own_specs = [pl.BlockSpec((tr, cols), lambda l, i, f=held(li): (f(l, i), 0)) for li in range(nl)]
    land_specs = [pl.BlockSpec((N_DEV - 1, tr, cols), lambda l, i, f=held(li): (0, f(l, i), 0)) for li in range(nl)]
    return pl.pallas_call(
        body, name=name, grid=(nl, steps),
        in_specs=[blk, blk, blk] + own_specs + land_specs, out_specs=[blk] * 4,
        out_shape=[jax.ShapeDtypeStruct(w.shape, F32)] * 4,
    )(w, m, v, *owns, *lands)


_WEIGHTS = (
    ("meta_tokens", (16, 1024), 1), ("mix_norm_even", (1, 1024), None), ("w_in", (1, 1024, 2568), 2),
    ("b_f", (1, 8), None), ("conv_w", (1, 31, 512), 2), ("conv_b", (1, 512), None), ("ln_g", (1, 512), None),
    ("ln_b", (1, 512), None), ("w_out", (1, 1024, 1024), 1), ("mix_norm_odd", (1, 1024), 1),
    ("pool_w", (1, 4, 256, 256), 2), ("pool_b", (1, 4, 256), 2), ("pool_scale", (1, 1024), 1),
    ("ffn_norm", (2, 1024), None), ("w_up", (2, 1024, 5632), 2), ("ffn_conv_w", (2, 3, 5632), 2),
    ("ffn_conv_b", (2, 5632), None), ("w_down", (2, 2816, 1024), 1), ("final_norm", (1024,), None),
)
_MATMUL_WEIGHTS = ("w_in", "w_out", "pool_w", "w_up", "w_down")
_ADAM_ROWS = dict(w_in=256, w_out=128, pool_w=128, w_up=352, w_down=352)


def _shard_shape(shape, axis):
    return shape[:axis] + (shape[axis] // N_DEV,) + shape[axis + 1:]


def _size(shape):
    n = 1
    for s in shape:
        n *= s
    return n


def _pack(parts, dtype, lead=(), align=16):
    flat = jnp.concatenate([p.reshape(lead + (-1,)).astype(dtype) for p in parts], axis=-1)
    n = flat.shape[-1]
    rows = _round_up(-(-n // FLAT_W), align)
    flat = jnp.pad(flat, [(0, 0)] * len(lead) + [(0, rows * FLAT_W - n)])
    return flat.reshape(lead + (rows, FLAT_W))


def _unpack(buf, shapes, lead=()):
    flat = buf.reshape(lead + (-1,))
    out, off = [], 0
    for shp in shapes:
        n = _size(shp)
        out.append(flat[..., off:off + n].reshape(lead + shp))
        off += n
    return out


def _gathered_to_full(stack, shape, axis):
    return jnp.moveaxis(stack, 0, axis).reshape(shape)


def _full_to_slabs(full, shape, axis):
    split = shape[:axis] + (N_DEV, shape[axis] // N_DEV) + shape[axis + 1:]
    return jnp.moveaxis(full.reshape(split), axis, 0)


def kernel(x, meta_tokens, mix_norm_even, w_in, b_f, conv_w, conv_b, ln_g, ln_b, w_out, mix_norm_odd, pool_w, pool_b, pool_scale, ffn_norm, w_up, ffn_conv_w, ffn_conv_b, w_down, final_norm, loss_target, m_meta_tokens, m_mix_norm_even, m_w_in, m_b_f, m_conv_w, m_conv_b, m_ln_g, m_ln_b, m_w_out, m_mix_norm_odd, m_pool_w, m_pool_b, m_pool_scale, m_ffn_norm, m_w_up, m_ffn_conv_w, m_ffn_conv_b, m_w_down, m_final_norm, v_meta_tokens, v_mix_norm_even, v_w_in, v_b_f, v_conv_w, v_conv_b, v_ln_g, v_ln_b, v_w_out, v_mix_norm_odd, v_pool_w, v_pool_b, v_pool_scale, v_ffn_norm, v_w_up, v_ffn_conv_w, v_ffn_conv_b, v_w_down, v_final_norm):
    names = [n for n, _, _ in _WEIGHTS]
    w_loc = dict(zip(names, (meta_tokens, mix_norm_even, w_in, b_f, conv_w, conv_b, ln_g, ln_b, w_out, mix_norm_odd,
                             pool_w, pool_b, pool_scale, ffn_norm, w_up, ffn_conv_w, ffn_conv_b, w_down, final_norm)))
    m_loc = dict(zip(names, (m_meta_tokens, m_mix_norm_even, m_w_in, m_b_f, m_conv_w, m_conv_b, m_ln_g, m_ln_b,
                             m_w_out, m_mix_norm_odd, m_pool_w, m_pool_b, m_pool_scale, m_ffn_norm, m_w_up,
                             m_ffn_conv_w, m_ffn_conv_b, m_w_down, m_final_norm)))
    v_loc = dict(zip(names, (v_meta_tokens, v_mix_norm_even, v_w_in, v_b_f, v_conv_w, v_conv_b, v_ln_g, v_ln_b,
                             v_w_out, v_mix_norm_odd, v_pool_w, v_pool_b, v_pool_scale, v_ffn_norm, v_w_up,
                             v_ffn_conv_w, v_ffn_conv_b, v_w_down, v_final_norm)))
    replicated = [(n, s) for n, s, a in _WEIGHTS if a is None]
    little = [(n, s, a) for n, s, a in _WEIGHTS if a is not None and n not in _MATMUL_WEIGHTS]
    little_shards = [_shard_shape(s, a) for _, s, a in little]
    out_rows, down_rows, pool_rows = D_MODEL // N_DEV, D_FF // N_DEV, POOL_GROUP // N_DEV
    n_groups = len(POOL_WINDOWS)

    little_pack = _pack([w_loc[n] for n, _, _ in little], F32)
    g_win, g_wout, g_poolw, g_little = _all_gather(
        [w_in[0].astype(BF16), w_out[0].astype(BF16), pool_w[0].astype(BF16), little_pack],
        [(N_DEV, D_MODEL, _IN_SHARD), (D_MODEL, D_MODEL), (n_groups, POOL_GROUP, POOL_GROUP),
         (N_DEV,) + little_pack.shape],
        [_by_owner, _row_block(out_rows), _row_block(pool_rows, axis=1), _by_owner],
        name="gather_weights")
    me = _slot(lax.axis_index("x"), lax.axis_index("y"), lax.axis_index("c"))
    up_t = lambda a: jnp.transpose(a, (0, 2, 1))
    w_loc["w_up"], m_loc["w_up"], v_loc["w_up"] = up_t(w_up), up_t(m_w_up), up_t(v_w_up)
    w_up_b, w_down_b = w_loc["w_up"].astype(BF16), w_down.astype(BF16)
    whole = lambda ref, slot: ref
    ffn_lands = [jax.ShapeDtypeStruct((2 * D_FF, D_MODEL), BF16), jax.ShapeDtypeStruct((D_FF, D_MODEL), BF16)]
    ffn_gathers, behind = [], g_little
    for l in range(DEPTH):
        handle, behind = _send_start([w_up_b[l], w_down_b[l]], ffn_lands,
                                     [(0, whole, 0, _row_block(_UP_SHARD)), (1, whole, 1, _row_block(down_rows))],
                                     name=f"gather_ffn{l}_start", after=behind)
        ffn_gathers.append(handle)

    def ffn_weights(l, after):
        return _send_wait(ffn_gathers[l], after, name=f"gather_ffn{l}_wait")

    w_in_p = _assemble_w_in(g_win, name="assemble_w_in")
    full = {n: _gathered_to_full(st, s, a)
            for (n, s, a), st in zip(little, _unpack(g_little, little_shards, lead=(N_DEV,)))}
    f0 = _QKV + 2 * CONV_CH
    wt = dict(
        meta=full["meta_tokens"], g_even=mix_norm_even, w_in_p=w_in_p, wf_t=w_in_p[:, f0:f0 + FOX_HEADS].T,
        b_f=b_f.reshape(FOX_HEADS, 1), conv_w=full["conv_w"][0], conv_b=conv_b, ln_g=ln_g, ln_b=ln_b, w_out=g_wout,
        g_odd=full["mix_norm_odd"], pool_w=g_poolw, pool_b=full["pool_b"].reshape(1, D_MODEL),
        pool_scale=full["pool_scale"], ffn_norm=ffn_norm, ffn_weights=ffn_weights, ffn_started=behind,
        fcw3=full["ffn_conv_w"].reshape(DEPTH, FFN_CONV_WIDTH, 2, D_FF).transpose(0, 2, 1, 3),
        fcb3=ffn_conv_b.reshape(DEPTH, 2, 1, D_FF), g_final=final_norm.reshape(1, D_MODEL))

    loss_part, grad_x, g = _local_step(x[0], loss_target[0], wt)

    small = _pack([loss_part[:, 0:1]] + [g[n] for n, _ in replicated] + [g["meta_tokens"]], F32, align=8)
    (everyone,) = _exchange([small], [jax.ShapeDtypeStruct((N_DEV,) + small.shape, F32)],
                            [(0, lambda ref, slot: ref, 0, _by_owner)], name="exchange_small")
    summed = _unpack(_sum_slabs(everyone, name="sum_small"),
                     [(1, 1)] + [s for _, s in replicated] + [(N_META, D_MODEL)])
    loss = summed[0].reshape(())
    grads = {n: gr for (n, _), gr in zip(replicated, summed[1:-1])}
    grads["meta_tokens"] = lax.dynamic_slice_in_dim(summed[-1], me * out_rows, out_rows, 1)

    delta, new_m, new_v = {}, {}, {}
    send0, send1, send_rest, send_in = g["sends"]
    ffn_lands = [_send_wait(send, everyone, name=f"wait_ffn{l}") for l, send in enumerate((send0, send1))]
    own_up = [lax.dynamic_index_in_dim(d, me, 0, keepdims=False) for d in g["w_up"]]
    own_down = [lax.dynamic_slice_in_dim(d, me * down_rows, down_rows, 0) for d in g["w_down"]]
    for n, owns, idx in (("w_up", own_up, 0), ("w_down", own_down, 1)):
        grads[n], delta[n], new_m[n], new_v[n] = _adamw_layers(
            w_loc[n], owns, [ffn_lands[l][idx] for l in range(DEPTH)], m_loc[n], v_loc[n], _ADAM_ROWS[n],
            name=f"adamw_{n}")
    for d in (grads, delta, new_m, new_v):
        d["w_up"] = up_t(d["w_up"])
    land_out, land_pool, land_little = _send_wait(send_rest, delta["w_down"], name="wait_rest")
    (land_in,) = _send_wait(send_in, land_out, name="wait_w_in")
    pool_2d = (n_groups * pool_rows, POOL_GROUP)
    own_pool = lax.dynamic_slice_in_dim(g["pool_w"], me * pool_rows, pool_rows, 1)
    for n, own, land, shp in (
            ("w_in", lax.dynamic_index_in_dim(g["w_in"], me, 0, keepdims=False), land_in, w_in.shape),
            ("w_out", lax.dynamic_slice_in_dim(g["w_out"], me * out_rows, out_rows, 0), land_out, w_out.shape),
            ("pool_w", own_pool.reshape(pool_2d), land_pool.reshape((N_DEV - 1,) + pool_2d), (1,) + pool_2d)):
        outs = _adamw_layers(w_loc[n].reshape(shp), [own], [land], m_loc[n].reshape(shp), v_loc[n].reshape(shp),
                             _ADAM_ROWS[n], name=f"adamw_{n}")
        grads[n], delta[n], new_m[n], new_v[n] = (o.reshape(w_loc[n].shape) for o in outs)
    own_little = lax.dynamic_index_in_dim(g["little_slabs"], me, 0, keepdims=False)
    g_little = _unpack(_sum_slabs(land_little, own=own_little, name="sum_little"),
                       [_shard_shape(s, a) for _, s, a in _LITTLE])
    grads.update({n: gl for (n, _, _), gl in zip(_LITTLE, g_little)})
    two_d = lambda shp: (_size(shp[:-1]), shp[-1])
    for n in names:
        if n in _MATMUL_WEIGHTS:
            continue
        shp = w_loc[n].shape
        d, mn, vn = _adamw(w_loc[n].reshape(two_d(shp)), grads[n].reshape(two_d(shp)), m_loc[n].reshape(two_d(shp)),
                           v_loc[n].reshape(two_d(shp)), name=f"adamw_{n}")
        delta[n], new_m[n], new_v[n] = d.reshape(shp), mn.reshape(shp), vn.reshape(shp)
    return (loss, grad_x[None], *[grads[n] for n in names], *[delta[n] for n in names],
            *[new_m[n] for n in names], *[new_v[n] for n in names])
```

```python
import functools

import jax
import jax.numpy as jnp
from jax import lax
from jax.experimental import pallas as pl
from jax.experimental.pallas import tpu as pltpu

F32 = jnp.float32
BF16 = jnp.bfloat16

N_DEV = 8
DEPTH = 2
D_MODEL = 1024
N_META = 16
FOX_HEADS = 8
FOX_HEAD_DIM = 64
FOX_WIDTH = 512
CONV_CH = 512
CONV_WIDTH = 31
POOL_WINDOWS = (2, 4, 8, 16)
POOL_GROUP = 256
D_FF = 2816
FFN_CONV_WIDTH = 3
RMS_EPS = 1e-6
LN_EPS = 1e-5
ADAM_LR = 0.001
ADAM_B1 = 0.9
ADAM_B2 = 0.999
ADAM_EPS = 1e-08
ADAM_WD = 0.01
ADAM_STEP = 10

CHUNK = 128
HALO = 32
NEG_BIG = -1e30
FLAT_W = 1024


def _round_up(n, m):
    return (n + m - 1) // m * m


def _sigmoid(x):
    return 1.0 / (1.0 + jnp.exp(-x))


def _fold8(p):
    acc = p[0:8, :]
    for r in range(1, p.shape[0] // 8):
        acc = acc + p[8 * r:8 * r + 8, :]
    return acc


def _mm(a, b, *, name, tb=False, tm=None, tn=None, tk=None, out_dtype=F32, res=None,
        a_map=None, b_map=None, o_map=None, out_shape=None, dims=None, after=None):
    if dims is None:
        m, k = a.shape
        n = b.shape[-2] if tb else b.shape[-1]
    else:
        m, n, k = dims
    tm, tn, tk = tm or m, tn or n, tk or k
    assert m % tm == 0 and n % tn == 0 and k % tk == 0, (name, m, n, k, tm, tn, tk)
    nk = k // tk
    a_map = a_map or (lambda i, j, kk: (i, kk))
    b_map = b_map or ((lambda i, j, kk: (j, kk)) if tb else (lambda i, j, kk: (kk, j)))
    o_map = o_map or (lambda i, j, kk: (i, j))
    out_shape = out_shape or (m, n)
    contract = (((1,), (1,)), ((), ())) if tb else (((1,), (0,)), ((), ()))
    has_res = res is not None

    def body(*refs):
        a_ref, b_ref = refs[0], refs[1]
        res_ref = refs[2] if has_res else None
        o_ref = refs[2 + has_res + (after is not None)]
        p = lax.dot_general(a_ref[...], b_ref[...], contract, preferred_element_type=F32)
        if nk == 1:
            if has_res:
                p = p + res_ref[...]
            o_ref[...] = p.astype(o_ref.dtype)
        else:
            acc_ref = refs[-1]
            kk = pl.program_id(2)

            @pl.when(kk == 0)
            def _():
                acc_ref[...] = p

            @pl.when(kk > 0)
            def _():
                acc_ref[...] += p

            @pl.when(kk == nk - 1)
            def _():
                r = acc_ref[...]
                if has_res:
                    r = r + res_ref[...]
                o_ref[...] = r.astype(o_ref.dtype)

    in_specs = [pl.BlockSpec((tm, tk), a_map), pl.BlockSpec((tn, tk) if tb else (tk, tn), b_map)]
    operands = [a, b]
    if has_res:
        in_specs.append(pl.BlockSpec((tm, tn), o_map))
        operands.append(res)
    if after is not None:
        in_specs.append(pl.BlockSpec(memory_space=pl.ANY))
        operands.append(after)
    return pl.pallas_call(
        body, name=name, grid=(m // tm, n // tn, nk),
        in_specs=in_specs, out_specs=pl.BlockSpec((tm, tn), o_map),
        out_shape=jax.ShapeDtypeStruct(out_shape, out_dtype),
        scratch_shapes=[pltpu.VMEM((tm, tn), F32)] if nk > 1 else [],
    )(*operands)


def _transpose(x, *, name, out_dtype, cols=None, after=None):
    r, c = x.shape
    cols = cols or c
    assert r % CHUNK == 0
    behind = [] if after is None else [after]

    def body(x_ref, *rest):
        o_ref = rest[-1]
        o_ref[...] = x_ref[...].astype(F32).T.astype(o_ref.dtype)

    return pl.pallas_call(
        body, name=name, grid=(r // CHUNK,),
        in_specs=[pl.BlockSpec((CHUNK, cols), lambda i: (i, 0))] + [pl.BlockSpec(memory_space=pl.ANY)] * len(behind),
        out_specs=pl.BlockSpec((cols, CHUNK), lambda i: (0, i)),
        out_shape=jax.ShapeDtypeStruct((cols, r), out_dtype),
    )(x, *behind)


def _rms_fwd(x, g, *, name, out_dtype, after=None, transposed=False):
    lp, dm = x.shape
    tr = CHUNK if transposed else lp // 4
    behind = [] if after is None else [after]

    def body(x_ref, g_ref, *rest):
        xv = x_ref[...]
        r = lax.rsqrt(jnp.mean(xv * xv, axis=-1, keepdims=True) + RMS_EPS)
        y = xv * r * g_ref[...]
        if transposed:
            rest[-2][...] = y.astype(out_dtype)
            rest[-1][...] = y.T.astype(out_dtype)
        else:
            rest[-1][...] = y.astype(out_dtype)

    row = pl.BlockSpec((tr, dm), lambda i: (i, 0))
    out_specs, out_shape = row, jax.ShapeDtypeStruct((lp, dm), out_dtype)
    if transposed:
        out_specs = [row, pl.BlockSpec((dm, tr), lambda i: (0, i))]
        out_shape = [out_shape, jax.ShapeDtypeStruct((dm, lp), out_dtype)]
    return pl.pallas_call(
        body, name=name, grid=(lp // tr,),
        in_specs=[row, pl.BlockSpec((1, dm), lambda i: (0, 0))] + [pl.BlockSpec(memory_space=pl.ANY)] * len(behind),
        out_specs=out_specs, out_shape=out_shape,
    )(x, g, *behind)


def _rms_bwd(x, g, dn, dres, *, name):
    lp, dm = x.shape
    tr = lp // 4

    def body(x_ref, g_ref, dn_ref, dres_ref, dh_ref, dhb_ref, dg_ref):
        xv = x_ref[...]
        r = lax.rsqrt(jnp.mean(xv * xv, axis=-1, keepdims=True) + RMS_EPS)
        xhat = xv * r
        dnv = dn_ref[...]

        @pl.when(pl.program_id(0) == 0)
        def _():
            dg_ref[...] = jnp.zeros_like(dg_ref)

        dg_ref[...] += jnp.sum(dnv * xhat, axis=0, keepdims=True)
        dxhat = dnv * g_ref[...]
        dx = r * (dxhat - xhat * jnp.mean(dxhat * xhat, axis=-1, keepdims=True))
        dh = dres_ref[...] + dx
        dh_ref[...] = dh
        dhb_ref[...] = dh.astype(BF16)

    row = pl.BlockSpec((tr, dm), lambda i: (i, 0))
    vec = pl.BlockSpec((1, dm), lambda i: (0, 0))
    return pl.pallas_call(
        body, name=name, grid=(lp // tr,),
        in_specs=[row, vec, row, row], out_specs=[row, row, vec],
        out_shape=[jax.ShapeDtypeStruct((lp, dm), F32), jax.ShapeDtypeStruct((lp, dm), BF16),
                   jax.ShapeDtypeStruct((1, dm), F32)],
    )(x, g, dn, dres)


def _loss_head(h, g, tgt, n_real, *, name):
    lp, dm = h.shape
    tr = lp // 4

    def body(x_ref, g_ref, t_ref, loss_ref, dh_ref, dhb_ref, dg_ref):
        i = pl.program_id(0)
        xv = x_ref[...]
        r = lax.rsqrt(jnp.mean(xv * xv, axis=-1, keepdims=True) + RMS_EPS)
        xhat = xv * r
        gv = g_ref[...]
        y = xhat * gv
        t = i * tr + lax.broadcasted_iota(jnp.int32, (tr, 1), 0)
        valid = (t >= N_META) & (t < n_real)
        diff = jnp.where(valid, y - t_ref[...], 0.0)

        @pl.when(i == 0)
        def _():
            loss_ref[...] = jnp.zeros_like(loss_ref)
            dg_ref[...] = jnp.zeros_like(dg_ref)

        row_sq = jnp.sum(diff * diff, axis=-1, keepdims=True) * (1.0 / dm)
        part = 0.5 * jnp.sum(row_sq, axis=0, keepdims=True)
        loss_ref[...] += jnp.broadcast_to(part, loss_ref.shape)
        dy = diff * (1.0 / dm)
        dg_ref[...] += jnp.sum(dy * xhat, axis=0, keepdims=True)
        dxhat = dy * gv
        dx = r * (dxhat - xhat * jnp.mean(dxhat * xhat, axis=-1, keepdims=True))
        dh_ref[...] = dx
        dhb_ref[...] = dx.astype(BF16)

    row = pl.BlockSpec((tr, dm), lambda i: (i, 0))
    vec = pl.BlockSpec((1, dm), lambda i: (0, 0))
    return pl.pallas_call(
        body, name=name, grid=(lp // tr,),
        in_specs=[row, vec, row],
        out_specs=[pl.BlockSpec((1, 128), lambda i: (0, 0)), row, row, vec],
        out_shape=[jax.ShapeDtypeStruct((1, 128), F32), jax.ShapeDtypeStruct((lp, dm), F32),
                   jax.ShapeDtypeStruct((lp, dm), BF16), jax.ShapeDtypeStruct((1, dm), F32)],
    )(h, g, tgt)


def _tri(upper):
    r = lax.broadcasted_iota(jnp.int32, (CHUNK, CHUNK), 0)
    c = lax.broadcasted_iota(jnp.int32, (CHUNK, CHUNK), 1)
    return jnp.where(r <= c if upper else r >= c, 1.0, 0.0).astype(F32)


def _fox_prep(f_t, b_f, *, name):
    nh, lp = f_t.shape
    nch = lp // CHUNK

    def body(f_ref, b_ref, c_ref):
        tri = _tri(True)
        carry = jnp.zeros((nh, 1), F32)
        for blk in range(nch):
            cols = slice(blk * CHUNK, (blk + 1) * CHUNK)
            z = f_ref[:, cols] + b_ref[...]
            logf = jnp.minimum(z, 0.0) - jnp.log(1.0 + jnp.exp(-jnp.abs(z)))
            cb = jnp.dot(logf, tri, preferred_element_type=F32, precision=lax.Precision.HIGHEST)
            c_ref[:, cols] = cb + carry
            carry = carry + jnp.sum(logf, axis=1, keepdims=True)

    return pl.pallas_call(
        body, name=name, out_shape=jax.ShapeDtypeStruct((nh, lp), F32),
    )(f_t, b_f)


def _fox_bwd(dc, f_t, b_f, *, name):
    nh, lp = f_t.shape
    nch = lp // CHUNK

    def body(dc_ref, f_ref, b_ref, df_ref, db_ref):
        tri = _tri(False)
        carry = jnp.zeros((nh, 1), F32)
        db = jnp.zeros((nh, 1), F32)
        df_ref[...] = jnp.zeros_like(df_ref)
        for blk in reversed(range(nch)):
            cols = slice(blk * CHUNK, (blk + 1) * CHUNK)
            dcb = dc_ref[:, cols]
            dlogf = jnp.dot(dcb, tri, preferred_element_type=F32, precision=lax.Precision.HIGHEST) + carry
            carry = carry + jnp.sum(dcb, axis=1, keepdims=True)
            z = f_ref[:, cols] + b_ref[...]
            dz = dlogf * _sigmoid(-z)
            df_ref[0:nh, cols] = dz
            db = db + jnp.sum(dz, axis=1, keepdims=True)
        db_ref[...] = db

    return pl.pallas_call(
        body, name=name,
        out_shape=[jax.ShapeDtypeStruct((128, lp), F32), jax.ShapeDtypeStruct((nh, 1), F32)],
    )(dc, f_t, b_f)


def _attn_blocks(lp):
    tq = lp // 4
    return tq, [(i * tq, min(lp, _round_up((i + 1) * tq, CHUNK))) for i in range(4)]


ATTN_SCALE = FOX_HEAD_DIM ** -0.5


def _attn_probs(q2s, k_h, c_row, row0, n):
    tq = q2s.shape[0]
    lo = row0 // CHUNK * CHUNK
    logits = []
    for c0, c1 in ([(0, lo)] if lo else []) + [(lo, n)]:
        s = lax.dot_general(q2s, k_h[c0:c1], (((1,), (1,)), ((), ())), preferred_element_type=F32) - c_row[:, c0:c1]
        if c1 > row0:
            t = row0 + lax.broadcasted_iota(jnp.int32, (tq, c1 - c0), 0)
            sidx = c0 + lax.broadcasted_iota(jnp.int32, (tq, c1 - c0), 1)
            s = jnp.where(sidx <= t, s, NEG_BIG)
        logits.append((s, c0, c1))
    m = functools.reduce(jnp.maximum, [jnp.max(s, axis=1, keepdims=True) for s, _, _ in logits])
    ps = [(jnp.exp(s - m), c0, c1) for s, c0, c1 in logits]
    inv = 1.0 / sum(jnp.sum(p, axis=1, keepdims=True) for p, _, _ in ps)
    return [(p * inv, c0, c1) for p, c0, c1 in ps]


def _attn_fwd(qkv, c3, *, name):
    lp = qkv.shape[0]
    tq, blocks = _attn_blocks(lp)

    def body(q_ref, k_ref, v_ref, c_ref, o_ref):
        lane = lax.broadcasted_iota(jnp.int32, (1, 128), 1)
        zero = jnp.zeros((), BF16)
        for i, (row0, n) in enumerate(blocks):
            q2s = q_ref[row0:row0 + tq, :] * ATTN_SCALE
            acc = jnp.zeros((tq, 128), F32)
            for hd in range(2):
                sel = (lane < 64) if hd == 0 else (lane >= 64)
                k_h = jnp.where(sel, k_ref[0:n, :], zero)
                v_h = jnp.where(sel, v_ref[0:n, :], zero)
                for p, c0, c1 in _attn_probs(q2s, k_h, c_ref[hd:hd + 1, 0:n], row0, n):
                    acc = acc + jnp.dot(p.astype(BF16), v_h[c0:c1], preferred_element_type=F32)
            o_ref[row0:row0 + tq, :] = acc.astype(BF16)

    blk = lambda off: pl.BlockSpec((lp, 128), lambda p: (0, off + p))
    return pl.pallas_call(
        body, name=name, grid=(4,),
        in_specs=[blk(0), blk(4), blk(8), pl.BlockSpec((None, 2, lp), lambda p: (p, 0, 0))],
        out_specs=pl.BlockSpec((lp, 128), lambda p: (0, p)),
        out_shape=jax.ShapeDtypeStruct((lp, FOX_WIDTH), BF16),
    )(qkv, qkv, qkv, c3)


def _attn_bwd(qkv, q_t, dcat, do_t, c3, *, name):
    lp = qkv.shape[0]
    tq, blocks = _attn_blocks(lp)
    scale = FOX_HEAD_DIM ** -0.5

    def body(q_ref, k_ref, v_ref, qt_ref, do_ref, dot_ref, c_ref, dq_ref, dkt_ref, dvt_ref, dc_ref,
             dkt_acc, dvt_acc):
        lane = lax.broadcasted_iota(jnp.int32, (1, 128), 1)
        sub = lax.broadcasted_iota(jnp.int32, (128, 1), 0)
        zero = jnp.zeros((), BF16)
        dkt_acc[...] = jnp.zeros_like(dkt_acc)
        dvt_acc[...] = jnp.zeros_like(dvt_acc)
        dc_ref[...] = jnp.zeros_like(dc_ref)
        for i, (row0, n) in enumerate(blocks):
            rows = slice(row0, row0 + tq)
            q2s = q_ref[rows, :] * ATTN_SCALE
            do2 = do_ref[rows, :].astype(BF16)
            dq_acc = jnp.zeros((tq, 128), F32)
            for hd in range(2):
                sel = (lane < 64) if hd == 0 else (lane >= 64)
                sel_t = (sub < 64) if hd == 0 else (sub >= 64)
                k_h = jnp.where(sel, k_ref[0:n, :], zero)
                v_h = jnp.where(sel, v_ref[0:n, :], zero)
                qt_h = jnp.where(sel_t, qt_ref[:, rows], zero)
                dot_h = jnp.where(sel_t, dot_ref[:, rows], zero)
                segs = [(p, lax.dot_general(do2, v_h[c0:c1], (((1,), (1,)), ((), ())), preferred_element_type=F32),
                         c0, c1) for p, c0, c1 in _attn_probs(q2s, k_h, c_ref[hd:hd + 1, 0:n], row0, n)]
                delta = sum(jnp.sum(p * dp, axis=1, keepdims=True) for p, dp, _, _ in segs)
                for p, dp, c0, c1 in segs:
                    ds = p * (dp - delta)
                    dsb = ds.astype(BF16)
                    dq_acc = dq_acc + jnp.dot(dsb, k_h[c0:c1], preferred_element_type=F32)
                    dkt_acc[:, c0:c1] += jnp.dot(qt_h, dsb, preferred_element_type=F32)
                    dvt_acc[:, c0:c1] += jnp.dot(dot_h, p.astype(BF16), preferred_element_type=F32)
                    dc_ref[hd:hd + 1, c0:c1] -= jnp.sum(ds, axis=0, keepdims=True)
            dq_ref[rows, :] = (dq_acc * scale).astype(BF16)
        dkt_ref[...] = (dkt_acc[...] * scale).astype(BF16)
        dvt_ref[...] = dvt_acc[...].astype(BF16)

    blk = lambda off: pl.BlockSpec((lp, 128), lambda p: (0, off + p))
    blk_t = pl.BlockSpec((128, lp), lambda p: (p, 0))
    c_spec = pl.BlockSpec((None, 2, lp), lambda p: (p, 0, 0))
    return pl.pallas_call(
        body, name=name, grid=(4,),
        in_specs=[blk(0), blk(4), blk(8), blk_t, blk(0), blk_t, c_spec],
        out_specs=[blk(0), blk_t, blk_t, c_spec],
        out_shape=[jax.ShapeDtypeStruct((lp, FOX_WIDTH), BF16), jax.ShapeDtypeStruct((FOX_WIDTH, lp), BF16),
                   jax.ShapeDtypeStruct((FOX_WIDTH, lp), BF16), jax.ShapeDtypeStruct((4, 2, lp), F32)],
        scratch_shapes=[pltpu.VMEM((128, lp), F32), pltpu.VMEM((128, lp), F32)],
    )(qkv, qkv, qkv, q_t, dcat, do_t, c3)


def _ln_stats(x):
    mu = jnp.mean(x, axis=-1, keepdims=True)
    xc = x - mu
    var = jnp.mean(xc * xc, axis=-1, keepdims=True)
    rstd = lax.rsqrt(var + LN_EPS)
    return xc * rstd, rstd


def _conv_fwd(agf, conv_w, conv_b, ln_g, ln_b, *, name):
    lp = agf.shape[0]
    nch = lp // CHUNK
    c = CONV_CH

    def body(a_ref, g_ref, w_ref, b_ref, lg_ref, lb_ref, u0_ref, u1_ref, u3_ref, u0s):
        u0s[0:HALO, :] = jnp.zeros((HALO, c), F32)

        def glu(ci, _):
            rows = pl.ds(pl.multiple_of(ci * CHUNK, CHUNK), CHUNK)
            u0 = a_ref[rows, :] * _sigmoid(g_ref[rows, :])
            u0_ref[rows, :] = u0
            u0s[pl.ds(pl.multiple_of(ci * CHUNK + HALO, 8), CHUNK), :] = u0
            return 0

        lax.fori_loop(0, nch, glu, 0)

        def conv(ci, _):
            r0 = pl.multiple_of(ci * CHUNK, CHUNK)
            rows = pl.ds(r0, CHUNK)
            for lg in range(c // 128):
                lanes = slice(lg * 128, (lg + 1) * 128)
                win = u0s[pl.ds(r0, CHUNK + HALO), lanes]
                acc = jnp.broadcast_to(b_ref[:, lanes], (CHUNK, 128))
                for k in range(CONV_WIDTH):
                    s = CONV_WIDTH - 1 - k
                    sh = win if s == 0 else pltpu.roll(win, s, 0)
                    acc = acc + w_ref[k:k + 1, lanes] * sh[HALO:HALO + CHUNK, :]
                u1_ref[rows, lanes] = acc
            xhat, _ = _ln_stats(u1_ref[rows, :])
            y = xhat * lg_ref[...] + lb_ref[...]
            u3_ref[rows, :] = (y * _sigmoid(y)).astype(BF16)
            return 0

        lax.fori_loop(0, nch, conv, 0)

    full = lambda shape: pl.BlockSpec(shape, lambda i: (0, 0))
    return pl.pallas_call(
        body, name=name, grid=(1,),
        in_specs=[pl.BlockSpec((lp, c), lambda i: (0, 0)), pl.BlockSpec((lp, c), lambda i: (0, 1)),
                  full((CONV_WIDTH, c)), full((1, c)), full((1, c)), full((1, c))],
        out_specs=[full((lp, c)), full((lp, c)), full((lp, c))],
        out_shape=[jax.ShapeDtypeStruct((lp, c), F32), jax.ShapeDtypeStruct((lp, c), F32),
                   jax.ShapeDtypeStruct((lp, c), BF16)],
        scratch_shapes=[pltpu.VMEM((lp + HALO, c), F32)],
    )(agf, agf, conv_w, conv_b, ln_g, ln_b)


def _conv_bwd(dcat, u0, u1, agf, conv_w, ln_g, ln_b, *, name):
    lp = agf.shape[0]
    nch = lp // CHUNK
    c = CONV_CH
    wlen = CHUNK + HALO

    def body(du3_ref, u0_ref, u1_ref, a_ref, g_ref, w_ref, lg_ref, lb_ref,
             dag_ref, dw_ref, db_ref, dlg_ref, dlb_ref, du1s, dwacc, vacc):
        du1s[lp:lp + HALO, :] = jnp.zeros((HALO, c), F32)
        dwacc[...] = jnp.zeros_like(dwacc)
        vacc[...] = jnp.zeros_like(vacc)

        def ln_bwd(ci, _):
            r0 = pl.multiple_of(ci * CHUNK, CHUNK)
            rows = pl.ds(r0, CHUNK)
            xhat, rstd = _ln_stats(u1_ref[rows, :])
            y = xhat * lg_ref[...] + lb_ref[...]
            sg = _sigmoid(y)
            du2 = du3_ref[rows, :] * (sg * (1.0 + y * (1.0 - sg)))
            vacc[0:8, :] += _fold8(du2 * xhat)
            vacc[8:16, :] += _fold8(du2)
            dxhat = du2 * lg_ref[...]
            du1 = rstd * (dxhat - jnp.mean(dxhat, axis=-1, keepdims=True)
                          - xhat * jnp.mean(dxhat * xhat, axis=-1, keepdims=True))
            vacc[16:24, :] += _fold8(du1)
            du1s[rows, :] = du1
            return 0

        lax.fori_loop(0, nch, ln_bwd, 0)

        def conv_bwd(ci, _):
            r0 = pl.multiple_of(ci * CHUNK, CHUNK)
            rows = pl.ds(r0, CHUNK)
            for lg in range(c // 128):
                lanes = slice(lg * 128, (lg + 1) * 128)
                dwin = du1s[pl.ds(r0, wlen), lanes]
                u0 = u0_ref[rows, lanes]
                acc = jnp.zeros((CHUNK, 128), F32)
                for k in range(CONV_WIDTH):
                    s = CONV_WIDTH - 1 - k
                    d_s = (dwin if s == 0 else pltpu.roll(dwin, wlen - s, 0))[0:CHUNK, :]
                    acc = acc + w_ref[k:k + 1, lanes] * d_s
                    dwacc[8 * k:8 * k + 8, lanes] += _fold8(d_s * u0)
                sg = _sigmoid(g_ref[rows, lanes])
                a = a_ref[rows, lanes]
                dag_ref[rows, lanes] = (acc * sg).astype(BF16)
                dag_ref[rows, slice(c + lg * 128, c + (lg + 1) * 128)] = (acc * a * sg * (1.0 - sg)).astype(BF16)
            return 0

        lax.fori_loop(0, nch, conv_bwd, 0)
        for k in range(CONV_WIDTH):
            dw_ref[k:k + 1, :] = jnp.sum(dwacc[8 * k:8 * k + 8, :], axis=0, keepdims=True)
        dlg_ref[...] = jnp.sum(vacc[0:8, :], axis=0, keepdims=True)
        dlb_ref[...] = jnp.sum(vacc[8:16, :], axis=0, keepdims=True)
        db_ref[...] = jnp.sum(vacc[16:24, :], axis=0, keepdims=True)

    full = lambda shape: pl.BlockSpec(shape, lambda i: (0, 0))
    vec = jax.ShapeDtypeStruct((1, c), F32)
    return pl.pallas_call(
        body, name=name, grid=(1,),
        in_specs=[pl.BlockSpec((lp, c), lambda i: (0, 1)), full((lp, c)), full((lp, c)),
                  pl.BlockSpec((lp, c), lambda i: (0, 0)), pl.BlockSpec((lp, c), lambda i: (0, 1)),
                  full((CONV_WIDTH, c)), full((1, c)), full((1, c))],
        out_specs=[full((lp, 2 * c)), full((CONV_WIDTH, c)), full((1, c)), full((1, c)), full((1, c))],
        out_shape=[jax.ShapeDtypeStruct((lp, 2 * c), BF16), jax.ShapeDtypeStruct((CONV_WIDTH, c), F32), vec, vec, vec],
        scratch_shapes=[pltpu.VMEM((lp + HALO, c), F32), pltpu.VMEM((8 * CONV_WIDTH, c), F32),
                        pltpu.VMEM((24, c), F32)],
    )(dcat, u0, u1, agf, agf, conv_w, ln_g, ln_b)


FFN_TILE = 256
FFN_PAD = 8


def _ffn_conv(xs, w_ref, b_ref, half, r0):
    win = xs[half, pl.ds(r0, CHUNK + FFN_PAD), :]
    acc = jnp.broadcast_to(b_ref[half], (CHUNK, FFN_TILE))
    for k in range(FFN_CONV_WIDTH):
        s = FFN_CONV_WIDTH - 1 - k
        sh = win if s == 0 else pltpu.roll(win, s, 0)
        acc = acc + w_ref[half, k:k + 1, :] * sh[FFN_PAD:FFN_PAD + CHUNK, :]
    return acc


def _ffn_act_fwd(up3, w3, b3, *, name):
    _, lp, f = up3.shape
    nch = lp // CHUNK

    def body(up_ref, w_ref, b_ref, act_ref, act_t_ref, gv_ref, xs):
        for half in range(2):
            xs[half, 0:FFN_PAD, :] = jnp.zeros((FFN_PAD, FFN_TILE), F32)
            xs[half, FFN_PAD:FFN_PAD + lp, :] = up_ref[half]

        def chunk(ci, _):
            r0 = pl.multiple_of(ci * CHUNK, CHUNK)
            rows = pl.ds(r0, CHUNK)
            gate = _ffn_conv(xs, w_ref, b_ref, 0, r0)
            val = _ffn_conv(xs, w_ref, b_ref, 1, r0)
            gv_ref[0, rows, :] = gate
            gv_ref[1, rows, :] = val
            act = gate * _sigmoid(gate) * val
            act_ref[rows, :] = act.astype(BF16)
            act_t_ref[:, rows] = act.T.astype(BF16)
            return 0

        lax.fori_loop(0, nch, chunk, 0)

    halves = pl.BlockSpec((2, lp, FFN_TILE), lambda j: (0, 0, j))
    return pl.pallas_call(
        body, name=name, grid=(f // FFN_TILE,),
        in_specs=[halves, pl.BlockSpec((2, FFN_CONV_WIDTH, FFN_TILE), lambda j: (0, 0, j)),
                  pl.BlockSpec((2, 1, FFN_TILE), lambda j: (0, 0, j))],
        out_specs=[pl.BlockSpec((lp, FFN_TILE), lambda j: (0, j)), pl.BlockSpec((FFN_TILE, lp), lambda j: (j, 0)),
                   halves],
        out_shape=[jax.ShapeDtypeStruct((lp, f), BF16), jax.ShapeDtypeStruct((f, lp), BF16),
                   jax.ShapeDtypeStruct((2, lp, f), F32)],
        scratch_shapes=[pltpu.VMEM((2, lp + FFN_PAD, FFN_TILE), F32)],
    )(up3, w3, b3)


def _ffn_act_bwd(up3, gv3, w3, dact, *, name):
    _, lp, f = up3.shape
    nch = lp // CHUNK
    wlen = CHUNK + FFN_PAD

    def body(up_ref, gv_ref, w_ref, dact_ref, dup_ref, dw_ref, db_ref, ds, wacc):
        for half in range(2):
            ds[half, lp:lp + FFN_PAD, :] = jnp.zeros((FFN_PAD, FFN_TILE), F32)
        wacc[...] = jnp.zeros_like(wacc)

        def act_bwd(ci, _):
            rows = pl.ds(pl.multiple_of(ci * CHUNK, CHUNK), CHUNK)
            gate, val = gv_ref[0, rows, :], gv_ref[1, rows, :]
            sg = _sigmoid(gate)
            da = dact_ref[rows, :]
            ds[0, rows, :] = da * val * (sg * (1.0 + gate * (1.0 - sg)))
            ds[1, rows, :] = da * (gate * sg)
            return 0

        lax.fori_loop(0, nch, act_bwd, 0)

        def conv_bwd(ci, _):
            r0 = pl.multiple_of(ci * CHUNK, CHUNK)
            rows = pl.ds(r0, CHUNK)
            for half in range(2):
                dwin = ds[half, pl.ds(r0, wlen), :]
                x = up_ref[half, rows, :]
                acc = jnp.zeros((CHUNK, FFN_TILE), F32)
                for k in range(FFN_CONV_WIDTH):
                    s = FFN_CONV_WIDTH - 1 - k
                    d_s = (dwin if s == 0 else pltpu.roll(dwin, wlen - s, 0))[0:CHUNK, :]
                    acc = acc + w_ref[half, k:k + 1, :] * d_s
                    wacc[half, 8 * k:8 * k + 8, :] += _fold8(d_s * x)
                wacc[half, 24:32, :] += _fold8(dwin[0:CHUNK, :])
                dup_ref[half, rows, :] = acc.astype(BF16)
            return 0

        lax.fori_loop(0, nch, conv_bwd, 0)
        for half in range(2):
            for k in range(FFN_CONV_WIDTH):
                dw_ref[half, k:k + 1, :] = jnp.sum(wacc[half, 8 * k:8 * k + 8, :], axis=0, keepdims=True)
            db_ref[half] = jnp.sum(wacc[half, 24:32, :], axis=0, keepdims=True)

    halves = pl.BlockSpec((2, lp, FFN_TILE), lambda j: (0, 0, j))
    taps = pl.BlockSpec((2, FFN_CONV_WIDTH, FFN_TILE), lambda j: (0, 0, j))
    bias = pl.BlockSpec((2, 1, FFN_TILE), lambda j: (0, 0, j))
    return pl.pallas_call(
        body, name=name, grid=(f // FFN_TILE,),
        in_specs=[halves, halves, taps, pl.BlockSpec((lp, FFN_TILE), lambda j: (0, j))],
        out_specs=[halves, taps, bias],
        out_shape=[jax.ShapeDtypeStruct((2, lp, f), BF16), jax.ShapeDtypeStruct((2, FFN_CONV_WIDTH, f), F32),
                   jax.ShapeDtypeStruct((2, 1, f), F32)],
        scratch_shapes=[pltpu.VMEM((2, lp + FFN_PAD, FFN_TILE), F32), pltpu.VMEM((2, 32, FFN_TILE), F32)],
    )(up3, gv3, w3, dact)


POOL_PAD = 16


def _inv_count(r0, w):
    t = r0 + lax.broadcasted_iota(jnp.int32, (CHUNK, 1), 0)
    return 1.0 / jnp.minimum(t + 1, w).astype(F32)


def _pool_fwd(n, pool_w, pool_b, pool_scale, h, *, name):
    lp, dm = n.shape
    nch = lp // CHUNK
    g = POOL_GROUP

    def body(n_ref, w_ref, b_ref, s_ref, h_ref, ho_ref, dt_ref, z_ref, xs, d_ref):
        gi = pl.program_id(0)
        xs[0:POOL_PAD, :] = jnp.zeros((POOL_PAD, g), F32)
        xs[POOL_PAD:POOL_PAD + lp, :] = n_ref[...]
        for idx, w in enumerate(POOL_WINDOWS):
            @pl.when(gi == idx)
            def _(w=w):
                def chunk(ci, _):
                    r0 = pl.multiple_of(ci * CHUNK, CHUNK)
                    win = xs[pl.ds(r0, CHUNK + POOL_PAD), :]
                    acc = win
                    for j in range(1, w):
                        acc = acc + pltpu.roll(win, j, 0)
                    x = win[POOL_PAD:POOL_PAD + CHUNK, :]
                    d = acc[POOL_PAD:POOL_PAD + CHUNK, :] * _inv_count(r0, w) - x
                    d_ref[pl.ds(r0, CHUNK), :] = d.astype(BF16)
                    dt_ref[:, pl.ds(r0, CHUNK)] = d.T.astype(BF16)
                    return 0

                lax.fori_loop(0, nch, chunk, 0)

        z = jnp.dot(d_ref[...], w_ref[...], preferred_element_type=F32) + b_ref[...]
        z_ref[...] = z
        ho_ref[...] = h_ref[...] + z * s_ref[...]

    col = pl.BlockSpec((lp, g), lambda i: (0, i))
    vec = pl.BlockSpec((1, g), lambda i: (0, i))
    return pl.pallas_call(
        body, name=name, grid=(len(POOL_WINDOWS),),
        in_specs=[col, pl.BlockSpec((None, g, g), lambda i: (i, 0, 0)), vec, vec, col],
        out_specs=[col, pl.BlockSpec((g, lp), lambda i: (i, 0)), col],
        out_shape=[jax.ShapeDtypeStruct((lp, dm), F32), jax.ShapeDtypeStruct((dm, lp), BF16),
                   jax.ShapeDtypeStruct((lp, dm), F32)],
        scratch_shapes=[pltpu.VMEM((lp + POOL_PAD, g), F32), pltpu.VMEM((lp, g), BF16)],
    )(n, pool_w, pool_b, pool_scale, h)


def _pool_bwd(dy, z, pool_w, pool_scale, *, name):
    lp, dm = dy.shape
    nch = lp // CHUNK
    g = POOL_GROUP
    wlen = CHUNK + POOL_PAD

    def body(dy_ref, z_ref, w_ref, s_ref, dn_ref, dz_ref, dsc_ref, db_ref, ys, dd):
        gi = pl.program_id(0)
        dyv = dy_ref[...]
        dsc_ref[...] = jnp.sum(dyv * z_ref[...], axis=0, keepdims=True)
        dz = dyv * s_ref[...]
        db_ref[...] = jnp.sum(dz, axis=0, keepdims=True)
        dzb = dz.astype(BF16)
        dz_ref[...] = dzb
        dd[...] = lax.dot_general(dzb, w_ref[...], (((1,), (1,)), ((), ())), preferred_element_type=F32)
        ys[lp:lp + POOL_PAD, :] = jnp.zeros((POOL_PAD, g), F32)
        for idx, w in enumerate(POOL_WINDOWS):
            @pl.when(gi == idx)
            def _(w=w):
                def scale(ci, _):
                    r0 = pl.multiple_of(ci * CHUNK, CHUNK)
                    ys[pl.ds(r0, CHUNK), :] = dd[pl.ds(r0, CHUNK), :] * _inv_count(r0, w)
                    return 0

                lax.fori_loop(0, nch, scale, 0)

                def chunk(ci, _):
                    r0 = pl.multiple_of(ci * CHUNK, CHUNK)
                    win = ys[pl.ds(r0, wlen), :]
                    acc = win
                    for j in range(1, w):
                        acc = acc + pltpu.roll(win, wlen - j, 0)
                    dn_ref[pl.ds(r0, CHUNK), :] = acc[0:CHUNK, :] - dd[pl.ds(r0, CHUNK), :]
                    return 0

                lax.fori_loop(0, nch, chunk, 0)

    col = pl.BlockSpec((lp, g), lambda i: (0, i))
    vec = pl.BlockSpec((1, g), lambda i: (0, i))
    return pl.pallas_call(
        body, name=name, grid=(len(POOL_WINDOWS),),
        in_specs=[col, col, pl.BlockSpec((None, g, g), lambda i: (i, 0, 0)), vec],
        out_specs=[col, col, vec, vec],
        out_shape=[jax.ShapeDtypeStruct((lp, dm), F32), jax.ShapeDtypeStruct((lp, dm), BF16),
                   jax.ShapeDtypeStruct((1, dm), F32), jax.ShapeDtypeStruct((1, dm), F32)],
        scratch_shapes=[pltpu.VMEM((lp + POOL_PAD, g), F32), pltpu.VMEM((lp, g), F32)],
    )(dy, z, pool_w, pool_scale)


def _ffn_fwd(h, g, w_up_t, w3, b3, w_down, tag):
    lp = h.shape[0]
    nj = D_FF // FFN_TILE
    n, n_t = _rms_fwd(h, g, name=f"rms_ffn{tag}", out_dtype=BF16, transposed=True)
    up2 = _mm(n, w_up_t, name=f"mm_up{tag}", tb=True, tn=FFN_TILE, dims=(lp, 2 * D_FF, D_MODEL),
              o_map=lambda i, j, k: (j // nj, j % nj), out_shape=(2 * lp, D_FF))
    up3 = up2.reshape(2, lp, D_FF)
    act, act_t, gv3 = _ffn_act_fwd(up3, w3, b3, name=f"ffn_act{tag}")
    h_out = _mm(act, w_down, name=f"mm_down{tag}", tn=256, res=h)
    return h_out, (n_t, up3, gv3, act_t)


def _ffn_bwd(dh, dhb, h, g, saved, w_up_t, w3, w_down, tag, after=None):
    lp = h.shape[0]
    n_t, up3, gv3, act_t = saved
    dw_down = _mm(act_t, dhb, name=f"mm_dwdown{tag}", tm=704, out_dtype=BF16)
    dact = _mm(dhb, w_down, name=f"mm_dact{tag}", tb=True, tn=256, after=after)
    dup3, dcw, dcb = _ffn_act_bwd(up3, gv3, w3, dact, name=f"ffn_act_bwd{tag}")
    dup2 = dup3.reshape(2 * lp, D_FF)
    dw_up = _mm_dw_up(n_t, dup2, name=f"mm_dwup{tag}")
    dn = _mm(dup2, w_up_t, name=f"mm_dnffn{tag}", tm=lp // 2, tk=D_FF // 2, dims=(lp, D_MODEL, 2 * D_FF),
             a_map=lambda i, j, k: (2 * (k // 2) + i, k % 2))
    dh_in, dh_in_b, dg = _rms_bwd(h, g, dn, dh, name=f"rms_bwd_ffn{tag}")
    return dh_in, dh_in_b, (dg, dw_up, dcw, dcb, dw_down)


def _local_step(x, tgt, wt):
    seq = x.shape[0]
    n_real = N_META + seq
    lp = _round_up(n_real, CHUNK)
    pad = jnp.zeros((lp - n_real, D_MODEL), F32)
    h0 = jnp.concatenate([wt["meta"], x, pad], axis=0)
    tgt_p = jnp.concatenate([jnp.zeros((N_META, D_MODEL), F32), tgt, pad], axis=0)
    w_in_p = wt["w_in_p"]

    n0, n0_t = _rms_fwd(h0, wt["g_even"], name="rms_even", out_dtype=BF16, after=wt["ffn_started"], transposed=True)
    qkv = _mm(n0, w_in_p, name="mm_qkv", tn=512, dims=(lp, 3 * FOX_WIDTH, D_MODEL), out_dtype=BF16)
    ag = _mm(n0, w_in_p, name="mm_ag", tn=512, dims=(lp, 2 * CONV_CH, D_MODEL),
             b_map=lambda i, j, k: (0, 3 + j))
    f_t = _mm(wt["wf_t"], n0, name="mm_ft", tb=True)
    c_row = _fox_prep(f_t, wt["b_f"], name="fox_prep")
    c3 = c_row.reshape(4, 2, lp)
    o = _attn_fwd(qkv, c3, name="attn_fwd")
    u0, u1, u3 = _conv_fwd(ag, wt["conv_w"], wt["conv_b"], wt["ln_g"], wt["ln_b"], name="conv_fwd")
    cat = jnp.concatenate([o, u3], axis=1)
    h1 = _mm(cat, wt["w_out"], name="mm_out", tn=256, res=h0)
    w_up0, w_down0 = wt["ffn_weights"](0, h1)
    h2, saved0 = _ffn_fwd(h1, wt["ffn_norm"][0:1], w_up0, wt["fcw3"][0], wt["fcb3"][0], w_down0, 0)

    n2 = _rms_fwd(h2, wt["g_odd"], name="rms_odd", out_dtype=F32)
    h3, dpool_t, z = _pool_fwd(n2, wt["pool_w"], wt["pool_b"], wt["pool_scale"], h2, name="pool_fwd")
    w_up1, w_down1 = wt["ffn_weights"](1, h3)
    h4, saved1 = _ffn_fwd(h3, wt["ffn_norm"][1:2], w_up1, wt["fcw3"][1], wt["fcb3"][1], w_down1, 1)

    loss, dh4, dh4b, d_gfinal = _loss_head(h4, wt["g_final"], tgt_p, n_real, name="loss_head")

    dh3, dh3b, gf1 = _ffn_bwd(dh4, dh4b, h3, wt["ffn_norm"][1:2], saved1, w_up1, wt["fcw3"][1], w_down1, 1)
    send1, token1 = _send_ffn_grads(gf1[1], gf1[4], 1)
    dn2, dzb, d_pscale, d_pb = _pool_bwd(dh3, z, wt["pool_w"], wt["pool_scale"], name="pool_bwd")
    d_pw = _mm(dpool_t, dzb, name="mm_dpoolw", tm=POOL_GROUP, tn=POOL_GROUP, dims=(D_MODEL, POOL_GROUP, lp),
               b_map=lambda i, j, k: (0, i), o_map=lambda i, j, k: (i, 0), out_shape=(D_MODEL, POOL_GROUP),
               out_dtype=BF16)
    dh2, dh2b, d_godd = _rms_bwd(h2, wt["g_odd"], dn2, dh3, name="rms_bwd_odd")
    dh1, dh1b, gf0 = _ffn_bwd(dh2, dh2b, h1, wt["ffn_norm"][0:1], saved0, w_up0, wt["fcw3"][0], w_down0, 0,
                              after=token1)

    send0, token0 = _send_ffn_grads(gf0[1], gf0[4], 0)
    cat_t = _transpose(cat, name="t_cat", out_dtype=BF16)
    d_wout = _mm(cat_t, dh1b, name="mm_dwout", tm=512, out_dtype=BF16)
    dcat = _mm(dh1b, wt["w_out"], name="mm_dcat", tb=True, tn=256, after=token0)
    dag, d_convw, d_convb, d_lng, d_lnb = _conv_bwd(dcat, u0, u1, ag, wt["conv_w"], wt["ln_g"], wt["ln_b"],
                                                    name="conv_bwd")
    layers = lambda i: jnp.stack([gf0[i], gf1[i]])
    grads = dict(
        conv_w=d_convw[None], w_out=d_wout, mix_norm_odd=d_godd,
        pool_w=d_pw.reshape(len(POOL_WINDOWS), POOL_GROUP, POOL_GROUP),
        pool_b=d_pb.reshape(1, len(POOL_WINDOWS), POOL_GROUP), pool_scale=d_pscale, w_up=(gf0[1], gf1[1]),
        ffn_conv_w=layers(2).transpose(0, 2, 1, 3).reshape(DEPTH, FFN_CONV_WIDTH, 2 * D_FF), w_down=(gf0[4], gf1[4]))
    send_rest, token_rest, grads["little_slabs"] = _send_rest_grads(grads)
    q_t = _transpose(qkv, name="t_q", out_dtype=BF16, cols=FOX_WIDTH, after=token_rest)
    do_t = _transpose(dcat, name="t_do", out_dtype=BF16, cols=FOX_WIDTH)
    dq, dk_t, dv_t, dc3 = _attn_bwd(qkv, q_t, dcat, do_t, c3, name="attn_bwd")
    dk = _transpose(dk_t, name="t_dk", out_dtype=BF16)
    dv = _transpose(dv_t, name="t_dv", out_dtype=BF16)
    df_t, d_bf = _fox_bwd(dc3.reshape(FOX_HEADS, lp), f_t, wt["b_f"], name="fox_bwd")
    df = _transpose(df_t, name="t_df", out_dtype=BF16)
    dproj = jnp.concatenate([dq, dk, dv, dag, df], axis=1)
    grads["w_in"] = _mm_dw_in(n0_t, dproj, name="mm_dwin")
    send_in, token_in = _send_start(
        [grads["w_in"]], [jax.ShapeDtypeStruct((N_DEV - 1, D_MODEL, _IN_SHARD), BF16)], [(0, _by_owner, 0, None)],
        name="send_w_in")
    dn0 = _mm(dproj, w_in_p, name="mm_dn0", tb=True, tm=lp // 2, tk=896, after=token_in)
    dh0, _, d_geven = _rms_bwd(h0, wt["g_even"], dn0, dh1, name="rms_bwd_even")
    grads.update(
        meta_tokens=dh0[0:N_META], mix_norm_even=d_geven, b_f=d_bf.reshape(1, FOX_HEADS), conv_b=d_convb, ln_g=d_lng,
        ln_b=d_lnb, ffn_norm=jnp.concatenate([gf0[0], gf1[0]], axis=0),
        ffn_conv_b=layers(3).reshape(DEPTH, 2 * D_FF), final_norm=d_gfinal.reshape(D_MODEL),
        sends=(send0, send1, send_rest, send_in))
    return loss, dh0[N_META:n_real], grads


_LITTLE = (("conv_w", (1, 31, 512), 2), ("mix_norm_odd", (1, 1024), 1), ("pool_b", (1, 4, 256), 2),
           ("pool_scale", (1, 1024), 1), ("ffn_conv_w", (2, 3, 5632), 2))


def _send_rest_grads(g):
    out_rows, pool_rows, groups = D_MODEL // N_DEV, POOL_GROUP // N_DEV, len(POOL_WINDOWS)
    little_slabs = _pack([_full_to_slabs(g[n], s, a) for n, s, a in _LITTLE], F32, lead=(N_DEV,), align=8)
    land = lambda shape, dtype: jax.ShapeDtypeStruct((N_DEV - 1,) + shape, dtype)
    handle, token = _send_start(
        [g["w_out"], g["pool_w"], little_slabs],
        [land((out_rows, D_MODEL), BF16), land((groups, pool_rows, POOL_GROUP), BF16), land(little_slabs.shape[1:], F32)],
        [(0, _row_block(out_rows), 0, None), (1, _row_block(pool_rows, axis=1), 1, None), (2, _by_owner, 2, None)],
        name="send_rest")
    return handle, token, little_slabs


def _send_ffn_grads(dw_up, dw_down, tag):
    rows = D_FF // N_DEV
    lands = [jax.ShapeDtypeStruct((N_DEV - 1,) + dw_up.shape[1:], BF16),
             jax.ShapeDtypeStruct((N_DEV - 1, rows, D_MODEL), BF16)]
    return _send_start([dw_up, dw_down], lands, [(0, _by_owner, 0, None), (1, _row_block(rows), 1, None)],
                       name=f"send_ffn{tag}")


_QKV = 3 * FOX_WIDTH
_GLU0 = _QKV + FOX_HEADS
_IN_COLS = _GLU0 + 2 * CONV_CH
_F_PAD = 128


_IN_SHARD = _IN_COLS // N_DEV
_UP_SHARD = 2 * D_FF // N_DEV
_ROW_TILE = 256


def _assemble_w_in(st, *, name):
    tr = _ROW_TILE

    def body(s_ref, o_ref):
        full = jnp.concatenate([s_ref[i].astype(F32) for i in range(N_DEV)], axis=1)
        parts = [full[:, :_QKV], full[:, _GLU0:], full[:, _QKV:_GLU0], jnp.zeros((tr, _F_PAD - FOX_HEADS), F32)]
        o_ref[...] = jnp.concatenate(parts, axis=1).astype(BF16)

    return pl.pallas_call(
        body, name=name, grid=(D_MODEL // tr,),
        in_specs=[pl.BlockSpec((N_DEV, tr, _IN_SHARD), lambda i: (0, i, 0))],
        out_specs=pl.BlockSpec((tr, _QKV + 2 * CONV_CH + _F_PAD), lambda i: (i, 0)),
        out_shape=jax.ShapeDtypeStruct((D_MODEL, _QKV + 2 * CONV_CH + _F_PAD), BF16),
    )(st)


def _mm_dw_in(n_t, dproj, *, name):
    dm, lp = n_t.shape
    tr = _ROW_TILE
    ag0 = _QKV + 2 * CONV_CH

    def body(a_ref, b_ref, o_ref):
        r = jnp.dot(a_ref[...], b_ref[...], preferred_element_type=F32)
        full = jnp.concatenate([r[:, :_QKV], r[:, ag0:ag0 + FOX_HEADS], r[:, _QKV:ag0]], axis=1)
        for i in range(N_DEV):
            o_ref[i] = full[:, i * _IN_SHARD:(i + 1) * _IN_SHARD].astype(BF16)

    return pl.pallas_call(
        body, name=name, grid=(dm // tr,),
        in_specs=[pl.BlockSpec((tr, lp), lambda i: (i, 0)), pl.BlockSpec(dproj.shape, lambda i: (0, 0))],
        out_specs=pl.BlockSpec((N_DEV, tr, _IN_SHARD), lambda i: (0, i, 0)),
        out_shape=jax.ShapeDtypeStruct((N_DEV, dm, _IN_SHARD), BF16),
    )(n_t, dproj)


def _mm_dw_up(n_t, dup2, *, name):
    dm, lp = n_t.shape
    pairs_per_half = D_FF // (2 * _UP_SHARD)

    def body(a_ref, b_ref, o_ref):
        r_t = jnp.dot(a_ref[...], b_ref[...], preferred_element_type=F32).T
        o_ref[0] = r_t[:_UP_SHARD, :].astype(BF16)
        o_ref[1] = r_t[_UP_SHARD:, :].astype(BF16)

    return pl.pallas_call(
        body, name=name, grid=(N_DEV // 2,),
        in_specs=[pl.BlockSpec((dm, lp), lambda p: (0, 0)),
                  pl.BlockSpec((lp, 2 * _UP_SHARD), lambda p: (p // pairs_per_half, p % pairs_per_half))],
        out_specs=pl.BlockSpec((2, _UP_SHARD, dm), lambda p: (p, 0, 0)),
        out_shape=jax.ShapeDtypeStruct((N_DEV, _UP_SHARD, dm), BF16),
    )(n_t, dup2)


MESH = pl.DeviceIdType.MESH
ANY = pl.BlockSpec(memory_space=pl.ANY)


def _slot(px, py, pc):
    return 4 * px + 2 * py + pc


def _by_owner(ref, slot):
    return ref.at[slot]


def _row_block(rows, axis=0):
    def place(ref, slot):
        idx = (slice(None),) * axis + (pl.ds(slot * rows, rows),)
        return ref.at[idx]
    return place


def _all_gather(arrs, out_shapes, places, *, name):
    n = len(arrs)

    def body(*refs):
        ins, outs = refs[:n], refs[n:2 * n]
        send_sems, recv_sems, local_sems = refs[2 * n:]
        x, y, c = lax.axis_index("x"), lax.axis_index("y"), lax.axis_index("c")
        me, sibling = (x, y, c), (x, y, 1 - c)
        chips = [(1 - x, y), (x, 1 - y), (1 - x, 1 - y)]

        def copy(a, k, block, to, from_input=False):
            dst = places[a](outs[a], _slot(*block))
            return pltpu.make_async_remote_copy(
                src_ref=ins[a] if from_input else dst, dst_ref=dst,
                send_sem=send_sems.at[7 * a + k], recv_sem=recv_sems.at[7 * a + k],
                device_id=to, device_id_type=MESH)

        own, sent = [], []
        for a in range(n):
            mine = pltpu.make_async_copy(ins[a], places[a](outs[a], _slot(*me)), local_sems.at[a])
            mine.start()
            own.append(mine)
            first = [copy(a, 0, me, sibling, True)]
            first += [copy(a, 1 + j, me, (*chip, c), True) for j, chip in enumerate(chips)]
            for cp in first:
                cp.start()
            sent += first
        for a in range(n):
            for j, chip in enumerate(chips):
                copy(a, 1 + j, (*chip, c), me).wait_recv()
                passed = copy(a, 4 + j, (*chip, c), sibling)
                passed.start()
                sent.append(passed)
        for a in range(n):
            copy(a, 0, sibling, me).wait_recv()
            for j, chip in enumerate(chips):
                copy(a, 4 + j, (*chip, 1 - c), me).wait_recv()
        for cp in sent:
            cp.wait_send()
        for cp in own:
            cp.wait()

    return pl.pallas_call(
        body, name=name,
        in_specs=[ANY] * n, out_specs=[ANY] * n,
        out_shape=[jax.ShapeDtypeStruct(s, a.dtype) for s, a in zip(out_shapes, arrs)],
        scratch_shapes=[pltpu.SemaphoreType.DMA((7 * n,)), pltpu.SemaphoreType.DMA((7 * n,)),
                        pltpu.SemaphoreType.DMA((n,))],
    )(*arrs)


HBM = pl.BlockSpec(memory_space=pltpu.HBM)
SEM = pl.BlockSpec(memory_space=pltpu.SEMAPHORE)
EFFECT = pltpu.SideEffectType.DATAFLOW_SIDE_EFFECTING


def _relation_copies(src_refs, land_refs, copies, send_sems, recv_sems):
    x, y, c = lax.axis_index("x"), lax.axis_index("y"), lax.axis_index("c")
    flip = lambda v, bit: 1 - v if bit else v
    out = []
    for k in range(1, N_DEV):
        p = (flip(x, k & 4), flip(y, k & 2), flip(c, k & 1))
        for j, (si, take, li, put) in enumerate(copies):
            sem = (k - 1) * len(copies) + j
            dst = land_refs[li].at[k - 1] if put is None else put(land_refs[li], _slot(x, y, c))
            out.append(pltpu.make_async_remote_copy(
                src_ref=take(src_refs[si], _slot(*p)), dst_ref=dst,
                send_sem=send_sems.at[sem], recv_sem=recv_sems.at[sem], device_id=p, device_id_type=MESH))
    return out


def _own_copies(src_refs, land_refs, copies, sems):
    me = _slot(lax.axis_index("x"), lax.axis_index("y"), lax.axis_index("c"))
    placed = [(si, take, li, put) for si, take, li, put in copies if put is not None]
    return [pltpu.make_async_copy(take(src_refs[si], me), put(land_refs[li], me),
                                  sems.at[(N_DEV - 1) * len(copies) + j])
            for j, (si, take, li, put) in enumerate(placed)]


def _send_start(srcs, land_structs, copies, *, name, after=None):
    ns, nl = len(srcs), len(land_structs)
    n_sem = (N_DEV - 1) * len(copies) + sum(put is not None for _, _, _, put in copies)
    behind = [] if after is None else [after]

    def body(*refs):
        first_out = ns + nl + len(behind)
        send_sems, recv_sems, token = refs[first_out], refs[first_out + 1], refs[-1]
        for cp in _relation_copies(refs[:ns], refs[ns:ns + nl], copies, send_sems, recv_sems):
            cp.start()
        for cp in _own_copies(refs[:ns], refs[ns:ns + nl], copies, send_sems):
            cp.start()
        token[...] = jnp.zeros_like(token)

    in_hbm = lambda a: pltpu.with_memory_space_constraint(a, pltpu.HBM)
    outs = pl.pallas_call(
        body, name=name,
        out_shape=(pltpu.SemaphoreType.DMA((n_sem,)), pltpu.SemaphoreType.DMA((n_sem,)),
                   *[pltpu.HBM(s.shape, s.dtype) for s in srcs],
                   *[pltpu.HBM(s.shape, s.dtype) for s in land_structs],
                   jax.ShapeDtypeStruct((8, 128), F32)),
        in_specs=(HBM,) * (ns + nl) + (ANY,) * len(behind),
        out_specs=(SEM, SEM) + (HBM,) * (ns + nl) + (pl.BlockSpec(memory_space=pltpu.VMEM),),
        input_output_aliases={i: 2 + i for i in range(ns + nl)},
        compiler_params=pltpu.CompilerParams(has_side_effects=EFFECT),
    )(*[in_hbm(s) for s in srcs], *[in_hbm(lax.empty(s.shape, s.dtype)) for s in land_structs], *behind)
    return (outs[0], outs[1], outs[2:2 + ns], outs[2 + ns:2 + ns + nl], copies), outs[-1]


def _send_wait(handle, after, *, name):
    send_sems, recv_sems, srcs, lands, copies = handle
    ns, nl = len(srcs), len(lands)

    def body(*refs):
        for cp in _relation_copies(refs[:ns], refs[ns:ns + nl], copies, refs[ns + nl], refs[ns + nl + 1]):
            cp.wait_send()
            cp.wait_recv()
        for cp in _own_copies(refs[:ns], refs[ns:ns + nl], copies, refs[ns + nl]):
            cp.wait()

    outs = pl.pallas_call(
        body, name=name,
        out_shape=tuple(pltpu.HBM(a.shape, a.dtype) for a in (*srcs, *lands)),
        in_specs=(HBM,) * (ns + nl) + (SEM, SEM, ANY), out_specs=(HBM,) * (ns + nl),
        input_output_aliases={i: i for i in range(ns + nl)},
        compiler_params=pltpu.CompilerParams(has_side_effects=EFFECT),
    )(*srcs, *lands, send_sems, recv_sems, after)
    return outs[ns:]


def _sum_slabs(stack, *, name, own=None):
    n, rows, w = stack.shape

    def body(*refs):
        s_ref, o_ref = refs[-2], refs[-1]
        acc = s_ref[0] if own is None else refs[0][...] + s_ref[0]
        for i in range(1, n):
            acc = acc + s_ref[i]
        o_ref[...] = acc

    return pl.pallas_call(body, name=name, out_shape=jax.ShapeDtypeStruct((rows, w), F32))(
        *([] if own is None else [own]), stack)


def _adam_math(w, g, m, v):
    mn = ADAM_B1 * m + (1.0 - ADAM_B1) * g
    vn = ADAM_B2 * v + (1.0 - ADAM_B2) * (g * g)
    m_hat = mn / (1.0 - ADAM_B1 ** ADAM_STEP)
    v_hat = vn / (1.0 - ADAM_B2 ** ADAM_STEP)
    return -ADAM_LR * (m_hat / (jnp.sqrt(v_hat) + ADAM_EPS) + ADAM_WD * w), mn, vn


def _adamw(w, g, m, v, *, name):
    def body(w_ref, g_ref, m_ref, v_ref, d_ref, mo_ref, vo_ref):
        d_ref[...], mo_ref[...], vo_ref[...] = _adam_math(w_ref[...], g_ref[...], m_ref[...], v_ref[...])

    return pl.pallas_call(body, name=name, out_shape=[jax.ShapeDtypeStruct(w.shape, F32)] * 3)(w, g, m, v)


def _adamw_layers(w, owns, lands, m, v, tr, *, name):
    nl, rows, cols = w.shape
    steps = rows // tr
    assert rows % tr == 0

    def body(*refs):
        w_ref, m_ref, v_ref = refs[:3]
        own_refs, land_refs = refs[3:3 + nl], refs[3 + nl:3 + 2 * nl]
        g_ref, d_ref, mo_ref, vo_ref = refs[3 + 2 * nl:]
        for li in range(nl):
            @pl.when(pl.program_id(0) == li)
            def _(li=li):
                g = own_refs[li][...].astype(F32)
                for k in range(N_DEV - 1):
                    g = g + land_refs[li][k].astype(F32)
                g_ref[...] = g
                d_ref[...], mo_ref[...], vo_ref[...] = _adam_math(w_ref[...], g, m_ref[...], v_ref[...])

    def held(li):
        return lambda l, i: jnp.where(l == li, i, jnp.where(l < li, 0, steps - 1))

    blk = pl.BlockSpec((None, tr, cols), lambda l, i: (l, i, 0))
    own_specs = [pl.BlockSpec((tr, cols), lambda l, i, f=held(li): (f(l, i), 0)) for li in range(nl)]
    land_specs = [pl.BlockSpec((N_DEV - 1, tr, cols), lambda l, i, f=held(li): (0, f(l, i), 0)) for li in range(nl)]
    return pl.pallas_call(
        body, name=name, grid=(nl, steps),
        in_specs=[blk, blk, blk] + own_specs + land_specs, out_specs=[blk] * 4,
        out_shape=[jax.ShapeDtypeStruct(w.shape, F32)] * 4,
    )(w, m, v, *owns, *lands)


_WEIGHTS = (
    ("meta_tokens", (16, 1024), 1), ("mix_norm_even", (1, 1024), None), ("w_in", (1, 1024, 2568), 2),
    ("b_f", (1, 8), None), ("conv_w", (1, 31, 512), 2), ("conv_b", (1, 512), None), ("ln_g", (1, 512), None),
    ("ln_b", (1, 512), None), ("w_out", (1, 1024, 1024), 1), ("mix_norm_odd", (1, 1024), 1),
    ("pool_w", (1, 4, 256, 256), 2), ("pool_b", (1, 4, 256), 2), ("pool_scale", (1, 1024), 1),
    ("ffn_norm", (2, 1024), None), ("w_up", (2, 1024, 5632), 2), ("ffn_conv_w", (2, 3, 5632), 2),
    ("ffn_conv_b", (2, 5632), None), ("w_down", (2, 2816, 1024), 1), ("final_norm", (1024,), None),
)
_MATMUL_WEIGHTS = ("w_in", "w_out", "pool_w", "w_up", "w_down")
_ADAM_ROWS = dict(w_in=256, w_out=128, pool_w=128, w_up=352, w_down=352)


def _shard_shape(shape, axis):
    return shape[:axis] + (shape[axis] // N_DEV,) + shape[axis + 1:]


def _size(shape):
    n = 1
    for s in shape:
        n *= s
    return n


def _pack(parts, dtype, lead=(), align=16):
    flat = jnp.concatenate([p.reshape(lead + (-1,)).astype(dtype) for p in parts], axis=-1)
    n = flat.shape[-1]
    rows = _round_up(-(-n // FLAT_W), align)
    flat = jnp.pad(flat, [(0, 0)] * len(lead) + [(0, rows * FLAT_W - n)])
    return flat.reshape(lead + (rows, FLAT_W))


def _unpack(buf, shapes, lead=()):
    flat = buf.reshape(lead + (-1,))
    out, off = [], 0
    for shp in shapes:
        n = _size(shp)
        out.append(flat[..., off:off + n].reshape(lead + shp))
        off += n
    return out


def _gathered_to_full(stack, shape, axis):
    return jnp.moveaxis(stack, 0, axis).reshape(shape)


def _full_to_slabs(full, shape, axis):
    split = shape[:axis] + (N_DEV, shape[axis] // N_DEV) + shape[axis + 1:]
    return jnp.moveaxis(full.reshape(split), axis, 0)


def kernel(x, meta_tokens, mix_norm_even, w_in, b_f, conv_w, conv_b, ln_g, ln_b, w_out, mix_norm_odd, pool_w, pool_b, pool_scale, ffn_norm, w_up, ffn_conv_w, ffn_conv_b, w_down, final_norm, loss_target, m_meta_tokens, m_mix_norm_even, m_w_in, m_b_f, m_conv_w, m_conv_b, m_ln_g, m_ln_b, m_w_out, m_mix_norm_odd, m_pool_w, m_pool_b, m_pool_scale, m_ffn_norm, m_w_up, m_ffn_conv_w, m_ffn_conv_b, m_w_down, m_final_norm, v_meta_tokens, v_mix_norm_even, v_w_in, v_b_f, v_conv_w, v_conv_b, v_ln_g, v_ln_b, v_w_out, v_mix_norm_odd, v_pool_w, v_pool_b, v_pool_scale, v_ffn_norm, v_w_up, v_ffn_conv_w, v_ffn_conv_b, v_w_down, v_final_norm):
    names = [n for n, _, _ in _WEIGHTS]
    w_loc = dict(zip(names, (meta_tokens, mix_norm_even, w_in, b_f, conv_w, conv_b, ln_g, ln_b, w_out, mix_norm_odd,
                             pool_w, pool_b, pool_scale, ffn_norm, w_up, ffn_conv_w, ffn_conv_b, w_down, final_norm)))
    m_loc = dict(zip(names, (m_meta_tokens, m_mix_norm_even, m_w_in, m_b_f, m_conv_w, m_conv_b, m_ln_g, m_ln_b,
                             m_w_out, m_mix_norm_odd, m_pool_w, m_pool_b, m_pool_scale, m_ffn_norm, m_w_up,
                             m_ffn_conv_w, m_ffn_conv_b, m_w_down, m_final_norm)))
    v_loc = dict(zip(names, (v_meta_tokens, v_mix_norm_even, v_w_in, v_b_f, v_conv_w, v_conv_b, v_ln_g, v_ln_b,
                             v_w_out, v_mix_norm_odd, v_pool_w, v_pool_b, v_pool_scale, v_ffn_norm, v_w_up,
                             v_ffn_conv_w, v_ffn_conv_b, v_w_down, v_final_norm)))
    replicated = [(n, s) for n, s, a in _WEIGHTS if a is None]
    little = [(n, s, a) for n, s, a in _WEIGHTS if a is not None and n not in _MATMUL_WEIGHTS]
    little_shards = [_shard_shape(s, a) for _, s, a in little]
    out_rows, down_rows, pool_rows = D_MODEL // N_DEV, D_FF // N_DEV, POOL_GROUP // N_DEV
    n_groups = len(POOL_WINDOWS)

    little_pack = _pack([w_loc[n] for n, _, _ in little], F32)
    g_win, g_wout, g_poolw, g_little = _all_gather(
        [w_in[0].astype(BF16), w_out[0].astype(BF16), pool_w[0].astype(BF16), little_pack],
        [(N_DEV, D_MODEL, _IN_SHARD), (D_MODEL, D_MODEL), (n_groups, POOL_GROUP, POOL_GROUP),
         (N_DEV,) + little_pack.shape],
        [_by_owner, _row_block(out_rows), _row_block(pool_rows, axis=1), _by_owner],
        name="gather_weights")
    me = _slot(lax.axis_index("x"), lax.axis_index("y"), lax.axis_index("c"))
    up_t = lambda a: jnp.transpose(a, (0, 2, 1))
    w_loc["w_up"], m_loc["w_up"], v_loc["w_up"] = up_t(w_up), up_t(m_w_up), up_t(v_w_up)
    w_up_b, w_down_b = w_loc["w_up"].astype(BF16), w_down.astype(BF16)
    whole = lambda ref, slot: ref
    ffn_lands = [jax.ShapeDtypeStruct((2 * D_FF, D_MODEL), BF16), jax.ShapeDtypeStruct((D_FF, D_MODEL), BF16)]
    ffn_gathers, behind = [], g_little
    for l in range(DEPTH):
        handle, behind = _send_start([w_up_b[l], w_down_b[l]], ffn_lands,
                                     [(0, whole, 0, _row_block(_UP_SHARD)), (1, whole, 1, _row_block(down_rows))],
                                     name=f"gather_ffn{l}_start", after=behind)
        ffn_gathers.append(handle)

    def ffn_weights(l, after):
        return _send_wait(ffn_gathers[l], after, name=f"gather_ffn{l}_wait")

    w_in_p = _assemble_w_in(g_win, name="assemble_w_in")
    full = {n: _gathered_to_full(st, s, a)
            for (n, s, a), st in zip(little, _unpack(g_little, little_shards, lead=(N_DEV,)))}
    f0 = _QKV + 2 * CONV_CH
    wt = dict(
        meta=full["meta_tokens"], g_even=mix_norm_even, w_in_p=w_in_p, wf_t=w_in_p[:, f0:f0 + FOX_HEADS].T,
        b_f=b_f.reshape(FOX_HEADS, 1), conv_w=full["conv_w"][0], conv_b=conv_b, ln_g=ln_g, ln_b=ln_b, w_out=g_wout,
        g_odd=full["mix_norm_odd"], pool_w=g_poolw, pool_b=full["pool_b"].reshape(1, D_MODEL),
        pool_scale=full["pool_scale"], ffn_norm=ffn_norm, ffn_weights=ffn_weights, ffn_started=behind,
        fcw3=full["ffn_conv_w"].reshape(DEPTH, FFN_CONV_WIDTH, 2, D_FF).transpose(0, 2, 1, 3),
        fcb3=ffn_conv_b.reshape(DEPTH, 2, 1, D_FF), g_final=final_norm.reshape(1, D_MODEL))

    loss_part, grad_x, g = _local_step(x[0], loss_target[0], wt)

    small = _pack([loss_part[:, 0:1]] + [g[n] for n, _ in replicated] + [g["meta_tokens"]], F32, align=8)
    send_small, token_small = _send_start([small], [jax.ShapeDtypeStruct((N_DEV,) + small.shape, F32)],
                                          [(0, whole, 0, _by_owner)], name="send_small")

    grads, delta, new_m, new_v = {}, {}, {}, {}
    send0, send1, send_rest, send_in = g["sends"]
    ffn_lands = [_send_wait(send, token_small, name=f"wait_ffn{l}") for l, send in enumerate((send0, send1))]
    own_up = [lax.dynamic_index_in_dim(d, me, 0, keepdims=False) for d in g["w_up"]]
    own_down = [lax.dynamic_slice_in_dim(d, me * down_rows, down_rows, 0) for d in g["w_down"]]
    for n, owns, idx in (("w_up", own_up, 0), ("w_down", own_down, 1)):
        grads[n], delta[n], new_m[n], new_v[n] = _adamw_layers(
            w_loc[n], owns, [ffn_lands[l][idx] for l in range(DEPTH)], m_loc[n], v_loc[n], _ADAM_ROWS[n],
            name=f"adamw_{n}")
    for d in (grads, delta, new_m, new_v):
        d["w_up"] = up_t(d["w_up"])
    land_out, land_pool, land_little = _send_wait(send_rest, delta["w_down"], name="wait_rest")
    (land_in,) = _send_wait(send_in, land_out, name="wait_w_in")
    pool_2d = (n_groups * pool_rows, POOL_GROUP)
    own_pool = lax.dynamic_slice_in_dim(g["pool_w"], me * pool_rows, pool_rows, 1)
    for n, own, land, shp in (
            ("w_in", lax.dynamic_index_in_dim(g["w_in"], me, 0, keepdims=False), land_in, w_in.shape),
            ("w_out", lax.dynamic_slice_in_dim(g["w_out"], me * out_rows, out_rows, 0), land_out, w_out.shape),
            ("pool_w", own_pool.reshape(pool_2d), land_pool.reshape((N_DEV - 1,) + pool_2d), (1,) + pool_2d)):
        outs = _adamw_layers(w_loc[n].reshape(shp), [own], [land], m_loc[n].reshape(shp), v_loc[n].reshape(shp),
                             _ADAM_ROWS[n], name=f"adamw_{n}")
        grads[n], delta[n], new_m[n], new_v[n] = (o.reshape(w_loc[n].shape) for o in outs)
    own_little = lax.dynamic_index_in_dim(g["little_slabs"], me, 0, keepdims=False)
    g_little = _unpack(_sum_slabs(land_little, own=own_little, name="sum_little"),
                       [_shard_shape(s, a) for _, s, a in _LITTLE])
    grads.update({n: gl for (n, _, _), gl in zip(_LITTLE, g_little)})
    (everyone,) = _send_wait(send_small, delta["w_in"], name="wait_small")
    summed = _unpack(_sum_slabs(everyone, name="sum_small"),
                     [(1, 1)] + [s for _, s in replicated] + [(N_META, D_MODEL)])
    loss = summed[0].reshape(())
    grads.update({n: gr for (n, _), gr in zip(replicated, summed[1:-1])})
    grads["meta_tokens"] = lax.dynamic_slice_in_dim(summed[-1], me * out_rows, out_rows, 1)
    two_d = lambda shp: (_size(shp[:-1]), shp[-1])
    for n in names:
        if n in _MATMUL_WEIGHTS:
            continue
        shp = w_loc[n].shape
        d, mn, vn = _adamw(w_loc[n].reshape(two_d(shp)), grads[n].reshape(two_d(shp)), m_loc[n].reshape(two_d(shp)),
                           v_loc[n].reshape(two_d(shp)), name=f"adamw_{n}")
        delta[n], new_m[n], new_v[n] = d.reshape(shp), mn.reshape(shp), vn.reshape(shp)
    return (loss, grad_x[None], *[grads[n] for n in names], *[delta[n] for n in names],
            *[new_m[n] for n in names], *[new_v[n] for n in names])
```

```python
import functools

import jax
import jax.numpy as jnp
from jax import lax
from jax.experimental import pallas as pl
from jax.experimental.pallas import tpu as pltpu

F32 = jnp.float32
BF16 = jnp.bfloat16

N_DEV = 8
DEPTH = 2
D_MODEL = 1024
N_META = 16
FOX_HEADS = 8
FOX_HEAD_DIM = 64
FOX_WIDTH = 512
CONV_CH = 512
CONV_WIDTH = 31
POOL_WINDOWS = (2, 4, 8, 16)
POOL_GROUP = 256
D_FF = 2816
FFN_CONV_WIDTH = 3
RMS_EPS = 1e-6
LN_EPS = 1e-5
ADAM_LR = 0.001
ADAM_B1 = 0.9
ADAM_B2 = 0.999
ADAM_EPS = 1e-08
ADAM_WD = 0.01
ADAM_STEP = 10

CHUNK = 128
HALO = 32
NEG_BIG = -1e30
FLAT_W = 1024


def _round_up(n, m):
    return (n + m - 1) // m * m


def _sigmoid(x):
    return 1.0 / (1.0 + jnp.exp(-x))


def _fold8(p):
    acc = p[0:8, :]
    for r in range(1, p.shape[0] // 8):
        acc = acc + p[8 * r:8 * r + 8, :]
    return acc


def _mm(a, b, *, name, tb=False, tm=None, tn=None, tk=None, out_dtype=F32, res=None,
        a_map=None, b_map=None, o_map=None, out_shape=None, dims=None, after=None):
    if dims is None:
        m, k = a.shape
        n = b.shape[-2] if tb else b.shape[-1]
    else:
        m, n, k = dims
    tm, tn, tk = tm or m, tn or n, tk or k
    assert m % tm == 0 and n % tn == 0 and k % tk == 0, (name, m, n, k, tm, tn, tk)
    nk = k // tk
    a_map = a_map or (lambda i, j, kk: (i, kk))
    b_map = b_map or ((lambda i, j, kk: (j, kk)) if tb else (lambda i, j, kk: (kk, j)))
    o_map = o_map or (lambda i, j, kk: (i, j))
    out_shape = out_shape or (m, n)
    contract = (((1,), (1,)), ((), ())) if tb else (((1,), (0,)), ((), ()))
    has_res = res is not None

    def body(*refs):
        a_ref, b_ref = refs[0], refs[1]
        res_ref = refs[2] if has_res else None
        o_ref = refs[2 + has_res + (after is not None)]
        p = lax.dot_general(a_ref[...], b_ref[...], contract, preferred_element_type=F32)
        if nk == 1:
            if has_res:
                p = p + res_ref[...]
            o_ref[...] = p.astype(o_ref.dtype)
        else:
            acc_ref = refs[-1]
            kk = pl.program_id(2)

            @pl.when(kk == 0)
            def _():
                acc_ref[...] = p

            @pl.when(kk > 0)
            def _():
                acc_ref[...] += p

            @pl.when(kk == nk - 1)
            def _():
                r = acc_ref[...]
                if has_res:
                    r = r + res_ref[...]
                o_ref[...] = r.astype(o_ref.dtype)

    in_specs = [pl.BlockSpec((tm, tk), a_map), pl.BlockSpec((tn, tk) if tb else (tk, tn), b_map)]
    operands = [a, b]
    if has_res:
        in_specs.append(pl.BlockSpec((tm, tn), o_map))
        operands.append(res)
    if after is not None:
        in_specs.append(pl.BlockSpec(memory_space=pl.ANY))
        operands.append(after)
    return pl.pallas_call(
        body, name=name, grid=(m // tm, n // tn, nk),
        in_specs=in_specs, out_specs=pl.BlockSpec((tm, tn), o_map),
        out_shape=jax.ShapeDtypeStruct(out_shape, out_dtype),
        scratch_shapes=[pltpu.VMEM((tm, tn), F32)] if nk > 1 else [],
    )(*operands)


def _transpose(x, *, name, out_dtype, cols=None, after=None):
    r, c = x.shape
    cols = cols or c
    assert r % CHUNK == 0
    behind = [] if after is None else [after]

    def body(x_ref, *rest):
        o_ref = rest[-1]
        o_ref[...] = x_ref[...].astype(F32).T.astype(o_ref.dtype)

    return pl.pallas_call(
        body, name=name, grid=(r // CHUNK,),
        in_specs=[pl.BlockSpec((CHUNK, cols), lambda i: (i, 0))] + [pl.BlockSpec(memory_space=pl.ANY)] * len(behind),
        out_specs=pl.BlockSpec((cols, CHUNK), lambda i: (0, i)),
        out_shape=jax.ShapeDtypeStruct((cols, r), out_dtype),
    )(x, *behind)


def _rms_fwd(x, g, *, name, out_dtype, after=None, transposed=False):
    lp, dm = x.shape
    tr = CHUNK if transposed else lp // 4
    behind = [] if after is None else [after]

    def body(x_ref, g_ref, *rest):
        xv = x_ref[...]
        r = lax.rsqrt(jnp.mean(xv * xv, axis=-1, keepdims=True) + RMS_EPS)
        y = xv * r * g_ref[...]
        if transposed:
            rest[-2][...] = y.astype(out_dtype)
            rest[-1][...] = y.T.astype(out_dtype)
        else:
            rest[-1][...] = y.astype(out_dtype)

    row = pl.BlockSpec((tr, dm), lambda i: (i, 0))
    out_specs, out_shape = row, jax.ShapeDtypeStruct((lp, dm), out_dtype)
    if transposed:
        out_specs = [row, pl.BlockSpec((dm, tr), lambda i: (0, i))]
        out_shape = [out_shape, jax.ShapeDtypeStruct((dm, lp), out_dtype)]
    return pl.pallas_call(
        body, name=name, grid=(lp // tr,),
        in_specs=[row, pl.BlockSpec((1, dm), lambda i: (0, 0))] + [pl.BlockSpec(memory_space=pl.ANY)] * len(behind),
        out_specs=out_specs, out_shape=out_shape,
    )(x, g, *behind)


def _rms_bwd(x, g, dn, dres, *, name):
    lp, dm = x.shape
    tr = lp // 4

    def body(x_ref, g_ref, dn_ref, dres_ref, dh_ref, dhb_ref, dg_ref):
        xv = x_ref[...]
        r = lax.rsqrt(jnp.mean(xv * xv, axis=-1, keepdims=True) + RMS_EPS)
        xhat = xv * r
        dnv = dn_ref[...]

        @pl.when(pl.program_id(0) == 0)
        def _():
            dg_ref[...] = jnp.zeros_like(dg_ref)

        dg_ref[...] += jnp.sum(dnv * xhat, axis=0, keepdims=True)
        dxhat = dnv * g_ref[...]
        dx = r * (dxhat - xhat * jnp.mean(dxhat * xhat, axis=-1, keepdims=True))
        dh = dres_ref[...] + dx
        dh_ref[...] = dh
        dhb_ref[...] = dh.astype(BF16)

    row = pl.BlockSpec((tr, dm), lambda i: (i, 0))
    vec = pl.BlockSpec((1, dm), lambda i: (0, 0))
    return pl.pallas_call(
        body, name=name, grid=(lp // tr,),
        in_specs=[row, vec, row, row], out_specs=[row, row, vec],
        out_shape=[jax.ShapeDtypeStruct((lp, dm), F32), jax.ShapeDtypeStruct((lp, dm), BF16),
                   jax.ShapeDtypeStruct((1, dm), F32)],
    )(x, g, dn, dres)


def _loss_head(h, g, tgt, n_real, *, name):
    lp, dm = h.shape
    tr = lp // 4

    def body(x_ref, g_ref, t_ref, loss_ref, dh_ref, dhb_ref, dg_ref):
        i = pl.program_id(0)
        xv = x_ref[...]
        r = lax.rsqrt(jnp.mean(xv * xv, axis=-1, keepdims=True) + RMS_EPS)
        xhat = xv * r
        gv = g_ref[...]
        y = xhat * gv
        t = i * tr + lax.broadcasted_iota(jnp.int32, (tr, 1), 0)
        valid = (t >= N_META) & (t < n_real)
        diff = jnp.where(valid, y - t_ref[...], 0.0)

        @pl.when(i == 0)
        def _():
            loss_ref[...] = jnp.zeros_like(loss_ref)
            dg_ref[...] = jnp.zeros_like(dg_ref)

        row_sq = jnp.sum(diff * diff, axis=-1, keepdims=True) * (1.0 / dm)
        part = 0.5 * jnp.sum(row_sq, axis=0, keepdims=True)
        loss_ref[...] += jnp.broadcast_to(part, loss_ref.shape)
        dy = diff * (1.0 / dm)
        dg_ref[...] += jnp.sum(dy * xhat, axis=0, keepdims=True)
        dxhat = dy * gv
        dx = r * (dxhat - xhat * jnp.mean(dxhat * xhat, axis=-1, keepdims=True))
        dh_ref[...] = dx
        dhb_ref[...] = dx.astype(BF16)

    row = pl.BlockSpec((tr, dm), lambda i: (i, 0))
    vec = pl.BlockSpec((1, dm), lambda i: (0, 0))
    return pl.pallas_call(
        body, name=name, grid=(lp // tr,),
        in_specs=[row, vec, row],
        out_specs=[pl.BlockSpec((1, 128), lambda i: (0, 0)), row, row, vec],
        out_shape=[jax.ShapeDtypeStruct((1, 128), F32), jax.ShapeDtypeStruct((lp, dm), F32),
                   jax.ShapeDtypeStruct((lp, dm), BF16), jax.ShapeDtypeStruct((1, dm), F32)],
    )(h, g, tgt)


def _tri(upper):
    r = lax.broadcasted_iota(jnp.int32, (CHUNK, CHUNK), 0)
    c = lax.broadcasted_iota(jnp.int32, (CHUNK, CHUNK), 1)
    return jnp.where(r <= c if upper else r >= c, 1.0, 0.0).astype(F32)


def _fox_prep(f_t, b_f, *, name):
    nh, lp = f_t.shape
    nch = lp // CHUNK

    def body(f_ref, b_ref, c_ref):
        tri = _tri(True)
        carry = jnp.zeros((nh, 1), F32)
        for blk in range(nch):
            cols = slice(blk * CHUNK, (blk + 1) * CHUNK)
            z = f_ref[:, cols] + b_ref[...]
            logf = jnp.minimum(z, 0.0) - jnp.log(1.0 + jnp.exp(-jnp.abs(z)))
            cb = jnp.dot(logf, tri, preferred_element_type=F32, precision=lax.Precision.HIGHEST)
            c_ref[:, cols] = cb + carry
            carry = carry + jnp.sum(logf, axis=1, keepdims=True)

    return pl.pallas_call(
        body, name=name, out_shape=jax.ShapeDtypeStruct((nh, lp), F32),
    )(f_t, b_f)


def _fox_bwd(dc, f_t, b_f, *, name):
    nh, lp = f_t.shape
    nch = lp // CHUNK

    def body(dc_ref, f_ref, b_ref, df_ref, db_ref):
        tri = _tri(False)
        carry = jnp.zeros((nh, 1), F32)
        db = jnp.zeros((nh, 1), F32)
        df_ref[...] = jnp.zeros_like(df_ref)
        for blk in reversed(range(nch)):
            cols = slice(blk * CHUNK, (blk + 1) * CHUNK)
            dcb = dc_ref[:, cols]
            dlogf = jnp.dot(dcb, tri, preferred_element_type=F32, precision=lax.Precision.HIGHEST) + carry
            carry = carry + jnp.sum(dcb, axis=1, keepdims=True)
            z = f_ref[:, cols] + b_ref[...]
            dz = dlogf * _sigmoid(-z)
            df_ref[0:nh, cols] = dz
            db = db + jnp.sum(dz, axis=1, keepdims=True)
        db_ref[...] = db

    return pl.pallas_call(
        body, name=name,
        out_shape=[jax.ShapeDtypeStruct((128, lp), F32), jax.ShapeDtypeStruct((nh, 1), F32)],
    )(dc, f_t, b_f)


def _attn_blocks(lp):
    tq = lp // 4
    return tq, [(i * tq, min(lp, _round_up((i + 1) * tq, CHUNK))) for i in range(4)]


ATTN_SCALE = FOX_HEAD_DIM ** -0.5


def _attn_probs(q2s, k_h, c_row, row0, n):
    tq = q2s.shape[0]
    lo = row0 // CHUNK * CHUNK
    logits = []
    for c0, c1 in ([(0, lo)] if lo else []) + [(lo, n)]:
        s = lax.dot_general(q2s, k_h[c0:c1], (((1,), (1,)), ((), ())), preferred_element_type=F32) - c_row[:, c0:c1]
        if c1 > row0:
            t = row0 + lax.broadcasted_iota(jnp.int32, (tq, c1 - c0), 0)
            sidx = c0 + lax.broadcasted_iota(jnp.int32, (tq, c1 - c0), 1)
            s = jnp.where(sidx <= t, s, NEG_BIG)
        logits.append((s, c0, c1))
    m = functools.reduce(jnp.maximum, [jnp.max(s, axis=1, keepdims=True) for s, _, _ in logits])
    ps = [(jnp.exp(s - m), c0, c1) for s, c0, c1 in logits]
    inv = 1.0 / sum(jnp.sum(p, axis=1, keepdims=True) for p, _, _ in ps)
    return [(p * inv, c0, c1) for p, c0, c1 in ps]


def _attn_fwd(qkv, c3, *, name):
    lp = qkv.shape[0]
    tq, blocks = _attn_blocks(lp)

    def body(q_ref, k_ref, v_ref, c_ref, o_ref):
        lane = lax.broadcasted_iota(jnp.int32, (1, 128), 1)
        zero = jnp.zeros((), BF16)
        for i, (row0, n) in enumerate(blocks):
            q2s = q_ref[row0:row0 + tq, :] * ATTN_SCALE
            acc = jnp.zeros((tq, 128), F32)
            for hd in range(2):
                sel = (lane < 64) if hd == 0 else (lane >= 64)
                k_h = jnp.where(sel, k_ref[0:n, :], zero)
                v_h = jnp.where(sel, v_ref[0:n, :], zero)
                for p, c0, c1 in _attn_probs(q2s, k_h, c_ref[hd:hd + 1, 0:n], row0, n):
                    acc = acc + jnp.dot(p.astype(BF16), v_h[c0:c1], preferred_element_type=F32)
            o_ref[row0:row0 + tq, :] = acc.astype(BF16)

    blk = lambda off: pl.BlockSpec((lp, 128), lambda p: (0, off + p))
    return pl.pallas_call(
        body, name=name, grid=(4,),
        in_specs=[blk(0), blk(4), blk(8), pl.BlockSpec((None, 2, lp), lambda p: (p, 0, 0))],
        out_specs=pl.BlockSpec((lp, 128), lambda p: (0, p)),
        out_shape=jax.ShapeDtypeStruct((lp, FOX_WIDTH), BF16),
    )(qkv, qkv, qkv, c3)


def _attn_bwd(qkv, q_t, dcat, do_t, c3, *, name):
    lp = qkv.shape[0]
    tq, blocks = _attn_blocks(lp)
    scale = FOX_HEAD_DIM ** -0.5

    def body(q_ref, k_ref, v_ref, qt_ref, do_ref, dot_ref, c_ref, dq_ref, dkt_ref, dvt_ref, dc_ref,
             dkt_acc, dvt_acc):
        lane = lax.broadcasted_iota(jnp.int32, (1, 128), 1)
        sub = lax.broadcasted_iota(jnp.int32, (128, 1), 0)
        zero = jnp.zeros((), BF16)
        dkt_acc[...] = jnp.zeros_like(dkt_acc)
        dvt_acc[...] = jnp.zeros_like(dvt_acc)
        dc_ref[...] = jnp.zeros_like(dc_ref)
        for i, (row0, n) in enumerate(blocks):
            rows = slice(row0, row0 + tq)
            q2s = q_ref[rows, :] * ATTN_SCALE
            do2 = do_ref[rows, :].astype(BF16)
            dq_acc = jnp.zeros((tq, 128), F32)
            for hd in range(2):
                sel = (lane < 64) if hd == 0 else (lane >= 64)
                sel_t = (sub < 64) if hd == 0 else (sub >= 64)
                k_h = jnp.where(sel, k_ref[0:n, :], zero)
                v_h = jnp.where(sel, v_ref[0:n, :], zero)
                qt_h = jnp.where(sel_t, qt_ref[:, rows], zero)
                dot_h = jnp.where(sel_t, dot_ref[:, rows], zero)
                segs = [(p, lax.dot_general(do2, v_h[c0:c1], (((1,), (1,)), ((), ())), preferred_element_type=F32),
                         c0, c1) for p, c0, c1 in _attn_probs(q2s, k_h, c_ref[hd:hd + 1, 0:n], row0, n)]
                delta = sum(jnp.sum(p * dp, axis=1, keepdims=True) for p, dp, _, _ in segs)
                for p, dp, c0, c1 in segs:
                    ds = p * (dp - delta)
                    dsb = ds.astype(BF16)
                    dq_acc = dq_acc + jnp.dot(dsb, k_h[c0:c1], preferred_element_type=F32)
                    dkt_acc[:, c0:c1] += jnp.dot(qt_h, dsb, preferred_element_type=F32)
                    dvt_acc[:, c0:c1] += jnp.dot(dot_h, p.astype(BF16), preferred_element_type=F32)
                    dc_ref[hd:hd + 1, c0:c1] -= jnp.sum(ds, axis=0, keepdims=True)
            dq_ref[rows, :] = (dq_acc * scale).astype(BF16)
        dkt_ref[...] = (dkt_acc[...] * scale).astype(BF16)
        dvt_ref[...] = dvt_acc[...].astype(BF16)

    blk = lambda off: pl.BlockSpec((lp, 128), lambda p: (0, off + p))
    blk_t = pl.BlockSpec((128, lp), lambda p: (p, 0))
    c_spec = pl.BlockSpec((None, 2, lp), lambda p: (p, 0, 0))
    return pl.pallas_call(
        body, name=name, grid=(4,),
        in_specs=[blk(0), blk(4), blk(8), blk_t, blk(0), blk_t, c_spec],
        out_specs=[blk(0), blk_t, blk_t, c_spec],
        out_shape=[jax.ShapeDtypeStruct((lp, FOX_WIDTH), BF16), jax.ShapeDtypeStruct((FOX_WIDTH, lp), BF16),
                   jax.ShapeDtypeStruct((FOX_WIDTH, lp), BF16), jax.ShapeDtypeStruct((4, 2, lp), F32)],
        scratch_shapes=[pltpu.VMEM((128, lp), F32), pltpu.VMEM((128, lp), F32)],
    )(qkv, qkv, qkv, q_t, dcat, do_t, c3)


def _ln_stats(x):
    mu = jnp.mean(x, axis=-1, keepdims=True)
    xc = x - mu
    var = jnp.mean(xc * xc, axis=-1, keepdims=True)
    rstd = lax.rsqrt(var + LN_EPS)
    return xc * rstd, rstd


def _conv_fwd(agf, conv_w, conv_b, ln_g, ln_b, *, name):
    lp = agf.shape[0]
    nch = lp // CHUNK
    c = CONV_CH

    def body(a_ref, g_ref, w_ref, b_ref, lg_ref, lb_ref, u0_ref, u1_ref, u3_ref, u0s):
        u0s[0:HALO, :] = jnp.zeros((HALO, c), F32)

        def glu(ci, _):
            rows = pl.ds(pl.multiple_of(ci * CHUNK, CHUNK), CHUNK)
            u0 = a_ref[rows, :] * _sigmoid(g_ref[rows, :])
            u0_ref[rows, :] = u0
            u0s[pl.ds(pl.multiple_of(ci * CHUNK + HALO, 8), CHUNK), :] = u0
            return 0

        lax.fori_loop(0, nch, glu, 0)

        def conv(ci, _):
            r0 = pl.multiple_of(ci * CHUNK, CHUNK)
            rows = pl.ds(r0, CHUNK)
            for lg in range(c // 128):
                lanes = slice(lg * 128, (lg + 1) * 128)
                win = u0s[pl.ds(r0, CHUNK + HALO), lanes]
                acc = jnp.broadcast_to(b_ref[:, lanes], (CHUNK, 128))
                for k in range(CONV_WIDTH):
                    s = CONV_WIDTH - 1 - k
                    sh = win if s == 0 else pltpu.roll(win, s, 0)
                    acc = acc + w_ref[k:k + 1, lanes] * sh[HALO:HALO + CHUNK, :]
                u1_ref[rows, lanes] = acc
            xhat, _ = _ln_stats(u1_ref[rows, :])
            y = xhat * lg_ref[...] + lb_ref[...]
            u3_ref[rows, :] = (y * _sigmoid(y)).astype(BF16)
            return 0

        lax.fori_loop(0, nch, conv, 0)

    full = lambda shape: pl.BlockSpec(shape, lambda i: (0, 0))
    return pl.pallas_call(
        body, name=name, grid=(1,),
        in_specs=[pl.BlockSpec((lp, c), lambda i: (0, 0)), pl.BlockSpec((lp, c), lambda i: (0, 1)),
                  full((CONV_WIDTH, c)), full((1, c)), full((1, c)), full((1, c))],
        out_specs=[full((lp, c)), full((lp, c)), full((lp, c))],
        out_shape=[jax.ShapeDtypeStruct((lp, c), F32), jax.ShapeDtypeStruct((lp, c), F32),
                   jax.ShapeDtypeStruct((lp, c), BF16)],
        scratch_shapes=[pltpu.VMEM((lp + HALO, c), F32)],
    )(agf, agf, conv_w, conv_b, ln_g, ln_b)


def _conv_bwd(dcat, u0, u1, agf, conv_w, ln_g, ln_b, *, name):
    lp = agf.shape[0]
    nch = lp // CHUNK
    c = CONV_CH
    wlen = CHUNK + HALO

    def body(du3_ref, u0_ref, u1_ref, a_ref, g_ref, w_ref, lg_ref, lb_ref,
             dag_ref, dw_ref, db_ref, dlg_ref, dlb_ref, du1s, dwacc, vacc):
        du1s[lp:lp + HALO, :] = jnp.zeros((HALO, c), F32)
        dwacc[...] = jnp.zeros_like(dwacc)
        vacc[...] = jnp.zeros_like(vacc)

        def ln_bwd(ci, _):
            r0 = pl.multiple_of(ci * CHUNK, CHUNK)
            rows = pl.ds(r0, CHUNK)
            xhat, rstd = _ln_stats(u1_ref[rows, :])
            y = xhat * lg_ref[...] + lb_ref[...]
            sg = _sigmoid(y)
            du2 = du3_ref[rows, :] * (sg * (1.0 + y * (1.0 - sg)))
            vacc[0:8, :] += _fold8(du2 * xhat)
            vacc[8:16, :] += _fold8(du2)
            dxhat = du2 * lg_ref[...]
            du1 = rstd * (dxhat - jnp.mean(dxhat, axis=-1, keepdims=True)
                          - xhat * jnp.mean(dxhat * xhat, axis=-1, keepdims=True))
            vacc[16:24, :] += _fold8(du1)
            du1s[rows, :] = du1
            return 0

        lax.fori_loop(0, nch, ln_bwd, 0)

        def conv_bwd(ci, _):
            r0 = pl.multiple_of(ci * CHUNK, CHUNK)
            rows = pl.ds(r0, CHUNK)
            for lg in range(c // 128):
                lanes = slice(lg * 128, (lg + 1) * 128)
                dwin = du1s[pl.ds(r0, wlen), lanes]
                u0 = u0_ref[rows, lanes]
                acc = jnp.zeros((CHUNK, 128), F32)
                for k in range(CONV_WIDTH):
                    s = CONV_WIDTH - 1 - k
                    d_s = (dwin if s == 0 else pltpu.roll(dwin, wlen - s, 0))[0:CHUNK, :]
                    acc = acc + w_ref[k:k + 1, lanes] * d_s
                    dwacc[8 * k:8 * k + 8, lanes] += _fold8(d_s * u0)
                sg = _sigmoid(g_ref[rows, lanes])
                a = a_ref[rows, lanes]
                dag_ref[rows, lanes] = (acc * sg).astype(BF16)
                dag_ref[rows, slice(c + lg * 128, c + (lg + 1) * 128)] = (acc * a * sg * (1.0 - sg)).astype(BF16)
            return 0

        lax.fori_loop(0, nch, conv_bwd, 0)
        for k in range(CONV_WIDTH):
            dw_ref[k:k + 1, :] = jnp.sum(dwacc[8 * k:8 * k + 8, :], axis=0, keepdims=True)
        dlg_ref[...] = jnp.sum(vacc[0:8, :], axis=0, keepdims=True)
        dlb_ref[...] = jnp.sum(vacc[8:16, :], axis=0, keepdims=True)
        db_ref[...] = jnp.sum(vacc[16:24, :], axis=0, keepdims=True)

    full = lambda shape: pl.BlockSpec(shape, lambda i: (0, 0))
    vec = jax.ShapeDtypeStruct((1, c), F32)
    return pl.pallas_call(
        body, name=name, grid=(1,),
        in_specs=[pl.BlockSpec((lp, c), lambda i: (0, 1)), full((lp, c)), full((lp, c)),
                  pl.BlockSpec((lp, c), lambda i: (0, 0)), pl.BlockSpec((lp, c), lambda i: (0, 1)),
                  full((CONV_WIDTH, c)), full((1, c)), full((1, c))],
        out_specs=[full((lp, 2 * c)), full((CONV_WIDTH, c)), full((1, c)), full((1, c)), full((1, c))],
        out_shape=[jax.ShapeDtypeStruct((lp, 2 * c), BF16), jax.ShapeDtypeStruct((CONV_WIDTH, c), F32), vec, vec, vec],
        scratch_shapes=[pltpu.VMEM((lp + HALO, c), F32), pltpu.VMEM((8 * CONV_WIDTH, c), F32),
                        pltpu.VMEM((24, c), F32)],
    )(dcat, u0, u1, agf, agf, conv_w, ln_g, ln_b)


FFN_TILE = 256
FFN_PAD = 8


def _ffn_conv(xs, w_ref, b_ref, half, r0):
    win = xs[half, pl.ds(r0, CHUNK + FFN_PAD), :]
    acc = jnp.broadcast_to(b_ref[half], (CHUNK, FFN_TILE))
    for k in range(FFN_CONV_WIDTH):
        s = FFN_CONV_WIDTH - 1 - k
        sh = win if s == 0 else pltpu.roll(win, s, 0)
        acc = acc + w_ref[half, k:k + 1, :] * sh[FFN_PAD:FFN_PAD + CHUNK, :]
    return acc


def _ffn_act_fwd(up3, w3, b3, *, name):
    _, lp, f = up3.shape
    nch = lp // CHUNK

    def body(up_ref, w_ref, b_ref, act_ref, act_t_ref, gv_ref, xs):
        for half in range(2):
            xs[half, 0:FFN_PAD, :] = jnp.zeros((FFN_PAD, FFN_TILE), F32)
            xs[half, FFN_PAD:FFN_PAD + lp, :] = up_ref[half].astype(F32)

        def chunk(ci, _):
            r0 = pl.multiple_of(ci * CHUNK, CHUNK)
            rows = pl.ds(r0, CHUNK)
            gate = _ffn_conv(xs, w_ref, b_ref, 0, r0)
            val = _ffn_conv(xs, w_ref, b_ref, 1, r0)
            gv_ref[0, rows, :] = gate.astype(BF16)
            gv_ref[1, rows, :] = val.astype(BF16)
            act = gate * _sigmoid(gate) * val
            act_ref[rows, :] = act.astype(BF16)
            act_t_ref[:, rows] = act.T.astype(BF16)
            return 0

        lax.fori_loop(0, nch, chunk, 0, unroll=True)

    halves = pl.BlockSpec((2, lp, FFN_TILE), lambda j: (0, 0, j))
    return pl.pallas_call(
        body, name=name, grid=(f // FFN_TILE,),
        in_specs=[halves, pl.BlockSpec((2, FFN_CONV_WIDTH, FFN_TILE), lambda j: (0, 0, j)),
                  pl.BlockSpec((2, 1, FFN_TILE), lambda j: (0, 0, j))],
        out_specs=[pl.BlockSpec((lp, FFN_TILE), lambda j: (0, j)), pl.BlockSpec((FFN_TILE, lp), lambda j: (j, 0)),
                   halves],
        out_shape=[jax.ShapeDtypeStruct((lp, f), BF16), jax.ShapeDtypeStruct((f, lp), BF16),
                   jax.ShapeDtypeStruct((2, lp, f), BF16)],
        scratch_shapes=[pltpu.VMEM((2, lp + FFN_PAD, FFN_TILE), F32)],
    )(up3, w3, b3)


def _ffn_act_bwd(up3, gv3, w3, dact, *, name):
    _, lp, f = up3.shape
    nch = lp // CHUNK
    wlen = CHUNK + FFN_PAD

    def body(up_ref, gv_ref, w_ref, dact_ref, dup_ref, dw_ref, db_ref, ds, wacc):
        for half in range(2):
            ds[half, lp:lp + FFN_PAD, :] = jnp.zeros((FFN_PAD, FFN_TILE), F32)
        wacc[...] = jnp.zeros_like(wacc)

        def act_bwd(ci, _):
            rows = pl.ds(pl.multiple_of(ci * CHUNK, CHUNK), CHUNK)
            gate, val = gv_ref[0, rows, :].astype(F32), gv_ref[1, rows, :].astype(F32)
            sg = _sigmoid(gate)
            da = dact_ref[rows, :].astype(F32)
            ds[0, rows, :] = da * val * (sg * (1.0 + gate * (1.0 - sg)))
            ds[1, rows, :] = da * (gate * sg)
            return 0

        lax.fori_loop(0, nch, act_bwd, 0, unroll=True)

        def conv_bwd(ci, _):
            r0 = pl.multiple_of(ci * CHUNK, CHUNK)
            rows = pl.ds(r0, CHUNK)
            for half in range(2):
                dwin = ds[half, pl.ds(r0, wlen), :]
                x = up_ref[half, rows, :].astype(F32)
                acc = jnp.zeros((CHUNK, FFN_TILE), F32)
                for k in range(FFN_CONV_WIDTH):
                    s = FFN_CONV_WIDTH - 1 - k
                    d_s = (dwin if s == 0 else pltpu.roll(dwin, wlen - s, 0))[0:CHUNK, :]
                    acc = acc + w_ref[half, k:k + 1, :] * d_s
                    wacc[half, 8 * k:8 * k + 8, :] += _fold8(d_s * x)
                wacc[half, 24:32, :] += _fold8(dwin[0:CHUNK, :])
                dup_ref[half, rows, :] = acc.astype(BF16)
            return 0

        lax.fori_loop(0, nch, conv_bwd, 0, unroll=True)
        for half in range(2):
            for k in range(FFN_CONV_WIDTH):
                dw_ref[half, k:k + 1, :] = jnp.sum(wacc[half, 8 * k:8 * k + 8, :], axis=0, keepdims=True)
            db_ref[half] = jnp.sum(wacc[half, 24:32, :], axis=0, keepdims=True)

    halves = pl.BlockSpec((2, lp, FFN_TILE), lambda j: (0, 0, j))
    taps = pl.BlockSpec((2, FFN_CONV_WIDTH, FFN_TILE), lambda j: (0, 0, j))
    bias = pl.BlockSpec((2, 1, FFN_TILE), lambda j: (0, 0, j))
    return pl.pallas_call(
        body, name=name, grid=(f // FFN_TILE,),
        in_specs=[halves, halves, taps, pl.BlockSpec((lp, FFN_TILE), lambda j: (0, j))],
        out_specs=[halves, taps, bias],
        out_shape=[jax.ShapeDtypeStruct((2, lp, f), BF16), jax.ShapeDtypeStruct((2, FFN_CONV_WIDTH, f), F32),
                   jax.ShapeDtypeStruct((2, 1, f), F32)],
        scratch_shapes=[pltpu.VMEM((2, lp + FFN_PAD, FFN_TILE), F32), pltpu.VMEM((2, 32, FFN_TILE), F32)],
    )(up3, gv3, w3, dact)


POOL_PAD = 16


def _inv_count(r0, w):
    t = r0 + lax.broadcasted_iota(jnp.int32, (CHUNK, 1), 0)
    return 1.0 / jnp.minimum(t + 1, w).astype(F32)


def _pool_fwd(n, pool_w, pool_b, pool_scale, h, *, name):
    lp, dm = n.shape
    nch = lp // CHUNK
    g = POOL_GROUP

    def body(n_ref, w_ref, b_ref, s_ref, h_ref, ho_ref, dt_ref, z_ref, xs, d_ref):
        gi = pl.program_id(0)
        xs[0:POOL_PAD, :] = jnp.zeros((POOL_PAD, g), F32)
        xs[POOL_PAD:POOL_PAD + lp, :] = n_ref[...]
        for idx, w in enumerate(POOL_WINDOWS):
            @pl.when(gi == idx)
            def _(w=w):
                def chunk(ci, _):
                    r0 = pl.multiple_of(ci * CHUNK, CHUNK)
                    win = xs[pl.ds(r0, CHUNK + POOL_PAD), :]
                    acc = win
                    for j in range(1, w):
                        acc = acc + pltpu.roll(win, j, 0)
                    x = win[POOL_PAD:POOL_PAD + CHUNK, :]
                    d = acc[POOL_PAD:POOL_PAD + CHUNK, :] * _inv_count(r0, w) - x
                    d_ref[pl.ds(r0, CHUNK), :] = d.astype(BF16)
                    dt_ref[:, pl.ds(r0, CHUNK)] = d.T.astype(BF16)
                    return 0

                lax.fori_loop(0, nch, chunk, 0, unroll=True)

        z = jnp.dot(d_ref[...], w_ref[...], preferred_element_type=F32) + b_ref[...]
        z_ref[...] = z
        ho_ref[...] = h_ref[...] + z * s_ref[...]

    col = pl.BlockSpec((lp, g), lambda i: (0, i))
    vec = pl.BlockSpec((1, g), lambda i: (0, i))
    return pl.pallas_call(
        body, name=name, grid=(len(POOL_WINDOWS),),
        in_specs=[col, pl.BlockSpec((None, g, g), lambda i: (i, 0, 0)), vec, vec, col],
        out_specs=[col, pl.BlockSpec((g, lp), lambda i: (i, 0)), col],
        out_shape=[jax.ShapeDtypeStruct((lp, dm), F32), jax.ShapeDtypeStruct((dm, lp), BF16),
                   jax.ShapeDtypeStruct((lp, dm), F32)],
        scratch_shapes=[pltpu.VMEM((lp + POOL_PAD, g), F32), pltpu.VMEM((lp, g), BF16)],
    )(n, pool_w, pool_b, pool_scale, h)


def _pool_bwd(dy, z, pool_w, pool_scale, *, name):
    lp, dm = dy.shape
    nch = lp // CHUNK
    g = POOL_GROUP
    wlen = CHUNK + POOL_PAD

    def body(dy_ref, z_ref, w_ref, s_ref, dn_ref, dz_ref, dsc_ref, db_ref, ys, dd):
        gi = pl.program_id(0)
        dyv = dy_ref[...]
        dsc_ref[...] = jnp.sum(dyv * z_ref[...], axis=0, keepdims=True)
        dz = dyv * s_ref[...]
        db_ref[...] = jnp.sum(dz, axis=0, keepdims=True)
        dzb = dz.astype(BF16)
        dz_ref[...] = dzb
        dd[...] = lax.dot_general(dzb, w_ref[...], (((1,), (1,)), ((), ())), preferred_element_type=F32)
        ys[lp:lp + POOL_PAD, :] = jnp.zeros((POOL_PAD, g), F32)
        for idx, w in enumerate(POOL_WINDOWS):
            @pl.when(gi == idx)
            def _(w=w):
                def scale(ci, _):
                    r0 = pl.multiple_of(ci * CHUNK, CHUNK)
                    ys[pl.ds(r0, CHUNK), :] = dd[pl.ds(r0, CHUNK), :] * _inv_count(r0, w)
                    return 0

                lax.fori_loop(0, nch, scale, 0, unroll=True)

                def chunk(ci, _):
                    r0 = pl.multiple_of(ci * CHUNK, CHUNK)
                    win = ys[pl.ds(r0, wlen), :]
                    acc = win
                    for j in range(1, w):
                        acc = acc + pltpu.roll(win, wlen - j, 0)
                    dn_ref[pl.ds(r0, CHUNK), :] = acc[0:CHUNK, :] - dd[pl.ds(r0, CHUNK), :]
                    return 0

                lax.fori_loop(0, nch, chunk, 0, unroll=True)

    col = pl.BlockSpec((lp, g), lambda i: (0, i))
    vec = pl.BlockSpec((1, g), lambda i: (0, i))
    return pl.pallas_call(
        body, name=name, grid=(len(POOL_WINDOWS),),
        in_specs=[col, col, pl.BlockSpec((None, g, g), lambda i: (i, 0, 0)), vec],
        out_specs=[col, col, vec, vec],
        out_shape=[jax.ShapeDtypeStruct((lp, dm), F32), jax.ShapeDtypeStruct((lp, dm), BF16),
                   jax.ShapeDtypeStruct((1, dm), F32), jax.ShapeDtypeStruct((1, dm), F32)],
        scratch_shapes=[pltpu.VMEM((lp + POOL_PAD, g), F32), pltpu.VMEM((lp, g), F32)],
    )(dy, z, pool_w, pool_scale)


def _ffn_fwd(h, g, w_up_t, w3, b3, w_down, tag):
    lp = h.shape[0]
    nj = D_FF // FFN_TILE
    n, n_t = _rms_fwd(h, g, name=f"rms_ffn{tag}", out_dtype=BF16, transposed=True)
    up2 = _mm(n, w_up_t, name=f"mm_up{tag}", tb=True, tn=FFN_TILE, dims=(lp, 2 * D_FF, D_MODEL),
              o_map=lambda i, j, k: (j // nj, j % nj), out_shape=(2 * lp, D_FF), out_dtype=BF16)
    up3 = up2.reshape(2, lp, D_FF)
    act, act_t, gv3 = _ffn_act_fwd(up3, w3, b3, name=f"ffn_act{tag}")
    h_out = _mm(act, w_down, name=f"mm_down{tag}", tn=256, res=h)
    return h_out, (n_t, up3, gv3, act_t)


def _ffn_bwd(dh, dhb, h, g, saved, w_up_t, w3, w_down, tag, after=None):
    lp = h.shape[0]
    n_t, up3, gv3, act_t = saved
    dw_down = _mm(act_t, dhb, name=f"mm_dwdown{tag}", tm=704, out_dtype=BF16)
    dact = _mm(dhb, w_down, name=f"mm_dact{tag}", tb=True, tn=256, out_dtype=BF16, after=after)
    dup3, dcw, dcb = _ffn_act_bwd(up3, gv3, w3, dact, name=f"ffn_act_bwd{tag}")
    dup2 = dup3.reshape(2 * lp, D_FF)
    dw_up = _mm_dw_up(n_t, dup2, name=f"mm_dwup{tag}")
    dn = _mm(dup2, w_up_t, name=f"mm_dnffn{tag}", tm=lp // 2, tk=D_FF // 2, dims=(lp, D_MODEL, 2 * D_FF),
             a_map=lambda i, j, k: (2 * (k // 2) + i, k % 2))
    dh_in, dh_in_b, dg = _rms_bwd(h, g, dn, dh, name=f"rms_bwd_ffn{tag}")
    return dh_in, dh_in_b, (dg, dw_up, dcw, dcb, dw_down)


def _local_step(x, tgt, wt):
    seq = x.shape[0]
    n_real = N_META + seq
    lp = _round_up(n_real, CHUNK)
    pad = jnp.zeros((lp - n_real, D_MODEL), F32)
    h0 = jnp.concatenate([wt["meta"], x, pad], axis=0)
    tgt_p = jnp.concatenate([jnp.zeros((N_META, D_MODEL), F32), tgt, pad], axis=0)
    w_in_p = wt["w_in_p"]

    n0, n0_t = _rms_fwd(h0, wt["g_even"], name="rms_even", out_dtype=BF16, after=wt["ffn_started"], transposed=True)
    qkv = _mm(n0, w_in_p, name="mm_qkv", tn=512, dims=(lp, 3 * FOX_WIDTH, D_MODEL), out_dtype=BF16)
    ag = _mm(n0, w_in_p, name="mm_ag", tn=512, dims=(lp, 2 * CONV_CH, D_MODEL),
             b_map=lambda i, j, k: (0, 3 + j))
    f_t = _mm(wt["wf_t"], n0, name="mm_ft", tb=True)
    c_row = _fox_prep(f_t, wt["b_f"], name="fox_prep")
    c3 = c_row.reshape(4, 2, lp)
    o = _attn_fwd(qkv, c3, name="attn_fwd")
    u0, u1, u3 = _conv_fwd(ag, wt["conv_w"], wt["conv_b"], wt["ln_g"], wt["ln_b"], name="conv_fwd")
    cat = jnp.concatenate([o, u3], axis=1)
    h1 = _mm(cat, wt["w_out"], name="mm_out", tn=256, res=h0)
    w_up0, w_down0 = wt["ffn_weights"](0, h1)
    h2, saved0 = _ffn_fwd(h1, wt["ffn_norm"][0:1], w_up0, wt["fcw3"][0], wt["fcb3"][0], w_down0, 0)

    n2 = _rms_fwd(h2, wt["g_odd"], name="rms_odd", out_dtype=F32)
    h3, dpool_t, z = _pool_fwd(n2, wt["pool_w"], wt["pool_b"], wt["pool_scale"], h2, name="pool_fwd")
    w_up1, w_down1 = wt["ffn_weights"](1, h3)
    h4, saved1 = _ffn_fwd(h3, wt["ffn_norm"][1:2], w_up1, wt["fcw3"][1], wt["fcb3"][1], w_down1, 1)

    loss, dh4, dh4b, d_gfinal = _loss_head(h4, wt["g_final"], tgt_p, n_real, name="loss_head")

    dh3, dh3b, gf1 = _ffn_bwd(dh4, dh4b, h3, wt["ffn_norm"][1:2], saved1, w_up1, wt["fcw3"][1], w_down1, 1)
    send1, token1 = _send_ffn_grads(gf1[1], gf1[4], 1)
    dn2, dzb, d_pscale, d_pb = _pool_bwd(dh3, z, wt["pool_w"], wt["pool_scale"], name="pool_bwd")
    d_pw = _mm(dpool_t, dzb, name="mm_dpoolw", tm=POOL_GROUP, tn=POOL_GROUP, dims=(D_MODEL, POOL_GROUP, lp),
               b_map=lambda i, j, k: (0, i), o_map=lambda i, j, k: (i, 0), out_shape=(D_MODEL, POOL_GROUP),
               out_dtype=BF16)
    dh2, dh2b, d_godd = _rms_bwd(h2, wt["g_odd"], dn2, dh3, name="rms_bwd_odd")
    dh1, dh1b, gf0 = _ffn_bwd(dh2, dh2b, h1, wt["ffn_norm"][0:1], saved0, w_up0, wt["fcw3"][0], w_down0, 0,
                              after=token1)

    send0, token0 = _send_ffn_grads(gf0[1], gf0[4], 0)
    cat_t = _transpose(cat, name="t_cat", out_dtype=BF16)
    d_wout = _mm(cat_t, dh1b, name="mm_dwout", tm=512, out_dtype=BF16)
    dcat = _mm(dh1b, wt["w_out"], name="mm_dcat", tb=True, tn=256, after=token0)
    dag, d_convw, d_convb, d_lng, d_lnb = _conv_bwd(dcat, u0, u1, ag, wt["conv_w"], wt["ln_g"], wt["ln_b"],
                                                    name="conv_bwd")
    layers = lambda i: jnp.stack([gf0[i], gf1[i]])
    grads = dict(
        conv_w=d_convw[None], w_out=d_wout, mix_norm_odd=d_godd,
        pool_w=d_pw.reshape(len(POOL_WINDOWS), POOL_GROUP, POOL_GROUP),
        pool_b=d_pb.reshape(1, len(POOL_WINDOWS), POOL_GROUP), pool_scale=d_pscale, w_up=(gf0[1], gf1[1]),
        ffn_conv_w=layers(2).transpose(0, 2, 1, 3).reshape(DEPTH, FFN_CONV_WIDTH, 2 * D_FF), w_down=(gf0[4], gf1[4]))
    send_rest, token_rest, grads["little_slabs"] = _send_rest_grads(grads)
    q_t = _transpose(qkv, name="t_q", out_dtype=BF16, cols=FOX_WIDTH, after=token_rest)
    do_t = _transpose(dcat, name="t_do", out_dtype=BF16, cols=FOX_WIDTH)
    dq, dk_t, dv_t, dc3 = _attn_bwd(qkv, q_t, dcat, do_t, c3, name="attn_bwd")
    dk = _transpose(dk_t, name="t_dk", out_dtype=BF16)
    dv = _transpose(dv_t, name="t_dv", out_dtype=BF16)
    df_t, d_bf = _fox_bwd(dc3.reshape(FOX_HEADS, lp), f_t, wt["b_f"], name="fox_bwd")
    df = _transpose(df_t, name="t_df", out_dtype=BF16)
    dproj = jnp.concatenate([dq, dk, dv, dag, df], axis=1)
    grads["w_in"] = _mm_dw_in(n0_t, dproj, name="mm_dwin")
    send_in, token_in = _send_start(
        [grads["w_in"]], [jax.ShapeDtypeStruct((N_DEV - 1, D_MODEL, _IN_SHARD), BF16)], [(0, _by_owner, 0, None)],
        name="send_w_in")
    dn0 = _mm(dproj, w_in_p, name="mm_dn0", tb=True, tm=lp // 2, tk=896, after=token_in)
    dh0, _, d_geven = _rms_bwd(h0, wt["g_even"], dn0, dh1, name="rms_bwd_even")
    grads.update(
        meta_tokens=dh0[0:N_META], mix_norm_even=d_geven, b_f=d_bf.reshape(1, FOX_HEADS), conv_b=d_convb, ln_g=d_lng,
        ln_b=d_lnb, ffn_norm=jnp.concatenate([gf0[0], gf1[0]], axis=0),
        ffn_conv_b=layers(3).reshape(DEPTH, 2 * D_FF), final_norm=d_gfinal.reshape(D_MODEL),
        sends=(send0, send1, send_rest, send_in))
    return loss, dh0[N_META:n_real], grads


_LITTLE = (("conv_w", (1, 31, 512), 2), ("mix_norm_odd", (1, 1024), 1), ("pool_b", (1, 4, 256), 2),
           ("pool_scale", (1, 1024), 1), ("ffn_conv_w", (2, 3, 5632), 2))


def _send_rest_grads(g):
    out_rows, pool_rows, groups = D_MODEL // N_DEV, POOL_GROUP // N_DEV, len(POOL_WINDOWS)
    little_slabs = _pack([_full_to_slabs(g[n], s, a) for n, s, a in _LITTLE], F32, lead=(N_DEV,), align=8)
    land = lambda shape, dtype: jax.ShapeDtypeStruct((N_DEV - 1,) + shape, dtype)
    handle, token = _send_start(
        [g["w_out"], g["pool_w"], little_slabs],
        [land((out_rows, D_MODEL), BF16), land((groups, pool_rows, POOL_GROUP), BF16), land(little_slabs.shape[1:], F32)],
        [(0, _row_block(out_rows), 0, None), (1, _row_block(pool_rows, axis=1), 1, None), (2, _by_owner, 2, None)],
        name="send_rest")
    return handle, token, little_slabs


def _send_ffn_grads(dw_up, dw_down, tag):
    rows = D_FF // N_DEV
    lands = [jax.ShapeDtypeStruct((N_DEV - 1,) + dw_up.shape[1:], BF16),
             jax.ShapeDtypeStruct((N_DEV - 1, rows, D_MODEL), BF16)]
    return _send_start([dw_up, dw_down], lands, [(0, _by_owner, 0, None), (1, _row_block(rows), 1, None)],
                       name=f"send_ffn{tag}")


_QKV = 3 * FOX_WIDTH
_GLU0 = _QKV + FOX_HEADS
_IN_COLS = _GLU0 + 2 * CONV_CH
_F_PAD = 128


_IN_SHARD = _IN_COLS // N_DEV
_UP_SHARD = 2 * D_FF // N_DEV
_ROW_TILE = 256


def _assemble_w_in(st, *, name):
    tr = _ROW_TILE

    def body(s_ref, o_ref):
        full = jnp.concatenate([s_ref[i].astype(F32) for i in range(N_DEV)], axis=1)
        parts = [full[:, :_QKV], full[:, _GLU0:], full[:, _QKV:_GLU0], jnp.zeros((tr, _F_PAD - FOX_HEADS), F32)]
        o_ref[...] = jnp.concatenate(parts, axis=1).astype(BF16)

    return pl.pallas_call(
        body, name=name, grid=(D_MODEL // tr,),
        in_specs=[pl.BlockSpec((N_DEV, tr, _IN_SHARD), lambda i: (0, i, 0))],
        out_specs=pl.BlockSpec((tr, _QKV + 2 * CONV_CH + _F_PAD), lambda i: (i, 0)),
        out_shape=jax.ShapeDtypeStruct((D_MODEL, _QKV + 2 * CONV_CH + _F_PAD), BF16),
    )(st)


def _mm_dw_in(n_t, dproj, *, name):
    dm, lp = n_t.shape
    tr = _ROW_TILE
    ag0 = _QKV + 2 * CONV_CH

    def body(a_ref, b_ref, o_ref):
        r = jnp.dot(a_ref[...], b_ref[...], preferred_element_type=F32)
        full = jnp.concatenate([r[:, :_QKV], r[:, ag0:ag0 + FOX_HEADS], r[:, _QKV:ag0]], axis=1)
        for i in range(N_DEV):
            o_ref[i] = full[:, i * _IN_SHARD:(i + 1) * _IN_SHARD].astype(BF16)

    return pl.pallas_call(
        body, name=name, grid=(dm // tr,),
        in_specs=[pl.BlockSpec((tr, lp), lambda i: (i, 0)), pl.BlockSpec(dproj.shape, lambda i: (0, 0))],
        out_specs=pl.BlockSpec((N_DEV, tr, _IN_SHARD), lambda i: (0, i, 0)),
        out_shape=jax.ShapeDtypeStruct((N_DEV, dm, _IN_SHARD), BF16),
    )(n_t, dproj)


def _mm_dw_up(n_t, dup2, *, name):
    dm, lp = n_t.shape
    pairs_per_half = D_FF // (2 * _UP_SHARD)

    def body(a_ref, b_ref, o_ref):
        r_t = jnp.dot(a_ref[...], b_ref[...], preferred_element_type=F32).T
        o_ref[0] = r_t[:_UP_SHARD, :].astype(BF16)
        o_ref[1] = r_t[_UP_SHARD:, :].astype(BF16)

    return pl.pallas_call(
        body, name=name, grid=(N_DEV // 2,),
        in_specs=[pl.BlockSpec((dm, lp), lambda p: (0, 0)),
                  pl.BlockSpec((lp, 2 * _UP_SHARD), lambda p: (p // pairs_per_half, p % pairs_per_half))],
        out_specs=pl.BlockSpec((2, _UP_SHARD, dm), lambda p: (p, 0, 0)),
        out_shape=jax.ShapeDtypeStruct((N_DEV, _UP_SHARD, dm), BF16),
    )(n_t, dup2)


MESH = pl.DeviceIdType.MESH
ANY = pl.BlockSpec(memory_space=pl.ANY)


def _slot(px, py, pc):
    return 4 * px + 2 * py + pc


def _by_owner(ref, slot):
    return ref.at[slot]


def _row_block(rows, axis=0):
    def place(ref, slot):
        idx = (slice(None),) * axis + (pl.ds(slot * rows, rows),)
        return ref.at[idx]
    return place


def _all_gather(arrs, out_shapes, places, *, name):
    n = len(arrs)

    def body(*refs):
        ins, outs = refs[:n], refs[n:2 * n]
        send_sems, recv_sems, local_sems = refs[2 * n:]
        x, y, c = lax.axis_index("x"), lax.axis_index("y"), lax.axis_index("c")
        me, sibling = (x, y, c), (x, y, 1 - c)
        chips = [(1 - x, y), (x, 1 - y), (1 - x, 1 - y)]

        def copy(a, k, block, to, from_input=False):
            dst = places[a](outs[a], _slot(*block))
            return pltpu.make_async_remote_copy(
                src_ref=ins[a] if from_input else dst, dst_ref=dst,
                send_sem=send_sems.at[7 * a + k], recv_sem=recv_sems.at[7 * a + k],
                device_id=to, device_id_type=MESH)

        own, sent = [], []
        for a in range(n):
            mine = pltpu.make_async_copy(ins[a], places[a](outs[a], _slot(*me)), local_sems.at[a])
            mine.start()
            own.append(mine)
            first = [copy(a, 0, me, sibling, True)]
            first += [copy(a, 1 + j, me, (*chip, c), True) for j, chip in enumerate(chips)]
            for cp in first:
                cp.start()
            sent += first
        for a in range(n):
            for j, chip in enumerate(chips):
                copy(a, 1 + j, (*chip, c), me).wait_recv()
                passed = copy(a, 4 + j, (*chip, c), sibling)
                passed.start()
                sent.append(passed)
        for a in range(n):
            copy(a, 0, sibling, me).wait_recv()
            for j, chip in enumerate(chips):
                copy(a, 4 + j, (*chip, 1 - c), me).wait_recv()
        for cp in sent:
            cp.wait_send()
        for cp in own:
            cp.wait()

    return pl.pallas_call(
        body, name=name,
        in_specs=[ANY] * n, out_specs=[ANY] * n,
        out_shape=[jax.ShapeDtypeStruct(s, a.dtype) for s, a in zip(out_shapes, arrs)],
        scratch_shapes=[pltpu.SemaphoreType.DMA((7 * n,)), pltpu.SemaphoreType.DMA((7 * n,)),
                        pltpu.SemaphoreType.DMA((n,))],
    )(*arrs)


HBM = pl.BlockSpec(memory_space=pltpu.HBM)
SEM = pl.BlockSpec(memory_space=pltpu.SEMAPHORE)
EFFECT = pltpu.SideEffectType.DATAFLOW_SIDE_EFFECTING


def _relation_copies(src_refs, land_refs, copies, send_sems, recv_sems):
    x, y, c = lax.axis_index("x"), lax.axis_index("y"), lax.axis_index("c")
    flip = lambda v, bit: 1 - v if bit else v
    out = []
    for k in range(1, N_DEV):
        p = (flip(x, k & 4), flip(y, k & 2), flip(c, k & 1))
        for j, (si, take, li, put) in enumerate(copies):
            sem = (k - 1) * len(copies) + j
            dst = land_refs[li].at[k - 1] if put is None else put(land_refs[li], _slot(x, y, c))
            out.append(pltpu.make_async_remote_copy(
                src_ref=take(src_refs[si], _slot(*p)), dst_ref=dst,
                send_sem=send_sems.at[sem], recv_sem=recv_sems.at[sem], device_id=p, device_id_type=MESH))
    return out


def _own_copies(src_refs, land_refs, copies, sems):
    me = _slot(lax.axis_index("x"), lax.axis_index("y"), lax.axis_index("c"))
    placed = [(si, take, li, put) for si, take, li, put in copies if put is not None]
    return [pltpu.make_async_copy(take(src_refs[si], me), put(land_refs[li], me),
                                  sems.at[(N_DEV - 1) * len(copies) + j])
            for j, (si, take, li, put) in enumerate(placed)]


def _send_start(srcs, land_structs, copies, *, name, after=None):
    ns, nl = len(srcs), len(land_structs)
    n_sem = (N_DEV - 1) * len(copies) + sum(put is not None for _, _, _, put in copies)
    behind = [] if after is None else [after]

    def body(*refs):
        first_out = ns + nl + len(behind)
        send_sems, recv_sems, token = refs[first_out], refs[first_out + 1], refs[-1]
        for cp in _relation_copies(refs[:ns], refs[ns:ns + nl], copies, send_sems, recv_sems):
            cp.start()
        for cp in _own_copies(refs[:ns], refs[ns:ns + nl], copies, send_sems):
            cp.start()
        token[...] = jnp.zeros_like(token)

    in_hbm = lambda a: pltpu.with_memory_space_constraint(a, pltpu.HBM)
    outs = pl.pallas_call(
        body, name=name,
        out_shape=(pltpu.SemaphoreType.DMA((n_sem,)), pltpu.SemaphoreType.DMA((n_sem,)),
                   *[pltpu.HBM(s.shape, s.dtype) for s in srcs],
                   *[pltpu.HBM(s.shape, s.dtype) for s in land_structs],
                   jax.ShapeDtypeStruct((8, 128), F32)),
        in_specs=(HBM,) * (ns + nl) + (ANY,) * len(behind),
        out_specs=(SEM, SEM) + (HBM,) * (ns + nl) + (pl.BlockSpec(memory_space=pltpu.VMEM),),
        input_output_aliases={i: 2 + i for i in range(ns + nl)},
        compiler_params=pltpu.CompilerParams(has_side_effects=EFFECT),
    )(*[in_hbm(s) for s in srcs], *[in_hbm(lax.empty(s.shape, s.dtype)) for s in land_structs], *behind)
    return (outs[0], outs[1], outs[2:2 + ns], outs[2 + ns:2 + ns + nl], copies), outs[-1]


def _send_wait(handle, after, *, name):
    send_sems, recv_sems, srcs, lands, copies = handle
    ns, nl = len(srcs), len(lands)

    def body(*refs):
        for cp in _relation_copies(refs[:ns], refs[ns:ns + nl], copies, refs[ns + nl], refs[ns + nl + 1]):
            cp.wait_send()
            cp.wait_recv()
        for cp in _own_copies(refs[:ns], refs[ns:ns + nl], copies, refs[ns + nl]):
            cp.wait()

    outs = pl.pallas_call(
        body, name=name,
        out_shape=tuple(pltpu.HBM(a.shape, a.dtype) for a in (*srcs, *lands)),
        in_specs=(HBM,) * (ns + nl) + (SEM, SEM, ANY), out_specs=(HBM,) * (ns + nl),
        input_output_aliases={i: i for i in range(ns + nl)},
        compiler_params=pltpu.CompilerParams(has_side_effects=EFFECT),
    )(*srcs, *lands, send_sems, recv_sems, after)
    return outs[:ns], outs[ns:]


def _sum_slabs(stack, *, name, own=None):
    n, rows, w = stack.shape

    def body(*refs):
        s_ref, o_ref = refs[-2], refs[-1]
        acc = s_ref[0] if own is None else refs[0][...] + s_ref[0]
        for i in range(1, n):
            acc = acc + s_ref[i]
        o_ref[...] = acc

    return pl.pallas_call(body, name=name, out_shape=jax.ShapeDtypeStruct((rows, w), F32))(
        *([] if own is None else [own]), stack)


def _adam_math(w, g, m, v):
    mn = ADAM_B1 * m + (1.0 - ADAM_B1) * g
    vn = ADAM_B2 * v + (1.0 - ADAM_B2) * (g * g)
    m_hat = mn / (1.0 - ADAM_B1 ** ADAM_STEP)
    v_hat = vn / (1.0 - ADAM_B2 ** ADAM_STEP)
    return -ADAM_LR * (m_hat / (jnp.sqrt(v_hat) + ADAM_EPS) + ADAM_WD * w), mn, vn


def _adamw_many(ws, gs, ms, vs, *, name):
    n = len(ws)

    def body(*refs):
        for i in range(n):
            w_ref, g_ref, m_ref, v_ref = (refs[j * n + i] for j in range(4))
            d_ref, mo_ref, vo_ref = refs[4 * n + 3 * i:4 * n + 3 * i + 3]
            d_ref[...], mo_ref[...], vo_ref[...] = _adam_math(w_ref[...], g_ref[...], m_ref[...], v_ref[...])

    return pl.pallas_call(
        body, name=name, out_shape=[jax.ShapeDtypeStruct(w.shape, F32) for w in ws for _ in range(3)],
    )(*ws, *gs, *ms, *vs)


def _adamw_layers(w, owns, lands, m, v, tr, *, name):
    nl, rows, cols = w.shape
    steps = rows // tr
    assert rows % tr == 0

    def body(*refs):
        w_ref, m_ref, v_ref = refs[:3]
        own_refs, land_refs = refs[3:3 + nl], refs[3 + nl:3 + 2 * nl]
        g_ref, d_ref, mo_ref, vo_ref = refs[3 + 2 * nl:]
        for li in range(nl):
            @pl.when(pl.program_id(0) == li)
            def _(li=li):
                g = own_refs[li][...].astype(F32)
                for k in range(N_DEV - 1):
                    g = g + land_refs[li][k].astype(F32)
                g_ref[...] = g
                d_ref[...], mo_ref[...], vo_ref[...] = _adam_math(w_ref[...], g, m_ref[...], v_ref[...])

    def held(li):
        return lambda l, i: jnp.where(l == li, i, jnp.where(l < li, 0, steps - 1))

    blk = pl.BlockSpec((None, tr, cols), lambda l, i: (l, i, 0))
    own_specs = [pl.BlockSpec((tr, cols), lambda l, i, f=held(li): (f(l, i), 0)) for li in range(nl)]
    land_specs = [pl.BlockSpec((N_DEV - 1, tr, cols), lambda l, i, f=held(li): (0, f(l, i), 0)) for li in range(nl)]
    return pl.pallas_call(
        body, name=name, grid=(nl, steps),
        in_specs=[blk, blk, blk] + own_specs + land_specs, out_specs=[blk] * 4,
        out_shape=[jax.ShapeDtypeStruct(w.shape, F32)] * 4,
    )(w, m, v, *owns, *lands)


_WEIGHTS = (
    ("meta_tokens", (16, 1024), 1), ("mix_norm_even", (1, 1024), None), ("w_in", (1, 1024, 2568), 2),
    ("b_f", (1, 8), None), ("conv_w", (1, 31, 512), 2), ("conv_b", (1, 512), None), ("ln_g", (1, 512), None),
    ("ln_b", (1, 512), None), ("w_out", (1, 1024, 1024), 1), ("mix_norm_odd", (1, 1024), 1),
    ("pool_w", (1, 4, 256, 256), 2), ("pool_b", (1, 4, 256), 2), ("pool_scale", (1, 1024), 1),
    ("ffn_norm", (2, 1024), None), ("w_up", (2, 1024, 5632), 2), ("ffn_conv_w", (2, 3, 5632), 2),
    ("ffn_conv_b", (2, 5632), None), ("w_down", (2, 2816, 1024), 1), ("final_norm", (1024,), None),
)
_MATMUL_WEIGHTS = ("w_in", "w_out", "pool_w", "w_up", "w_down")
_ADAM_ROWS = dict(w_in=256, w_out=128, pool_w=128, w_up=352, w_down=352)


def _shard_shape(shape, axis):
    return shape[:axis] + (shape[axis] // N_DEV,) + shape[axis + 1:]


def _size(shape):
    n = 1
    for s in shape:
        n *= s
    return n


def _pack(parts, dtype, lead=(), align=16):
    flat = jnp.concatenate([p.reshape(lead + (-1,)).astype(dtype) for p in parts], axis=-1)
    n = flat.shape[-1]
    rows = _round_up(-(-n // FLAT_W), align)
    flat = jnp.pad(flat, [(0, 0)] * len(lead) + [(0, rows * FLAT_W - n)])
    return flat.reshape(lead + (rows, FLAT_W))


def _unpack(buf, shapes, lead=()):
    flat = buf.reshape(lead + (-1,))
    out, off = [], 0
    for shp in shapes:
        n = _size(shp)
        out.append(flat[..., off:off + n].reshape(lead + shp))
        off += n
    return out


def _gathered_to_full(stack, shape, axis):
    return jnp.moveaxis(stack, 0, axis).reshape(shape)


def _full_to_slabs(full, shape, axis):
    split = shape[:axis] + (N_DEV, shape[axis] // N_DEV) + shape[axis + 1:]
    return jnp.moveaxis(full.reshape(split), axis, 0)


def kernel(x, meta_tokens, mix_norm_even, w_in, b_f, conv_w, conv_b, ln_g, ln_b, w_out, mix_norm_odd, pool_w, pool_b, pool_scale, ffn_norm, w_up, ffn_conv_w, ffn_conv_b, w_down, final_norm, loss_target, m_meta_tokens, m_mix_norm_even, m_w_in, m_b_f, m_conv_w, m_conv_b, m_ln_g, m_ln_b, m_w_out, m_mix_norm_odd, m_pool_w, m_pool_b, m_pool_scale, m_ffn_norm, m_w_up, m_ffn_conv_w, m_ffn_conv_b, m_w_down, m_final_norm, v_meta_tokens, v_mix_norm_even, v_w_in, v_b_f, v_conv_w, v_conv_b, v_ln_g, v_ln_b, v_w_out, v_mix_norm_odd, v_pool_w, v_pool_b, v_pool_scale, v_ffn_norm, v_w_up, v_ffn_conv_w, v_ffn_conv_b, v_w_down, v_final_norm):
    names = [n for n, _, _ in _WEIGHTS]
    w_loc = dict(zip(names, (meta_tokens, mix_norm_even, w_in, b_f, conv_w, conv_b, ln_g, ln_b, w_out, mix_norm_odd,
                             pool_w, pool_b, pool_scale, ffn_norm, w_up, ffn_conv_w, ffn_conv_b, w_down, final_norm)))
    m_loc = dict(zip(names, (m_meta_tokens, m_mix_norm_even, m_w_in, m_b_f, m_conv_w, m_conv_b, m_ln_g, m_ln_b,
                             m_w_out, m_mix_norm_odd, m_pool_w, m_pool_b, m_pool_scale, m_ffn_norm, m_w_up,
                             m_ffn_conv_w, m_ffn_conv_b, m_w_down, m_final_norm)))
    v_loc = dict(zip(names, (v_meta_tokens, v_mix_norm_even, v_w_in, v_b_f, v_conv_w, v_conv_b, v_ln_g, v_ln_b,
                             v_w_out, v_mix_norm_odd, v_pool_w, v_pool_b, v_pool_scale, v_ffn_norm, v_w_up,
                             v_ffn_conv_w, v_ffn_conv_b, v_w_down, v_final_norm)))
    replicated = [(n, s) for n, s, a in _WEIGHTS if a is None]
    little = [(n, s, a) for n, s, a in _WEIGHTS if a is not None and n not in _MATMUL_WEIGHTS]
    little_shards = [_shard_shape(s, a) for _, s, a in little]
    out_rows, down_rows, pool_rows = D_MODEL // N_DEV, D_FF // N_DEV, POOL_GROUP // N_DEV
    n_groups = len(POOL_WINDOWS)

    little_pack = _pack([w_loc[n] for n, _, _ in little], F32)
    g_win, g_wout, g_poolw, g_little = _all_gather(
        [w_in[0].astype(BF16), w_out[0].astype(BF16), pool_w[0].astype(BF16), little_pack],
        [(N_DEV, D_MODEL, _IN_SHARD), (D_MODEL, D_MODEL), (n_groups, POOL_GROUP, POOL_GROUP),
         (N_DEV,) + little_pack.shape],
        [_by_owner, _row_block(out_rows), _row_block(pool_rows, axis=1), _by_owner],
        name="gather_weights")
    me = _slot(lax.axis_index("x"), lax.axis_index("y"), lax.axis_index("c"))
    up_t = lambda a: jnp.transpose(a, (0, 2, 1))
    w_loc["w_up"], m_loc["w_up"], v_loc["w_up"] = up_t(w_up), up_t(m_w_up), up_t(v_w_up)
    w_up_b, w_down_b = w_loc["w_up"].astype(BF16), w_down.astype(BF16)
    whole = lambda ref, slot: ref
    ffn_lands = [jax.ShapeDtypeStruct((2 * D_FF, D_MODEL), BF16), jax.ShapeDtypeStruct((D_FF, D_MODEL), BF16)]
    ffn_gathers, behind = [], g_little
    for l in range(DEPTH):
        handle, behind = _send_start([w_up_b[l], w_down_b[l]], ffn_lands,
                                     [(0, whole, 0, _row_block(_UP_SHARD)), (1, whole, 1, _row_block(down_rows))],
                                     name=f"gather_ffn{l}_start", after=behind)
        ffn_gathers.append(handle)

    def ffn_weights(l, after):
        return _send_wait(ffn_gathers[l], after, name=f"gather_ffn{l}_wait")[1]

    w_in_p = _assemble_w_in(g_win, name="assemble_w_in")
    full = {n: _gathered_to_full(st, s, a)
            for (n, s, a), st in zip(little, _unpack(g_little, little_shards, lead=(N_DEV,)))}
    f0 = _QKV + 2 * CONV_CH
    wt = dict(
        meta=full["meta_tokens"], g_even=mix_norm_even, w_in_p=w_in_p, wf_t=w_in_p[:, f0:f0 + FOX_HEADS].T,
        b_f=b_f.reshape(FOX_HEADS, 1), conv_w=full["conv_w"][0], conv_b=conv_b, ln_g=ln_g, ln_b=ln_b, w_out=g_wout,
        g_odd=full["mix_norm_odd"], pool_w=g_poolw, pool_b=full["pool_b"].reshape(1, D_MODEL),
        pool_scale=full["pool_scale"], ffn_norm=ffn_norm, ffn_weights=ffn_weights, ffn_started=behind,
        fcw3=full["ffn_conv_w"].reshape(DEPTH, FFN_CONV_WIDTH, 2, D_FF).transpose(0, 2, 1, 3),
        fcb3=ffn_conv_b.reshape(DEPTH, 2, 1, D_FF), g_final=final_norm.reshape(1, D_MODEL))

    loss_part, grad_x, g = _local_step(x[0], loss_target[0], wt)

    small = _pack([loss_part[:, 0:1]] + [g[n] for n, _ in replicated] + [g["meta_tokens"]], F32, align=8)
    send_small, token_small = _send_start([small], [jax.ShapeDtypeStruct((N_DEV,) + small.shape, F32)],
                                          [(0, whole, 0, _by_owner)], name="send_small")

    grads, delta, new_m, new_v = {}, {}, {}, {}
    send0, send1, send_rest, send_in = g["sends"]
    ffn_sent = [_send_wait(send, token_small, name=f"wait_ffn{l}") for l, send in enumerate((send0, send1))]
    own_up = [lax.dynamic_index_in_dim(srcs[0], me, 0, keepdims=False) for srcs, _ in ffn_sent]
    own_down = [lax.dynamic_slice_in_dim(srcs[1], me * down_rows, down_rows, 0) for srcs, _ in ffn_sent]
    for n, owns, idx in (("w_up", own_up, 0), ("w_down", own_down, 1)):
        grads[n], delta[n], new_m[n], new_v[n] = _adamw_layers(
            w_loc[n], owns, [lands[idx] for _, lands in ffn_sent], m_loc[n], v_loc[n], _ADAM_ROWS[n],
            name=f"adamw_{n}")
    for d in (grads, delta, new_m, new_v):
        d["w_up"] = up_t(d["w_up"])
    (d_out, d_pool, little_slabs), (land_out, land_pool, land_little) = _send_wait(
        send_rest, delta["w_down"], name="wait_rest")
    (d_in,), (land_in,) = _send_wait(send_in, land_out, name="wait_w_in")
    pool_2d = (n_groups * pool_rows, POOL_GROUP)
    own_pool = lax.dynamic_slice_in_dim(d_pool, me * pool_rows, pool_rows, 1)
    for n, own, land, shp in (
            ("w_in", lax.dynamic_index_in_dim(d_in, me, 0, keepdims=False), land_in, w_in.shape),
            ("w_out", lax.dynamic_slice_in_dim(d_out, me * out_rows, out_rows, 0), land_out, w_out.shape),
            ("pool_w", own_pool.reshape(pool_2d), land_pool.reshape((N_DEV - 1,) + pool_2d), (1,) + pool_2d)):
        outs = _adamw_layers(w_loc[n].reshape(shp), [own], [land], m_loc[n].reshape(shp), v_loc[n].reshape(shp),
                             _ADAM_ROWS[n], name=f"adamw_{n}")
        grads[n], delta[n], new_m[n], new_v[n] = (o.reshape(w_loc[n].shape) for o in outs)
    own_little = lax.dynamic_index_in_dim(little_slabs, me, 0, keepdims=False)
    g_little = _unpack(_sum_slabs(land_little, own=own_little, name="sum_little"),
                       [_shard_shape(s, a) for _, s, a in _LITTLE])
    grads.update({n: gl for (n, _, _), gl in zip(_LITTLE, g_little)})
    _, (everyone,) = _send_wait(send_small, delta["w_in"], name="wait_small")
    summed = _unpack(_sum_slabs(everyone, name="sum_small"),
                     [(1, 1)] + [s for _, s in replicated] + [(N_META, D_MODEL)])
    loss = summed[0].reshape(())
    grads.update({n: gr for (n, _), gr in zip(replicated, summed[1:-1])})
    grads["meta_tokens"] = lax.dynamic_slice_in_dim(summed[-1], me * out_rows, out_rows, 1)
    two_d = lambda a: a.reshape(_size(a.shape[:-1]), a.shape[-1])
    rest = [n for n in names if n not in _MATMUL_WEIGHTS]
    outs = _adamw_many([two_d(w_loc[n]) for n in rest], [two_d(grads[n]) for n in rest],
                       [two_d(m_loc[n]) for n in rest], [two_d(v_loc[n]) for n in rest], name="adamw_rest")
    for i, n in enumerate(rest):
        delta[n], new_m[n], new_v[n] = (o.reshape(w_loc[n].shape) for o in outs[3 * i:3 * i + 3])
    return (loss, grad_x[None], *[grads[n] for n in names], *[delta[n] for n in names],
            *[new_m[n] for n in names], *[new_v[n] for n in names])
```

```python
import functools

import jax
import jax.numpy as jnp
from jax import lax
from jax.experimental import pallas as pl
from jax.experimental.pallas import tpu as pltpu

F32 = jnp.float32
BF16 = jnp.bfloat16

N_DEV = 8
DEPTH = 2
D_MODEL = 1024
N_META = 16
FOX_HEADS = 8
FOX_HEAD_DIM = 64
FOX_WIDTH = 512
CONV_CH = 512
CONV_WIDTH = 31
POOL_WINDOWS = (2, 4, 8, 16)
POOL_GROUP = 256
D_FF = 2816
FFN_CONV_WIDTH = 3
RMS_EPS = 1e-6
LN_EPS = 1e-5
ADAM_LR = 0.001
ADAM_B1 = 0.9
ADAM_B2 = 0.999
ADAM_EPS = 1e-08
ADAM_WD = 0.01
ADAM_STEP = 10

CHUNK = 128
HALO = 32
NEG_BIG = -1e30
FLAT_W = 1024


def _round_up(n, m):
    return (n + m - 1) // m * m


def _sigmoid(x):
    return 1.0 / (1.0 + jnp.exp(-x))


def _fold8(p):
    acc = p[0:8, :]
    for r in range(1, p.shape[0] // 8):
        acc = acc + p[8 * r:8 * r + 8, :]
    return acc


def _mm(a, b, *, name, tb=False, tm=None, tn=None, tk=None, out_dtype=F32, res=None,
        a_map=None, b_map=None, o_map=None, out_shape=None, dims=None, after=None):
    if dims is None:
        m, k = a.shape
        n = b.shape[-2] if tb else b.shape[-1]
    else:
        m, n, k = dims
    tm, tn, tk = tm or m, tn or n, tk or k
    assert m % tm == 0 and n % tn == 0 and k % tk == 0, (name, m, n, k, tm, tn, tk)
    nk = k // tk
    a_map = a_map or (lambda i, j, kk: (i, kk))
    b_map = b_map or ((lambda i, j, kk: (j, kk)) if tb else (lambda i, j, kk: (kk, j)))
    o_map = o_map or (lambda i, j, kk: (i, j))
    out_shape = out_shape or (m, n)
    contract = (((1,), (1,)), ((), ())) if tb else (((1,), (0,)), ((), ()))
    has_res = res is not None

    def body(*refs):
        a_ref, b_ref = refs[0], refs[1]
        res_ref = refs[2] if has_res else None
        o_ref = refs[2 + has_res + (after is not None)]
        p = lax.dot_general(a_ref[...], b_ref[...], contract, preferred_element_type=F32)
        if nk == 1:
            if has_res:
                p = p + res_ref[...]
            o_ref[...] = p.astype(o_ref.dtype)
        else:
            acc_ref = refs[-1]
            kk = pl.program_id(2)

            @pl.when(kk == 0)
            def _():
                acc_ref[...] = p

            @pl.when(kk > 0)
            def _():
                acc_ref[...] += p

            @pl.when(kk == nk - 1)
            def _():
                r = acc_ref[...]
                if has_res:
                    r = r + res_ref[...]
                o_ref[...] = r.astype(o_ref.dtype)

    in_specs = [pl.BlockSpec((tm, tk), a_map), pl.BlockSpec((tn, tk) if tb else (tk, tn), b_map)]
    operands = [a, b]
    if has_res:
        in_specs.append(pl.BlockSpec((tm, tn), o_map))
        operands.append(res)
    if after is not None:
        in_specs.append(pl.BlockSpec(memory_space=pl.ANY))
        operands.append(after)
    return pl.pallas_call(
        body, name=name, grid=(m // tm, n // tn, nk),
        in_specs=in_specs, out_specs=pl.BlockSpec((tm, tn), o_map),
        out_shape=jax.ShapeDtypeStruct(out_shape, out_dtype),
        scratch_shapes=[pltpu.VMEM((tm, tn), F32)] if nk > 1 else [],
    )(*operands)


def _transpose(x, *, name, out_dtype, cols=None, after=None):
    r, c = x.shape
    cols = cols or c
    assert r % CHUNK == 0
    behind = [] if after is None else [after]

    def body(x_ref, *rest):
        o_ref = rest[-1]
        o_ref[...] = x_ref[...].astype(F32).T.astype(o_ref.dtype)

    return pl.pallas_call(
        body, name=name, grid=(r // CHUNK,),
        in_specs=[pl.BlockSpec((CHUNK, cols), lambda i: (i, 0))] + [pl.BlockSpec(memory_space=pl.ANY)] * len(behind),
        out_specs=pl.BlockSpec((cols, CHUNK), lambda i: (0, i)),
        out_shape=jax.ShapeDtypeStruct((cols, r), out_dtype),
    )(x, *behind)


def _rms_fwd(x, g, *, name, out_dtype, after=None, transposed=False):
    lp, dm = x.shape
    tr = CHUNK if transposed else lp // 4
    behind = [] if after is None else [after]

    def body(x_ref, g_ref, *rest):
        xv = x_ref[...]
        r = lax.rsqrt(jnp.mean(xv * xv, axis=-1, keepdims=True) + RMS_EPS)
        y = xv * r * g_ref[...]
        if transposed:
            rest[-2][...] = y.astype(out_dtype)
            rest[-1][...] = y.T.astype(out_dtype)
        else:
            rest[-1][...] = y.astype(out_dtype)

    row = pl.BlockSpec((tr, dm), lambda i: (i, 0))
    out_specs, out_shape = row, jax.ShapeDtypeStruct((lp, dm), out_dtype)
    if transposed:
        out_specs = [row, pl.BlockSpec((dm, tr), lambda i: (0, i))]
        out_shape = [out_shape, jax.ShapeDtypeStruct((dm, lp), out_dtype)]
    return pl.pallas_call(
        body, name=name, grid=(lp // tr,),
        in_specs=[row, pl.BlockSpec((1, dm), lambda i: (0, 0))] + [pl.BlockSpec(memory_space=pl.ANY)] * len(behind),
        out_specs=out_specs, out_shape=out_shape,
    )(x, g, *behind)


def _rms_bwd_rows(x_ref, g_ref, dnv, dres_ref, dh_ref, dhb_ref, dg_ref, first):
    xv = x_ref[...]
    r = lax.rsqrt(jnp.mean(xv * xv, axis=-1, keepdims=True) + RMS_EPS)
    xhat = xv * r

    @pl.when(first)
    def _():
        dg_ref[...] = jnp.zeros_like(dg_ref)

    dg_ref[...] += jnp.sum(dnv * xhat, axis=0, keepdims=True)
    dxhat = dnv * g_ref[...]
    dx = r * (dxhat - xhat * jnp.mean(dxhat * xhat, axis=-1, keepdims=True))
    dh = dres_ref[...] + dx
    dh_ref[...] = dh
    dhb_ref[...] = dh.astype(BF16)


def _mm_rms_bwd(a, b, x, g, dres, *, name, tk, tb=False, a_map=None, after=None):
    lp, dm = x.shape
    tm = lp // 4
    nk = (b.shape[1] if tb else b.shape[0]) // tk
    a_map = a_map or (lambda i, kk: (i, kk))
    contract = (((1,), (1,)), ((), ())) if tb else (((1,), (0,)), ((), ()))
    behind = [] if after is None else [after]

    def body(a_ref, b_ref, x_ref, g_ref, dres_ref, *rest):
        dh_ref, dhb_ref, dg_ref, acc_ref = rest[len(behind):]
        i, kk = pl.program_id(0), pl.program_id(1)
        p = lax.dot_general(a_ref[...], b_ref[...], contract, preferred_element_type=F32)

        @pl.when(kk == 0)
        def _():
            acc_ref[...] = p

        @pl.when(kk > 0)
        def _():
            acc_ref[...] += p

        @pl.when(kk == nk - 1)
        def _():
            _rms_bwd_rows(x_ref, g_ref, acc_ref[...], dres_ref, dh_ref, dhb_ref, dg_ref, i == 0)

    row = pl.BlockSpec((tm, dm), lambda i, kk: (i, 0))
    vec = pl.BlockSpec((1, dm), lambda i, kk: (0, 0))
    b_spec = pl.BlockSpec((dm, tk), lambda i, kk: (0, kk)) if tb else pl.BlockSpec((tk, dm), lambda i, kk: (kk, 0))
    return pl.pallas_call(
        body, name=name, grid=(lp // tm, nk),
        in_specs=[pl.BlockSpec((tm, tk), a_map), b_spec, row, vec, row] + [pl.BlockSpec(memory_space=pl.ANY)] * len(behind),
        out_specs=[row, row, vec],
        out_shape=[jax.ShapeDtypeStruct((lp, dm), F32), jax.ShapeDtypeStruct((lp, dm), BF16),
                   jax.ShapeDtypeStruct((1, dm), F32)],
        scratch_shapes=[pltpu.VMEM((tm, dm), F32)],
    )(a, b, x, g, dres, *behind)


def _rms_bwd(x, g, dn, dres, *, name):
    lp, dm = x.shape
    tr = lp // 4

    def body(x_ref, g_ref, dn_ref, dres_ref, dh_ref, dhb_ref, dg_ref):
        _rms_bwd_rows(x_ref, g_ref, dn_ref[...], dres_ref, dh_ref, dhb_ref, dg_ref, pl.program_id(0) == 0)

    row = pl.BlockSpec((tr, dm), lambda i: (i, 0))
    vec = pl.BlockSpec((1, dm), lambda i: (0, 0))
    return pl.pallas_call(
        body, name=name, grid=(lp // tr,),
        in_specs=[row, vec, row, row], out_specs=[row, row, vec],
        out_shape=[jax.ShapeDtypeStruct((lp, dm), F32), jax.ShapeDtypeStruct((lp, dm), BF16),
                   jax.ShapeDtypeStruct((1, dm), F32)],
    )(x, g, dn, dres)


def _loss_head(h, g, tgt, n_real, *, name):
    lp, dm = h.shape
    tr = lp // 4

    def body(x_ref, g_ref, t_ref, loss_ref, dh_ref, dhb_ref, dg_ref):
        i = pl.program_id(0)
        xv = x_ref[...]
        r = lax.rsqrt(jnp.mean(xv * xv, axis=-1, keepdims=True) + RMS_EPS)
        xhat = xv * r
        gv = g_ref[...]
        y = xhat * gv
        t = i * tr + lax.broadcasted_iota(jnp.int32, (tr, 1), 0)
        valid = (t >= N_META) & (t < n_real)
        diff = jnp.where(valid, y - t_ref[...], 0.0)

        @pl.when(i == 0)
        def _():
            loss_ref[...] = jnp.zeros_like(loss_ref)
            dg_ref[...] = jnp.zeros_like(dg_ref)

        row_sq = jnp.sum(diff * diff, axis=-1, keepdims=True) * (1.0 / dm)
        part = 0.5 * jnp.sum(row_sq, axis=0, keepdims=True)
        loss_ref[...] += jnp.broadcast_to(part, loss_ref.shape)
        dy = diff * (1.0 / dm)
        dg_ref[...] += jnp.sum(dy * xhat, axis=0, keepdims=True)
        dxhat = dy * gv
        dx = r * (dxhat - xhat * jnp.mean(dxhat * xhat, axis=-1, keepdims=True))
        dh_ref[...] = dx
        dhb_ref[...] = dx.astype(BF16)

    row = pl.BlockSpec((tr, dm), lambda i: (i, 0))
    vec = pl.BlockSpec((1, dm), lambda i: (0, 0))
    return pl.pallas_call(
        body, name=name, grid=(lp // tr,),
        in_specs=[row, vec, row],
        out_specs=[pl.BlockSpec((1, 128), lambda i: (0, 0)), row, row, vec],
        out_shape=[jax.ShapeDtypeStruct((1, 128), F32), jax.ShapeDtypeStruct((lp, dm), F32),
                   jax.ShapeDtypeStruct((lp, dm), BF16), jax.ShapeDtypeStruct((1, dm), F32)],
    )(h, g, tgt)


def _tri(upper):
    r = lax.broadcasted_iota(jnp.int32, (CHUNK, CHUNK), 0)
    c = lax.broadcasted_iota(jnp.int32, (CHUNK, CHUNK), 1)
    return jnp.where(r <= c if upper else r >= c, 1.0, 0.0).astype(F32)


def _fox_prep(f_t, b_f, *, name):
    nh, lp = f_t.shape
    nch = lp // CHUNK

    def body(f_ref, b_ref, c_ref):
        tri = _tri(True)
        carry = jnp.zeros((nh, 1), F32)
        for blk in range(nch):
            cols = slice(blk * CHUNK, (blk + 1) * CHUNK)
            z = f_ref[:, cols] + b_ref[...]
            logf = jnp.minimum(z, 0.0) - jnp.log(1.0 + jnp.exp(-jnp.abs(z)))
            cb = jnp.dot(logf, tri, preferred_element_type=F32, precision=lax.Precision.HIGHEST)
            c_ref[:, cols] = cb + carry
            carry = carry + jnp.sum(logf, axis=1, keepdims=True)

    return pl.pallas_call(
        body, name=name, out_shape=jax.ShapeDtypeStruct((nh, lp), F32),
    )(f_t, b_f)


def _fox_bwd(dc, f_t, b_f, *, name):
    nh, lp = f_t.shape
    nch = lp // CHUNK

    def body(dc_ref, f_ref, b_ref, df_ref, db_ref):
        tri = _tri(False)
        carry = jnp.zeros((nh, 1), F32)
        db = jnp.zeros((nh, 1), F32)
        df_ref[...] = jnp.zeros_like(df_ref)
        for blk in reversed(range(nch)):
            cols = slice(blk * CHUNK, (blk + 1) * CHUNK)
            dcb = dc_ref[:, cols]
            dlogf = jnp.dot(dcb, tri, preferred_element_type=F32, precision=lax.Precision.HIGHEST) + carry
            carry = carry + jnp.sum(dcb, axis=1, keepdims=True)
            z = f_ref[:, cols] + b_ref[...]
            dz = dlogf * _sigmoid(-z)
            df_ref[0:nh, cols] = dz
            db = db + jnp.sum(dz, axis=1, keepdims=True)
        db_ref[...] = db

    return pl.pallas_call(
        body, name=name,
        out_shape=[jax.ShapeDtypeStruct((128, lp), F32), jax.ShapeDtypeStruct((nh, 1), F32)],
    )(dc, f_t, b_f)


def _attn_blocks(lp):
    tq = lp // 4
    return tq, [(i * tq, min(lp, _round_up((i + 1) * tq, CHUNK))) for i in range(4)]


ATTN_SCALE = FOX_HEAD_DIM ** -0.5


def _attn_probs(q2s, k_h, c_row, row0, n):
    tq = q2s.shape[0]
    lo = row0 // CHUNK * CHUNK
    logits = []
    for c0, c1 in ([(0, lo)] if lo else []) + [(lo, n)]:
        s = lax.dot_general(q2s, k_h[c0:c1], (((1,), (1,)), ((), ())), preferred_element_type=F32) - c_row[:, c0:c1]
        if c1 > row0:
            t = row0 + lax.broadcasted_iota(jnp.int32, (tq, c1 - c0), 0)
            sidx = c0 + lax.broadcasted_iota(jnp.int32, (tq, c1 - c0), 1)
            s = jnp.where(sidx <= t, s, NEG_BIG)
        logits.append((s, c0, c1))
    m = functools.reduce(jnp.maximum, [jnp.max(s, axis=1, keepdims=True) for s, _, _ in logits])
    ps = [(jnp.exp(s - m), c0, c1) for s, c0, c1 in logits]
    inv = 1.0 / sum(jnp.sum(p, axis=1, keepdims=True) for p, _, _ in ps)
    return [(p * inv, c0, c1) for p, c0, c1 in ps]


def _attn_fwd(qkv, c3, *, name):
    lp = qkv.shape[0]
    tq, blocks = _attn_blocks(lp)

    def body(q_ref, k_ref, v_ref, c_ref, o_ref):
        lane = lax.broadcasted_iota(jnp.int32, (1, 128), 1)
        zero = jnp.zeros((), BF16)
        for i, (row0, n) in enumerate(blocks):
            q2s = q_ref[row0:row0 + tq, :] * ATTN_SCALE
            acc = jnp.zeros((tq, 128), F32)
            for hd in range(2):
                sel = (lane < 64) if hd == 0 else (lane >= 64)
                k_h = jnp.where(sel, k_ref[0:n, :], zero)
                v_h = jnp.where(sel, v_ref[0:n, :], zero)
                for p, c0, c1 in _attn_probs(q2s, k_h, c_ref[hd:hd + 1, 0:n], row0, n):
                    acc = acc + jnp.dot(p.astype(BF16), v_h[c0:c1], preferred_element_type=F32)
            o_ref[row0:row0 + tq, :] = acc.astype(BF16)

    blk = lambda off: pl.BlockSpec((lp, 128), lambda p: (0, off + p))
    return pl.pallas_call(
        body, name=name, grid=(4,),
        in_specs=[blk(0), blk(4), blk(8), pl.BlockSpec((None, 2, lp), lambda p: (p, 0, 0))],
        out_specs=pl.BlockSpec((lp, 128), lambda p: (0, p)),
        out_shape=jax.ShapeDtypeStruct((lp, FOX_WIDTH), BF16),
    )(qkv, qkv, qkv, c3)


def _attn_bwd(qkv, q_t, dcat, do_t, c3, *, name):
    lp = qkv.shape[0]
    tq, blocks = _attn_blocks(lp)
    scale = FOX_HEAD_DIM ** -0.5

    def body(q_ref, k_ref, v_ref, qt_ref, do_ref, dot_ref, c_ref, dq_ref, dkt_ref, dvt_ref, dc_ref,
             dkt_acc, dvt_acc):
        lane = lax.broadcasted_iota(jnp.int32, (1, 128), 1)
        sub = lax.broadcasted_iota(jnp.int32, (128, 1), 0)
        zero = jnp.zeros((), BF16)
        dkt_acc[...] = jnp.zeros_like(dkt_acc)
        dvt_acc[...] = jnp.zeros_like(dvt_acc)
        dc_ref[...] = jnp.zeros_like(dc_ref)
        for i, (row0, n) in enumerate(blocks):
            rows = slice(row0, row0 + tq)
            q2s = q_ref[rows, :] * ATTN_SCALE
            do2 = do_ref[rows, :].astype(BF16)
            dq_acc = jnp.zeros((tq, 128), F32)
            for hd in range(2):
                sel = (lane < 64) if hd == 0 else (lane >= 64)
                sel_t = (sub < 64) if hd == 0 else (sub >= 64)
                k_h = jnp.where(sel, k_ref[0:n, :], zero)
                v_h = jnp.where(sel, v_ref[0:n, :], zero)
                qt_h = jnp.where(sel_t, qt_ref[:, rows], zero)
                dot_h = jnp.where(sel_t, dot_ref[:, rows], zero)
                segs = [(p, lax.dot_general(do2, v_h[c0:c1], (((1,), (1,)), ((), ())), preferred_element_type=F32),
                         c0, c1) for p, c0, c1 in _attn_probs(q2s, k_h, c_ref[hd:hd + 1, 0:n], row0, n)]
                delta = sum(jnp.sum(p * dp, axis=1, keepdims=True) for p, dp, _, _ in segs)
                for p, dp, c0, c1 in segs:
                    ds = p * (dp - delta)
                    dsb = ds.astype(BF16)
                    dq_acc = dq_acc + jnp.dot(dsb, k_h[c0:c1], preferred_element_type=F32)
                    dkt_acc[:, c0:c1] += jnp.dot(qt_h, dsb, preferred_element_type=F32)
                    dvt_acc[:, c0:c1] += jnp.dot(dot_h, p.astype(BF16), preferred_element_type=F32)
                    dc_ref[hd:hd + 1, c0:c1] -= jnp.sum(ds, axis=0, keepdims=True)
            dq_ref[rows, :] = (dq_acc * scale).astype(BF16)
        dkt_ref[...] = (dkt_acc[...] * scale).astype(BF16)
        dvt_ref[...] = dvt_acc[...].astype(BF16)

    blk = lambda off: pl.BlockSpec((lp, 128), lambda p: (0, off + p))
    blk_t = pl.BlockSpec((128, lp), lambda p: (p, 0))
    c_spec = pl.BlockSpec((None, 2, lp), lambda p: (p, 0, 0))
    return pl.pallas_call(
        body, name=name, grid=(4,),
        in_specs=[blk(0), blk(4), blk(8), blk_t, blk(0), blk_t, c_spec],
        out_specs=[blk(0), blk_t, blk_t, c_spec],
        out_shape=[jax.ShapeDtypeStruct((lp, FOX_WIDTH), BF16), jax.ShapeDtypeStruct((FOX_WIDTH, lp), BF16),
                   jax.ShapeDtypeStruct((FOX_WIDTH, lp), BF16), jax.ShapeDtypeStruct((4, 2, lp), F32)],
        scratch_shapes=[pltpu.VMEM((128, lp), F32), pltpu.VMEM((128, lp), F32)],
    )(qkv, qkv, qkv, q_t, dcat, do_t, c3)


def _ln_stats(x):
    mu = jnp.mean(x, axis=-1, keepdims=True)
    xc = x - mu
    var = jnp.mean(xc * xc, axis=-1, keepdims=True)
    rstd = lax.rsqrt(var + LN_EPS)
    return xc * rstd, rstd


def _conv_fwd(agf, conv_w, conv_b, ln_g, ln_b, *, name):
    lp = agf.shape[0]
    nch = lp // CHUNK
    c = CONV_CH

    def body(a_ref, g_ref, w_ref, b_ref, lg_ref, lb_ref, u0_ref, u1_ref, u3_ref, u0s):
        u0s[0:HALO, :] = jnp.zeros((HALO, c), F32)

        def glu(ci, _):
            rows = pl.ds(pl.multiple_of(ci * CHUNK, CHUNK), CHUNK)
            u0 = a_ref[rows, :] * _sigmoid(g_ref[rows, :])
            u0_ref[rows, :] = u0
            u0s[pl.ds(pl.multiple_of(ci * CHUNK + HALO, 8), CHUNK), :] = u0
            return 0

        lax.fori_loop(0, nch, glu, 0)

        def conv(ci, _):
            r0 = pl.multiple_of(ci * CHUNK, CHUNK)
            rows = pl.ds(r0, CHUNK)
            for lg in range(c // 128):
                lanes = slice(lg * 128, (lg + 1) * 128)
                win = u0s[pl.ds(r0, CHUNK + HALO), lanes]
                acc = jnp.broadcast_to(b_ref[:, lanes], (CHUNK, 128))
                for k in range(CONV_WIDTH):
                    s = CONV_WIDTH - 1 - k
                    sh = win if s == 0 else pltpu.roll(win, s, 0)
                    acc = acc + w_ref[k:k + 1, lanes] * sh[HALO:HALO + CHUNK, :]
                u1_ref[rows, lanes] = acc
            xhat, _ = _ln_stats(u1_ref[rows, :])
            y = xhat * lg_ref[...] + lb_ref[...]
            u3_ref[rows, :] = (y * _sigmoid(y)).astype(BF16)
            return 0

        lax.fori_loop(0, nch, conv, 0)

    full = lambda shape: pl.BlockSpec(shape, lambda i: (0, 0))
    return pl.pallas_call(
        body, name=name, grid=(1,),
        in_specs=[pl.BlockSpec((lp, c), lambda i: (0, 0)), pl.BlockSpec((lp, c), lambda i: (0, 1)),
                  full((CONV_WIDTH, c)), full((1, c)), full((1, c)), full((1, c))],
        out_specs=[full((lp, c)), full((lp, c)), full((lp, c))],
        out_shape=[jax.ShapeDtypeStruct((lp, c), F32), jax.ShapeDtypeStruct((lp, c), F32),
                   jax.ShapeDtypeStruct((lp, c), BF16)],
        scratch_shapes=[pltpu.VMEM((lp + HALO, c), F32)],
    )(agf, agf, conv_w, conv_b, ln_g, ln_b)


def _conv_bwd(dcat, u0, u1, agf, conv_w, ln_g, ln_b, *, name):
    lp = agf.shape[0]
    nch = lp // CHUNK
    c = CONV_CH
    wlen = CHUNK + HALO

    def body(du3_ref, u0_ref, u1_ref, a_ref, g_ref, w_ref, lg_ref, lb_ref,
             dag_ref, dw_ref, db_ref, dlg_ref, dlb_ref, du1s, dwacc, vacc):
        du1s[lp:lp + HALO, :] = jnp.zeros((HALO, c), F32)
        dwacc[...] = jnp.zeros_like(dwacc)
        vacc[...] = jnp.zeros_like(vacc)

        def ln_bwd(ci, _):
            r0 = pl.multiple_of(ci * CHUNK, CHUNK)
            rows = pl.ds(r0, CHUNK)
            xhat, rstd = _ln_stats(u1_ref[rows, :])
            y = xhat * lg_ref[...] + lb_ref[...]
            sg = _sigmoid(y)
            du2 = du3_ref[rows, :] * (sg * (1.0 + y * (1.0 - sg)))
            vacc[0:8, :] += _fold8(du2 * xhat)
            vacc[8:16, :] += _fold8(du2)
            dxhat = du2 * lg_ref[...]
            du1 = rstd * (dxhat - jnp.mean(dxhat, axis=-1, keepdims=True)
                          - xhat * jnp.mean(dxhat * xhat, axis=-1, keepdims=True))
            vacc[16:24, :] += _fold8(du1)
            du1s[rows, :] = du1
            return 0

        lax.fori_loop(0, nch, ln_bwd, 0)

        def conv_bwd(ci, _):
            r0 = pl.multiple_of(ci * CHUNK, CHUNK)
            rows = pl.ds(r0, CHUNK)
            for lg in range(c // 128):
                lanes = slice(lg * 128, (lg + 1) * 128)
                dwin = du1s[pl.ds(r0, wlen), lanes]
                u0 = u0_ref[rows, lanes]
                acc = jnp.zeros((CHUNK, 128), F32)
                for k in range(CONV_WIDTH):
                    s = CONV_WIDTH - 1 - k
                    d_s = (dwin if s == 0 else pltpu.roll(dwin, wlen - s, 0))[0:CHUNK, :]
                    acc = acc + w_ref[k:k + 1, lanes] * d_s
                    dwacc[8 * k:8 * k + 8, lanes] += _fold8(d_s * u0)
                sg = _sigmoid(g_ref[rows, lanes])
                a = a_ref[rows, lanes]
                dag_ref[rows, lanes] = (acc * sg).astype(BF16)
                dag_ref[rows, slice(c + lg * 128, c + (lg + 1) * 128)] = (acc * a * sg * (1.0 - sg)).astype(BF16)
            return 0

        lax.fori_loop(0, nch, conv_bwd, 0)
        for k in range(CONV_WIDTH):
            dw_ref[k:k + 1, :] = jnp.sum(dwacc[8 * k:8 * k + 8, :], axis=0, keepdims=True)
        dlg_ref[...] = jnp.sum(vacc[0:8, :], axis=0, keepdims=True)
        dlb_ref[...] = jnp.sum(vacc[8:16, :], axis=0, keepdims=True)
        db_ref[...] = jnp.sum(vacc[16:24, :], axis=0, keepdims=True)

    full = lambda shape: pl.BlockSpec(shape, lambda i: (0, 0))
    vec = jax.ShapeDtypeStruct((1, c), F32)
    return pl.pallas_call(
        body, name=name, grid=(1,),
        in_specs=[pl.BlockSpec((lp, c), lambda i: (0, 1)), full((lp, c)), full((lp, c)),
                  pl.BlockSpec((lp, c), lambda i: (0, 0)), pl.BlockSpec((lp, c), lambda i: (0, 1)),
                  full((CONV_WIDTH, c)), full((1, c)), full((1, c))],
        out_specs=[full((lp, 2 * c)), full((CONV_WIDTH, c)), full((1, c)), full((1, c)), full((1, c))],
        out_shape=[jax.ShapeDtypeStruct((lp, 2 * c), BF16), jax.ShapeDtypeStruct((CONV_WIDTH, c), F32), vec, vec, vec],
        scratch_shapes=[pltpu.VMEM((lp + HALO, c), F32), pltpu.VMEM((8 * CONV_WIDTH, c), F32),
                        pltpu.VMEM((24, c), F32)],
    )(dcat, u0, u1, agf, agf, conv_w, ln_g, ln_b)


FFN_TILE = 256
FFN_PAD = 8


def _ffn_conv(xs, w_ref, b_ref, half, r0):
    win = xs[half, pl.ds(r0, CHUNK + FFN_PAD), :]
    acc = jnp.broadcast_to(b_ref[half], (CHUNK, FFN_TILE))
    for k in range(FFN_CONV_WIDTH):
        s = FFN_CONV_WIDTH - 1 - k
        sh = win if s == 0 else pltpu.roll(win, s, 0)
        acc = acc + w_ref[half, k:k + 1, :] * sh[FFN_PAD:FFN_PAD + CHUNK, :]
    return acc


def _ffn_act_fwd(up3, w3, b3, *, name):
    _, lp, f = up3.shape
    nch = lp // CHUNK

    def body(up_ref, w_ref, b_ref, act_ref, act_t_ref, gv_ref, xs):
        for half in range(2):
            xs[half, 0:FFN_PAD, :] = jnp.zeros((FFN_PAD, FFN_TILE), F32)
            xs[half, FFN_PAD:FFN_PAD + lp, :] = up_ref[half].astype(F32)

        def chunk(ci, _):
            r0 = pl.multiple_of(ci * CHUNK, CHUNK)
            rows = pl.ds(r0, CHUNK)
            gate = _ffn_conv(xs, w_ref, b_ref, 0, r0)
            val = _ffn_conv(xs, w_ref, b_ref, 1, r0)
            gv_ref[0, rows, :] = gate.astype(BF16)
            gv_ref[1, rows, :] = val.astype(BF16)
            act = gate * _sigmoid(gate) * val
            act_ref[rows, :] = act.astype(BF16)
            act_t_ref[:, rows] = act.T.astype(BF16)
            return 0

        lax.fori_loop(0, nch, chunk, 0, unroll=True)

    halves = pl.BlockSpec((2, lp, FFN_TILE), lambda j: (0, 0, j))
    return pl.pallas_call(
        body, name=name, grid=(f // FFN_TILE,),
        in_specs=[halves, pl.BlockSpec((2, FFN_CONV_WIDTH, FFN_TILE), lambda j: (0, 0, j)),
                  pl.BlockSpec((2, 1, FFN_TILE), lambda j: (0, 0, j))],
        out_specs=[pl.BlockSpec((lp, FFN_TILE), lambda j: (0, j)), pl.BlockSpec((FFN_TILE, lp), lambda j: (j, 0)),
                   halves],
        out_shape=[jax.ShapeDtypeStruct((lp, f), BF16), jax.ShapeDtypeStruct((f, lp), BF16),
                   jax.ShapeDtypeStruct((2, lp, f), BF16)],
        scratch_shapes=[pltpu.VMEM((2, lp + FFN_PAD, FFN_TILE), F32)],
    )(up3, w3, b3)


def _ffn_act_bwd(up3, gv3, w3, dact, *, name):
    _, lp, f = up3.shape
    nch = lp // CHUNK
    wlen = CHUNK + FFN_PAD

    def body(up_ref, gv_ref, w_ref, dact_ref, dup_ref, dw_ref, db_ref, ds, wacc):
        for half in range(2):
            ds[half, lp:lp + FFN_PAD, :] = jnp.zeros((FFN_PAD, FFN_TILE), F32)
        wacc[...] = jnp.zeros_like(wacc)

        def act_bwd(ci, _):
            rows = pl.ds(pl.multiple_of(ci * CHUNK, CHUNK), CHUNK)
            gate, val = gv_ref[0, rows, :].astype(F32), gv_ref[1, rows, :].astype(F32)
            sg = _sigmoid(gate)
            da = dact_ref[rows, :].astype(F32)
            ds[0, rows, :] = da * val * (sg * (1.0 + gate * (1.0 - sg)))
            ds[1, rows, :] = da * (gate * sg)
            return 0

        lax.fori_loop(0, nch, act_bwd, 0, unroll=True)

        def conv_bwd(ci, _):
            r0 = pl.multiple_of(ci * CHUNK, CHUNK)
            rows = pl.ds(r0, CHUNK)
            for half in range(2):
                dwin = ds[half, pl.ds(r0, wlen), :]
                x = up_ref[half, rows, :].astype(F32)
                acc = jnp.zeros((CHUNK, FFN_TILE), F32)
                for k in range(FFN_CONV_WIDTH):
                    s = FFN_CONV_WIDTH - 1 - k
                    d_s = (dwin if s == 0 else pltpu.roll(dwin, wlen - s, 0))[0:CHUNK, :]
                    acc = acc + w_ref[half, k:k + 1, :] * d_s
                    wacc[half, 8 * k:8 * k + 8, :] += _fold8(d_s * x)
                wacc[half, 24:32, :] += _fold8(dwin[0:CHUNK, :])
                dup_ref[half, rows, :] = acc.astype(BF16)
            return 0

        lax.fori_loop(0, nch, conv_bwd, 0)
        for half in range(2):
            for k in range(FFN_CONV_WIDTH):
                dw_ref[half, k:k + 1, :] = jnp.sum(wacc[half, 8 * k:8 * k + 8, :], axis=0, keepdims=True)
            db_ref[half] = jnp.sum(wacc[half, 24:32, :], axis=0, keepdims=True)

    halves = pl.BlockSpec((2, lp, FFN_TILE), lambda j: (0, 0, j))
    taps = pl.BlockSpec((2, FFN_CONV_WIDTH, FFN_TILE), lambda j: (0, 0, j))
    bias = pl.BlockSpec((2, 1, FFN_TILE), lambda j: (0, 0, j))
    return pl.pallas_call(
        body, name=name, grid=(f // FFN_TILE,),
        in_specs=[halves, halves, taps, pl.BlockSpec((lp, FFN_TILE), lambda j: (0, j))],
        out_specs=[halves, taps, bias],
        out_shape=[jax.ShapeDtypeStruct((2, lp, f), BF16), jax.ShapeDtypeStruct((2, FFN_CONV_WIDTH, f), F32),
                   jax.ShapeDtypeStruct((2, 1, f), F32)],
        scratch_shapes=[pltpu.VMEM((2, lp + FFN_PAD, FFN_TILE), F32), pltpu.VMEM((2, 32, FFN_TILE), F32)],
    )(up3, gv3, w3, dact)


POOL_PAD = 16


def _inv_count(r0, w):
    t = r0 + lax.broadcasted_iota(jnp.int32, (CHUNK, 1), 0)
    return 1.0 / jnp.minimum(t + 1, w).astype(F32)


def _pool_fwd(n, pool_w, pool_b, pool_scale, h, *, name):
    lp, dm = n.shape
    nch = lp // CHUNK
    g = POOL_GROUP

    def body(n_ref, w_ref, b_ref, s_ref, h_ref, ho_ref, dt_ref, z_ref, xs, d_ref):
        gi = pl.program_id(0)
        xs[0:POOL_PAD, :] = jnp.zeros((POOL_PAD, g), F32)
        xs[POOL_PAD:POOL_PAD + lp, :] = n_ref[...]
        for idx, w in enumerate(POOL_WINDOWS):
            @pl.when(gi == idx)
            def _(w=w):
                def chunk(ci, _):
                    r0 = pl.multiple_of(ci * CHUNK, CHUNK)
                    win = xs[pl.ds(r0, CHUNK + POOL_PAD), :]
                    acc = win
                    for j in range(1, w):
                        acc = acc + pltpu.roll(win, j, 0)
                    x = win[POOL_PAD:POOL_PAD + CHUNK, :]
                    d = acc[POOL_PAD:POOL_PAD + CHUNK, :] * _inv_count(r0, w) - x
                    d_ref[pl.ds(r0, CHUNK), :] = d.astype(BF16)
                    dt_ref[:, pl.ds(r0, CHUNK)] = d.T.astype(BF16)
                    return 0

                lax.fori_loop(0, nch, chunk, 0, unroll=True)

        z = jnp.dot(d_ref[...], w_ref[...], preferred_element_type=F32) + b_ref[...]
        z_ref[...] = z
        ho_ref[...] = h_ref[...] + z * s_ref[...]

    col = pl.BlockSpec((lp, g), lambda i: (0, i))
    vec = pl.BlockSpec((1, g), lambda i: (0, i))
    return pl.pallas_call(
        body, name=name, grid=(len(POOL_WINDOWS),),
        in_specs=[col, pl.BlockSpec((None, g, g), lambda i: (i, 0, 0)), vec, vec, col],
        out_specs=[col, pl.BlockSpec((g, lp), lambda i: (i, 0)), col],
        out_shape=[jax.ShapeDtypeStruct((lp, dm), F32), jax.ShapeDtypeStruct((dm, lp), BF16),
                   jax.ShapeDtypeStruct((lp, dm), F32)],
        scratch_shapes=[pltpu.VMEM((lp + POOL_PAD, g), F32), pltpu.VMEM((lp, g), BF16)],
    )(n, pool_w, pool_b, pool_scale, h)


def _pool_bwd(dy, z, pool_w, pool_scale, *, name):
    lp, dm = dy.shape
    nch = lp // CHUNK
    g = POOL_GROUP
    wlen = CHUNK + POOL_PAD

    def body(dy_ref, z_ref, w_ref, s_ref, dn_ref, dz_ref, dsc_ref, db_ref, ys, dd):
        gi = pl.program_id(0)
        dyv = dy_ref[...]
        dsc_ref[...] = jnp.sum(dyv * z_ref[...], axis=0, keepdims=True)
        dz = dyv * s_ref[...]
        db_ref[...] = jnp.sum(dz, axis=0, keepdims=True)
        dzb = dz.astype(BF16)
        dz_ref[...] = dzb
        dd[...] = lax.dot_general(dzb, w_ref[...], (((1,), (1,)), ((), ())), preferred_element_type=F32)
        ys[lp:lp + POOL_PAD, :] = jnp.zeros((POOL_PAD, g), F32)
        for idx, w in enumerate(POOL_WINDOWS):
            @pl.when(gi == idx)
            def _(w=w):
                def scale(ci, _):
                    r0 = pl.multiple_of(ci * CHUNK, CHUNK)
                    ys[pl.ds(r0, CHUNK), :] = dd[pl.ds(r0, CHUNK), :] * _inv_count(r0, w)
                    return 0

                lax.fori_loop(0, nch, scale, 0, unroll=True)

                def chunk(ci, _):
                    r0 = pl.multiple_of(ci * CHUNK, CHUNK)
                    win = ys[pl.ds(r0, wlen), :]
                    acc = win
                    for j in range(1, w):
                        acc = acc + pltpu.roll(win, wlen - j, 0)
                    dn_ref[pl.ds(r0, CHUNK), :] = acc[0:CHUNK, :] - dd[pl.ds(r0, CHUNK), :]
                    return 0

                lax.fori_loop(0, nch, chunk, 0, unroll=True)

    col = pl.BlockSpec((lp, g), lambda i: (0, i))
    vec = pl.BlockSpec((1, g), lambda i: (0, i))
    return pl.pallas_call(
        body, name=name, grid=(len(POOL_WINDOWS),),
        in_specs=[col, col, pl.BlockSpec((None, g, g), lambda i: (i, 0, 0)), vec],
        out_specs=[col, col, vec, vec],
        out_shape=[jax.ShapeDtypeStruct((lp, dm), F32), jax.ShapeDtypeStruct((lp, dm), BF16),
                   jax.ShapeDtypeStruct((1, dm), F32), jax.ShapeDtypeStruct((1, dm), F32)],
        scratch_shapes=[pltpu.VMEM((lp + POOL_PAD, g), F32), pltpu.VMEM((lp, g), F32)],
    )(dy, z, pool_w, pool_scale)


def _ffn_fwd(h, g, w_up_t, w3, b3, w_down, tag):
    lp = h.shape[0]
    nj = D_FF // FFN_TILE
    n, n_t = _rms_fwd(h, g, name=f"rms_ffn{tag}", out_dtype=BF16, transposed=True)
    up2 = _mm(n, w_up_t, name=f"mm_up{tag}", tb=True, tn=FFN_TILE, dims=(lp, 2 * D_FF, D_MODEL),
              o_map=lambda i, j, k: (j // nj, j % nj), out_shape=(2 * lp, D_FF), out_dtype=BF16)
    up3 = up2.reshape(2, lp, D_FF)
    act, act_t, gv3 = _ffn_act_fwd(up3, w3, b3, name=f"ffn_act{tag}")
    h_out = _mm(act, w_down, name=f"mm_down{tag}", tn=256, res=h)
    return h_out, (n_t, up3, gv3, act_t)


def _ffn_bwd(dh, dhb, h, g, saved, w_up_t, w3, w_down, tag, after=None):
    lp = h.shape[0]
    n_t, up3, gv3, act_t = saved
    dw_down = _mm(act_t, dhb, name=f"mm_dwdown{tag}", tm=704, out_dtype=BF16)
    dact = _mm(dhb, w_down, name=f"mm_dact{tag}", tb=True, tn=256, out_dtype=BF16, after=after)
    dup3, dcw, dcb = _ffn_act_bwd(up3, gv3, w3, dact, name=f"ffn_act_bwd{tag}")
    dup2 = dup3.reshape(2 * lp, D_FF)
    dw_up = _mm_dw_up(n_t, dup2, name=f"mm_dwup{tag}")
    dh_in, dh_in_b, dg = _mm_rms_bwd(dup2, w_up_t, h, g, dh, name=f"mm_dnffn{tag}", tk=D_FF // 2,
                                     a_map=lambda i, kk: (4 * (kk // 2) + i, kk % 2))
    return dh_in, dh_in_b, (dg, dw_up, dcw, dcb, dw_down)


def _local_step(x, tgt, wt):
    seq = x.shape[0]
    n_real = N_META + seq
    lp = _round_up(n_real, CHUNK)
    pad = jnp.zeros((lp - n_real, D_MODEL), F32)
    h0 = jnp.concatenate([wt["meta"], x, pad], axis=0)
    tgt_p = jnp.concatenate([jnp.zeros((N_META, D_MODEL), F32), tgt, pad], axis=0)
    w_in_p = wt["w_in_p"]

    n0, n0_t = _rms_fwd(h0, wt["g_even"], name="rms_even", out_dtype=BF16, after=wt["ffn_started"], transposed=True)
    qkv = _mm(n0, w_in_p, name="mm_qkv", tn=512, dims=(lp, 3 * FOX_WIDTH, D_MODEL), out_dtype=BF16)
    ag = _mm(n0, w_in_p, name="mm_ag", tn=512, dims=(lp, 2 * CONV_CH, D_MODEL),
             b_map=lambda i, j, k: (0, 3 + j))
    f_t = _mm(wt["wf_t"], n0, name="mm_ft", tb=True)
    c_row = _fox_prep(f_t, wt["b_f"], name="fox_prep")
    c3 = c_row.reshape(4, 2, lp)
    o = _attn_fwd(qkv, c3, name="attn_fwd")
    u0, u1, u3 = _conv_fwd(ag, wt["conv_w"], wt["conv_b"], wt["ln_g"], wt["ln_b"], name="conv_fwd")
    cat = jnp.concatenate([o, u3], axis=1)
    h1 = _mm(cat, wt["w_out"], name="mm_out", tn=256, res=h0)
    w_up0, w_down0 = wt["ffn_weights"](0, h1)
    h2, saved0 = _ffn_fwd(h1, wt["ffn_norm"][0:1], w_up0, wt["fcw3"][0], wt["fcb3"][0], w_down0, 0)

    n2 = _rms_fwd(h2, wt["g_odd"], name="rms_odd", out_dtype=F32)
    h3, dpool_t, z = _pool_fwd(n2, wt["pool_w"], wt["pool_b"], wt["pool_scale"], h2, name="pool_fwd")
    w_up1, w_down1 = wt["ffn_weights"](1, h3)
    h4, saved1 = _ffn_fwd(h3, wt["ffn_norm"][1:2], w_up1, wt["fcw3"][1], wt["fcb3"][1], w_down1, 1)

    loss, dh4, dh4b, d_gfinal = _loss_head(h4, wt["g_final"], tgt_p, n_real, name="loss_head")

    dh3, dh3b, gf1 = _ffn_bwd(dh4, dh4b, h3, wt["ffn_norm"][1:2], saved1, w_up1, wt["fcw3"][1], w_down1, 1)
    send1, token1 = _send_ffn_grads(gf1[1], gf1[4], 1)
    dn2, dzb, d_pscale, d_pb = _pool_bwd(dh3, z, wt["pool_w"], wt["pool_scale"], name="pool_bwd")
    d_pw = _mm(dpool_t, dzb, name="mm_dpoolw", tm=POOL_GROUP, tn=POOL_GROUP, dims=(D_MODEL, POOL_GROUP, lp),
               b_map=lambda i, j, k: (0, i), o_map=lambda i, j, k: (i, 0), out_shape=(D_MODEL, POOL_GROUP),
               out_dtype=BF16)
    dh2, dh2b, d_godd = _rms_bwd(h2, wt["g_odd"], dn2, dh3, name="rms_bwd_odd")
    dh1, dh1b, gf0 = _ffn_bwd(dh2, dh2b, h1, wt["ffn_norm"][0:1], saved0, w_up0, wt["fcw3"][0], w_down0, 0,
                              after=token1)

    send0, token0 = _send_ffn_grads(gf0[1], gf0[4], 0)
    cat_t = _transpose(cat, name="t_cat", out_dtype=BF16)
    d_wout = _mm(cat_t, dh1b, name="mm_dwout", tm=512, out_dtype=BF16)
    dcat = _mm(dh1b, wt["w_out"], name="mm_dcat", tb=True, tn=256, after=token0)
    dag, d_convw, d_convb, d_lng, d_lnb = _conv_bwd(dcat, u0, u1, ag, wt["conv_w"], wt["ln_g"], wt["ln_b"],
                                                    name="conv_bwd")
    layers = lambda i: jnp.stack([gf0[i], gf1[i]])
    grads = dict(
        conv_w=d_convw[None], w_out=d_wout, mix_norm_odd=d_godd,
        pool_w=d_pw.reshape(len(POOL_WINDOWS), POOL_GROUP, POOL_GROUP),
        pool_b=d_pb.reshape(1, len(POOL_WINDOWS), POOL_GROUP), pool_scale=d_pscale, w_up=(gf0[1], gf1[1]),
        ffn_conv_w=layers(2).transpose(0, 2, 1, 3).reshape(DEPTH, FFN_CONV_WIDTH, 2 * D_FF), w_down=(gf0[4], gf1[4]))
    send_rest, token_rest, grads["little_slabs"] = _send_rest_grads(grads)
    q_t = _transpose(qkv, name="t_q", out_dtype=BF16, cols=FOX_WIDTH, after=token_rest)
    do_t = _transpose(dcat, name="t_do", out_dtype=BF16, cols=FOX_WIDTH)
    dq, dk_t, dv_t, dc3 = _attn_bwd(qkv, q_t, dcat, do_t, c3, name="attn_bwd")
    dk = _transpose(dk_t, name="t_dk", out_dtype=BF16)
    dv = _transpose(dv_t, name="t_dv", out_dtype=BF16)
    df_t, d_bf = _fox_bwd(dc3.reshape(FOX_HEADS, lp), f_t, wt["b_f"], name="fox_bwd")
    df = _transpose(df_t, name="t_df", out_dtype=BF16)
    dproj = jnp.concatenate([dq, dk, dv, dag, df], axis=1)
    grads["w_in"] = _mm_dw_in(n0_t, dproj, name="mm_dwin")
    send_in, token_in = _send_start(
        [grads["w_in"]], [jax.ShapeDtypeStruct((N_DEV - 1, D_MODEL, _IN_SHARD), BF16)], [(0, _by_owner, 0, None)],
        name="send_w_in")
    dh0, _, d_geven = _mm_rms_bwd(dproj, w_in_p, h0, wt["g_even"], dh1, name="mm_dn0", tb=True, tk=896,
                                  after=token_in)
    grads.update(
        meta_tokens=dh0[0:N_META], mix_norm_even=d_geven, b_f=d_bf.reshape(1, FOX_HEADS), conv_b=d_convb, ln_g=d_lng,
        ln_b=d_lnb, ffn_norm=jnp.concatenate([gf0[0], gf1[0]], axis=0),
        ffn_conv_b=layers(3).reshape(DEPTH, 2 * D_FF), final_norm=d_gfinal.reshape(D_MODEL),
        sends=(send0, send1, send_rest, send_in))
    return loss, dh0[N_META:n_real], grads


_LITTLE = (("conv_w", (1, 31, 512), 2), ("mix_norm_odd", (1, 1024), 1), ("pool_b", (1, 4, 256), 2),
           ("pool_scale", (1, 1024), 1), ("ffn_conv_w", (2, 3, 5632), 2))


def _send_rest_grads(g):
    out_rows, pool_rows, groups = D_MODEL // N_DEV, POOL_GROUP // N_DEV, len(POOL_WINDOWS)
    little_slabs = _pack([_full_to_slabs(g[n], s, a) for n, s, a in _LITTLE], F32, lead=(N_DEV,), align=8)
    land = lambda shape, dtype: jax.ShapeDtypeStruct((N_DEV - 1,) + shape, dtype)
    handle, token = _send_start(
        [g["w_out"], g["pool_w"], little_slabs],
        [land((out_rows, D_MODEL), BF16), land((groups, pool_rows, POOL_GROUP), BF16), land(little_slabs.shape[1:], F32)],
        [(0, _row_block(out_rows), 0, None), (1, _row_block(pool_rows, axis=1), 1, None), (2, _by_owner, 2, None)],
        name="send_rest")
    return handle, token, little_slabs


def _send_ffn_grads(dw_up, dw_down, tag):
    rows = D_FF // N_DEV
    lands = [jax.ShapeDtypeStruct((N_DEV - 1,) + dw_up.shape[1:], BF16),
             jax.ShapeDtypeStruct((N_DEV - 1, rows, D_MODEL), BF16)]
    return _send_start([dw_up, dw_down], lands, [(0, _by_owner, 0, None), (1, _row_block(rows), 1, None)],
                       name=f"send_ffn{tag}")


_QKV = 3 * FOX_WIDTH
_GLU0 = _QKV + FOX_HEADS
_IN_COLS = _GLU0 + 2 * CONV_CH
_F_PAD = 128


_IN_SHARD = _IN_COLS // N_DEV
_UP_SHARD = 2 * D_FF // N_DEV
_ROW_TILE = 256


def _assemble_w_in(st, *, name):
    tr = _ROW_TILE

    def body(s_ref, o_ref):
        full = jnp.concatenate([s_ref[i].astype(F32) for i in range(N_DEV)], axis=1)
        parts = [full[:, :_QKV], full[:, _GLU0:], full[:, _QKV:_GLU0], jnp.zeros((tr, _F_PAD - FOX_HEADS), F32)]
        o_ref[...] = jnp.concatenate(parts, axis=1).astype(BF16)

    return pl.pallas_call(
        body, name=name, grid=(D_MODEL // tr,),
        in_specs=[pl.BlockSpec((N_DEV, tr, _IN_SHARD), lambda i: (0, i, 0))],
        out_specs=pl.BlockSpec((tr, _QKV + 2 * CONV_CH + _F_PAD), lambda i: (i, 0)),
        out_shape=jax.ShapeDtypeStruct((D_MODEL, _QKV + 2 * CONV_CH + _F_PAD), BF16),
    )(st)


def _mm_dw_in(n_t, dproj, *, name):
    dm, lp = n_t.shape
    tr = _ROW_TILE
    ag0 = _QKV + 2 * CONV_CH

    def body(a_ref, b_ref, o_ref):
        r = jnp.dot(a_ref[...], b_ref[...], preferred_element_type=F32)
        full = jnp.concatenate([r[:, :_QKV], r[:, ag0:ag0 + FOX_HEADS], r[:, _QKV:ag0]], axis=1)
        for i in range(N_DEV):
            o_ref[i] = full[:, i * _IN_SHARD:(i + 1) * _IN_SHARD].astype(BF16)

    return pl.pallas_call(
        body, name=name, grid=(dm // tr,),
        in_specs=[pl.BlockSpec((tr, lp), lambda i: (i, 0)), pl.BlockSpec(dproj.shape, lambda i: (0, 0))],
        out_specs=pl.BlockSpec((N_DEV, tr, _IN_SHARD), lambda i: (0, i, 0)),
        out_shape=jax.ShapeDtypeStruct((N_DEV, dm, _IN_SHARD), BF16),
    )(n_t, dproj)


def _mm_dw_up(n_t, dup2, *, name):
    dm, lp = n_t.shape
    pairs_per_half = D_FF // (2 * _UP_SHARD)

    def body(a_ref, b_ref, o_ref):
        r_t = jnp.dot(a_ref[...], b_ref[...], preferred_element_type=F32).T
        o_ref[0] = r_t[:_UP_SHARD, :].astype(BF16)
        o_ref[1] = r_t[_UP_SHARD:, :].astype(BF16)

    return pl.pallas_call(
        body, name=name, grid=(N_DEV // 2,),
        in_specs=[pl.BlockSpec((dm, lp), lambda p: (0, 0)),
                  pl.BlockSpec((lp, 2 * _UP_SHARD), lambda p: (p // pairs_per_half, p % pairs_per_half))],
        out_specs=pl.BlockSpec((2, _UP_SHARD, dm), lambda p: (p, 0, 0)),
        out_shape=jax.ShapeDtypeStruct((N_DEV, _UP_SHARD, dm), BF16),
    )(n_t, dup2)


MESH = pl.DeviceIdType.MESH
ANY = pl.BlockSpec(memory_space=pl.ANY)


def _slot(px, py, pc):
    return 4 * px + 2 * py + pc


def _by_owner(ref, slot):
    return ref.at[slot]


def _row_block(rows, axis=0):
    def place(ref, slot):
        idx = (slice(None),) * axis + (pl.ds(slot * rows, rows),)
        return ref.at[idx]
    return place


def _all_gather(arrs, out_shapes, places, *, name):
    n = len(arrs)

    def body(*refs):
        ins, outs = refs[:n], refs[n:2 * n]
        send_sems, recv_sems, local_sems = refs[2 * n:]
        x, y, c = lax.axis_index("x"), lax.axis_index("y"), lax.axis_index("c")
        me, sibling = (x, y, c), (x, y, 1 - c)
        chips = [(1 - x, y), (x, 1 - y), (1 - x, 1 - y)]

        def copy(a, k, block, to, from_input=False):
            dst = places[a](outs[a], _slot(*block))
            return pltpu.make_async_remote_copy(
                src_ref=ins[a] if from_input else dst, dst_ref=dst,
                send_sem=send_sems.at[7 * a + k], recv_sem=recv_sems.at[7 * a + k],
                device_id=to, device_id_type=MESH)

        own, sent = [], []
        for a in range(n):
            mine = pltpu.make_async_copy(ins[a], places[a](outs[a], _slot(*me)), local_sems.at[a])
            mine.start()
            own.append(mine)
            first = [copy(a, 0, me, sibling, True)]
            first += [copy(a, 1 + j, me, (*chip, c), True) for j, chip in enumerate(chips)]
            for cp in first:
                cp.start()
            sent += first
        for a in range(n):
            for j, chip in enumerate(chips):
                copy(a, 1 + j, (*chip, c), me).wait_recv()
                passed = copy(a, 4 + j, (*chip, c), sibling)
                passed.start()
                sent.append(passed)
        for a in range(n):
            copy(a, 0, sibling, me).wait_recv()
            for j, chip in enumerate(chips):
                copy(a, 4 + j, (*chip, 1 - c), me).wait_recv()
        for cp in sent:
            cp.wait_send()
        for cp in own:
            cp.wait()

    return pl.pallas_call(
        body, name=name,
        in_specs=[ANY] * n, out_specs=[ANY] * n,
        out_shape=[jax.ShapeDtypeStruct(s, a.dtype) for s, a in zip(out_shapes, arrs)],
        scratch_shapes=[pltpu.SemaphoreType.DMA((7 * n,)), pltpu.SemaphoreType.DMA((7 * n,)),
                        pltpu.SemaphoreType.DMA((n,))],
    )(*arrs)


HBM = pl.BlockSpec(memory_space=pltpu.HBM)
SEM = pl.BlockSpec(memory_space=pltpu.SEMAPHORE)
EFFECT = pltpu.SideEffectType.DATAFLOW_SIDE_EFFECTING


def _relation_copies(src_refs, land_refs, copies, send_sems, recv_sems):
    x, y, c = lax.axis_index("x"), lax.axis_index("y"), lax.axis_index("c")
    flip = lambda v, bit: 1 - v if bit else v
    out = []
    for k in range(1, N_DEV):
        p = (flip(x, k & 4), flip(y, k & 2), flip(c, k & 1))
        for j, (si, take, li, put) in enumerate(copies):
            sem = (k - 1) * len(copies) + j
            dst = land_refs[li].at[k - 1] if put is None else put(land_refs[li], _slot(x, y, c))
            out.append(pltpu.make_async_remote_copy(
                src_ref=take(src_refs[si], _slot(*p)), dst_ref=dst,
                send_sem=send_sems.at[sem], recv_sem=recv_sems.at[sem], device_id=p, device_id_type=MESH))
    return out


def _own_copies(src_refs, land_refs, copies, sems):
    me = _slot(lax.axis_index("x"), lax.axis_index("y"), lax.axis_index("c"))
    placed = [(si, take, li, put) for si, take, li, put in copies if put is not None]
    return [pltpu.make_async_copy(take(src_refs[si], me), put(land_refs[li], me),
                                  sems.at[(N_DEV - 1) * len(copies) + j])
            for j, (si, take, li, put) in enumerate(placed)]


def _send_start(srcs, land_structs, copies, *, name, after=None):
    ns, nl = len(srcs), len(land_structs)
    n_sem = (N_DEV - 1) * len(copies) + sum(put is not None for _, _, _, put in copies)
    behind = [] if after is None else [after]

    def body(*refs):
        first_out = ns + nl + len(behind)
        send_sems, recv_sems, token = refs[first_out], refs[first_out + 1], refs[-1]
        for cp in _relation_copies(refs[:ns], refs[ns:ns + nl], copies, send_sems, recv_sems):
            cp.start()
        for cp in _own_copies(refs[:ns], refs[ns:ns + nl], copies, send_sems):
            cp.start()
        token[...] = jnp.zeros_like(token)

    in_hbm = lambda a: pltpu.with_memory_space_constraint(a, pltpu.HBM)
    outs = pl.pallas_call(
        body, name=name,
        out_shape=(pltpu.SemaphoreType.DMA((n_sem,)), pltpu.SemaphoreType.DMA((n_sem,)),
                   *[pltpu.HBM(s.shape, s.dtype) for s in srcs],
                   *[pltpu.HBM(s.shape, s.dtype) for s in land_structs],
                   jax.ShapeDtypeStruct((8, 128), F32)),
        in_specs=(HBM,) * (ns + nl) + (ANY,) * len(behind),
        out_specs=(SEM, SEM) + (HBM,) * (ns + nl) + (pl.BlockSpec(memory_space=pltpu.VMEM),),
        input_output_aliases={i: 2 + i for i in range(ns + nl)},
        compiler_params=pltpu.CompilerParams(has_side_effects=EFFECT),
    )(*[in_hbm(s) for s in srcs], *[in_hbm(lax.empty(s.shape, s.dtype)) for s in land_structs], *behind)
    return (outs[0], outs[1], outs[2:2 + ns], outs[2 + ns:2 + ns + nl], copies), outs[-1]


def _send_wait(handle, after, *, name):
    send_sems, recv_sems, srcs, lands, copies = handle
    ns, nl = len(srcs), len(lands)

    def body(*refs):
        for cp in _relation_copies(refs[:ns], refs[ns:ns + nl], copies, refs[ns + nl], refs[ns + nl + 1]):
            cp.wait_send()
            cp.wait_recv()
        for cp in _own_copies(refs[:ns], refs[ns:ns + nl], copies, refs[ns + nl]):
            cp.wait()

    outs = pl.pallas_call(
        body, name=name,
        out_shape=tuple(pltpu.HBM(a.shape, a.dtype) for a in (*srcs, *lands)),
        in_specs=(HBM,) * (ns + nl) + (SEM, SEM, ANY), out_specs=(HBM,) * (ns + nl),
        input_output_aliases={i: i for i in range(ns + nl)},
        compiler_params=pltpu.CompilerParams(has_side_effects=EFFECT),
    )(*srcs, *lands, send_sems, recv_sems, after)
    return outs[:ns], outs[ns:]


def _sum_slabs(stack, *, name, own=None):
    n, rows, w = stack.shape

    def body(*refs):
        s_ref, o_ref = refs[-2], refs[-1]
        acc = s_ref[0] if own is None else refs[0][...] + s_ref[0]
        for i in range(1, n):
            acc = acc + s_ref[i]
        o_ref[...] = acc

    return pl.pallas_call(body, name=name, out_shape=jax.ShapeDtypeStruct((rows, w), F32))(
        *([] if own is None else [own]), stack)


def _adam_math(w, g, m, v):
    mn = ADAM_B1 * m + (1.0 - ADAM_B1) * g
    vn = ADAM_B2 * v + (1.0 - ADAM_B2) * (g * g)
    m_hat = mn / (1.0 - ADAM_B1 ** ADAM_STEP)
    v_hat = vn / (1.0 - ADAM_B2 ** ADAM_STEP)
    return -ADAM_LR * (m_hat / (jnp.sqrt(v_hat) + ADAM_EPS) + ADAM_WD * w), mn, vn


def _adamw_many(ws, gs, ms, vs, *, name):
    n = len(ws)

    def body(*refs):
        for i in range(n):
            w_ref, g_ref, m_ref, v_ref = (refs[j * n + i] for j in range(4))
            d_ref, mo_ref, vo_ref = refs[4 * n + 3 * i:4 * n + 3 * i + 3]
            d_ref[...], mo_ref[...], vo_ref[...] = _adam_math(w_ref[...], g_ref[...], m_ref[...], v_ref[...])

    return pl.pallas_call(
        body, name=name, out_shape=[jax.ShapeDtypeStruct(w.shape, F32) for w in ws for _ in range(3)],
    )(*ws, *gs, *ms, *vs)


def _adamw_layers(w, owns, lands, m, v, tr, *, name):
    nl, rows, cols = w.shape
    steps = rows // tr
    assert rows % tr == 0

    def body(*refs):
        w_ref, m_ref, v_ref = refs[:3]
        own_refs, land_refs = refs[3:3 + nl], refs[3 + nl:3 + 2 * nl]
        g_ref, d_ref, mo_ref, vo_ref = refs[3 + 2 * nl:]
        for li in range(nl):
            @pl.when(pl.program_id(0) == li)
            def _(li=li):
                g = own_refs[li][...].astype(F32)
                for k in range(N_DEV - 1):
                    g = g + land_refs[li][k].astype(F32)
                g_ref[...] = g
                d_ref[...], mo_ref[...], vo_ref[...] = _adam_math(w_ref[...], g, m_ref[...], v_ref[...])

    def held(li):
        return lambda l, i: jnp.where(l == li, i, jnp.where(l < li, 0, steps - 1))

    blk = pl.BlockSpec((None, tr, cols), lambda l, i: (l, i, 0))
    own_specs = [pl.BlockSpec((tr, cols), lambda l, i, f=held(li): (f(l, i), 0)) for li in range(nl)]
    land_specs = [pl.BlockSpec((N_DEV - 1, tr, cols), lambda l, i, f=held(li): (0, f(l, i), 0)) for li in range(nl)]
    return pl.pallas_call(
        body, name=name, grid=(nl, steps),
        in_specs=[blk, blk, blk] + own_specs + land_specs, out_specs=[blk] * 4,
        out_shape=[jax.ShapeDtypeStruct(w.shape, F32)] * 4,
    )(w, m, v, *owns, *lands)


_WEIGHTS = (
    ("meta_tokens", (16, 1024), 1), ("mix_norm_even", (1, 1024), None), ("w_in", (1, 1024, 2568), 2),
    ("b_f", (1, 8), None), ("conv_w", (1, 31, 512), 2), ("conv_b", (1, 512), None), ("ln_g", (1, 512), None),
    ("ln_b", (1, 512), None), ("w_out", (1, 1024, 1024), 1), ("mix_norm_odd", (1, 1024), 1),
    ("pool_w", (1, 4, 256, 256), 2), ("pool_b", (1, 4, 256), 2), ("pool_scale", (1, 1024), 1),
    ("ffn_norm", (2, 1024), None), ("w_up", (2, 1024, 5632), 2), ("ffn_conv_w", (2, 3, 5632), 2),
    ("ffn_conv_b", (2, 5632), None), ("w_down", (2, 2816, 1024), 1), ("final_norm", (1024,), None),
)
_MATMUL_WEIGHTS = ("w_in", "w_out", "pool_w", "w_up", "w_down")
_ADAM_ROWS = dict(w_in=256, w_out=128, pool_w=128, w_up=352, w_down=352)


def _shard_shape(shape, axis):
    return shape[:axis] + (shape[axis] // N_DEV,) + shape[axis + 1:]


def _size(shape):
    n = 1
    for s in shape:
        n *= s
    return n


def _pack(parts, dtype, lead=(), align=16):
    flat = jnp.concatenate([p.reshape(lead + (-1,)).astype(dtype) for p in parts], axis=-1)
    n = flat.shape[-1]
    rows = _round_up(-(-n // FLAT_W), align)
    flat = jnp.pad(flat, [(0, 0)] * len(lead) + [(0, rows * FLAT_W - n)])
    return flat.reshape(lead + (rows, FLAT_W))


def _unpack(buf, shapes, lead=()):
    flat = buf.reshape(lead + (-1,))
    out, off = [], 0
    for shp in shapes:
        n = _size(shp)
        out.append(flat[..., off:off + n].reshape(lead + shp))
        off += n
    return out


def _gathered_to_full(stack, shape, axis):
    return jnp.moveaxis(stack, 0, axis).reshape(shape)


def _full_to_slabs(full, shape, axis):
    split = shape[:axis] + (N_DEV, shape[axis] // N_DEV) + shape[axis + 1:]
    return jnp.moveaxis(full.reshape(split), axis, 0)


def kernel(x, meta_tokens, mix_norm_even, w_in, b_f, conv_w, conv_b, ln_g, ln_b, w_out, mix_norm_odd, pool_w, pool_b, pool_scale, ffn_norm, w_up, ffn_conv_w, ffn_conv_b, w_down, final_norm, loss_target, m_meta_tokens, m_mix_norm_even, m_w_in, m_b_f, m_conv_w, m_conv_b, m_ln_g, m_ln_b, m_w_out, m_mix_norm_odd, m_pool_w, m_pool_b, m_pool_scale, m_ffn_norm, m_w_up, m_ffn_conv_w, m_ffn_conv_b, m_w_down, m_final_norm, v_meta_tokens, v_mix_norm_even, v_w_in, v_b_f, v_conv_w, v_conv_b, v_ln_g, v_ln_b, v_w_out, v_mix_norm_odd, v_pool_w, v_pool_b, v_pool_scale, v_ffn_norm, v_w_up, v_ffn_conv_w, v_ffn_conv_b, v_w_down, v_final_norm):
    names = [n for n, _, _ in _WEIGHTS]
    w_loc = dict(zip(names, (meta_tokens, mix_norm_even, w_in, b_f, conv_w, conv_b, ln_g, ln_b, w_out, mix_norm_odd,
                             pool_w, pool_b, pool_scale, ffn_norm, w_up, ffn_conv_w, ffn_conv_b, w_down, final_norm)))
    m_loc = dict(zip(names, (m_meta_tokens, m_mix_norm_even, m_w_in, m_b_f, m_conv_w, m_conv_b, m_ln_g, m_ln_b,
                             m_w_out, m_mix_norm_odd, m_pool_w, m_pool_b, m_pool_scale, m_ffn_norm, m_w_up,
                             m_ffn_conv_w, m_ffn_conv_b, m_w_down, m_final_norm)))
    v_loc = dict(zip(names, (v_meta_tokens, v_mix_norm_even, v_w_in, v_b_f, v_conv_w, v_conv_b, v_ln_g, v_ln_b,
                             v_w_out, v_mix_norm_odd, v_pool_w, v_pool_b, v_pool_scale, v_ffn_norm, v_w_up,
                             v_ffn_conv_w, v_ffn_conv_b, v_w_down, v_final_norm)))
    replicated = [(n, s) for n, s, a in _WEIGHTS if a is None]
    little = [(n, s, a) for n, s, a in _WEIGHTS if a is not None and n not in _MATMUL_WEIGHTS]
    little_shards = [_shard_shape(s, a) for _, s, a in little]
    out_rows, down_rows, pool_rows = D_MODEL // N_DEV, D_FF // N_DEV, POOL_GROUP // N_DEV
    n_groups = len(POOL_WINDOWS)

    little_pack = _pack([w_loc[n] for n, _, _ in little], F32)
    g_win, g_wout, g_poolw, g_little = _all_gather(
        [w_in[0].astype(BF16), w_out[0].astype(BF16), pool_w[0].astype(BF16), little_pack],
        [(N_DEV, D_MODEL, _IN_SHARD), (D_MODEL, D_MODEL), (n_groups, POOL_GROUP, POOL_GROUP),
         (N_DEV,) + little_pack.shape],
        [_by_owner, _row_block(out_rows), _row_block(pool_rows, axis=1), _by_owner],
        name="gather_weights")
    me = _slot(lax.axis_index("x"), lax.axis_index("y"), lax.axis_index("c"))
    up_t = lambda a: jnp.transpose(a, (0, 2, 1))
    w_loc["w_up"], m_loc["w_up"], v_loc["w_up"] = up_t(w_up), up_t(m_w_up), up_t(v_w_up)
    w_up_b, w_down_b = w_loc["w_up"].astype(BF16), w_down.astype(BF16)
    whole = lambda ref, slot: ref
    ffn_lands = [jax.ShapeDtypeStruct((2 * D_FF, D_MODEL), BF16), jax.ShapeDtypeStruct((D_FF, D_MODEL), BF16)]
    ffn_gathers, behind = [], g_little
    for l in range(DEPTH):
        handle, behind = _send_start([w_up_b[l], w_down_b[l]], ffn_lands,
                                     [(0, whole, 0, _row_block(_UP_SHARD)), (1, whole, 1, _row_block(down_rows))],
                                     name=f"gather_ffn{l}_start", after=behind)
        ffn_gathers.append(handle)

    def ffn_weights(l, after):
        return _send_wait(ffn_gathers[l], after, name=f"gather_ffn{l}_wait")[1]

    w_in_p = _assemble_w_in(g_win, name="assemble_w_in")
    full = {n: _gathered_to_full(st, s, a)
            for (n, s, a), st in zip(little, _unpack(g_little, little_shards, lead=(N_DEV,)))}
    f0 = _QKV + 2 * CONV_CH
    wt = dict(
        meta=full["meta_tokens"], g_even=mix_norm_even, w_in_p=w_in_p, wf_t=w_in_p[:, f0:f0 + FOX_HEADS].T,
        b_f=b_f.reshape(FOX_HEADS, 1), conv_w=full["conv_w"][0], conv_b=conv_b, ln_g=ln_g, ln_b=ln_b, w_out=g_wout,
        g_odd=full["mix_norm_odd"], pool_w=g_poolw, pool_b=full["pool_b"].reshape(1, D_MODEL),
        pool_scale=full["pool_scale"], ffn_norm=ffn_norm, ffn_weights=ffn_weights, ffn_started=behind,
        fcw3=full["ffn_conv_w"].reshape(DEPTH, FFN_CONV_WIDTH, 2, D_FF).transpose(0, 2, 1, 3),
        fcb3=ffn_conv_b.reshape(DEPTH, 2, 1, D_FF), g_final=final_norm.reshape(1, D_MODEL))

    loss_part, grad_x, g = _local_step(x[0], loss_target[0], wt)

    small = _pack([loss_part[:, 0:1]] + [g[n] for n, _ in replicated] + [g["meta_tokens"]], F32, align=8)
    send_small, token_small = _send_start([small], [jax.ShapeDtypeStruct((N_DEV,) + small.shape, F32)],
                                          [(0, whole, 0, _by_owner)], name="send_small")

    grads, delta, new_m, new_v = {}, {}, {}, {}
    send0, send1, send_rest, send_in = g["sends"]
    ffn_sent = [_send_wait(send, token_small, name=f"wait_ffn{l}") for l, send in enumerate((send0, send1))]
    own_up = [lax.dynamic_index_in_dim(srcs[0], me, 0, keepdims=False) for srcs, _ in ffn_sent]
    own_down = [lax.dynamic_slice_in_dim(srcs[1], me * down_rows, down_rows, 0) for srcs, _ in ffn_sent]
    for n, owns, idx in (("w_up", own_up, 0), ("w_down", own_down, 1)):
        grads[n], delta[n], new_m[n], new_v[n] = _adamw_layers(
            w_loc[n], owns, [lands[idx] for _, lands in ffn_sent], m_loc[n], v_loc[n], _ADAM_ROWS[n],
            name=f"adamw_{n}")
    for d in (grads, delta, new_m, new_v):
        d["w_up"] = up_t(d["w_up"])
    (d_out, d_pool, little_slabs), (land_out, land_pool, land_little) = _send_wait(
        send_rest, delta["w_down"], name="wait_rest")
    (d_in,), (land_in,) = _send_wait(send_in, land_out, name="wait_w_in")
    pool_2d = (n_groups * pool_rows, POOL_GROUP)
    own_pool = lax.dynamic_slice_in_dim(d_pool, me * pool_rows, pool_rows, 1)
    for n, own, land, shp in (
            ("w_in", lax.dynamic_index_in_dim(d_in, me, 0, keepdims=False), land_in, w_in.shape),
            ("w_out", lax.dynamic_slice_in_dim(d_out, me * out_rows, out_rows, 0), land_out, w_out.shape),
            ("pool_w", own_pool.reshape(pool_2d), land_pool.reshape((N_DEV - 1,) + pool_2d), (1,) + pool_2d)):
        outs = _adamw_layers(w_loc[n].reshape(shp), [own], [land], m_loc[n].reshape(shp), v_loc[n].reshape(shp),
                             _ADAM_ROWS[n], name=f"adamw_{n}")
        grads[n], delta[n], new_m[n], new_v[n] = (o.reshape(w_loc[n].shape) for o in outs)
    own_little = lax.dynamic_index_in_dim(little_slabs, me, 0, keepdims=False)
    g_little = _unpack(_sum_slabs(land_little, own=own_little, name="sum_little"),
                       [_shard_shape(s, a) for _, s, a in _LITTLE])
    grads.update({n: gl for (n, _, _), gl in zip(_LITTLE, g_little)})
    _, (everyone,) = _send_wait(send_small, delta["w_in"], name="wait_small")
    summed = _unpack(_sum_slabs(everyone, name="sum_small"),
                     [(1, 1)] + [s for _, s in replicated] + [(N_META, D_MODEL)])
    loss = summed[0].reshape(())
    grads.update({n: gr for (n, _), gr in zip(replicated, summed[1:-1])})
    grads["meta_tokens"] = lax.dynamic_slice_in_dim(summed[-1], me * out_rows, out_rows, 1)
    two_d = lambda a: a.reshape(_size(a.shape[:-1]), a.shape[-1])
    rest = [n for n in names if n not in _MATMUL_WEIGHTS]
    outs = _adamw_many([two_d(w_loc[n]) for n in rest], [two_d(grads[n]) for n in rest],
                       [two_d(m_loc[n]) for n in rest], [two_d(v_loc[n]) for n in rest], name="adamw_rest")
    for i, n in enumerate(rest):
        delta[n], new_m[n], new_v[n] = (o.reshape(w_loc[n].shape) for o in outs[3 * i:3 * i + 3])
    return (loss, grad_x[None], *[grads[n] for n in names], *[delta[n] for n in names],
            *[new_m[n] for n in names], *[new_v[n] for n in names])
```

```python
import functools

import jax
import jax.numpy as jnp
from jax import lax
from jax.experimental import pallas as pl
from jax.experimental.pallas import tpu as pltpu

F32 = jnp.float32
BF16 = jnp.bfloat16

N_DEV = 8
DEPTH = 2
D_MODEL = 1024
N_META = 16
FOX_HEADS = 8
FOX_HEAD_DIM = 64
FOX_WIDTH = 512
CONV_CH = 512
CONV_WIDTH = 31
POOL_WINDOWS = (2, 4, 8, 16)
POOL_GROUP = 256
D_FF = 2816
FFN_CONV_WIDTH = 3
RMS_EPS = 1e-6
LN_EPS = 1e-5
ADAM_LR = 0.001
ADAM_B1 = 0.9
ADAM_B2 = 0.999
ADAM_EPS = 1e-08
ADAM_WD = 0.01
ADAM_STEP = 10

CHUNK = 128
HALO = 32
NEG_BIG = -1e30
FLAT_W = 1024


def _round_up(n, m):
    return (n + m - 1) // m * m


def _sigmoid(x):
    return 1.0 / (1.0 + jnp.exp(-x))


def _fold8(p):
    acc = p[0:8, :]
    for r in range(1, p.shape[0] // 8):
        acc = acc + p[8 * r:8 * r + 8, :]
    return acc


def _mm(a, b, *, name, tb=False, tm=None, tn=None, tk=None, out_dtype=F32, res=None,
        a_map=None, b_map=None, o_map=None, out_shape=None, dims=None, after=None):
    if dims is None:
        m, k = a.shape
        n = b.shape[-2] if tb else b.shape[-1]
    else:
        m, n, k = dims
    tm, tn, tk = tm or m, tn or n, tk or k
    assert m % tm == 0 and n % tn == 0 and k % tk == 0, (name, m, n, k, tm, tn, tk)
    nk = k // tk
    a_map = a_map or (lambda i, j, kk: (i, kk))
    b_map = b_map or ((lambda i, j, kk: (j, kk)) if tb else (lambda i, j, kk: (kk, j)))
    o_map = o_map or (lambda i, j, kk: (i, j))
    out_shape = out_shape or (m, n)
    contract = (((1,), (1,)), ((), ())) if tb else (((1,), (0,)), ((), ()))
    has_res = res is not None

    def body(*refs):
        a_ref, b_ref = refs[0], refs[1]
        res_ref = refs[2] if has_res else None
        o_ref = refs[2 + has_res + (after is not None)]
        p = lax.dot_general(a_ref[...], b_ref[...], contract, preferred_element_type=F32)
        if nk == 1:
            if has_res:
                p = p + res_ref[...]
            o_ref[...] = p.astype(o_ref.dtype)
        else:
            acc_ref = refs[-1]
            kk = pl.program_id(2)

            @pl.when(kk == 0)
            def _():
                acc_ref[...] = p

            @pl.when(kk > 0)
            def _():
                acc_ref[...] += p

            @pl.when(kk == nk - 1)
            def _():
                r = acc_ref[...]
                if has_res:
                    r = r + res_ref[...]
                o_ref[...] = r.astype(o_ref.dtype)

    in_specs = [pl.BlockSpec((tm, tk), a_map), pl.BlockSpec((tn, tk) if tb else (tk, tn), b_map)]
    operands = [a, b]
    if has_res:
        in_specs.append(pl.BlockSpec((tm, tn), o_map))
        operands.append(res)
    if after is not None:
        in_specs.append(pl.BlockSpec(memory_space=pl.ANY))
        operands.append(after)
    return pl.pallas_call(
        body, name=name, grid=(m // tm, n // tn, nk),
        in_specs=in_specs, out_specs=pl.BlockSpec((tm, tn), o_map),
        out_shape=jax.ShapeDtypeStruct(out_shape, out_dtype),
        scratch_shapes=[pltpu.VMEM((tm, tn), F32)] if nk > 1 else [],
    )(*operands)


def _transpose(x, *, name, out_dtype, cols=None, after=None):
    r, c = x.shape
    cols = cols or c
    assert r % CHUNK == 0
    behind = [] if after is None else [after]

    def body(x_ref, *rest):
        o_ref = rest[-1]
        o_ref[...] = x_ref[...].astype(o_ref.dtype).T

    return pl.pallas_call(
        body, name=name, grid=(r // CHUNK,),
        in_specs=[pl.BlockSpec((CHUNK, cols), lambda i: (i, 0))] + [pl.BlockSpec(memory_space=pl.ANY)] * len(behind),
        out_specs=pl.BlockSpec((cols, CHUNK), lambda i: (0, i)),
        out_shape=jax.ShapeDtypeStruct((cols, r), out_dtype),
    )(x, *behind)


def _rms_fwd(x, g, *, name, out_dtype, after=None, transposed=False):
    lp, dm = x.shape
    tr = CHUNK if transposed else lp // 4
    behind = [] if after is None else [after]

    def body(x_ref, g_ref, *rest):
        xv = x_ref[...]
        r = lax.rsqrt(jnp.mean(xv * xv, axis=-1, keepdims=True) + RMS_EPS)
        y = xv * r * g_ref[...]
        if transposed:
            yb = y.astype(out_dtype)
            rest[-2][...] = yb
            rest[-1][...] = yb.T
        else:
            rest[-1][...] = y.astype(out_dtype)

    row = pl.BlockSpec((tr, dm), lambda i: (i, 0))
    out_specs, out_shape = row, jax.ShapeDtypeStruct((lp, dm), out_dtype)
    if transposed:
        out_specs = [row, pl.BlockSpec((dm, tr), lambda i: (0, i))]
        out_shape = [out_shape, jax.ShapeDtypeStruct((dm, lp), out_dtype)]
    return pl.pallas_call(
        body, name=name, grid=(lp // tr,),
        in_specs=[row, pl.BlockSpec((1, dm), lambda i: (0, 0))] + [pl.BlockSpec(memory_space=pl.ANY)] * len(behind),
        out_specs=out_specs, out_shape=out_shape,
    )(x, g, *behind)


def _rms_bwd_rows(x_ref, g_ref, dnv, dres_ref, dh_ref, dhb_ref, dg_ref, first):
    xv = x_ref[...]
    r = lax.rsqrt(jnp.mean(xv * xv, axis=-1, keepdims=True) + RMS_EPS)
    xhat = xv * r

    @pl.when(first)
    def _():
        dg_ref[...] = jnp.zeros_like(dg_ref)

    dg_ref[...] += jnp.sum(dnv * xhat, axis=0, keepdims=True)
    dxhat = dnv * g_ref[...]
    dx = r * (dxhat - xhat * jnp.mean(dxhat * xhat, axis=-1, keepdims=True))
    dh = dres_ref[...] + dx
    dh_ref[...] = dh
    dhb_ref[...] = dh.astype(BF16)


def _mm_rms_bwd(a, b, x, g, dres, *, name, tk, tb=False, a_map=None, after=None):
    lp, dm = x.shape
    tm = lp // 4
    nk = (b.shape[1] if tb else b.shape[0]) // tk
    a_map = a_map or (lambda i, kk: (i, kk))
    contract = (((1,), (1,)), ((), ())) if tb else (((1,), (0,)), ((), ()))
    behind = [] if after is None else [after]

    def body(a_ref, b_ref, x_ref, g_ref, dres_ref, *rest):
        dh_ref, dhb_ref, dg_ref, acc_ref = rest[len(behind):]
        i, kk = pl.program_id(0), pl.program_id(1)
        p = lax.dot_general(a_ref[...], b_ref[...], contract, preferred_element_type=F32)

        @pl.when(kk == 0)
        def _():
            acc_ref[...] = p

        @pl.when(kk > 0)
        def _():
            acc_ref[...] += p

        @pl.when(kk == nk - 1)
        def _():
            _rms_bwd_rows(x_ref, g_ref, acc_ref[...], dres_ref, dh_ref, dhb_ref, dg_ref, i == 0)

    row = pl.BlockSpec((tm, dm), lambda i, kk: (i, 0))
    vec = pl.BlockSpec((1, dm), lambda i, kk: (0, 0))
    b_spec = pl.BlockSpec((dm, tk), lambda i, kk: (0, kk)) if tb else pl.BlockSpec((tk, dm), lambda i, kk: (kk, 0))
    return pl.pallas_call(
        body, name=name, grid=(lp // tm, nk),
        in_specs=[pl.BlockSpec((tm, tk), a_map), b_spec, row, vec, row] + [pl.BlockSpec(memory_space=pl.ANY)] * len(behind),
        out_specs=[row, row, vec],
        out_shape=[jax.ShapeDtypeStruct((lp, dm), F32), jax.ShapeDtypeStruct((lp, dm), BF16),
                   jax.ShapeDtypeStruct((1, dm), F32)],
        scratch_shapes=[pltpu.VMEM((tm, dm), F32)],
    )(a, b, x, g, dres, *behind)


def _rms_bwd(x, g, dn, dres, *, name):
    lp, dm = x.shape
    tr = lp // 4

    def body(x_ref, g_ref, dn_ref, dres_ref, dh_ref, dhb_ref, dg_ref):
        _rms_bwd_rows(x_ref, g_ref, dn_ref[...], dres_ref, dh_ref, dhb_ref, dg_ref, pl.program_id(0) == 0)

    row = pl.BlockSpec((tr, dm), lambda i: (i, 0))
    vec = pl.BlockSpec((1, dm), lambda i: (0, 0))
    return pl.pallas_call(
        body, name=name, grid=(lp // tr,),
        in_specs=[row, vec, row, row], out_specs=[row, row, vec],
        out_shape=[jax.ShapeDtypeStruct((lp, dm), F32), jax.ShapeDtypeStruct((lp, dm), BF16),
                   jax.ShapeDtypeStruct((1, dm), F32)],
    )(x, g, dn, dres)


def _loss_head(h, g, tgt, n_real, *, name):
    lp, dm = h.shape
    tr = lp // 4

    def body(x_ref, g_ref, t_ref, loss_ref, dh_ref, dhb_ref, dg_ref):
        i = pl.program_id(0)
        xv = x_ref[...]
        r = lax.rsqrt(jnp.mean(xv * xv, axis=-1, keepdims=True) + RMS_EPS)
        xhat = xv * r
        gv = g_ref[...]
        y = xhat * gv
        t = i * tr + lax.broadcasted_iota(jnp.int32, (tr, 1), 0)
        valid = (t >= N_META) & (t < n_real)
        diff = jnp.where(valid, y - t_ref[...], 0.0)

        @pl.when(i == 0)
        def _():
            loss_ref[...] = jnp.zeros_like(loss_ref)
            dg_ref[...] = jnp.zeros_like(dg_ref)

        row_sq = jnp.sum(diff * diff, axis=-1, keepdims=True) * (1.0 / dm)
        part = 0.5 * jnp.sum(row_sq, axis=0, keepdims=True)
        loss_ref[...] += jnp.broadcast_to(part, loss_ref.shape)
        dy = diff * (1.0 / dm)
        dg_ref[...] += jnp.sum(dy * xhat, axis=0, keepdims=True)
        dxhat = dy * gv
        dx = r * (dxhat - xhat * jnp.mean(dxhat * xhat, axis=-1, keepdims=True))
        dh_ref[...] = dx
        dhb_ref[...] = dx.astype(BF16)

    row = pl.BlockSpec((tr, dm), lambda i: (i, 0))
    vec = pl.BlockSpec((1, dm), lambda i: (0, 0))
    return pl.pallas_call(
        body, name=name, grid=(lp // tr,),
        in_specs=[row, vec, row],
        out_specs=[pl.BlockSpec((1, 128), lambda i: (0, 0)), row, row, vec],
        out_shape=[jax.ShapeDtypeStruct((1, 128), F32), jax.ShapeDtypeStruct((lp, dm), F32),
                   jax.ShapeDtypeStruct((lp, dm), BF16), jax.ShapeDtypeStruct((1, dm), F32)],
    )(h, g, tgt)


def _tri(upper):
    r = lax.broadcasted_iota(jnp.int32, (CHUNK, CHUNK), 0)
    c = lax.broadcasted_iota(jnp.int32, (CHUNK, CHUNK), 1)
    return jnp.where(r <= c if upper else r >= c, 1.0, 0.0).astype(F32)


def _fox_prep(f_t, b_f, *, name):
    nh, lp = f_t.shape
    nch = lp // CHUNK

    def body(f_ref, b_ref, c_ref):
        tri = _tri(True)
        carry = jnp.zeros((nh, 1), F32)
        for blk in range(nch):
            cols = slice(blk * CHUNK, (blk + 1) * CHUNK)
            z = f_ref[:, cols] + b_ref[...]
            logf = jnp.minimum(z, 0.0) - jnp.log(1.0 + jnp.exp(-jnp.abs(z)))
            cb = jnp.dot(logf, tri, preferred_element_type=F32, precision=lax.Precision.HIGHEST)
            c_ref[:, cols] = cb + carry
            carry = carry + jnp.sum(logf, axis=1, keepdims=True)

    return pl.pallas_call(
        body, name=name, out_shape=jax.ShapeDtypeStruct((nh, lp), F32),
    )(f_t, b_f)


def _fox_bwd(dc, f_t, b_f, *, name):
    nh, lp = f_t.shape
    nch = lp // CHUNK

    def body(dc_ref, f_ref, b_ref, df_ref, db_ref):
        tri = _tri(False)
        carry = jnp.zeros((nh, 1), F32)
        db = jnp.zeros((nh, 1), F32)
        df_ref[...] = jnp.zeros_like(df_ref)
        for blk in reversed(range(nch)):
            cols = slice(blk * CHUNK, (blk + 1) * CHUNK)
            dcb = dc_ref[:, cols]
            dlogf = jnp.dot(dcb, tri, preferred_element_type=F32, precision=lax.Precision.HIGHEST) + carry
            carry = carry + jnp.sum(dcb, axis=1, keepdims=True)
            z = f_ref[:, cols] + b_ref[...]
            dz = dlogf * _sigmoid(-z)
            df_ref[0:nh, cols] = dz
            db = db + jnp.sum(dz, axis=1, keepdims=True)
        db_ref[...] = db

    return pl.pallas_call(
        body, name=name,
        out_shape=[jax.ShapeDtypeStruct((128, lp), F32), jax.ShapeDtypeStruct((nh, 1), F32)],
    )(dc, f_t, b_f)


ATTN_BLOCKS = 4


def _attn_blocks(lp):
    tq = lp // ATTN_BLOCKS
    return tq, [(i * tq, min(lp, _round_up((i + 1) * tq, CHUNK))) for i in range(ATTN_BLOCKS)]


ATTN_SCALE = FOX_HEAD_DIM ** -0.5


def _attn_probs(q2s, k_h, c_row, row0, n):
    tq = q2s.shape[0]
    lo = row0 // CHUNK * CHUNK
    logits = []
    for c0, c1 in ([(0, lo)] if lo else []) + [(lo, n)]:
        s = lax.dot_general(q2s, k_h[c0:c1], (((1,), (1,)), ((), ())), preferred_element_type=F32) - c_row[:, c0:c1]
        if c1 > row0:
            t = row0 + lax.broadcasted_iota(jnp.int32, (tq, c1 - c0), 0)
            sidx = c0 + lax.broadcasted_iota(jnp.int32, (tq, c1 - c0), 1)
            s = jnp.where(sidx <= t, s, NEG_BIG)
        logits.append((s, c0, c1))
    m = functools.reduce(jnp.maximum, [jnp.max(s, axis=1, keepdims=True) for s, _, _ in logits])
    ps = [(jnp.exp(s - m), c0, c1) for s, c0, c1 in logits]
    inv = 1.0 / sum(jnp.sum(p, axis=1, keepdims=True) for p, _, _ in ps)
    return [(p * inv, c0, c1) for p, c0, c1 in ps]


def _attn_fwd(qkv, c3, *, name):
    lp = qkv.shape[0]
    tq, blocks = _attn_blocks(lp)

    def body(q_ref, k_ref, v_ref, c_ref, o_ref):
        lane = lax.broadcasted_iota(jnp.int32, (1, 128), 1)
        zero = jnp.zeros((), BF16)
        for i, (row0, n) in enumerate(blocks):
            q2s = q_ref[row0:row0 + tq, :] * ATTN_SCALE
            acc = jnp.zeros((tq, 128), F32)
            for hd in range(2):
                sel = (lane < 64) if hd == 0 else (lane >= 64)
                k_h = jnp.where(sel, k_ref[0:n, :], zero)
                v_h = jnp.where(sel, v_ref[0:n, :], zero)
                for p, c0, c1 in _attn_probs(q2s, k_h, c_ref[hd:hd + 1, 0:n], row0, n):
                    acc = acc + jnp.dot(p.astype(BF16), v_h[c0:c1], preferred_element_type=F32)
            o_ref[row0:row0 + tq, :] = acc.astype(BF16)

    blk = lambda off: pl.BlockSpec((lp, 128), lambda p: (0, off + p))
    return pl.pallas_call(
        body, name=name, grid=(4,),
        in_specs=[blk(0), blk(4), blk(8), pl.BlockSpec((None, 2, lp), lambda p: (p, 0, 0))],
        out_specs=pl.BlockSpec((lp, 128), lambda p: (0, p)),
        out_shape=jax.ShapeDtypeStruct((lp, FOX_WIDTH), BF16),
    )(qkv, qkv, qkv, c3)


def _attn_bwd(qkv, q_t, dcat, do_t, c3, *, name):
    lp = qkv.shape[0]
    tq, blocks = _attn_blocks(lp)
    scale = FOX_HEAD_DIM ** -0.5

    def body(q_ref, k_ref, v_ref, qt_ref, do_ref, dot_ref, c_ref, dq_ref, dkt_ref, dvt_ref, dc_ref,
             dkt_acc, dvt_acc):
        lane = lax.broadcasted_iota(jnp.int32, (1, 128), 1)
        sub = lax.broadcasted_iota(jnp.int32, (128, 1), 0)
        zero = jnp.zeros((), BF16)
        dkt_acc[...] = jnp.zeros_like(dkt_acc)
        dvt_acc[...] = jnp.zeros_like(dvt_acc)
        dc_ref[...] = jnp.zeros_like(dc_ref)
        for i, (row0, n) in enumerate(blocks):
            rows = slice(row0, row0 + tq)
            q2s = q_ref[rows, :] * ATTN_SCALE
            do2 = do_ref[rows, :].astype(BF16)
            dq_acc = jnp.zeros((tq, 128), F32)
            for hd in range(2):
                sel = (lane < 64) if hd == 0 else (lane >= 64)
                sel_t = (sub < 64) if hd == 0 else (sub >= 64)
                k_h = jnp.where(sel, k_ref[0:n, :], zero)
                v_h = jnp.where(sel, v_ref[0:n, :], zero)
                qt_h = jnp.where(sel_t, qt_ref[:, rows], zero)
                dot_h = jnp.where(sel_t, dot_ref[:, rows], zero)
                segs = [(p, lax.dot_general(do2, v_h[c0:c1], (((1,), (1,)), ((), ())), preferred_element_type=F32),
                         c0, c1) for p, c0, c1 in _attn_probs(q2s, k_h, c_ref[hd:hd + 1, 0:n], row0, n)]
                delta = sum(jnp.sum(p * dp, axis=1, keepdims=True) for p, dp, _, _ in segs)
                for p, dp, c0, c1 in segs:
                    ds = p * (dp - delta)
                    dsb = ds.astype(BF16)
                    dq_acc = dq_acc + jnp.dot(dsb, k_h[c0:c1], preferred_element_type=F32)
                    dkt_acc[:, c0:c1] += jnp.dot(qt_h, dsb, preferred_element_type=F32)
                    dvt_acc[:, c0:c1] += jnp.dot(dot_h, p.astype(BF16), preferred_element_type=F32)
                    dc_ref[hd:hd + 1, c0:c1] -= jnp.sum(ds, axis=0, keepdims=True)
            dq_ref[rows, :] = (dq_acc * scale).astype(BF16)
        dkt_ref[...] = (dkt_acc[...] * scale).astype(BF16)
        dvt_ref[...] = dvt_acc[...].astype(BF16)

    blk = lambda off: pl.BlockSpec((lp, 128), lambda p: (0, off + p))
    blk_t = pl.BlockSpec((128, lp), lambda p: (p, 0))
    c_spec = pl.BlockSpec((None, 2, lp), lambda p: (p, 0, 0))
    return pl.pallas_call(
        body, name=name, grid=(4,),
        in_specs=[blk(0), blk(4), blk(8), blk_t, blk(0), blk_t, c_spec],
        out_specs=[blk(0), blk_t, blk_t, c_spec],
        out_shape=[jax.ShapeDtypeStruct((lp, FOX_WIDTH), BF16), jax.ShapeDtypeStruct((FOX_WIDTH, lp), BF16),
                   jax.ShapeDtypeStruct((FOX_WIDTH, lp), BF16), jax.ShapeDtypeStruct((4, 2, lp), F32)],
        scratch_shapes=[pltpu.VMEM((128, lp), F32), pltpu.VMEM((128, lp), F32)],
    )(qkv, qkv, qkv, q_t, dcat, do_t, c3)


def _ln_stats(x):
    mu = jnp.mean(x, axis=-1, keepdims=True)
    xc = x - mu
    var = jnp.mean(xc * xc, axis=-1, keepdims=True)
    rstd = lax.rsqrt(var + LN_EPS)
    return xc * rstd, rstd


def _conv_fwd(agf, conv_w, conv_b, ln_g, ln_b, *, name):
    lp = agf.shape[0]
    nch = lp // CHUNK
    c = CONV_CH

    def body(a_ref, g_ref, w_ref, b_ref, lg_ref, lb_ref, u0_ref, u1_ref, u3_ref, u0s):
        u0s[0:HALO, :] = jnp.zeros((HALO, c), F32)

        def glu(ci, _):
            rows = pl.ds(pl.multiple_of(ci * CHUNK, CHUNK), CHUNK)
            u0 = a_ref[rows, :] * _sigmoid(g_ref[rows, :])
            u0_ref[rows, :] = u0
            u0s[pl.ds(pl.multiple_of(ci * CHUNK + HALO, 8), CHUNK), :] = u0
            return 0

        lax.fori_loop(0, nch, glu, 0)

        def conv(ci, _):
            r0 = pl.multiple_of(ci * CHUNK, CHUNK)
            rows = pl.ds(r0, CHUNK)
            for lg in range(c // 128):
                lanes = slice(lg * 128, (lg + 1) * 128)
                win = u0s[pl.ds(r0, CHUNK + HALO), lanes]
                acc = jnp.broadcast_to(b_ref[:, lanes], (CHUNK, 128))
                for k in range(CONV_WIDTH):
                    s = CONV_WIDTH - 1 - k
                    sh = win if s == 0 else pltpu.roll(win, s, 0)
                    acc = acc + w_ref[k:k + 1, lanes] * sh[HALO:HALO + CHUNK, :]
                u1_ref[rows, lanes] = acc
            xhat, _ = _ln_stats(u1_ref[rows, :])
            y = xhat * lg_ref[...] + lb_ref[...]
            u3_ref[rows, :] = (y * _sigmoid(y)).astype(BF16)
            return 0

        lax.fori_loop(0, nch, conv, 0)

    full = lambda shape: pl.BlockSpec(shape, lambda i: (0, 0))
    return pl.pallas_call(
        body, name=name, grid=(1,),
        in_specs=[pl.BlockSpec((lp, c), lambda i: (0, 0)), pl.BlockSpec((lp, c), lambda i: (0, 1)),
                  full((CONV_WIDTH, c)), full((1, c)), full((1, c)), full((1, c))],
        out_specs=[full((lp, c)), full((lp, c)), full((lp, c))],
        out_shape=[jax.ShapeDtypeStruct((lp, c), F32), jax.ShapeDtypeStruct((lp, c), F32),
                   jax.ShapeDtypeStruct((lp, c), BF16)],
        scratch_shapes=[pltpu.VMEM((lp + HALO, c), F32)],
    )(agf, agf, conv_w, conv_b, ln_g, ln_b)


def _conv_bwd(dcat, u0, u1, agf, conv_w, ln_g, ln_b, *, name):
    lp = agf.shape[0]
    nch = lp // CHUNK
    c = CONV_CH
    wlen = CHUNK + HALO

    def body(du3_ref, u0_ref, u1_ref, a_ref, g_ref, w_ref, lg_ref, lb_ref,
             dag_ref, dw_ref, db_ref, dlg_ref, dlb_ref, du1s, dwacc, vacc):
        du1s[lp:lp + HALO, :] = jnp.zeros((HALO, c), F32)
        dwacc[...] = jnp.zeros_like(dwacc)
        vacc[...] = jnp.zeros_like(vacc)

        def ln_bwd(ci, _):
            r0 = pl.multiple_of(ci * CHUNK, CHUNK)
            rows = pl.ds(r0, CHUNK)
            xhat, rstd = _ln_stats(u1_ref[rows, :])
            y = xhat * lg_ref[...] + lb_ref[...]
            sg = _sigmoid(y)
            du2 = du3_ref[rows, :] * (sg * (1.0 + y * (1.0 - sg)))
            vacc[0:8, :] += _fold8(du2 * xhat)
            vacc[8:16, :] += _fold8(du2)
            dxhat = du2 * lg_ref[...]
            du1 = rstd * (dxhat - jnp.mean(dxhat, axis=-1, keepdims=True)
                          - xhat * jnp.mean(dxhat * xhat, axis=-1, keepdims=True))
            vacc[16:24, :] += _fold8(du1)
            du1s[rows, :] = du1
            return 0

        lax.fori_loop(0, nch, ln_bwd, 0)

        def conv_bwd(ci, _):
            r0 = pl.multiple_of(ci * CHUNK, CHUNK)
            rows = pl.ds(r0, CHUNK)
            for lg in range(c // 128):
                lanes = slice(lg * 128, (lg + 1) * 128)
                dwin = du1s[pl.ds(r0, wlen), lanes]
                u0 = u0_ref[rows, lanes]
                acc = jnp.zeros((CHUNK, 128), F32)
                for k in range(CONV_WIDTH):
                    s = CONV_WIDTH - 1 - k
                    d_s = (dwin if s == 0 else pltpu.roll(dwin, wlen - s, 0))[0:CHUNK, :]
                    acc = acc + w_ref[k:k + 1, lanes] * d_s
                    dwacc[8 * k:8 * k + 8, lanes] += _fold8(d_s * u0)
                sg = _sigmoid(g_ref[rows, lanes])
                a = a_ref[rows, lanes]
                dag_ref[rows, lanes] = (acc * sg).astype(BF16)
                dag_ref[rows, slice(c + lg * 128, c + (lg + 1) * 128)] = (acc * a * sg * (1.0 - sg)).astype(BF16)
            return 0

        lax.fori_loop(0, nch, conv_bwd, 0)
        for k in range(CONV_WIDTH):
            dw_ref[k:k + 1, :] = jnp.sum(dwacc[8 * k:8 * k + 8, :], axis=0, keepdims=True)
        dlg_ref[...] = jnp.sum(vacc[0:8, :], axis=0, keepdims=True)
        dlb_ref[...] = jnp.sum(vacc[8:16, :], axis=0, keepdims=True)
        db_ref[...] = jnp.sum(vacc[16:24, :], axis=0, keepdims=True)

    full = lambda shape: pl.BlockSpec(shape, lambda i: (0, 0))
    vec = jax.ShapeDtypeStruct((1, c), F32)
    return pl.pallas_call(
        body, name=name, grid=(1,),
        in_specs=[pl.BlockSpec((lp, c), lambda i: (0, 1)), full((lp, c)), full((lp, c)),
                  pl.BlockSpec((lp, c), lambda i: (0, 0)), pl.BlockSpec((lp, c), lambda i: (0, 1)),
                  full((CONV_WIDTH, c)), full((1, c)), full((1, c))],
        out_specs=[full((lp, 2 * c)), full((CONV_WIDTH, c)), full((1, c)), full((1, c)), full((1, c))],
        out_shape=[jax.ShapeDtypeStruct((lp, 2 * c), BF16), jax.ShapeDtypeStruct((CONV_WIDTH, c), F32), vec, vec, vec],
        scratch_shapes=[pltpu.VMEM((lp + HALO, c), F32), pltpu.VMEM((8 * CONV_WIDTH, c), F32),
                        pltpu.VMEM((24, c), F32)],
    )(dcat, u0, u1, agf, agf, conv_w, ln_g, ln_b)


FFN_TILE = 256
FFN_PAD = 8


def _ffn_conv(xs, w_ref, b_ref, half, r0):
    win = xs[half, pl.ds(r0, CHUNK + FFN_PAD), :]
    acc = jnp.broadcast_to(b_ref[half], (CHUNK, FFN_TILE))
    for k in range(FFN_CONV_WIDTH):
        s = FFN_CONV_WIDTH - 1 - k
        sh = win if s == 0 else pltpu.roll(win, s, 0)
        acc = acc + w_ref[half, k:k + 1, :] * sh[FFN_PAD:FFN_PAD + CHUNK, :]
    return acc


def _ffn_up_act(n, w_up_t, w3, b3, *, name):
    lp = n.shape[0]
    nch, nt = lp // CHUNK, D_FF // FFN_TILE
    nt_dims = (((1,), (1,)), ((), ()))

    def project(n_ref, wg_ref, wv_ref, u):
        for half, w_ref in ((0, wg_ref), (1, wv_ref)):
            u[half, FFN_PAD:FFN_PAD + lp, :] = lax.dot_general(n_ref[...], w_ref[...], nt_dims,
                                                               preferred_element_type=F32)

    def activate(u, w_ref, b_ref, act_ref, act_t_ref, gv_ref, up_ref):
        for ci in range(nch):
            r0 = ci * CHUNK
            rows = slice(r0, r0 + CHUNK)
            gate = _ffn_conv(u, w_ref, b_ref, 0, r0)
            val = _ffn_conv(u, w_ref, b_ref, 1, r0)
            gv_ref[0, rows, :] = gate.astype(BF16)
            gv_ref[1, rows, :] = val.astype(BF16)
            for half in range(2):
                up_ref[half, rows, :] = u[half, FFN_PAD + r0:FFN_PAD + r0 + CHUNK, :].astype(BF16)
            act = (gate * _sigmoid(gate) * val).astype(BF16)
            act_ref[rows, :] = act
            act_t_ref[:, rows] = act.T

    def body(n_ref, wg_ref, wv_ref, wg_next, wv_next, w_ref, b_ref, act_ref, act_t_ref, gv_ref, up_ref, u0, u1):
        j = pl.program_id(0)

        @pl.when(j == 0)
        def _():
            for u in (u0, u1):
                u[:, 0:FFN_PAD, :] = jnp.zeros((2, FFN_PAD, FFN_TILE), F32)
            project(n_ref, wg_ref, wv_ref, u0)

        for parity, (mine, other) in enumerate(((u0, u1), (u1, u0))):
            @pl.when(j % 2 == parity)
            def _(mine=mine, other=other):
                project(n_ref, wg_next, wv_next, other)
                activate(mine, w_ref, b_ref, act_ref, act_t_ref, gv_ref, up_ref)

    halves = pl.BlockSpec((2, lp, FFN_TILE), lambda j: (0, 0, j))
    rows_of = lambda half, ahead: pl.BlockSpec(
        (FFN_TILE, D_MODEL), lambda j: (half * nt + jnp.minimum(j + ahead, nt - 1), 0))
    return pl.pallas_call(
        body, name=name, grid=(nt,),
        in_specs=[pl.BlockSpec((lp, D_MODEL), lambda j: (0, 0)), rows_of(0, 0), rows_of(1, 0), rows_of(0, 1),
                  rows_of(1, 1), pl.BlockSpec((2, FFN_CONV_WIDTH, FFN_TILE), lambda j: (0, 0, j)),
                  pl.BlockSpec((2, 1, FFN_TILE), lambda j: (0, 0, j))],
        out_specs=[pl.BlockSpec((lp, FFN_TILE), lambda j: (0, j)), pl.BlockSpec((FFN_TILE, lp), lambda j: (j, 0)),
                   halves, halves],
        out_shape=[jax.ShapeDtypeStruct((lp, D_FF), BF16), jax.ShapeDtypeStruct((D_FF, lp), BF16),
                   jax.ShapeDtypeStruct((2, lp, D_FF), BF16), jax.ShapeDtypeStruct((2, lp, D_FF), BF16)],
        scratch_shapes=[pltpu.VMEM((2, lp + FFN_PAD, FFN_TILE), F32), pltpu.VMEM((2, lp + FFN_PAD, FFN_TILE), F32)],
    )(n, w_up_t, w_up_t, w_up_t, w_up_t, w3, b3)


def _ffn_act_bwd(up3, gv3, w3, dact, *, name):
    _, lp, f = up3.shape
    nch = lp // CHUNK
    wlen = CHUNK + FFN_PAD

    def body(up_ref, gv_ref, w_ref, dact_ref, dup_ref, dw_ref, db_ref, ds, wacc):
        for half in range(2):
            ds[half, lp:lp + FFN_PAD, :] = jnp.zeros((FFN_PAD, FFN_TILE), F32)
        wacc[...] = jnp.zeros_like(wacc)

        def act_bwd(ci, _):
            rows = pl.ds(pl.multiple_of(ci * CHUNK, CHUNK), CHUNK)
            gate, val = gv_ref[0, rows, :].astype(F32), gv_ref[1, rows, :].astype(F32)
            sg = _sigmoid(gate)
            da = dact_ref[rows, :].astype(F32)
            ds[0, rows, :] = da * val * (sg * (1.0 + gate * (1.0 - sg)))
            ds[1, rows, :] = da * (gate * sg)
            return 0

        lax.fori_loop(0, nch, act_bwd, 0, unroll=True)

        def conv_bwd(ci, _):
            r0 = pl.multiple_of(ci * CHUNK, CHUNK)
            rows = pl.ds(r0, CHUNK)
            for half in range(2):
                dwin = ds[half, pl.ds(r0, wlen), :]
                x = up_ref[half, rows, :].astype(F32)
                acc = jnp.zeros((CHUNK, FFN_TILE), F32)
                for k in range(FFN_CONV_WIDTH):
                    s = FFN_CONV_WIDTH - 1 - k
                    d_s = (dwin if s == 0 else pltpu.roll(dwin, wlen - s, 0))[0:CHUNK, :]
                    acc = acc + w_ref[half, k:k + 1, :] * d_s
                    wacc[half, 8 * k:8 * k + 8, :] += _fold8(d_s * x)
                wacc[half, 24:32, :] += _fold8(dwin[0:CHUNK, :])
                dup_ref[half, rows, :] = acc.astype(BF16)
            return 0

        lax.fori_loop(0, nch, conv_bwd, 0)
        for half in range(2):
            for k in range(FFN_CONV_WIDTH):
                dw_ref[half, k:k + 1, :] = jnp.sum(wacc[half, 8 * k:8 * k + 8, :], axis=0, keepdims=True)
            db_ref[half] = jnp.sum(wacc[half, 24:32, :], axis=0, keepdims=True)

    halves = pl.BlockSpec((2, lp, FFN_TILE), lambda j: (0, 0, j))
    taps = pl.BlockSpec((2, FFN_CONV_WIDTH, FFN_TILE), lambda j: (0, 0, j))
    bias = pl.BlockSpec((2, 1, FFN_TILE), lambda j: (0, 0, j))
    return pl.pallas_call(
        body, name=name, grid=(f // FFN_TILE,),
        in_specs=[halves, halves, taps, pl.BlockSpec((lp, FFN_TILE), lambda j: (0, j))],
        out_specs=[halves, taps, bias],
        out_shape=[jax.ShapeDtypeStruct((2, lp, f), BF16), jax.ShapeDtypeStruct((2, FFN_CONV_WIDTH, f), F32),
                   jax.ShapeDtypeStruct((2, 1, f), F32)],
        scratch_shapes=[pltpu.VMEM((2, lp + FFN_PAD, FFN_TILE), F32), pltpu.VMEM((2, 32, FFN_TILE), F32)],
    )(up3, gv3, w3, dact)


POOL_PAD = 16


def _inv_count(r0, w):
    t = r0 + lax.broadcasted_iota(jnp.int32, (CHUNK, 1), 0)
    return 1.0 / jnp.minimum(t + 1, w).astype(F32)


def _pool_fwd(n, pool_w, pool_b, pool_scale, h, *, name):
    lp, dm = n.shape
    nch = lp // CHUNK
    g = POOL_GROUP

    def body(n_ref, w_ref, b_ref, s_ref, h_ref, ho_ref, dt_ref, z_ref, xs, d_ref):
        gi = pl.program_id(0)
        xs[0:POOL_PAD, :] = jnp.zeros((POOL_PAD, g), F32)
        xs[POOL_PAD:POOL_PAD + lp, :] = n_ref[...]
        for idx, w in enumerate(POOL_WINDOWS):
            @pl.when(gi == idx)
            def _(w=w):
                def chunk(ci, _):
                    r0 = pl.multiple_of(ci * CHUNK, CHUNK)
                    win = xs[pl.ds(r0, CHUNK + POOL_PAD), :]
                    acc = win
                    for j in range(1, w):
                        acc = acc + pltpu.roll(win, j, 0)
                    x = win[POOL_PAD:POOL_PAD + CHUNK, :]
                    d = acc[POOL_PAD:POOL_PAD + CHUNK, :] * _inv_count(r0, w) - x
                    d_ref[pl.ds(r0, CHUNK), :] = d.astype(BF16)
                    dt_ref[:, pl.ds(r0, CHUNK)] = d.astype(BF16).T
                    return 0

                lax.fori_loop(0, nch, chunk, 0, unroll=True)

        z = jnp.dot(d_ref[...], w_ref[...], preferred_element_type=F32) + b_ref[...]
        z_ref[...] = z
        ho_ref[...] = h_ref[...] + z * s_ref[...]

    col = pl.BlockSpec((lp, g), lambda i: (0, i))
    vec = pl.BlockSpec((1, g), lambda i: (0, i))
    return pl.pallas_call(
        body, name=name, grid=(len(POOL_WINDOWS),),
        in_specs=[col, pl.BlockSpec((None, g, g), lambda i: (i, 0, 0)), vec, vec, col],
        out_specs=[col, pl.BlockSpec((g, lp), lambda i: (i, 0)), col],
        out_shape=[jax.ShapeDtypeStruct((lp, dm), F32), jax.ShapeDtypeStruct((dm, lp), BF16),
                   jax.ShapeDtypeStruct((lp, dm), F32)],
        scratch_shapes=[pltpu.VMEM((lp + POOL_PAD, g), F32), pltpu.VMEM((lp, g), BF16)],
    )(n, pool_w, pool_b, pool_scale, h)


def _pool_bwd(dy, z, pool_w, pool_scale, *, name):
    lp, dm = dy.shape
    nch = lp // CHUNK
    g = POOL_GROUP
    wlen = CHUNK + POOL_PAD

    def body(dy_ref, z_ref, w_ref, s_ref, dn_ref, dz_ref, dsc_ref, db_ref, ys, dd):
        gi = pl.program_id(0)
        dyv = dy_ref[...]
        dsc_ref[...] = jnp.sum(dyv * z_ref[...], axis=0, keepdims=True)
        dz = dyv * s_ref[...]
        db_ref[...] = jnp.sum(dz, axis=0, keepdims=True)
        dzb = dz.astype(BF16)
        dz_ref[...] = dzb
        dd[...] = lax.dot_general(dzb, w_ref[...], (((1,), (1,)), ((), ())), preferred_element_type=F32)
        ys[lp:lp + POOL_PAD, :] = jnp.zeros((POOL_PAD, g), F32)
        for idx, w in enumerate(POOL_WINDOWS):
            @pl.when(gi == idx)
            def _(w=w):
                def scale(ci, _):
                    r0 = pl.multiple_of(ci * CHUNK, CHUNK)
                    ys[pl.ds(r0, CHUNK), :] = dd[pl.ds(r0, CHUNK), :] * _inv_count(r0, w)
                    return 0

                lax.fori_loop(0, nch, scale, 0, unroll=True)

                def chunk(ci, _):
                    r0 = pl.multiple_of(ci * CHUNK, CHUNK)
                    win = ys[pl.ds(r0, wlen), :]
                    acc = win
                    for j in range(1, w):
                        acc = acc + pltpu.roll(win, wlen - j, 0)
                    dn_ref[pl.ds(r0, CHUNK), :] = acc[0:CHUNK, :] - dd[pl.ds(r0, CHUNK), :]
                    return 0

                lax.fori_loop(0, nch, chunk, 0, unroll=True)

    col = pl.BlockSpec((lp, g), lambda i: (0, i))
    vec = pl.BlockSpec((1, g), lambda i: (0, i))
    return pl.pallas_call(
        body, name=name, grid=(len(POOL_WINDOWS),),
        in_specs=[col, col, pl.BlockSpec((None, g, g), lambda i: (i, 0, 0)), vec],
        out_specs=[col, col, vec, vec],
        out_shape=[jax.ShapeDtypeStruct((lp, dm), F32), jax.ShapeDtypeStruct((lp, dm), BF16),
                   jax.ShapeDtypeStruct((1, dm), F32), jax.ShapeDtypeStruct((1, dm), F32)],
        scratch_shapes=[pltpu.VMEM((lp + POOL_PAD, g), F32), pltpu.VMEM((lp, g), F32)],
    )(dy, z, pool_w, pool_scale)


def _ffn_fwd(h, g, weight, w3, b3, tag):
    n, n_t = _rms_fwd(h, g, name=f"rms_ffn{tag}", out_dtype=BF16, transposed=True)
    w_up_t = weight("up", n)
    act, act_t, gv3, up3 = _ffn_up_act(n, w_up_t, w3, b3, name=f"ffn_up_act{tag}")
    w_down = weight("down", act)
    h_out = _mm(act, w_down, name=f"mm_down{tag}", tn=256, res=h)
    return h_out, (n_t, up3, gv3, act_t), w_up_t, w_down


def _ffn_bwd(dh, dhb, h, g, saved, w_up_t, w3, w_down, tag, after=None):
    lp = h.shape[0]
    n_t, up3, gv3, act_t = saved
    dw_down = _mm(act_t, dhb, name=f"mm_dwdown{tag}", tm=704, out_dtype=BF16)
    dact = _mm(dhb, w_down, name=f"mm_dact{tag}", tb=True, tn=D_FF // 2, out_dtype=BF16, after=after)
    dup3, dcw, dcb = _ffn_act_bwd(up3, gv3, w3, dact, name=f"ffn_act_bwd{tag}")
    dup2 = dup3.reshape(2 * lp, D_FF)
    dw_up = _mm_dw_up(n_t, dup2, name=f"mm_dwup{tag}")
    dh_in, dh_in_b, dg = _mm_rms_bwd(dup2, w_up_t, h, g, dh, name=f"mm_dnffn{tag}", tk=D_FF // 2,
                                     a_map=lambda i, kk: (4 * (kk // 2) + i, kk % 2))
    return dh_in, dh_in_b, (dg, dw_up, dcw, dcb, dw_down)


def _local_step(x, tgt, wt):
    seq = x.shape[0]
    n_real = N_META + seq
    lp = _round_up(n_real, CHUNK)
    pad = jnp.zeros((lp - n_real, D_MODEL), F32)
    h0 = jnp.concatenate([wt["meta"], x, pad], axis=0)
    tgt_p = jnp.concatenate([jnp.zeros((N_META, D_MODEL), F32), tgt, pad], axis=0)
    w_in_p = wt["w_in_p"]

    n0, n0_t = _rms_fwd(h0, wt["g_even"], name="rms_even", out_dtype=BF16, after=wt["ffn_started"], transposed=True)
    qkv = _mm(n0, w_in_p, name="mm_qkv", tn=512, dims=(lp, 3 * FOX_WIDTH, D_MODEL), out_dtype=BF16)
    ag = _mm(n0, w_in_p, name="mm_ag", tn=512, dims=(lp, 2 * CONV_CH, D_MODEL),
             b_map=lambda i, j, k: (0, 3 + j))
    f_t = _mm(wt["wf_t"], n0, name="mm_ft", tb=True)
    c_row = _fox_prep(f_t, wt["b_f"], name="fox_prep")
    c3 = c_row.reshape(4, 2, lp)
    o = _attn_fwd(qkv, c3, name="attn_fwd")
    u0, u1, u3 = _conv_fwd(ag, wt["conv_w"], wt["conv_b"], wt["ln_g"], wt["ln_b"], name="conv_fwd")
    cat = jnp.concatenate([o, u3], axis=1)
    h1 = _mm(cat, wt["w_out"], name="mm_out", tn=512, res=h0)
    h2, saved0, w_up0, w_down0 = _ffn_fwd(h1, wt["ffn_norm"][0:1], functools.partial(wt["ffn_weight"], 0),
                                          wt["fcw3"][0], wt["fcb3"][0], 0)

    n2 = _rms_fwd(h2, wt["g_odd"], name="rms_odd", out_dtype=F32)
    h3, dpool_t, z = _pool_fwd(n2, wt["pool_w"], wt["pool_b"], wt["pool_scale"], h2, name="pool_fwd")
    h4, saved1, w_up1, w_down1 = _ffn_fwd(h3, wt["ffn_norm"][1:2], functools.partial(wt["ffn_weight"], 1),
                                          wt["fcw3"][1], wt["fcb3"][1], 1)

    loss, dh4, dh4b, d_gfinal = _loss_head(h4, wt["g_final"], tgt_p, n_real, name="loss_head")

    dh3, dh3b, gf1 = _ffn_bwd(dh4, dh4b, h3, wt["ffn_norm"][1:2], saved1, w_up1, wt["fcw3"][1], w_down1, 1)
    send1, token1 = _send_ffn_grads(gf1[1], gf1[4], 1)
    dn2, dzb, d_pscale, d_pb = _pool_bwd(dh3, z, wt["pool_w"], wt["pool_scale"], name="pool_bwd")
    d_pw = _mm(dpool_t, dzb, name="mm_dpoolw", tm=POOL_GROUP, tn=POOL_GROUP, dims=(D_MODEL, POOL_GROUP, lp),
               b_map=lambda i, j, k: (0, i), o_map=lambda i, j, k: (i, 0), out_shape=(D_MODEL, POOL_GROUP),
               out_dtype=BF16)
    dh2, dh2b, d_godd = _rms_bwd(h2, wt["g_odd"], dn2, dh3, name="rms_bwd_odd")
    dh1, dh1b, gf0 = _ffn_bwd(dh2, dh2b, h1, wt["ffn_norm"][0:1], saved0, w_up0, wt["fcw3"][0], w_down0, 0,
                              after=token1)

    send0, token0 = _send_ffn_grads(gf0[1], gf0[4], 0)
    cat_t = _transpose(cat, name="t_cat", out_dtype=BF16)
    d_wout = _mm(cat_t, dh1b, name="mm_dwout", tm=512, out_dtype=BF16)
    dcat = _mm(dh1b, wt["w_out"], name="mm_dcat", tb=True, tn=512, after=token0)
    dag, d_convw, d_convb, d_lng, d_lnb = _conv_bwd(dcat, u0, u1, ag, wt["conv_w"], wt["ln_g"], wt["ln_b"],
                                                    name="conv_bwd")
    layers = lambda i: jnp.stack([gf0[i], gf1[i]])
    grads = dict(
        conv_w=d_convw[None], w_out=d_wout, mix_norm_odd=d_godd,
        pool_w=d_pw.reshape(len(POOL_WINDOWS), POOL_GROUP, POOL_GROUP),
        pool_b=d_pb.reshape(1, len(POOL_WINDOWS), POOL_GROUP), pool_scale=d_pscale, w_up=(gf0[1], gf1[1]),
        ffn_conv_w=layers(2).transpose(0, 2, 1, 3).reshape(DEPTH, FFN_CONV_WIDTH, 2 * D_FF), w_down=(gf0[4], gf1[4]))
    send_rest, token_rest, grads["little_slabs"] = _send_rest_grads(grads)
    q_t = _transpose(qkv, name="t_q", out_dtype=BF16, cols=FOX_WIDTH, after=token_rest)
    do_t = _transpose(dcat, name="t_do", out_dtype=BF16, cols=FOX_WIDTH)
    dq, dk_t, dv_t, dc3 = _attn_bwd(qkv, q_t, dcat, do_t, c3, name="attn_bwd")
    dk = _transpose(dk_t, name="t_dk", out_dtype=BF16)
    dv = _transpose(dv_t, name="t_dv", out_dtype=BF16)
    df_t, d_bf = _fox_bwd(dc3.reshape(FOX_HEADS, lp), f_t, wt["b_f"], name="fox_bwd")
    df = _transpose(df_t, name="t_df", out_dtype=BF16)
    dproj = jnp.concatenate([dq, dk, dv, dag, df], axis=1)
    grads["w_in"] = _mm_dw_in(n0_t, dproj, name="mm_dwin")
    send_in, token_in = _send_start(
        [grads["w_in"]], [jax.ShapeDtypeStruct((N_DEV - 1, D_MODEL, _IN_SHARD), BF16)], [(0, _by_owner, 0, None)],
        name="send_w_in")
    dh0, _, d_geven = _mm_rms_bwd(dproj, w_in_p, h0, wt["g_even"], dh1, name="mm_dn0", tb=True, tk=896,
                                  after=token_in)
    grads.update(
        meta_tokens=dh0[0:N_META], mix_norm_even=d_geven, b_f=d_bf.reshape(1, FOX_HEADS), conv_b=d_convb, ln_g=d_lng,
        ln_b=d_lnb, ffn_norm=jnp.concatenate([gf0[0], gf1[0]], axis=0),
        ffn_conv_b=layers(3).reshape(DEPTH, 2 * D_FF), final_norm=d_gfinal.reshape(D_MODEL),
        sends=(send0, send1, send_rest, send_in))
    return loss, dh0[N_META:n_real], grads


_LITTLE = (("conv_w", (1, 31, 512), 2), ("mix_norm_odd", (1, 1024), 1), ("pool_b", (1, 4, 256), 2),
           ("pool_scale", (1, 1024), 1), ("ffn_conv_w", (2, 3, 5632), 2))


def _send_rest_grads(g):
    out_rows, pool_rows, groups = D_MODEL // N_DEV, POOL_GROUP // N_DEV, len(POOL_WINDOWS)
    little_slabs = _pack([_full_to_slabs(g[n], s, a) for n, s, a in _LITTLE], F32, lead=(N_DEV,), align=8)
    land = lambda shape, dtype: jax.ShapeDtypeStruct((N_DEV - 1,) + shape, dtype)
    handle, token = _send_start(
        [g["w_out"], g["pool_w"], little_slabs],
        [land((out_rows, D_MODEL), BF16), land((groups, pool_rows, POOL_GROUP), BF16), land(little_slabs.shape[1:], F32)],
        [(0, _row_block(out_rows), 0, None), (1, _row_block(pool_rows, axis=1), 1, None), (2, _by_owner, 2, None)],
        name="send_rest")
    return handle, token, little_slabs


def _send_ffn_grads(dw_up, dw_down, tag):
    rows = D_FF // N_DEV
    lands = [jax.ShapeDtypeStruct((N_DEV - 1,) + dw_up.shape[1:], BF16),
             jax.ShapeDtypeStruct((N_DEV - 1, rows, D_MODEL), BF16)]
    return _send_start([dw_up, dw_down], lands, [(0, _by_owner, 0, None), (1, _row_block(rows), 1, None)],
                       name=f"send_ffn{tag}")


_QKV = 3 * FOX_WIDTH
_GLU0 = _QKV + FOX_HEADS
_IN_COLS = _GLU0 + 2 * CONV_CH
_F_PAD = 128


_IN_SHARD = _IN_COLS // N_DEV
_UP_SHARD = 2 * D_FF // N_DEV
_ROW_TILE = 256


def _assemble_w_in(st, *, name):
    tr = _ROW_TILE

    def body(s_ref, o_ref):
        full = jnp.concatenate([s_ref[i].astype(F32) for i in range(N_DEV)], axis=1)
        parts = [full[:, :_QKV], full[:, _GLU0:], full[:, _QKV:_GLU0], jnp.zeros((tr, _F_PAD - FOX_HEADS), F32)]
        o_ref[...] = jnp.concatenate(parts, axis=1).astype(BF16)

    return pl.pallas_call(
        body, name=name, grid=(D_MODEL // tr,),
        in_specs=[pl.BlockSpec((N_DEV, tr, _IN_SHARD), lambda i: (0, i, 0))],
        out_specs=pl.BlockSpec((tr, _QKV + 2 * CONV_CH + _F_PAD), lambda i: (i, 0)),
        out_shape=jax.ShapeDtypeStruct((D_MODEL, _QKV + 2 * CONV_CH + _F_PAD), BF16),
    )(st)


def _mm_dw_in(n_t, dproj, *, name):
    dm, lp = n_t.shape
    tr = _ROW_TILE
    ag0 = _QKV + 2 * CONV_CH

    def body(a_ref, b_ref, o_ref):
        r = jnp.dot(a_ref[...], b_ref[...], preferred_element_type=F32)
        full = jnp.concatenate([r[:, :_QKV], r[:, ag0:ag0 + FOX_HEADS], r[:, _QKV:ag0]], axis=1)
        for i in range(N_DEV):
            o_ref[i] = full[:, i * _IN_SHARD:(i + 1) * _IN_SHARD].astype(BF16)

    return pl.pallas_call(
        body, name=name, grid=(dm // tr,),
        in_specs=[pl.BlockSpec((tr, lp), lambda i: (i, 0)), pl.BlockSpec(dproj.shape, lambda i: (0, 0))],
        out_specs=pl.BlockSpec((N_DEV, tr, _IN_SHARD), lambda i: (0, i, 0)),
        out_shape=jax.ShapeDtypeStruct((N_DEV, dm, _IN_SHARD), BF16),
    )(n_t, dproj)


def _mm_dw_up(n_t, dup2, *, name):
    dm, lp = n_t.shape
    pairs_per_half = D_FF // (2 * _UP_SHARD)

    def body(a_ref, b_ref, o_ref):
        r_t = jnp.dot(a_ref[...], b_ref[...], preferred_element_type=F32).T
        o_ref[0] = r_t[:_UP_SHARD, :].astype(BF16)
        o_ref[1] = r_t[_UP_SHARD:, :].astype(BF16)

    return pl.pallas_call(
        body, name=name, grid=(N_DEV // 2,),
        in_specs=[pl.BlockSpec((dm, lp), lambda p: (0, 0)),
                  pl.BlockSpec((lp, 2 * _UP_SHARD), lambda p: (p // pairs_per_half, p % pairs_per_half))],
        out_specs=pl.BlockSpec((2, _UP_SHARD, dm), lambda p: (p, 0, 0)),
        out_shape=jax.ShapeDtypeStruct((N_DEV, _UP_SHARD, dm), BF16),
    )(n_t, dup2)


MESH = pl.DeviceIdType.MESH
ANY = pl.BlockSpec(memory_space=pl.ANY)


def _slot(px, py, pc):
    return 4 * px + 2 * py + pc


def _by_owner(ref, slot):
    return ref.at[slot]


def _row_block(rows, axis=0):
    def place(ref, slot):
        idx = (slice(None),) * axis + (pl.ds(slot * rows, rows),)
        return ref.at[idx]
    return place


def _all_gather(arrs, out_shapes, places, *, name):
    n = len(arrs)

    def body(*refs):
        ins, outs = refs[:n], refs[n:2 * n]
        send_sems, recv_sems, local_sems = refs[2 * n:]
        x, y, c = lax.axis_index("x"), lax.axis_index("y"), lax.axis_index("c")
        me, sibling = (x, y, c), (x, y, 1 - c)
        chips = [(1 - x, y), (x, 1 - y), (1 - x, 1 - y)]

        def copy(a, k, block, to, from_input=False):
            dst = places[a](outs[a], _slot(*block))
            return pltpu.make_async_remote_copy(
                src_ref=ins[a] if from_input else dst, dst_ref=dst,
                send_sem=send_sems.at[7 * a + k], recv_sem=recv_sems.at[7 * a + k],
                device_id=to, device_id_type=MESH)

        own, sent = [], []
        for a in range(n):
            mine = pltpu.make_async_copy(ins[a], places[a](outs[a], _slot(*me)), local_sems.at[a])
            mine.start()
            own.append(mine)
            first = [copy(a, 0, me, sibling, True)]
            first += [copy(a, 1 + j, me, (*chip, c), True) for j, chip in enumerate(chips)]
            for cp in first:
                cp.start()
            sent += first
        for a in range(n):
            for j, chip in enumerate(chips):
                copy(a, 1 + j, (*chip, c), me).wait_recv()
                passed = copy(a, 4 + j, (*chip, c), sibling)
                passed.start()
                sent.append(passed)
        for a in range(n):
            copy(a, 0, sibling, me).wait_recv()
            for j, chip in enumerate(chips):
                copy(a, 4 + j, (*chip, 1 - c), me).wait_recv()
        for cp in sent:
            cp.wait_send()
        for cp in own:
            cp.wait()

    return pl.pallas_call(
        body, name=name,
        in_specs=[ANY] * n, out_specs=[ANY] * n,
        out_shape=[jax.ShapeDtypeStruct(s, a.dtype) for s, a in zip(out_shapes, arrs)],
        scratch_shapes=[pltpu.SemaphoreType.DMA((7 * n,)), pltpu.SemaphoreType.DMA((7 * n,)),
                        pltpu.SemaphoreType.DMA((n,))],
    )(*arrs)


HBM = pl.BlockSpec(memory_space=pltpu.HBM)
SEM = pl.BlockSpec(memory_space=pltpu.SEMAPHORE)
EFFECT = pltpu.SideEffectType.DATAFLOW_SIDE_EFFECTING


def _relation_copies(src_refs, land_refs, copies, send_sems, recv_sems):
    x, y, c = lax.axis_index("x"), lax.axis_index("y"), lax.axis_index("c")
    flip = lambda v, bit: 1 - v if bit else v
    out = []
    for k in range(1, N_DEV):
        p = (flip(x, k & 4), flip(y, k & 2), flip(c, k & 1))
        for j, (si, take, li, put) in enumerate(copies):
            sem = (k - 1) * len(copies) + j
            dst = land_refs[li].at[k - 1] if put is None else put(land_refs[li], _slot(x, y, c))
            out.append(pltpu.make_async_remote_copy(
                src_ref=take(src_refs[si], _slot(*p)), dst_ref=dst,
                send_sem=send_sems.at[sem], recv_sem=recv_sems.at[sem], device_id=p, device_id_type=MESH))
    return out


def _own_copies(src_refs, land_refs, copies, sems):
    me = _slot(lax.axis_index("x"), lax.axis_index("y"), lax.axis_index("c"))
    placed = [(si, take, li, put) for si, take, li, put in copies if put is not None]
    return [pltpu.make_async_copy(take(src_refs[si], me), put(land_refs[li], me),
                                  sems.at[(N_DEV - 1) * len(copies) + j])
            for j, (si, take, li, put) in enumerate(placed)]


def _send_start(srcs, land_structs, copies, *, name, after=None):
    ns, nl = len(srcs), len(land_structs)
    n_sem = (N_DEV - 1) * len(copies) + sum(put is not None for _, _, _, put in copies)
    behind = [] if after is None else [after]

    def body(*refs):
        first_out = ns + nl + len(behind)
        send_sems, recv_sems, token = refs[first_out], refs[first_out + 1], refs[-1]
        for cp in _relation_copies(refs[:ns], refs[ns:ns + nl], copies, send_sems, recv_sems):
            cp.start()
        for cp in _own_copies(refs[:ns], refs[ns:ns + nl], copies, send_sems):
            cp.start()
        token[...] = jnp.zeros_like(token)

    in_hbm = lambda a: pltpu.with_memory_space_constraint(a, pltpu.HBM)
    outs = pl.pallas_call(
        body, name=name,
        out_shape=(pltpu.SemaphoreType.DMA((n_sem,)), pltpu.SemaphoreType.DMA((n_sem,)),
                   *[pltpu.HBM(s.shape, s.dtype) for s in srcs],
                   *[pltpu.HBM(s.shape, s.dtype) for s in land_structs],
                   jax.ShapeDtypeStruct((8, 128), F32)),
        in_specs=(HBM,) * (ns + nl) + (ANY,) * len(behind),
        out_specs=(SEM, SEM) + (HBM,) * (ns + nl) + (pl.BlockSpec(memory_space=pltpu.VMEM),),
        input_output_aliases={i: 2 + i for i in range(ns + nl)},
        compiler_params=pltpu.CompilerParams(has_side_effects=EFFECT),
    )(*[in_hbm(s) for s in srcs], *[in_hbm(lax.empty(s.shape, s.dtype)) for s in land_structs], *behind)
    return (outs[0], outs[1], outs[2:2 + ns], outs[2 + ns:2 + ns + nl], copies), outs[-1]


def _send_wait(handle, after, *, name):
    send_sems, recv_sems, srcs, lands, copies = handle
    ns, nl = len(srcs), len(lands)

    def body(*refs):
        for cp in _relation_copies(refs[:ns], refs[ns:ns + nl], copies, refs[ns + nl], refs[ns + nl + 1]):
            cp.wait_send()
            cp.wait_recv()
        for cp in _own_copies(refs[:ns], refs[ns:ns + nl], copies, refs[ns + nl]):
            cp.wait()

    outs = pl.pallas_call(
        body, name=name,
        out_shape=tuple(pltpu.HBM(a.shape, a.dtype) for a in (*srcs, *lands)),
        in_specs=(HBM,) * (ns + nl) + (SEM, SEM, ANY), out_specs=(HBM,) * (ns + nl),
        input_output_aliases={i: i for i in range(ns + nl)},
        compiler_params=pltpu.CompilerParams(has_side_effects=EFFECT),
    )(*srcs, *lands, send_sems, recv_sems, after)
    return outs[:ns], outs[ns:]


def _sum_slabs(stack, *, name, own=None):
    n, rows, w = stack.shape

    def body(*refs):
        s_ref, o_ref = refs[-2], refs[-1]
        acc = s_ref[0] if own is None else refs[0][...] + s_ref[0]
        for i in range(1, n):
            acc = acc + s_ref[i]
        o_ref[...] = acc

    return pl.pallas_call(body, name=name, out_shape=jax.ShapeDtypeStruct((rows, w), F32))(
        *([] if own is None else [own]), stack)


def _adam_math(w, g, m, v):
    mn = ADAM_B1 * m + (1.0 - ADAM_B1) * g
    vn = ADAM_B2 * v + (1.0 - ADAM_B2) * (g * g)
    m_hat = mn / (1.0 - ADAM_B1 ** ADAM_STEP)
    v_hat = vn / (1.0 - ADAM_B2 ** ADAM_STEP)
    return -ADAM_LR * (m_hat / (jnp.sqrt(v_hat) + ADAM_EPS) + ADAM_WD * w), mn, vn


def _adamw_many(ws, gs, ms, vs, *, name):
    n = len(ws)

    def body(*refs):
        for i in range(n):
            w_ref, g_ref, m_ref, v_ref = (refs[j * n + i] for j in range(4))
            d_ref, mo_ref, vo_ref = refs[4 * n + 3 * i:4 * n + 3 * i + 3]
            d_ref[...], mo_ref[...], vo_ref[...] = _adam_math(w_ref[...], g_ref[...], m_ref[...], v_ref[...])

    return pl.pallas_call(
        body, name=name, out_shape=[jax.ShapeDtypeStruct(w.shape, F32) for w in ws for _ in range(3)],
    )(*ws, *gs, *ms, *vs)


def _adamw_layers(w, owns, lands, m, v, tr, *, name):
    nl, rows, cols = w.shape
    steps = rows // tr
    assert rows % tr == 0

    def body(*refs):
        w_ref, m_ref, v_ref = refs[:3]
        own_refs, land_refs = refs[3:3 + nl], refs[3 + nl:3 + 2 * nl]
        g_ref, d_ref, mo_ref, vo_ref = refs[3 + 2 * nl:]
        for li in range(nl):
            @pl.when(pl.program_id(0) == li)
            def _(li=li):
                g = own_refs[li][...].astype(F32)
                for k in range(N_DEV - 1):
                    g = g + land_refs[li][k].astype(F32)
                g_ref[...] = g
                d_ref[...], mo_ref[...], vo_ref[...] = _adam_math(w_ref[...], g, m_ref[...], v_ref[...])

    def held(li):
        return lambda l, i: jnp.where(l == li, i, jnp.where(l < li, 0, steps - 1))

    blk = pl.BlockSpec((None, tr, cols), lambda l, i: (l, i, 0))
    own_specs = [pl.BlockSpec((tr, cols), lambda l, i, f=held(li): (f(l, i), 0)) for li in range(nl)]
    land_specs = [pl.BlockSpec((N_DEV - 1, tr, cols), lambda l, i, f=held(li): (0, f(l, i), 0)) for li in range(nl)]
    return pl.pallas_call(
        body, name=name, grid=(nl, steps),
        in_specs=[blk, blk, blk] + own_specs + land_specs, out_specs=[blk] * 4,
        out_shape=[jax.ShapeDtypeStruct(w.shape, F32)] * 4,
    )(w, m, v, *owns, *lands)


_WEIGHTS = (
    ("meta_tokens", (16, 1024), 1), ("mix_norm_even", (1, 1024), None), ("w_in", (1, 1024, 2568), 2),
    ("b_f", (1, 8), None), ("conv_w", (1, 31, 512), 2), ("conv_b", (1, 512), None), ("ln_g", (1, 512), None),
    ("ln_b", (1, 512), None), ("w_out", (1, 1024, 1024), 1), ("mix_norm_odd", (1, 1024), 1),
    ("pool_w", (1, 4, 256, 256), 2), ("pool_b", (1, 4, 256), 2), ("pool_scale", (1, 1024), 1),
    ("ffn_norm", (2, 1024), None), ("w_up", (2, 1024, 5632), 2), ("ffn_conv_w", (2, 3, 5632), 2),
    ("ffn_conv_b", (2, 5632), None), ("w_down", (2, 2816, 1024), 1), ("final_norm", (1024,), None),
)
_MATMUL_WEIGHTS = ("w_in", "w_out", "pool_w", "w_up", "w_down")
_ADAM_ROWS = dict(w_in=256, w_out=128, pool_w=128, w_up=352, w_down=352)


def _shard_shape(shape, axis):
    return shape[:axis] + (shape[axis] // N_DEV,) + shape[axis + 1:]


def _size(shape):
    n = 1
    for s in shape:
        n *= s
    return n


def _pack(parts, dtype, lead=(), align=16):
    flat = jnp.concatenate([p.reshape(lead + (-1,)).astype(dtype) for p in parts], axis=-1)
    n = flat.shape[-1]
    rows = _round_up(-(-n // FLAT_W), align)
    flat = jnp.pad(flat, [(0, 0)] * len(lead) + [(0, rows * FLAT_W - n)])
    return flat.reshape(lead + (rows, FLAT_W))


def _unpack(buf, shapes, lead=()):
    flat = buf.reshape(lead + (-1,))
    out, off = [], 0
    for shp in shapes:
        n = _size(shp)
        out.append(flat[..., off:off + n].reshape(lead + shp))
        off += n
    return out


def _gathered_to_full(stack, shape, axis):
    return jnp.moveaxis(stack, 0, axis).reshape(shape)


def _full_to_slabs(full, shape, axis):
    split = shape[:axis] + (N_DEV, shape[axis] // N_DEV) + shape[axis + 1:]
    return jnp.moveaxis(full.reshape(split), axis, 0)


def kernel(x, meta_tokens, mix_norm_even, w_in, b_f, conv_w, conv_b, ln_g, ln_b, w_out, mix_norm_odd, pool_w, pool_b, pool_scale, ffn_norm, w_up, ffn_conv_w, ffn_conv_b, w_down, final_norm, loss_target, m_meta_tokens, m_mix_norm_even, m_w_in, m_b_f, m_conv_w, m_conv_b, m_ln_g, m_ln_b, m_w_out, m_mix_norm_odd, m_pool_w, m_pool_b, m_pool_scale, m_ffn_norm, m_w_up, m_ffn_conv_w, m_ffn_conv_b, m_w_down, m_final_norm, v_meta_tokens, v_mix_norm_even, v_w_in, v_b_f, v_conv_w, v_conv_b, v_ln_g, v_ln_b, v_w_out, v_mix_norm_odd, v_pool_w, v_pool_b, v_pool_scale, v_ffn_norm, v_w_up, v_ffn_conv_w, v_ffn_conv_b, v_w_down, v_final_norm):
    names = [n for n, _, _ in _WEIGHTS]
    w_loc = dict(zip(names, (meta_tokens, mix_norm_even, w_in, b_f, conv_w, conv_b, ln_g, ln_b, w_out, mix_norm_odd,
                             pool_w, pool_b, pool_scale, ffn_norm, w_up, ffn_conv_w, ffn_conv_b, w_down, final_norm)))
    m_loc = dict(zip(names, (m_meta_tokens, m_mix_norm_even, m_w_in, m_b_f, m_conv_w, m_conv_b, m_ln_g, m_ln_b,
                             m_w_out, m_mix_norm_odd, m_pool_w, m_pool_b, m_pool_scale, m_ffn_norm, m_w_up,
                             m_ffn_conv_w, m_ffn_conv_b, m_w_down, m_final_norm)))
    v_loc = dict(zip(names, (v_meta_tokens, v_mix_norm_even, v_w_in, v_b_f, v_conv_w, v_conv_b, v_ln_g, v_ln_b,
                             v_w_out, v_mix_norm_odd, v_pool_w, v_pool_b, v_pool_scale, v_ffn_norm, v_w_up,
                             v_ffn_conv_w, v_ffn_conv_b, v_w_down, v_final_norm)))
    replicated = [(n, s) for n, s, a in _WEIGHTS if a is None]
    little = [(n, s, a) for n, s, a in _WEIGHTS if a is not None and n not in _MATMUL_WEIGHTS]
    little_shards = [_shard_shape(s, a) for _, s, a in little]
    out_rows, down_rows, pool_rows = D_MODEL // N_DEV, D_FF // N_DEV, POOL_GROUP // N_DEV
    n_groups = len(POOL_WINDOWS)

    little_pack = _pack([w_loc[n] for n, _, _ in little], F32)
    g_win, g_wout, g_poolw, g_little = _all_gather(
        [w_in[0].astype(BF16), w_out[0].astype(BF16), pool_w[0].astype(BF16), little_pack],
        [(N_DEV, D_MODEL, _IN_SHARD), (D_MODEL, D_MODEL), (n_groups, POOL_GROUP, POOL_GROUP),
         (N_DEV,) + little_pack.shape],
        [_by_owner, _row_block(out_rows), _row_block(pool_rows, axis=1), _by_owner],
        name="gather_weights")
    me = _slot(lax.axis_index("x"), lax.axis_index("y"), lax.axis_index("c"))
    up_t = lambda a: jnp.transpose(a, (0, 2, 1))
    w_loc["w_up"], m_loc["w_up"], v_loc["w_up"] = up_t(w_up), up_t(m_w_up), up_t(v_w_up)
    w_up_b, w_down_b = w_loc["w_up"].astype(BF16), w_down.astype(BF16)
    whole = lambda ref, slot: ref
    ffn_gathers, behind = {}, g_little
    for l in range(DEPTH):
        for part, shard, rows in (("up", w_up_b[l], _UP_SHARD), ("down", w_down_b[l], down_rows)):
            ffn_gathers[l, part], behind = _send_start(
                [shard], [jax.ShapeDtypeStruct((N_DEV * rows, D_MODEL), BF16)], [(0, whole, 0, _row_block(rows))],
                name=f"gather_{part}{l}_start", after=behind)

    def ffn_weight(l, part, after):
        return _send_wait(ffn_gathers[l, part], after, name=f"gather_{part}{l}_wait")[1][0]

    w_in_p = _assemble_w_in(g_win, name="assemble_w_in")
    full = {n: _gathered_to_full(st, s, a)
            for (n, s, a), st in zip(little, _unpack(g_little, little_shards, lead=(N_DEV,)))}
    f0 = _QKV + 2 * CONV_CH
    wt = dict(
        meta=full["meta_tokens"], g_even=mix_norm_even, w_in_p=w_in_p, wf_t=w_in_p[:, f0:f0 + FOX_HEADS].T,
        b_f=b_f.reshape(FOX_HEADS, 1), conv_w=full["conv_w"][0], conv_b=conv_b, ln_g=ln_g, ln_b=ln_b, w_out=g_wout,
        g_odd=full["mix_norm_odd"], pool_w=g_poolw, pool_b=full["pool_b"].reshape(1, D_MODEL),
        pool_scale=full["pool_scale"], ffn_norm=ffn_norm, ffn_weight=ffn_weight, ffn_started=behind,
        fcw3=full["ffn_conv_w"].reshape(DEPTH, FFN_CONV_WIDTH, 2, D_FF).transpose(0, 2, 1, 3),
        fcb3=ffn_conv_b.reshape(DEPTH, 2, 1, D_FF), g_final=final_norm.reshape(1, D_MODEL))

    loss_part, grad_x, g = _local_step(x[0], loss_target[0], wt)

    small = _pack([loss_part[:, 0:1]] + [g[n] for n, _ in replicated] + [g["meta_tokens"]], F32, align=8)
    send_small, token_small = _send_start([small], [jax.ShapeDtypeStruct((N_DEV,) + small.shape, F32)],
                                          [(0, whole, 0, _by_owner)], name="send_small")

    grads, delta, new_m, new_v = {}, {}, {}, {}
    send0, send1, send_rest, send_in = g["sends"]
    ffn_sent = [_send_wait(send, token_small, name=f"wait_ffn{l}") for l, send in enumerate((send0, send1))]
    own_up = [lax.dynamic_index_in_dim(srcs[0], me, 0, keepdims=False) for srcs, _ in ffn_sent]
    own_down = [lax.dynamic_slice_in_dim(srcs[1], me * down_rows, down_rows, 0) for srcs, _ in ffn_sent]
    for n, owns, idx in (("w_up", own_up, 0), ("w_down", own_down, 1)):
        grads[n], delta[n], new_m[n], new_v[n] = _adamw_layers(
            w_loc[n], owns, [lands[idx] for _, lands in ffn_sent], m_loc[n], v_loc[n], _ADAM_ROWS[n],
            name=f"adamw_{n}")
    for d in (grads, delta, new_m, new_v):
        d["w_up"] = up_t(d["w_up"])
    (d_out, d_pool, little_slabs), (land_out, land_pool, land_little) = _send_wait(
        send_rest, delta["w_down"], name="wait_rest")
    (d_in,), (land_in,) = _send_wait(send_in, land_out, name="wait_w_in")
    pool_2d = (n_groups * pool_rows, POOL_GROUP)
    own_pool = lax.dynamic_slice_in_dim(d_pool, me * pool_rows, pool_rows, 1)
    for n, own, land, shp in (
            ("w_in", lax.dynamic_index_in_dim(d_in, me, 0, keepdims=False), land_in, w_in.shape),
            ("w_out", lax.dynamic_slice_in_dim(d_out, me * out_rows, out_rows, 0), land_out, w_out.shape),
            ("pool_w", own_pool.reshape(pool_2d), land_pool.reshape((N_DEV - 1,) + pool_2d), (1,) + pool_2d)):
        outs = _adamw_layers(w_loc[n].reshape(shp), [own], [land], m_loc[n].reshape(shp), v_loc[n].reshape(shp),
                             _ADAM_ROWS[n], name=f"adamw_{n}")
        grads[n], delta[n], new_m[n], new_v[n] = (o.reshape(w_loc[n].shape) for o in outs)
    own_little = lax.dynamic_index_in_dim(little_slabs, me, 0, keepdims=False)
    g_little = _unpack(_sum_slabs(land_little, own=own_little, name="sum_little"),
                       [_shard_shape(s, a) for _, s, a in _LITTLE])
    grads.update({n: gl for (n, _, _), gl in zip(_LITTLE, g_little)})
    _, (everyone,) = _send_wait(send_small, delta["w_in"], name="wait_small")
    summed = _unpack(_sum_slabs(everyone, name="sum_small"),
                     [(1, 1)] + [s for _, s in replicated] + [(N_META, D_MODEL)])
    loss = summed[0].reshape(())
    grads.update({n: gr for (n, _), gr in zip(replicated, summed[1:-1])})
    grads["meta_tokens"] = lax.dynamic_slice_in_dim(summed[-1], me * out_rows, out_rows, 1)
    at_least_2d = lambda a: a.reshape((1,) * (2 - a.ndim) + a.shape)
    rest = [n for n in names if n not in _MATMUL_WEIGHTS]
    outs = _adamw_many([at_least_2d(w_loc[n]) for n in rest], [at_least_2d(grads[n]) for n in rest],
                       [at_least_2d(m_loc[n]) for n in rest], [at_least_2d(v_loc[n]) for n in rest],
                       name="adamw_rest")
    for i, n in enumerate(rest):
        delta[n], new_m[n], new_v[n] = (o.reshape(w_loc[n].shape) for o in outs[3 * i:3 * i + 3])
    return (loss, grad_x[None], *[grads[n] for n in names], *[delta[n] for n in names],
            *[new_m[n] for n in names], *[new_v[n] for n in names])
```

```python
import functools

import jax
import jax.numpy as jnp
from jax import lax
from jax.experimental import pallas as pl
from jax.experimental.pallas import tpu as pltpu

F32 = jnp.float32
BF16 = jnp.bfloat16

N_DEV = 8
DEPTH = 2
D_MODEL = 1024
N_META = 16
FOX_HEADS = 8
FOX_HEAD_DIM = 64
FOX_WIDTH = 512
CONV_CH = 512
CONV_WIDTH = 31
POOL_WINDOWS = (2, 4, 8, 16)
POOL_GROUP = 256
D_FF = 2816
FFN_CONV_WIDTH = 3
RMS_EPS = 1e-6
LN_EPS = 1e-5
ADAM_LR = 0.001
ADAM_B1 = 0.9
ADAM_B2 = 0.999
ADAM_EPS = 1e-08
ADAM_WD = 0.01
ADAM_STEP = 10

CHUNK = 128
HALO = 32
NEG_BIG = -1e30
FLAT_W = 1024


def _round_up(n, m):
    return (n + m - 1) // m * m


def _sigmoid(x):
    return 1.0 / (1.0 + jnp.exp(-x))


def _fold8(p):
    acc = p[0:8, :]
    for r in range(1, p.shape[0] // 8):
        acc = acc + p[8 * r:8 * r + 8, :]
    return acc


def _mm(a, b, *, name, tb=False, tm=None, tn=None, tk=None, out_dtype=F32, res=None,
        a_map=None, b_map=None, o_map=None, out_shape=None, dims=None, after=None):
    if dims is None:
        m, k = a.shape
        n = b.shape[-2] if tb else b.shape[-1]
    else:
        m, n, k = dims
    tm, tn, tk = tm or m, tn or n, tk or k
    assert m % tm == 0 and n % tn == 0 and k % tk == 0, (name, m, n, k, tm, tn, tk)
    nk = k // tk
    a_map = a_map or (lambda i, j, kk: (i, kk))
    b_map = b_map or ((lambda i, j, kk: (j, kk)) if tb else (lambda i, j, kk: (kk, j)))
    o_map = o_map or (lambda i, j, kk: (i, j))
    out_shape = out_shape or (m, n)
    contract = (((1,), (1,)), ((), ())) if tb else (((1,), (0,)), ((), ()))
    has_res = res is not None

    def body(*refs):
        a_ref, b_ref = refs[0], refs[1]
        res_ref = refs[2] if has_res else None
        o_ref = refs[2 + has_res + (after is not None)]
        p = lax.dot_general(a_ref[...], b_ref[...], contract, preferred_element_type=F32)
        if nk == 1:
            if has_res:
                p = p + res_ref[...]
            o_ref[...] = p.astype(o_ref.dtype)
        else:
            acc_ref = refs[-1]
            kk = pl.program_id(2)

            @pl.when(kk == 0)
            def _():
                acc_ref[...] = p

            @pl.when(kk > 0)
            def _():
                acc_ref[...] += p

            @pl.when(kk == nk - 1)
            def _():
                r = acc_ref[...]
                if has_res:
                    r = r + res_ref[...]
                o_ref[...] = r.astype(o_ref.dtype)

    in_specs = [pl.BlockSpec((tm, tk), a_map), pl.BlockSpec((tn, tk) if tb else (tk, tn), b_map)]
    operands = [a, b]
    if has_res:
        in_specs.append(pl.BlockSpec((tm, tn), o_map))
        operands.append(res)
    if after is not None:
        in_specs.append(pl.BlockSpec(memory_space=pl.ANY))
        operands.append(after)
    return pl.pallas_call(
        body, name=name, grid=(m // tm, n // tn, nk),
        in_specs=in_specs, out_specs=pl.BlockSpec((tm, tn), o_map),
        out_shape=jax.ShapeDtypeStruct(out_shape, out_dtype),
        scratch_shapes=[pltpu.VMEM((tm, tn), F32)] if nk > 1 else [],
    )(*operands)


def _transpose(x, *, name, out_dtype, cols=None, after=None):
    r, c = x.shape
    cols = cols or c
    assert r % CHUNK == 0
    behind = [] if after is None else [after]

    def body(x_ref, *rest):
        o_ref = rest[-1]
        o_ref[...] = x_ref[...].astype(o_ref.dtype).T

    return pl.pallas_call(
        body, name=name, grid=(r // CHUNK,),
        in_specs=[pl.BlockSpec((CHUNK, cols), lambda i: (i, 0))] + [pl.BlockSpec(memory_space=pl.ANY)] * len(behind),
        out_specs=pl.BlockSpec((cols, CHUNK), lambda i: (0, i)),
        out_shape=jax.ShapeDtypeStruct((cols, r), out_dtype),
    )(x, *behind)


def _rms_fwd(x, g, *, name, out_dtype, after=None, transposed=False):
    lp, dm = x.shape
    tr = CHUNK if transposed else lp // 4
    behind = [] if after is None else [after]

    def body(x_ref, g_ref, *rest):
        xv = x_ref[...]
        r = lax.rsqrt(jnp.mean(xv * xv, axis=-1, keepdims=True) + RMS_EPS)
        y = xv * r * g_ref[...]
        if transposed:
            yb = y.astype(out_dtype)
            rest[-2][...] = yb
            rest[-1][...] = yb.T
        else:
            rest[-1][...] = y.astype(out_dtype)

    row = pl.BlockSpec((tr, dm), lambda i: (i, 0))
    out_specs, out_shape = row, jax.ShapeDtypeStruct((lp, dm), out_dtype)
    if transposed:
        out_specs = [row, pl.BlockSpec((dm, tr), lambda i: (0, i))]
        out_shape = [out_shape, jax.ShapeDtypeStruct((dm, lp), out_dtype)]
    return pl.pallas_call(
        body, name=name, grid=(lp // tr,),
        in_specs=[row, pl.BlockSpec((1, dm), lambda i: (0, 0))] + [pl.BlockSpec(memory_space=pl.ANY)] * len(behind),
        out_specs=out_specs, out_shape=out_shape,
    )(x, g, *behind)


def _rms_bwd_rows(x_ref, g_ref, dnv, dres_ref, dh_ref, dhb_ref, dg_ref, first):
    xv = x_ref[...]
    r = lax.rsqrt(jnp.mean(xv * xv, axis=-1, keepdims=True) + RMS_EPS)
    xhat = xv * r

    @pl.when(first)
    def _():
        dg_ref[...] = jnp.zeros_like(dg_ref)

    dg_ref[...] += jnp.sum(dnv * xhat, axis=0, keepdims=True)
    dxhat = dnv * g_ref[...]
    dx = r * (dxhat - xhat * jnp.mean(dxhat * xhat, axis=-1, keepdims=True))
    dh = dres_ref[...] + dx
    dh_ref[...] = dh
    dhb_ref[...] = dh.astype(BF16)


def _mm_rms_bwd(a, b, x, g, dres, *, name, tk, tb=False, a_map=None, after=None):
    lp, dm = x.shape
    tm = lp // 4
    nk = (b.shape[1] if tb else b.shape[0]) // tk
    a_map = a_map or (lambda i, kk: (i, kk))
    contract = (((1,), (1,)), ((), ())) if tb else (((1,), (0,)), ((), ()))
    behind = [] if after is None else [after]

    def body(a_ref, b_ref, x_ref, g_ref, dres_ref, *rest):
        dh_ref, dhb_ref, dg_ref, acc_ref = rest[len(behind):]
        i, kk = pl.program_id(0), pl.program_id(1)
        p = lax.dot_general(a_ref[...], b_ref[...], contract, preferred_element_type=F32)

        @pl.when(kk == 0)
        def _():
            acc_ref[...] = p

        @pl.when(kk > 0)
        def _():
            acc_ref[...] += p

        @pl.when(kk == nk - 1)
        def _():
            _rms_bwd_rows(x_ref, g_ref, acc_ref[...], dres_ref, dh_ref, dhb_ref, dg_ref, i == 0)

    row = pl.BlockSpec((tm, dm), lambda i, kk: (i, 0))
    vec = pl.BlockSpec((1, dm), lambda i, kk: (0, 0))
    b_spec = pl.BlockSpec((dm, tk), lambda i, kk: (0, kk)) if tb else pl.BlockSpec((tk, dm), lambda i, kk: (kk, 0))
    return pl.pallas_call(
        body, name=name, grid=(lp // tm, nk),
        in_specs=[pl.BlockSpec((tm, tk), a_map), b_spec, row, vec, row] + [pl.BlockSpec(memory_space=pl.ANY)] * len(behind),
        out_specs=[row, row, vec],
        out_shape=[jax.ShapeDtypeStruct((lp, dm), F32), jax.ShapeDtypeStruct((lp, dm), BF16),
                   jax.ShapeDtypeStruct((1, dm), F32)],
        scratch_shapes=[pltpu.VMEM((tm, dm), F32)],
    )(a, b, x, g, dres, *behind)


def _rms_bwd(x, g, dn, dres, *, name):
    lp, dm = x.shape
    tr = lp // 4

    def body(x_ref, g_ref, dn_ref, dres_ref, dh_ref, dhb_ref, dg_ref):
        _rms_bwd_rows(x_ref, g_ref, dn_ref[...], dres_ref, dh_ref, dhb_ref, dg_ref, pl.program_id(0) == 0)

    row = pl.BlockSpec((tr, dm), lambda i: (i, 0))
    vec = pl.BlockSpec((1, dm), lambda i: (0, 0))
    return pl.pallas_call(
        body, name=name, grid=(lp // tr,),
        in_specs=[row, vec, row, row], out_specs=[row, row, vec],
        out_shape=[jax.ShapeDtypeStruct((lp, dm), F32), jax.ShapeDtypeStruct((lp, dm), BF16),
                   jax.ShapeDtypeStruct((1, dm), F32)],
    )(x, g, dn, dres)


def _loss_head(h, g, tgt, n_real, *, name):
    lp, dm = h.shape
    tr = lp // 4

    def body(x_ref, g_ref, t_ref, loss_ref, dh_ref, dhb_ref, dg_ref):
        i = pl.program_id(0)
        xv = x_ref[...]
        r = lax.rsqrt(jnp.mean(xv * xv, axis=-1, keepdims=True) + RMS_EPS)
        xhat = xv * r
        gv = g_ref[...]
        y = xhat * gv
        t = i * tr + lax.broadcasted_iota(jnp.int32, (tr, 1), 0)
        valid = (t >= N_META) & (t < n_real)
        diff = jnp.where(valid, y - t_ref[...], 0.0)

        @pl.when(i == 0)
        def _():
            loss_ref[...] = jnp.zeros_like(loss_ref)
            dg_ref[...] = jnp.zeros_like(dg_ref)

        row_sq = jnp.sum(diff * diff, axis=-1, keepdims=True) * (1.0 / dm)
        part = 0.5 * jnp.sum(row_sq, axis=0, keepdims=True)
        loss_ref[...] += jnp.broadcast_to(part, loss_ref.shape)
        dy = diff * (1.0 / dm)
        dg_ref[...] += jnp.sum(dy * xhat, axis=0, keepdims=True)
        dxhat = dy * gv
        dx = r * (dxhat - xhat * jnp.mean(dxhat * xhat, axis=-1, keepdims=True))
        dh_ref[...] = dx
        dhb_ref[...] = dx.astype(BF16)

    row = pl.BlockSpec((tr, dm), lambda i: (i, 0))
    vec = pl.BlockSpec((1, dm), lambda i: (0, 0))
    return pl.pallas_call(
        body, name=name, grid=(lp // tr,),
        in_specs=[row, vec, row],
        out_specs=[pl.BlockSpec((1, 128), lambda i: (0, 0)), row, row, vec],
        out_shape=[jax.ShapeDtypeStruct((1, 128), F32), jax.ShapeDtypeStruct((lp, dm), F32),
                   jax.ShapeDtypeStruct((lp, dm), BF16), jax.ShapeDtypeStruct((1, dm), F32)],
    )(h, g, tgt)


def _tri(upper):
    r = lax.broadcasted_iota(jnp.int32, (CHUNK, CHUNK), 0)
    c = lax.broadcasted_iota(jnp.int32, (CHUNK, CHUNK), 1)
    return jnp.where(r <= c if upper else r >= c, 1.0, 0.0).astype(F32)


def _fox_prep(f_t, b_f, *, name):
    nh, lp = f_t.shape
    nch = lp // CHUNK

    def body(f_ref, b_ref, c_ref):
        tri = _tri(True)
        carry = jnp.zeros((nh, 1), F32)
        for blk in range(nch):
            cols = slice(blk * CHUNK, (blk + 1) * CHUNK)
            z = f_ref[:, cols] + b_ref[...]
            logf = jnp.minimum(z, 0.0) - jnp.log(1.0 + jnp.exp(-jnp.abs(z)))
            cb = jnp.dot(logf, tri, preferred_element_type=F32, precision=lax.Precision.HIGHEST)
            c_ref[:, cols] = cb + carry
            carry = carry + jnp.sum(logf, axis=1, keepdims=True)

    return pl.pallas_call(
        body, name=name, out_shape=jax.ShapeDtypeStruct((nh, lp), F32),
    )(f_t, b_f)


def _fox_bwd(dc, f_t, b_f, *, name):
    nh, lp = f_t.shape
    nch = lp // CHUNK

    def body(dc_ref, f_ref, b_ref, df_ref, db_ref):
        tri = _tri(False)
        carry = jnp.zeros((nh, 1), F32)
        db = jnp.zeros((nh, 1), F32)
        df_ref[...] = jnp.zeros_like(df_ref)
        for blk in reversed(range(nch)):
            cols = slice(blk * CHUNK, (blk + 1) * CHUNK)
            dcb = dc_ref[:, cols]
            dlogf = jnp.dot(dcb, tri, preferred_element_type=F32, precision=lax.Precision.HIGHEST) + carry
            carry = carry + jnp.sum(dcb, axis=1, keepdims=True)
            z = f_ref[:, cols] + b_ref[...]
            dz = dlogf * _sigmoid(-z)
            df_ref[0:nh, cols] = dz
            db = db + jnp.sum(dz, axis=1, keepdims=True)
        db_ref[...] = db

    return pl.pallas_call(
        body, name=name,
        out_shape=[jax.ShapeDtypeStruct((128, lp), F32), jax.ShapeDtypeStruct((nh, 1), F32)],
    )(dc, f_t, b_f)


ATTN_BLOCKS = 4


def _attn_blocks(lp):
    tq = lp // ATTN_BLOCKS
    return tq, [(i * tq, min(lp, _round_up((i + 1) * tq, CHUNK))) for i in range(ATTN_BLOCKS)]


ATTN_SCALE = FOX_HEAD_DIM ** -0.5


def _attn_probs(q2s, k_h, c_row, row0, n):
    tq = q2s.shape[0]
    lo = row0 // CHUNK * CHUNK
    logits = []
    for c0, c1 in ([(0, lo)] if lo else []) + [(lo, n)]:
        s = lax.dot_general(q2s, k_h[c0:c1], (((1,), (1,)), ((), ())), preferred_element_type=F32) - c_row[:, c0:c1]
        if c1 > row0:
            t = row0 + lax.broadcasted_iota(jnp.int32, (tq, c1 - c0), 0)
            sidx = c0 + lax.broadcasted_iota(jnp.int32, (tq, c1 - c0), 1)
            s = jnp.where(sidx <= t, s, NEG_BIG)
        logits.append((s, c0, c1))
    m = functools.reduce(jnp.maximum, [jnp.max(s, axis=1, keepdims=True) for s, _, _ in logits])
    ps = [(jnp.exp(s - m), c0, c1) for s, c0, c1 in logits]
    inv = 1.0 / sum(jnp.sum(p, axis=1, keepdims=True) for p, _, _ in ps)
    return [(p * inv, c0, c1) for p, c0, c1 in ps]


def _attn_fwd(qkv, c3, *, name):
    lp = qkv.shape[0]
    tq, blocks = _attn_blocks(lp)

    def body(q_ref, k_ref, v_ref, c_ref, o_ref):
        lane = lax.broadcasted_iota(jnp.int32, (1, 128), 1)
        zero = jnp.zeros((), BF16)
        for i, (row0, n) in enumerate(blocks):
            q2s = q_ref[row0:row0 + tq, :] * ATTN_SCALE
            acc = jnp.zeros((tq, 128), F32)
            for hd in range(2):
                sel = (lane < 64) if hd == 0 else (lane >= 64)
                k_h = jnp.where(sel, k_ref[0:n, :], zero)
                v_h = jnp.where(sel, v_ref[0:n, :], zero)
                for p, c0, c1 in _attn_probs(q2s, k_h, c_ref[hd:hd + 1, 0:n], row0, n):
                    acc = acc + jnp.dot(p.astype(BF16), v_h[c0:c1], preferred_element_type=F32)
            o_ref[row0:row0 + tq, :] = acc.astype(BF16)

    blk = lambda off: pl.BlockSpec((lp, 128), lambda p: (0, off + p))
    return pl.pallas_call(
        body, name=name, grid=(4,),
        in_specs=[blk(0), blk(4), blk(8), pl.BlockSpec((None, 2, lp), lambda p: (p, 0, 0))],
        out_specs=pl.BlockSpec((lp, 128), lambda p: (0, p)),
        out_shape=jax.ShapeDtypeStruct((lp, FOX_WIDTH), BF16),
    )(qkv, qkv, qkv, c3)


def _attn_bwd(qkv, q_t, dcat, do_t, c3, *, name):
    lp = qkv.shape[0]
    tq, blocks = _attn_blocks(lp)
    scale = FOX_HEAD_DIM ** -0.5

    def body(q_ref, k_ref, v_ref, qt_ref, do_ref, dot_ref, c_ref, dq_ref, dkt_ref, dvt_ref, dc_ref,
             dkt_acc, dvt_acc):
        lane = lax.broadcasted_iota(jnp.int32, (1, 128), 1)
        sub = lax.broadcasted_iota(jnp.int32, (128, 1), 0)
        zero = jnp.zeros((), BF16)
        dkt_acc[...] = jnp.zeros_like(dkt_acc)
        dvt_acc[...] = jnp.zeros_like(dvt_acc)
        dc_ref[...] = jnp.zeros_like(dc_ref)
        for i, (row0, n) in enumerate(blocks):
            rows = slice(row0, row0 + tq)
            q2s = q_ref[rows, :] * ATTN_SCALE
            do2 = do_ref[rows, :].astype(BF16)
            dq_acc = jnp.zeros((tq, 128), F32)
            for hd in range(2):
                sel = (lane < 64) if hd == 0 else (lane >= 64)
                sel_t = (sub < 64) if hd == 0 else (sub >= 64)
                k_h = jnp.where(sel, k_ref[0:n, :], zero)
                v_h = jnp.where(sel, v_ref[0:n, :], zero)
                qt_h = jnp.where(sel_t, qt_ref[:, rows], zero)
                dot_h = jnp.where(sel_t, dot_ref[:, rows], zero)
                segs = [(p, lax.dot_general(do2, v_h[c0:c1], (((1,), (1,)), ((), ())), preferred_element_type=F32),
                         c0, c1) for p, c0, c1 in _attn_probs(q2s, k_h, c_ref[hd:hd + 1, 0:n], row0, n)]
                delta = sum(jnp.sum(p * dp, axis=1, keepdims=True) for p, dp, _, _ in segs)
                for p, dp, c0, c1 in segs:
                    ds = p * (dp - delta)
                    dsb = ds.astype(BF16)
                    dq_acc = dq_acc + jnp.dot(dsb, k_h[c0:c1], preferred_element_type=F32)
                    dkt_acc[:, c0:c1] += jnp.dot(qt_h, dsb, preferred_element_type=F32)
                    dvt_acc[:, c0:c1] += jnp.dot(dot_h, p.astype(BF16), preferred_element_type=F32)
                    dc_ref[hd:hd + 1, c0:c1] -= jnp.sum(ds, axis=0, keepdims=True)
            dq_ref[rows, :] = (dq_acc * scale).astype(BF16)
        dkt_ref[...] = (dkt_acc[...] * scale).astype(BF16)
        dvt_ref[...] = dvt_acc[...].astype(BF16)

    blk = lambda off: pl.BlockSpec((lp, 128), lambda p: (0, off + p))
    blk_t = pl.BlockSpec((128, lp), lambda p: (p, 0))
    c_spec = pl.BlockSpec((None, 2, lp), lambda p: (p, 0, 0))
    return pl.pallas_call(
        body, name=name, grid=(4,),
        in_specs=[blk(0), blk(4), blk(8), blk_t, blk(0), blk_t, c_spec],
        out_specs=[blk(0), blk_t, blk_t, c_spec],
        out_shape=[jax.ShapeDtypeStruct((lp, FOX_WIDTH), BF16), jax.ShapeDtypeStruct((FOX_WIDTH, lp), BF16),
                   jax.ShapeDtypeStruct((FOX_WIDTH, lp), BF16), jax.ShapeDtypeStruct((4, 2, lp), F32)],
        scratch_shapes=[pltpu.VMEM((128, lp), F32), pltpu.VMEM((128, lp), F32)],
    )(qkv, qkv, qkv, q_t, dcat, do_t, c3)


def _ln_stats(x):
    mu = jnp.mean(x, axis=-1, keepdims=True)
    xc = x - mu
    var = jnp.mean(xc * xc, axis=-1, keepdims=True)
    rstd = lax.rsqrt(var + LN_EPS)
    return xc * rstd, rstd


def _conv_fwd(agf, conv_w, conv_b, ln_g, ln_b, *, name):
    lp = agf.shape[0]
    nch = lp // CHUNK
    c = CONV_CH

    def body(a_ref, g_ref, w_ref, b_ref, lg_ref, lb_ref, u0_ref, u1_ref, u3_ref, u0s):
        u0s[0:HALO, :] = jnp.zeros((HALO, c), F32)

        def glu(ci, _):
            rows = pl.ds(pl.multiple_of(ci * CHUNK, CHUNK), CHUNK)
            u0 = a_ref[rows, :] * _sigmoid(g_ref[rows, :])
            u0_ref[rows, :] = u0
            u0s[pl.ds(pl.multiple_of(ci * CHUNK + HALO, 8), CHUNK), :] = u0
            return 0

        lax.fori_loop(0, nch, glu, 0)

        def conv(ci, _):
            r0 = pl.multiple_of(ci * CHUNK, CHUNK)
            rows = pl.ds(r0, CHUNK)
            for lg in range(c // 128):
                lanes = slice(lg * 128, (lg + 1) * 128)
                win = u0s[pl.ds(r0, CHUNK + HALO), lanes]
                acc = jnp.broadcast_to(b_ref[:, lanes], (CHUNK, 128))
                for k in range(CONV_WIDTH):
                    s = CONV_WIDTH - 1 - k
                    sh = win if s == 0 else pltpu.roll(win, s, 0)
                    acc = acc + w_ref[k:k + 1, lanes] * sh[HALO:HALO + CHUNK, :]
                u1_ref[rows, lanes] = acc
            xhat, _ = _ln_stats(u1_ref[rows, :])
            y = xhat * lg_ref[...] + lb_ref[...]
            u3_ref[rows, :] = (y * _sigmoid(y)).astype(BF16)
            return 0

        lax.fori_loop(0, nch, conv, 0)

    full = lambda shape: pl.BlockSpec(shape, lambda i: (0, 0))
    return pl.pallas_call(
        body, name=name, grid=(1,),
        in_specs=[pl.BlockSpec((lp, c), lambda i: (0, 0)), pl.BlockSpec((lp, c), lambda i: (0, 1)),
                  full((CONV_WIDTH, c)), full((1, c)), full((1, c)), full((1, c))],
        out_specs=[full((lp, c)), full((lp, c)), full((lp, c))],
        out_shape=[jax.ShapeDtypeStruct((lp, c), F32), jax.ShapeDtypeStruct((lp, c), F32),
                   jax.ShapeDtypeStruct((lp, c), BF16)],
        scratch_shapes=[pltpu.VMEM((lp + HALO, c), F32)],
    )(agf, agf, conv_w, conv_b, ln_g, ln_b)


def _conv_bwd(dcat, u0, u1, agf, conv_w, ln_g, ln_b, *, name):
    lp = agf.shape[0]
    nch = lp // CHUNK
    c = CONV_CH
    wlen = CHUNK + HALO

    def body(du3_ref, u0_ref, u1_ref, a_ref, g_ref, w_ref, lg_ref, lb_ref,
             dag_ref, dw_ref, db_ref, dlg_ref, dlb_ref, du1s, dwacc, vacc):
        du1s[lp:lp + HALO, :] = jnp.zeros((HALO, c), F32)
        dwacc[...] = jnp.zeros_like(dwacc)
        vacc[...] = jnp.zeros_like(vacc)

        def ln_bwd(ci, _):
            r0 = pl.multiple_of(ci * CHUNK, CHUNK)
            rows = pl.ds(r0, CHUNK)
            xhat, rstd = _ln_stats(u1_ref[rows, :])
            y = xhat * lg_ref[...] + lb_ref[...]
            sg = _sigmoid(y)
            du2 = du3_ref[rows, :] * (sg * (1.0 + y * (1.0 - sg)))
            vacc[0:8, :] += _fold8(du2 * xhat)
            vacc[8:16, :] += _fold8(du2)
            dxhat = du2 * lg_ref[...]
            du1 = rstd * (dxhat - jnp.mean(dxhat, axis=-1, keepdims=True)
                          - xhat * jnp.mean(dxhat * xhat, axis=-1, keepdims=True))
            vacc[16:24, :] += _fold8(du1)
            du1s[rows, :] = du1
            return 0

        lax.fori_loop(0, nch, ln_bwd, 0)

        def conv_bwd(ci, _):
            r0 = pl.multiple_of(ci * CHUNK, CHUNK)
            rows = pl.ds(r0, CHUNK)
            for lg in range(c // 128):
                lanes = slice(lg * 128, (lg + 1) * 128)
                dwin = du1s[pl.ds(r0, wlen), lanes]
                u0 = u0_ref[rows, lanes]
                acc = jnp.zeros((CHUNK, 128), F32)
                for k in range(CONV_WIDTH):
                    s = CONV_WIDTH - 1 - k
                    d_s = (dwin if s == 0 else pltpu.roll(dwin, wlen - s, 0))[0:CHUNK, :]
                    acc = acc + w_ref[k:k + 1, lanes] * d_s
                    dwacc[8 * k:8 * k + 8, lanes] += _fold8(d_s * u0)
                sg = _sigmoid(g_ref[rows, lanes])
                a = a_ref[rows, lanes]
                dag_ref[rows, lanes] = (acc * sg).astype(BF16)
                dag_ref[rows, slice(c + lg * 128, c + (lg + 1) * 128)] = (acc * a * sg * (1.0 - sg)).astype(BF16)
            return 0

        lax.fori_loop(0, nch, conv_bwd, 0)
        for k in range(CONV_WIDTH):
            dw_ref[k:k + 1, :] = jnp.sum(dwacc[8 * k:8 * k + 8, :], axis=0, keepdims=True)
        dlg_ref[...] = jnp.sum(vacc[0:8, :], axis=0, keepdims=True)
        dlb_ref[...] = jnp.sum(vacc[8:16, :], axis=0, keepdims=True)
        db_ref[...] = jnp.sum(vacc[16:24, :], axis=0, keepdims=True)

    full = lambda shape: pl.BlockSpec(shape, lambda i: (0, 0))
    vec = jax.ShapeDtypeStruct((1, c), F32)
    return pl.pallas_call(
        body, name=name, grid=(1,),
        in_specs=[pl.BlockSpec((lp, c), lambda i: (0, 1)), full((lp, c)), full((lp, c)),
                  pl.BlockSpec((lp, c), lambda i: (0, 0)), pl.BlockSpec((lp, c), lambda i: (0, 1)),
                  full((CONV_WIDTH, c)), full((1, c)), full((1, c))],
        out_specs=[full((lp, 2 * c)), full((CONV_WIDTH, c)), full((1, c)), full((1, c)), full((1, c))],
        out_shape=[jax.ShapeDtypeStruct((lp, 2 * c), BF16), jax.ShapeDtypeStruct((CONV_WIDTH, c), F32), vec, vec, vec],
        scratch_shapes=[pltpu.VMEM((lp + HALO, c), F32), pltpu.VMEM((8 * CONV_WIDTH, c), F32),
                        pltpu.VMEM((24, c), F32)],
    )(dcat, u0, u1, agf, agf, conv_w, ln_g, ln_b)


FFN_TILE = 256
FFN_PAD = 8


def _ffn_conv(xs, w_ref, b_ref, half, r0):
    win = xs[half, pl.ds(r0, CHUNK + FFN_PAD), :]
    acc = jnp.broadcast_to(b_ref[half], (CHUNK, FFN_TILE))
    for k in range(FFN_CONV_WIDTH):
        s = FFN_CONV_WIDTH - 1 - k
        sh = win if s == 0 else pltpu.roll(win, s, 0)
        acc = acc + w_ref[half, k:k + 1, :] * sh[FFN_PAD:FFN_PAD + CHUNK, :]
    return acc


def _ffn_up_act(n, w_up_t, w3, b3, *, name):
    lp = n.shape[0]
    nch, nt = lp // CHUNK, D_FF // FFN_TILE
    nt_dims = (((1,), (1,)), ((), ()))

    parts = 4

    def project(n_ref, wg_ref, wv_ref, u, part=None):
        rows = slice(0, lp) if part is None else slice(part * lp // parts, (part + 1) * lp // parts)
        for half, w_ref in ((0, wg_ref), (1, wv_ref)):
            u[half, FFN_PAD + rows.start:FFN_PAD + rows.stop, :] = lax.dot_general(
                n_ref[rows, :], w_ref[...], nt_dims, preferred_element_type=F32)

    def activate(u, w_ref, b_ref, act_ref, act_t_ref, gv_ref, up_ref, part):
        for ci in range(part * nch // parts, (part + 1) * nch // parts):
            r0 = ci * CHUNK
            rows = slice(r0, r0 + CHUNK)
            gate = _ffn_conv(u, w_ref, b_ref, 0, r0)
            val = _ffn_conv(u, w_ref, b_ref, 1, r0)
            gv_ref[0, rows, :] = gate.astype(BF16)
            gv_ref[1, rows, :] = val.astype(BF16)
            for half in range(2):
                up_ref[half, rows, :] = u[half, FFN_PAD + r0:FFN_PAD + r0 + CHUNK, :].astype(BF16)
            act = (gate * _sigmoid(gate) * val).astype(BF16)
            act_ref[rows, :] = act
            act_t_ref[:, rows] = act.T

    def body(n_ref, wg_ref, wv_ref, wg_next, wv_next, w_ref, b_ref, act_ref, act_t_ref, gv_ref, up_ref, u0, u1):
        j = pl.program_id(0)

        @pl.when(j == 0)
        def _():
            for u in (u0, u1):
                u[:, 0:FFN_PAD, :] = jnp.zeros((2, FFN_PAD, FFN_TILE), F32)
            project(n_ref, wg_ref, wv_ref, u0)

        for parity, (mine, other) in enumerate(((u0, u1), (u1, u0))):
            @pl.when(j % 2 == parity)
            def _(mine=mine, other=other):
                for part in range(parts):
                    project(n_ref, wg_next, wv_next, other, part)
                    activate(mine, w_ref, b_ref, act_ref, act_t_ref, gv_ref, up_ref, part)

    halves = pl.BlockSpec((2, lp, FFN_TILE), lambda j: (0, 0, j))
    rows_of = lambda half, ahead: pl.BlockSpec(
        (FFN_TILE, D_MODEL), lambda j: (half * nt + jnp.minimum(j + ahead, nt - 1), 0))
    return pl.pallas_call(
        body, name=name, grid=(nt,),
        in_specs=[pl.BlockSpec((lp, D_MODEL), lambda j: (0, 0)), rows_of(0, 0), rows_of(1, 0), rows_of(0, 1),
                  rows_of(1, 1), pl.BlockSpec((2, FFN_CONV_WIDTH, FFN_TILE), lambda j: (0, 0, j)),
                  pl.BlockSpec((2, 1, FFN_TILE), lambda j: (0, 0, j))],
        out_specs=[pl.BlockSpec((lp, FFN_TILE), lambda j: (0, j)), pl.BlockSpec((FFN_TILE, lp), lambda j: (j, 0)),
                   halves, halves],
        out_shape=[jax.ShapeDtypeStruct((lp, D_FF), BF16), jax.ShapeDtypeStruct((D_FF, lp), BF16),
                   jax.ShapeDtypeStruct((2, lp, D_FF), BF16), jax.ShapeDtypeStruct((2, lp, D_FF), BF16)],
        scratch_shapes=[pltpu.VMEM((2, lp + FFN_PAD, FFN_TILE), F32), pltpu.VMEM((2, lp + FFN_PAD, FFN_TILE), F32)],
    )(n, w_up_t, w_up_t, w_up_t, w_up_t, w3, b3)


def _ffn_act_bwd(up3, gv3, w3, dact, *, name):
    _, lp, f = up3.shape
    nch = lp // CHUNK
    wlen = CHUNK + FFN_PAD

    def body(up_ref, gv_ref, w_ref, dact_ref, dup_ref, dw_ref, db_ref, ds, wacc):
        for half in range(2):
            ds[half, lp:lp + FFN_PAD, :] = jnp.zeros((FFN_PAD, FFN_TILE), F32)
        wacc[...] = jnp.zeros_like(wacc)

        def act_bwd(ci, _):
            rows = pl.ds(pl.multiple_of(ci * CHUNK, CHUNK), CHUNK)
            gate, val = gv_ref[0, rows, :].astype(F32), gv_ref[1, rows, :].astype(F32)
            sg = _sigmoid(gate)
            da = dact_ref[rows, :].astype(F32)
            ds[0, rows, :] = da * val * (sg * (1.0 + gate * (1.0 - sg)))
            ds[1, rows, :] = da * (gate * sg)
            return 0

        lax.fori_loop(0, nch, act_bwd, 0, unroll=True)

        def conv_bwd(ci, _):
            r0 = pl.multiple_of(ci * CHUNK, CHUNK)
            rows = pl.ds(r0, CHUNK)
            for half in range(2):
                dwin = ds[half, pl.ds(r0, wlen), :]
                x = up_ref[half, rows, :].astype(F32)
                acc = jnp.zeros((CHUNK, FFN_TILE), F32)
                for k in range(FFN_CONV_WIDTH):
                    s = FFN_CONV_WIDTH - 1 - k
                    d_s = (dwin if s == 0 else pltpu.roll(dwin, wlen - s, 0))[0:CHUNK, :]
                    acc = acc + w_ref[half, k:k + 1, :] * d_s
                    wacc[half, 8 * k:8 * k + 8, :] += _fold8(d_s * x)
                wacc[half, 24:32, :] += _fold8(dwin[0:CHUNK, :])
                dup_ref[half, rows, :] = acc.astype(BF16)
            return 0

        lax.fori_loop(0, nch, conv_bwd, 0)
        for half in range(2):
            for k in range(FFN_CONV_WIDTH):
                dw_ref[half, k:k + 1, :] = jnp.sum(wacc[half, 8 * k:8 * k + 8, :], axis=0, keepdims=True)
            db_ref[half] = jnp.sum(wacc[half, 24:32, :], axis=0, keepdims=True)

    halves = pl.BlockSpec((2, lp, FFN_TILE), lambda j: (0, 0, j))
    taps = pl.BlockSpec((2, FFN_CONV_WIDTH, FFN_TILE), lambda j: (0, 0, j))
    bias = pl.BlockSpec((2, 1, FFN_TILE), lambda j: (0, 0, j))
    return pl.pallas_call(
        body, name=name, grid=(f // FFN_TILE,),
        in_specs=[halves, halves, taps, pl.BlockSpec((lp, FFN_TILE), lambda j: (0, j))],
        out_specs=[halves, taps, bias],
        out_shape=[jax.ShapeDtypeStruct((2, lp, f), BF16), jax.ShapeDtypeStruct((2, FFN_CONV_WIDTH, f), F32),
                   jax.ShapeDtypeStruct((2, 1, f), F32)],
        scratch_shapes=[pltpu.VMEM((2, lp + FFN_PAD, FFN_TILE), F32), pltpu.VMEM((2, 32, FFN_TILE), F32)],
    )(up3, gv3, w3, dact)


POOL_PAD = 16


def _inv_count(r0, w):
    t = r0 + lax.broadcasted_iota(jnp.int32, (CHUNK, 1), 0)
    return 1.0 / jnp.minimum(t + 1, w).astype(F32)


def _pool_fwd(n, pool_w, pool_b, pool_scale, h, *, name):
    lp, dm = n.shape
    nch = lp // CHUNK
    g = POOL_GROUP

    def body(n_ref, w_ref, b_ref, s_ref, h_ref, ho_ref, dt_ref, z_ref, xs, d_ref):
        gi = pl.program_id(0)
        xs[0:POOL_PAD, :] = jnp.zeros((POOL_PAD, g), F32)
        xs[POOL_PAD:POOL_PAD + lp, :] = n_ref[...]
        for idx, w in enumerate(POOL_WINDOWS):
            @pl.when(gi == idx)
            def _(w=w):
                def chunk(ci, _):
                    r0 = pl.multiple_of(ci * CHUNK, CHUNK)
                    win = xs[pl.ds(r0, CHUNK + POOL_PAD), :]
                    acc = win
                    for j in range(1, w):
                        acc = acc + pltpu.roll(win, j, 0)
                    x = win[POOL_PAD:POOL_PAD + CHUNK, :]
                    d = acc[POOL_PAD:POOL_PAD + CHUNK, :] * _inv_count(r0, w) - x
                    d_ref[pl.ds(r0, CHUNK), :] = d.astype(BF16)
                    dt_ref[:, pl.ds(r0, CHUNK)] = d.astype(BF16).T
                    return 0

                lax.fori_loop(0, nch, chunk, 0, unroll=True)

        z = jnp.dot(d_ref[...], w_ref[...], preferred_element_type=F32) + b_ref[...]
        z_ref[...] = z
        ho_ref[...] = h_ref[...] + z * s_ref[...]

    col = pl.BlockSpec((lp, g), lambda i: (0, i))
    vec = pl.BlockSpec((1, g), lambda i: (0, i))
    return pl.pallas_call(
        body, name=name, grid=(len(POOL_WINDOWS),),
        in_specs=[col, pl.BlockSpec((None, g, g), lambda i: (i, 0, 0)), vec, vec, col],
        out_specs=[col, pl.BlockSpec((g, lp), lambda i: (i, 0)), col],
        out_shape=[jax.ShapeDtypeStruct((lp, dm), F32), jax.ShapeDtypeStruct((dm, lp), BF16),
                   jax.ShapeDtypeStruct((lp, dm), F32)],
        scratch_shapes=[pltpu.VMEM((lp + POOL_PAD, g), F32), pltpu.VMEM((lp, g), BF16)],
    )(n, pool_w, pool_b, pool_scale, h)


def _pool_bwd(dy, z, pool_w, pool_scale, *, name):
    lp, dm = dy.shape
    nch = lp // CHUNK
    g = POOL_GROUP
    wlen = CHUNK + POOL_PAD

    def body(dy_ref, z_ref, w_ref, s_ref, dn_ref, dz_ref, dsc_ref, db_ref, ys, dd):
        gi = pl.program_id(0)
        dyv = dy_ref[...]
        dsc_ref[...] = jnp.sum(dyv * z_ref[...], axis=0, keepdims=True)
        dz = dyv * s_ref[...]
        db_ref[...] = jnp.sum(dz, axis=0, keepdims=True)
        dzb = dz.astype(BF16)
        dz_ref[...] = dzb
        dd[...] = lax.dot_general(dzb, w_ref[...], (((1,), (1,)), ((), ())), preferred_element_type=F32)
        ys[lp:lp + POOL_PAD, :] = jnp.zeros((POOL_PAD, g), F32)
        for idx, w in enumerate(POOL_WINDOWS):
            @pl.when(gi == idx)
            def _(w=w):
                def scale(ci, _):
                    r0 = pl.multiple_of(ci * CHUNK, CHUNK)
                    ys[pl.ds(r0, CHUNK), :] = dd[pl.ds(r0, CHUNK), :] * _inv_count(r0, w)
                    return 0

                lax.fori_loop(0, nch, scale, 0, unroll=True)

                def chunk(ci, _):
                    r0 = pl.multiple_of(ci * CHUNK, CHUNK)
                    win = ys[pl.ds(r0, wlen), :]
                    acc = win
                    for j in range(1, w):
                        acc = acc + pltpu.roll(win, wlen - j, 0)
                    dn_ref[pl.ds(r0, CHUNK), :] = acc[0:CHUNK, :] - dd[pl.ds(r0, CHUNK), :]
                    return 0

                lax.fori_loop(0, nch, chunk, 0, unroll=True)

    col = pl.BlockSpec((lp, g), lambda i: (0, i))
    vec = pl.BlockSpec((1, g), lambda i: (0, i))
    return pl.pallas_call(
        body, name=name, grid=(len(POOL_WINDOWS),),
        in_specs=[col, col, pl.BlockSpec((None, g, g), lambda i: (i, 0, 0)), vec],
        out_specs=[col, col, vec, vec],
        out_shape=[jax.ShapeDtypeStruct((lp, dm), F32), jax.ShapeDtypeStruct((lp, dm), BF16),
                   jax.ShapeDtypeStruct((1, dm), F32), jax.ShapeDtypeStruct((1, dm), F32)],
        scratch_shapes=[pltpu.VMEM((lp + POOL_PAD, g), F32), pltpu.VMEM((lp, g), F32)],
    )(dy, z, pool_w, pool_scale)


def _ffn_fwd(h, g, weight, w3, b3, tag):
    n, n_t = _rms_fwd(h, g, name=f"rms_ffn{tag}", out_dtype=BF16, transposed=True)
    w_up_t = weight("up", n)
    act, act_t, gv3, up3 = _ffn_up_act(n, w_up_t, w3, b3, name=f"ffn_up_act{tag}")
    w_down = weight("down", act)
    h_out = _mm(act, w_down, name=f"mm_down{tag}", tn=256, res=h)
    return h_out, (n_t, up3, gv3, act_t), w_up_t, w_down


def _ffn_bwd(dh, dhb, h, g, saved, w_up_t, w3, w_down, tag, after=None):
    lp = h.shape[0]
    n_t, up3, gv3, act_t = saved
    dw_down = _mm(act_t, dhb, name=f"mm_dwdown{tag}", tm=704, out_dtype=BF16)
    dact = _mm(dhb, w_down, name=f"mm_dact{tag}", tb=True, tn=D_FF // 2, out_dtype=BF16, after=after)
    dup3, dcw, dcb = _ffn_act_bwd(up3, gv3, w3, dact, name=f"ffn_act_bwd{tag}")
    dup2 = dup3.reshape(2 * lp, D_FF)
    dw_up = _mm_dw_up(n_t, dup2, name=f"mm_dwup{tag}")
    dh_in, dh_in_b, dg = _mm_rms_bwd(dup2, w_up_t, h, g, dh, name=f"mm_dnffn{tag}", tk=D_FF // 2,
                                     a_map=lambda i, kk: (4 * (kk // 2) + i, kk % 2))
    return dh_in, dh_in_b, (dg, dw_up, dcw, dcb, dw_down)


def _local_step(x, tgt, wt):
    seq = x.shape[0]
    n_real = N_META + seq
    lp = _round_up(n_real, CHUNK)
    pad = jnp.zeros((lp - n_real, D_MODEL), F32)
    h0 = jnp.concatenate([wt["meta"], x, pad], axis=0)
    tgt_p = jnp.concatenate([jnp.zeros((N_META, D_MODEL), F32), tgt, pad], axis=0)
    w_in_p = wt["w_in_p"]

    n0, n0_t = _rms_fwd(h0, wt["g_even"], name="rms_even", out_dtype=BF16, after=wt["ffn_started"], transposed=True)
    qkv = _mm(n0, w_in_p, name="mm_qkv", tn=512, dims=(lp, 3 * FOX_WIDTH, D_MODEL), out_dtype=BF16)
    ag = _mm(n0, w_in_p, name="mm_ag", tn=512, dims=(lp, 2 * CONV_CH, D_MODEL),
             b_map=lambda i, j, k: (0, 3 + j))
    f_t = _mm(wt["wf_t"], n0, name="mm_ft", tb=True)
    c_row = _fox_prep(f_t, wt["b_f"], name="fox_prep")
    c3 = c_row.reshape(4, 2, lp)
    o = _attn_fwd(qkv, c3, name="attn_fwd")
    u0, u1, u3 = _conv_fwd(ag, wt["conv_w"], wt["conv_b"], wt["ln_g"], wt["ln_b"], name="conv_fwd")
    cat = jnp.concatenate([o, u3], axis=1)
    h1 = _mm(cat, wt["w_out"], name="mm_out", tn=512, res=h0)
    h2, saved0, w_up0, w_down0 = _ffn_fwd(h1, wt["ffn_norm"][0:1], functools.partial(wt["ffn_weight"], 0),
                                          wt["fcw3"][0], wt["fcb3"][0], 0)

    n2 = _rms_fwd(h2, wt["g_odd"], name="rms_odd", out_dtype=F32)
    h3, dpool_t, z = _pool_fwd(n2, wt["pool_w"], wt["pool_b"], wt["pool_scale"], h2, name="pool_fwd")
    h4, saved1, w_up1, w_down1 = _ffn_fwd(h3, wt["ffn_norm"][1:2], functools.partial(wt["ffn_weight"], 1),
                                          wt["fcw3"][1], wt["fcb3"][1], 1)

    loss, dh4, dh4b, d_gfinal = _loss_head(h4, wt["g_final"], tgt_p, n_real, name="loss_head")

    dh3, dh3b, gf1 = _ffn_bwd(dh4, dh4b, h3, wt["ffn_norm"][1:2], saved1, w_up1, wt["fcw3"][1], w_down1, 1)
    send1, token1 = _send_ffn_grads(gf1[1], gf1[4], 1)
    dn2, dzb, d_pscale, d_pb = _pool_bwd(dh3, z, wt["pool_w"], wt["pool_scale"], name="pool_bwd")
    d_pw = _mm(dpool_t, dzb, name="mm_dpoolw", tm=POOL_GROUP, tn=POOL_GROUP, dims=(D_MODEL, POOL_GROUP, lp),
               b_map=lambda i, j, k: (0, i), o_map=lambda i, j, k: (i, 0), out_shape=(D_MODEL, POOL_GROUP),
               out_dtype=BF16)
    dh2, dh2b, d_godd = _rms_bwd(h2, wt["g_odd"], dn2, dh3, name="rms_bwd_odd")
    dh1, dh1b, gf0 = _ffn_bwd(dh2, dh2b, h1, wt["ffn_norm"][0:1], saved0, w_up0, wt["fcw3"][0], w_down0, 0,
                              after=token1)

    send0, token0 = _send_ffn_grads(gf0[1], gf0[4], 0)
    cat_t = _transpose(cat, name="t_cat", out_dtype=BF16)
    d_wout = _mm(cat_t, dh1b, name="mm_dwout", tm=512, out_dtype=BF16)
    dcat = _mm(dh1b, wt["w_out"], name="mm_dcat", tb=True, tn=512, after=token0)
    dag, d_convw, d_convb, d_lng, d_lnb = _conv_bwd(dcat, u0, u1, ag, wt["conv_w"], wt["ln_g"], wt["ln_b"],
                                                    name="conv_bwd")
    layers = lambda i: jnp.stack([gf0[i], gf1[i]])
    grads = dict(
        conv_w=d_convw[None], w_out=d_wout, mix_norm_odd=d_godd,
        pool_w=d_pw.reshape(len(POOL_WINDOWS), POOL_GROUP, POOL_GROUP),
        pool_b=d_pb.reshape(1, len(POOL_WINDOWS), POOL_GROUP), pool_scale=d_pscale, w_up=(gf0[1], gf1[1]),
        ffn_conv_w=layers(2).transpose(0, 2, 1, 3).reshape(DEPTH, FFN_CONV_WIDTH, 2 * D_FF), w_down=(gf0[4], gf1[4]))
    send_rest, token_rest = _send_rest_grads(grads)
    q_t = _transpose(qkv, name="t_q", out_dtype=BF16, cols=FOX_WIDTH, after=token_rest)
    do_t = _transpose(dcat, name="t_do", out_dtype=BF16, cols=FOX_WIDTH)
    dq, dk_t, dv_t, dc3 = _attn_bwd(qkv, q_t, dcat, do_t, c3, name="attn_bwd")
    dk = _transpose(dk_t, name="t_dk", out_dtype=BF16)
    dv = _transpose(dv_t, name="t_dv", out_dtype=BF16)
    df_t, d_bf = _fox_bwd(dc3.reshape(FOX_HEADS, lp), f_t, wt["b_f"], name="fox_bwd")
    df = _transpose(df_t, name="t_df", out_dtype=BF16)
    dproj = jnp.concatenate([dq, dk, dv, dag, df], axis=1)
    grads["w_in"] = _mm_dw_in(n0_t, dproj, name="mm_dwin")
    send_in, token_in = _send_start(
        [grads["w_in"]], [jax.ShapeDtypeStruct((N_DEV - 1, D_MODEL, _IN_SHARD), BF16)], [(0, _by_owner, 0, None)],
        name="send_w_in")
    dh0, _, d_geven = _mm_rms_bwd(dproj, w_in_p, h0, wt["g_even"], dh1, name="mm_dn0", tb=True, tk=896,
                                  after=token_in)
    grads.update(
        meta_tokens=dh0[0:N_META], mix_norm_even=d_geven, b_f=d_bf.reshape(1, FOX_HEADS), conv_b=d_convb, ln_g=d_lng,
        ln_b=d_lnb, ffn_norm=jnp.concatenate([gf0[0], gf1[0]], axis=0),
        ffn_conv_b=layers(3).reshape(DEPTH, 2 * D_FF), final_norm=d_gfinal.reshape(D_MODEL),
        sends=(send0, send1, send_rest, send_in))
    return loss, dh0[N_META:n_real], grads


_LITTLE = (("conv_w", (1, 31, 512), 2), ("mix_norm_odd", (1, 1024), 1), ("pool_b", (1, 4, 256), 2),
           ("pool_scale", (1, 1024), 1), ("ffn_conv_w", (2, 3, 5632), 2))


def _send_rest_grads(g):
    out_rows, pool_rows, groups = D_MODEL // N_DEV, POOL_GROUP // N_DEV, len(POOL_WINDOWS)
    little_slabs = _pack([_full_to_slabs(g[n], s, a) for n, s, a in _LITTLE], F32, lead=(N_DEV,), align=8)
    land = lambda shape, dtype: jax.ShapeDtypeStruct((N_DEV - 1,) + shape, dtype)
    handle, token = _send_start(
        [g["w_out"], g["pool_w"], little_slabs],
        [land((out_rows, D_MODEL), BF16), land((groups, pool_rows, POOL_GROUP), BF16), land(little_slabs.shape[1:], F32)],
        [(0, _row_block(out_rows), 0, None), (1, _row_block(pool_rows, axis=1), 1, None), (2, _by_owner, 2, None)],
        name="send_rest")
    return handle, token


def _send_ffn_grads(dw_up, dw_down, tag):
    rows = D_FF // N_DEV
    lands = [jax.ShapeDtypeStruct((N_DEV - 1,) + dw_up.shape[1:], BF16),
             jax.ShapeDtypeStruct((N_DEV - 1, rows, D_MODEL), BF16)]
    return _send_start([dw_up, dw_down], lands, [(0, _by_owner, 0, None), (1, _row_block(rows), 1, None)],
                       name=f"send_ffn{tag}")


_QKV = 3 * FOX_WIDTH
_GLU0 = _QKV + FOX_HEADS
_IN_COLS = _GLU0 + 2 * CONV_CH
_F_PAD = 128


_IN_SHARD = _IN_COLS // N_DEV
_UP_SHARD = 2 * D_FF // N_DEV
_ROW_TILE = 256


def _assemble_w_in(st, *, name):
    tr = _ROW_TILE

    def body(s_ref, o_ref):
        full = jnp.concatenate([s_ref[i].astype(F32) for i in range(N_DEV)], axis=1)
        parts = [full[:, :_QKV], full[:, _GLU0:], full[:, _QKV:_GLU0], jnp.zeros((tr, _F_PAD - FOX_HEADS), F32)]
        o_ref[...] = jnp.concatenate(parts, axis=1).astype(BF16)

    return pl.pallas_call(
        body, name=name, grid=(D_MODEL // tr,),
        in_specs=[pl.BlockSpec((N_DEV, tr, _IN_SHARD), lambda i: (0, i, 0))],
        out_specs=pl.BlockSpec((tr, _QKV + 2 * CONV_CH + _F_PAD), lambda i: (i, 0)),
        out_shape=jax.ShapeDtypeStruct((D_MODEL, _QKV + 2 * CONV_CH + _F_PAD), BF16),
    )(st)


def _mm_dw_in(n_t, dproj, *, name):
    dm, lp = n_t.shape
    tr = _ROW_TILE
    ag0 = _QKV + 2 * CONV_CH

    def body(a_ref, b_ref, o_ref):
        r = jnp.dot(a_ref[...], b_ref[...], preferred_element_type=F32)
        full = jnp.concatenate([r[:, :_QKV], r[:, ag0:ag0 + FOX_HEADS], r[:, _QKV:ag0]], axis=1)
        for i in range(N_DEV):
            o_ref[i] = full[:, i * _IN_SHARD:(i + 1) * _IN_SHARD].astype(BF16)

    return pl.pallas_call(
        body, name=name, grid=(dm // tr,),
        in_specs=[pl.BlockSpec((tr, lp), lambda i: (i, 0)), pl.BlockSpec(dproj.shape, lambda i: (0, 0))],
        out_specs=pl.BlockSpec((N_DEV, tr, _IN_SHARD), lambda i: (0, i, 0)),
        out_shape=jax.ShapeDtypeStruct((N_DEV, dm, _IN_SHARD), BF16),
    )(n_t, dproj)


def _mm_dw_up(n_t, dup2, *, name):
    dm, lp = n_t.shape
    pairs_per_half = D_FF // (2 * _UP_SHARD)

    def body(a_ref, b_ref, o_ref):
        r_t = jnp.dot(a_ref[...], b_ref[...], preferred_element_type=F32).T
        o_ref[0] = r_t[:_UP_SHARD, :].astype(BF16)
        o_ref[1] = r_t[_UP_SHARD:, :].astype(BF16)

    return pl.pallas_call(
        body, name=name, grid=(N_DEV // 2,),
        in_specs=[pl.BlockSpec((dm, lp), lambda p: (0, 0)),
                  pl.BlockSpec((lp, 2 * _UP_SHARD), lambda p: (p // pairs_per_half, p % pairs_per_half))],
        out_specs=pl.BlockSpec((2, _UP_SHARD, dm), lambda p: (p, 0, 0)),
        out_shape=jax.ShapeDtypeStruct((N_DEV, _UP_SHARD, dm), BF16),
    )(n_t, dup2)


MESH = pl.DeviceIdType.MESH
ANY = pl.BlockSpec(memory_space=pl.ANY)


def _slot(px, py, pc):
    return 4 * px + 2 * py + pc


def _by_owner(ref, slot):
    return ref.at[slot]


def _row_block(rows, axis=0):
    def place(ref, slot):
        idx = (slice(None),) * axis + (pl.ds(slot * rows, rows),)
        return ref.at[idx]
    return place


def _all_gather(arrs, out_shapes, places, *, name):
    n = len(arrs)

    def body(*refs):
        ins, outs = refs[:n], refs[n:2 * n]
        send_sems, recv_sems, local_sems = refs[2 * n:]
        x, y, c = lax.axis_index("x"), lax.axis_index("y"), lax.axis_index("c")
        me, sibling = (x, y, c), (x, y, 1 - c)
        chips = [(1 - x, y), (x, 1 - y), (1 - x, 1 - y)]

        def copy(a, k, block, to, from_input=False):
            dst = places[a](outs[a], _slot(*block))
            return pltpu.make_async_remote_copy(
                src_ref=ins[a] if from_input else dst, dst_ref=dst,
                send_sem=send_sems.at[7 * a + k], recv_sem=recv_sems.at[7 * a + k],
                device_id=to, device_id_type=MESH)

        own, sent = [], []
        for a in range(n):
            mine = pltpu.make_async_copy(ins[a], places[a](outs[a], _slot(*me)), local_sems.at[a])
            mine.start()
            own.append(mine)
            first = [copy(a, 0, me, sibling, True)]
            first += [copy(a, 1 + j, me, (*chip, c), True) for j, chip in enumerate(chips)]
            for cp in first:
                cp.start()
            sent += first
        for a in range(n):
            for j, chip in enumerate(chips):
                copy(a, 1 + j, (*chip, c), me).wait_recv()
                passed = copy(a, 4 + j, (*chip, c), sibling)
                passed.start()
                sent.append(passed)
        for a in range(n):
            copy(a, 0, sibling, me).wait_recv()
            for j, chip in enumerate(chips):
                copy(a, 4 + j, (*chip, 1 - c), me).wait_recv()
        for cp in sent:
            cp.wait_send()
        for cp in own:
            cp.wait()

    return pl.pallas_call(
        body, name=name,
        in_specs=[ANY] * n, out_specs=[ANY] * n,
        out_shape=[jax.ShapeDtypeStruct(s, a.dtype) for s, a in zip(out_shapes, arrs)],
        scratch_shapes=[pltpu.SemaphoreType.DMA((7 * n,)), pltpu.SemaphoreType.DMA((7 * n,)),
                        pltpu.SemaphoreType.DMA((n,))],
    )(*arrs)


HBM = pl.BlockSpec(memory_space=pltpu.HBM)
SEM = pl.BlockSpec(memory_space=pltpu.SEMAPHORE)
EFFECT = pltpu.SideEffectType.DATAFLOW_SIDE_EFFECTING


def _relation_copies(src_refs, land_refs, copies, send_sems, recv_sems):
    x, y, c = lax.axis_index("x"), lax.axis_index("y"), lax.axis_index("c")
    flip = lambda v, bit: 1 - v if bit else v
    out = []
    for k in range(1, N_DEV):
        p = (flip(x, k & 4), flip(y, k & 2), flip(c, k & 1))
        for j, (si, take, li, put) in enumerate(copies):
            sem = (k - 1) * len(copies) + j
            dst = land_refs[li].at[k - 1] if put is None else put(land_refs[li], _slot(x, y, c))
            out.append(pltpu.make_async_remote_copy(
                src_ref=take(src_refs[si], _slot(*p)), dst_ref=dst,
                send_sem=send_sems.at[sem], recv_sem=recv_sems.at[sem], device_id=p, device_id_type=MESH))
    return out


def _own_copies(src_refs, land_refs, copies, sems):
    me = _slot(lax.axis_index("x"), lax.axis_index("y"), lax.axis_index("c"))
    placed = [(si, take, li, put) for si, take, li, put in copies if put is not None]
    return [pltpu.make_async_copy(take(src_refs[si], me), put(land_refs[li], me),
                                  sems.at[(N_DEV - 1) * len(copies) + j])
            for j, (si, take, li, put) in enumerate(placed)]


def _send_start(srcs, land_structs, copies, *, name, after=None):
    ns, nl = len(srcs), len(land_structs)
    n_sem = (N_DEV - 1) * len(copies) + sum(put is not None for _, _, _, put in copies)
    behind = [] if after is None else [after]

    def body(*refs):
        first_out = ns + nl + len(behind)
        send_sems, recv_sems, token = refs[first_out], refs[first_out + 1], refs[-1]
        for cp in _relation_copies(refs[:ns], refs[ns:ns + nl], copies, send_sems, recv_sems):
            cp.start()
        for cp in _own_copies(refs[:ns], refs[ns:ns + nl], copies, send_sems):
            cp.start()
        token[...] = jnp.zeros_like(token)

    in_hbm = lambda a: pltpu.with_memory_space_constraint(a, pltpu.HBM)
    outs = pl.pallas_call(
        body, name=name,
        out_shape=(pltpu.SemaphoreType.DMA((n_sem,)), pltpu.SemaphoreType.DMA((n_sem,)),
                   *[pltpu.HBM(s.shape, s.dtype) for s in srcs],
                   *[pltpu.HBM(s.shape, s.dtype) for s in land_structs],
                   jax.ShapeDtypeStruct((8, 128), F32)),
        in_specs=(HBM,) * (ns + nl) + (ANY,) * len(behind),
        out_specs=(SEM, SEM) + (HBM,) * (ns + nl) + (pl.BlockSpec(memory_space=pltpu.VMEM),),
        input_output_aliases={i: 2 + i for i in range(ns + nl)},
        compiler_params=pltpu.CompilerParams(has_side_effects=EFFECT),
    )(*[in_hbm(s) for s in srcs], *[in_hbm(lax.empty(s.shape, s.dtype)) for s in land_structs], *behind)
    return (outs[0], outs[1], outs[2:2 + ns], outs[2 + ns:2 + ns + nl], copies), outs[-1]


def _send_wait(handle, after, *, name):
    send_sems, recv_sems, srcs, lands, copies = handle
    ns, nl = len(srcs), len(lands)

    def body(*refs):
        for cp in _relation_copies(refs[:ns], refs[ns:ns + nl], copies, refs[ns + nl], refs[ns + nl + 1]):
            cp.wait_send()
            cp.wait_recv()
        for cp in _own_copies(refs[:ns], refs[ns:ns + nl], copies, refs[ns + nl]):
            cp.wait()

    outs = pl.pallas_call(
        body, name=name,
        out_shape=tuple(pltpu.HBM(a.shape, a.dtype) for a in (*srcs, *lands)),
        in_specs=(HBM,) * (ns + nl) + (SEM, SEM, ANY), out_specs=(HBM,) * (ns + nl),
        input_output_aliases={i: i for i in range(ns + nl)},
        compiler_params=pltpu.CompilerParams(has_side_effects=EFFECT),
    )(*srcs, *lands, send_sems, recv_sems, after)
    return outs[:ns], outs[ns:]


def _sum_slabs(stack, *, name, own=None):
    n, rows, w = stack.shape

    def body(*refs):
        s_ref, o_ref = refs[-2], refs[-1]
        acc = s_ref[0] if own is None else refs[0][...] + s_ref[0]
        for i in range(1, n):
            acc = acc + s_ref[i]
        o_ref[...] = acc

    return pl.pallas_call(body, name=name, out_shape=jax.ShapeDtypeStruct((rows, w), F32))(
        *([] if own is None else [own]), stack)


def _adam_math(w, g, m, v):
    mn = ADAM_B1 * m + (1.0 - ADAM_B1) * g
    vn = ADAM_B2 * v + (1.0 - ADAM_B2) * (g * g)
    m_hat = mn / (1.0 - ADAM_B1 ** ADAM_STEP)
    v_hat = vn / (1.0 - ADAM_B2 ** ADAM_STEP)
    return -ADAM_LR * (m_hat / (jnp.sqrt(v_hat) + ADAM_EPS) + ADAM_WD * w), mn, vn


def _adamw_many(ws, gs, ms, vs, *, name):
    n = len(ws)

    def body(*refs):
        for i in range(n):
            w_ref, g_ref, m_ref, v_ref = (refs[j * n + i] for j in range(4))
            d_ref, mo_ref, vo_ref = refs[4 * n + 3 * i:4 * n + 3 * i + 3]
            d_ref[...], mo_ref[...], vo_ref[...] = _adam_math(w_ref[...], g_ref[...], m_ref[...], v_ref[...])

    return pl.pallas_call(
        body, name=name, out_shape=[jax.ShapeDtypeStruct(w.shape, F32) for w in ws for _ in range(3)],
    )(*ws, *gs, *ms, *vs)


def _adamw_layers(w, owns, lands, m, v, tr, *, name):
    nl, rows, cols = w.shape
    steps = rows // tr
    assert rows % tr == 0

    def body(*refs):
        w_ref, m_ref, v_ref = refs[:3]
        own_refs, land_refs = refs[3:3 + nl], refs[3 + nl:3 + 2 * nl]
        g_ref, d_ref, mo_ref, vo_ref = refs[3 + 2 * nl:]
        for li in range(nl):
            @pl.when(pl.program_id(0) == li)
            def _(li=li):
                g = own_refs[li][...].astype(F32)
                for k in range(N_DEV - 1):
                    g = g + land_refs[li][k].astype(F32)
                g_ref[...] = g
                d_ref[...], mo_ref[...], vo_ref[...] = _adam_math(w_ref[...], g, m_ref[...], v_ref[...])

    def held(li):
        return lambda l, i: jnp.where(l == li, i, jnp.where(l < li, 0, steps - 1))

    blk = pl.BlockSpec((None, tr, cols), lambda l, i: (l, i, 0))
    own_specs = [pl.BlockSpec((tr, cols), lambda l, i, f=held(li): (f(l, i), 0)) for li in range(nl)]
    land_specs = [pl.BlockSpec((N_DEV - 1, tr, cols), lambda l, i, f=held(li): (0, f(l, i), 0)) for li in range(nl)]
    return pl.pallas_call(
        body, name=name, grid=(nl, steps),
        in_specs=[blk, blk, blk] + own_specs + land_specs, out_specs=[blk] * 4,
        out_shape=[jax.ShapeDtypeStruct(w.shape, F32)] * 4,
    )(w, m, v, *owns, *lands)


_WEIGHTS = (
    ("meta_tokens", (16, 1024), 1), ("mix_norm_even", (1, 1024), None), ("w_in", (1, 1024, 2568), 2),
    ("b_f", (1, 8), None), ("conv_w", (1, 31, 512), 2), ("conv_b", (1, 512), None), ("ln_g", (1, 512), None),
    ("ln_b", (1, 512), None), ("w_out", (1, 1024, 1024), 1), ("mix_norm_odd", (1, 1024), 1),
    ("pool_w", (1, 4, 256, 256), 2), ("pool_b", (1, 4, 256), 2), ("pool_scale", (1, 1024), 1),
    ("ffn_norm", (2, 1024), None), ("w_up", (2, 1024, 5632), 2), ("ffn_conv_w", (2, 3, 5632), 2),
    ("ffn_conv_b", (2, 5632), None), ("w_down", (2, 2816, 1024), 1), ("final_norm", (1024,), None),
)
_MATMUL_WEIGHTS = ("w_in", "w_out", "pool_w", "w_up", "w_down")
_ADAM_ROWS = dict(w_in=256, w_out=128, pool_w=128, w_up=352, w_down=352)


def _shard_shape(shape, axis):
    return shape[:axis] + (shape[axis] // N_DEV,) + shape[axis + 1:]


def _size(shape):
    n = 1
    for s in shape:
        n *= s
    return n


def _pack(parts, dtype, lead=(), align=16):
    flat = jnp.concatenate([p.reshape(lead + (-1,)).astype(dtype) for p in parts], axis=-1)
    n = flat.shape[-1]
    rows = _round_up(-(-n // FLAT_W), align)
    flat = jnp.pad(flat, [(0, 0)] * len(lead) + [(0, rows * FLAT_W - n)])
    return flat.reshape(lead + (rows, FLAT_W))


def _unpack(buf, shapes, lead=()):
    flat = buf.reshape(lead + (-1,))
    out, off = [], 0
    for shp in shapes:
        n = _size(shp)
        out.append(flat[..., off:off + n].reshape(lead + shp))
        off += n
    return out


def _gathered_to_full(stack, shape, axis):
    return jnp.moveaxis(stack, 0, axis).reshape(shape)


def _full_to_slabs(full, shape, axis):
    split = shape[:axis] + (N_DEV, shape[axis] // N_DEV) + shape[axis + 1:]
    return jnp.moveaxis(full.reshape(split), axis, 0)


def kernel(x, meta_tokens, mix_norm_even, w_in, b_f, conv_w, conv_b, ln_g, ln_b, w_out, mix_norm_odd, pool_w, pool_b, pool_scale, ffn_norm, w_up, ffn_conv_w, ffn_conv_b, w_down, final_norm, loss_target, m_meta_tokens, m_mix_norm_even, m_w_in, m_b_f, m_conv_w, m_conv_b, m_ln_g, m_ln_b, m_w_out, m_mix_norm_odd, m_pool_w, m_pool_b, m_pool_scale, m_ffn_norm, m_w_up, m_ffn_conv_w, m_ffn_conv_b, m_w_down, m_final_norm, v_meta_tokens, v_mix_norm_even, v_w_in, v_b_f, v_conv_w, v_conv_b, v_ln_g, v_ln_b, v_w_out, v_mix_norm_odd, v_pool_w, v_pool_b, v_pool_scale, v_ffn_norm, v_w_up, v_ffn_conv_w, v_ffn_conv_b, v_w_down, v_final_norm):
    names = [n for n, _, _ in _WEIGHTS]
    w_loc = dict(zip(names, (meta_tokens, mix_norm_even, w_in, b_f, conv_w, conv_b, ln_g, ln_b, w_out, mix_norm_odd,
                             pool_w, pool_b, pool_scale, ffn_norm, w_up, ffn_conv_w, ffn_conv_b, w_down, final_norm)))
    m_loc = dict(zip(names, (m_meta_tokens, m_mix_norm_even, m_w_in, m_b_f, m_conv_w, m_conv_b, m_ln_g, m_ln_b,
                             m_w_out, m_mix_norm_odd, m_pool_w, m_pool_b, m_pool_scale, m_ffn_norm, m_w_up,
                             m_ffn_conv_w, m_ffn_conv_b, m_w_down, m_final_norm)))
    v_loc = dict(zip(names, (v_meta_tokens, v_mix_norm_even, v_w_in, v_b_f, v_conv_w, v_conv_b, v_ln_g, v_ln_b,
                             v_w_out, v_mix_norm_odd, v_pool_w, v_pool_b, v_pool_scale, v_ffn_norm, v_w_up,
                             v_ffn_conv_w, v_ffn_conv_b, v_w_down, v_final_norm)))
    replicated = [(n, s) for n, s, a in _WEIGHTS if a is None]
    little = [(n, s, a) for n, s, a in _WEIGHTS if a is not None and n not in _MATMUL_WEIGHTS]
    little_shards = [_shard_shape(s, a) for _, s, a in little]
    out_rows, down_rows, pool_rows = D_MODEL // N_DEV, D_FF // N_DEV, POOL_GROUP // N_DEV
    n_groups = len(POOL_WINDOWS)

    little_pack = _pack([w_loc[n] for n, _, _ in little], F32)
    g_win, g_wout, g_poolw, g_little = _all_gather(
        [w_in[0].astype(BF16), w_out[0].astype(BF16), pool_w[0].astype(BF16), little_pack],
        [(N_DEV, D_MODEL, _IN_SHARD), (D_MODEL, D_MODEL), (n_groups, POOL_GROUP, POOL_GROUP),
         (N_DEV,) + little_pack.shape],
        [_by_owner, _row_block(out_rows), _row_block(pool_rows, axis=1), _by_owner],
        name="gather_weights")
    me = _slot(lax.axis_index("x"), lax.axis_index("y"), lax.axis_index("c"))
    up_t = lambda a: jnp.transpose(a, (0, 2, 1))
    w_loc["w_up"], m_loc["w_up"], v_loc["w_up"] = up_t(w_up), up_t(m_w_up), up_t(v_w_up)
    w_up_b, w_down_b = w_loc["w_up"].astype(BF16), w_down.astype(BF16)
    whole = lambda ref, slot: ref
    ffn_gathers, behind = {}, g_little
    for l in range(DEPTH):
        for part, shard, rows in (("up", w_up_b[l], _UP_SHARD), ("down", w_down_b[l], down_rows)):
            ffn_gathers[l, part], behind = _send_start(
                [shard], [jax.ShapeDtypeStruct((N_DEV * rows, D_MODEL), BF16)], [(0, whole, 0, _row_block(rows))],
                name=f"gather_{part}{l}_start", after=behind)

    def ffn_weight(l, part, after):
        return _send_wait(ffn_gathers[l, part], after, name=f"gather_{part}{l}_wait")[1][0]

    w_in_p = _assemble_w_in(g_win, name="assemble_w_in")
    full = {n: _gathered_to_full(st, s, a)
            for (n, s, a), st in zip(little, _unpack(g_little, little_shards, lead=(N_DEV,)))}
    f0 = _QKV + 2 * CONV_CH
    wt = dict(
        meta=full["meta_tokens"], g_even=mix_norm_even, w_in_p=w_in_p, wf_t=w_in_p[:, f0:f0 + FOX_HEADS].T,
        b_f=b_f.reshape(FOX_HEADS, 1), conv_w=full["conv_w"][0], conv_b=conv_b, ln_g=ln_g, ln_b=ln_b, w_out=g_wout,
        g_odd=full["mix_norm_odd"], pool_w=g_poolw, pool_b=full["pool_b"].reshape(1, D_MODEL),
        pool_scale=full["pool_scale"], ffn_norm=ffn_norm, ffn_weight=ffn_weight, ffn_started=behind,
        fcw3=full["ffn_conv_w"].reshape(DEPTH, FFN_CONV_WIDTH, 2, D_FF).transpose(0, 2, 1, 3),
        fcb3=ffn_conv_b.reshape(DEPTH, 2, 1, D_FF), g_final=final_norm.reshape(1, D_MODEL))

    loss_part, grad_x, g = _local_step(x[0], loss_target[0], wt)

    small = _pack([loss_part[:, 0:1]] + [g[n] for n, _ in replicated] + [g["meta_tokens"]], F32, align=8)
    send_small, token_small = _send_start([small], [jax.ShapeDtypeStruct((N_DEV,) + small.shape, F32)],
                                          [(0, whole, 0, _by_owner)], name="send_small")

    grads, delta, new_m, new_v = {}, {}, {}, {}
    send0, send1, send_rest, send_in = g["sends"]
    ffn_sent = [_send_wait(send, token_small, name=f"wait_ffn{l}") for l, send in enumerate((send0, send1))]
    own_up = [lax.dynamic_index_in_dim(srcs[0], me, 0, keepdims=False) for srcs, _ in ffn_sent]
    own_down = [lax.dynamic_slice_in_dim(srcs[1], me * down_rows, down_rows, 0) for srcs, _ in ffn_sent]
    for n, owns, idx in (("w_up", own_up, 0), ("w_down", own_down, 1)):
        grads[n], delta[n], new_m[n], new_v[n] = _adamw_layers(
            w_loc[n], owns, [lands[idx] for _, lands in ffn_sent], m_loc[n], v_loc[n], _ADAM_ROWS[n],
            name=f"adamw_{n}")
    for d in (grads, delta, new_m, new_v):
        d["w_up"] = up_t(d["w_up"])
    (d_out, d_pool, little_slabs), (land_out, land_pool, land_little) = _send_wait(
        send_rest, delta["w_down"], name="wait_rest")
    (d_in,), (land_in,) = _send_wait(send_in, land_out, name="wait_w_in")
    pool_2d = (n_groups * pool_rows, POOL_GROUP)
    own_pool = lax.dynamic_slice_in_dim(d_pool, me * pool_rows, pool_rows, 1)
    for n, own, land, shp in (
            ("w_in", lax.dynamic_index_in_dim(d_in, me, 0, keepdims=False), land_in, w_in.shape),
            ("w_out", lax.dynamic_slice_in_dim(d_out, me * out_rows, out_rows, 0), land_out, w_out.shape),
            ("pool_w", own_pool.reshape(pool_2d), land_pool.reshape((N_DEV - 1,) + pool_2d), (1,) + pool_2d)):
        outs = _adamw_layers(w_loc[n].reshape(shp), [own], [land], m_loc[n].reshape(shp), v_loc[n].reshape(shp),
                             _ADAM_ROWS[n], name=f"adamw_{n}")
        grads[n], delta[n], new_m[n], new_v[n] = (o.reshape(w_loc[n].shape) for o in outs)
    own_little = lax.dynamic_index_in_dim(little_slabs, me, 0, keepdims=False)
    g_little = _unpack(_sum_slabs(land_little, own=own_little, name="sum_little"),
                       [_shard_shape(s, a) for _, s, a in _LITTLE])
    grads.update({n: gl for (n, _, _), gl in zip(_LITTLE, g_little)})
    _, (everyone,) = _send_wait(send_small, delta["w_in"], name="wait_small")
    summed = _unpack(_sum_slabs(everyone, name="sum_small"),
                     [(1, 1)] + [s for _, s in replicated] + [(N_META, D_MODEL)])
    loss = summed[0].reshape(())
    grads.update({n: gr for (n, _), gr in zip(replicated, summed[1:-1])})
    grads["meta_tokens"] = lax.dynamic_slice_in_dim(summed[-1], me * out_rows, out_rows, 1)
    at_least_2d = lambda a: a.reshape((1,) * (2 - a.ndim) + a.shape)
    rest = [n for n in names if n not in _MATMUL_WEIGHTS]
    outs = _adamw_many([at_least_2d(w_loc[n]) for n in rest], [at_least_2d(grads[n]) for n in rest],
                       [at_least_2d(m_loc[n]) for n in rest], [at_least_2d(v_loc[n]) for n in rest],
                       name="adamw_rest")
    for i, n in enumerate(rest):
        delta[n], new_m[n], new_v[n] = (o.reshape(w_loc[n].shape) for o in outs[3 * i:3 * i + 3])
    return (loss, grad_x[None], *[grads[n] for n in names], *[delta[n] for n in names],
            *[new_m[n] for n in names], *[new_v[n] for n in names])
```

```python
import functools

import jax
import jax.numpy as jnp
from jax import lax
from jax.experimental import pallas as pl
from jax.experimental.pallas import tpu as pltpu

F32 = jnp.float32
BF16 = jnp.bfloat16

N_DEV = 8
DEPTH = 2
D_MODEL = 1024
N_META = 16
FOX_HEADS = 8
FOX_HEAD_DIM = 64
FOX_WIDTH = 512
CONV_CH = 512
CONV_WIDTH = 31
POOL_WINDOWS = (2, 4, 8, 16)
POOL_GROUP = 256
D_FF = 2816
FFN_CONV_WIDTH = 3
RMS_EPS = 1e-6
LN_EPS = 1e-5
ADAM_LR = 0.001
ADAM_B1 = 0.9
ADAM_B2 = 0.999
ADAM_EPS = 1e-08
ADAM_WD = 0.01
ADAM_STEP = 10

CHUNK = 128
HALO = 32
NEG_BIG = -1e30
FLAT_W = 1024


def _round_up(n, m):
    return (n + m - 1) // m * m


def _sigmoid(x):
    return 1.0 / (1.0 + jnp.exp(-x))


def _fold8(p):
    acc = p[0:8, :]
    for r in range(1, p.shape[0] // 8):
        acc = acc + p[8 * r:8 * r + 8, :]
    return acc


def _mm(a, b, *, name, tb=False, tm=None, tn=None, tk=None, out_dtype=F32, res=None,
        a_map=None, b_map=None, o_map=None, out_shape=None, dims=None, after=None, first_tile_t=False):
    if dims is None:
        m, k = a.shape
        n = b.shape[-2] if tb else b.shape[-1]
    else:
        m, n, k = dims
    tm, tn, tk = tm or m, tn or n, tk or k
    assert m % tm == 0 and n % tn == 0 and k % tk == 0, (name, m, n, k, tm, tn, tk)
    nk = k // tk
    a_map = a_map or (lambda i, j, kk: (i, kk))
    b_map = b_map or ((lambda i, j, kk: (j, kk)) if tb else (lambda i, j, kk: (kk, j)))
    o_map = o_map or (lambda i, j, kk: (i, j))
    out_shape = out_shape or (m, n)
    contract = (((1,), (1,)), ((), ())) if tb else (((1,), (0,)), ((), ()))
    has_res = res is not None

    def body(*refs):
        a_ref, b_ref = refs[0], refs[1]
        res_ref = refs[2] if has_res else None
        o_ref = refs[2 + has_res + (after is not None)]
        p = lax.dot_general(a_ref[...], b_ref[...], contract, preferred_element_type=F32)
        if nk == 1:
            if has_res:
                p = p + res_ref[...]
            o_ref[...] = p.astype(o_ref.dtype)
            if first_tile_t:
                @pl.when(pl.program_id(1) == 0)
                def _():
                    refs[3 + has_res + (after is not None)][...] = p.astype(BF16).T
        else:
            acc_ref = refs[-1]
            kk = pl.program_id(2)

            @pl.when(kk == 0)
            def _():
                acc_ref[...] = p

            @pl.when(kk > 0)
            def _():
                acc_ref[...] += p

            @pl.when(kk == nk - 1)
            def _():
                r = acc_ref[...]
                if has_res:
                    r = r + res_ref[...]
                o_ref[...] = r.astype(o_ref.dtype)

    in_specs = [pl.BlockSpec((tm, tk), a_map), pl.BlockSpec((tn, tk) if tb else (tk, tn), b_map)]
    operands = [a, b]
    if has_res:
        in_specs.append(pl.BlockSpec((tm, tn), o_map))
        operands.append(res)
    if after is not None:
        in_specs.append(pl.BlockSpec(memory_space=pl.ANY))
        operands.append(after)
    out_specs, out_struct = pl.BlockSpec((tm, tn), o_map), jax.ShapeDtypeStruct(out_shape, out_dtype)
    if first_tile_t:
        assert tm == m and nk == 1
        out_specs = [out_specs, pl.BlockSpec((tn, m), lambda i, j, kk: (0, 0))]
        out_struct = [out_struct, jax.ShapeDtypeStruct((tn, m), BF16)]
    return pl.pallas_call(
        body, name=name, grid=(m // tm, n // tn, nk),
        in_specs=in_specs, out_specs=out_specs, out_shape=out_struct,
        scratch_shapes=[pltpu.VMEM((tm, tn), F32)] if nk > 1 else [],
    )(*operands)


def _transpose(x, *, name, out_dtype):
    r, c = x.shape
    assert r % CHUNK == 0

    def body(x_ref, o_ref):
        o_ref[...] = x_ref[...].astype(o_ref.dtype).T

    return pl.pallas_call(
        body, name=name, grid=(r // CHUNK,),
        in_specs=[pl.BlockSpec((CHUNK, c), lambda i: (i, 0))],
        out_specs=pl.BlockSpec((c, CHUNK), lambda i: (0, i)),
        out_shape=jax.ShapeDtypeStruct((c, r), out_dtype),
    )(x)


def _rms_fwd(x, g, *, name, out_dtype, after=None, transposed=False):
    lp, dm = x.shape
    tr = CHUNK if transposed else lp // 4
    behind = [] if after is None else [after]

    def body(x_ref, g_ref, *rest):
        xv = x_ref[...]
        r = lax.rsqrt(jnp.mean(xv * xv, axis=-1, keepdims=True) + RMS_EPS)
        y = xv * r * g_ref[...]
        if transposed:
            yb = y.astype(out_dtype)
            rest[-2][...] = yb
            rest[-1][...] = yb.T
        else:
            rest[-1][...] = y.astype(out_dtype)

    row = pl.BlockSpec((tr, dm), lambda i: (i, 0))
    out_specs, out_shape = row, jax.ShapeDtypeStruct((lp, dm), out_dtype)
    if transposed:
        out_specs = [row, pl.BlockSpec((dm, tr), lambda i: (0, i))]
        out_shape = [out_shape, jax.ShapeDtypeStruct((dm, lp), out_dtype)]
    return pl.pallas_call(
        body, name=name, grid=(lp // tr,),
        in_specs=[row, pl.BlockSpec((1, dm), lambda i: (0, 0))] + [pl.BlockSpec(memory_space=pl.ANY)] * len(behind),
        out_specs=out_specs, out_shape=out_shape,
    )(x, g, *behind)


def _rms_bwd_rows(x_ref, g_ref, dnv, dres_ref, dh_ref, dhb_ref, dg_ref, first):
    xv = x_ref[...]
    r = lax.rsqrt(jnp.mean(xv * xv, axis=-1, keepdims=True) + RMS_EPS)
    xhat = xv * r

    @pl.when(first)
    def _():
        dg_ref[...] = jnp.zeros_like(dg_ref)

    dg_ref[...] += jnp.sum(dnv * xhat, axis=0, keepdims=True)
    dxhat = dnv * g_ref[...]
    dx = r * (dxhat - xhat * jnp.mean(dxhat * xhat, axis=-1, keepdims=True))
    dh = dres_ref[...] + dx
    dh_ref[...] = dh
    dhb_ref[...] = dh.astype(BF16)


def _mm_rms_bwd(a, b, x, g, dres, *, name, tk, tb=False, a_map=None, after=None):
    lp, dm = x.shape
    tm = lp // 4
    nk = (b.shape[1] if tb else b.shape[0]) // tk
    a_map = a_map or (lambda i, kk: (i, kk))
    contract = (((1,), (1,)), ((), ())) if tb else (((1,), (0,)), ((), ()))
    behind = [] if after is None else [after]

    def body(a_ref, b_ref, x_ref, g_ref, dres_ref, *rest):
        dh_ref, dhb_ref, dg_ref, acc_ref = rest[len(behind):]
        i, kk = pl.program_id(0), pl.program_id(1)
        p = lax.dot_general(a_ref[...], b_ref[...], contract, preferred_element_type=F32)

        @pl.when(kk == 0)
        def _():
            acc_ref[...] = p

        @pl.when(kk > 0)
        def _():
            acc_ref[...] += p

        @pl.when(kk == nk - 1)
        def _():
            _rms_bwd_rows(x_ref, g_ref, acc_ref[...], dres_ref, dh_ref, dhb_ref, dg_ref, i == 0)

    row = pl.BlockSpec((tm, dm), lambda i, kk: (i, 0))
    vec = pl.BlockSpec((1, dm), lambda i, kk: (0, 0))
    b_spec = pl.BlockSpec((dm, tk), lambda i, kk: (0, kk)) if tb else pl.BlockSpec((tk, dm), lambda i, kk: (kk, 0))
    return pl.pallas_call(
        body, name=name, grid=(lp // tm, nk),
        in_specs=[pl.BlockSpec((tm, tk), a_map), b_spec, row, vec, row] + [pl.BlockSpec(memory_space=pl.ANY)] * len(behind),
        out_specs=[row, row, vec],
        out_shape=[jax.ShapeDtypeStruct((lp, dm), F32), jax.ShapeDtypeStruct((lp, dm), BF16),
                   jax.ShapeDtypeStruct((1, dm), F32)],
        scratch_shapes=[pltpu.VMEM((tm, dm), F32)],
    )(a, b, x, g, dres, *behind)


def _rms_bwd(x, g, dn, dres, *, name):
    lp, dm = x.shape
    tr = lp // 4

    def body(x_ref, g_ref, dn_ref, dres_ref, dh_ref, dhb_ref, dg_ref):
        _rms_bwd_rows(x_ref, g_ref, dn_ref[...], dres_ref, dh_ref, dhb_ref, dg_ref, pl.program_id(0) == 0)

    row = pl.BlockSpec((tr, dm), lambda i: (i, 0))
    vec = pl.BlockSpec((1, dm), lambda i: (0, 0))
    return pl.pallas_call(
        body, name=name, grid=(lp // tr,),
        in_specs=[row, vec, row, row], out_specs=[row, row, vec],
        out_shape=[jax.ShapeDtypeStruct((lp, dm), F32), jax.ShapeDtypeStruct((lp, dm), BF16),
                   jax.ShapeDtypeStruct((1, dm), F32)],
    )(x, g, dn, dres)


def _loss_head(h, g, tgt, n_real, *, name):
    lp, dm = h.shape
    tr = lp // 4

    def body(x_ref, g_ref, t_ref, loss_ref, dh_ref, dhb_ref, dg_ref):
        i = pl.program_id(0)
        xv = x_ref[...]
        r = lax.rsqrt(jnp.mean(xv * xv, axis=-1, keepdims=True) + RMS_EPS)
        xhat = xv * r
        gv = g_ref[...]
        y = xhat * gv
        t = i * tr + lax.broadcasted_iota(jnp.int32, (tr, 1), 0)
        valid = (t >= N_META) & (t < n_real)
        diff = jnp.where(valid, y - t_ref[...], 0.0)

        @pl.when(i == 0)
        def _():
            loss_ref[...] = jnp.zeros_like(loss_ref)
            dg_ref[...] = jnp.zeros_like(dg_ref)

        row_sq = jnp.sum(diff * diff, axis=-1, keepdims=True) * (1.0 / dm)
        part = 0.5 * jnp.sum(row_sq, axis=0, keepdims=True)
        loss_ref[...] += jnp.broadcast_to(part, loss_ref.shape)
        dy = diff * (1.0 / dm)
        dg_ref[...] += jnp.sum(dy * xhat, axis=0, keepdims=True)
        dxhat = dy * gv
        dx = r * (dxhat - xhat * jnp.mean(dxhat * xhat, axis=-1, keepdims=True))
        dh_ref[...] = dx
        dhb_ref[...] = dx.astype(BF16)

    row = pl.BlockSpec((tr, dm), lambda i: (i, 0))
    vec = pl.BlockSpec((1, dm), lambda i: (0, 0))
    return pl.pallas_call(
        body, name=name, grid=(lp // tr,),
        in_specs=[row, vec, row],
        out_specs=[pl.BlockSpec((1, 128), lambda i: (0, 0)), row, row, vec],
        out_shape=[jax.ShapeDtypeStruct((1, 128), F32), jax.ShapeDtypeStruct((lp, dm), F32),
                   jax.ShapeDtypeStruct((lp, dm), BF16), jax.ShapeDtypeStruct((1, dm), F32)],
    )(h, g, tgt)


def _tri(upper):
    r = lax.broadcasted_iota(jnp.int32, (CHUNK, CHUNK), 0)
    c = lax.broadcasted_iota(jnp.int32, (CHUNK, CHUNK), 1)
    return jnp.where(r <= c if upper else r >= c, 1.0, 0.0).astype(F32)


def _fox_prep(f_t, b_f, *, name):
    nh, lp = f_t.shape
    nch = lp // CHUNK

    def body(f_ref, b_ref, c_ref):
        tri = _tri(True)
        carry = jnp.zeros((nh, 1), F32)
        for blk in range(nch):
            cols = slice(blk * CHUNK, (blk + 1) * CHUNK)
            z = f_ref[:, cols] + b_ref[...]
            logf = jnp.minimum(z, 0.0) - jnp.log(1.0 + jnp.exp(-jnp.abs(z)))
            cb = jnp.dot(logf, tri, preferred_element_type=F32, precision=lax.Precision.HIGHEST)
            c_ref[:, cols] = cb + carry
            carry = carry + jnp.sum(logf, axis=1, keepdims=True)

    return pl.pallas_call(
        body, name=name, out_shape=jax.ShapeDtypeStruct((nh, lp), F32),
    )(f_t, b_f)


def _fox_bwd(dc, f_t, b_f, *, name):
    nh, lp = f_t.shape
    nch = lp // CHUNK

    def body(dc_ref, f_ref, b_ref, df_ref, db_ref):
        tri = _tri(False)
        carry = jnp.zeros((nh, 1), F32)
        db = jnp.zeros((nh, 1), F32)
        df_ref[...] = jnp.zeros_like(df_ref)
        for blk in reversed(range(nch)):
            cols = slice(blk * CHUNK, (blk + 1) * CHUNK)
            dcb = dc_ref[:, cols]
            dlogf = jnp.dot(dcb, tri, preferred_element_type=F32, precision=lax.Precision.HIGHEST) + carry
            carry = carry + jnp.sum(dcb, axis=1, keepdims=True)
            z = f_ref[:, cols] + b_ref[...]
            dz = dlogf * _sigmoid(-z)
            df_ref[0:nh, cols] = dz
            db = db + jnp.sum(dz, axis=1, keepdims=True)
        db_ref[...] = db

    return pl.pallas_call(
        body, name=name,
        out_shape=[jax.ShapeDtypeStruct((128, lp), F32), jax.ShapeDtypeStruct((nh, 1), F32)],
    )(dc, f_t, b_f)


ATTN_BLOCKS = 4


def _attn_blocks(lp):
    tq = lp // ATTN_BLOCKS
    return tq, [(i * tq, min(lp, _round_up((i + 1) * tq, CHUNK))) for i in range(ATTN_BLOCKS)]


ATTN_SCALE = FOX_HEAD_DIM ** -0.5


def _attn_probs(q2s, k_h, c_row, row0, n):
    tq = q2s.shape[0]
    lo = row0 // CHUNK * CHUNK
    logits = []
    for c0, c1 in ([(0, lo)] if lo else []) + [(lo, n)]:
        s = lax.dot_general(q2s, k_h[c0:c1], (((1,), (1,)), ((), ())), preferred_element_type=F32) - c_row[:, c0:c1]
        if c1 > row0:
            t = row0 + lax.broadcasted_iota(jnp.int32, (tq, c1 - c0), 0)
            sidx = c0 + lax.broadcasted_iota(jnp.int32, (tq, c1 - c0), 1)
            s = jnp.where(sidx <= t, s, NEG_BIG)
        logits.append((s, c0, c1))
    m = functools.reduce(jnp.maximum, [jnp.max(s, axis=1, keepdims=True) for s, _, _ in logits])
    ps = [(jnp.exp(s - m), c0, c1) for s, c0, c1 in logits]
    inv = 1.0 / sum(jnp.sum(p, axis=1, keepdims=True) for p, _, _ in ps)
    return [(p * inv, c0, c1) for p, c0, c1 in ps]


def _attn_fwd(qkv, c3, *, name):
    lp = qkv.shape[0]
    tq, blocks = _attn_blocks(lp)

    def body(q_ref, k_ref, v_ref, c_ref, o_ref):
        lane = lax.broadcasted_iota(jnp.int32, (1, 128), 1)
        zero = jnp.zeros((), BF16)
        for i, (row0, n) in enumerate(blocks):
            q2s = q_ref[row0:row0 + tq, :] * ATTN_SCALE
            acc = jnp.zeros((tq, 128), F32)
            for hd in range(2):
                sel = (lane < 64) if hd == 0 else (lane >= 64)
                k_h = jnp.where(sel, k_ref[0:n, :], zero)
                v_h = jnp.where(sel, v_ref[0:n, :], zero)
                for p, c0, c1 in _attn_probs(q2s, k_h, c_ref[hd:hd + 1, 0:n], row0, n):
                    acc = acc + jnp.dot(p.astype(BF16), v_h[c0:c1], preferred_element_type=F32)
            o_ref[row0:row0 + tq, :] = acc.astype(BF16)

    blk = lambda off: pl.BlockSpec((lp, 128), lambda p: (0, off + p))
    return pl.pallas_call(
        body, name=name, grid=(4,),
        in_specs=[blk(0), blk(4), blk(8), pl.BlockSpec((None, 2, lp), lambda p: (p, 0, 0))],
        out_specs=pl.BlockSpec((lp, 128), lambda p: (0, p)),
        out_shape=jax.ShapeDtypeStruct((lp, FOX_WIDTH), BF16),
    )(qkv, qkv, qkv, c3)


def _attn_bwd(qkv, q_t, dcat, do_t, c3, *, name, after):
    lp = qkv.shape[0]
    tq, blocks = _attn_blocks(lp)
    scale = FOX_HEAD_DIM ** -0.5

    def body(q_ref, k_ref, v_ref, qt_ref, do_ref, dot_ref, c_ref, _, dq_ref, dk_ref, dv_ref, dc_ref,
             dkt_acc, dvt_acc):
        lane = lax.broadcasted_iota(jnp.int32, (1, 128), 1)
        sub = lax.broadcasted_iota(jnp.int32, (128, 1), 0)
        zero = jnp.zeros((), BF16)
        dkt_acc[...] = jnp.zeros_like(dkt_acc)
        dvt_acc[...] = jnp.zeros_like(dvt_acc)
        dc_ref[...] = jnp.zeros_like(dc_ref)
        for i, (row0, n) in enumerate(blocks):
            rows = slice(row0, row0 + tq)
            q2s = q_ref[rows, :] * ATTN_SCALE
            do2 = do_ref[rows, :].astype(BF16)
            dq_acc = jnp.zeros((tq, 128), F32)
            for hd in range(2):
                sel = (lane < 64) if hd == 0 else (lane >= 64)
                sel_t = (sub < 64) if hd == 0 else (sub >= 64)
                k_h = jnp.where(sel, k_ref[0:n, :], zero)
                v_h = jnp.where(sel, v_ref[0:n, :], zero)
                qt_h = jnp.where(sel_t, qt_ref[:, rows], zero)
                dot_h = jnp.where(sel_t, dot_ref[:, rows], zero)
                segs = [(p, lax.dot_general(do2, v_h[c0:c1], (((1,), (1,)), ((), ())), preferred_element_type=F32),
                         c0, c1) for p, c0, c1 in _attn_probs(q2s, k_h, c_ref[hd:hd + 1, 0:n], row0, n)]
                delta = sum(jnp.sum(p * dp, axis=1, keepdims=True) for p, dp, _, _ in segs)
                for p, dp, c0, c1 in segs:
                    ds = p * (dp - delta)
                    dsb = ds.astype(BF16)
                    dq_acc = dq_acc + jnp.dot(dsb, k_h[c0:c1], preferred_element_type=F32)
                    dkt_acc[:, c0:c1] += jnp.dot(qt_h, dsb, preferred_element_type=F32)
                    dvt_acc[:, c0:c1] += jnp.dot(dot_h, p.astype(BF16), preferred_element_type=F32)
                    dc_ref[hd:hd + 1, c0:c1] -= jnp.sum(ds, axis=0, keepdims=True)
            dq_ref[rows, :] = (dq_acc * scale).astype(BF16)
        dk_ref[...] = (dkt_acc[...] * scale).astype(BF16).T
        dv_ref[...] = dvt_acc[...].astype(BF16).T

    blk = lambda off: pl.BlockSpec((lp, 128), lambda p: (0, off + p))
    blk_t = pl.BlockSpec((128, lp), lambda p: (p, 0))
    c_spec = pl.BlockSpec((None, 2, lp), lambda p: (p, 0, 0))
    return pl.pallas_call(
        body, name=name, grid=(4,),
        in_specs=[blk(0), blk(4), blk(8), blk_t, blk(0), blk_t, c_spec, pl.BlockSpec(memory_space=pl.ANY)],
        out_specs=[blk(0), blk(0), blk(0), c_spec],
        out_shape=[jax.ShapeDtypeStruct((lp, FOX_WIDTH), BF16)] * 3 + [jax.ShapeDtypeStruct((4, 2, lp), F32)],
        scratch_shapes=[pltpu.VMEM((128, lp), F32), pltpu.VMEM((128, lp), F32)],
    )(qkv, qkv, qkv, q_t, dcat, do_t, c3, after)


def _ln_stats(x):
    mu = jnp.mean(x, axis=-1, keepdims=True)
    xc = x - mu
    var = jnp.mean(xc * xc, axis=-1, keepdims=True)
    rstd = lax.rsqrt(var + LN_EPS)
    return xc * rstd, rstd


def _conv_fwd(agf, conv_w, conv_b, ln_g, ln_b, *, name):
    lp = agf.shape[0]
    nch = lp // CHUNK
    c = CONV_CH

    def body(a_ref, g_ref, w_ref, b_ref, lg_ref, lb_ref, u0_ref, u1_ref, u3_ref, u0s):
        u0s[0:HALO, :] = jnp.zeros((HALO, c), F32)

        def glu(ci, _):
            rows = pl.ds(pl.multiple_of(ci * CHUNK, CHUNK), CHUNK)
            u0 = a_ref[rows, :] * _sigmoid(g_ref[rows, :])
            u0_ref[rows, :] = u0
            u0s[pl.ds(pl.multiple_of(ci * CHUNK + HALO, 8), CHUNK), :] = u0
            return 0

        lax.fori_loop(0, nch, glu, 0)

        def conv(ci, _):
            r0 = pl.multiple_of(ci * CHUNK, CHUNK)
            rows = pl.ds(r0, CHUNK)
            for lg in range(c // 128):
                lanes = slice(lg * 128, (lg + 1) * 128)
                win = u0s[pl.ds(r0, CHUNK + HALO), lanes]
                acc = jnp.broadcast_to(b_ref[:, lanes], (CHUNK, 128))
                for k in range(CONV_WIDTH):
                    s = CONV_WIDTH - 1 - k
                    sh = win if s == 0 else pltpu.roll(win, s, 0)
                    acc = acc + w_ref[k:k + 1, lanes] * sh[HALO:HALO + CHUNK, :]
                u1_ref[rows, lanes] = acc
            xhat, _ = _ln_stats(u1_ref[rows, :])
            y = xhat * lg_ref[...] + lb_ref[...]
            u3_ref[rows, :] = (y * _sigmoid(y)).astype(BF16)
            return 0

        lax.fori_loop(0, nch, conv, 0)

    full = lambda shape: pl.BlockSpec(shape, lambda i: (0, 0))
    return pl.pallas_call(
        body, name=name, grid=(1,),
        in_specs=[pl.BlockSpec((lp, c), lambda i: (0, 0)), pl.BlockSpec((lp, c), lambda i: (0, 1)),
                  full((CONV_WIDTH, c)), full((1, c)), full((1, c)), full((1, c))],
        out_specs=[full((lp, c)), full((lp, c)), full((lp, c))],
        out_shape=[jax.ShapeDtypeStruct((lp, c), F32), jax.ShapeDtypeStruct((lp, c), F32),
                   jax.ShapeDtypeStruct((lp, c), BF16)],
        scratch_shapes=[pltpu.VMEM((lp + HALO, c), F32)],
    )(agf, agf, conv_w, conv_b, ln_g, ln_b)


def _conv_bwd(dcat, u0, u1, agf, conv_w, ln_g, ln_b, *, name):
    lp = agf.shape[0]
    nch = lp // CHUNK
    c = CONV_CH
    wlen = CHUNK + HALO

    def body(du3_ref, u0_ref, u1_ref, a_ref, g_ref, w_ref, lg_ref, lb_ref,
             dag_ref, dw_ref, db_ref, dlg_ref, dlb_ref, du1s, dwacc, vacc):
        du1s[lp:lp + HALO, :] = jnp.zeros((HALO, c), F32)
        dwacc[...] = jnp.zeros_like(dwacc)
        vacc[...] = jnp.zeros_like(vacc)

        def ln_bwd(ci, _):
            r0 = pl.multiple_of(ci * CHUNK, CHUNK)
            rows = pl.ds(r0, CHUNK)
            xhat, rstd = _ln_stats(u1_ref[rows, :])
            y = xhat * lg_ref[...] + lb_ref[...]
            sg = _sigmoid(y)
            du2 = du3_ref[rows, :] * (sg * (1.0 + y * (1.0 - sg)))
            vacc[0:8, :] += _fold8(du2 * xhat)
            vacc[8:16, :] += _fold8(du2)
            dxhat = du2 * lg_ref[...]
            du1 = rstd * (dxhat - jnp.mean(dxhat, axis=-1, keepdims=True)
                          - xhat * jnp.mean(dxhat * xhat, axis=-1, keepdims=True))
            vacc[16:24, :] += _fold8(du1)
            du1s[rows, :] = du1
            return 0

        lax.fori_loop(0, nch, ln_bwd, 0)

        def conv_bwd(ci, _):
            r0 = pl.multiple_of(ci * CHUNK, CHUNK)
            rows = pl.ds(r0, CHUNK)
            for lg in range(c // 128):
                lanes = slice(lg * 128, (lg + 1) * 128)
                dwin = du1s[pl.ds(r0, wlen), lanes]
                u0 = u0_ref[rows, lanes]
                acc = jnp.zeros((CHUNK, 128), F32)
                for k in range(CONV_WIDTH):
                    s = CONV_WIDTH - 1 - k
                    d_s = (dwin if s == 0 else pltpu.roll(dwin, wlen - s, 0))[0:CHUNK, :]
                    acc = acc + w_ref[k:k + 1, lanes] * d_s
                    dwacc[8 * k:8 * k + 8, lanes] += _fold8(d_s * u0)
                sg = _sigmoid(g_ref[rows, lanes])
                a = a_ref[rows, lanes]
                dag_ref[rows, lanes] = (acc * sg).astype(BF16)
                dag_ref[rows, slice(c + lg * 128, c + (lg + 1) * 128)] = (acc * a * sg * (1.0 - sg)).astype(BF16)
            return 0

        lax.fori_loop(0, nch, conv_bwd, 0)
        for k in range(CONV_WIDTH):
            dw_ref[k:k + 1, :] = jnp.sum(dwacc[8 * k:8 * k + 8, :], axis=0, keepdims=True)
        dlg_ref[...] = jnp.sum(vacc[0:8, :], axis=0, keepdims=True)
        dlb_ref[...] = jnp.sum(vacc[8:16, :], axis=0, keepdims=True)
        db_ref[...] = jnp.sum(vacc[16:24, :], axis=0, keepdims=True)

    full = lambda shape: pl.BlockSpec(shape, lambda i: (0, 0))
    vec = jax.ShapeDtypeStruct((1, c), F32)
    return pl.pallas_call(
        body, name=name, grid=(1,),
        in_specs=[pl.BlockSpec((lp, c), lambda i: (0, 1)), full((lp, c)), full((lp, c)),
                  pl.BlockSpec((lp, c), lambda i: (0, 0)), pl.BlockSpec((lp, c), lambda i: (0, 1)),
                  full((CONV_WIDTH, c)), full((1, c)), full((1, c))],
        out_specs=[full((lp, 2 * c)), full((CONV_WIDTH, c)), full((1, c)), full((1, c)), full((1, c))],
        out_shape=[jax.ShapeDtypeStruct((lp, 2 * c), BF16), jax.ShapeDtypeStruct((CONV_WIDTH, c), F32), vec, vec, vec],
        scratch_shapes=[pltpu.VMEM((lp + HALO, c), F32), pltpu.VMEM((8 * CONV_WIDTH, c), F32),
                        pltpu.VMEM((24, c), F32)],
    )(dcat, u0, u1, agf, agf, conv_w, ln_g, ln_b)


FFN_TILE = 256
FFN_PAD = 8


def _ffn_conv(xs, w_ref, b_ref, half, r0):
    win = xs[half, pl.ds(r0, CHUNK + FFN_PAD), :]
    acc = jnp.broadcast_to(b_ref[half], (CHUNK, FFN_TILE))
    for k in range(FFN_CONV_WIDTH):
        s = FFN_CONV_WIDTH - 1 - k
        sh = win if s == 0 else pltpu.roll(win, s, 0)
        acc = acc + w_ref[half, k:k + 1, :] * sh[FFN_PAD:FFN_PAD + CHUNK, :]
    return acc


def _ffn_up_act(n, w_up_t, w3, b3, *, name):
    lp = n.shape[0]
    nch, nt = lp // CHUNK, D_FF // FFN_TILE
    nt_dims = (((1,), (1,)), ((), ()))

    parts = 4

    def project(n_ref, wg_ref, wv_ref, u, part=None):
        rows = slice(0, lp) if part is None else slice(part * lp // parts, (part + 1) * lp // parts)
        for half, w_ref in ((0, wg_ref), (1, wv_ref)):
            u[half, FFN_PAD + rows.start:FFN_PAD + rows.stop, :] = lax.dot_general(
                n_ref[rows, :], w_ref[...], nt_dims, preferred_element_type=F32)

    def activate(u, w_ref, b_ref, act_ref, act_t_ref, gv_ref, up_ref, part):
        for ci in range(part * nch // parts, (part + 1) * nch // parts):
            r0 = ci * CHUNK
            rows = slice(r0, r0 + CHUNK)
            gate = _ffn_conv(u, w_ref, b_ref, 0, r0)
            val = _ffn_conv(u, w_ref, b_ref, 1, r0)
            gv_ref[0, rows, :] = gate.astype(BF16)
            gv_ref[1, rows, :] = val.astype(BF16)
            for half in range(2):
                up_ref[half, rows, :] = u[half, FFN_PAD + r0:FFN_PAD + r0 + CHUNK, :].astype(BF16)
            act = (gate * _sigmoid(gate) * val).astype(BF16)
            act_ref[rows, :] = act
            act_t_ref[:, rows] = act.T

    def body(n_ref, wg_ref, wv_ref, wg_next, wv_next, w_ref, b_ref, act_ref, act_t_ref, gv_ref, up_ref, u0, u1):
        j = pl.program_id(0)

        @pl.when(j == 0)
        def _():
            for u in (u0, u1):
                u[:, 0:FFN_PAD, :] = jnp.zeros((2, FFN_PAD, FFN_TILE), F32)
            project(n_ref, wg_ref, wv_ref, u0)

        for parity, (mine, other) in enumerate(((u0, u1), (u1, u0))):
            @pl.when(j % 2 == parity)
            def _(mine=mine, other=other):
                for part in range(parts):
                    project(n_ref, wg_next, wv_next, other, part)
                    activate(mine, w_ref, b_ref, act_ref, act_t_ref, gv_ref, up_ref, part)

    halves = pl.BlockSpec((2, lp, FFN_TILE), lambda j: (0, 0, j))
    rows_of = lambda half, ahead: pl.BlockSpec(
        (FFN_TILE, D_MODEL), lambda j: (half * nt + jnp.minimum(j + ahead, nt - 1), 0))
    return pl.pallas_call(
        body, name=name, grid=(nt,),
        in_specs=[pl.BlockSpec((lp, D_MODEL), lambda j: (0, 0)), rows_of(0, 0), rows_of(1, 0), rows_of(0, 1),
                  rows_of(1, 1), pl.BlockSpec((2, FFN_CONV_WIDTH, FFN_TILE), lambda j: (0, 0, j)),
                  pl.BlockSpec((2, 1, FFN_TILE), lambda j: (0, 0, j))],
        out_specs=[pl.BlockSpec((lp, FFN_TILE), lambda j: (0, j)), pl.BlockSpec((FFN_TILE, lp), lambda j: (j, 0)),
                   halves, halves],
        out_shape=[jax.ShapeDtypeStruct((lp, D_FF), BF16), jax.ShapeDtypeStruct((D_FF, lp), BF16),
                   jax.ShapeDtypeStruct((2, lp, D_FF), BF16), jax.ShapeDtypeStruct((2, lp, D_FF), BF16)],
        scratch_shapes=[pltpu.VMEM((2, lp + FFN_PAD, FFN_TILE), F32), pltpu.VMEM((2, lp + FFN_PAD, FFN_TILE), F32)],
    )(n, w_up_t, w_up_t, w_up_t, w_up_t, w3, b3)


def _ffn_act_bwd(up3, gv3, w3, dact, *, name):
    _, lp, f = up3.shape
    nch = lp // CHUNK
    wlen = CHUNK + FFN_PAD

    def body(up_ref, gv_ref, w_ref, dact_ref, dup_ref, dw_ref, db_ref, ds, wacc):
        for half in range(2):
            ds[half, lp:lp + FFN_PAD, :] = jnp.zeros((FFN_PAD, FFN_TILE), F32)
        wacc[...] = jnp.zeros_like(wacc)

        def act_bwd(ci, _):
            rows = pl.ds(pl.multiple_of(ci * CHUNK, CHUNK), CHUNK)
            gate, val = gv_ref[0, rows, :].astype(F32), gv_ref[1, rows, :].astype(F32)
            sg = _sigmoid(gate)
            da = dact_ref[rows, :].astype(F32)
            ds[0, rows, :] = da * val * (sg * (1.0 + gate * (1.0 - sg)))
            ds[1, rows, :] = da * (gate * sg)
            return 0

        lax.fori_loop(0, nch, act_bwd, 0, unroll=True)

        def conv_bwd(ci, _):
            r0 = pl.multiple_of(ci * CHUNK, CHUNK)
            rows = pl.ds(r0, CHUNK)
            for half in range(2):
                dwin = ds[half, pl.ds(r0, wlen), :]
                x = up_ref[half, rows, :].astype(F32)
                acc = jnp.zeros((CHUNK, FFN_TILE), F32)
                for k in range(FFN_CONV_WIDTH):
                    s = FFN_CONV_WIDTH - 1 - k
                    d_s = (dwin if s == 0 else pltpu.roll(dwin, wlen - s, 0))[0:CHUNK, :]
                    acc = acc + w_ref[half, k:k + 1, :] * d_s
                    wacc[half, 8 * k:8 * k + 8, :] += _fold8(d_s * x)
                wacc[half, 24:32, :] += _fold8(dwin[0:CHUNK, :])
                dup_ref[half, rows, :] = acc.astype(BF16)
            return 0

        lax.fori_loop(0, nch, conv_bwd, 0)
        for half in range(2):
            for k in range(FFN_CONV_WIDTH):
                dw_ref[half, k:k + 1, :] = jnp.sum(wacc[half, 8 * k:8 * k + 8, :], axis=0, keepdims=True)
            db_ref[half] = jnp.sum(wacc[half, 24:32, :], axis=0, keepdims=True)

    halves = pl.BlockSpec((2, lp, FFN_TILE), lambda j: (0, 0, j))
    taps = pl.BlockSpec((2, FFN_CONV_WIDTH, FFN_TILE), lambda j: (0, 0, j))
    bias = pl.BlockSpec((2, 1, FFN_TILE), lambda j: (0, 0, j))
    return pl.pallas_call(
        body, name=name, grid=(f // FFN_TILE,),
        in_specs=[halves, halves, taps, pl.BlockSpec((lp, FFN_TILE), lambda j: (0, j))],
        out_specs=[halves, taps, bias],
        out_shape=[jax.ShapeDtypeStruct((2, lp, f), BF16), jax.ShapeDtypeStruct((2, FFN_CONV_WIDTH, f), F32),
                   jax.ShapeDtypeStruct((2, 1, f), F32)],
        scratch_shapes=[pltpu.VMEM((2, lp + FFN_PAD, FFN_TILE), F32), pltpu.VMEM((2, 32, FFN_TILE), F32)],
    )(up3, gv3, w3, dact)


POOL_PAD = 16


def _inv_count(r0, w):
    t = r0 + lax.broadcasted_iota(jnp.int32, (CHUNK, 1), 0)
    return 1.0 / jnp.minimum(t + 1, w).astype(F32)


def _pool_fwd(n, pool_w, pool_b, pool_scale, h, *, name):
    lp, dm = n.shape
    nch = lp // CHUNK
    g = POOL_GROUP

    def body(n_ref, w_ref, b_ref, s_ref, h_ref, ho_ref, dt_ref, z_ref, xs, d_ref):
        gi = pl.program_id(0)
        xs[0:POOL_PAD, :] = jnp.zeros((POOL_PAD, g), F32)
        xs[POOL_PAD:POOL_PAD + lp, :] = n_ref[...]
        for idx, w in enumerate(POOL_WINDOWS):
            @pl.when(gi == idx)
            def _(w=w):
                def chunk(ci, _):
                    r0 = pl.multiple_of(ci * CHUNK, CHUNK)
                    win = xs[pl.ds(r0, CHUNK + POOL_PAD), :]
                    acc = win
                    for j in range(1, w):
                        acc = acc + pltpu.roll(win, j, 0)
                    x = win[POOL_PAD:POOL_PAD + CHUNK, :]
                    d = acc[POOL_PAD:POOL_PAD + CHUNK, :] * _inv_count(r0, w) - x
                    d_ref[pl.ds(r0, CHUNK), :] = d.astype(BF16)
                    dt_ref[:, pl.ds(r0, CHUNK)] = d.astype(BF16).T
                    return 0

                lax.fori_loop(0, nch, chunk, 0, unroll=True)

        z = jnp.dot(d_ref[...], w_ref[...], preferred_element_type=F32) + b_ref[...]
        z_ref[...] = z
        ho_ref[...] = h_ref[...] + z * s_ref[...]

    col = pl.BlockSpec((lp, g), lambda i: (0, i))
    vec = pl.BlockSpec((1, g), lambda i: (0, i))
    return pl.pallas_call(
        body, name=name, grid=(len(POOL_WINDOWS),),
        in_specs=[col, pl.BlockSpec((None, g, g), lambda i: (i, 0, 0)), vec, vec, col],
        out_specs=[col, pl.BlockSpec((g, lp), lambda i: (i, 0)), col],
        out_shape=[jax.ShapeDtypeStruct((lp, dm), F32), jax.ShapeDtypeStruct((dm, lp), BF16),
                   jax.ShapeDtypeStruct((lp, dm), F32)],
        scratch_shapes=[pltpu.VMEM((lp + POOL_PAD, g), F32), pltpu.VMEM((lp, g), BF16)],
    )(n, pool_w, pool_b, pool_scale, h)


def _pool_bwd(dy, z, pool_w, pool_scale, *, name):
    lp, dm = dy.shape
    nch = lp // CHUNK
    g = POOL_GROUP
    wlen = CHUNK + POOL_PAD

    def body(dy_ref, z_ref, w_ref, s_ref, dn_ref, dz_ref, dsc_ref, db_ref, ys, dd):
        gi = pl.program_id(0)
        dyv = dy_ref[...]
        dsc_ref[...] = jnp.sum(dyv * z_ref[...], axis=0, keepdims=True)
        dz = dyv * s_ref[...]
        db_ref[...] = jnp.sum(dz, axis=0, keepdims=True)
        dzb = dz.astype(BF16)
        dz_ref[...] = dzb
        dd[...] = lax.dot_general(dzb, w_ref[...], (((1,), (1,)), ((), ())), preferred_element_type=F32)
        ys[lp:lp + POOL_PAD, :] = jnp.zeros((POOL_PAD, g), F32)
        for idx, w in enumerate(POOL_WINDOWS):
            @pl.when(gi == idx)
            def _(w=w):
                def scale(ci, _):
                    r0 = pl.multiple_of(ci * CHUNK, CHUNK)
                    ys[pl.ds(r0, CHUNK), :] = dd[pl.ds(r0, CHUNK), :] * _inv_count(r0, w)
                    return 0

                lax.fori_loop(0, nch, scale, 0, unroll=True)

                def chunk(ci, _):
                    r0 = pl.multiple_of(ci * CHUNK, CHUNK)
                    win = ys[pl.ds(r0, wlen), :]
                    acc = win
                    for j in range(1, w):
                        acc = acc + pltpu.roll(win, wlen - j, 0)
                    dn_ref[pl.ds(r0, CHUNK), :] = acc[0:CHUNK, :] - dd[pl.ds(r0, CHUNK), :]
                    return 0

                lax.fori_loop(0, nch, chunk, 0, unroll=True)

    col = pl.BlockSpec((lp, g), lambda i: (0, i))
    vec = pl.BlockSpec((1, g), lambda i: (0, i))
    return pl.pallas_call(
        body, name=name, grid=(len(POOL_WINDOWS),),
        in_specs=[col, col, pl.BlockSpec((None, g, g), lambda i: (i, 0, 0)), vec],
        out_specs=[col, col, vec, vec],
        out_shape=[jax.ShapeDtypeStruct((lp, dm), F32), jax.ShapeDtypeStruct((lp, dm), BF16),
                   jax.ShapeDtypeStruct((1, dm), F32), jax.ShapeDtypeStruct((1, dm), F32)],
        scratch_shapes=[pltpu.VMEM((lp + POOL_PAD, g), F32), pltpu.VMEM((lp, g), F32)],
    )(dy, z, pool_w, pool_scale)


def _ffn_fwd(h, g, weight, w3, b3, tag):
    n, n_t = _rms_fwd(h, g, name=f"rms_ffn{tag}", out_dtype=BF16, transposed=True)
    w_up_t = weight("up", n)
    act, act_t, gv3, up3 = _ffn_up_act(n, w_up_t, w3, b3, name=f"ffn_up_act{tag}")
    w_down = weight("down", act)
    h_out = _mm(act, w_down, name=f"mm_down{tag}", tn=256, res=h)
    return h_out, (n_t, up3, gv3, act_t), w_up_t, w_down


def _ffn_bwd(dh, dhb, h, g, saved, w_up_t, w3, w_down, tag, after=None):
    lp = h.shape[0]
    n_t, up3, gv3, act_t = saved
    dw_down = _mm(act_t, dhb, name=f"mm_dwdown{tag}", tm=704, out_dtype=BF16)
    dact = _mm(dhb, w_down, name=f"mm_dact{tag}", tb=True, tn=D_FF // 2, out_dtype=BF16, after=after)
    dup3, dcw, dcb = _ffn_act_bwd(up3, gv3, w3, dact, name=f"ffn_act_bwd{tag}")
    dup2 = dup3.reshape(2 * lp, D_FF)
    dw_up = _mm_dw_up(n_t, dup2, name=f"mm_dwup{tag}")
    dh_in, dh_in_b, dg = _mm_rms_bwd(dup2, w_up_t, h, g, dh, name=f"mm_dnffn{tag}", tk=D_FF // 2,
                                     a_map=lambda i, kk: (4 * (kk // 2) + i, kk % 2))
    return dh_in, dh_in_b, (dg, dw_up, dcw, dcb, dw_down)


def _local_step(x, tgt, wt):
    seq = x.shape[0]
    n_real = N_META + seq
    lp = _round_up(n_real, CHUNK)
    pad = jnp.zeros((lp - n_real, D_MODEL), F32)
    h0 = jnp.concatenate([wt["meta"], x, pad], axis=0)
    tgt_p = jnp.concatenate([jnp.zeros((N_META, D_MODEL), F32), tgt, pad], axis=0)
    w_in_p = wt["w_in_p"]

    n0, n0_t = _rms_fwd(h0, wt["g_even"], name="rms_even", out_dtype=BF16, after=wt["ffn_started"], transposed=True)
    qkv, q_t = _mm(n0, w_in_p, name="mm_qkv", tn=FOX_WIDTH, dims=(lp, 3 * FOX_WIDTH, D_MODEL), out_dtype=BF16,
                   first_tile_t=True)
    ag = _mm(n0, w_in_p, name="mm_ag", tn=512, dims=(lp, 2 * CONV_CH, D_MODEL),
             b_map=lambda i, j, k: (0, 3 + j))
    f_t = _mm(wt["wf_t"], n0, name="mm_ft", tb=True)
    c_row = _fox_prep(f_t, wt["b_f"], name="fox_prep")
    c3 = c_row.reshape(4, 2, lp)
    o = _attn_fwd(qkv, c3, name="attn_fwd")
    u0, u1, u3 = _conv_fwd(ag, wt["conv_w"], wt["conv_b"], wt["ln_g"], wt["ln_b"], name="conv_fwd")
    cat = jnp.concatenate([o, u3], axis=1)
    h1 = _mm(cat, wt["w_out"], name="mm_out", tn=512, res=h0)
    h2, saved0, w_up0, w_down0 = _ffn_fwd(h1, wt["ffn_norm"][0:1], functools.partial(wt["ffn_weight"], 0),
                                          wt["fcw3"][0], wt["fcb3"][0], 0)

    n2 = _rms_fwd(h2, wt["g_odd"], name="rms_odd", out_dtype=F32)
    h3, dpool_t, z = _pool_fwd(n2, wt["pool_w"], wt["pool_b"], wt["pool_scale"], h2, name="pool_fwd")
    h4, saved1, w_up1, w_down1 = _ffn_fwd(h3, wt["ffn_norm"][1:2], functools.partial(wt["ffn_weight"], 1),
                                          wt["fcw3"][1], wt["fcb3"][1], 1)

    loss, dh4, dh4b, d_gfinal = _loss_head(h4, wt["g_final"], tgt_p, n_real, name="loss_head")

    dh3, dh3b, gf1 = _ffn_bwd(dh4, dh4b, h3, wt["ffn_norm"][1:2], saved1, w_up1, wt["fcw3"][1], w_down1, 1)
    send1, token1 = _send_ffn_grads(gf1[1], gf1[4], 1)
    dn2, dzb, d_pscale, d_pb = _pool_bwd(dh3, z, wt["pool_w"], wt["pool_scale"], name="pool_bwd")
    d_pw = _mm(dpool_t, dzb, name="mm_dpoolw", tm=POOL_GROUP, tn=POOL_GROUP, dims=(D_MODEL, POOL_GROUP, lp),
               b_map=lambda i, j, k: (0, i), o_map=lambda i, j, k: (i, 0), out_shape=(D_MODEL, POOL_GROUP),
               out_dtype=BF16)
    dh2, dh2b, d_godd = _rms_bwd(h2, wt["g_odd"], dn2, dh3, name="rms_bwd_odd")
    dh1, dh1b, gf0 = _ffn_bwd(dh2, dh2b, h1, wt["ffn_norm"][0:1], saved0, w_up0, wt["fcw3"][0], w_down0, 0,
                              after=token1)

    send0, token0 = _send_ffn_grads(gf0[1], gf0[4], 0)
    cat_t = _transpose(cat, name="t_cat", out_dtype=BF16)
    d_wout = _mm(cat_t, dh1b, name="mm_dwout", tm=512, out_dtype=BF16)
    dcat, do_t = _mm(dh1b, wt["w_out"], name="mm_dcat", tb=True, tn=FOX_WIDTH, after=token0, first_tile_t=True)
    dag, d_convw, d_convb, d_lng, d_lnb = _conv_bwd(dcat, u0, u1, ag, wt["conv_w"], wt["ln_g"], wt["ln_b"],
                                                    name="conv_bwd")
    layers = lambda i: jnp.stack([gf0[i], gf1[i]])
    grads = dict(
        conv_w=d_convw[None], w_out=d_wout, mix_norm_odd=d_godd,
        pool_w=d_pw.reshape(len(POOL_WINDOWS), POOL_GROUP, POOL_GROUP),
        pool_b=d_pb.reshape(1, len(POOL_WINDOWS), POOL_GROUP), pool_scale=d_pscale, w_up=(gf0[1], gf1[1]),
        ffn_conv_w=layers(2).transpose(0, 2, 1, 3).reshape(DEPTH, FFN_CONV_WIDTH, 2 * D_FF), w_down=(gf0[4], gf1[4]))
    send_rest, token_rest = _send_rest_grads(grads)
    dq, dk, dv, dc3 = _attn_bwd(qkv, q_t, dcat, do_t, c3, name="attn_bwd", after=token_rest)
    df_t, d_bf = _fox_bwd(dc3.reshape(FOX_HEADS, lp), f_t, wt["b_f"], name="fox_bwd")
    df = _transpose(df_t, name="t_df", out_dtype=BF16)
    dproj = jnp.concatenate([dq, dk, dv, dag, df], axis=1)
    grads["w_in"] = _mm_dw_in(n0_t, dproj, name="mm_dwin")
    send_in, token_in = _send_start(
        [grads["w_in"]], [jax.ShapeDtypeStruct((N_DEV - 1, D_MODEL, _IN_SHARD), BF16)], [(0, _by_owner, 0, None)],
        name="send_w_in")
    dh0, _, d_geven = _mm_rms_bwd(dproj, w_in_p, h0, wt["g_even"], dh1, name="mm_dn0", tb=True, tk=896,
                                  after=token_in)
    grads.update(
        meta_tokens=dh0[0:N_META], mix_norm_even=d_geven, b_f=d_bf.reshape(1, FOX_HEADS), conv_b=d_convb, ln_g=d_lng,
        ln_b=d_lnb, ffn_norm=jnp.concatenate([gf0[0], gf1[0]], axis=0),
        ffn_conv_b=layers(3).reshape(DEPTH, 2 * D_FF), final_norm=d_gfinal.reshape(D_MODEL),
        sends=(send0, send1, send_rest, send_in))
    return loss, dh0[N_META:n_real], grads


_LITTLE = (("conv_w", (1, 31, 512), 2), ("mix_norm_odd", (1, 1024), 1), ("pool_b", (1, 4, 256), 2),
           ("pool_scale", (1, 1024), 1), ("ffn_conv_w", (2, 3, 5632), 2))


def _send_rest_grads(g):
    out_rows, pool_rows, groups = D_MODEL // N_DEV, POOL_GROUP // N_DEV, len(POOL_WINDOWS)
    little_slabs = _pack([_full_to_slabs(g[n], s, a) for n, s, a in _LITTLE], F32, lead=(N_DEV,), align=8)
    land = lambda shape, dtype: jax.ShapeDtypeStruct((N_DEV - 1,) + shape, dtype)
    handle, token = _send_start(
        [g["w_out"], g["pool_w"], little_slabs],
        [land((out_rows, D_MODEL), BF16), land((groups, pool_rows, POOL_GROUP), BF16), land(little_slabs.shape[1:], F32)],
        [(0, _row_block(out_rows), 0, None), (1, _row_block(pool_rows, axis=1), 1, None), (2, _by_owner, 2, None)],
        name="send_rest")
    return handle, token


def _send_ffn_grads(dw_up, dw_down, tag):
    rows = D_FF // N_DEV
    lands = [jax.ShapeDtypeStruct((N_DEV - 1,) + dw_up.shape[1:], BF16),
             jax.ShapeDtypeStruct((N_DEV - 1, rows, D_MODEL), BF16)]
    return _send_start([dw_up, dw_down], lands, [(0, _by_owner, 0, None), (1, _row_block(rows), 1, None)],
                       name=f"send_ffn{tag}")


_QKV = 3 * FOX_WIDTH
_GLU0 = _QKV + FOX_HEADS
_IN_COLS = _GLU0 + 2 * CONV_CH
_F_PAD = 128


_IN_SHARD = _IN_COLS // N_DEV
_UP_SHARD = 2 * D_FF // N_DEV
_ROW_TILE = 256


def _assemble_w_in(st, *, name):
    tr = _ROW_TILE

    def body(s_ref, o_ref):
        full = jnp.concatenate([s_ref[i].astype(F32) for i in range(N_DEV)], axis=1)
        parts = [full[:, :_QKV], full[:, _GLU0:], full[:, _QKV:_GLU0], jnp.zeros((tr, _F_PAD - FOX_HEADS), F32)]
        o_ref[...] = jnp.concatenate(parts, axis=1).astype(BF16)

    return pl.pallas_call(
        body, name=name, grid=(D_MODEL // tr,),
        in_specs=[pl.BlockSpec((N_DEV, tr, _IN_SHARD), lambda i: (0, i, 0))],
        out_specs=pl.BlockSpec((tr, _QKV + 2 * CONV_CH + _F_PAD), lambda i: (i, 0)),
        out_shape=jax.ShapeDtypeStruct((D_MODEL, _QKV + 2 * CONV_CH + _F_PAD), BF16),
    )(st)


def _mm_dw_in(n_t, dproj, *, name):
    dm, lp = n_t.shape
    tr = _ROW_TILE
    ag0 = _QKV + 2 * CONV_CH

    def body(a_ref, b_ref, o_ref):
        r = jnp.dot(a_ref[...], b_ref[...], preferred_element_type=F32)
        full = jnp.concatenate([r[:, :_QKV], r[:, ag0:ag0 + FOX_HEADS], r[:, _QKV:ag0]], axis=1)
        for i in range(N_DEV):
            o_ref[i] = full[:, i * _IN_SHARD:(i + 1) * _IN_SHARD].astype(BF16)

    return pl.pallas_call(
        body, name=name, grid=(dm // tr,),
        in_specs=[pl.BlockSpec((tr, lp), lambda i: (i, 0)), pl.BlockSpec(dproj.shape, lambda i: (0, 0))],
        out_specs=pl.BlockSpec((N_DEV, tr, _IN_SHARD), lambda i: (0, i, 0)),
        out_shape=jax.ShapeDtypeStruct((N_DEV, dm, _IN_SHARD), BF16),
    )(n_t, dproj)


def _mm_dw_up(n_t, dup2, *, name):
    dm, lp = n_t.shape
    pairs_per_half = D_FF // (2 * _UP_SHARD)

    def body(a_ref, b_ref, o_ref):
        r_t = jnp.dot(a_ref[...], b_ref[...], preferred_element_type=F32).T
        o_ref[0] = r_t[:_UP_SHARD, :].astype(BF16)
        o_ref[1] = r_t[_UP_SHARD:, :].astype(BF16)

    return pl.pallas_call(
        body, name=name, grid=(N_DEV // 2,),
        in_specs=[pl.BlockSpec((dm, lp), lambda p: (0, 0)),
                  pl.BlockSpec((lp, 2 * _UP_SHARD), lambda p: (p // pairs_per_half, p % pairs_per_half))],
        out_specs=pl.BlockSpec((2, _UP_SHARD, dm), lambda p: (p, 0, 0)),
        out_shape=jax.ShapeDtypeStruct((N_DEV, _UP_SHARD, dm), BF16),
    )(n_t, dup2)


MESH = pl.DeviceIdType.MESH
ANY = pl.BlockSpec(memory_space=pl.ANY)


def _slot(px, py, pc):
    return 4 * px + 2 * py + pc


def _by_owner(ref, slot):
    return ref.at[slot]


def _row_block(rows, axis=0):
    def place(ref, slot):
        idx = (slice(None),) * axis + (pl.ds(slot * rows, rows),)
        return ref.at[idx]
    return place


def _all_gather(arrs, out_shapes, places, *, name):
    n = len(arrs)

    def body(*refs):
        ins, outs = refs[:n], refs[n:2 * n]
        send_sems, recv_sems, local_sems = refs[2 * n:]
        x, y, c = lax.axis_index("x"), lax.axis_index("y"), lax.axis_index("c")
        me, sibling = (x, y, c), (x, y, 1 - c)
        chips = [(1 - x, y), (x, 1 - y), (1 - x, 1 - y)]

        def copy(a, k, block, to, from_input=False):
            dst = places[a](outs[a], _slot(*block))
            return pltpu.make_async_remote_copy(
                src_ref=ins[a] if from_input else dst, dst_ref=dst,
                send_sem=send_sems.at[7 * a + k], recv_sem=recv_sems.at[7 * a + k],
                device_id=to, device_id_type=MESH)

        own, sent = [], []
        for a in range(n):
            mine = pltpu.make_async_copy(ins[a], places[a](outs[a], _slot(*me)), local_sems.at[a])
            mine.start()
            own.append(mine)
            first = [copy(a, 0, me, sibling, True)]
            first += [copy(a, 1 + j, me, (*chip, c), True) for j, chip in enumerate(chips)]
            for cp in first:
                cp.start()
            sent += first
        for a in range(n):
            for j, chip in enumerate(chips):
                copy(a, 1 + j, (*chip, c), me).wait_recv()
                passed = copy(a, 4 + j, (*chip, c), sibling)
                passed.start()
                sent.append(passed)
        for a in range(n):
            copy(a, 0, sibling, me).wait_recv()
            for j, chip in enumerate(chips):
                copy(a, 4 + j, (*chip, 1 - c), me).wait_recv()
        for cp in sent:
            cp.wait_send()
        for cp in own:
            cp.wait()

    return pl.pallas_call(
        body, name=name,
        in_specs=[ANY] * n, out_specs=[ANY] * n,
        out_shape=[jax.ShapeDtypeStruct(s, a.dtype) for s, a in zip(out_shapes, arrs)],
        scratch_shapes=[pltpu.SemaphoreType.DMA((7 * n,)), pltpu.SemaphoreType.DMA((7 * n,)),
                        pltpu.SemaphoreType.DMA((n,))],
    )(*arrs)


HBM = pl.BlockSpec(memory_space=pltpu.HBM)
SEM = pl.BlockSpec(memory_space=pltpu.SEMAPHORE)
EFFECT = pltpu.SideEffectType.DATAFLOW_SIDE_EFFECTING


def _relation_copies(src_refs, land_refs, copies, send_sems, recv_sems):
    x, y, c = lax.axis_index("x"), lax.axis_index("y"), lax.axis_index("c")
    flip = lambda v, bit: 1 - v if bit else v
    out = []
    for k in range(1, N_DEV):
        p = (flip(x, k & 4), flip(y, k & 2), flip(c, k & 1))
        for j, (si, take, li, put) in enumerate(copies):
            sem = (k - 1) * len(copies) + j
            dst = land_refs[li].at[k - 1] if put is None else put(land_refs[li], _slot(x, y, c))
            out.append(pltpu.make_async_remote_copy(
                src_ref=take(src_refs[si], _slot(*p)), dst_ref=dst,
                send_sem=send_sems.at[sem], recv_sem=recv_sems.at[sem], device_id=p, device_id_type=MESH))
    return out


def _own_copies(src_refs, land_refs, copies, sems):
    me = _slot(lax.axis_index("x"), lax.axis_index("y"), lax.axis_index("c"))
    placed = [(si, take, li, put) for si, take, li, put in copies if put is not None]
    return [pltpu.make_async_copy(take(src_refs[si], me), put(land_refs[li], me),
                                  sems.at[(N_DEV - 1) * len(copies) + j])
            for j, (si, take, li, put) in enumerate(placed)]


def _send_start(srcs, land_structs, copies, *, name, after=None):
    ns, nl = len(srcs), len(land_structs)
    n_sem = (N_DEV - 1) * len(copies) + sum(put is not None for _, _, _, put in copies)
    behind = [] if after is None else [after]

    def body(*refs):
        first_out = ns + nl + len(behind)
        send_sems, recv_sems, token = refs[first_out], refs[first_out + 1], refs[-1]
        for cp in _relation_copies(refs[:ns], refs[ns:ns + nl], copies, send_sems, recv_sems):
            cp.start()
        for cp in _own_copies(refs[:ns], refs[ns:ns + nl], copies, send_sems):
            cp.start()
        token[...] = jnp.zeros_like(token)

    in_hbm = lambda a: pltpu.with_memory_space_constraint(a, pltpu.HBM)
    outs = pl.pallas_call(
        body, name=name,
        out_shape=(pltpu.SemaphoreType.DMA((n_sem,)), pltpu.SemaphoreType.DMA((n_sem,)),
                   *[pltpu.HBM(s.shape, s.dtype) for s in srcs],
                   *[pltpu.HBM(s.shape, s.dtype) for s in land_structs],
                   jax.ShapeDtypeStruct((8, 128), F32)),
        in_specs=(HBM,) * (ns + nl) + (ANY,) * len(behind),
        out_specs=(SEM, SEM) + (HBM,) * (ns + nl) + (pl.BlockSpec(memory_space=pltpu.VMEM),),
        input_output_aliases={i: 2 + i for i in range(ns + nl)},
        compiler_params=pltpu.CompilerParams(has_side_effects=EFFECT),
    )(*[in_hbm(s) for s in srcs], *[in_hbm(lax.empty(s.shape, s.dtype)) for s in land_structs], *behind)
    return (outs[0], outs[1], outs[2:2 + ns], outs[2 + ns:2 + ns + nl], copies), outs[-1]


def _send_wait(handle, after, *, name):
    send_sems, recv_sems, srcs, lands, copies = handle
    ns, nl = len(srcs), len(lands)

    def body(*refs):
        for cp in _relation_copies(refs[:ns], refs[ns:ns + nl], copies, refs[ns + nl], refs[ns + nl + 1]):
            cp.wait_send()
            cp.wait_recv()
        for cp in _own_copies(refs[:ns], refs[ns:ns + nl], copies, refs[ns + nl]):
            cp.wait()

    outs = pl.pallas_call(
        body, name=name,
        out_shape=tuple(pltpu.HBM(a.shape, a.dtype) for a in (*srcs, *lands)),
        in_specs=(HBM,) * (ns + nl) + (SEM, SEM, ANY), out_specs=(HBM,) * (ns + nl),
        input_output_aliases={i: i for i in range(ns + nl)},
        compiler_params=pltpu.CompilerParams(has_side_effects=EFFECT),
    )(*srcs, *lands, send_sems, recv_sems, after)
    return outs[:ns], outs[ns:]


def _sum_slabs(stack, *, name, own=None):
    n, rows, w = stack.shape

    def body(*refs):
        s_ref, o_ref = refs[-2], refs[-1]
        acc = s_ref[0] if own is None else refs[0][...] + s_ref[0]
        for i in range(1, n):
            acc = acc + s_ref[i]
        o_ref[...] = acc

    return pl.pallas_call(body, name=name, out_shape=jax.ShapeDtypeStruct((rows, w), F32))(
        *([] if own is None else [own]), stack)


def _adam_math(w, g, m, v):
    mn = ADAM_B1 * m + (1.0 - ADAM_B1) * g
    vn = ADAM_B2 * v + (1.0 - ADAM_B2) * (g * g)
    m_hat = mn / (1.0 - ADAM_B1 ** ADAM_STEP)
    v_hat = vn / (1.0 - ADAM_B2 ** ADAM_STEP)
    return -ADAM_LR * (m_hat / (jnp.sqrt(v_hat) + ADAM_EPS) + ADAM_WD * w), mn, vn


def _adamw_many(ws, gs, ms, vs, *, name):
    n = len(ws)

    def body(*refs):
        for i in range(n):
            w_ref, g_ref, m_ref, v_ref = (refs[j * n + i] for j in range(4))
            d_ref, mo_ref, vo_ref = refs[4 * n + 3 * i:4 * n + 3 * i + 3]
            d_ref[...], mo_ref[...], vo_ref[...] = _adam_math(w_ref[...], g_ref[...], m_ref[...], v_ref[...])

    return pl.pallas_call(
        body, name=name, out_shape=[jax.ShapeDtypeStruct(w.shape, F32) for w in ws for _ in range(3)],
    )(*ws, *gs, *ms, *vs)


def _adamw_layers(w, owns, lands, m, v, tr, *, name):
    nl, rows, cols = w.shape
    steps = rows // tr
    assert rows % tr == 0

    def body(*refs):
        w_ref, m_ref, v_ref = refs[:3]
        own_refs, land_refs = refs[3:3 + nl], refs[3 + nl:3 + 2 * nl]
        g_ref, d_ref, mo_ref, vo_ref = refs[3 + 2 * nl:]
        for li in range(nl):
            @pl.when(pl.program_id(0) == li)
            def _(li=li):
                g = own_refs[li][...].astype(F32)
                for k in range(N_DEV - 1):
                    g = g + land_refs[li][k].astype(F32)
                g_ref[...] = g
                d_ref[...], mo_ref[...], vo_ref[...] = _adam_math(w_ref[...], g, m_ref[...], v_ref[...])

    def held(li):
        return lambda l, i: jnp.where(l == li, i, jnp.where(l < li, 0, steps - 1))

    blk = pl.BlockSpec((None, tr, cols), lambda l, i: (l, i, 0))
    own_specs = [pl.BlockSpec((tr, cols), lambda l, i, f=held(li): (f(l, i), 0)) for li in range(nl)]
    land_specs = [pl.BlockSpec((N_DEV - 1, tr, cols), lambda l, i, f=held(li): (0, f(l, i), 0)) for li in range(nl)]
    return pl.pallas_call(
        body, name=name, grid=(nl, steps),
        in_specs=[blk, blk, blk] + own_specs + land_specs, out_specs=[blk] * 4,
        out_shape=[jax.ShapeDtypeStruct(w.shape, F32)] * 4,
    )(w, m, v, *owns, *lands)


_WEIGHTS = (
    ("meta_tokens", (16, 1024), 1), ("mix_norm_even", (1, 1024), None), ("w_in", (1, 1024, 2568), 2),
    ("b_f", (1, 8), None), ("conv_w", (1, 31, 512), 2), ("conv_b", (1, 512), None), ("ln_g", (1, 512), None),
    ("ln_b", (1, 512), None), ("w_out", (1, 1024, 1024), 1), ("mix_norm_odd", (1, 1024), 1),
    ("pool_w", (1, 4, 256, 256), 2), ("pool_b", (1, 4, 256), 2), ("pool_scale", (1, 1024), 1),
    ("ffn_norm", (2, 1024), None), ("w_up", (2, 1024, 5632), 2), ("ffn_conv_w", (2, 3, 5632), 2),
    ("ffn_conv_b", (2, 5632), None), ("w_down", (2, 2816, 1024), 1), ("final_norm", (1024,), None),
)
_MATMUL_WEIGHTS = ("w_in", "w_out", "pool_w", "w_up", "w_down")
_ADAM_ROWS = dict(w_in=256, w_out=128, pool_w=128, w_up=352, w_down=352)


def _shard_shape(shape, axis):
    return shape[:axis] + (shape[axis] // N_DEV,) + shape[axis + 1:]


def _size(shape):
    n = 1
    for s in shape:
        n *= s
    return n


def _pack(parts, dtype, lead=(), align=16):
    flat = jnp.concatenate([p.reshape(lead + (-1,)).astype(dtype) for p in parts], axis=-1)
    n = flat.shape[-1]
    rows = _round_up(-(-n // FLAT_W), align)
    flat = jnp.pad(flat, [(0, 0)] * len(lead) + [(0, rows * FLAT_W - n)])
    return flat.reshape(lead + (rows, FLAT_W))


def _unpack(buf, shapes, lead=()):
    flat = buf.reshape(lead + (-1,))
    out, off = [], 0
    for shp in shapes:
        n = _size(shp)
        out.append(flat[..., off:off + n].reshape(lead + shp))
        off += n
    return out


def _gathered_to_full(stack, shape, axis):
    return jnp.moveaxis(stack, 0, axis).reshape(shape)


def _full_to_slabs(full, shape, axis):
    split = shape[:axis] + (N_DEV, shape[axis] // N_DEV) + shape[axis + 1:]
    return jnp.moveaxis(full.reshape(split), axis, 0)


def kernel(x, meta_tokens, mix_norm_even, w_in, b_f, conv_w, conv_b, ln_g, ln_b, w_out, mix_norm_odd, pool_w, pool_b, pool_scale, ffn_norm, w_up, ffn_conv_w, ffn_conv_b, w_down, final_norm, loss_target, m_meta_tokens, m_mix_norm_even, m_w_in, m_b_f, m_conv_w, m_conv_b, m_ln_g, m_ln_b, m_w_out, m_mix_norm_odd, m_pool_w, m_pool_b, m_pool_scale, m_ffn_norm, m_w_up, m_ffn_conv_w, m_ffn_conv_b, m_w_down, m_final_norm, v_meta_tokens, v_mix_norm_even, v_w_in, v_b_f, v_conv_w, v_conv_b, v_ln_g, v_ln_b, v_w_out, v_mix_norm_odd, v_pool_w, v_pool_b, v_pool_scale, v_ffn_norm, v_w_up, v_ffn_conv_w, v_ffn_conv_b, v_w_down, v_final_norm):
    names = [n for n, _, _ in _WEIGHTS]
    w_loc = dict(zip(names, (meta_tokens, mix_norm_even, w_in, b_f, conv_w, conv_b, ln_g, ln_b, w_out, mix_norm_odd,
                             pool_w, pool_b, pool_scale, ffn_norm, w_up, ffn_conv_w, ffn_conv_b, w_down, final_norm)))
    m_loc = dict(zip(names, (m_meta_tokens, m_mix_norm_even, m_w_in, m_b_f, m_conv_w, m_conv_b, m_ln_g, m_ln_b,
                             m_w_out, m_mix_norm_odd, m_pool_w, m_pool_b, m_pool_scale, m_ffn_norm, m_w_up,
                             m_ffn_conv_w, m_ffn_conv_b, m_w_down, m_final_norm)))
    v_loc = dict(zip(names, (v_meta_tokens, v_mix_norm_even, v_w_in, v_b_f, v_conv_w, v_conv_b, v_ln_g, v_ln_b,
                             v_w_out, v_mix_norm_odd, v_pool_w, v_pool_b, v_pool_scale, v_ffn_norm, v_w_up,
                             v_ffn_conv_w, v_ffn_conv_b, v_w_down, v_final_norm)))
    replicated = [(n, s) for n, s, a in _WEIGHTS if a is None]
    little = [(n, s, a) for n, s, a in _WEIGHTS if a is not None and n not in _MATMUL_WEIGHTS]
    little_shards = [_shard_shape(s, a) for _, s, a in little]
    out_rows, down_rows, pool_rows = D_MODEL // N_DEV, D_FF // N_DEV, POOL_GROUP // N_DEV
    n_groups = len(POOL_WINDOWS)

    little_pack = _pack([w_loc[n] for n, _, _ in little], F32)
    g_win, g_wout, g_poolw, g_little = _all_gather(
        [w_in[0].astype(BF16), w_out[0].astype(BF16), pool_w[0].astype(BF16), little_pack],
        [(N_DEV, D_MODEL, _IN_SHARD), (D_MODEL, D_MODEL), (n_groups, POOL_GROUP, POOL_GROUP),
         (N_DEV,) + little_pack.shape],
        [_by_owner, _row_block(out_rows), _row_block(pool_rows, axis=1), _by_owner],
        name="gather_weights")
    me = _slot(lax.axis_index("x"), lax.axis_index("y"), lax.axis_index("c"))
    up_t = lambda a: jnp.transpose(a, (0, 2, 1))
    w_loc["w_up"], m_loc["w_up"], v_loc["w_up"] = up_t(w_up), up_t(m_w_up), up_t(v_w_up)
    w_up_b, w_down_b = w_loc["w_up"].astype(BF16), w_down.astype(BF16)
    whole = lambda ref, slot: ref
    ffn_gathers, behind = {}, g_little
    for l in range(DEPTH):
        for part, shard, rows in (("up", w_up_b[l], _UP_SHARD), ("down", w_down_b[l], down_rows)):
            ffn_gathers[l, part], behind = _send_start(
                [shard], [jax.ShapeDtypeStruct((N_DEV * rows, D_MODEL), BF16)], [(0, whole, 0, _row_block(rows))],
                name=f"gather_{part}{l}_start", after=behind)

    def ffn_weight(l, part, after):
        return _send_wait(ffn_gathers[l, part], after, name=f"gather_{part}{l}_wait")[1][0]

    w_in_p = _assemble_w_in(g_win, name="assemble_w_in")
    full = {n: _gathered_to_full(st, s, a)
            for (n, s, a), st in zip(little, _unpack(g_little, little_shards, lead=(N_DEV,)))}
    f0 = _QKV + 2 * CONV_CH
    wt = dict(
        meta=full["meta_tokens"], g_even=mix_norm_even, w_in_p=w_in_p, wf_t=w_in_p[:, f0:f0 + FOX_HEADS].T,
        b_f=b_f.reshape(FOX_HEADS, 1), conv_w=full["conv_w"][0], conv_b=conv_b, ln_g=ln_g, ln_b=ln_b, w_out=g_wout,
        g_odd=full["mix_norm_odd"], pool_w=g_poolw, pool_b=full["pool_b"].reshape(1, D_MODEL),
        pool_scale=full["pool_scale"], ffn_norm=ffn_norm, ffn_weight=ffn_weight, ffn_started=behind,
        fcw3=full["ffn_conv_w"].reshape(DEPTH, FFN_CONV_WIDTH, 2, D_FF).transpose(0, 2, 1, 3),
        fcb3=ffn_conv_b.reshape(DEPTH, 2, 1, D_FF), g_final=final_norm.reshape(1, D_MODEL))

    loss_part, grad_x, g = _local_step(x[0], loss_target[0], wt)

    small = _pack([loss_part[:, 0:1]] + [g[n] for n, _ in replicated] + [g["meta_tokens"]], F32, align=8)
    send_small, token_small = _send_start([small], [jax.ShapeDtypeStruct((N_DEV,) + small.shape, F32)],
                                          [(0, whole, 0, _by_owner)], name="send_small")

    grads, delta, new_m, new_v = {}, {}, {}, {}
    send0, send1, send_rest, send_in = g["sends"]
    ffn_sent = [_send_wait(send, token_small, name=f"wait_ffn{l}") for l, send in enumerate((send0, send1))]
    own_up = [lax.dynamic_index_in_dim(srcs[0], me, 0, keepdims=False) for srcs, _ in ffn_sent]
    own_down = [lax.dynamic_slice_in_dim(srcs[1], me * down_rows, down_rows, 0) for srcs, _ in ffn_sent]
    for n, owns, idx in (("w_up", own_up, 0), ("w_down", own_down, 1)):
        grads[n], delta[n], new_m[n], new_v[n] = _adamw_layers(
            w_loc[n], owns, [lands[idx] for _, lands in ffn_sent], m_loc[n], v_loc[n], _ADAM_ROWS[n],
            name=f"adamw_{n}")
    for d in (grads, delta, new_m, new_v):
        d["w_up"] = up_t(d["w_up"])
    (d_out, d_pool, little_slabs), (land_out, land_pool, land_little) = _send_wait(
        send_rest, delta["w_down"], name="wait_rest")
    (d_in,), (land_in,) = _send_wait(send_in, land_out, name="wait_w_in")
    pool_2d = (n_groups * pool_rows, POOL_GROUP)
    own_pool = lax.dynamic_slice_in_dim(d_pool, me * pool_rows, pool_rows, 1)
    for n, own, land, shp in (
            ("w_in", lax.dynamic_index_in_dim(d_in, me, 0, keepdims=False), land_in, w_in.shape),
            ("w_out", lax.dynamic_slice_in_dim(d_out, me * out_rows, out_rows, 0), land_out, w_out.shape),
            ("pool_w", own_pool.reshape(pool_2d), land_pool.reshape((N_DEV - 1,) + pool_2d), (1,) + pool_2d)):
        outs = _adamw_layers(w_loc[n].reshape(shp), [own], [land], m_loc[n].reshape(shp), v_loc[n].reshape(shp),
                             _ADAM_ROWS[n], name=f"adamw_{n}")
        grads[n], delta[n], new_m[n], new_v[n] = (o.reshape(w_loc[n].shape) for o in outs)
    own_little = lax.dynamic_index_in_dim(little_slabs, me, 0, keepdims=False)
    g_little = _unpack(_sum_slabs(land_little, own=own_little, name="sum_little"),
                       [_shard_shape(s, a) for _, s, a in _LITTLE])
    grads.update({n: gl for (n, _, _), gl in zip(_LITTLE, g_little)})
    _, (everyone,) = _send_wait(send_small, delta["w_in"], name="wait_small")
    summed = _unpack(_sum_slabs(everyone, name="sum_small"),
                     [(1, 1)] + [s for _, s in replicated] + [(N_META, D_MODEL)])
    loss = summed[0].reshape(())
    grads.update({n: gr for (n, _), gr in zip(replicated, summed[1:-1])})
    grads["meta_tokens"] = lax.dynamic_slice_in_dim(summed[-1], me * out_rows, out_rows, 1)
    at_least_2d = lambda a: a.reshape((1,) * (2 - a.ndim) + a.shape)
    rest = [n for n in names if n not in _MATMUL_WEIGHTS]
    outs = _adamw_many([at_least_2d(w_loc[n]) for n in rest], [at_least_2d(grads[n]) for n in rest],
                       [at_least_2d(m_loc[n]) for n in rest], [at_least_2d(v_loc[n]) for n in rest],
                       name="adamw_rest")
    for i, n in enumerate(rest):
        delta[n], new_m[n], new_v[n] = (o.reshape(w_loc[n].shape) for o in outs[3 * i:3 * i + 3])
    return (loss, grad_x[None], *[grads[n] for n in names], *[delta[n] for n in names],
            *[new_m[n] for n in names], *[new_v[n] for n in names])
```

```python
import functools

import jax
import jax.numpy as jnp
from jax import lax
from jax.experimental import pallas as pl
from jax.experimental.pallas import tpu as pltpu

F32 = jnp.float32
BF16 = jnp.bfloat16

N_DEV = 8
DEPTH = 2
D_MODEL = 1024
N_META = 16
FOX_HEADS = 8
FOX_HEAD_DIM = 64
FOX_WIDTH = 512
CONV_CH = 512
CONV_WIDTH = 31
POOL_WINDOWS = (2, 4, 8, 16)
POOL_GROUP = 256
D_FF = 2816
FFN_CONV_WIDTH = 3
RMS_EPS = 1e-6
LN_EPS = 1e-5
ADAM_LR = 0.001
ADAM_B1 = 0.9
ADAM_B2 = 0.999
ADAM_EPS = 1e-08
ADAM_WD = 0.01
ADAM_STEP = 10

CHUNK = 128
HALO = 32
NEG_BIG = -1e30
FLAT_W = 1024


def _round_up(n, m):
    return (n + m - 1) // m * m


def _sigmoid(x):
    return 1.0 / (1.0 + jnp.exp(-x))


def _fold8(p):
    acc = p[0:8, :]
    for r in range(1, p.shape[0] // 8):
        acc = acc + p[8 * r:8 * r + 8, :]
    return acc


def _mm(a, b, *, name, tb=False, tm=None, tn=None, tk=None, out_dtype=F32, res=None,
        a_map=None, b_map=None, o_map=None, out_shape=None, dims=None, after=None, first_tile_t=False):
    if dims is None:
        m, k = a.shape
        n = b.shape[-2] if tb else b.shape[-1]
    else:
        m, n, k = dims
    tm, tn, tk = tm or m, tn or n, tk or k
    assert m % tm == 0 and n % tn == 0 and k % tk == 0, (name, m, n, k, tm, tn, tk)
    nk = k // tk
    a_map = a_map or (lambda i, j, kk: (i, kk))
    b_map = b_map or ((lambda i, j, kk: (j, kk)) if tb else (lambda i, j, kk: (kk, j)))
    o_map = o_map or (lambda i, j, kk: (i, j))
    out_shape = out_shape or (m, n)
    contract = (((1,), (1,)), ((), ())) if tb else (((1,), (0,)), ((), ()))
    has_res = res is not None

    def body(*refs):
        a_ref, b_ref = refs[0], refs[1]
        res_ref = refs[2] if has_res else None
        o_ref = refs[2 + has_res + (after is not None)]
        p = lax.dot_general(a_ref[...], b_ref[...], contract, preferred_element_type=F32)
        if nk == 1:
            if has_res:
                p = p + res_ref[...]
            o_ref[...] = p.astype(o_ref.dtype)
            if first_tile_t:
                @pl.when(pl.program_id(1) == 0)
                def _():
                    refs[3 + has_res + (after is not None)][...] = p.astype(BF16).T
        else:
            acc_ref = refs[-1]
            kk = pl.program_id(2)

            @pl.when(kk == 0)
            def _():
                acc_ref[...] = p

            @pl.when(kk > 0)
            def _():
                acc_ref[...] += p

            @pl.when(kk == nk - 1)
            def _():
                r = acc_ref[...]
                if has_res:
                    r = r + res_ref[...]
                o_ref[...] = r.astype(o_ref.dtype)

    in_specs = [pl.BlockSpec((tm, tk), a_map), pl.BlockSpec((tn, tk) if tb else (tk, tn), b_map)]
    operands = [a, b]
    if has_res:
        in_specs.append(pl.BlockSpec((tm, tn), o_map))
        operands.append(res)
    if after is not None:
        in_specs.append(pl.BlockSpec(memory_space=pl.ANY))
        operands.append(after)
    out_specs, out_struct = pl.BlockSpec((tm, tn), o_map), jax.ShapeDtypeStruct(out_shape, out_dtype)
    if first_tile_t:
        assert tm == m and nk == 1
        out_specs = [out_specs, pl.BlockSpec((tn, m), lambda i, j, kk: (0, 0))]
        out_struct = [out_struct, jax.ShapeDtypeStruct((tn, m), BF16)]
    return pl.pallas_call(
        body, name=name, grid=(m // tm, n // tn, nk),
        in_specs=in_specs, out_specs=out_specs, out_shape=out_struct,
        scratch_shapes=[pltpu.VMEM((tm, tn), F32)] if nk > 1 else [],
    )(*operands)


def _transpose(x, *, name, out_dtype):
    r, c = x.shape
    assert r % CHUNK == 0

    def body(x_ref, o_ref):
        o_ref[...] = x_ref[...].astype(o_ref.dtype).T

    return pl.pallas_call(
        body, name=name, grid=(r // CHUNK,),
        in_specs=[pl.BlockSpec((CHUNK, c), lambda i: (i, 0))],
        out_specs=pl.BlockSpec((c, CHUNK), lambda i: (0, i)),
        out_shape=jax.ShapeDtypeStruct((c, r), out_dtype),
    )(x)


def _rms_fwd(x, g, *, name, out_dtype, after=None, transposed=False):
    lp, dm = x.shape
    tr = CHUNK if transposed else lp // 4
    behind = [] if after is None else [after]

    def body(x_ref, g_ref, *rest):
        xv = x_ref[...]
        r = lax.rsqrt(jnp.mean(xv * xv, axis=-1, keepdims=True) + RMS_EPS)
        y = xv * r * g_ref[...]
        if transposed:
            yb = y.astype(out_dtype)
            rest[-2][...] = yb
            rest[-1][...] = yb.T
        else:
            rest[-1][...] = y.astype(out_dtype)

    row = pl.BlockSpec((tr, dm), lambda i: (i, 0))
    out_specs, out_shape = row, jax.ShapeDtypeStruct((lp, dm), out_dtype)
    if transposed:
        out_specs = [row, pl.BlockSpec((dm, tr), lambda i: (0, i))]
        out_shape = [out_shape, jax.ShapeDtypeStruct((dm, lp), out_dtype)]
    return pl.pallas_call(
        body, name=name, grid=(lp // tr,),
        in_specs=[row, pl.BlockSpec((1, dm), lambda i: (0, 0))] + [pl.BlockSpec(memory_space=pl.ANY)] * len(behind),
        out_specs=out_specs, out_shape=out_shape,
    )(x, g, *behind)


def _rms_bwd_rows(x_ref, g_ref, dnv, dres_ref, dh_ref, dhb_ref, dg_ref, first):
    xv = x_ref[...]
    r = lax.rsqrt(jnp.mean(xv * xv, axis=-1, keepdims=True) + RMS_EPS)
    xhat = xv * r

    @pl.when(first)
    def _():
        dg_ref[...] = jnp.zeros_like(dg_ref)

    dg_ref[...] += jnp.sum(dnv * xhat, axis=0, keepdims=True)
    dxhat = dnv * g_ref[...]
    dx = r * (dxhat - xhat * jnp.mean(dxhat * xhat, axis=-1, keepdims=True))
    dh = dres_ref[...] + dx
    dh_ref[...] = dh
    dhb_ref[...] = dh.astype(BF16)


def _mm_rms_bwd(a, b, x, g, dres, *, name, tk, tb=False, a_map=None, after=None):
    lp, dm = x.shape
    tm = lp // 4
    nk = (b.shape[1] if tb else b.shape[0]) // tk
    a_map = a_map or (lambda i, kk: (i, kk))
    contract = (((1,), (1,)), ((), ())) if tb else (((1,), (0,)), ((), ()))
    behind = [] if after is None else [after]

    def body(a_ref, b_ref, x_ref, g_ref, dres_ref, *rest):
        dh_ref, dhb_ref, dg_ref, acc_ref = rest[len(behind):]
        i, kk = pl.program_id(0), pl.program_id(1)
        p = lax.dot_general(a_ref[...], b_ref[...], contract, preferred_element_type=F32)

        @pl.when(kk == 0)
        def _():
            acc_ref[...] = p

        @pl.when(kk > 0)
        def _():
            acc_ref[...] += p

        @pl.when(kk == nk - 1)
        def _():
            _rms_bwd_rows(x_ref, g_ref, acc_ref[...], dres_ref, dh_ref, dhb_ref, dg_ref, i == 0)

    row = pl.BlockSpec((tm, dm), lambda i, kk: (i, 0))
    vec = pl.BlockSpec((1, dm), lambda i, kk: (0, 0))
    b_spec = pl.BlockSpec((dm, tk), lambda i, kk: (0, kk)) if tb else pl.BlockSpec((tk, dm), lambda i, kk: (kk, 0))
    return pl.pallas_call(
        body, name=name, grid=(lp // tm, nk),
        in_specs=[pl.BlockSpec((tm, tk), a_map), b_spec, row, vec, row] + [pl.BlockSpec(memory_space=pl.ANY)] * len(behind),
        out_specs=[row, row, vec],
        out_shape=[jax.ShapeDtypeStruct((lp, dm), F32), jax.ShapeDtypeStruct((lp, dm), BF16),
                   jax.ShapeDtypeStruct((1, dm), F32)],
        scratch_shapes=[pltpu.VMEM((tm, dm), F32)],
    )(a, b, x, g, dres, *behind)


def _rms_bwd(x, g, dn, dres, *, name):
    lp, dm = x.shape
    tr = lp // 4

    def body(x_ref, g_ref, dn_ref, dres_ref, dh_ref, dhb_ref, dg_ref):
        _rms_bwd_rows(x_ref, g_ref, dn_ref[...], dres_ref, dh_ref, dhb_ref, dg_ref, pl.program_id(0) == 0)

    row = pl.BlockSpec((tr, dm), lambda i: (i, 0))
    vec = pl.BlockSpec((1, dm), lambda i: (0, 0))
    return pl.pallas_call(
        body, name=name, grid=(lp // tr,),
        in_specs=[row, vec, row, row], out_specs=[row, row, vec],
        out_shape=[jax.ShapeDtypeStruct((lp, dm), F32), jax.ShapeDtypeStruct((lp, dm), BF16),
                   jax.ShapeDtypeStruct((1, dm), F32)],
    )(x, g, dn, dres)


def _loss_head(h, g, tgt, n_real, *, name):
    lp, dm = h.shape
    tr = lp // 4

    def body(x_ref, g_ref, t_ref, loss_ref, dh_ref, dhb_ref, dg_ref):
        i = pl.program_id(0)
        xv = x_ref[...]
        r = lax.rsqrt(jnp.mean(xv * xv, axis=-1, keepdims=True) + RMS_EPS)
        xhat = xv * r
        gv = g_ref[...]
        y = xhat * gv
        t = i * tr + lax.broadcasted_iota(jnp.int32, (tr, 1), 0)
        valid = (t >= N_META) & (t < n_real)
        diff = jnp.where(valid, y - t_ref[...], 0.0)

        @pl.when(i == 0)
        def _():
            loss_ref[...] = jnp.zeros_like(loss_ref)
            dg_ref[...] = jnp.zeros_like(dg_ref)

        row_sq = jnp.sum(diff * diff, axis=-1, keepdims=True) * (1.0 / dm)
        part = 0.5 * jnp.sum(row_sq, axis=0, keepdims=True)
        loss_ref[...] += jnp.broadcast_to(part, loss_ref.shape)
        dy = diff * (1.0 / dm)
        dg_ref[...] += jnp.sum(dy * xhat, axis=0, keepdims=True)
        dxhat = dy * gv
        dx = r * (dxhat - xhat * jnp.mean(dxhat * xhat, axis=-1, keepdims=True))
        dh_ref[...] = dx
        dhb_ref[...] = dx.astype(BF16)

    row = pl.BlockSpec((tr, dm), lambda i: (i, 0))
    vec = pl.BlockSpec((1, dm), lambda i: (0, 0))
    return pl.pallas_call(
        body, name=name, grid=(lp // tr,),
        in_specs=[row, vec, row],
        out_specs=[pl.BlockSpec((1, 128), lambda i: (0, 0)), row, row, vec],
        out_shape=[jax.ShapeDtypeStruct((1, 128), F32), jax.ShapeDtypeStruct((lp, dm), F32),
                   jax.ShapeDtypeStruct((lp, dm), BF16), jax.ShapeDtypeStruct((1, dm), F32)],
    )(h, g, tgt)


def _tri(upper):
    r = lax.broadcasted_iota(jnp.int32, (CHUNK, CHUNK), 0)
    c = lax.broadcasted_iota(jnp.int32, (CHUNK, CHUNK), 1)
    return jnp.where(r <= c if upper else r >= c, 1.0, 0.0).astype(F32)


def _fox_prep(f_t, b_f, *, name):
    nh, lp = f_t.shape
    nch = lp // CHUNK

    def body(f_ref, b_ref, c_ref):
        tri = _tri(True)
        carry = jnp.zeros((nh, 1), F32)
        for blk in range(nch):
            cols = slice(blk * CHUNK, (blk + 1) * CHUNK)
            z = f_ref[:, cols] + b_ref[...]
            logf = jnp.minimum(z, 0.0) - jnp.log(1.0 + jnp.exp(-jnp.abs(z)))
            cb = jnp.dot(logf, tri, preferred_element_type=F32, precision=lax.Precision.HIGHEST)
            c_ref[:, cols] = cb + carry
            carry = carry + jnp.sum(logf, axis=1, keepdims=True)

    return pl.pallas_call(
        body, name=name, out_shape=jax.ShapeDtypeStruct((nh, lp), F32),
    )(f_t, b_f)


def _fox_bwd(dc, f_t, b_f, *, name):
    nh, lp = f_t.shape
    nch = lp // CHUNK

    def body(dc_ref, f_ref, b_ref, df_ref, db_ref):
        tri = _tri(False)
        carry = jnp.zeros((nh, 1), F32)
        db = jnp.zeros((nh, 1), F32)
        df_ref[...] = jnp.zeros_like(df_ref)
        for blk in reversed(range(nch)):
            cols = slice(blk * CHUNK, (blk + 1) * CHUNK)
            dcb = dc_ref[:, cols]
            dlogf = jnp.dot(dcb, tri, preferred_element_type=F32, precision=lax.Precision.HIGHEST) + carry
            carry = carry + jnp.sum(dcb, axis=1, keepdims=True)
            z = f_ref[:, cols] + b_ref[...]
            dz = dlogf * _sigmoid(-z)
            df_ref[0:nh, cols] = dz
            db = db + jnp.sum(dz, axis=1, keepdims=True)
        db_ref[...] = db

    return pl.pallas_call(
        body, name=name,
        out_shape=[jax.ShapeDtypeStruct((128, lp), F32), jax.ShapeDtypeStruct((nh, 1), F32)],
    )(dc, f_t, b_f)


ATTN_BLOCKS = 4


def _attn_blocks(lp):
    tq = lp // ATTN_BLOCKS
    return tq, [(i * tq, min(lp, _round_up((i + 1) * tq, CHUNK))) for i in range(ATTN_BLOCKS)]


ATTN_SCALE = FOX_HEAD_DIM ** -0.5


def _attn_probs(q2s, k_h, c_row, row0, n):
    tq = q2s.shape[0]
    lo = row0 // CHUNK * CHUNK
    logits = []
    for c0, c1 in ([(0, lo)] if lo else []) + [(lo, n)]:
        s = lax.dot_general(q2s, k_h[c0:c1], (((1,), (1,)), ((), ())), preferred_element_type=F32) - c_row[:, c0:c1]
        if c1 > row0:
            t = row0 + lax.broadcasted_iota(jnp.int32, (tq, c1 - c0), 0)
            sidx = c0 + lax.broadcasted_iota(jnp.int32, (tq, c1 - c0), 1)
            s = jnp.where(sidx <= t, s, NEG_BIG)
        logits.append((s, c0, c1))
    m = functools.reduce(jnp.maximum, [jnp.max(s, axis=1, keepdims=True) for s, _, _ in logits])
    ps = [(jnp.exp(s - m), c0, c1) for s, c0, c1 in logits]
    inv = 1.0 / sum(jnp.sum(p, axis=1, keepdims=True) for p, _, _ in ps)
    return [(p * inv, c0, c1) for p, c0, c1 in ps]


def _attn_fwd(qkv, c3, *, name):
    lp = qkv.shape[0]
    tq, blocks = _attn_blocks(lp)

    def body(q_ref, k_ref, v_ref, c_ref, o_ref):
        lane = lax.broadcasted_iota(jnp.int32, (1, 128), 1)
        zero = jnp.zeros((), BF16)
        for i, (row0, n) in enumerate(blocks):
            q2s = q_ref[row0:row0 + tq, :] * ATTN_SCALE
            acc = jnp.zeros((tq, 128), F32)
            for hd in range(2):
                sel = (lane < 64) if hd == 0 else (lane >= 64)
                k_h = jnp.where(sel, k_ref[0:n, :], zero)
                v_h = jnp.where(sel, v_ref[0:n, :], zero)
                for p, c0, c1 in _attn_probs(q2s, k_h, c_ref[hd:hd + 1, 0:n], row0, n):
                    acc = acc + jnp.dot(p.astype(BF16), v_h[c0:c1], preferred_element_type=F32)
            o_ref[row0:row0 + tq, :] = acc.astype(BF16)

    blk = lambda off: pl.BlockSpec((lp, 128), lambda p: (0, off + p))
    return pl.pallas_call(
        body, name=name, grid=(4,),
        in_specs=[blk(0), blk(4), blk(8), pl.BlockSpec((None, 2, lp), lambda p: (p, 0, 0))],
        out_specs=pl.BlockSpec((lp, 128), lambda p: (0, p)),
        out_shape=jax.ShapeDtypeStruct((lp, FOX_WIDTH), BF16),
    )(qkv, qkv, qkv, c3)


def _attn_bwd(qkv, q_t, dcat, do_t, c3, *, name, after):
    lp = qkv.shape[0]
    tq, blocks = _attn_blocks(lp)
    scale = FOX_HEAD_DIM ** -0.5

    def body(q_ref, k_ref, v_ref, qt_ref, do_ref, dot_ref, c_ref, _, dq_ref, dk_ref, dv_ref, dc_ref,
             dkt_acc, dvt_acc):
        lane = lax.broadcasted_iota(jnp.int32, (1, 128), 1)
        sub = lax.broadcasted_iota(jnp.int32, (128, 1), 0)
        zero = jnp.zeros((), BF16)
        dkt_acc[...] = jnp.zeros_like(dkt_acc)
        dvt_acc[...] = jnp.zeros_like(dvt_acc)
        dc_ref[...] = jnp.zeros_like(dc_ref)
        for i, (row0, n) in enumerate(blocks):
            rows = slice(row0, row0 + tq)
            q2s = q_ref[rows, :] * ATTN_SCALE
            do2 = do_ref[rows, :].astype(BF16)
            dq_acc = jnp.zeros((tq, 128), F32)
            for hd in range(2):
                sel = (lane < 64) if hd == 0 else (lane >= 64)
                sel_t = (sub < 64) if hd == 0 else (sub >= 64)
                k_h = jnp.where(sel, k_ref[0:n, :], zero)
                v_h = jnp.where(sel, v_ref[0:n, :], zero)
                qt_h = jnp.where(sel_t, qt_ref[:, rows], zero)
                dot_h = jnp.where(sel_t, dot_ref[:, rows], zero)
                segs = [(p, lax.dot_general(do2, v_h[c0:c1], (((1,), (1,)), ((), ())), preferred_element_type=F32),
                         c0, c1) for p, c0, c1 in _attn_probs(q2s, k_h, c_ref[hd:hd + 1, 0:n], row0, n)]
                delta = sum(jnp.sum(p * dp, axis=1, keepdims=True) for p, dp, _, _ in segs)
                for p, dp, c0, c1 in segs:
                    ds = p * (dp - delta)
                    dsb = ds.astype(BF16)
                    dq_acc = dq_acc + jnp.dot(dsb, k_h[c0:c1], preferred_element_type=F32)
                    dkt_acc[:, c0:c1] += jnp.dot(qt_h, dsb, preferred_element_type=F32)
                    dvt_acc[:, c0:c1] += jnp.dot(dot_h, p.astype(BF16), preferred_element_type=F32)
                    dc_ref[hd:hd + 1, c0:c1] -= jnp.sum(ds, axis=0, keepdims=True)
            dq_ref[rows, :] = (dq_acc * scale).astype(BF16)
        dk_ref[...] = (dkt_acc[...] * scale).astype(BF16).T
        dv_ref[...] = dvt_acc[...].astype(BF16).T

    blk = lambda off: pl.BlockSpec((lp, 128), lambda p: (0, off + p))
    blk_t = pl.BlockSpec((128, lp), lambda p: (p, 0))
    c_spec = pl.BlockSpec((None, 2, lp), lambda p: (p, 0, 0))
    return pl.pallas_call(
        body, name=name, grid=(4,),
        in_specs=[blk(0), blk(4), blk(8), blk_t, blk(0), blk_t, c_spec, pl.BlockSpec(memory_space=pl.ANY)],
        out_specs=[blk(0), blk(0), blk(0), c_spec],
        out_shape=[jax.ShapeDtypeStruct((lp, FOX_WIDTH), BF16)] * 3 + [jax.ShapeDtypeStruct((4, 2, lp), F32)],
        scratch_shapes=[pltpu.VMEM((128, lp), F32), pltpu.VMEM((128, lp), F32)],
    )(qkv, qkv, qkv, q_t, dcat, do_t, c3, after)


def _ln_stats(x):
    mu = jnp.mean(x, axis=-1, keepdims=True)
    xc = x - mu
    var = jnp.mean(xc * xc, axis=-1, keepdims=True)
    rstd = lax.rsqrt(var + LN_EPS)
    return xc * rstd, rstd


def _conv_fwd(agf, conv_w, conv_b, ln_g, ln_b, *, name):
    lp = agf.shape[0]
    nch = lp // CHUNK
    c = CONV_CH

    def body(a_ref, g_ref, w_ref, b_ref, lg_ref, lb_ref, u0_ref, u1_ref, u3_ref, u0s):
        u0s[0:HALO, :] = jnp.zeros((HALO, c), F32)

        def glu(ci, _):
            rows = pl.ds(pl.multiple_of(ci * CHUNK, CHUNK), CHUNK)
            u0 = a_ref[rows, :] * _sigmoid(g_ref[rows, :])
            u0_ref[rows, :] = u0
            u0s[pl.ds(pl.multiple_of(ci * CHUNK + HALO, 8), CHUNK), :] = u0
            return 0

        lax.fori_loop(0, nch, glu, 0)

        def conv(ci, _):
            r0 = pl.multiple_of(ci * CHUNK, CHUNK)
            rows = pl.ds(r0, CHUNK)
            for lg in range(c // 128):
                lanes = slice(lg * 128, (lg + 1) * 128)
                win = u0s[pl.ds(r0, CHUNK + HALO), lanes]
                acc = jnp.broadcast_to(b_ref[:, lanes], (CHUNK, 128))
                for k in range(CONV_WIDTH):
                    s = CONV_WIDTH - 1 - k
                    sh = win if s == 0 else pltpu.roll(win, s, 0)
                    acc = acc + w_ref[k:k + 1, lanes] * sh[HALO:HALO + CHUNK, :]
                u1_ref[rows, lanes] = acc
            xhat, _ = _ln_stats(u1_ref[rows, :])
            y = xhat * lg_ref[...] + lb_ref[...]
            u3_ref[rows, :] = (y * _sigmoid(y)).astype(BF16)
            return 0

        lax.fori_loop(0, nch, conv, 0)

    full = lambda shape: pl.BlockSpec(shape, lambda i: (0, 0))
    return pl.pallas_call(
        body, name=name, grid=(1,),
        in_specs=[pl.BlockSpec((lp, c), lambda i: (0, 0)), pl.BlockSpec((lp, c), lambda i: (0, 1)),
                  full((CONV_WIDTH, c)), full((1, c)), full((1, c)), full((1, c))],
        out_specs=[full((lp, c)), full((lp, c)), full((lp, c))],
        out_shape=[jax.ShapeDtypeStruct((lp, c), F32), jax.ShapeDtypeStruct((lp, c), F32),
                   jax.ShapeDtypeStruct((lp, c), BF16)],
        scratch_shapes=[pltpu.VMEM((lp + HALO, c), F32)],
    )(agf, agf, conv_w, conv_b, ln_g, ln_b)


def _conv_bwd(dcat, u0, u1, agf, conv_w, ln_g, ln_b, *, name):
    lp = agf.shape[0]
    nch = lp // CHUNK
    c = CONV_CH
    wlen = CHUNK + HALO

    def body(du3_ref, u0_ref, u1_ref, a_ref, g_ref, w_ref, lg_ref, lb_ref,
             dag_ref, dw_ref, db_ref, dlg_ref, dlb_ref, du1s, dwacc, vacc):
        du1s[lp:lp + HALO, :] = jnp.zeros((HALO, c), F32)
        dwacc[...] = jnp.zeros_like(dwacc)
        vacc[...] = jnp.zeros_like(vacc)

        def ln_bwd(ci, _):
            r0 = pl.multiple_of(ci * CHUNK, CHUNK)
            rows = pl.ds(r0, CHUNK)
            xhat, rstd = _ln_stats(u1_ref[rows, :])
            y = xhat * lg_ref[...] + lb_ref[...]
            sg = _sigmoid(y)
            du2 = du3_ref[rows, :] * (sg * (1.0 + y * (1.0 - sg)))
            vacc[0:8, :] += _fold8(du2 * xhat)
            vacc[8:16, :] += _fold8(du2)
            dxhat = du2 * lg_ref[...]
            du1 = rstd * (dxhat - jnp.mean(dxhat, axis=-1, keepdims=True)
                          - xhat * jnp.mean(dxhat * xhat, axis=-1, keepdims=True))
            vacc[16:24, :] += _fold8(du1)
            du1s[rows, :] = du1
            return 0

        lax.fori_loop(0, nch, ln_bwd, 0)

        def conv_bwd(ci, _):
            r0 = pl.multiple_of(ci * CHUNK, CHUNK)
            rows = pl.ds(r0, CHUNK)
            for lg in range(c // 128):
                lanes = slice(lg * 128, (lg + 1) * 128)
                dwin = du1s[pl.ds(r0, wlen), lanes]
                u0 = u0_ref[rows, lanes]
                acc = jnp.zeros((CHUNK, 128), F32)
                for k in range(CONV_WIDTH):
                    s = CONV_WIDTH - 1 - k
                    d_s = (dwin if s == 0 else pltpu.roll(dwin, wlen - s, 0))[0:CHUNK, :]
                    acc = acc + w_ref[k:k + 1, lanes] * d_s
                    dwacc[8 * k:8 * k + 8, lanes] += _fold8(d_s * u0)
                sg = _sigmoid(g_ref[rows, lanes])
                a = a_ref[rows, lanes]
                dag_ref[rows, lanes] = (acc * sg).astype(BF16)
                dag_ref[rows, slice(c + lg * 128, c + (lg + 1) * 128)] = (acc * a * sg * (1.0 - sg)).astype(BF16)
            return 0

        lax.fori_loop(0, nch, conv_bwd, 0)
        for k in range(CONV_WIDTH):
            dw_ref[k:k + 1, :] = jnp.sum(dwacc[8 * k:8 * k + 8, :], axis=0, keepdims=True)
        dlg_ref[...] = jnp.sum(vacc[0:8, :], axis=0, keepdims=True)
        dlb_ref[...] = jnp.sum(vacc[8:16, :], axis=0, keepdims=True)
        db_ref[...] = jnp.sum(vacc[16:24, :], axis=0, keepdims=True)

    full = lambda shape: pl.BlockSpec(shape, lambda i: (0, 0))
    vec = jax.ShapeDtypeStruct((1, c), F32)
    return pl.pallas_call(
        body, name=name, grid=(1,),
        in_specs=[pl.BlockSpec((lp, c), lambda i: (0, 1)), full((lp, c)), full((lp, c)),
                  pl.BlockSpec((lp, c), lambda i: (0, 0)), pl.BlockSpec((lp, c), lambda i: (0, 1)),
                  full((CONV_WIDTH, c)), full((1, c)), full((1, c))],
        out_specs=[full((lp, 2 * c)), full((CONV_WIDTH, c)), full((1, c)), full((1, c)), full((1, c))],
        out_shape=[jax.ShapeDtypeStruct((lp, 2 * c), BF16), jax.ShapeDtypeStruct((CONV_WIDTH, c), F32), vec, vec, vec],
        scratch_shapes=[pltpu.VMEM((lp + HALO, c), F32), pltpu.VMEM((8 * CONV_WIDTH, c), F32),
                        pltpu.VMEM((24, c), F32)],
    )(dcat, u0, u1, agf, agf, conv_w, ln_g, ln_b)


FFN_TILE = 256
FFN_PAD = 8


def _ffn_conv(xs, w_ref, b_ref, half, r0):
    win = xs[half, pl.ds(r0, CHUNK + FFN_PAD), :]
    acc = jnp.broadcast_to(b_ref[half], (CHUNK, FFN_TILE))
    for k in range(FFN_CONV_WIDTH):
        s = FFN_CONV_WIDTH - 1 - k
        sh = win if s == 0 else pltpu.roll(win, s, 0)
        acc = acc + w_ref[half, k:k + 1, :] * sh[FFN_PAD:FFN_PAD + CHUNK, :]
    return acc


def _ffn_up_act(n, w_up_t, w3, b3, *, name):
    lp = n.shape[0]
    nch, nt = lp // CHUNK, D_FF // FFN_TILE
    nt_dims = (((1,), (1,)), ((), ()))

    parts = 4

    def project(n_ref, wg_ref, wv_ref, u, part=None):
        rows = slice(0, lp) if part is None else slice(part * lp // parts, (part + 1) * lp // parts)
        for half, w_ref in ((0, wg_ref), (1, wv_ref)):
            u[half, FFN_PAD + rows.start:FFN_PAD + rows.stop, :] = lax.dot_general(
                n_ref[rows, :], w_ref[...], nt_dims, preferred_element_type=F32)

    def activate(u, w_ref, b_ref, act_ref, act_t_ref, gv_ref, up_ref, part):
        for ci in range(part * nch // parts, (part + 1) * nch // parts):
            r0 = ci * CHUNK
            rows = slice(r0, r0 + CHUNK)
            gate = _ffn_conv(u, w_ref, b_ref, 0, r0)
            val = _ffn_conv(u, w_ref, b_ref, 1, r0)
            gv_ref[0, rows, :] = gate.astype(BF16)
            gv_ref[1, rows, :] = val.astype(BF16)
            for half in range(2):
                up_ref[half, rows, :] = u[half, FFN_PAD + r0:FFN_PAD + r0 + CHUNK, :].astype(BF16)
            act = (gate * _sigmoid(gate) * val).astype(BF16)
            act_ref[rows, :] = act
            act_t_ref[:, rows] = act.T

    def body(n_ref, wg_ref, wv_ref, wg_next, wv_next, w_ref, b_ref, act_ref, act_t_ref, gv_ref, up_ref, u0, u1):
        j = pl.program_id(0)

        @pl.when(j == 0)
        def _():
            for u in (u0, u1):
                u[:, 0:FFN_PAD, :] = jnp.zeros((2, FFN_PAD, FFN_TILE), F32)
            project(n_ref, wg_ref, wv_ref, u0)

        for parity, (mine, other) in enumerate(((u0, u1), (u1, u0))):
            @pl.when(j % 2 == parity)
            def _(mine=mine, other=other):
                for part in range(parts):
                    project(n_ref, wg_next, wv_next, other, part)
                    activate(mine, w_ref, b_ref, act_ref, act_t_ref, gv_ref, up_ref, part)

    halves = pl.BlockSpec((2, lp, FFN_TILE), lambda j: (0, 0, j))
    rows_of = lambda half, ahead: pl.BlockSpec(
        (FFN_TILE, D_MODEL), lambda j: (half * nt + jnp.minimum(j + ahead, nt - 1), 0))
    return pl.pallas_call(
        body, name=name, grid=(nt,),
        in_specs=[pl.BlockSpec((lp, D_MODEL), lambda j: (0, 0)), rows_of(0, 0), rows_of(1, 0), rows_of(0, 1),
                  rows_of(1, 1), pl.BlockSpec((2, FFN_CONV_WIDTH, FFN_TILE), lambda j: (0, 0, j)),
                  pl.BlockSpec((2, 1, FFN_TILE), lambda j: (0, 0, j))],
        out_specs=[pl.BlockSpec((lp, FFN_TILE), lambda j: (0, j)), pl.BlockSpec((FFN_TILE, lp), lambda j: (j, 0)),
                   halves, halves],
        out_shape=[jax.ShapeDtypeStruct((lp, D_FF), BF16), jax.ShapeDtypeStruct((D_FF, lp), BF16),
                   jax.ShapeDtypeStruct((2, lp, D_FF), BF16), jax.ShapeDtypeStruct((2, lp, D_FF), BF16)],
        scratch_shapes=[pltpu.VMEM((2, lp + FFN_PAD, FFN_TILE), F32), pltpu.VMEM((2, lp + FFN_PAD, FFN_TILE), F32)],
    )(n, w_up_t, w_up_t, w_up_t, w_up_t, w3, b3)


def _ffn_act_bwd(up3, gv3, w3, dact, *, name):
    _, lp, f = up3.shape
    nch = lp // CHUNK
    wlen = CHUNK + FFN_PAD

    def body(up_ref, gv_ref, w_ref, dact_ref, dup_ref, dw_ref, db_ref, ds, wacc):
        for half in range(2):
            ds[half, lp:lp + FFN_PAD, :] = jnp.zeros((FFN_PAD, FFN_TILE), F32)
        wacc[...] = jnp.zeros_like(wacc)

        def act_bwd(ci, _):
            rows = pl.ds(pl.multiple_of(ci * CHUNK, CHUNK), CHUNK)
            gate, val = gv_ref[0, rows, :].astype(F32), gv_ref[1, rows, :].astype(F32)
            sg = _sigmoid(gate)
            da = dact_ref[rows, :].astype(F32)
            ds[0, rows, :] = da * val * (sg * (1.0 + gate * (1.0 - sg)))
            ds[1, rows, :] = da * (gate * sg)
            return 0

        lax.fori_loop(0, nch, act_bwd, 0, unroll=True)

        def conv_bwd(ci, _):
            r0 = pl.multiple_of(ci * CHUNK, CHUNK)
            rows = pl.ds(r0, CHUNK)
            for half in range(2):
                dwin = ds[half, pl.ds(r0, wlen), :]
                x = up_ref[half, rows, :].astype(F32)
                acc = jnp.zeros((CHUNK, FFN_TILE), F32)
                for k in range(FFN_CONV_WIDTH):
                    s = FFN_CONV_WIDTH - 1 - k
                    d_s = (dwin if s == 0 else pltpu.roll(dwin, wlen - s, 0))[0:CHUNK, :]
                    acc = acc + w_ref[half, k:k + 1, :] * d_s
                    wacc[half, 8 * k:8 * k + 8, :] += _fold8(d_s * x)
                wacc[half, 24:32, :] += _fold8(dwin[0:CHUNK, :])
                dup_ref[half, rows, :] = acc.astype(BF16)
            return 0

        lax.fori_loop(0, nch, conv_bwd, 0)
        for half in range(2):
            for k in range(FFN_CONV_WIDTH):
                dw_ref[half, k:k + 1, :] = jnp.sum(wacc[half, 8 * k:8 * k + 8, :], axis=0, keepdims=True)
            db_ref[half] = jnp.sum(wacc[half, 24:32, :], axis=0, keepdims=True)

    halves = pl.BlockSpec((2, lp, FFN_TILE), lambda j: (0, 0, j))
    taps = pl.BlockSpec((2, FFN_CONV_WIDTH, FFN_TILE), lambda j: (0, 0, j))
    bias = pl.BlockSpec((2, 1, FFN_TILE), lambda j: (0, 0, j))
    return pl.pallas_call(
        body, name=name, grid=(f // FFN_TILE,),
        in_specs=[halves, halves, taps, pl.BlockSpec((lp, FFN_TILE), lambda j: (0, j))],
        out_specs=[halves, taps, bias],
        out_shape=[jax.ShapeDtypeStruct((2, lp, f), BF16), jax.ShapeDtypeStruct((2, FFN_CONV_WIDTH, f), F32),
                   jax.ShapeDtypeStruct((2, 1, f), F32)],
        scratch_shapes=[pltpu.VMEM((2, lp + FFN_PAD, FFN_TILE), F32), pltpu.VMEM((2, 32, FFN_TILE), F32)],
    )(up3, gv3, w3, dact)


POOL_PAD = 16


def _inv_count(r0, w):
    t = r0 + lax.broadcasted_iota(jnp.int32, (CHUNK, 1), 0)
    return 1.0 / jnp.minimum(t + 1, w).astype(F32)


def _pool_fwd(n, pool_w, pool_b, pool_scale, h, *, name):
    lp, dm = n.shape
    nch = lp // CHUNK
    g = POOL_GROUP

    def body(n_ref, w_ref, b_ref, s_ref, h_ref, ho_ref, dt_ref, z_ref, xs, d_ref):
        gi = pl.program_id(0)
        xs[0:POOL_PAD, :] = jnp.zeros((POOL_PAD, g), F32)
        xs[POOL_PAD:POOL_PAD + lp, :] = n_ref[...]
        for idx, w in enumerate(POOL_WINDOWS):
            @pl.when(gi == idx)
            def _(w=w):
                def chunk(ci, _):
                    r0 = pl.multiple_of(ci * CHUNK, CHUNK)
                    win = xs[pl.ds(r0, CHUNK + POOL_PAD), :]
                    acc = win
                    for j in range(1, w):
                        acc = acc + pltpu.roll(win, j, 0)
                    x = win[POOL_PAD:POOL_PAD + CHUNK, :]
                    d = acc[POOL_PAD:POOL_PAD + CHUNK, :] * _inv_count(r0, w) - x
                    d_ref[pl.ds(r0, CHUNK), :] = d.astype(BF16)
                    dt_ref[:, pl.ds(r0, CHUNK)] = d.astype(BF16).T
                    return 0

                lax.fori_loop(0, nch, chunk, 0, unroll=True)

        z = jnp.dot(d_ref[...], w_ref[...], preferred_element_type=F32) + b_ref[...]
        z_ref[...] = z
        ho_ref[...] = h_ref[...] + z * s_ref[...]

    col = pl.BlockSpec((lp, g), lambda i: (0, i))
    vec = pl.BlockSpec((1, g), lambda i: (0, i))
    return pl.pallas_call(
        body, name=name, grid=(len(POOL_WINDOWS),),
        in_specs=[col, pl.BlockSpec((None, g, g), lambda i: (i, 0, 0)), vec, vec, col],
        out_specs=[col, pl.BlockSpec((g, lp), lambda i: (i, 0)), col],
        out_shape=[jax.ShapeDtypeStruct((lp, dm), F32), jax.ShapeDtypeStruct((dm, lp), BF16),
                   jax.ShapeDtypeStruct((lp, dm), F32)],
        scratch_shapes=[pltpu.VMEM((lp + POOL_PAD, g), F32), pltpu.VMEM((lp, g), BF16)],
    )(n, pool_w, pool_b, pool_scale, h)


def _pool_bwd(dy, z, pool_w, pool_scale, *, name):
    lp, dm = dy.shape
    nch = lp // CHUNK
    g = POOL_GROUP
    wlen = CHUNK + POOL_PAD

    def body(dy_ref, z_ref, w_ref, s_ref, dn_ref, dz_ref, dsc_ref, db_ref, ys, dd):
        gi = pl.program_id(0)
        dyv = dy_ref[...]
        dsc_ref[...] = jnp.sum(dyv * z_ref[...], axis=0, keepdims=True)
        dz = dyv * s_ref[...]
        db_ref[...] = jnp.sum(dz, axis=0, keepdims=True)
        dzb = dz.astype(BF16)
        dz_ref[...] = dzb
        dd[...] = lax.dot_general(dzb, w_ref[...], (((1,), (1,)), ((), ())), preferred_element_type=F32)
        ys[lp:lp + POOL_PAD, :] = jnp.zeros((POOL_PAD, g), F32)
        for idx, w in enumerate(POOL_WINDOWS):
            @pl.when(gi == idx)
            def _(w=w):
                def scale(ci, _):
                    r0 = pl.multiple_of(ci * CHUNK, CHUNK)
                    ys[pl.ds(r0, CHUNK), :] = dd[pl.ds(r0, CHUNK), :] * _inv_count(r0, w)
                    return 0

                lax.fori_loop(0, nch, scale, 0, unroll=True)

                def chunk(ci, _):
                    r0 = pl.multiple_of(ci * CHUNK, CHUNK)
                    win = ys[pl.ds(r0, wlen), :]
                    acc = win
                    for j in range(1, w):
                        acc = acc + pltpu.roll(win, wlen - j, 0)
                    dn_ref[pl.ds(r0, CHUNK), :] = acc[0:CHUNK, :] - dd[pl.ds(r0, CHUNK), :]
                    return 0

                lax.fori_loop(0, nch, chunk, 0, unroll=True)

    col = pl.BlockSpec((lp, g), lambda i: (0, i))
    vec = pl.BlockSpec((1, g), lambda i: (0, i))
    return pl.pallas_call(
        body, name=name, grid=(len(POOL_WINDOWS),),
        in_specs=[col, col, pl.BlockSpec((None, g, g), lambda i: (i, 0, 0)), vec],
        out_specs=[col, col, vec, vec],
        out_shape=[jax.ShapeDtypeStruct((lp, dm), F32), jax.ShapeDtypeStruct((lp, dm), BF16),
                   jax.ShapeDtypeStruct((1, dm), F32), jax.ShapeDtypeStruct((1, dm), F32)],
        scratch_shapes=[pltpu.VMEM((lp + POOL_PAD, g), F32), pltpu.VMEM((lp, g), F32)],
    )(dy, z, pool_w, pool_scale)


def _ffn_fwd(h, g, weight, w3, b3, tag):
    n, n_t = _rms_fwd(h, g, name=f"rms_ffn{tag}", out_dtype=BF16, transposed=True)
    w_up_t = weight("up", n)
    act, act_t, gv3, up3 = _ffn_up_act(n, w_up_t, w3, b3, name=f"ffn_up_act{tag}")
    w_down = weight("down", act)
    h_out = _mm(act, w_down, name=f"mm_down{tag}", tn=256, res=h)
    return h_out, (n_t, up3, gv3, act_t), w_up_t, w_down


def _ffn_bwd(dh, dhb, h, g, saved, w_up_t, w3, w_down, tag, after=None):
    lp = h.shape[0]
    n_t, up3, gv3, act_t = saved
    dw_down = _mm(act_t, dhb, name=f"mm_dwdown{tag}", tm=704, out_dtype=BF16)
    dact = _mm(dhb, w_down, name=f"mm_dact{tag}", tb=True, tn=D_FF // 2, out_dtype=BF16, after=after)
    dup3, dcw, dcb = _ffn_act_bwd(up3, gv3, w3, dact, name=f"ffn_act_bwd{tag}")
    dup2 = dup3.reshape(2 * lp, D_FF)
    dw_up = _mm_dw_up(n_t, dup2, name=f"mm_dwup{tag}")
    dh_in, dh_in_b, dg = _mm_rms_bwd(dup2, w_up_t, h, g, dh, name=f"mm_dnffn{tag}", tk=D_FF // 2,
                                     a_map=lambda i, kk: (4 * (kk // 2) + i, kk % 2))
    return dh_in, dh_in_b, (dg, dw_up, dcw, dcb, dw_down)


def _local_step(x, tgt, wt):
    seq = x.shape[0]
    n_real = N_META + seq
    lp = _round_up(n_real, CHUNK)
    pad = jnp.zeros((lp - n_real, D_MODEL), F32)
    h0 = jnp.concatenate([wt["meta"], x, pad], axis=0)
    tgt_p = jnp.concatenate([jnp.zeros((N_META, D_MODEL), F32), tgt, pad], axis=0)
    w_in_p = wt["w_in_p"]

    n0, n0_t = _rms_fwd(h0, wt["g_even"], name="rms_even", out_dtype=BF16, after=wt["ffn_started"], transposed=True)
    qkv, q_t = _mm(n0, w_in_p, name="mm_qkv", tn=FOX_WIDTH, dims=(lp, 3 * FOX_WIDTH, D_MODEL), out_dtype=BF16,
                   first_tile_t=True)
    ag = _mm(n0, w_in_p, name="mm_ag", tn=512, dims=(lp, 2 * CONV_CH, D_MODEL),
             b_map=lambda i, j, k: (0, 3 + j))
    f_t = _mm(wt["wf_t"], n0, name="mm_ft", tb=True)
    c_row = _fox_prep(f_t, wt["b_f"], name="fox_prep")
    c3 = c_row.reshape(4, 2, lp)
    o = _attn_fwd(qkv, c3, name="attn_fwd")
    u0, u1, u3 = _conv_fwd(ag, wt["conv_w"], wt["conv_b"], wt["ln_g"], wt["ln_b"], name="conv_fwd")
    cat = jnp.concatenate([o, u3], axis=1)
    h1 = _mm(cat, wt["w_out"], name="mm_out", tn=512, res=h0)
    h2, saved0, w_up0, w_down0 = _ffn_fwd(h1, wt["ffn_norm"][0:1], functools.partial(wt["ffn_weight"], 0),
                                          wt["fcw3"][0], wt["fcb3"][0], 0)

    n2 = _rms_fwd(h2, wt["g_odd"], name="rms_odd", out_dtype=F32)
    h3, dpool_t, z = _pool_fwd(n2, wt["pool_w"], wt["pool_b"], wt["pool_scale"], h2, name="pool_fwd")
    h4, saved1, w_up1, w_down1 = _ffn_fwd(h3, wt["ffn_norm"][1:2], functools.partial(wt["ffn_weight"], 1),
                                          wt["fcw3"][1], wt["fcb3"][1], 1)

    loss, dh4, dh4b, d_gfinal = _loss_head(h4, wt["g_final"], tgt_p, n_real, name="loss_head")

    dh3, dh3b, gf1 = _ffn_bwd(dh4, dh4b, h3, wt["ffn_norm"][1:2], saved1, w_up1, wt["fcw3"][1], w_down1, 1)
    send1, token1 = _send_ffn_grads(gf1[1], gf1[4], 1)
    dn2, dzb, d_pscale, d_pb = _pool_bwd(dh3, z, wt["pool_w"], wt["pool_scale"], name="pool_bwd")
    d_pw = _mm(dpool_t, dzb, name="mm_dpoolw", tm=POOL_GROUP, tn=POOL_GROUP, dims=(D_MODEL, POOL_GROUP, lp),
               b_map=lambda i, j, k: (0, i), o_map=lambda i, j, k: (i, 0), out_shape=(D_MODEL, POOL_GROUP),
               out_dtype=BF16)
    dh2, dh2b, d_godd = _rms_bwd(h2, wt["g_odd"], dn2, dh3, name="rms_bwd_odd")
    dh1, dh1b, gf0 = _ffn_bwd(dh2, dh2b, h1, wt["ffn_norm"][0:1], saved0, w_up0, wt["fcw3"][0], w_down0, 0,
                              after=token1)

    send0, token0 = _send_ffn_grads(gf0[1], gf0[4], 0)
    cat_t = _transpose(cat, name="t_cat", out_dtype=BF16)
    d_wout = _mm(cat_t, dh1b, name="mm_dwout", tm=512, out_dtype=BF16)
    dcat, do_t = _mm(dh1b, wt["w_out"], name="mm_dcat", tb=True, tn=FOX_WIDTH, after=token0, first_tile_t=True)
    dag, d_convw, d_convb, d_lng, d_lnb = _conv_bwd(dcat, u0, u1, ag, wt["conv_w"], wt["ln_g"], wt["ln_b"],
                                                    name="conv_bwd")
    layers = lambda i: jnp.stack([gf0[i], gf1[i]])
    grads = dict(
        conv_w=d_convw[None], w_out=d_wout, mix_norm_odd=d_godd,
        pool_w=d_pw.reshape(len(POOL_WINDOWS), POOL_GROUP, POOL_GROUP),
        pool_b=d_pb.reshape(1, len(POOL_WINDOWS), POOL_GROUP), pool_scale=d_pscale, w_up=(gf0[1], gf1[1]),
        ffn_conv_w=layers(2).transpose(0, 2, 1, 3).reshape(DEPTH, FFN_CONV_WIDTH, 2 * D_FF), w_down=(gf0[4], gf1[4]))
    send_rest, token_rest = _send_rest_grads(grads)
    dq, dk, dv, dc3 = _attn_bwd(qkv, q_t, dcat, do_t, c3, name="attn_bwd", after=token_rest)
    df_t, d_bf = _fox_bwd(dc3.reshape(FOX_HEADS, lp), f_t, wt["b_f"], name="fox_bwd")
    df = _transpose(df_t, name="t_df", out_dtype=BF16)
    dproj = jnp.concatenate([dq, dk, dv, dag, df], axis=1)
    grads["w_in"] = _mm_dw_in(n0_t, dproj, name="mm_dwin")
    send_in, token_in = _send_start(
        [grads["w_in"]], [jax.ShapeDtypeStruct((N_DEV - 1, D_MODEL, _IN_SHARD), BF16)], [(0, _by_owner, 0, None)],
        name="send_w_in")
    dh0, _, d_geven = _mm_rms_bwd(dproj, w_in_p, h0, wt["g_even"], dh1, name="mm_dn0", tb=True, tk=896,
                                  after=token_in)
    grads.update(
        meta_tokens=dh0[0:N_META], mix_norm_even=d_geven, b_f=d_bf.reshape(1, FOX_HEADS), conv_b=d_convb, ln_g=d_lng,
        ln_b=d_lnb, ffn_norm=jnp.concatenate([gf0[0], gf1[0]], axis=0),
        ffn_conv_b=layers(3).reshape(DEPTH, 2 * D_FF), final_norm=d_gfinal.reshape(D_MODEL),
        sends=(send0, send1, send_rest, send_in))
    return loss, dh0[N_META:n_real], grads


_LITTLE = (("conv_w", (1, 31, 512), 2), ("mix_norm_odd", (1, 1024), 1), ("pool_b", (1, 4, 256), 2),
           ("pool_scale", (1, 1024), 1), ("ffn_conv_w", (2, 3, 5632), 2))


def _send_rest_grads(g):
    out_rows, pool_rows, groups = D_MODEL // N_DEV, POOL_GROUP // N_DEV, len(POOL_WINDOWS)
    little_slabs = _pack([_full_to_slabs(g[n], s, a) for n, s, a in _LITTLE], F32, lead=(N_DEV,), align=8)
    land = lambda shape, dtype: jax.ShapeDtypeStruct((N_DEV - 1,) + shape, dtype)
    handle, token = _send_start(
        [g["w_out"], g["pool_w"], little_slabs],
        [land((out_rows, D_MODEL), BF16), land((groups, pool_rows, POOL_GROUP), BF16), land(little_slabs.shape[1:], F32)],
        [(0, _row_block(out_rows), 0, None), (1, _row_block(pool_rows, axis=1), 1, None), (2, _by_owner, 2, None)],
        name="send_rest")
    return handle, token


def _send_ffn_grads(dw_up, dw_down, tag):
    rows = D_FF // N_DEV
    lands = [jax.ShapeDtypeStruct((N_DEV - 1,) + dw_up.shape[1:], BF16),
             jax.ShapeDtypeStruct((N_DEV - 1, rows, D_MODEL), BF16)]
    return _send_start([dw_up, dw_down], lands, [(0, _by_owner, 0, None), (1, _row_block(rows), 1, None)],
                       name=f"send_ffn{tag}")


_QKV = 3 * FOX_WIDTH
_GLU0 = _QKV + FOX_HEADS
_IN_COLS = _GLU0 + 2 * CONV_CH
_F_PAD = 128


_IN_SHARD = _IN_COLS // N_DEV
_UP_SHARD = 2 * D_FF // N_DEV
_ROW_TILE = 256


def _assemble_w_in(st, *, name):
    tr = _ROW_TILE

    def body(s_ref, o_ref):
        full = jnp.concatenate([s_ref[i].astype(F32) for i in range(N_DEV)], axis=1)
        parts = [full[:, :_QKV], full[:, _GLU0:], full[:, _QKV:_GLU0], jnp.zeros((tr, _F_PAD - FOX_HEADS), F32)]
        o_ref[...] = jnp.concatenate(parts, axis=1).astype(BF16)

    return pl.pallas_call(
        body, name=name, grid=(D_MODEL // tr,),
        in_specs=[pl.BlockSpec((N_DEV, tr, _IN_SHARD), lambda i: (0, i, 0))],
        out_specs=pl.BlockSpec((tr, _QKV + 2 * CONV_CH + _F_PAD), lambda i: (i, 0)),
        out_shape=jax.ShapeDtypeStruct((D_MODEL, _QKV + 2 * CONV_CH + _F_PAD), BF16),
    )(st)


def _mm_dw_in(n_t, dproj, *, name):
    dm, lp = n_t.shape
    tr = _ROW_TILE
    ag0 = _QKV + 2 * CONV_CH

    def body(a_ref, b_ref, o_ref):
        r = jnp.dot(a_ref[...], b_ref[...], preferred_element_type=F32)
        full = jnp.concatenate([r[:, :_QKV], r[:, ag0:ag0 + FOX_HEADS], r[:, _QKV:ag0]], axis=1)
        for i in range(N_DEV):
            o_ref[i] = full[:, i * _IN_SHARD:(i + 1) * _IN_SHARD].astype(BF16)

    return pl.pallas_call(
        body, name=name, grid=(dm // tr,),
        in_specs=[pl.BlockSpec((tr, lp), lambda i: (i, 0)), pl.BlockSpec(dproj.shape, lambda i: (0, 0))],
        out_specs=pl.BlockSpec((N_DEV, tr, _IN_SHARD), lambda i: (0, i, 0)),
        out_shape=jax.ShapeDtypeStruct((N_DEV, dm, _IN_SHARD), BF16),
    )(n_t, dproj)


def _mm_dw_up(n_t, dup2, *, name):
    dm, lp = n_t.shape
    pairs_per_half = D_FF // (2 * _UP_SHARD)

    def body(a_ref, b_ref, o_ref):
        r_t = jnp.dot(a_ref[...], b_ref[...], preferred_element_type=F32).astype(BF16).T
        o_ref[0] = r_t[:_UP_SHARD, :]
        o_ref[1] = r_t[_UP_SHARD:, :]

    return pl.pallas_call(
        body, name=name, grid=(N_DEV // 2,),
        in_specs=[pl.BlockSpec((dm, lp), lambda p: (0, 0)),
                  pl.BlockSpec((lp, 2 * _UP_SHARD), lambda p: (p // pairs_per_half, p % pairs_per_half))],
        out_specs=pl.BlockSpec((2, _UP_SHARD, dm), lambda p: (p, 0, 0)),
        out_shape=jax.ShapeDtypeStruct((N_DEV, _UP_SHARD, dm), BF16),
    )(n_t, dup2)


MESH = pl.DeviceIdType.MESH
ANY = pl.BlockSpec(memory_space=pl.ANY)


def _slot(px, py, pc):
    return 4 * px + 2 * py + pc


def _by_owner(ref, slot):
    return ref.at[slot]


def _row_block(rows, axis=0):
    def place(ref, slot):
        idx = (slice(None),) * axis + (pl.ds(slot * rows, rows),)
        return ref.at[idx]
    return place


def _all_gather(arrs, out_shapes, places, *, name):
    n = len(arrs)

    def body(*refs):
        ins, outs = refs[:n], refs[n:2 * n]
        send_sems, recv_sems, local_sems = refs[2 * n:]
        x, y, c = lax.axis_index("x"), lax.axis_index("y"), lax.axis_index("c")
        me, sibling = (x, y, c), (x, y, 1 - c)
        chips = [(1 - x, y), (x, 1 - y), (1 - x, 1 - y)]

        def copy(a, k, block, to, from_input=False):
            dst = places[a](outs[a], _slot(*block))
            return pltpu.make_async_remote_copy(
                src_ref=ins[a] if from_input else dst, dst_ref=dst,
                send_sem=send_sems.at[7 * a + k], recv_sem=recv_sems.at[7 * a + k],
                device_id=to, device_id_type=MESH)

        own, sent = [], []
        for a in range(n):
            mine = pltpu.make_async_copy(ins[a], places[a](outs[a], _slot(*me)), local_sems.at[a])
            mine.start()
            own.append(mine)
            first = [copy(a, 0, me, sibling, True)]
            first += [copy(a, 1 + j, me, (*chip, c), True) for j, chip in enumerate(chips)]
            for cp in first:
                cp.start()
            sent += first
        for a in range(n):
            for j, chip in enumerate(chips):
                copy(a, 1 + j, (*chip, c), me).wait_recv()
                passed = copy(a, 4 + j, (*chip, c), sibling)
                passed.start()
                sent.append(passed)
        for a in range(n):
            copy(a, 0, sibling, me).wait_recv()
            for j, chip in enumerate(chips):
                copy(a, 4 + j, (*chip, 1 - c), me).wait_recv()
        for cp in sent:
            cp.wait_send()
        for cp in own:
            cp.wait()

    return pl.pallas_call(
        body, name=name,
        in_specs=[ANY] * n, out_specs=[ANY] * n,
        out_shape=[jax.ShapeDtypeStruct(s, a.dtype) for s, a in zip(out_shapes, arrs)],
        scratch_shapes=[pltpu.SemaphoreType.DMA((7 * n,)), pltpu.SemaphoreType.DMA((7 * n,)),
                        pltpu.SemaphoreType.DMA((n,))],
    )(*arrs)


HBM = pl.BlockSpec(memory_space=pltpu.HBM)
SEM = pl.BlockSpec(memory_space=pltpu.SEMAPHORE)
EFFECT = pltpu.SideEffectType.DATAFLOW_SIDE_EFFECTING


def _relation_copies(src_refs, land_refs, copies, send_sems, recv_sems):
    x, y, c = lax.axis_index("x"), lax.axis_index("y"), lax.axis_index("c")
    flip = lambda v, bit: 1 - v if bit else v
    out = []
    for k in range(1, N_DEV):
        p = (flip(x, k & 4), flip(y, k & 2), flip(c, k & 1))
        for j, (si, take, li, put) in enumerate(copies):
            sem = (k - 1) * len(copies) + j
            dst = land_refs[li].at[k - 1] if put is None else put(land_refs[li], _slot(x, y, c))
            out.append(pltpu.make_async_remote_copy(
                src_ref=take(src_refs[si], _slot(*p)), dst_ref=dst,
                send_sem=send_sems.at[sem], recv_sem=recv_sems.at[sem], device_id=p, device_id_type=MESH))
    return out


def _own_copies(src_refs, land_refs, copies, sems):
    me = _slot(lax.axis_index("x"), lax.axis_index("y"), lax.axis_index("c"))
    placed = [(si, take, li, put) for si, take, li, put in copies if put is not None]
    return [pltpu.make_async_copy(take(src_refs[si], me), put(land_refs[li], me),
                                  sems.at[(N_DEV - 1) * len(copies) + j])
            for j, (si, take, li, put) in enumerate(placed)]


def _send_start(srcs, land_structs, copies, *, name, after=None):
    ns, nl = len(srcs), len(land_structs)
    n_sem = (N_DEV - 1) * len(copies) + sum(put is not None for _, _, _, put in copies)
    behind = [] if after is None else [after]

    def body(*refs):
        first_out = ns + nl + len(behind)
        send_sems, recv_sems, token = refs[first_out], refs[first_out + 1], refs[-1]
        for cp in _relation_copies(refs[:ns], refs[ns:ns + nl], copies, send_sems, recv_sems):
            cp.start()
        for cp in _own_copies(refs[:ns], refs[ns:ns + nl], copies, send_sems):
            cp.start()
        token[...] = jnp.zeros_like(token)

    in_hbm = lambda a: pltpu.with_memory_space_constraint(a, pltpu.HBM)
    outs = pl.pallas_call(
        body, name=name,
        out_shape=(pltpu.SemaphoreType.DMA((n_sem,)), pltpu.SemaphoreType.DMA((n_sem,)),
                   *[pltpu.HBM(s.shape, s.dtype) for s in srcs],
                   *[pltpu.HBM(s.shape, s.dtype) for s in land_structs],
                   jax.ShapeDtypeStruct((8, 128), F32)),
        in_specs=(HBM,) * (ns + nl) + (ANY,) * len(behind),
        out_specs=(SEM, SEM) + (HBM,) * (ns + nl) + (pl.BlockSpec(memory_space=pltpu.VMEM),),
        input_output_aliases={i: 2 + i for i in range(ns + nl)},
        compiler_params=pltpu.CompilerParams(has_side_effects=EFFECT),
    )(*[in_hbm(s) for s in srcs], *[in_hbm(lax.empty(s.shape, s.dtype)) for s in land_structs], *behind)
    return (outs[0], outs[1], outs[2:2 + ns], outs[2 + ns:2 + ns + nl], copies), outs[-1]


def _send_wait(handle, after, *, name):
    send_sems, recv_sems, srcs, lands, copies = handle
    ns, nl = len(srcs), len(lands)

    def body(*refs):
        for cp in _relation_copies(refs[:ns], refs[ns:ns + nl], copies, refs[ns + nl], refs[ns + nl + 1]):
            cp.wait_send()
            cp.wait_recv()
        for cp in _own_copies(refs[:ns], refs[ns:ns + nl], copies, refs[ns + nl]):
            cp.wait()

    outs = pl.pallas_call(
        body, name=name,
        out_shape=tuple(pltpu.HBM(a.shape, a.dtype) for a in (*srcs, *lands)),
        in_specs=(HBM,) * (ns + nl) + (SEM, SEM, ANY), out_specs=(HBM,) * (ns + nl),
        input_output_aliases={i: i for i in range(ns + nl)},
        compiler_params=pltpu.CompilerParams(has_side_effects=EFFECT),
    )(*srcs, *lands, send_sems, recv_sems, after)
    return outs[:ns], outs[ns:]


def _sum_slabs(stack, *, name, own=None):
    n, rows, w = stack.shape

    def body(*refs):
        s_ref, o_ref = refs[-2], refs[-1]
        acc = s_ref[0] if own is None else refs[0][...] + s_ref[0]
        for i in range(1, n):
            acc = acc + s_ref[i]
        o_ref[...] = acc

    return pl.pallas_call(body, name=name, out_shape=jax.ShapeDtypeStruct((rows, w), F32))(
        *([] if own is None else [own]), stack)


def _adam_math(w, g, m, v):
    mn = ADAM_B1 * m + (1.0 - ADAM_B1) * g
    vn = ADAM_B2 * v + (1.0 - ADAM_B2) * (g * g)
    m_hat = mn / (1.0 - ADAM_B1 ** ADAM_STEP)
    v_hat = vn / (1.0 - ADAM_B2 ** ADAM_STEP)
    return -ADAM_LR * (m_hat / (jnp.sqrt(v_hat) + ADAM_EPS) + ADAM_WD * w), mn, vn


def _adamw_many(ws, gs, ms, vs, *, name):
    n = len(ws)

    def body(*refs):
        for i in range(n):
            w_ref, g_ref, m_ref, v_ref = (refs[j * n + i] for j in range(4))
            d_ref, mo_ref, vo_ref = refs[4 * n + 3 * i:4 * n + 3 * i + 3]
            d_ref[...], mo_ref[...], vo_ref[...] = _adam_math(w_ref[...], g_ref[...], m_ref[...], v_ref[...])

    return pl.pallas_call(
        body, name=name, out_shape=[jax.ShapeDtypeStruct(w.shape, F32) for w in ws for _ in range(3)],
    )(*ws, *gs, *ms, *vs)


def _adamw_layers(w, owns, lands, m, v, tr, *, name):
    nl, rows, cols = w.shape
    steps = rows // tr
    assert rows % tr == 0

    def body(*refs):
        w_ref, m_ref, v_ref = refs[:3]
        own_refs, land_refs = refs[3:3 + nl], refs[3 + nl:3 + 2 * nl]
        g_ref, d_ref, mo_ref, vo_ref = refs[3 + 2 * nl:]
        for li in range(nl):
            @pl.when(pl.program_id(0) == li)
            def _(li=li):
                g = own_refs[li][...].astype(F32)
                for k in range(N_DEV - 1):
                    g = g + land_refs[li][k].astype(F32)
                g_ref[...] = g
                d_ref[...], mo_ref[...], vo_ref[...] = _adam_math(w_ref[...], g, m_ref[...], v_ref[...])

    def held(li):
        return lambda l, i: jnp.where(l == li, i, jnp.where(l < li, 0, steps - 1))

    blk = pl.BlockSpec((None, tr, cols), lambda l, i: (l, i, 0))
    own_specs = [pl.BlockSpec((tr, cols), lambda l, i, f=held(li): (f(l, i), 0)) for li in range(nl)]
    land_specs = [pl.BlockSpec((N_DEV - 1, tr, cols), lambda l, i, f=held(li): (0, f(l, i), 0)) for li in range(nl)]
    return pl.pallas_call(
        body, name=name, grid=(nl, steps),
        in_specs=[blk, blk, blk] + own_specs + land_specs, out_specs=[blk] * 4,
        out_shape=[jax.ShapeDtypeStruct(w.shape, F32)] * 4,
    )(w, m, v, *owns, *lands)


_WEIGHTS = (
    ("meta_tokens", (16, 1024), 1), ("mix_norm_even", (1, 1024), None), ("w_in", (1, 1024, 2568), 2),
    ("b_f", (1, 8), None), ("conv_w", (1, 31, 512), 2), ("conv_b", (1, 512), None), ("ln_g", (1, 512), None),
    ("ln_b", (1, 512), None), ("w_out", (1, 1024, 1024), 1), ("mix_norm_odd", (1, 1024), 1),
    ("pool_w", (1, 4, 256, 256), 2), ("pool_b", (1, 4, 256), 2), ("pool_scale", (1, 1024), 1),
    ("ffn_norm", (2, 1024), None), ("w_up", (2, 1024, 5632), 2), ("ffn_conv_w", (2, 3, 5632), 2),
    ("ffn_conv_b", (2, 5632), None), ("w_down", (2, 2816, 1024), 1), ("final_norm", (1024,), None),
)
_MATMUL_WEIGHTS = ("w_in", "w_out", "pool_w", "w_up", "w_down")
_ADAM_ROWS = dict(w_in=256, w_out=128, pool_w=128, w_up=352, w_down=352)


def _shard_shape(shape, axis):
    return shape[:axis] + (shape[axis] // N_DEV,) + shape[axis + 1:]


def _size(shape):
    n = 1
    for s in shape:
        n *= s
    return n


def _pack(parts, dtype, lead=(), align=16):
    flat = jnp.concatenate([p.reshape(lead + (-1,)).astype(dtype) for p in parts], axis=-1)
    n = flat.shape[-1]
    rows = _round_up(-(-n // FLAT_W), align)
    flat = jnp.pad(flat, [(0, 0)] * len(lead) + [(0, rows * FLAT_W - n)])
    return flat.reshape(lead + (rows, FLAT_W))


def _unpack(buf, shapes, lead=()):
    flat = buf.reshape(lead + (-1,))
    out, off = [], 0
    for shp in shapes:
        n = _size(shp)
        out.append(flat[..., off:off + n].reshape(lead + shp))
        off += n
    return out


def _gathered_to_full(stack, shape, axis):
    return jnp.moveaxis(stack, 0, axis).reshape(shape)


def _full_to_slabs(full, shape, axis):
    split = shape[:axis] + (N_DEV, shape[axis] // N_DEV) + shape[axis + 1:]
    return jnp.moveaxis(full.reshape(split), axis, 0)


def kernel(x, meta_tokens, mix_norm_even, w_in, b_f, conv_w, conv_b, ln_g, ln_b, w_out, mix_norm_odd, pool_w, pool_b, pool_scale, ffn_norm, w_up, ffn_conv_w, ffn_conv_b, w_down, final_norm, loss_target, m_meta_tokens, m_mix_norm_even, m_w_in, m_b_f, m_conv_w, m_conv_b, m_ln_g, m_ln_b, m_w_out, m_mix_norm_odd, m_pool_w, m_pool_b, m_pool_scale, m_ffn_norm, m_w_up, m_ffn_conv_w, m_ffn_conv_b, m_w_down, m_final_norm, v_meta_tokens, v_mix_norm_even, v_w_in, v_b_f, v_conv_w, v_conv_b, v_ln_g, v_ln_b, v_w_out, v_mix_norm_odd, v_pool_w, v_pool_b, v_pool_scale, v_ffn_norm, v_w_up, v_ffn_conv_w, v_ffn_conv_b, v_w_down, v_final_norm):
    names = [n for n, _, _ in _WEIGHTS]
    w_loc = dict(zip(names, (meta_tokens, mix_norm_even, w_in, b_f, conv_w, conv_b, ln_g, ln_b, w_out, mix_norm_odd,
                             pool_w, pool_b, pool_scale, ffn_norm, w_up, ffn_conv_w, ffn_conv_b, w_down, final_norm)))
    m_loc = dict(zip(names, (m_meta_tokens, m_mix_norm_even, m_w_in, m_b_f, m_conv_w, m_conv_b, m_ln_g, m_ln_b,
                             m_w_out, m_mix_norm_odd, m_pool_w, m_pool_b, m_pool_scale, m_ffn_norm, m_w_up,
                             m_ffn_conv_w, m_ffn_conv_b, m_w_down, m_final_norm)))
    v_loc = dict(zip(names, (v_meta_tokens, v_mix_norm_even, v_w_in, v_b_f, v_conv_w, v_conv_b, v_ln_g, v_ln_b,
                             v_w_out, v_mix_norm_odd, v_pool_w, v_pool_b, v_pool_scale, v_ffn_norm, v_w_up,
                             v_ffn_conv_w, v_ffn_conv_b, v_w_down, v_final_norm)))
    replicated = [(n, s) for n, s, a in _WEIGHTS if a is None]
    little = [(n, s, a) for n, s, a in _WEIGHTS if a is not None and n not in _MATMUL_WEIGHTS]
    little_shards = [_shard_shape(s, a) for _, s, a in little]
    out_rows, down_rows, pool_rows = D_MODEL // N_DEV, D_FF // N_DEV, POOL_GROUP // N_DEV
    n_groups = len(POOL_WINDOWS)

    little_pack = _pack([w_loc[n] for n, _, _ in little], F32)
    g_win, g_wout, g_poolw, g_little = _all_gather(
        [w_in[0].astype(BF16), w_out[0].astype(BF16), pool_w[0].astype(BF16), little_pack],
        [(N_DEV, D_MODEL, _IN_SHARD), (D_MODEL, D_MODEL), (n_groups, POOL_GROUP, POOL_GROUP),
         (N_DEV,) + little_pack.shape],
        [_by_owner, _row_block(out_rows), _row_block(pool_rows, axis=1), _by_owner],
        name="gather_weights")
    me = _slot(lax.axis_index("x"), lax.axis_index("y"), lax.axis_index("c"))
    up_t = lambda a: jnp.transpose(a, (0, 2, 1))
    w_loc["w_up"], m_loc["w_up"], v_loc["w_up"] = up_t(w_up), up_t(m_w_up), up_t(v_w_up)
    w_up_b, w_down_b = w_loc["w_up"].astype(BF16), w_down.astype(BF16)
    whole = lambda ref, slot: ref
    ffn_gathers, behind = {}, g_little
    for l in range(DEPTH):
        for part, shard, rows in (("up", w_up_b[l], _UP_SHARD), ("down", w_down_b[l], down_rows)):
            ffn_gathers[l, part], behind = _send_start(
                [shard], [jax.ShapeDtypeStruct((N_DEV * rows, D_MODEL), BF16)], [(0, whole, 0, _row_block(rows))],
                name=f"gather_{part}{l}_start", after=behind)

    def ffn_weight(l, part, after):
        return _send_wait(ffn_gathers[l, part], after, name=f"gather_{part}{l}_wait")[1][0]

    w_in_p = _assemble_w_in(g_win, name="assemble_w_in")
    full = {n: _gathered_to_full(st, s, a)
            for (n, s, a), st in zip(little, _unpack(g_little, little_shards, lead=(N_DEV,)))}
    f0 = _QKV + 2 * CONV_CH
    wt = dict(
        meta=full["meta_tokens"], g_even=mix_norm_even, w_in_p=w_in_p, wf_t=w_in_p[:, f0:f0 + FOX_HEADS].T,
        b_f=b_f.reshape(FOX_HEADS, 1), conv_w=full["conv_w"][0], conv_b=conv_b, ln_g=ln_g, ln_b=ln_b, w_out=g_wout,
        g_odd=full["mix_norm_odd"], pool_w=g_poolw, pool_b=full["pool_b"].reshape(1, D_MODEL),
        pool_scale=full["pool_scale"], ffn_norm=ffn_norm, ffn_weight=ffn_weight, ffn_started=behind,
        fcw3=full["ffn_conv_w"].reshape(DEPTH, FFN_CONV_WIDTH, 2, D_FF).transpose(0, 2, 1, 3),
        fcb3=ffn_conv_b.reshape(DEPTH, 2, 1, D_FF), g_final=final_norm.reshape(1, D_MODEL))

    loss_part, grad_x, g = _local_step(x[0], loss_target[0], wt)

    small = _pack([loss_part[:, 0:1]] + [g[n] for n, _ in replicated] + [g["meta_tokens"]], F32, align=8)
    send_small, token_small = _send_start([small], [jax.ShapeDtypeStruct((N_DEV,) + small.shape, F32)],
                                          [(0, whole, 0, _by_owner)], name="send_small")

    grads, delta, new_m, new_v = {}, {}, {}, {}
    send0, send1, send_rest, send_in = g["sends"]
    ffn_sent = [_send_wait(send, token_small, name=f"wait_ffn{l}") for l, send in enumerate((send0, send1))]
    own_up = [lax.dynamic_index_in_dim(srcs[0], me, 0, keepdims=False) for srcs, _ in ffn_sent]
    own_down = [lax.dynamic_slice_in_dim(srcs[1], me * down_rows, down_rows, 0) for srcs, _ in ffn_sent]
    for n, owns, idx in (("w_up", own_up, 0), ("w_down", own_down, 1)):
        grads[n], delta[n], new_m[n], new_v[n] = _adamw_layers(
            w_loc[n], owns, [lands[idx] for _, lands in ffn_sent], m_loc[n], v_loc[n], _ADAM_ROWS[n],
            name=f"adamw_{n}")
    for d in (grads, delta, new_m, new_v):
        d["w_up"] = up_t(d["w_up"])
    (d_out, d_pool, little_slabs), (land_out, land_pool, land_little) = _send_wait(
        send_rest, delta["w_down"], name="wait_rest")
    (d_in,), (land_in,) = _send_wait(send_in, land_out, name="wait_w_in")
    pool_2d = (n_groups * pool_rows, POOL_GROUP)
    own_pool = lax.dynamic_slice_in_dim(d_pool, me * pool_rows, pool_rows, 1)
    for n, own, land, shp in (
            ("w_in", lax.dynamic_index_in_dim(d_in, me, 0, keepdims=False), land_in, w_in.shape),
            ("w_out", lax.dynamic_slice_in_dim(d_out, me * out_rows, out_rows, 0), land_out, w_out.shape),
            ("pool_w", own_pool.reshape(pool_2d), land_pool.reshape((N_DEV - 1,) + pool_2d), (1,) + pool_2d)):
        outs = _adamw_layers(w_loc[n].reshape(shp), [own], [land], m_loc[n].reshape(shp), v_loc[n].reshape(shp),
                             _ADAM_ROWS[n], name=f"adamw_{n}")
        grads[n], delta[n], new_m[n], new_v[n] = (o.reshape(w_loc[n].shape) for o in outs)
    own_little = lax.dynamic_index_in_dim(little_slabs, me, 0, keepdims=False)
    g_little = _unpack(_sum_slabs(land_little, own=own_little, name="sum_little"),
                       [_shard_shape(s, a) for _, s, a in _LITTLE])
    grads.update({n: gl for (n, _, _), gl in zip(_LITTLE, g_little)})
    _, (everyone,) = _send_wait(send_small, delta["w_in"], name="wait_small")
    summed = _unpack(_sum_slabs(everyone, name="sum_small"),
                     [(1, 1)] + [s for _, s in replicated] + [(N_META, D_MODEL)])
    loss = summed[0].reshape(())
    grads.update({n: gr for (n, _), gr in zip(replicated, summed[1:-1])})
    grads["meta_tokens"] = lax.dynamic_slice_in_dim(summed[-1], me * out_rows, out_rows, 1)
    at_least_2d = lambda a: a.reshape((1,) * (2 - a.ndim) + a.shape)
    rest = [n for n in names if n not in _MATMUL_WEIGHTS]
    outs = _adamw_many([at_least_2d(w_loc[n]) for n in rest], [at_least_2d(grads[n]) for n in rest],
                       [at_least_2d(m_loc[n]) for n in rest], [at_least_2d(v_loc[n]) for n in rest],
                       name="adamw_rest")
    for i, n in enumerate(rest):
        delta[n], new_m[n], new_v[n] = (o.reshape(w_loc[n].shape) for o in outs[3 * i:3 * i + 3])
    return (loss, grad_x[None], *[grads[n] for n in names], *[delta[n] for n in names],
            *[new_m[n] for n in names], *[new_v[n] for n in names])
```

```python
import functools

import jax
import jax.numpy as jnp
from jax import lax
from jax.experimental import pallas as pl
from jax.experimental.pallas import tpu as pltpu

F32 = jnp.float32
BF16 = jnp.bfloat16

N_DEV = 8
DEPTH = 2
D_MODEL = 1024
N_META = 16
FOX_HEADS = 8
FOX_HEAD_DIM = 64
FOX_WIDTH = 512
CONV_CH = 512
CONV_WIDTH = 31
POOL_WINDOWS = (2, 4, 8, 16)
POOL_GROUP = 256
D_FF = 2816
FFN_CONV_WIDTH = 3
RMS_EPS = 1e-6
LN_EPS = 1e-5
ADAM_LR = 0.001
ADAM_B1 = 0.9
ADAM_B2 = 0.999
ADAM_EPS = 1e-08
ADAM_WD = 0.01
ADAM_STEP = 10

CHUNK = 128
HALO = 32
NEG_BIG = -1e30
FLAT_W = 1024


def _round_up(n, m):
    return (n + m - 1) // m * m


def _sigmoid(x):
    return 1.0 / (1.0 + jnp.exp(-x))


def _fold8(p):
    acc = p[0:8, :]
    for r in range(1, p.shape[0] // 8):
        acc = acc + p[8 * r:8 * r + 8, :]
    return acc


def _mm(a, b, *, name, tb=False, tm=None, tn=None, tk=None, out_dtype=F32, res=None,
        a_map=None, b_map=None, o_map=None, out_shape=None, dims=None, after=None, first_tile_t=False):
    if dims is None:
        m, k = a.shape
        n = b.shape[-2] if tb else b.shape[-1]
    else:
        m, n, k = dims
    tm, tn, tk = tm or m, tn or n, tk or k
    assert m % tm == 0 and n % tn == 0 and k % tk == 0, (name, m, n, k, tm, tn, tk)
    nk = k // tk
    a_map = a_map or (lambda i, j, kk: (i, kk))
    b_map = b_map or ((lambda i, j, kk: (j, kk)) if tb else (lambda i, j, kk: (kk, j)))
    o_map = o_map or (lambda i, j, kk: (i, j))
    out_shape = out_shape or (m, n)
    contract = (((1,), (1,)), ((), ())) if tb else (((1,), (0,)), ((), ()))
    has_res = res is not None

    def body(*refs):
        a_ref, b_ref = refs[0], refs[1]
        res_ref = refs[2] if has_res else None
        o_ref = refs[2 + has_res + (after is not None)]
        p = lax.dot_general(a_ref[...], b_ref[...], contract, preferred_element_type=F32)
        if nk == 1:
            if has_res:
                p = p + res_ref[...]
            o_ref[...] = p.astype(o_ref.dtype)
            if first_tile_t:
                @pl.when(pl.program_id(1) == 0)
                def _():
                    refs[3 + has_res + (after is not None)][...] = p.astype(BF16).T
        else:
            acc_ref = refs[-1]
            kk = pl.program_id(2)

            @pl.when(kk == 0)
            def _():
                acc_ref[...] = p

            @pl.when(kk > 0)
            def _():
                acc_ref[...] += p

            @pl.when(kk == nk - 1)
            def _():
                r = acc_ref[...]
                if has_res:
                    r = r + res_ref[...]
                o_ref[...] = r.astype(o_ref.dtype)

    in_specs = [pl.BlockSpec((tm, tk), a_map), pl.BlockSpec((tn, tk) if tb else (tk, tn), b_map)]
    operands = [a, b]
    if has_res:
        in_specs.append(pl.BlockSpec((tm, tn), o_map))
        operands.append(res)
    if after is not None:
        in_specs.append(pl.BlockSpec(memory_space=pl.ANY))
        operands.append(after)
    out_specs, out_struct = pl.BlockSpec((tm, tn), o_map), jax.ShapeDtypeStruct(out_shape, out_dtype)
    if first_tile_t:
        assert tm == m and nk == 1
        out_specs = [out_specs, pl.BlockSpec((tn, m), lambda i, j, kk: (0, 0))]
        out_struct = [out_struct, jax.ShapeDtypeStruct((tn, m), BF16)]
    return pl.pallas_call(
        body, name=name, grid=(m // tm, n // tn, nk),
        in_specs=in_specs, out_specs=out_specs, out_shape=out_struct,
        scratch_shapes=[pltpu.VMEM((tm, tn), F32)] if nk > 1 else [],
    )(*operands)


def _transpose(x, *, name, out_dtype):
    r, c = x.shape
    assert r % CHUNK == 0

    def body(x_ref, o_ref):
        o_ref[...] = x_ref[...].astype(o_ref.dtype).T

    return pl.pallas_call(
        body, name=name, grid=(r // CHUNK,),
        in_specs=[pl.BlockSpec((CHUNK, c), lambda i: (i, 0))],
        out_specs=pl.BlockSpec((c, CHUNK), lambda i: (0, i)),
        out_shape=jax.ShapeDtypeStruct((c, r), out_dtype),
    )(x)


def _rms_fwd(x, g, *, name, out_dtype, after=None, transposed=False):
    lp, dm = x.shape
    tr = CHUNK if transposed else lp // 4
    behind = [] if after is None else [after]

    def body(x_ref, g_ref, *rest):
        xv = x_ref[...]
        r = lax.rsqrt(jnp.mean(xv * xv, axis=-1, keepdims=True) + RMS_EPS)
        y = xv * r * g_ref[...]
        if transposed:
            yb = y.astype(out_dtype)
            rest[-2][...] = yb
            rest[-1][...] = yb.T
        else:
            rest[-1][...] = y.astype(out_dtype)

    row = pl.BlockSpec((tr, dm), lambda i: (i, 0))
    out_specs, out_shape = row, jax.ShapeDtypeStruct((lp, dm), out_dtype)
    if transposed:
        out_specs = [row, pl.BlockSpec((dm, tr), lambda i: (0, i))]
        out_shape = [out_shape, jax.ShapeDtypeStruct((dm, lp), out_dtype)]
    return pl.pallas_call(
        body, name=name, grid=(lp // tr,),
        in_specs=[row, pl.BlockSpec((1, dm), lambda i: (0, 0))] + [pl.BlockSpec(memory_space=pl.ANY)] * len(behind),
        out_specs=out_specs, out_shape=out_shape,
    )(x, g, *behind)


def _rms_bwd_rows(x_ref, g_ref, dnv, dres_ref, dh_ref, dhb_ref, dg_ref, first):
    xv = x_ref[...]
    r = lax.rsqrt(jnp.mean(xv * xv, axis=-1, keepdims=True) + RMS_EPS)
    xhat = xv * r

    @pl.when(first)
    def _():
        dg_ref[...] = jnp.zeros_like(dg_ref)

    dg_ref[...] += jnp.sum(dnv * xhat, axis=0, keepdims=True)
    dxhat = dnv * g_ref[...]
    dx = r * (dxhat - xhat * jnp.mean(dxhat * xhat, axis=-1, keepdims=True))
    dh = dres_ref[...] + dx
    dh_ref[...] = dh
    dhb_ref[...] = dh.astype(BF16)


def _mm_rms_bwd(a, b, x, g, dres, *, name, tk, tb=False, a_map=None, after=None):
    lp, dm = x.shape
    tm = lp // 4
    nk = (b.shape[1] if tb else b.shape[0]) // tk
    a_map = a_map or (lambda i, kk: (i, kk))
    contract = (((1,), (1,)), ((), ())) if tb else (((1,), (0,)), ((), ()))
    behind = [] if after is None else [after]

    def body(a_ref, b_ref, x_ref, g_ref, dres_ref, *rest):
        dh_ref, dhb_ref, dg_ref, acc_ref = rest[len(behind):]
        i, kk = pl.program_id(0), pl.program_id(1)
        p = lax.dot_general(a_ref[...], b_ref[...], contract, preferred_element_type=F32)

        @pl.when(kk == 0)
        def _():
            acc_ref[...] = p

        @pl.when(kk > 0)
        def _():
            acc_ref[...] += p

        @pl.when(kk == nk - 1)
        def _():
            _rms_bwd_rows(x_ref, g_ref, acc_ref[...], dres_ref, dh_ref, dhb_ref, dg_ref, i == 0)

    row = pl.BlockSpec((tm, dm), lambda i, kk: (i, 0))
    vec = pl.BlockSpec((1, dm), lambda i, kk: (0, 0))
    b_spec = pl.BlockSpec((dm, tk), lambda i, kk: (0, kk)) if tb else pl.BlockSpec((tk, dm), lambda i, kk: (kk, 0))
    return pl.pallas_call(
        body, name=name, grid=(lp // tm, nk),
        in_specs=[pl.BlockSpec((tm, tk), a_map), b_spec, row, vec, row] + [pl.BlockSpec(memory_space=pl.ANY)] * len(behind),
        out_specs=[row, row, vec],
        out_shape=[jax.ShapeDtypeStruct((lp, dm), F32), jax.ShapeDtypeStruct((lp, dm), BF16),
                   jax.ShapeDtypeStruct((1, dm), F32)],
        scratch_shapes=[pltpu.VMEM((tm, dm), F32)],
    )(a, b, x, g, dres, *behind)


def _rms_bwd(x, g, dn, dres, *, name):
    lp, dm = x.shape
    tr = lp // 4

    def body(x_ref, g_ref, dn_ref, dres_ref, dh_ref, dhb_ref, dg_ref):
        _rms_bwd_rows(x_ref, g_ref, dn_ref[...], dres_ref, dh_ref, dhb_ref, dg_ref, pl.program_id(0) == 0)

    row = pl.BlockSpec((tr, dm), lambda i: (i, 0))
    vec = pl.BlockSpec((1, dm), lambda i: (0, 0))
    return pl.pallas_call(
        body, name=name, grid=(lp // tr,),
        in_specs=[row, vec, row, row], out_specs=[row, row, vec],
        out_shape=[jax.ShapeDtypeStruct((lp, dm), F32), jax.ShapeDtypeStruct((lp, dm), BF16),
                   jax.ShapeDtypeStruct((1, dm), F32)],
    )(x, g, dn, dres)


def _loss_head(h, g, tgt, n_real, *, name):
    lp, dm = h.shape
    tr = lp // 4

    def body(x_ref, g_ref, t_ref, loss_ref, dh_ref, dhb_ref, dg_ref):
        i = pl.program_id(0)
        xv = x_ref[...]
        r = lax.rsqrt(jnp.mean(xv * xv, axis=-1, keepdims=True) + RMS_EPS)
        xhat = xv * r
        gv = g_ref[...]
        y = xhat * gv
        t = i * tr + lax.broadcasted_iota(jnp.int32, (tr, 1), 0)
        valid = (t >= N_META) & (t < n_real)
        diff = jnp.where(valid, y - t_ref[...], 0.0)

        @pl.when(i == 0)
        def _():
            loss_ref[...] = jnp.zeros_like(loss_ref)
            dg_ref[...] = jnp.zeros_like(dg_ref)

        row_sq = jnp.sum(diff * diff, axis=-1, keepdims=True) * (1.0 / dm)
        part = 0.5 * jnp.sum(row_sq, axis=0, keepdims=True)
        loss_ref[...] += jnp.broadcast_to(part, loss_ref.shape)
        dy = diff * (1.0 / dm)
        dg_ref[...] += jnp.sum(dy * xhat, axis=0, keepdims=True)
        dxhat = dy * gv
        dx = r * (dxhat - xhat * jnp.mean(dxhat * xhat, axis=-1, keepdims=True))
        dh_ref[...] = dx
        dhb_ref[...] = dx.astype(BF16)

    row = pl.BlockSpec((tr, dm), lambda i: (i, 0))
    vec = pl.BlockSpec((1, dm), lambda i: (0, 0))
    return pl.pallas_call(
        body, name=name, grid=(lp // tr,),
        in_specs=[row, vec, row],
        out_specs=[pl.BlockSpec((1, 128), lambda i: (0, 0)), row, row, vec],
        out_shape=[jax.ShapeDtypeStruct((1, 128), F32), jax.ShapeDtypeStruct((lp, dm), F32),
                   jax.ShapeDtypeStruct((lp, dm), BF16), jax.ShapeDtypeStruct((1, dm), F32)],
    )(h, g, tgt)


def _tri(upper):
    r = lax.broadcasted_iota(jnp.int32, (CHUNK, CHUNK), 0)
    c = lax.broadcasted_iota(jnp.int32, (CHUNK, CHUNK), 1)
    return jnp.where(r <= c if upper else r >= c, 1.0, 0.0).astype(F32)


def _fox_prep(f_t, b_f, *, name):
    nh, lp = f_t.shape
    nch = lp // CHUNK

    def body(f_ref, b_ref, c_ref):
        tri = _tri(True)
        carry = jnp.zeros((nh, 1), F32)
        for blk in range(nch):
            cols = slice(blk * CHUNK, (blk + 1) * CHUNK)
            z = f_ref[:, cols] + b_ref[...]
            logf = jnp.minimum(z, 0.0) - jnp.log(1.0 + jnp.exp(-jnp.abs(z)))
            cb = jnp.dot(logf, tri, preferred_element_type=F32, precision=lax.Precision.HIGHEST)
            c_ref[:, cols] = cb + carry
            carry = carry + jnp.sum(logf, axis=1, keepdims=True)

    return pl.pallas_call(
        body, name=name, out_shape=jax.ShapeDtypeStruct((nh, lp), F32),
    )(f_t, b_f)


def _fox_bwd(dc, f_t, b_f, *, name):
    nh, lp = f_t.shape
    nch = lp // CHUNK

    def body(dc_ref, f_ref, b_ref, df_ref, db_ref):
        tri = _tri(False)
        carry = jnp.zeros((nh, 1), F32)
        db = jnp.zeros((nh, 1), F32)
        df_ref[...] = jnp.zeros_like(df_ref)
        for blk in reversed(range(nch)):
            cols = slice(blk * CHUNK, (blk + 1) * CHUNK)
            dcb = dc_ref[:, cols]
            dlogf = jnp.dot(dcb, tri, preferred_element_type=F32, precision=lax.Precision.HIGHEST) + carry
            carry = carry + jnp.sum(dcb, axis=1, keepdims=True)
            z = f_ref[:, cols] + b_ref[...]
            dz = dlogf * _sigmoid(-z)
            df_ref[0:nh, cols] = dz
            db = db + jnp.sum(dz, axis=1, keepdims=True)
        db_ref[...] = db

    return pl.pallas_call(
        body, name=name,
        out_shape=[jax.ShapeDtypeStruct((128, lp), F32), jax.ShapeDtypeStruct((nh, 1), F32)],
    )(dc, f_t, b_f)


ATTN_BLOCKS = 4


def _attn_blocks(lp):
    tq = lp // ATTN_BLOCKS
    return tq, [(i * tq, min(lp, _round_up((i + 1) * tq, CHUNK))) for i in range(ATTN_BLOCKS)]


ATTN_SCALE = FOX_HEAD_DIM ** -0.5


def _mask_heads(k_ref, v_ref, kv):
    lane = lax.broadcasted_iota(jnp.int32, (1, 128), 1)
    zero = jnp.zeros((), BF16)
    for hd, sel in enumerate((lane < 64, lane >= 64)):
        kv[hd] = jnp.where(sel, k_ref[...], zero)
        kv[2 + hd] = jnp.where(sel, v_ref[...], zero)


def _attn_probs(q2s, k_h, c_row, row0, n):
    tq = q2s.shape[0]
    lo = row0 // CHUNK * CHUNK
    logits = []
    for c0, c1 in ([(0, lo)] if lo else []) + [(lo, n)]:
        s = lax.dot_general(q2s, k_h[c0:c1, :], (((1,), (1,)), ((), ())), preferred_element_type=F32) - c_row[:, c0:c1]
        if c1 > row0:
            t = row0 + lax.broadcasted_iota(jnp.int32, (tq, c1 - c0), 0)
            sidx = c0 + lax.broadcasted_iota(jnp.int32, (tq, c1 - c0), 1)
            s = jnp.where(sidx <= t, s, NEG_BIG)
        logits.append((s, c0, c1))
    m = functools.reduce(jnp.maximum, [jnp.max(s, axis=1, keepdims=True) for s, _, _ in logits])
    ps = [(jnp.exp(s - m), c0, c1) for s, c0, c1 in logits]
    inv = 1.0 / sum(jnp.sum(p, axis=1, keepdims=True) for p, _, _ in ps)
    return [(p * inv, c0, c1) for p, c0, c1 in ps]


def _attn_fwd(qkv, c3, *, name):
    lp = qkv.shape[0]
    tq, blocks = _attn_blocks(lp)

    def body(q_ref, k_ref, v_ref, c_ref, o_ref, kv):
        _mask_heads(k_ref, v_ref, kv)
        for i, (row0, n) in enumerate(blocks):
            q2s = q_ref[row0:row0 + tq, :] * ATTN_SCALE
            acc = jnp.zeros((tq, 128), F32)
            for hd in range(2):
                for p, c0, c1 in _attn_probs(q2s, kv.at[hd], c_ref[hd:hd + 1, 0:n], row0, n):
                    acc = acc + jnp.dot(p.astype(BF16), kv[2 + hd, c0:c1, :], preferred_element_type=F32)
            o_ref[row0:row0 + tq, :] = acc.astype(BF16)

    blk = lambda off: pl.BlockSpec((lp, 128), lambda p: (0, off + p))
    return pl.pallas_call(
        body, name=name, grid=(4,),
        in_specs=[blk(0), blk(4), blk(8), pl.BlockSpec((None, 2, lp), lambda p: (p, 0, 0))],
        out_specs=pl.BlockSpec((lp, 128), lambda p: (0, p)),
        out_shape=jax.ShapeDtypeStruct((lp, FOX_WIDTH), BF16),
        scratch_shapes=[pltpu.VMEM((4, lp, 128), BF16)],
    )(qkv, qkv, qkv, c3)


def _attn_bwd(qkv, q_t, dcat, do_t, c3, *, name, after):
    lp = qkv.shape[0]
    tq, blocks = _attn_blocks(lp)
    scale = FOX_HEAD_DIM ** -0.5

    def body(q_ref, k_ref, v_ref, qt_ref, do_ref, dot_ref, c_ref, _, dq_ref, dk_ref, dv_ref, dc_ref,
             dkt_acc, dvt_acc, kv):
        sub = lax.broadcasted_iota(jnp.int32, (128, 1), 0)
        zero = jnp.zeros((), BF16)
        _mask_heads(k_ref, v_ref, kv)
        dkt_acc[...] = jnp.zeros_like(dkt_acc)
        dvt_acc[...] = jnp.zeros_like(dvt_acc)
        dc_ref[...] = jnp.zeros_like(dc_ref)
        for i, (row0, n) in enumerate(blocks):
            rows = slice(row0, row0 + tq)
            q2s = q_ref[rows, :] * ATTN_SCALE
            do2 = do_ref[rows, :].astype(BF16)
            dq_acc = jnp.zeros((tq, 128), F32)
            for hd in range(2):
                sel_t = (sub < 64) if hd == 0 else (sub >= 64)
                qt_h = jnp.where(sel_t, qt_ref[:, rows], zero)
                dot_h = jnp.where(sel_t, dot_ref[:, rows], zero)
                segs = [(p, lax.dot_general(do2, kv[2 + hd, c0:c1, :], (((1,), (1,)), ((), ())),
                                            preferred_element_type=F32), c0, c1)
                        for p, c0, c1 in _attn_probs(q2s, kv.at[hd], c_ref[hd:hd + 1, 0:n], row0, n)]
                delta = sum(jnp.sum(p * dp, axis=1, keepdims=True) for p, dp, _, _ in segs)
                for p, dp, c0, c1 in segs:
                    ds = p * (dp - delta)
                    dsb = ds.astype(BF16)
                    dq_acc = dq_acc + jnp.dot(dsb, kv[hd, c0:c1, :], preferred_element_type=F32)
                    dkt_acc[:, c0:c1] += jnp.dot(qt_h, dsb, preferred_element_type=F32)
                    dvt_acc[:, c0:c1] += jnp.dot(dot_h, p.astype(BF16), preferred_element_type=F32)
                    dc_ref[hd:hd + 1, c0:c1] -= jnp.sum(ds, axis=0, keepdims=True)
            dq_ref[rows, :] = (dq_acc * scale).astype(BF16)
        dk_ref[...] = (dkt_acc[...] * scale).astype(BF16).T
        dv_ref[...] = dvt_acc[...].astype(BF16).T

    blk = lambda off: pl.BlockSpec((lp, 128), lambda p: (0, off + p))
    blk_t = pl.BlockSpec((128, lp), lambda p: (p, 0))
    c_spec = pl.BlockSpec((None, 2, lp), lambda p: (p, 0, 0))
    return pl.pallas_call(
        body, name=name, grid=(4,),
        in_specs=[blk(0), blk(4), blk(8), blk_t, blk(0), blk_t, c_spec, pl.BlockSpec(memory_space=pl.ANY)],
        out_specs=[blk(0), blk(0), blk(0), c_spec],
        out_shape=[jax.ShapeDtypeStruct((lp, FOX_WIDTH), BF16)] * 3 + [jax.ShapeDtypeStruct((4, 2, lp), F32)],
        scratch_shapes=[pltpu.VMEM((128, lp), F32), pltpu.VMEM((128, lp), F32), pltpu.VMEM((4, lp, 128), BF16)],
    )(qkv, qkv, qkv, q_t, dcat, do_t, c3, after)


def _ln_stats(x):
    mu = jnp.mean(x, axis=-1, keepdims=True)
    xc = x - mu
    var = jnp.mean(xc * xc, axis=-1, keepdims=True)
    rstd = lax.rsqrt(var + LN_EPS)
    return xc * rstd, rstd


def _conv_fwd(agf, conv_w, conv_b, ln_g, ln_b, *, name):
    lp = agf.shape[0]
    nch = lp // CHUNK
    c = CONV_CH

    def body(a_ref, g_ref, w_ref, b_ref, lg_ref, lb_ref, u0_ref, u1_ref, u3_ref, u0s):
        u0s[0:HALO, :] = jnp.zeros((HALO, c), F32)

        def glu(ci, _):
            rows = pl.ds(pl.multiple_of(ci * CHUNK, CHUNK), CHUNK)
            u0 = a_ref[rows, :] * _sigmoid(g_ref[rows, :])
            u0_ref[rows, :] = u0
            u0s[pl.ds(pl.multiple_of(ci * CHUNK + HALO, 8), CHUNK), :] = u0
            return 0

        lax.fori_loop(0, nch, glu, 0)

        def conv(ci, _):
            r0 = pl.multiple_of(ci * CHUNK, CHUNK)
            rows = pl.ds(r0, CHUNK)
            for lg in range(c // 128):
                lanes = slice(lg * 128, (lg + 1) * 128)
                win = u0s[pl.ds(r0, CHUNK + HALO), lanes]
                acc = jnp.broadcast_to(b_ref[:, lanes], (CHUNK, 128))
                for k in range(CONV_WIDTH):
                    s = CONV_WIDTH - 1 - k
                    sh = win if s == 0 else pltpu.roll(win, s, 0)
                    acc = acc + w_ref[k:k + 1, lanes] * sh[HALO:HALO + CHUNK, :]
                u1_ref[rows, lanes] = acc
            xhat, _ = _ln_stats(u1_ref[rows, :])
            y = xhat * lg_ref[...] + lb_ref[...]
            u3_ref[rows, :] = (y * _sigmoid(y)).astype(BF16)
            return 0

        lax.fori_loop(0, nch, conv, 0)

    full = lambda shape: pl.BlockSpec(shape, lambda i: (0, 0))
    return pl.pallas_call(
        body, name=name, grid=(1,),
        in_specs=[pl.BlockSpec((lp, c), lambda i: (0, 0)), pl.BlockSpec((lp, c), lambda i: (0, 1)),
                  full((CONV_WIDTH, c)), full((1, c)), full((1, c)), full((1, c))],
        out_specs=[full((lp, c)), full((lp, c)), full((lp, c))],
        out_shape=[jax.ShapeDtypeStruct((lp, c), F32), jax.ShapeDtypeStruct((lp, c), F32),
                   jax.ShapeDtypeStruct((lp, c), BF16)],
        scratch_shapes=[pltpu.VMEM((lp + HALO, c), F32)],
    )(agf, agf, conv_w, conv_b, ln_g, ln_b)


def _conv_bwd(dcat, u0, u1, agf, conv_w, ln_g, ln_b, *, name):
    lp = agf.shape[0]
    nch = lp // CHUNK
    c = CONV_CH
    wlen = CHUNK + HALO

    def body(du3_ref, u0_ref, u1_ref, a_ref, g_ref, w_ref, lg_ref, lb_ref,
             dag_ref, dw_ref, db_ref, dlg_ref, dlb_ref, du1s, dwacc, vacc):
        du1s[lp:lp + HALO, :] = jnp.zeros((HALO, c), F32)
        dwacc[...] = jnp.zeros_like(dwacc)
        vacc[...] = jnp.zeros_like(vacc)

        def ln_bwd(ci, _):
            r0 = pl.multiple_of(ci * CHUNK, CHUNK)
            rows = pl.ds(r0, CHUNK)
            xhat, rstd = _ln_stats(u1_ref[rows, :])
            y = xhat * lg_ref[...] + lb_ref[...]
            sg = _sigmoid(y)
            du2 = du3_ref[rows, :] * (sg * (1.0 + y * (1.0 - sg)))
            vacc[0:8, :] += _fold8(du2 * xhat)
            vacc[8:16, :] += _fold8(du2)
            dxhat = du2 * lg_ref[...]
            du1 = rstd * (dxhat - jnp.mean(dxhat, axis=-1, keepdims=True)
                          - xhat * jnp.mean(dxhat * xhat, axis=-1, keepdims=True))
            vacc[16:24, :] += _fold8(du1)
            du1s[rows, :] = du1
            return 0

        lax.fori_loop(0, nch, ln_bwd, 0)

        def conv_bwd(ci, _):
            r0 = pl.multiple_of(ci * CHUNK, CHUNK)
            rows = pl.ds(r0, CHUNK)
            for lg in range(c // 128):
                lanes = slice(lg * 128, (lg + 1) * 128)
                dwin = du1s[pl.ds(r0, wlen), lanes]
                u0 = u0_ref[rows, lanes]
                acc = jnp.zeros((CHUNK, 128), F32)
                for k in range(CONV_WIDTH):
                    s = CONV_WIDTH - 1 - k
                    d_s = (dwin if s == 0 else pltpu.roll(dwin, wlen - s, 0))[0:CHUNK, :]
                    acc = acc + w_ref[k:k + 1, lanes] * d_s
                    dwacc[8 * k:8 * k + 8, lanes] += _fold8(d_s * u0)
                sg = _sigmoid(g_ref[rows, lanes])
                a = a_ref[rows, lanes]
                dag_ref[rows, lanes] = (acc * sg).astype(BF16)
                dag_ref[rows, slice(c + lg * 128, c + (lg + 1) * 128)] = (acc * a * sg * (1.0 - sg)).astype(BF16)
            return 0

        lax.fori_loop(0, nch, conv_bwd, 0)
        for k in range(CONV_WIDTH):
            dw_ref[k:k + 1, :] = jnp.sum(dwacc[8 * k:8 * k + 8, :], axis=0, keepdims=True)
        dlg_ref[...] = jnp.sum(vacc[0:8, :], axis=0, keepdims=True)
        dlb_ref[...] = jnp.sum(vacc[8:16, :], axis=0, keepdims=True)
        db_ref[...] = jnp.sum(vacc[16:24, :], axis=0, keepdims=True)

    full = lambda shape: pl.BlockSpec(shape, lambda i: (0, 0))
    vec = jax.ShapeDtypeStruct((1, c), F32)
    return pl.pallas_call(
        body, name=name, grid=(1,),
        in_specs=[pl.BlockSpec((lp, c), lambda i: (0, 1)), full((lp, c)), full((lp, c)),
                  pl.BlockSpec((lp, c), lambda i: (0, 0)), pl.BlockSpec((lp, c), lambda i: (0, 1)),
                  full((CONV_WIDTH, c)), full((1, c)), full((1, c))],
        out_specs=[full((lp, 2 * c)), full((CONV_WIDTH, c)), full((1, c)), full((1, c)), full((1, c))],
        out_shape=[jax.ShapeDtypeStruct((lp, 2 * c), BF16), jax.ShapeDtypeStruct((CONV_WIDTH, c), F32), vec, vec, vec],
        scratch_shapes=[pltpu.VMEM((lp + HALO, c), F32), pltpu.VMEM((8 * CONV_WIDTH, c), F32),
                        pltpu.VMEM((24, c), F32)],
    )(dcat, u0, u1, agf, agf, conv_w, ln_g, ln_b)


FFN_TILE = 256
FFN_PAD = 8


def _ffn_conv(xs, w_ref, b_ref, half, r0):
    win = xs[half, pl.ds(r0, CHUNK + FFN_PAD), :]
    acc = jnp.broadcast_to(b_ref[half], (CHUNK, FFN_TILE))
    for k in range(FFN_CONV_WIDTH):
        s = FFN_CONV_WIDTH - 1 - k
        sh = win if s == 0 else pltpu.roll(win, s, 0)
        acc = acc + w_ref[half, k:k + 1, :] * sh[FFN_PAD:FFN_PAD + CHUNK, :]
    return acc


def _ffn_up_act(n, w_up_t, w3, b3, *, name):
    lp = n.shape[0]
    nch, nt = lp // CHUNK, D_FF // FFN_TILE
    nt_dims = (((1,), (1,)), ((), ()))

    parts = 4

    def project(n_ref, wg_ref, wv_ref, u, part=None):
        rows = slice(0, lp) if part is None else slice(part * lp // parts, (part + 1) * lp // parts)
        for half, w_ref in ((0, wg_ref), (1, wv_ref)):
            u[half, FFN_PAD + rows.start:FFN_PAD + rows.stop, :] = lax.dot_general(
                n_ref[rows, :], w_ref[...], nt_dims, preferred_element_type=F32)

    def activate(u, w_ref, b_ref, act_ref, act_t_ref, gv_ref, up_ref, part):
        for ci in range(part * nch // parts, (part + 1) * nch // parts):
            r0 = ci * CHUNK
            rows = slice(r0, r0 + CHUNK)
            gate = _ffn_conv(u, w_ref, b_ref, 0, r0)
            val = _ffn_conv(u, w_ref, b_ref, 1, r0)
            gv_ref[0, rows, :] = gate.astype(BF16)
            gv_ref[1, rows, :] = val.astype(BF16)
            for half in range(2):
                up_ref[half, rows, :] = u[half, FFN_PAD + r0:FFN_PAD + r0 + CHUNK, :].astype(BF16)
            act = (gate * _sigmoid(gate) * val).astype(BF16)
            act_ref[rows, :] = act
            act_t_ref[:, rows] = act.T

    def body(n_ref, wg_ref, wv_ref, wg_next, wv_next, w_ref, b_ref, act_ref, act_t_ref, gv_ref, up_ref, u0, u1):
        j = pl.program_id(0)

        @pl.when(j == 0)
        def _():
            for u in (u0, u1):
                u[:, 0:FFN_PAD, :] = jnp.zeros((2, FFN_PAD, FFN_TILE), F32)
            project(n_ref, wg_ref, wv_ref, u0)

        for parity, (mine, other) in enumerate(((u0, u1), (u1, u0))):
            @pl.when(j % 2 == parity)
            def _(mine=mine, other=other):
                for part in range(parts):
                    project(n_ref, wg_next, wv_next, other, part)
                    activate(mine, w_ref, b_ref, act_ref, act_t_ref, gv_ref, up_ref, part)

    halves = pl.BlockSpec((2, lp, FFN_TILE), lambda j: (0, 0, j))
    rows_of = lambda half, ahead: pl.BlockSpec(
        (FFN_TILE, D_MODEL), lambda j: (half * nt + jnp.minimum(j + ahead, nt - 1), 0))
    return pl.pallas_call(
        body, name=name, grid=(nt,),
        in_specs=[pl.BlockSpec((lp, D_MODEL), lambda j: (0, 0)), rows_of(0, 0), rows_of(1, 0), rows_of(0, 1),
                  rows_of(1, 1), pl.BlockSpec((2, FFN_CONV_WIDTH, FFN_TILE), lambda j: (0, 0, j)),
                  pl.BlockSpec((2, 1, FFN_TILE), lambda j: (0, 0, j))],
        out_specs=[pl.BlockSpec((lp, FFN_TILE), lambda j: (0, j)), pl.BlockSpec((FFN_TILE, lp), lambda j: (j, 0)),
                   halves, halves],
        out_shape=[jax.ShapeDtypeStruct((lp, D_FF), BF16), jax.ShapeDtypeStruct((D_FF, lp), BF16),
                   jax.ShapeDtypeStruct((2, lp, D_FF), BF16), jax.ShapeDtypeStruct((2, lp, D_FF), BF16)],
        scratch_shapes=[pltpu.VMEM((2, lp + FFN_PAD, FFN_TILE), F32), pltpu.VMEM((2, lp + FFN_PAD, FFN_TILE), F32)],
    )(n, w_up_t, w_up_t, w_up_t, w_up_t, w3, b3)


def _ffn_act_bwd(up3, gv3, w3, dact, *, name):
    _, lp, f = up3.shape
    nch = lp // CHUNK
    wlen = CHUNK + FFN_PAD

    def body(up_ref, gv_ref, w_ref, dact_ref, dup_ref, dw_ref, db_ref, ds, wacc):
        for half in range(2):
            ds[half, lp:lp + FFN_PAD, :] = jnp.zeros((FFN_PAD, FFN_TILE), F32)
        wacc[...] = jnp.zeros_like(wacc)

        def act_bwd(ci, _):
            rows = pl.ds(pl.multiple_of(ci * CHUNK, CHUNK), CHUNK)
            gate, val = gv_ref[0, rows, :].astype(F32), gv_ref[1, rows, :].astype(F32)
            sg = _sigmoid(gate)
            da = dact_ref[rows, :].astype(F32)
            ds[0, rows, :] = da * val * (sg * (1.0 + gate * (1.0 - sg)))
            ds[1, rows, :] = da * (gate * sg)
            return 0

        lax.fori_loop(0, nch, act_bwd, 0, unroll=True)

        def conv_bwd(ci, _):
            r0 = pl.multiple_of(ci * CHUNK, CHUNK)
            rows = pl.ds(r0, CHUNK)
            for half in range(2):
                dwin = ds[half, pl.ds(r0, wlen), :]
                x = up_ref[half, rows, :].astype(F32)
                acc = jnp.zeros((CHUNK, FFN_TILE), F32)
                for k in range(FFN_CONV_WIDTH):
                    s = FFN_CONV_WIDTH - 1 - k
                    d_s = (dwin if s == 0 else pltpu.roll(dwin, wlen - s, 0))[0:CHUNK, :]
                    acc = acc + w_ref[half, k:k + 1, :] * d_s
                    wacc[half, 8 * k:8 * k + 8, :] += _fold8(d_s * x)
                wacc[half, 24:32, :] += _fold8(dwin[0:CHUNK, :])
                dup_ref[half, rows, :] = acc.astype(BF16)
            return 0

        lax.fori_loop(0, nch, conv_bwd, 0)
        for half in range(2):
            for k in range(FFN_CONV_WIDTH):
                dw_ref[half, k:k + 1, :] = jnp.sum(wacc[half, 8 * k:8 * k + 8, :], axis=0, keepdims=True)
            db_ref[half] = jnp.sum(wacc[half, 24:32, :], axis=0, keepdims=True)

    halves = pl.BlockSpec((2, lp, FFN_TILE), lambda j: (0, 0, j))
    taps = pl.BlockSpec((2, FFN_CONV_WIDTH, FFN_TILE), lambda j: (0, 0, j))
    bias = pl.BlockSpec((2, 1, FFN_TILE), lambda j: (0, 0, j))
    return pl.pallas_call(
        body, name=name, grid=(f // FFN_TILE,),
        in_specs=[halves, halves, taps, pl.BlockSpec((lp, FFN_TILE), lambda j: (0, j))],
        out_specs=[halves, taps, bias],
        out_shape=[jax.ShapeDtypeStruct((2, lp, f), BF16), jax.ShapeDtypeStruct((2, FFN_CONV_WIDTH, f), F32),
                   jax.ShapeDtypeStruct((2, 1, f), F32)],
        scratch_shapes=[pltpu.VMEM((2, lp + FFN_PAD, FFN_TILE), F32), pltpu.VMEM((2, 32, FFN_TILE), F32)],
    )(up3, gv3, w3, dact)


POOL_PAD = 16


def _inv_count(r0, w):
    t = r0 + lax.broadcasted_iota(jnp.int32, (CHUNK, 1), 0)
    return 1.0 / jnp.minimum(t + 1, w).astype(F32)


def _pool_fwd(n, pool_w, pool_b, pool_scale, h, *, name):
    lp, dm = n.shape
    nch = lp // CHUNK
    g = POOL_GROUP

    def body(n_ref, w_ref, b_ref, s_ref, h_ref, ho_ref, dt_ref, z_ref, xs, d_ref):
        gi = pl.program_id(0)
        xs[0:POOL_PAD, :] = jnp.zeros((POOL_PAD, g), F32)
        xs[POOL_PAD:POOL_PAD + lp, :] = n_ref[...]
        for idx, w in enumerate(POOL_WINDOWS):
            @pl.when(gi == idx)
            def _(w=w):
                def chunk(ci, _):
                    r0 = pl.multiple_of(ci * CHUNK, CHUNK)
                    win = xs[pl.ds(r0, CHUNK + POOL_PAD), :]
                    acc = win
                    for j in range(1, w):
                        acc = acc + pltpu.roll(win, j, 0)
                    x = win[POOL_PAD:POOL_PAD + CHUNK, :]
                    d = acc[POOL_PAD:POOL_PAD + CHUNK, :] * _inv_count(r0, w) - x
                    d_ref[pl.ds(r0, CHUNK), :] = d.astype(BF16)
                    dt_ref[:, pl.ds(r0, CHUNK)] = d.astype(BF16).T
                    return 0

                lax.fori_loop(0, nch, chunk, 0, unroll=True)

        z = jnp.dot(d_ref[...], w_ref[...], preferred_element_type=F32) + b_ref[...]
        z_ref[...] = z
        ho_ref[...] = h_ref[...] + z * s_ref[...]

    col = pl.BlockSpec((lp, g), lambda i: (0, i))
    vec = pl.BlockSpec((1, g), lambda i: (0, i))
    return pl.pallas_call(
        body, name=name, grid=(len(POOL_WINDOWS),),
        in_specs=[col, pl.BlockSpec((None, g, g), lambda i: (i, 0, 0)), vec, vec, col],
        out_specs=[col, pl.BlockSpec((g, lp), lambda i: (i, 0)), col],
        out_shape=[jax.ShapeDtypeStruct((lp, dm), F32), jax.ShapeDtypeStruct((dm, lp), BF16),
                   jax.ShapeDtypeStruct((lp, dm), F32)],
        scratch_shapes=[pltpu.VMEM((lp + POOL_PAD, g), F32), pltpu.VMEM((lp, g), BF16)],
    )(n, pool_w, pool_b, pool_scale, h)


def _pool_bwd(dy, z, pool_w, pool_scale, *, name):
    lp, dm = dy.shape
    nch = lp // CHUNK
    g = POOL_GROUP
    wlen = CHUNK + POOL_PAD

    def body(dy_ref, z_ref, w_ref, s_ref, dn_ref, dz_ref, dsc_ref, db_ref, ys, dd):
        gi = pl.program_id(0)
        dyv = dy_ref[...]
        dsc_ref[...] = jnp.sum(dyv * z_ref[...], axis=0, keepdims=True)
        dz = dyv * s_ref[...]
        db_ref[...] = jnp.sum(dz, axis=0, keepdims=True)
        dzb = dz.astype(BF16)
        dz_ref[...] = dzb
        dd[...] = lax.dot_general(dzb, w_ref[...], (((1,), (1,)), ((), ())), preferred_element_type=F32)
        ys[lp:lp + POOL_PAD, :] = jnp.zeros((POOL_PAD, g), F32)
        for idx, w in enumerate(POOL_WINDOWS):
            @pl.when(gi == idx)
            def _(w=w):
                def scale(ci, _):
                    r0 = pl.multiple_of(ci * CHUNK, CHUNK)
                    ys[pl.ds(r0, CHUNK), :] = dd[pl.ds(r0, CHUNK), :] * _inv_count(r0, w)
                    return 0

                lax.fori_loop(0, nch, scale, 0, unroll=True)

                def chunk(ci, _):
                    r0 = pl.multiple_of(ci * CHUNK, CHUNK)
                    win = ys[pl.ds(r0, wlen), :]
                    acc = win
                    for j in range(1, w):
                        acc = acc + pltpu.roll(win, wlen - j, 0)
                    dn_ref[pl.ds(r0, CHUNK), :] = acc[0:CHUNK, :] - dd[pl.ds(r0, CHUNK), :]
                    return 0

                lax.fori_loop(0, nch, chunk, 0, unroll=True)

    col = pl.BlockSpec((lp, g), lambda i: (0, i))
    vec = pl.BlockSpec((1, g), lambda i: (0, i))
    return pl.pallas_call(
        body, name=name, grid=(len(POOL_WINDOWS),),
        in_specs=[col, col, pl.BlockSpec((None, g, g), lambda i: (i, 0, 0)), vec],
        out_specs=[col, col, vec, vec],
        out_shape=[jax.ShapeDtypeStruct((lp, dm), F32), jax.ShapeDtypeStruct((lp, dm), BF16),
                   jax.ShapeDtypeStruct((1, dm), F32), jax.ShapeDtypeStruct((1, dm), F32)],
        scratch_shapes=[pltpu.VMEM((lp + POOL_PAD, g), F32), pltpu.VMEM((lp, g), F32)],
    )(dy, z, pool_w, pool_scale)


def _ffn_fwd(h, g, weight, w3, b3, tag):
    n, n_t = _rms_fwd(h, g, name=f"rms_ffn{tag}", out_dtype=BF16, transposed=True)
    w_up_t = weight("up", n)
    act, act_t, gv3, up3 = _ffn_up_act(n, w_up_t, w3, b3, name=f"ffn_up_act{tag}")
    w_down = weight("down", act)
    h_out = _mm(act, w_down, name=f"mm_down{tag}", tn=256, res=h)
    return h_out, (n_t, up3, gv3, act_t), w_up_t, w_down


def _ffn_bwd(dh, dhb, h, g, saved, w_up_t, w3, w_down, tag, after=None):
    lp = h.shape[0]
    n_t, up3, gv3, act_t = saved
    dw_down = _mm(act_t, dhb, name=f"mm_dwdown{tag}", tm=704, out_dtype=BF16)
    dact = _mm(dhb, w_down, name=f"mm_dact{tag}", tb=True, tn=D_FF // 2, out_dtype=BF16, after=after)
    dup3, dcw, dcb = _ffn_act_bwd(up3, gv3, w3, dact, name=f"ffn_act_bwd{tag}")
    dup2 = dup3.reshape(2 * lp, D_FF)
    dw_up = _mm_dw_up(n_t, dup2, name=f"mm_dwup{tag}")
    dh_in, dh_in_b, dg = _mm_rms_bwd(dup2, w_up_t, h, g, dh, name=f"mm_dnffn{tag}", tk=D_FF // 2,
                                     a_map=lambda i, kk: (4 * (kk // 2) + i, kk % 2))
    return dh_in, dh_in_b, (dg, dw_up, dcw, dcb, dw_down)


def _local_step(x, tgt, wt):
    seq = x.shape[0]
    n_real = N_META + seq
    lp = _round_up(n_real, CHUNK)
    pad = jnp.zeros((lp - n_real, D_MODEL), F32)
    h0 = jnp.concatenate([wt["meta"], x, pad], axis=0)
    tgt_p = jnp.concatenate([jnp.zeros((N_META, D_MODEL), F32), tgt, pad], axis=0)
    w_in_p = wt["w_in_p"]

    n0, n0_t = _rms_fwd(h0, wt["g_even"], name="rms_even", out_dtype=BF16, after=wt["ffn_started"], transposed=True)
    qkv, q_t = _mm(n0, w_in_p, name="mm_qkv", tn=FOX_WIDTH, dims=(lp, 3 * FOX_WIDTH, D_MODEL), out_dtype=BF16,
                   first_tile_t=True)
    ag = _mm(n0, w_in_p, name="mm_ag", tn=512, dims=(lp, 2 * CONV_CH, D_MODEL),
             b_map=lambda i, j, k: (0, 3 + j))
    f_t = _mm(wt["wf_t"], n0, name="mm_ft", tb=True)
    c_row = _fox_prep(f_t, wt["b_f"], name="fox_prep")
    c3 = c_row.reshape(4, 2, lp)
    o = _attn_fwd(qkv, c3, name="attn_fwd")
    u0, u1, u3 = _conv_fwd(ag, wt["conv_w"], wt["conv_b"], wt["ln_g"], wt["ln_b"], name="conv_fwd")
    cat = jnp.concatenate([o, u3], axis=1)
    h1 = _mm(cat, wt["w_out"], name="mm_out", tn=512, res=h0)
    h2, saved0, w_up0, w_down0 = _ffn_fwd(h1, wt["ffn_norm"][0:1], functools.partial(wt["ffn_weight"], 0),
                                          wt["fcw3"][0], wt["fcb3"][0], 0)

    n2 = _rms_fwd(h2, wt["g_odd"], name="rms_odd", out_dtype=F32)
    h3, dpool_t, z = _pool_fwd(n2, wt["pool_w"], wt["pool_b"], wt["pool_scale"], h2, name="pool_fwd")
    h4, saved1, w_up1, w_down1 = _ffn_fwd(h3, wt["ffn_norm"][1:2], functools.partial(wt["ffn_weight"], 1),
                                          wt["fcw3"][1], wt["fcb3"][1], 1)

    loss, dh4, dh4b, d_gfinal = _loss_head(h4, wt["g_final"], tgt_p, n_real, name="loss_head")

    dh3, dh3b, gf1 = _ffn_bwd(dh4, dh4b, h3, wt["ffn_norm"][1:2], saved1, w_up1, wt["fcw3"][1], w_down1, 1)
    send1, token1 = _send_ffn_grads(gf1[1], gf1[4], 1)
    dn2, dzb, d_pscale, d_pb = _pool_bwd(dh3, z, wt["pool_w"], wt["pool_scale"], name="pool_bwd")
    d_pw = _mm(dpool_t, dzb, name="mm_dpoolw", tm=POOL_GROUP, tn=POOL_GROUP, dims=(D_MODEL, POOL_GROUP, lp),
               b_map=lambda i, j, k: (0, i), o_map=lambda i, j, k: (i, 0), out_shape=(D_MODEL, POOL_GROUP),
               out_dtype=BF16)
    dh2, dh2b, d_godd = _rms_bwd(h2, wt["g_odd"], dn2, dh3, name="rms_bwd_odd")
    dh1, dh1b, gf0 = _ffn_bwd(dh2, dh2b, h1, wt["ffn_norm"][0:1], saved0, w_up0, wt["fcw3"][0], w_down0, 0,
                              after=token1)

    send0, token0 = _send_ffn_grads(gf0[1], gf0[4], 0)
    cat_t = _transpose(cat, name="t_cat", out_dtype=BF16)
    d_wout = _mm(cat_t, dh1b, name="mm_dwout", tm=512, out_dtype=BF16)
    dcat, do_t = _mm(dh1b, wt["w_out"], name="mm_dcat", tb=True, tn=FOX_WIDTH, after=token0, first_tile_t=True)
    dag, d_convw, d_convb, d_lng, d_lnb = _conv_bwd(dcat, u0, u1, ag, wt["conv_w"], wt["ln_g"], wt["ln_b"],
                                                    name="conv_bwd")
    layers = lambda i: jnp.stack([gf0[i], gf1[i]])
    grads = dict(
        conv_w=d_convw[None], w_out=d_wout, mix_norm_odd=d_godd,
        pool_w=d_pw.reshape(len(POOL_WINDOWS), POOL_GROUP, POOL_GROUP),
        pool_b=d_pb.reshape(1, len(POOL_WINDOWS), POOL_GROUP), pool_scale=d_pscale, w_up=(gf0[1], gf1[1]),
        ffn_conv_w=layers(2).transpose(0, 2, 1, 3).reshape(DEPTH, FFN_CONV_WIDTH, 2 * D_FF), w_down=(gf0[4], gf1[4]))
    send_rest, token_rest = _send_rest_grads(grads)
    dq, dk, dv, dc3 = _attn_bwd(qkv, q_t, dcat, do_t, c3, name="attn_bwd", after=token_rest)
    df_t, d_bf = _fox_bwd(dc3.reshape(FOX_HEADS, lp), f_t, wt["b_f"], name="fox_bwd")
    df = _transpose(df_t, name="t_df", out_dtype=BF16)
    dproj = jnp.concatenate([dq, dk, dv, dag, df], axis=1)
    grads["w_in"] = _mm_dw_in(n0_t, dproj, name="mm_dwin")
    send_in, token_in = _send_start(
        [grads["w_in"]], [jax.ShapeDtypeStruct((N_DEV - 1, D_MODEL, _IN_SHARD), BF16)], [(0, _by_owner, 0, None)],
        name="send_w_in")
    dh0, _, d_geven = _mm_rms_bwd(dproj, w_in_p, h0, wt["g_even"], dh1, name="mm_dn0", tb=True, tk=896,
                                  after=token_in)
    grads.update(
        meta_tokens=dh0[0:N_META], mix_norm_even=d_geven, b_f=d_bf.reshape(1, FOX_HEADS), conv_b=d_convb, ln_g=d_lng,
        ln_b=d_lnb, ffn_norm=jnp.concatenate([gf0[0], gf1[0]], axis=0),
        ffn_conv_b=layers(3).reshape(DEPTH, 2 * D_FF), final_norm=d_gfinal.reshape(D_MODEL),
        sends=(send0, send1, send_rest, send_in))
    return loss, dh0[N_META:n_real], grads


_LITTLE = (("conv_w", (1, 31, 512), 2), ("mix_norm_odd", (1, 1024), 1), ("pool_b", (1, 4, 256), 2),
           ("pool_scale", (1, 1024), 1), ("ffn_conv_w", (2, 3, 5632), 2))


def _send_rest_grads(g):
    out_rows, pool_rows, groups = D_MODEL // N_DEV, POOL_GROUP // N_DEV, len(POOL_WINDOWS)
    little_slabs = _pack([_full_to_slabs(g[n], s, a) for n, s, a in _LITTLE], F32, lead=(N_DEV,), align=8)
    land = lambda shape, dtype: jax.ShapeDtypeStruct((N_DEV - 1,) + shape, dtype)
    handle, token = _send_start(
        [g["w_out"], g["pool_w"], little_slabs],
        [land((out_rows, D_MODEL), BF16), land((groups, pool_rows, POOL_GROUP), BF16), land(little_slabs.shape[1:], F32)],
        [(0, _row_block(out_rows), 0, None), (1, _row_block(pool_rows, axis=1), 1, None), (2, _by_owner, 2, None)],
        name="send_rest")
    return handle, token


def _send_ffn_grads(dw_up, dw_down, tag):
    rows = D_FF // N_DEV
    lands = [jax.ShapeDtypeStruct((N_DEV - 1,) + dw_up.shape[1:], BF16),
             jax.ShapeDtypeStruct((N_DEV - 1, rows, D_MODEL), BF16)]
    return _send_start([dw_up, dw_down], lands, [(0, _by_owner, 0, None), (1, _row_block(rows), 1, None)],
                       name=f"send_ffn{tag}")


_QKV = 3 * FOX_WIDTH
_GLU0 = _QKV + FOX_HEADS
_IN_COLS = _GLU0 + 2 * CONV_CH
_F_PAD = 128


_IN_SHARD = _IN_COLS // N_DEV
_UP_SHARD = 2 * D_FF // N_DEV
_ROW_TILE = 256


def _assemble_w_in(st, *, name):
    tr = _ROW_TILE

    def body(s_ref, o_ref):
        full = jnp.concatenate([s_ref[i].astype(F32) for i in range(N_DEV)], axis=1)
        parts = [full[:, :_QKV], full[:, _GLU0:], full[:, _QKV:_GLU0], jnp.zeros((tr, _F_PAD - FOX_HEADS), F32)]
        o_ref[...] = jnp.concatenate(parts, axis=1).astype(BF16)

    return pl.pallas_call(
        body, name=name, grid=(D_MODEL // tr,),
        in_specs=[pl.BlockSpec((N_DEV, tr, _IN_SHARD), lambda i: (0, i, 0))],
        out_specs=pl.BlockSpec((tr, _QKV + 2 * CONV_CH + _F_PAD), lambda i: (i, 0)),
        out_shape=jax.ShapeDtypeStruct((D_MODEL, _QKV + 2 * CONV_CH + _F_PAD), BF16),
    )(st)


def _mm_dw_in(n_t, dproj, *, name):
    dm, lp = n_t.shape
    tr = _ROW_TILE
    ag0 = _QKV + 2 * CONV_CH

    def body(a_ref, b_ref, o_ref):
        r = jnp.dot(a_ref[...], b_ref[...], preferred_element_type=F32)
        full = jnp.concatenate([r[:, :_QKV], r[:, ag0:ag0 + FOX_HEADS], r[:, _QKV:ag0]], axis=1)
        for i in range(N_DEV):
            o_ref[i] = full[:, i * _IN_SHARD:(i + 1) * _IN_SHARD].astype(BF16)

    return pl.pallas_call(
        body, name=name, grid=(dm // tr,),
        in_specs=[pl.BlockSpec((tr, lp), lambda i: (i, 0)), pl.BlockSpec(dproj.shape, lambda i: (0, 0))],
        out_specs=pl.BlockSpec((N_DEV, tr, _IN_SHARD), lambda i: (0, i, 0)),
        out_shape=jax.ShapeDtypeStruct((N_DEV, dm, _IN_SHARD), BF16),
    )(n_t, dproj)


def _mm_dw_up(n_t, dup2, *, name):
    dm, lp = n_t.shape
    pairs_per_half = D_FF // (2 * _UP_SHARD)

    def body(a_ref, b_ref, o_ref):
        r_t = jnp.dot(a_ref[...], b_ref[...], preferred_element_type=F32).astype(BF16).T
        o_ref[0] = r_t[:_UP_SHARD, :]
        o_ref[1] = r_t[_UP_SHARD:, :]

    return pl.pallas_call(
        body, name=name, grid=(N_DEV // 2,),
        in_specs=[pl.BlockSpec((dm, lp), lambda p: (0, 0)),
                  pl.BlockSpec((lp, 2 * _UP_SHARD), lambda p: (p // pairs_per_half, p % pairs_per_half))],
        out_specs=pl.BlockSpec((2, _UP_SHARD, dm), lambda p: (p, 0, 0)),
        out_shape=jax.ShapeDtypeStruct((N_DEV, _UP_SHARD, dm), BF16),
    )(n_t, dup2)


MESH = pl.DeviceIdType.MESH
ANY = pl.BlockSpec(memory_space=pl.ANY)


def _slot(px, py, pc):
    return 4 * px + 2 * py + pc


def _by_owner(ref, slot):
    return ref.at[slot]


def _row_block(rows, axis=0):
    def place(ref, slot):
        idx = (slice(None),) * axis + (pl.ds(slot * rows, rows),)
        return ref.at[idx]
    return place


def _all_gather(arrs, out_shapes, places, *, name):
    n = len(arrs)

    def body(*refs):
        ins, outs = refs[:n], refs[n:2 * n]
        send_sems, recv_sems, local_sems = refs[2 * n:]
        x, y, c = lax.axis_index("x"), lax.axis_index("y"), lax.axis_index("c")
        me, sibling = (x, y, c), (x, y, 1 - c)
        chips = [(1 - x, y), (x, 1 - y), (1 - x, 1 - y)]

        def copy(a, k, block, to, from_input=False):
            dst = places[a](outs[a], _slot(*block))
            return pltpu.make_async_remote_copy(
                src_ref=ins[a] if from_input else dst, dst_ref=dst,
                send_sem=send_sems.at[7 * a + k], recv_sem=recv_sems.at[7 * a + k],
                device_id=to, device_id_type=MESH)

        own, sent = [], []
        for a in range(n):
            mine = pltpu.make_async_copy(ins[a], places[a](outs[a], _slot(*me)), local_sems.at[a])
            mine.start()
            own.append(mine)
            first = [copy(a, 0, me, sibling, True)]
            first += [copy(a, 1 + j, me, (*chip, c), True) for j, chip in enumerate(chips)]
            for cp in first:
                cp.start()
            sent += first
        for a in range(n):
            for j, chip in enumerate(chips):
                copy(a, 1 + j, (*chip, c), me).wait_recv()
                passed = copy(a, 4 + j, (*chip, c), sibling)
                passed.start()
                sent.append(passed)
        for a in range(n):
            copy(a, 0, sibling, me).wait_recv()
            for j, chip in enumerate(chips):
                copy(a, 4 + j, (*chip, 1 - c), me).wait_recv()
        for cp in sent:
            cp.wait_send()
        for cp in own:
            cp.wait()

    return pl.pallas_call(
        body, name=name,
        in_specs=[ANY] * n, out_specs=[ANY] * n,
        out_shape=[jax.ShapeDtypeStruct(s, a.dtype) for s, a in zip(out_shapes, arrs)],
        scratch_shapes=[pltpu.SemaphoreType.DMA((7 * n,)), pltpu.SemaphoreType.DMA((7 * n,)),
                        pltpu.SemaphoreType.DMA((n,))],
    )(*arrs)


HBM = pl.BlockSpec(memory_space=pltpu.HBM)
SEM = pl.BlockSpec(memory_space=pltpu.SEMAPHORE)
EFFECT = pltpu.SideEffectType.DATAFLOW_SIDE_EFFECTING


def _relation_copies(src_refs, land_refs, copies, send_sems, recv_sems):
    x, y, c = lax.axis_index("x"), lax.axis_index("y"), lax.axis_index("c")
    flip = lambda v, bit: 1 - v if bit else v
    out = []
    for k in range(1, N_DEV):
        p = (flip(x, k & 4), flip(y, k & 2), flip(c, k & 1))
        for j, (si, take, li, put) in enumerate(copies):
            sem = (k - 1) * len(copies) + j
            dst = land_refs[li].at[k - 1] if put is None else put(land_refs[li], _slot(x, y, c))
            out.append(pltpu.make_async_remote_copy(
                src_ref=take(src_refs[si], _slot(*p)), dst_ref=dst,
                send_sem=send_sems.at[sem], recv_sem=recv_sems.at[sem], device_id=p, device_id_type=MESH))
    return out


def _own_copies(src_refs, land_refs, copies, sems):
    me = _slot(lax.axis_index("x"), lax.axis_index("y"), lax.axis_index("c"))
    placed = [(si, take, li, put) for si, take, li, put in copies if put is not None]
    return [pltpu.make_async_copy(take(src_refs[si], me), put(land_refs[li], me),
                                  sems.at[(N_DEV - 1) * len(copies) + j])
            for j, (si, take, li, put) in enumerate(placed)]


def _send_start(srcs, land_structs, copies, *, name, after=None):
    ns, nl = len(srcs), len(land_structs)
    n_sem = (N_DEV - 1) * len(copies) + sum(put is not None for _, _, _, put in copies)
    behind = [] if after is None else [after]

    def body(*refs):
        first_out = ns + nl + len(behind)
        send_sems, recv_sems, token = refs[first_out], refs[first_out + 1], refs[-1]
        for cp in _relation_copies(refs[:ns], refs[ns:ns + nl], copies, send_sems, recv_sems):
            cp.start()
        for cp in _own_copies(refs[:ns], refs[ns:ns + nl], copies, send_sems):
            cp.start()
        token[...] = jnp.zeros_like(token)

    in_hbm = lambda a: pltpu.with_memory_space_constraint(a, pltpu.HBM)
    outs = pl.pallas_call(
        body, name=name,
        out_shape=(pltpu.SemaphoreType.DMA((n_sem,)), pltpu.SemaphoreType.DMA((n_sem,)),
                   *[pltpu.HBM(s.shape, s.dtype) for s in srcs],
                   *[pltpu.HBM(s.shape, s.dtype) for s in land_structs],
                   jax.ShapeDtypeStruct((8, 128), F32)),
        in_specs=(HBM,) * (ns + nl) + (ANY,) * len(behind),
        out_specs=(SEM, SEM) + (HBM,) * (ns + nl) + (pl.BlockSpec(memory_space=pltpu.VMEM),),
        input_output_aliases={i: 2 + i for i in range(ns + nl)},
        compiler_params=pltpu.CompilerParams(has_side_effects=EFFECT),
    )(*[in_hbm(s) for s in srcs], *[in_hbm(lax.empty(s.shape, s.dtype)) for s in land_structs], *behind)
    return (outs[0], outs[1], outs[2:2 + ns], outs[2 + ns:2 + ns + nl], copies), outs[-1]


def _send_wait(handle, after, *, name):
    send_sems, recv_sems, srcs, lands, copies = handle
    ns, nl = len(srcs), len(lands)

    def body(*refs):
        for cp in _relation_copies(refs[:ns], refs[ns:ns + nl], copies, refs[ns + nl], refs[ns + nl + 1]):
            cp.wait_send()
            cp.wait_recv()
        for cp in _own_copies(refs[:ns], refs[ns:ns + nl], copies, refs[ns + nl]):
            cp.wait()

    outs = pl.pallas_call(
        body, name=name,
        out_shape=tuple(pltpu.HBM(a.shape, a.dtype) for a in (*srcs, *lands)),
        in_specs=(HBM,) * (ns + nl) + (SEM, SEM, ANY), out_specs=(HBM,) * (ns + nl),
        input_output_aliases={i: i for i in range(ns + nl)},
        compiler_params=pltpu.CompilerParams(has_side_effects=EFFECT),
    )(*srcs, *lands, send_sems, recv_sems, after)
    return outs[:ns], outs[ns:]


def _sum_slabs(stack, *, name, own=None):
    n, rows, w = stack.shape

    def body(*refs):
        s_ref, o_ref = refs[-2], refs[-1]
        acc = s_ref[0] if own is None else refs[0][...] + s_ref[0]
        for i in range(1, n):
            acc = acc + s_ref[i]
        o_ref[...] = acc

    return pl.pallas_call(body, name=name, out_shape=jax.ShapeDtypeStruct((rows, w), F32))(
        *([] if own is None else [own]), stack)


def _adam_math(w, g, m, v):
    mn = ADAM_B1 * m + (1.0 - ADAM_B1) * g
    vn = ADAM_B2 * v + (1.0 - ADAM_B2) * (g * g)
    m_hat = mn / (1.0 - ADAM_B1 ** ADAM_STEP)
    v_hat = vn / (1.0 - ADAM_B2 ** ADAM_STEP)
    return -ADAM_LR * (m_hat / (jnp.sqrt(v_hat) + ADAM_EPS) + ADAM_WD * w), mn, vn


def _adamw_many(ws, gs, ms, vs, *, name):
    n = len(ws)

    def body(*refs):
        for i in range(n):
            w_ref, g_ref, m_ref, v_ref = (refs[j * n + i] for j in range(4))
            d_ref, mo_ref, vo_ref = refs[4 * n + 3 * i:4 * n + 3 * i + 3]
            d_ref[...], mo_ref[...], vo_ref[...] = _adam_math(w_ref[...], g_ref[...], m_ref[...], v_ref[...])

    return pl.pallas_call(
        body, name=name, out_shape=[jax.ShapeDtypeStruct(w.shape, F32) for w in ws for _ in range(3)],
    )(*ws, *gs, *ms, *vs)


def _adamw_layers(w, owns, lands, m, v, tr, *, name):
    nl, rows, cols = w.shape
    steps = rows // tr
    assert rows % tr == 0

    def body(*refs):
        w_ref, m_ref, v_ref = refs[:3]
        own_refs, land_refs = refs[3:3 + nl], refs[3 + nl:3 + 2 * nl]
        g_ref, d_ref, mo_ref, vo_ref = refs[3 + 2 * nl:]
        for li in range(nl):
            @pl.when(pl.program_id(0) == li)
            def _(li=li):
                g = own_refs[li][...].astype(F32)
                for k in range(N_DEV - 1):
                    g = g + land_refs[li][k].astype(F32)
                g_ref[...] = g
                d_ref[...], mo_ref[...], vo_ref[...] = _adam_math(w_ref[...], g, m_ref[...], v_ref[...])

    def held(li):
        return lambda l, i: jnp.where(l == li, i, jnp.where(l < li, 0, steps - 1))

    blk = pl.BlockSpec((None, tr, cols), lambda l, i: (l, i, 0))
    own_specs = [pl.BlockSpec((tr, cols), lambda l, i, f=held(li): (f(l, i), 0)) for li in range(nl)]
    land_specs = [pl.BlockSpec((N_DEV - 1, tr, cols), lambda l, i, f=held(li): (0, f(l, i), 0)) for li in range(nl)]
    return pl.pallas_call(
        body, name=name, grid=(nl, steps),
        in_specs=[blk, blk, blk] + own_specs + land_specs, out_specs=[blk] * 4,
        out_shape=[jax.ShapeDtypeStruct(w.shape, F32)] * 4,
    )(w, m, v, *owns, *lands)


_WEIGHTS = (
    ("meta_tokens", (16, 1024), 1), ("mix_norm_even", (1, 1024), None), ("w_in", (1, 1024, 2568), 2),
    ("b_f", (1, 8), None), ("conv_w", (1, 31, 512), 2), ("conv_b", (1, 512), None), ("ln_g", (1, 512), None),
    ("ln_b", (1, 512), None), ("w_out", (1, 1024, 1024), 1), ("mix_norm_odd", (1, 1024), 1),
    ("pool_w", (1, 4, 256, 256), 2), ("pool_b", (1, 4, 256), 2), ("pool_scale", (1, 1024), 1),
    ("ffn_norm", (2, 1024), None), ("w_up", (2, 1024, 5632), 2), ("ffn_conv_w", (2, 3, 5632), 2),
    ("ffn_conv_b", (2, 5632), None), ("w_down", (2, 2816, 1024), 1), ("final_norm", (1024,), None),
)
_MATMUL_WEIGHTS = ("w_in", "w_out", "pool_w", "w_up", "w_down")
_ADAM_ROWS = dict(w_in=256, w_out=128, pool_w=128, w_up=352, w_down=352)


def _shard_shape(shape, axis):
    return shape[:axis] + (shape[axis] // N_DEV,) + shape[axis + 1:]


def _size(shape):
    n = 1
    for s in shape:
        n *= s
    return n


def _pack(parts, dtype, lead=(), align=16):
    flat = jnp.concatenate([p.reshape(lead + (-1,)).astype(dtype) for p in parts], axis=-1)
    n = flat.shape[-1]
    rows = _round_up(-(-n // FLAT_W), align)
    flat = jnp.pad(flat, [(0, 0)] * len(lead) + [(0, rows * FLAT_W - n)])
    return flat.reshape(lead + (rows, FLAT_W))


def _unpack(buf, shapes, lead=()):
    flat = buf.reshape(lead + (-1,))
    out, off = [], 0
    for shp in shapes:
        n = _size(shp)
        out.append(flat[..., off:off + n].reshape(lead + shp))
        off += n
    return out


def _gathered_to_full(stack, shape, axis):
    return jnp.moveaxis(stack, 0, axis).reshape(shape)


def _full_to_slabs(full, shape, axis):
    split = shape[:axis] + (N_DEV, shape[axis] // N_DEV) + shape[axis + 1:]
    return jnp.moveaxis(full.reshape(split), axis, 0)


def kernel(x, meta_tokens, mix_norm_even, w_in, b_f, conv_w, conv_b, ln_g, ln_b, w_out, mix_norm_odd, pool_w, pool_b, pool_scale, ffn_norm, w_up, ffn_conv_w, ffn_conv_b, w_down, final_norm, loss_target, m_meta_tokens, m_mix_norm_even, m_w_in, m_b_f, m_conv_w, m_conv_b, m_ln_g, m_ln_b, m_w_out, m_mix_norm_odd, m_pool_w, m_pool_b, m_pool_scale, m_ffn_norm, m_w_up, m_ffn_conv_w, m_ffn_conv_b, m_w_down, m_final_norm, v_meta_tokens, v_mix_norm_even, v_w_in, v_b_f, v_conv_w, v_conv_b, v_ln_g, v_ln_b, v_w_out, v_mix_norm_odd, v_pool_w, v_pool_b, v_pool_scale, v_ffn_norm, v_w_up, v_ffn_conv_w, v_ffn_conv_b, v_w_down, v_final_norm):
    names = [n for n, _, _ in _WEIGHTS]
    w_loc = dict(zip(names, (meta_tokens, mix_norm_even, w_in, b_f, conv_w, conv_b, ln_g, ln_b, w_out, mix_norm_odd,
                             pool_w, pool_b, pool_scale, ffn_norm, w_up, ffn_conv_w, ffn_conv_b, w_down, final_norm)))
    m_loc = dict(zip(names, (m_meta_tokens, m_mix_norm_even, m_w_in, m_b_f, m_conv_w, m_conv_b, m_ln_g, m_ln_b,
                             m_w_out, m_mix_norm_odd, m_pool_w, m_pool_b, m_pool_scale, m_ffn_norm, m_w_up,
                             m_ffn_conv_w, m_ffn_conv_b, m_w_down, m_final_norm)))
    v_loc = dict(zip(names, (v_meta_tokens, v_mix_norm_even, v_w_in, v_b_f, v_conv_w, v_conv_b, v_ln_g, v_ln_b,
                             v_w_out, v_mix_norm_odd, v_pool_w, v_pool_b, v_pool_scale, v_ffn_norm, v_w_up,
                             v_ffn_conv_w, v_ffn_conv_b, v_w_down, v_final_norm)))
    replicated = [(n, s) for n, s, a in _WEIGHTS if a is None]
    little = [(n, s, a) for n, s, a in _WEIGHTS if a is not None and n not in _MATMUL_WEIGHTS]
    little_shards = [_shard_shape(s, a) for _, s, a in little]
    out_rows, down_rows, pool_rows = D_MODEL // N_DEV, D_FF // N_DEV, POOL_GROUP // N_DEV
    n_groups = len(POOL_WINDOWS)

    little_pack = _pack([w_loc[n] for n, _, _ in little], F32)
    g_win, g_wout, g_poolw, g_little = _all_gather(
        [w_in[0].astype(BF16), w_out[0].astype(BF16), pool_w[0].astype(BF16), little_pack],
        [(N_DEV, D_MODEL, _IN_SHARD), (D_MODEL, D_MODEL), (n_groups, POOL_GROUP, POOL_GROUP),
         (N_DEV,) + little_pack.shape],
        [_by_owner, _row_block(out_rows), _row_block(pool_rows, axis=1), _by_owner],
        name="gather_weights")
    me = _slot(lax.axis_index("x"), lax.axis_index("y"), lax.axis_index("c"))
    up_t = lambda a: jnp.transpose(a, (0, 2, 1))
    w_loc["w_up"], m_loc["w_up"], v_loc["w_up"] = up_t(w_up), up_t(m_w_up), up_t(v_w_up)
    w_up_b, w_down_b = w_loc["w_up"].astype(BF16), w_down.astype(BF16)
    whole = lambda ref, slot: ref
    ffn_gathers, behind = {}, g_little
    for l in range(DEPTH):
        for part, shard, rows in (("up", w_up_b[l], _UP_SHARD), ("down", w_down_b[l], down_rows)):
            ffn_gathers[l, part], behind = _send_start(
                [shard], [jax.ShapeDtypeStruct((N_DEV * rows, D_MODEL), BF16)], [(0, whole, 0, _row_block(rows))],
                name=f"gather_{part}{l}_start", after=behind)

    def ffn_weight(l, part, after):
        return _send_wait(ffn_gathers[l, part], after, name=f"gather_{part}{l}_wait")[1][0]

    w_in_p = _assemble_w_in(g_win, name="assemble_w_in")
    full = {n: _gathered_to_full(st, s, a)
            for (n, s, a), st in zip(little, _unpack(g_little, little_shards, lead=(N_DEV,)))}
    f0 = _QKV + 2 * CONV_CH
    wt = dict(
        meta=full["meta_tokens"], g_even=mix_norm_even, w_in_p=w_in_p, wf_t=w_in_p[:, f0:f0 + FOX_HEADS].T,
        b_f=b_f.reshape(FOX_HEADS, 1), conv_w=full["conv_w"][0], conv_b=conv_b, ln_g=ln_g, ln_b=ln_b, w_out=g_wout,
        g_odd=full["mix_norm_odd"], pool_w=g_poolw, pool_b=full["pool_b"].reshape(1, D_MODEL),
        pool_scale=full["pool_scale"], ffn_norm=ffn_norm, ffn_weight=ffn_weight, ffn_started=behind,
        fcw3=full["ffn_conv_w"].reshape(DEPTH, FFN_CONV_WIDTH, 2, D_FF).transpose(0, 2, 1, 3),
        fcb3=ffn_conv_b.reshape(DEPTH, 2, 1, D_FF), g_final=final_norm.reshape(1, D_MODEL))

    loss_part, grad_x, g = _local_step(x[0], loss_target[0], wt)

    small = _pack([loss_part[:, 0:1]] + [g[n] for n, _ in replicated] + [g["meta_tokens"]], F32, align=8)
    send_small, token_small = _send_start([small], [jax.ShapeDtypeStruct((N_DEV,) + small.shape, F32)],
                                          [(0, whole, 0, _by_owner)], name="send_small")

    grads, delta, new_m, new_v = {}, {}, {}, {}
    send0, send1, send_rest, send_in = g["sends"]
    ffn_sent = [_send_wait(send, token_small, name=f"wait_ffn{l}") for l, send in enumerate((send0, send1))]
    own_up = [lax.dynamic_index_in_dim(srcs[0], me, 0, keepdims=False) for srcs, _ in ffn_sent]
    own_down = [lax.dynamic_slice_in_dim(srcs[1], me * down_rows, down_rows, 0) for srcs, _ in ffn_sent]
    for n, owns, idx in (("w_up", own_up, 0), ("w_down", own_down, 1)):
        grads[n], delta[n], new_m[n], new_v[n] = _adamw_layers(
            w_loc[n], owns, [lands[idx] for _, lands in ffn_sent], m_loc[n], v_loc[n], _ADAM_ROWS[n],
            name=f"adamw_{n}")
    for d in (grads, delta, new_m, new_v):
        d["w_up"] = up_t(d["w_up"])
    (d_out, d_pool, little_slabs), (land_out, land_pool, land_little) = _send_wait(
        send_rest, delta["w_down"], name="wait_rest")
    (d_in,), (land_in,) = _send_wait(send_in, land_out, name="wait_w_in")
    pool_2d = (n_groups * pool_rows, POOL_GROUP)
    own_pool = lax.dynamic_slice_in_dim(d_pool, me * pool_rows, pool_rows, 1)
    for n, own, land, shp in (
            ("w_in", lax.dynamic_index_in_dim(d_in, me, 0, keepdims=False), land_in, w_in.shape),
            ("w_out", lax.dynamic_slice_in_dim(d_out, me * out_rows, out_rows, 0), land_out, w_out.shape),
            ("pool_w", own_pool.reshape(pool_2d), land_pool.reshape((N_DEV - 1,) + pool_2d), (1,) + pool_2d)):
        outs = _adamw_layers(w_loc[n].reshape(shp), [own], [land], m_loc[n].reshape(shp), v_loc[n].reshape(shp),
                             _ADAM_ROWS[n], name=f"adamw_{n}")
        grads[n], delta[n], new_m[n], new_v[n] = (o.reshape(w_loc[n].shape) for o in outs)
    own_little = lax.dynamic_index_in_dim(little_slabs, me, 0, keepdims=False)
    g_little = _unpack(_sum_slabs(land_little, own=own_little, name="sum_little"),
                       [_shard_shape(s, a) for _, s, a in _LITTLE])
    grads.update({n: gl for (n, _, _), gl in zip(_LITTLE, g_little)})
    _, (everyone,) = _send_wait(send_small, delta["w_in"], name="wait_small")
    summed = _unpack(_sum_slabs(everyone, name="sum_small"),
                     [(1, 1)] + [s for _, s in replicated] + [(N_META, D_MODEL)])
    loss = summed[0].reshape(())
    grads.update({n: gr for (n, _), gr in zip(replicated, summed[1:-1])})
    grads["meta_tokens"] = lax.dynamic_slice_in_dim(summed[-1], me * out_rows, out_rows, 1)
    at_least_2d = lambda a: a.reshape((1,) * (2 - a.ndim) + a.shape)
    rest = [n for n in names if n not in _MATMUL_WEIGHTS]
    outs = _adamw_many([at_least_2d(w_loc[n]) for n in rest], [at_least_2d(grads[n]) for n in rest],
                       [at_least_2d(m_loc[n]) for n in rest], [at_least_2d(v_loc[n]) for n in rest],
                       name="adamw_rest")
    for i, n in enumerate(rest):
        delta[n], new_m[n], new_v[n] = (o.reshape(w_loc[n].shape) for o in outs[3 * i:3 * i + 3])
    return (loss, grad_x[None], *[grads[n] for n in names], *[delta[n] for n in names],
            *[new_m[n] for n in names], *[new_v[n] for n in names])
```

```python
import functools

import jax
import jax.numpy as jnp
from jax import lax
from jax.experimental import pallas as pl
from jax.experimental.pallas import tpu as pltpu

F32 = jnp.float32
BF16 = jnp.bfloat16

N_DEV = 8
DEPTH = 2
D_MODEL = 1024
N_META = 16
FOX_HEADS = 8
FOX_HEAD_DIM = 64
FOX_WIDTH = 512
CONV_CH = 512
CONV_WIDTH = 31
POOL_WINDOWS = (2, 4, 8, 16)
POOL_GROUP = 256
D_FF = 2816
FFN_CONV_WIDTH = 3
RMS_EPS = 1e-6
LN_EPS = 1e-5
ADAM_LR = 0.001
ADAM_B1 = 0.9
ADAM_B2 = 0.999
ADAM_EPS = 1e-08
ADAM_WD = 0.01
ADAM_STEP = 10

CHUNK = 128
HALO = 32
NEG_BIG = -1e30
FLAT_W = 1024


def _round_up(n, m):
    return (n + m - 1) // m * m


def _sigmoid(x):
    return 1.0 / (1.0 + jnp.exp(-x))


def _fold8(p):
    acc = p[0:8, :]
    for r in range(1, p.shape[0] // 8):
        acc = acc + p[8 * r:8 * r + 8, :]
    return acc


def _mm(a, b, *, name, tb=False, tm=None, tn=None, tk=None, out_dtype=F32, res=None,
        a_map=None, b_map=None, o_map=None, out_shape=None, dims=None, after=None, first_tile_t=False):
    if dims is None:
        m, k = a.shape
        n = b.shape[-2] if tb else b.shape[-1]
    else:
        m, n, k = dims
    tm, tn, tk = tm or m, tn or n, tk or k
    assert m % tm == 0 and n % tn == 0 and k % tk == 0, (name, m, n, k, tm, tn, tk)
    nk = k // tk
    a_map = a_map or (lambda i, j, kk: (i, kk))
    b_map = b_map or ((lambda i, j, kk: (j, kk)) if tb else (lambda i, j, kk: (kk, j)))
    o_map = o_map or (lambda i, j, kk: (i, j))
    out_shape = out_shape or (m, n)
    contract = (((1,), (1,)), ((), ())) if tb else (((1,), (0,)), ((), ()))
    has_res = res is not None

    def body(*refs):
        a_ref, b_ref = refs[0], refs[1]
        res_ref = refs[2] if has_res else None
        o_ref = refs[2 + has_res + (after is not None)]
        p = lax.dot_general(a_ref[...], b_ref[...], contract, preferred_element_type=F32)
        if nk == 1:
            if has_res:
                p = p + res_ref[...]
            o_ref[...] = p.astype(o_ref.dtype)
            if first_tile_t:
                @pl.when(pl.program_id(1) == 0)
                def _():
                    refs[3 + has_res + (after is not None)][...] = p.astype(BF16).T
        else:
            acc_ref = refs[-1]
            kk = pl.program_id(2)

            @pl.when(kk == 0)
            def _():
                acc_ref[...] = p

            @pl.when(kk > 0)
            def _():
                acc_ref[...] += p

            @pl.when(kk == nk - 1)
            def _():
                r = acc_ref[...]
                if has_res:
                    r = r + res_ref[...]
                o_ref[...] = r.astype(o_ref.dtype)

    in_specs = [pl.BlockSpec((tm, tk), a_map), pl.BlockSpec((tn, tk) if tb else (tk, tn), b_map)]
    operands = [a, b]
    if has_res:
        in_specs.append(pl.BlockSpec((tm, tn), o_map))
        operands.append(res)
    if after is not None:
        in_specs.append(pl.BlockSpec(memory_space=pl.ANY))
        operands.append(after)
    out_specs, out_struct = pl.BlockSpec((tm, tn), o_map), jax.ShapeDtypeStruct(out_shape, out_dtype)
    if first_tile_t:
        assert tm == m and nk == 1
        out_specs = [out_specs, pl.BlockSpec((tn, m), lambda i, j, kk: (0, 0))]
        out_struct = [out_struct, jax.ShapeDtypeStruct((tn, m), BF16)]
    return pl.pallas_call(
        body, name=name, grid=(m // tm, n // tn, nk),
        in_specs=in_specs, out_specs=out_specs, out_shape=out_struct,
        scratch_shapes=[pltpu.VMEM((tm, tn), F32)] if nk > 1 else [],
    )(*operands)


def _transpose(x, *, name, out_dtype):
    r, c = x.shape
    assert r % CHUNK == 0

    def body(x_ref, o_ref):
        o_ref[...] = x_ref[...].astype(o_ref.dtype).T

    return pl.pallas_call(
        body, name=name, grid=(r // CHUNK,),
        in_specs=[pl.BlockSpec((CHUNK, c), lambda i: (i, 0))],
        out_specs=pl.BlockSpec((c, CHUNK), lambda i: (0, i)),
        out_shape=jax.ShapeDtypeStruct((c, r), out_dtype),
    )(x)


def _rms_fwd(x, g, *, name, out_dtype, after=None, transposed=False):
    lp, dm = x.shape
    tr = CHUNK if transposed else lp // 4
    behind = [] if after is None else [after]

    def body(x_ref, g_ref, *rest):
        xv = x_ref[...]
        r = lax.rsqrt(jnp.mean(xv * xv, axis=-1, keepdims=True) + RMS_EPS)
        y = xv * r * g_ref[...]
        if transposed:
            yb = y.astype(out_dtype)
            rest[-2][...] = yb
            rest[-1][...] = yb.T
        else:
            rest[-1][...] = y.astype(out_dtype)

    row = pl.BlockSpec((tr, dm), lambda i: (i, 0))
    out_specs, out_shape = row, jax.ShapeDtypeStruct((lp, dm), out_dtype)
    if transposed:
        out_specs = [row, pl.BlockSpec((dm, tr), lambda i: (0, i))]
        out_shape = [out_shape, jax.ShapeDtypeStruct((dm, lp), out_dtype)]
    return pl.pallas_call(
        body, name=name, grid=(lp // tr,),
        in_specs=[row, pl.BlockSpec((1, dm), lambda i: (0, 0))] + [pl.BlockSpec(memory_space=pl.ANY)] * len(behind),
        out_specs=out_specs, out_shape=out_shape,
    )(x, g, *behind)


def _rms_bwd_rows(x_ref, g_ref, dnv, dres_ref, dh_ref, dhb_ref, dg_ref, first):
    xv = x_ref[...]
    r = lax.rsqrt(jnp.mean(xv * xv, axis=-1, keepdims=True) + RMS_EPS)
    xhat = xv * r

    @pl.when(first)
    def _():
        dg_ref[...] = jnp.zeros_like(dg_ref)

    dg_ref[...] += jnp.sum(dnv * xhat, axis=0, keepdims=True)
    dxhat = dnv * g_ref[...]
    dx = r * (dxhat - xhat * jnp.mean(dxhat * xhat, axis=-1, keepdims=True))
    dh = dres_ref[...] + dx
    dh_ref[...] = dh
    dhb_ref[...] = dh.astype(BF16)


def _mm_rms_bwd(a, b, x, g, dres, *, name, tk, tb=False, a_map=None, after=None):
    lp, dm = x.shape
    tm = lp // 4
    nk = (b.shape[1] if tb else b.shape[0]) // tk
    a_map = a_map or (lambda i, kk: (i, kk))
    contract = (((1,), (1,)), ((), ())) if tb else (((1,), (0,)), ((), ()))
    behind = [] if after is None else [after]

    def body(a_ref, b_ref, x_ref, g_ref, dres_ref, *rest):
        dh_ref, dhb_ref, dg_ref, acc_ref = rest[len(behind):]
        i, kk = pl.program_id(0), pl.program_id(1)
        p = lax.dot_general(a_ref[...], b_ref[...], contract, preferred_element_type=F32)

        @pl.when(kk == 0)
        def _():
            acc_ref[...] = p

        @pl.when(kk > 0)
        def _():
            acc_ref[...] += p

        @pl.when(kk == nk - 1)
        def _():
            _rms_bwd_rows(x_ref, g_ref, acc_ref[...], dres_ref, dh_ref, dhb_ref, dg_ref, i == 0)

    row = pl.BlockSpec((tm, dm), lambda i, kk: (i, 0))
    vec = pl.BlockSpec((1, dm), lambda i, kk: (0, 0))
    b_spec = pl.BlockSpec((dm, tk), lambda i, kk: (0, kk)) if tb else pl.BlockSpec((tk, dm), lambda i, kk: (kk, 0))
    return pl.pallas_call(
        body, name=name, grid=(lp // tm, nk),
        in_specs=[pl.BlockSpec((tm, tk), a_map), b_spec, row, vec, row] + [pl.BlockSpec(memory_space=pl.ANY)] * len(behind),
        out_specs=[row, row, vec],
        out_shape=[jax.ShapeDtypeStruct((lp, dm), F32), jax.ShapeDtypeStruct((lp, dm), BF16),
                   jax.ShapeDtypeStruct((1, dm), F32)],
        scratch_shapes=[pltpu.VMEM((tm, dm), F32)],
    )(a, b, x, g, dres, *behind)


def _rms_bwd(x, g, dn, dres, *, name):
    lp, dm = x.shape
    tr = lp // 4

    def body(x_ref, g_ref, dn_ref, dres_ref, dh_ref, dhb_ref, dg_ref):
        _rms_bwd_rows(x_ref, g_ref, dn_ref[...], dres_ref, dh_ref, dhb_ref, dg_ref, pl.program_id(0) == 0)

    row = pl.BlockSpec((tr, dm), lambda i: (i, 0))
    vec = pl.BlockSpec((1, dm), lambda i: (0, 0))
    return pl.pallas_call(
        body, name=name, grid=(lp // tr,),
        in_specs=[row, vec, row, row], out_specs=[row, row, vec],
        out_shape=[jax.ShapeDtypeStruct((lp, dm), F32), jax.ShapeDtypeStruct((lp, dm), BF16),
                   jax.ShapeDtypeStruct((1, dm), F32)],
    )(x, g, dn, dres)


def _loss_head(h, g, tgt, n_real, *, name):
    lp, dm = h.shape
    tr = lp // 4

    def body(x_ref, g_ref, t_ref, loss_ref, dh_ref, dhb_ref, dg_ref):
        i = pl.program_id(0)
        xv = x_ref[...]
        r = lax.rsqrt(jnp.mean(xv * xv, axis=-1, keepdims=True) + RMS_EPS)
        xhat = xv * r
        gv = g_ref[...]
        y = xhat * gv
        t = i * tr + lax.broadcasted_iota(jnp.int32, (tr, 1), 0)
        valid = (t >= N_META) & (t < n_real)
        diff = jnp.where(valid, y - t_ref[...], 0.0)

        @pl.when(i == 0)
        def _():
            loss_ref[...] = jnp.zeros_like(loss_ref)
            dg_ref[...] = jnp.zeros_like(dg_ref)

        row_sq = jnp.sum(diff * diff, axis=-1, keepdims=True) * (1.0 / dm)
        part = 0.5 * jnp.sum(row_sq, axis=0, keepdims=True)
        loss_ref[...] += jnp.broadcast_to(part, loss_ref.shape)
        dy = diff * (1.0 / dm)
        dg_ref[...] += jnp.sum(dy * xhat, axis=0, keepdims=True)
        dxhat = dy * gv
        dx = r * (dxhat - xhat * jnp.mean(dxhat * xhat, axis=-1, keepdims=True))
        dh_ref[...] = dx
        dhb_ref[...] = dx.astype(BF16)

    row = pl.BlockSpec((tr, dm), lambda i: (i, 0))
    vec = pl.BlockSpec((1, dm), lambda i: (0, 0))
    return pl.pallas_call(
        body, name=name, grid=(lp // tr,),
        in_specs=[row, vec, row],
        out_specs=[pl.BlockSpec((1, 128), lambda i: (0, 0)), row, row, vec],
        out_shape=[jax.ShapeDtypeStruct((1, 128), F32), jax.ShapeDtypeStruct((lp, dm), F32),
                   jax.ShapeDtypeStruct((lp, dm), BF16), jax.ShapeDtypeStruct((1, dm), F32)],
    )(h, g, tgt)


def _tri(upper):
    r = lax.broadcasted_iota(jnp.int32, (CHUNK, CHUNK), 0)
    c = lax.broadcasted_iota(jnp.int32, (CHUNK, CHUNK), 1)
    return jnp.where(r <= c if upper else r >= c, 1.0, 0.0).astype(F32)


def _fox_prep(f_t, b_f, *, name):
    nh, lp = f_t.shape
    nch = lp // CHUNK

    def body(f_ref, b_ref, c_ref):
        tri = _tri(True)
        carry = jnp.zeros((nh, 1), F32)
        for blk in range(nch):
            cols = slice(blk * CHUNK, (blk + 1) * CHUNK)
            z = f_ref[:, cols] + b_ref[...]
            logf = jnp.minimum(z, 0.0) - jnp.log(1.0 + jnp.exp(-jnp.abs(z)))
            cb = jnp.dot(logf, tri, preferred_element_type=F32, precision=lax.Precision.HIGHEST)
            c_ref[:, cols] = cb + carry
            carry = carry + jnp.sum(logf, axis=1, keepdims=True)

    return pl.pallas_call(
        body, name=name, out_shape=jax.ShapeDtypeStruct((nh, lp), F32),
    )(f_t, b_f)


def _fox_bwd(dc, f_t, b_f, *, name):
    nh, lp = f_t.shape
    nch = lp // CHUNK

    def body(dc_ref, f_ref, b_ref, df_ref, db_ref):
        tri = _tri(False)
        carry = jnp.zeros((nh, 1), F32)
        db = jnp.zeros((nh, 1), F32)
        df_ref[...] = jnp.zeros_like(df_ref)
        for blk in reversed(range(nch)):
            cols = slice(blk * CHUNK, (blk + 1) * CHUNK)
            dcb = dc_ref[:, cols]
            dlogf = jnp.dot(dcb, tri, preferred_element_type=F32, precision=lax.Precision.HIGHEST) + carry
            carry = carry + jnp.sum(dcb, axis=1, keepdims=True)
            z = f_ref[:, cols] + b_ref[...]
            dz = dlogf * _sigmoid(-z)
            df_ref[0:nh, cols] = dz
            db = db + jnp.sum(dz, axis=1, keepdims=True)
        db_ref[...] = db

    return pl.pallas_call(
        body, name=name,
        out_shape=[jax.ShapeDtypeStruct((128, lp), F32), jax.ShapeDtypeStruct((nh, 1), F32)],
    )(dc, f_t, b_f)


ATTN_BLOCKS = 4


def _attn_blocks(lp):
    tq = lp // ATTN_BLOCKS
    return tq, [(i * tq, min(lp, _round_up((i + 1) * tq, CHUNK))) for i in range(ATTN_BLOCKS)]


ATTN_SCALE = FOX_HEAD_DIM ** -0.5


def _mask_heads(k_ref, v_ref, kv):
    lane = lax.broadcasted_iota(jnp.int32, (1, 128), 1)
    zero = jnp.zeros((), BF16)
    for hd, sel in enumerate((lane < 64, lane >= 64)):
        kv[hd] = jnp.where(sel, k_ref[...], zero)
        kv[2 + hd] = jnp.where(sel, v_ref[...], zero)


def _attn_probs(q2s, k_h, c_row, row0, n):
    tq = q2s.shape[0]
    lo = row0 // CHUNK * CHUNK
    logits = []
    for c0, c1 in ([(0, lo)] if lo else []) + [(lo, n)]:
        s = lax.dot_general(q2s, k_h[c0:c1, :], (((1,), (1,)), ((), ())), preferred_element_type=F32) - c_row[:, c0:c1]
        if c1 > row0:
            t = row0 + lax.broadcasted_iota(jnp.int32, (tq, c1 - c0), 0)
            sidx = c0 + lax.broadcasted_iota(jnp.int32, (tq, c1 - c0), 1)
            s = jnp.where(sidx <= t, s, NEG_BIG)
        logits.append((s, c0, c1))
    m = functools.reduce(jnp.maximum, [jnp.max(s, axis=1, keepdims=True) for s, _, _ in logits])
    ps = [(jnp.exp(s - m), c0, c1) for s, c0, c1 in logits]
    inv = 1.0 / sum(jnp.sum(p, axis=1, keepdims=True) for p, _, _ in ps)
    return [(p * inv, c0, c1) for p, c0, c1 in ps]


def _attn_fwd(qkv, c3, *, name):
    lp = qkv.shape[0]
    tq, blocks = _attn_blocks(lp)

    def body(q_ref, k_ref, v_ref, c_ref, o_ref, kv):
        _mask_heads(k_ref, v_ref, kv)
        for i, (row0, n) in enumerate(blocks):
            q2s = q_ref[row0:row0 + tq, :] * ATTN_SCALE
            acc = jnp.zeros((tq, 128), F32)
            for hd in range(2):
                for p, c0, c1 in _attn_probs(q2s, kv.at[hd], c_ref[hd:hd + 1, 0:n], row0, n):
                    acc = acc + jnp.dot(p.astype(BF16), kv[2 + hd, c0:c1, :], preferred_element_type=F32)
            o_ref[row0:row0 + tq, :] = acc.astype(BF16)

    blk = lambda off: pl.BlockSpec((lp, 128), lambda p: (0, off + p))
    return pl.pallas_call(
        body, name=name, grid=(4,),
        in_specs=[blk(0), blk(4), blk(8), pl.BlockSpec((None, 2, lp), lambda p: (p, 0, 0))],
        out_specs=pl.BlockSpec((lp, 128), lambda p: (0, p)),
        out_shape=jax.ShapeDtypeStruct((lp, FOX_WIDTH), BF16),
        scratch_shapes=[pltpu.VMEM((4, lp, 128), BF16)],
    )(qkv, qkv, qkv, c3)


def _attn_bwd(qkv, q_t, dcat, do_t, c3, *, name, after):
    lp = qkv.shape[0]
    tq, blocks = _attn_blocks(lp)
    scale = FOX_HEAD_DIM ** -0.5

    def body(q_ref, k_ref, v_ref, qt_ref, do_ref, dot_ref, c_ref, _, dq_ref, dk_ref, dv_ref, dc_ref,
             dkt_acc, dvt_acc, kv):
        sub = lax.broadcasted_iota(jnp.int32, (128, 1), 0)
        zero = jnp.zeros((), BF16)
        _mask_heads(k_ref, v_ref, kv)
        dkt_acc[...] = jnp.zeros_like(dkt_acc)
        dvt_acc[...] = jnp.zeros_like(dvt_acc)
        dc_ref[...] = jnp.zeros_like(dc_ref)
        for i, (row0, n) in enumerate(blocks):
            rows = slice(row0, row0 + tq)
            q2s = q_ref[rows, :] * ATTN_SCALE
            do2 = do_ref[rows, :].astype(BF16)
            dq_acc = jnp.zeros((tq, 128), F32)
            for hd in range(2):
                sel_t = (sub < 64) if hd == 0 else (sub >= 64)
                qt_h = jnp.where(sel_t, qt_ref[:, rows], zero)
                dot_h = jnp.where(sel_t, dot_ref[:, rows], zero)
                segs = [(p, lax.dot_general(do2, kv[2 + hd, c0:c1, :], (((1,), (1,)), ((), ())),
                                            preferred_element_type=F32), c0, c1)
                        for p, c0, c1 in _attn_probs(q2s, kv.at[hd], c_ref[hd:hd + 1, 0:n], row0, n)]
                delta = sum(jnp.sum(p * dp, axis=1, keepdims=True) for p, dp, _, _ in segs)
                for p, dp, c0, c1 in segs:
                    ds = p * (dp - delta)
                    dsb = ds.astype(BF16)
                    dq_acc = dq_acc + jnp.dot(dsb, kv[hd, c0:c1, :], preferred_element_type=F32)
                    dkt_acc[:, c0:c1] += jnp.dot(qt_h, dsb, preferred_element_type=F32)
                    dvt_acc[:, c0:c1] += jnp.dot(dot_h, p.astype(BF16), preferred_element_type=F32)
                    dc_ref[hd:hd + 1, c0:c1] -= jnp.sum(ds, axis=0, keepdims=True)
            dq_ref[rows, :] = (dq_acc * scale).astype(BF16)
        dk_ref[...] = (dkt_acc[...] * scale).astype(BF16).T
        dv_ref[...] = dvt_acc[...].astype(BF16).T

    blk = lambda off: pl.BlockSpec((lp, 128), lambda p: (0, off + p))
    blk_t = pl.BlockSpec((128, lp), lambda p: (p, 0))
    c_spec = pl.BlockSpec((None, 2, lp), lambda p: (p, 0, 0))
    return pl.pallas_call(
        body, name=name, grid=(4,),
        in_specs=[blk(0), blk(4), blk(8), blk_t, blk(0), blk_t, c_spec, pl.BlockSpec(memory_space=pl.ANY)],
        out_specs=[blk(0), blk(0), blk(0), c_spec],
        out_shape=[jax.ShapeDtypeStruct((lp, FOX_WIDTH), BF16)] * 3 + [jax.ShapeDtypeStruct((4, 2, lp), F32)],
        scratch_shapes=[pltpu.VMEM((128, lp), F32), pltpu.VMEM((128, lp), F32), pltpu.VMEM((4, lp, 128), BF16)],
    )(qkv, qkv, qkv, q_t, dcat, do_t, c3, after)


def _ln_stats(x):
    mu = jnp.mean(x, axis=-1, keepdims=True)
    xc = x - mu
    var = jnp.mean(xc * xc, axis=-1, keepdims=True)
    rstd = lax.rsqrt(var + LN_EPS)
    return xc * rstd, rstd


def _conv_fwd(agf, conv_w, conv_b, ln_g, ln_b, *, name):
    lp = agf.shape[0]
    nch = lp // CHUNK
    c = CONV_CH

    def body(a_ref, g_ref, w_ref, b_ref, lg_ref, lb_ref, u0_ref, u1_ref, u3_ref, u0s):
        u0s[0:HALO, :] = jnp.zeros((HALO, c), F32)

        def glu(ci, _):
            rows = pl.ds(pl.multiple_of(ci * CHUNK, CHUNK), CHUNK)
            u0 = a_ref[rows, :] * _sigmoid(g_ref[rows, :])
            u0_ref[rows, :] = u0
            u0s[pl.ds(pl.multiple_of(ci * CHUNK + HALO, 8), CHUNK), :] = u0
            return 0

        lax.fori_loop(0, nch, glu, 0, unroll=True)

        def conv(ci, _):
            r0 = pl.multiple_of(ci * CHUNK, CHUNK)
            rows = pl.ds(r0, CHUNK)
            for lg in range(c // 128):
                lanes = slice(lg * 128, (lg + 1) * 128)
                win = u0s[pl.ds(r0, CHUNK + HALO), lanes]
                acc = jnp.broadcast_to(b_ref[:, lanes], (CHUNK, 128))
                for k in range(CONV_WIDTH):
                    s = CONV_WIDTH - 1 - k
                    sh = win if s == 0 else pltpu.roll(win, s, 0)
                    acc = acc + w_ref[k:k + 1, lanes] * sh[HALO:HALO + CHUNK, :]
                u1_ref[rows, lanes] = acc
            xhat, _ = _ln_stats(u1_ref[rows, :])
            y = xhat * lg_ref[...] + lb_ref[...]
            u3_ref[rows, :] = (y * _sigmoid(y)).astype(BF16)
            return 0

        lax.fori_loop(0, nch, conv, 0)

    full = lambda shape: pl.BlockSpec(shape, lambda i: (0, 0))
    return pl.pallas_call(
        body, name=name, grid=(1,),
        in_specs=[pl.BlockSpec((lp, c), lambda i: (0, 0)), pl.BlockSpec((lp, c), lambda i: (0, 1)),
                  full((CONV_WIDTH, c)), full((1, c)), full((1, c)), full((1, c))],
        out_specs=[full((lp, c)), full((lp, c)), full((lp, c))],
        out_shape=[jax.ShapeDtypeStruct((lp, c), F32), jax.ShapeDtypeStruct((lp, c), F32),
                   jax.ShapeDtypeStruct((lp, c), BF16)],
        scratch_shapes=[pltpu.VMEM((lp + HALO, c), F32)],
    )(agf, agf, conv_w, conv_b, ln_g, ln_b)


def _conv_bwd(dcat, u0, u1, agf, conv_w, ln_g, ln_b, *, name):
    lp = agf.shape[0]
    nch = lp // CHUNK
    c = CONV_CH
    wlen = CHUNK + HALO

    def body(du3_ref, u0_ref, u1_ref, a_ref, g_ref, w_ref, lg_ref, lb_ref,
             dag_ref, dw_ref, db_ref, dlg_ref, dlb_ref, du1s, dwacc, vacc):
        du1s[lp:lp + HALO, :] = jnp.zeros((HALO, c), F32)
        dwacc[...] = jnp.zeros_like(dwacc)
        vacc[...] = jnp.zeros_like(vacc)

        def ln_bwd(ci, _):
            r0 = pl.multiple_of(ci * CHUNK, CHUNK)
            rows = pl.ds(r0, CHUNK)
            xhat, rstd = _ln_stats(u1_ref[rows, :])
            y = xhat * lg_ref[...] + lb_ref[...]
            sg = _sigmoid(y)
            du2 = du3_ref[rows, :] * (sg * (1.0 + y * (1.0 - sg)))
            vacc[0:8, :] += _fold8(du2 * xhat)
            vacc[8:16, :] += _fold8(du2)
            dxhat = du2 * lg_ref[...]
            du1 = rstd * (dxhat - jnp.mean(dxhat, axis=-1, keepdims=True)
                          - xhat * jnp.mean(dxhat * xhat, axis=-1, keepdims=True))
            vacc[16:24, :] += _fold8(du1)
            du1s[rows, :] = du1
            return 0

        lax.fori_loop(0, nch, ln_bwd, 0, unroll=True)

        def conv_bwd(ci, _):
            r0 = pl.multiple_of(ci * CHUNK, CHUNK)
            rows = pl.ds(r0, CHUNK)
            for lg in range(c // 128):
                lanes = slice(lg * 128, (lg + 1) * 128)
                dwin = du1s[pl.ds(r0, wlen), lanes]
                u0 = u0_ref[rows, lanes]
                acc = jnp.zeros((CHUNK, 128), F32)
                for k in range(CONV_WIDTH):
                    s = CONV_WIDTH - 1 - k
                    d_s = (dwin if s == 0 else pltpu.roll(dwin, wlen - s, 0))[0:CHUNK, :]
                    acc = acc + w_ref[k:k + 1, lanes] * d_s
                    dwacc[8 * k:8 * k + 8, lanes] += _fold8(d_s * u0)
                sg = _sigmoid(g_ref[rows, lanes])
                a = a_ref[rows, lanes]
                dag_ref[rows, lanes] = (acc * sg).astype(BF16)
                dag_ref[rows, slice(c + lg * 128, c + (lg + 1) * 128)] = (acc * a * sg * (1.0 - sg)).astype(BF16)
            return 0

        lax.fori_loop(0, nch, conv_bwd, 0)
        for k in range(CONV_WIDTH):
            dw_ref[k:k + 1, :] = jnp.sum(dwacc[8 * k:8 * k + 8, :], axis=0, keepdims=True)
        dlg_ref[...] = jnp.sum(vacc[0:8, :], axis=0, keepdims=True)
        dlb_ref[...] = jnp.sum(vacc[8:16, :], axis=0, keepdims=True)
        db_ref[...] = jnp.sum(vacc[16:24, :], axis=0, keepdims=True)

    full = lambda shape: pl.BlockSpec(shape, lambda i: (0, 0))
    vec = jax.ShapeDtypeStruct((1, c), F32)
    return pl.pallas_call(
        body, name=name, grid=(1,),
        in_specs=[pl.BlockSpec((lp, c), lambda i: (0, 1)), full((lp, c)), full((lp, c)),
                  pl.BlockSpec((lp, c), lambda i: (0, 0)), pl.BlockSpec((lp, c), lambda i: (0, 1)),
                  full((CONV_WIDTH, c)), full((1, c)), full((1, c))],
        out_specs=[full((lp, 2 * c)), full((CONV_WIDTH, c)), full((1, c)), full((1, c)), full((1, c))],
        out_shape=[jax.ShapeDtypeStruct((lp, 2 * c), BF16), jax.ShapeDtypeStruct((CONV_WIDTH, c), F32), vec, vec, vec],
        scratch_shapes=[pltpu.VMEM((lp + HALO, c), F32), pltpu.VMEM((8 * CONV_WIDTH, c), F32),
                        pltpu.VMEM((24, c), F32)],
    )(dcat, u0, u1, agf, agf, conv_w, ln_g, ln_b)


FFN_TILE = 256
FFN_PAD = 8


def _ffn_conv(xs, w_ref, b_ref, half, r0):
    win = xs[half, pl.ds(r0, CHUNK + FFN_PAD), :]
    acc = jnp.broadcast_to(b_ref[half], (CHUNK, FFN_TILE))
    for k in range(FFN_CONV_WIDTH):
        s = FFN_CONV_WIDTH - 1 - k
        sh = win if s == 0 else pltpu.roll(win, s, 0)
        acc = acc + w_ref[half, k:k + 1, :] * sh[FFN_PAD:FFN_PAD + CHUNK, :]
    return acc


def _ffn_up_act(n, w_up_t, w3, b3, *, name):
    lp = n.shape[0]
    nch, nt = lp // CHUNK, D_FF // FFN_TILE
    nt_dims = (((1,), (1,)), ((), ()))

    parts = 4

    def project(n_ref, wg_ref, wv_ref, u, part=None):
        rows = slice(0, lp) if part is None else slice(part * lp // parts, (part + 1) * lp // parts)
        for half, w_ref in ((0, wg_ref), (1, wv_ref)):
            u[half, FFN_PAD + rows.start:FFN_PAD + rows.stop, :] = lax.dot_general(
                n_ref[rows, :], w_ref[...], nt_dims, preferred_element_type=F32)

    def activate(u, w_ref, b_ref, act_ref, act_t_ref, gv_ref, up_ref, part):
        for ci in range(part * nch // parts, (part + 1) * nch // parts):
            r0 = ci * CHUNK
            rows = slice(r0, r0 + CHUNK)
            gate = _ffn_conv(u, w_ref, b_ref, 0, r0)
            val = _ffn_conv(u, w_ref, b_ref, 1, r0)
            gv_ref[0, rows, :] = gate.astype(BF16)
            gv_ref[1, rows, :] = val.astype(BF16)
            for half in range(2):
                up_ref[half, rows, :] = u[half, FFN_PAD + r0:FFN_PAD + r0 + CHUNK, :].astype(BF16)
            act = (gate * _sigmoid(gate) * val).astype(BF16)
            act_ref[rows, :] = act
            act_t_ref[:, rows] = act.T

    def body(n_ref, wg_ref, wv_ref, wg_next, wv_next, w_ref, b_ref, act_ref, act_t_ref, gv_ref, up_ref, u0, u1):
        j = pl.program_id(0)

        @pl.when(j == 0)
        def _():
            for u in (u0, u1):
                u[:, 0:FFN_PAD, :] = jnp.zeros((2, FFN_PAD, FFN_TILE), F32)
            project(n_ref, wg_ref, wv_ref, u0)

        for parity, (mine, other) in enumerate(((u0, u1), (u1, u0))):
            @pl.when(j % 2 == parity)
            def _(mine=mine, other=other):
                for part in range(parts):
                    project(n_ref, wg_next, wv_next, other, part)
                    activate(mine, w_ref, b_ref, act_ref, act_t_ref, gv_ref, up_ref, part)

    halves = pl.BlockSpec((2, lp, FFN_TILE), lambda j: (0, 0, j))
    rows_of = lambda half, ahead: pl.BlockSpec(
        (FFN_TILE, D_MODEL), lambda j: (half * nt + jnp.minimum(j + ahead, nt - 1), 0))
    return pl.pallas_call(
        body, name=name, grid=(nt,),
        in_specs=[pl.BlockSpec((lp, D_MODEL), lambda j: (0, 0)), rows_of(0, 0), rows_of(1, 0), rows_of(0, 1),
                  rows_of(1, 1), pl.BlockSpec((2, FFN_CONV_WIDTH, FFN_TILE), lambda j: (0, 0, j)),
                  pl.BlockSpec((2, 1, FFN_TILE), lambda j: (0, 0, j))],
        out_specs=[pl.BlockSpec((lp, FFN_TILE), lambda j: (0, j)), pl.BlockSpec((FFN_TILE, lp), lambda j: (j, 0)),
                   halves, halves],
        out_shape=[jax.ShapeDtypeStruct((lp, D_FF), BF16), jax.ShapeDtypeStruct((D_FF, lp), BF16),
                   jax.ShapeDtypeStruct((2, lp, D_FF), BF16), jax.ShapeDtypeStruct((2, lp, D_FF), BF16)],
        scratch_shapes=[pltpu.VMEM((2, lp + FFN_PAD, FFN_TILE), F32), pltpu.VMEM((2, lp + FFN_PAD, FFN_TILE), F32)],
    )(n, w_up_t, w_up_t, w_up_t, w_up_t, w3, b3)


def _ffn_act_bwd(up3, gv3, w3, dact, *, name):
    _, lp, f = up3.shape
    nch = lp // CHUNK
    wlen = CHUNK + FFN_PAD

    def body(up_ref, gv_ref, w_ref, dact_ref, dup_ref, dw_ref, db_ref, ds, wacc):
        for half in range(2):
            ds[half, lp:lp + FFN_PAD, :] = jnp.zeros((FFN_PAD, FFN_TILE), F32)
        wacc[...] = jnp.zeros_like(wacc)

        def act_bwd(ci, _):
            rows = pl.ds(pl.multiple_of(ci * CHUNK, CHUNK), CHUNK)
            gate, val = gv_ref[0, rows, :].astype(F32), gv_ref[1, rows, :].astype(F32)
            sg = _sigmoid(gate)
            da = dact_ref[rows, :].astype(F32)
            ds[0, rows, :] = da * val * (sg * (1.0 + gate * (1.0 - sg)))
            ds[1, rows, :] = da * (gate * sg)
            return 0

        lax.fori_loop(0, nch, act_bwd, 0, unroll=True)

        def conv_bwd(ci, _):
            r0 = pl.multiple_of(ci * CHUNK, CHUNK)
            rows = pl.ds(r0, CHUNK)
            for half in range(2):
                dwin = ds[half, pl.ds(r0, wlen), :]
                x = up_ref[half, rows, :].astype(F32)
                acc = jnp.zeros((CHUNK, FFN_TILE), F32)
                for k in range(FFN_CONV_WIDTH):
                    s = FFN_CONV_WIDTH - 1 - k
                    d_s = (dwin if s == 0 else pltpu.roll(dwin, wlen - s, 0))[0:CHUNK, :]
                    acc = acc + w_ref[half, k:k + 1, :] * d_s
                    wacc[half, 8 * k:8 * k + 8, :] += _fold8(d_s * x)
                wacc[half, 24:32, :] += _fold8(dwin[0:CHUNK, :])
                dup_ref[half, rows, :] = acc.astype(BF16)
            return 0

        lax.fori_loop(0, nch, conv_bwd, 0)
        for half in range(2):
            for k in range(FFN_CONV_WIDTH):
                dw_ref[half, k:k + 1, :] = jnp.sum(wacc[half, 8 * k:8 * k + 8, :], axis=0, keepdims=True)
            db_ref[half] = jnp.sum(wacc[half, 24:32, :], axis=0, keepdims=True)

    halves = pl.BlockSpec((2, lp, FFN_TILE), lambda j: (0, 0, j))
    taps = pl.BlockSpec((2, FFN_CONV_WIDTH, FFN_TILE), lambda j: (0, 0, j))
    bias = pl.BlockSpec((2, 1, FFN_TILE), lambda j: (0, 0, j))
    return pl.pallas_call(
        body, name=name, grid=(f // FFN_TILE,),
        in_specs=[halves, halves, taps, pl.BlockSpec((lp, FFN_TILE), lambda j: (0, j))],
        out_specs=[halves, taps, bias],
        out_shape=[jax.ShapeDtypeStruct((2, lp, f), BF16), jax.ShapeDtypeStruct((2, FFN_CONV_WIDTH, f), F32),
                   jax.ShapeDtypeStruct((2, 1, f), F32)],
        scratch_shapes=[pltpu.VMEM((2, lp + FFN_PAD, FFN_TILE), F32), pltpu.VMEM((2, 32, FFN_TILE), F32)],
    )(up3, gv3, w3, dact)


POOL_PAD = 16


def _inv_count(r0, w):
    t = r0 + lax.broadcasted_iota(jnp.int32, (CHUNK, 1), 0)
    return 1.0 / jnp.minimum(t + 1, w).astype(F32)


def _pool_fwd(n, pool_w, pool_b, pool_scale, h, *, name):
    lp, dm = n.shape
    nch = lp // CHUNK
    g = POOL_GROUP

    def body(n_ref, w_ref, b_ref, s_ref, h_ref, ho_ref, dt_ref, z_ref, xs, d_ref):
        gi = pl.program_id(0)
        xs[0:POOL_PAD, :] = jnp.zeros((POOL_PAD, g), F32)
        xs[POOL_PAD:POOL_PAD + lp, :] = n_ref[...]
        for idx, w in enumerate(POOL_WINDOWS):
            @pl.when(gi == idx)
            def _(w=w):
                def chunk(ci, _):
                    r0 = pl.multiple_of(ci * CHUNK, CHUNK)
                    win = xs[pl.ds(r0, CHUNK + POOL_PAD), :]
                    acc = win
                    for j in range(1, w):
                        acc = acc + pltpu.roll(win, j, 0)
                    x = win[POOL_PAD:POOL_PAD + CHUNK, :]
                    d = acc[POOL_PAD:POOL_PAD + CHUNK, :] * _inv_count(r0, w) - x
                    d_ref[pl.ds(r0, CHUNK), :] = d.astype(BF16)
                    dt_ref[:, pl.ds(r0, CHUNK)] = d.astype(BF16).T
                    return 0

                lax.fori_loop(0, nch, chunk, 0, unroll=True)

        z = jnp.dot(d_ref[...], w_ref[...], preferred_element_type=F32) + b_ref[...]
        z_ref[...] = z
        ho_ref[...] = h_ref[...] + z * s_ref[...]

    col = pl.BlockSpec((lp, g), lambda i: (0, i))
    vec = pl.BlockSpec((1, g), lambda i: (0, i))
    return pl.pallas_call(
        body, name=name, grid=(len(POOL_WINDOWS),),
        in_specs=[col, pl.BlockSpec((None, g, g), lambda i: (i, 0, 0)), vec, vec, col],
        out_specs=[col, pl.BlockSpec((g, lp), lambda i: (i, 0)), col],
        out_shape=[jax.ShapeDtypeStruct((lp, dm), F32), jax.ShapeDtypeStruct((dm, lp), BF16),
                   jax.ShapeDtypeStruct((lp, dm), F32)],
        scratch_shapes=[pltpu.VMEM((lp + POOL_PAD, g), F32), pltpu.VMEM((lp, g), BF16)],
    )(n, pool_w, pool_b, pool_scale, h)


def _pool_bwd(dy, z, pool_w, pool_scale, *, name):
    lp, dm = dy.shape
    nch = lp // CHUNK
    g = POOL_GROUP
    wlen = CHUNK + POOL_PAD

    def body(dy_ref, z_ref, w_ref, s_ref, dn_ref, dz_ref, dsc_ref, db_ref, ys, dd):
        gi = pl.program_id(0)
        dyv = dy_ref[...]
        dsc_ref[...] = jnp.sum(dyv * z_ref[...], axis=0, keepdims=True)
        dz = dyv * s_ref[...]
        db_ref[...] = jnp.sum(dz, axis=0, keepdims=True)
        dzb = dz.astype(BF16)
        dz_ref[...] = dzb
        dd[...] = lax.dot_general(dzb, w_ref[...], (((1,), (1,)), ((), ())), preferred_element_type=F32)
        ys[lp:lp + POOL_PAD, :] = jnp.zeros((POOL_PAD, g), F32)
        for idx, w in enumerate(POOL_WINDOWS):
            @pl.when(gi == idx)
            def _(w=w):
                def scale(ci, _):
                    r0 = pl.multiple_of(ci * CHUNK, CHUNK)
                    ys[pl.ds(r0, CHUNK), :] = dd[pl.ds(r0, CHUNK), :] * _inv_count(r0, w)
                    return 0

                lax.fori_loop(0, nch, scale, 0, unroll=True)

                def chunk(ci, _):
                    r0 = pl.multiple_of(ci * CHUNK, CHUNK)
                    win = ys[pl.ds(r0, wlen), :]
                    acc = win
                    for j in range(1, w):
                        acc = acc + pltpu.roll(win, wlen - j, 0)
                    dn_ref[pl.ds(r0, CHUNK), :] = acc[0:CHUNK, :] - dd[pl.ds(r0, CHUNK), :]
                    return 0

                lax.fori_loop(0, nch, chunk, 0, unroll=True)

    col = pl.BlockSpec((lp, g), lambda i: (0, i))
    vec = pl.BlockSpec((1, g), lambda i: (0, i))
    return pl.pallas_call(
        body, name=name, grid=(len(POOL_WINDOWS),),
        in_specs=[col, col, pl.BlockSpec((None, g, g), lambda i: (i, 0, 0)), vec],
        out_specs=[col, col, vec, vec],
        out_shape=[jax.ShapeDtypeStruct((lp, dm), F32), jax.ShapeDtypeStruct((lp, dm), BF16),
                   jax.ShapeDtypeStruct((1, dm), F32), jax.ShapeDtypeStruct((1, dm), F32)],
        scratch_shapes=[pltpu.VMEM((lp + POOL_PAD, g), F32), pltpu.VMEM((lp, g), F32)],
    )(dy, z, pool_w, pool_scale)


def _ffn_fwd(h, g, weight, w3, b3, tag):
    n, n_t = _rms_fwd(h, g, name=f"rms_ffn{tag}", out_dtype=BF16, transposed=True)
    w_up_t = weight("up", n)
    act, act_t, gv3, up3 = _ffn_up_act(n, w_up_t, w3, b3, name=f"ffn_up_act{tag}")
    w_down = weight("down", act)
    h_out = _mm(act, w_down, name=f"mm_down{tag}", tn=256, res=h)
    return h_out, (n_t, up3, gv3, act_t), w_up_t, w_down


def _ffn_bwd(dh, dhb, h, g, saved, w_up_t, w3, w_down, tag, after=None):
    lp = h.shape[0]
    n_t, up3, gv3, act_t = saved
    dw_down = _mm(act_t, dhb, name=f"mm_dwdown{tag}", tm=704, out_dtype=BF16)
    dact = _mm(dhb, w_down, name=f"mm_dact{tag}", tb=True, tn=D_FF // 2, out_dtype=BF16, after=after)
    dup3, dcw, dcb = _ffn_act_bwd(up3, gv3, w3, dact, name=f"ffn_act_bwd{tag}")
    dup2 = dup3.reshape(2 * lp, D_FF)
    dw_up = _mm_dw_up(n_t, dup2, name=f"mm_dwup{tag}")
    dh_in, dh_in_b, dg = _mm_rms_bwd(dup2, w_up_t, h, g, dh, name=f"mm_dnffn{tag}", tk=D_FF // 2,
                                     a_map=lambda i, kk: (4 * (kk // 2) + i, kk % 2))
    return dh_in, dh_in_b, (dg, dw_up, dcw, dcb, dw_down)


def _local_step(x, tgt, wt):
    seq = x.shape[0]
    n_real = N_META + seq
    lp = _round_up(n_real, CHUNK)
    pad = jnp.zeros((lp - n_real, D_MODEL), F32)
    h0 = jnp.concatenate([wt["meta"], x, pad], axis=0)
    tgt_p = jnp.concatenate([jnp.zeros((N_META, D_MODEL), F32), tgt, pad], axis=0)
    w_in_p = wt["w_in_p"]

    n0, n0_t = _rms_fwd(h0, wt["g_even"], name="rms_even", out_dtype=BF16, after=wt["ffn_started"], transposed=True)
    qkv, q_t = _mm(n0, w_in_p, name="mm_qkv", tn=FOX_WIDTH, dims=(lp, 3 * FOX_WIDTH, D_MODEL), out_dtype=BF16,
                   first_tile_t=True)
    ag = _mm(n0, w_in_p, name="mm_ag", tn=512, dims=(lp, 2 * CONV_CH, D_MODEL),
             b_map=lambda i, j, k: (0, 3 + j))
    f_t = _mm(wt["wf_t"], n0, name="mm_ft", tb=True)
    c_row = _fox_prep(f_t, wt["b_f"], name="fox_prep")
    c3 = c_row.reshape(4, 2, lp)
    o = _attn_fwd(qkv, c3, name="attn_fwd")
    u0, u1, u3 = _conv_fwd(ag, wt["conv_w"], wt["conv_b"], wt["ln_g"], wt["ln_b"], name="conv_fwd")
    cat = jnp.concatenate([o, u3], axis=1)
    h1 = _mm(cat, wt["w_out"], name="mm_out", tn=512, res=h0)
    h2, saved0, w_up0, w_down0 = _ffn_fwd(h1, wt["ffn_norm"][0:1], functools.partial(wt["ffn_weight"], 0),
                                          wt["fcw3"][0], wt["fcb3"][0], 0)

    n2 = _rms_fwd(h2, wt["g_odd"], name="rms_odd", out_dtype=F32)
    h3, dpool_t, z = _pool_fwd(n2, wt["pool_w"], wt["pool_b"], wt["pool_scale"], h2, name="pool_fwd")
    h4, saved1, w_up1, w_down1 = _ffn_fwd(h3, wt["ffn_norm"][1:2], functools.partial(wt["ffn_weight"], 1),
                                          wt["fcw3"][1], wt["fcb3"][1], 1)

    loss, dh4, dh4b, d_gfinal = _loss_head(h4, wt["g_final"], tgt_p, n_real, name="loss_head")

    dh3, dh3b, gf1 = _ffn_bwd(dh4, dh4b, h3, wt["ffn_norm"][1:2], saved1, w_up1, wt["fcw3"][1], w_down1, 1)
    send1, token1 = _send_ffn_grads(gf1[1], gf1[4], 1)
    dn2, dzb, d_pscale, d_pb = _pool_bwd(dh3, z, wt["pool_w"], wt["pool_scale"], name="pool_bwd")
    d_pw = _mm(dpool_t, dzb, name="mm_dpoolw", tm=POOL_GROUP, tn=POOL_GROUP, dims=(D_MODEL, POOL_GROUP, lp),
               b_map=lambda i, j, k: (0, i), o_map=lambda i, j, k: (i, 0), out_shape=(D_MODEL, POOL_GROUP),
               out_dtype=BF16)
    dh2, dh2b, d_godd = _rms_bwd(h2, wt["g_odd"], dn2, dh3, name="rms_bwd_odd")
    dh1, dh1b, gf0 = _ffn_bwd(dh2, dh2b, h1, wt["ffn_norm"][0:1], saved0, w_up0, wt["fcw3"][0], w_down0, 0,
                              after=token1)

    send0, token0 = _send_ffn_grads(gf0[1], gf0[4], 0)
    cat_t = _transpose(cat, name="t_cat", out_dtype=BF16)
    d_wout = _mm(cat_t, dh1b, name="mm_dwout", tm=512, out_dtype=BF16)
    dcat, do_t = _mm(dh1b, wt["w_out"], name="mm_dcat", tb=True, tn=FOX_WIDTH, after=token0, first_tile_t=True)
    dag, d_convw, d_convb, d_lng, d_lnb = _conv_bwd(dcat, u0, u1, ag, wt["conv_w"], wt["ln_g"], wt["ln_b"],
                                                    name="conv_bwd")
    layers = lambda i: jnp.stack([gf0[i], gf1[i]])
    grads = dict(
        conv_w=d_convw[None], w_out=d_wout, mix_norm_odd=d_godd,
        pool_w=d_pw.reshape(len(POOL_WINDOWS), POOL_GROUP, POOL_GROUP),
        pool_b=d_pb.reshape(1, len(POOL_WINDOWS), POOL_GROUP), pool_scale=d_pscale, w_up=(gf0[1], gf1[1]),
        ffn_conv_w=layers(2).transpose(0, 2, 1, 3).reshape(DEPTH, FFN_CONV_WIDTH, 2 * D_FF), w_down=(gf0[4], gf1[4]))
    send_rest, token_rest = _send_rest_grads(grads)
    dq, dk, dv, dc3 = _attn_bwd(qkv, q_t, dcat, do_t, c3, name="attn_bwd", after=token_rest)
    df_t, d_bf = _fox_bwd(dc3.reshape(FOX_HEADS, lp), f_t, wt["b_f"], name="fox_bwd")
    df = _transpose(df_t, name="t_df", out_dtype=BF16)
    dproj = jnp.concatenate([dq, dk, dv, dag, df], axis=1)
    grads["w_in"] = _mm_dw_in(n0_t, dproj, name="mm_dwin")
    send_in, token_in = _send_start(
        [grads["w_in"]], [jax.ShapeDtypeStruct((N_DEV - 1, D_MODEL, _IN_SHARD), BF16)], [(0, _by_owner, 0, None)],
        name="send_w_in")
    dh0, _, d_geven = _mm_rms_bwd(dproj, w_in_p, h0, wt["g_even"], dh1, name="mm_dn0", tb=True, tk=896,
                                  after=token_in)
    grads.update(
        meta_tokens=dh0[0:N_META], mix_norm_even=d_geven, b_f=d_bf.reshape(1, FOX_HEADS), conv_b=d_convb, ln_g=d_lng,
        ln_b=d_lnb, ffn_norm=jnp.concatenate([gf0[0], gf1[0]], axis=0),
        ffn_conv_b=layers(3).reshape(DEPTH, 2 * D_FF), final_norm=d_gfinal.reshape(D_MODEL),
        sends=(send0, send1, send_rest, send_in))
    return loss, dh0[N_META:n_real], grads


_LITTLE = (("conv_w", (1, 31, 512), 2), ("mix_norm_odd", (1, 1024), 1), ("pool_b", (1, 4, 256), 2),
           ("pool_scale", (1, 1024), 1), ("ffn_conv_w", (2, 3, 5632), 2))


def _send_rest_grads(g):
    out_rows, pool_rows, groups = D_MODEL // N_DEV, POOL_GROUP // N_DEV, len(POOL_WINDOWS)
    little_slabs = _pack([_full_to_slabs(g[n], s, a) for n, s, a in _LITTLE], F32, lead=(N_DEV,), align=8)
    land = lambda shape, dtype: jax.ShapeDtypeStruct((N_DEV - 1,) + shape, dtype)
    handle, token = _send_start(
        [g["w_out"], g["pool_w"], little_slabs],
        [land((out_rows, D_MODEL), BF16), land((groups, pool_rows, POOL_GROUP), BF16), land(little_slabs.shape[1:], F32)],
        [(0, _row_block(out_rows), 0, None), (1, _row_block(pool_rows, axis=1), 1, None), (2, _by_owner, 2, None)],
        name="send_rest")
    return handle, token


def _send_ffn_grads(dw_up, dw_down, tag):
    rows = D_FF // N_DEV
    lands = [jax.ShapeDtypeStruct((N_DEV - 1,) + dw_up.shape[1:], BF16),
             jax.ShapeDtypeStruct((N_DEV - 1, rows, D_MODEL), BF16)]
    return _send_start([dw_up, dw_down], lands, [(0, _by_owner, 0, None), (1, _row_block(rows), 1, None)],
                       name=f"send_ffn{tag}")


_QKV = 3 * FOX_WIDTH
_GLU0 = _QKV + FOX_HEADS
_IN_COLS = _GLU0 + 2 * CONV_CH
_F_PAD = 128


_IN_SHARD = _IN_COLS // N_DEV
_UP_SHARD = 2 * D_FF // N_DEV
_ROW_TILE = 256


def _assemble_w_in(st, *, name):
    tr = _ROW_TILE

    def body(s_ref, o_ref):
        full = jnp.concatenate([s_ref[i].astype(F32) for i in range(N_DEV)], axis=1)
        parts = [full[:, :_QKV], full[:, _GLU0:], full[:, _QKV:_GLU0], jnp.zeros((tr, _F_PAD - FOX_HEADS), F32)]
        o_ref[...] = jnp.concatenate(parts, axis=1).astype(BF16)

    return pl.pallas_call(
        body, name=name, grid=(D_MODEL // tr,),
        in_specs=[pl.BlockSpec((N_DEV, tr, _IN_SHARD), lambda i: (0, i, 0))],
        out_specs=pl.BlockSpec((tr, _QKV + 2 * CONV_CH + _F_PAD), lambda i: (i, 0)),
        out_shape=jax.ShapeDtypeStruct((D_MODEL, _QKV + 2 * CONV_CH + _F_PAD), BF16),
    )(st)


def _mm_dw_in(n_t, dproj, *, name):
    dm, lp = n_t.shape
    tr = _ROW_TILE
    ag0 = _QKV + 2 * CONV_CH

    def body(a_ref, b_ref, o_ref):
        r = jnp.dot(a_ref[...], b_ref[...], preferred_element_type=F32)
        full = jnp.concatenate([r[:, :_QKV], r[:, ag0:ag0 + FOX_HEADS], r[:, _QKV:ag0]], axis=1)
        for i in range(N_DEV):
            o_ref[i] = full[:, i * _IN_SHARD:(i + 1) * _IN_SHARD].astype(BF16)

    return pl.pallas_call(
        body, name=name, grid=(dm // tr,),
        in_specs=[pl.BlockSpec((tr, lp), lambda i: (i, 0)), pl.BlockSpec(dproj.shape, lambda i: (0, 0))],
        out_specs=pl.BlockSpec((N_DEV, tr, _IN_SHARD), lambda i: (0, i, 0)),
        out_shape=jax.ShapeDtypeStruct((N_DEV, dm, _IN_SHARD), BF16),
    )(n_t, dproj)


def _mm_dw_up(n_t, dup2, *, name):
    dm, lp = n_t.shape
    pairs_per_half = D_FF // (2 * _UP_SHARD)

    def body(a_ref, b_ref, o_ref):
        r_t = jnp.dot(a_ref[...], b_ref[...], preferred_element_type=F32).astype(BF16).T
        o_ref[0] = r_t[:_UP_SHARD, :]
        o_ref[1] = r_t[_UP_SHARD:, :]

    return pl.pallas_call(
        body, name=name, grid=(N_DEV // 2,),
        in_specs=[pl.BlockSpec((dm, lp), lambda p: (0, 0)),
                  pl.BlockSpec((lp, 2 * _UP_SHARD), lambda p: (p // pairs_per_half, p % pairs_per_half))],
        out_specs=pl.BlockSpec((2, _UP_SHARD, dm), lambda p: (p, 0, 0)),
        out_shape=jax.ShapeDtypeStruct((N_DEV, _UP_SHARD, dm), BF16),
    )(n_t, dup2)


MESH = pl.DeviceIdType.MESH
ANY = pl.BlockSpec(memory_space=pl.ANY)


def _slot(px, py, pc):
    return 4 * px + 2 * py + pc


def _by_owner(ref, slot):
    return ref.at[slot]


def _row_block(rows, axis=0):
    def place(ref, slot):
        idx = (slice(None),) * axis + (pl.ds(slot * rows, rows),)
        return ref.at[idx]
    return place


def _all_gather(arrs, out_shapes, places, *, name):
    n = len(arrs)

    def body(*refs):
        ins, outs = refs[:n], refs[n:2 * n]
        send_sems, recv_sems, local_sems = refs[2 * n:]
        x, y, c = lax.axis_index("x"), lax.axis_index("y"), lax.axis_index("c")
        me, sibling = (x, y, c), (x, y, 1 - c)
        chips = [(1 - x, y), (x, 1 - y), (1 - x, 1 - y)]

        def copy(a, k, block, to, from_input=False):
            dst = places[a](outs[a], _slot(*block))
            return pltpu.make_async_remote_copy(
                src_ref=ins[a] if from_input else dst, dst_ref=dst,
                send_sem=send_sems.at[7 * a + k], recv_sem=recv_sems.at[7 * a + k],
                device_id=to, device_id_type=MESH)

        own, sent = [], []
        for a in range(n):
            mine = pltpu.make_async_copy(ins[a], places[a](outs[a], _slot(*me)), local_sems.at[a])
            mine.start()
            own.append(mine)
            first = [copy(a, 0, me, sibling, True)]
            first += [copy(a, 1 + j, me, (*chip, c), True) for j, chip in enumerate(chips)]
            for cp in first:
                cp.start()
            sent += first
        for a in range(n):
            for j, chip in enumerate(chips):
                copy(a, 1 + j, (*chip, c), me).wait_recv()
                passed = copy(a, 4 + j, (*chip, c), sibling)
                passed.start()
                sent.append(passed)
        for a in range(n):
            copy(a, 0, sibling, me).wait_recv()
            for j, chip in enumerate(chips):
                copy(a, 4 + j, (*chip, 1 - c), me).wait_recv()
        for cp in sent:
            cp.wait_send()
        for cp in own:
            cp.wait()

    return pl.pallas_call(
        body, name=name,
        in_specs=[ANY] * n, out_specs=[ANY] * n,
        out_shape=[jax.ShapeDtypeStruct(s, a.dtype) for s, a in zip(out_shapes, arrs)],
        scratch_shapes=[pltpu.SemaphoreType.DMA((7 * n,)), pltpu.SemaphoreType.DMA((7 * n,)),
                        pltpu.SemaphoreType.DMA((n,))],
    )(*arrs)


HBM = pl.BlockSpec(memory_space=pltpu.HBM)
SEM = pl.BlockSpec(memory_space=pltpu.SEMAPHORE)
EFFECT = pltpu.SideEffectType.DATAFLOW_SIDE_EFFECTING


def _relation_copies(src_refs, land_refs, copies, send_sems, recv_sems):
    x, y, c = lax.axis_index("x"), lax.axis_index("y"), lax.axis_index("c")
    flip = lambda v, bit: 1 - v if bit else v
    out = []
    for k in range(1, N_DEV):
        p = (flip(x, k & 4), flip(y, k & 2), flip(c, k & 1))
        for j, (si, take, li, put) in enumerate(copies):
            sem = (k - 1) * len(copies) + j
            dst = land_refs[li].at[k - 1] if put is None else put(land_refs[li], _slot(x, y, c))
            out.append(pltpu.make_async_remote_copy(
                src_ref=take(src_refs[si], _slot(*p)), dst_ref=dst,
                send_sem=send_sems.at[sem], recv_sem=recv_sems.at[sem], device_id=p, device_id_type=MESH))
    return out


def _own_copies(src_refs, land_refs, copies, sems):
    me = _slot(lax.axis_index("x"), lax.axis_index("y"), lax.axis_index("c"))
    placed = [(si, take, li, put) for si, take, li, put in copies if put is not None]
    return [pltpu.make_async_copy(take(src_refs[si], me), put(land_refs[li], me),
                                  sems.at[(N_DEV - 1) * len(copies) + j])
            for j, (si, take, li, put) in enumerate(placed)]


def _send_start(srcs, land_structs, copies, *, name, after=None):
    ns, nl = len(srcs), len(land_structs)
    n_sem = (N_DEV - 1) * len(copies) + sum(put is not None for _, _, _, put in copies)
    behind = [] if after is None else [after]

    def body(*refs):
        first_out = ns + nl + len(behind)
        send_sems, recv_sems, token = refs[first_out], refs[first_out + 1], refs[-1]
        for cp in _relation_copies(refs[:ns], refs[ns:ns + nl], copies, send_sems, recv_sems):
            cp.start()
        for cp in _own_copies(refs[:ns], refs[ns:ns + nl], copies, send_sems):
            cp.start()
        token[...] = jnp.zeros_like(token)

    in_hbm = lambda a: pltpu.with_memory_space_constraint(a, pltpu.HBM)
    outs = pl.pallas_call(
        body, name=name,
        out_shape=(pltpu.SemaphoreType.DMA((n_sem,)), pltpu.SemaphoreType.DMA((n_sem,)),
                   *[pltpu.HBM(s.shape, s.dtype) for s in srcs],
                   *[pltpu.HBM(s.shape, s.dtype) for s in land_structs],
                   jax.ShapeDtypeStruct((8, 128), F32)),
        in_specs=(HBM,) * (ns + nl) + (ANY,) * len(behind),
        out_specs=(SEM, SEM) + (HBM,) * (ns + nl) + (pl.BlockSpec(memory_space=pltpu.VMEM),),
        input_output_aliases={i: 2 + i for i in range(ns + nl)},
        compiler_params=pltpu.CompilerParams(has_side_effects=EFFECT),
    )(*[in_hbm(s) for s in srcs], *[in_hbm(lax.empty(s.shape, s.dtype)) for s in land_structs], *behind)
    return (outs[0], outs[1], outs[2:2 + ns], outs[2 + ns:2 + ns + nl], copies), outs[-1]


def _send_wait(handle, after, *, name):
    send_sems, recv_sems, srcs, lands, copies = handle
    ns, nl = len(srcs), len(lands)

    def body(*refs):
        for cp in _relation_copies(refs[:ns], refs[ns:ns + nl], copies, refs[ns + nl], refs[ns + nl + 1]):
            cp.wait_send()
            cp.wait_recv()
        for cp in _own_copies(refs[:ns], refs[ns:ns + nl], copies, refs[ns + nl]):
            cp.wait()

    outs = pl.pallas_call(
        body, name=name,
        out_shape=tuple(pltpu.HBM(a.shape, a.dtype) for a in (*srcs, *lands)),
        in_specs=(HBM,) * (ns + nl) + (SEM, SEM, ANY), out_specs=(HBM,) * (ns + nl),
        input_output_aliases={i: i for i in range(ns + nl)},
        compiler_params=pltpu.CompilerParams(has_side_effects=EFFECT),
    )(*srcs, *lands, send_sems, recv_sems, after)
    return outs[:ns], outs[ns:]


def _sum_slabs(stack, *, name, own=None):
    n, rows, w = stack.shape

    def body(*refs):
        s_ref, o_ref = refs[-2], refs[-1]
        acc = s_ref[0] if own is None else refs[0][...] + s_ref[0]
        for i in range(1, n):
            acc = acc + s_ref[i]
        o_ref[...] = acc

    return pl.pallas_call(body, name=name, out_shape=jax.ShapeDtypeStruct((rows, w), F32))(
        *([] if own is None else [own]), stack)


def _adam_math(w, g, m, v):
    mn = ADAM_B1 * m + (1.0 - ADAM_B1) * g
    vn = ADAM_B2 * v + (1.0 - ADAM_B2) * (g * g)
    m_hat = mn / (1.0 - ADAM_B1 ** ADAM_STEP)
    v_hat = vn / (1.0 - ADAM_B2 ** ADAM_STEP)
    return -ADAM_LR * (m_hat / (jnp.sqrt(v_hat) + ADAM_EPS) + ADAM_WD * w), mn, vn


def _adamw_many(ws, gs, ms, vs, *, name):
    n = len(ws)

    def body(*refs):
        for i in range(n):
            w_ref, g_ref, m_ref, v_ref = (refs[j * n + i] for j in range(4))
            d_ref, mo_ref, vo_ref = refs[4 * n + 3 * i:4 * n + 3 * i + 3]
            d_ref[...], mo_ref[...], vo_ref[...] = _adam_math(w_ref[...], g_ref[...], m_ref[...], v_ref[...])

    return pl.pallas_call(
        body, name=name, out_shape=[jax.ShapeDtypeStruct(w.shape, F32) for w in ws for _ in range(3)],
    )(*ws, *gs, *ms, *vs)


def _adamw_layers(w, owns, lands, m, v, tr, *, name):
    nl, rows, cols = w.shape
    steps = rows // tr
    assert rows % tr == 0

    def body(*refs):
        w_ref, m_ref, v_ref = refs[:3]
        own_refs, land_refs = refs[3:3 + nl], refs[3 + nl:3 + 2 * nl]
        g_ref, d_ref, mo_ref, vo_ref = refs[3 + 2 * nl:]
        for li in range(nl):
            @pl.when(pl.program_id(0) == li)
            def _(li=li):
                g = own_refs[li][...].astype(F32)
                for k in range(N_DEV - 1):
                    g = g + land_refs[li][k].astype(F32)
                g_ref[...] = g
                d_ref[...], mo_ref[...], vo_ref[...] = _adam_math(w_ref[...], g, m_ref[...], v_ref[...])

    def held(li):
        return lambda l, i: jnp.where(l == li, i, jnp.where(l < li, 0, steps - 1))

    blk = pl.BlockSpec((None, tr, cols), lambda l, i: (l, i, 0))
    own_specs = [pl.BlockSpec((tr, cols), lambda l, i, f=held(li): (f(l, i), 0)) for li in range(nl)]
    land_specs = [pl.BlockSpec((N_DEV - 1, tr, cols), lambda l, i, f=held(li): (0, f(l, i), 0)) for li in range(nl)]
    return pl.pallas_call(
        body, name=name, grid=(nl, steps),
        in_specs=[blk, blk, blk] + own_specs + land_specs, out_specs=[blk] * 4,
        out_shape=[jax.ShapeDtypeStruct(w.shape, F32)] * 4,
    )(w, m, v, *owns, *lands)


_WEIGHTS = (
    ("meta_tokens", (16, 1024), 1), ("mix_norm_even", (1, 1024), None), ("w_in", (1, 1024, 2568), 2),
    ("b_f", (1, 8), None), ("conv_w", (1, 31, 512), 2), ("conv_b", (1, 512), None), ("ln_g", (1, 512), None),
    ("ln_b", (1, 512), None), ("w_out", (1, 1024, 1024), 1), ("mix_norm_odd", (1, 1024), 1),
    ("pool_w", (1, 4, 256, 256), 2), ("pool_b", (1, 4, 256), 2), ("pool_scale", (1, 1024), 1),
    ("ffn_norm", (2, 1024), None), ("w_up", (2, 1024, 5632), 2), ("ffn_conv_w", (2, 3, 5632), 2),
    ("ffn_conv_b", (2, 5632), None), ("w_down", (2, 2816, 1024), 1), ("final_norm", (1024,), None),
)
_MATMUL_WEIGHTS = ("w_in", "w_out", "pool_w", "w_up", "w_down")
_ADAM_ROWS = dict(w_in=256, w_out=128, pool_w=128, w_up=352, w_down=352)


def _shard_shape(shape, axis):
    return shape[:axis] + (shape[axis] // N_DEV,) + shape[axis + 1:]


def _size(shape):
    n = 1
    for s in shape:
        n *= s
    return n


def _pack(parts, dtype, lead=(), align=16):
    flat = jnp.concatenate([p.reshape(lead + (-1,)).astype(dtype) for p in parts], axis=-1)
    n = flat.shape[-1]
    rows = _round_up(-(-n // FLAT_W), align)
    flat = jnp.pad(flat, [(0, 0)] * len(lead) + [(0, rows * FLAT_W - n)])
    return flat.reshape(lead + (rows, FLAT_W))


def _unpack(buf, shapes, lead=()):
    flat = buf.reshape(lead + (-1,))
    out, off = [], 0
    for shp in shapes:
        n = _size(shp)
        out.append(flat[..., off:off + n].reshape(lead + shp))
        off += n
    return out


def _gathered_to_full(stack, shape, axis):
    return jnp.moveaxis(stack, 0, axis).reshape(shape)


def _full_to_slabs(full, shape, axis):
    split = shape[:axis] + (N_DEV, shape[axis] // N_DEV) + shape[axis + 1:]
    return jnp.moveaxis(full.reshape(split), axis, 0)


def kernel(x, meta_tokens, mix_norm_even, w_in, b_f, conv_w, conv_b, ln_g, ln_b, w_out, mix_norm_odd, pool_w, pool_b, pool_scale, ffn_norm, w_up, ffn_conv_w, ffn_conv_b, w_down, final_norm, loss_target, m_meta_tokens, m_mix_norm_even, m_w_in, m_b_f, m_conv_w, m_conv_b, m_ln_g, m_ln_b, m_w_out, m_mix_norm_odd, m_pool_w, m_pool_b, m_pool_scale, m_ffn_norm, m_w_up, m_ffn_conv_w, m_ffn_conv_b, m_w_down, m_final_norm, v_meta_tokens, v_mix_norm_even, v_w_in, v_b_f, v_conv_w, v_conv_b, v_ln_g, v_ln_b, v_w_out, v_mix_norm_odd, v_pool_w, v_pool_b, v_pool_scale, v_ffn_norm, v_w_up, v_ffn_conv_w, v_ffn_conv_b, v_w_down, v_final_norm):
    names = [n for n, _, _ in _WEIGHTS]
    w_loc = dict(zip(names, (meta_tokens, mix_norm_even, w_in, b_f, conv_w, conv_b, ln_g, ln_b, w_out, mix_norm_odd,
                             pool_w, pool_b, pool_scale, ffn_norm, w_up, ffn_conv_w, ffn_conv_b, w_down, final_norm)))
    m_loc = dict(zip(names, (m_meta_tokens, m_mix_norm_even, m_w_in, m_b_f, m_conv_w, m_conv_b, m_ln_g, m_ln_b,
                             m_w_out, m_mix_norm_odd, m_pool_w, m_pool_b, m_pool_scale, m_ffn_norm, m_w_up,
                             m_ffn_conv_w, m_ffn_conv_b, m_w_down, m_final_norm)))
    v_loc = dict(zip(names, (v_meta_tokens, v_mix_norm_even, v_w_in, v_b_f, v_conv_w, v_conv_b, v_ln_g, v_ln_b,
                             v_w_out, v_mix_norm_odd, v_pool_w, v_pool_b, v_pool_scale, v_ffn_norm, v_w_up,
                             v_ffn_conv_w, v_ffn_conv_b, v_w_down, v_final_norm)))
    replicated = [(n, s) for n, s, a in _WEIGHTS if a is None]
    little = [(n, s, a) for n, s, a in _WEIGHTS if a is not None and n not in _MATMUL_WEIGHTS]
    little_shards = [_shard_shape(s, a) for _, s, a in little]
    out_rows, down_rows, pool_rows = D_MODEL // N_DEV, D_FF // N_DEV, POOL_GROUP // N_DEV
    n_groups = len(POOL_WINDOWS)

    little_pack = _pack([w_loc[n] for n, _, _ in little], F32)
    g_win, g_wout, g_poolw, g_little = _all_gather(
        [w_in[0].astype(BF16), w_out[0].astype(BF16), pool_w[0].astype(BF16), little_pack],
        [(N_DEV, D_MODEL, _IN_SHARD), (D_MODEL, D_MODEL), (n_groups, POOL_GROUP, POOL_GROUP),
         (N_DEV,) + little_pack.shape],
        [_by_owner, _row_block(out_rows), _row_block(pool_rows, axis=1), _by_owner],
        name="gather_weights")
    me = _slot(lax.axis_index("x"), lax.axis_index("y"), lax.axis_index("c"))
    up_t = lambda a: jnp.transpose(a, (0, 2, 1))
    w_loc["w_up"], m_loc["w_up"], v_loc["w_up"] = up_t(w_up), up_t(m_w_up), up_t(v_w_up)
    w_up_b, w_down_b = w_loc["w_up"].astype(BF16), w_down.astype(BF16)
    whole = lambda ref, slot: ref
    ffn_gathers, behind = {}, g_little
    for l in range(DEPTH):
        for part, shard, rows in (("up", w_up_b[l], _UP_SHARD), ("down", w_down_b[l], down_rows)):
            ffn_gathers[l, part], behind = _send_start(
                [shard], [jax.ShapeDtypeStruct((N_DEV * rows, D_MODEL), BF16)], [(0, whole, 0, _row_block(rows))],
                name=f"gather_{part}{l}_start", after=behind)

    def ffn_weight(l, part, after):
        return _send_wait(ffn_gathers[l, part], after, name=f"gather_{part}{l}_wait")[1][0]

    w_in_p = _assemble_w_in(g_win, name="assemble_w_in")
    full = {n: _gathered_to_full(st, s, a)
            for (n, s, a), st in zip(little, _unpack(g_little, little_shards, lead=(N_DEV,)))}
    f0 = _QKV + 2 * CONV_CH
    wt = dict(
        meta=full["meta_tokens"], g_even=mix_norm_even, w_in_p=w_in_p, wf_t=w_in_p[:, f0:f0 + FOX_HEADS].T,
        b_f=b_f.reshape(FOX_HEADS, 1), conv_w=full["conv_w"][0], conv_b=conv_b, ln_g=ln_g, ln_b=ln_b, w_out=g_wout,
        g_odd=full["mix_norm_odd"], pool_w=g_poolw, pool_b=full["pool_b"].reshape(1, D_MODEL),
        pool_scale=full["pool_scale"], ffn_norm=ffn_norm, ffn_weight=ffn_weight, ffn_started=behind,
        fcw3=full["ffn_conv_w"].reshape(DEPTH, FFN_CONV_WIDTH, 2, D_FF).transpose(0, 2, 1, 3),
        fcb3=ffn_conv_b.reshape(DEPTH, 2, 1, D_FF), g_final=final_norm.reshape(1, D_MODEL))

    loss_part, grad_x, g = _local_step(x[0], loss_target[0], wt)

    small = _pack([loss_part[:, 0:1]] + [g[n] for n, _ in replicated] + [g["meta_tokens"]], F32, align=8)
    send_small, token_small = _send_start([small], [jax.ShapeDtypeStruct((N_DEV,) + small.shape, F32)],
                                          [(0, whole, 0, _by_owner)], name="send_small")

    grads, delta, new_m, new_v = {}, {}, {}, {}
    send0, send1, send_rest, send_in = g["sends"]
    ffn_sent = [_send_wait(send, token_small, name=f"wait_ffn{l}") for l, send in enumerate((send0, send1))]
    own_up = [lax.dynamic_index_in_dim(srcs[0], me, 0, keepdims=False) for srcs, _ in ffn_sent]
    own_down = [lax.dynamic_slice_in_dim(srcs[1], me * down_rows, down_rows, 0) for srcs, _ in ffn_sent]
    for n, owns, idx in (("w_up", own_up, 0), ("w_down", own_down, 1)):
        grads[n], delta[n], new_m[n], new_v[n] = _adamw_layers(
            w_loc[n], owns, [lands[idx] for _, lands in ffn_sent], m_loc[n], v_loc[n], _ADAM_ROWS[n],
            name=f"adamw_{n}")
    for d in (grads, delta, new_m, new_v):
        d["w_up"] = up_t(d["w_up"])
    (d_out, d_pool, little_slabs), (land_out, land_pool, land_little) = _send_wait(
        send_rest, delta["w_down"], name="wait_rest")
    (d_in,), (land_in,) = _send_wait(send_in, land_out, name="wait_w_in")
    pool_2d = (n_groups * pool_rows, POOL_GROUP)
    own_pool = lax.dynamic_slice_in_dim(d_pool, me * pool_rows, pool_rows, 1)
    for n, own, land, shp in (
            ("w_in", lax.dynamic_index_in_dim(d_in, me, 0, keepdims=False), land_in, w_in.shape),
            ("w_out", lax.dynamic_slice_in_dim(d_out, me * out_rows, out_rows, 0), land_out, w_out.shape),
            ("pool_w", own_pool.reshape(pool_2d), land_pool.reshape((N_DEV - 1,) + pool_2d), (1,) + pool_2d)):
        outs = _adamw_layers(w_loc[n].reshape(shp), [own], [land], m_loc[n].reshape(shp), v_loc[n].reshape(shp),
                             _ADAM_ROWS[n], name=f"adamw_{n}")
        grads[n], delta[n], new_m[n], new_v[n] = (o.reshape(w_loc[n].shape) for o in outs)
    own_little = lax.dynamic_index_in_dim(little_slabs, me, 0, keepdims=False)
    g_little = _unpack(_sum_slabs(land_little, own=own_little, name="sum_little"),
                       [_shard_shape(s, a) for _, s, a in _LITTLE])
    grads.update({n: gl for (n, _, _), gl in zip(_LITTLE, g_little)})
    _, (everyone,) = _send_wait(send_small, delta["w_in"], name="wait_small")
    summed = _unpack(_sum_slabs(everyone, name="sum_small"),
                     [(1, 1)] + [s for _, s in replicated] + [(N_META, D_MODEL)])
    loss = summed[0].reshape(())
    grads.update({n: gr for (n, _), gr in zip(replicated, summed[1:-1])})
    grads["meta_tokens"] = lax.dynamic_slice_in_dim(summed[-1], me * out_rows, out_rows, 1)
    at_least_2d = lambda a: a.reshape((1,) * (2 - a.ndim) + a.shape)
    rest = [n for n in names if n not in _MATMUL_WEIGHTS]
    outs = _adamw_many([at_least_2d(w_loc[n]) for n in rest], [at_least_2d(grads[n]) for n in rest],
                       [at_least_2d(m_loc[n]) for n in rest], [at_least_2d(v_loc[n]) for n in rest],
                       name="adamw_rest")
    for i, n in enumerate(rest):
        delta[n], new_m[n], new_v[n] = (o.reshape(w_loc[n].shape) for o in outs[3 * i:3 * i + 3])
    return (loss, grad_x[None], *[grads[n] for n in names], *[delta[n] for n in names],
            *[new_m[n] for n in names], *[new_v[n] for n in names])
```
